```python
import jax, jax.numpy as jnp
from jax import lax
import numpy as np

D_MODEL = 1024
BATCH = 8
SEQ = 2048
DEPTH = 1

CHUNK = 64
D_POOL = D_MODEL // 2
POOL_WINDOWS = (2, 4, 8, 16)
POOL_GROUPS = len(POOL_WINDOWS)
POOL_GC = D_POOL // POOL_GROUPS
D_MLSTM = D_MODEL // 2
N_HEADS = 4
HEAD_DIM = D_MLSTM // N_HEADS
CONV_K = 4
N_GROUPS = 4
EXPERTS_PER_GROUP = 8
N_EXPERTS = N_GROUPS * EXPERTS_PER_GROUP
TOP_K = 2
D_EXPERT = D_MODEL // 2
MOE_BLOCK = 128
EPS = 1e-6
SPLIT_SIZES = (D_POOL, D_MLSTM, D_MLSTM, D_MLSTM, D_MLSTM, N_HEADS, N_HEADS, D_MODEL, D_MODEL)
N_IN = sum(SPLIT_SIZES)

kernel_name = "hybrid_pool_mlstm_hmoe_block"


def rmsnorm(x, g):
    xf = x.astype(jnp.float32)
    y = xf * lax.rsqrt(jnp.mean(xf * xf, axis=-1, keepdims=True) + EPS) * g.astype(jnp.float32)
    return y.astype(x.dtype)


def pool_mixer(u, w_pool, pool_scale):
    B, S, _ = u.shape
    uf = u.astype(jnp.float32)
    cs = jnp.pad(jnp.cumsum(uf, axis=1), ((0, 0), (1, 0), (0, 0)))
    t = jnp.arange(S)
    outs = []
    for g, w in enumerate(POOL_WINDOWS):
        sl = slice(g * POOL_GC, (g + 1) * POOL_GC)
        start = jnp.maximum(t + 1 - w, 0)
        win_sum = cs[:, 1:, sl] - cs[:, start, sl]
        cnt = (t + 1 - start).astype(jnp.float32)
        outs.append(win_sum / cnt[None, :, None] - uf[:, :, sl])
    d = jnp.stack(outs, axis=2)
    y = jnp.einsum('bsgc,gcd->bsgd', d, w_pool.astype(jnp.float32)).reshape(B, S, D_POOL)
    return (y * pool_scale.astype(jnp.float32)).astype(u.dtype)


def causal_conv(u, w):
    S = u.shape[1]
    up = jnp.pad(u, ((0, 0), (CONV_K - 1, 0), (0, 0)))
    return sum(up[:, j:j + S, :] * w[j] for j in range(CONV_K))


def mlstm_chunkwise(q, k, v, i_pre, f_pre):
    B, S, H, Dh = q.shape
    NC, L = S // CHUNK, CHUNK

    def to_chunks(a):
        a = a.astype(jnp.float32).reshape((B, NC, L, H) + a.shape[3:])
        return jnp.moveaxis(a, 3, 1)

    q = to_chunks(q)
    k = to_chunks(k) * (Dh ** -0.5)
    v = to_chunks(v)
    ig = to_chunks(i_pre)
    lf = jax.nn.log_sigmoid(to_chunks(f_pre))
    b = jnp.cumsum(lf, axis=-1)
    b_last = b[..., -1]

    a_log = b_last[..., None] - b + ig
    a_max = jnp.max(a_log, axis=-1)
    w_loc = jnp.exp(a_log - a_max[..., None])
    C_loc = jnp.einsum('bhcld,bhcle->bhcde', w_loc[..., None] * k, v)
    n_loc = jnp.einsum('bhcl,bhcld->bhcd', w_loc, k)

    def step(carry, inp):
        C, n, m = carry
        bl, am, Cl, nl = inp
        m_new = jnp.maximum(bl + m, am)
        s_prev = jnp.exp(bl + m - m_new)
        s_loc = jnp.exp(am - m_new)
        C_new = s_prev[..., None, None] * C + s_loc[..., None, None] * Cl
        n_new = s_prev[..., None] * n + s_loc[..., None] * nl
        return (C_new, n_new, m_new), (C, n, m)

    init = (jnp.zeros((B, H, Dh, Dh), jnp.float32), jnp.zeros((B, H, Dh), jnp.float32),
            jnp.zeros((B, H), jnp.float32))
    xs = (jnp.moveaxis(b_last, 2, 0), jnp.moveaxis(a_max, 2, 0),
          jnp.moveaxis(C_loc, 2, 0), jnp.moveaxis(n_loc, 2, 0))
    _, (C_prev, n_prev, m_prev) = lax.scan(step, init, xs)
    C_prev = jnp.moveaxis(C_prev, 0, 2)
    n_prev = jnp.moveaxis(n_prev, 0, 2)
    m_prev = jnp.moveaxis(m_prev, 0, 2)

    inter_log = b + m_prev[..., None]
    causal = jnp.tril(jnp.ones((L, L), dtype=bool))
    D = b[..., :, None] - b[..., None, :] + ig[..., None, :]
    D = jnp.where(causal, D, -jnp.inf)
    m_t = jnp.maximum(inter_log, jnp.max(D, axis=-1))
    w_inter = jnp.exp(inter_log - m_t)
    s = jnp.einsum('bhctd,bhcsd->bhcts', q, k) * jnp.exp(D - m_t[..., None])
    num = (w_inter[..., None] * jnp.einsum('bhctd,bhcde->bhcte', q, C_prev)
           + jnp.einsum('bhcts,bhcse->bhcte', s, v))
    nq = w_inter * jnp.einsum('bhctd,bhcd->bhct', q, n_prev) + jnp.sum(s, axis=-1)
    den = jnp.maximum(jnp.abs(nq), jnp.exp(-m_t))
    h = num / den[..., None]
    return jnp.moveaxis(h, 1, 3).reshape(B, S, H, Dh)


def head_layernorm(h, g):
    B, S, H, Dh = h.shape
    mu = jnp.mean(h, axis=-1, keepdims=True)
    var = jnp.mean(jnp.square(h - mu), axis=-1, keepdims=True)
    hn = (h - mu) * lax.rsqrt(var + EPS)
    return hn * g.astype(jnp.float32).reshape(H, Dh)


def hybrid_mixer(h, w_in, b_if, conv_q, conv_k, g_head, w_pool, pool_scale, w_br_a, w_br_b, w_out):
    B, S, _ = h.shape
    z = h @ w_in
    idx = [int(c) for c in np.cumsum(SPLIT_SIZES)[:-1]]
    u_pool, q, k, v, o, i_pre, f_pre, ga, gb = jnp.split(z, idx, axis=-1)
    y_a = pool_mixer(u_pool, w_pool, pool_scale) @ w_br_a
    q = jax.nn.silu(causal_conv(q, conv_q)).reshape(B, S, N_HEADS, HEAD_DIM)
    k = jax.nn.silu(causal_conv(k, conv_k)).reshape(B, S, N_HEADS, HEAD_DIM)
    v = v.reshape(B, S, N_HEADS, HEAD_DIM)
    i_pre = i_pre.astype(jnp.float32) + b_if[:N_HEADS].astype(jnp.float32)
    f_pre = f_pre.astype(jnp.float32) + b_if[N_HEADS:].astype(jnp.float32)
    hm = mlstm_chunkwise(q, k, v, i_pre, f_pre)
    hm = head_layernorm(hm, g_head).reshape(B, S, D_MLSTM)
    hm = (jax.nn.sigmoid(o.astype(jnp.float32)) * hm).astype(h.dtype)
    y_b = hm @ w_br_b
    merged = jax.nn.sigmoid(ga) * y_a + jax.nn.sigmoid(gb) * y_b
    return merged @ w_out


def hier_moe(h, w_rg, b_rg, w_re, b_re, w_gate, w_up, w_down):
    B, S, D = h.shape
    T = B * S
    xt = h.reshape(T, D)
    g_logits = (xt @ w_rg).astype(jnp.float32) + b_rg.astype(jnp.float32)
    g_prob = jax.nn.softmax(g_logits, axis=-1)
    g_sel = jnp.argmax(g_logits, axis=-1)
    p_g = jnp.take_along_axis(g_prob, g_sel[:, None], axis=-1)
    e_logits = ((xt @ w_re).astype(jnp.float32) + b_re.astype(jnp.float32)).reshape(
        T, N_GROUPS, EXPERTS_PER_GROUP)
    e_sel = jnp.take_along_axis(e_logits, g_sel[:, None, None], axis=1)[:, 0]
    top_p, top_i = lax.top_k(jax.nn.softmax(e_sel, axis=-1), TOP_K)
    gate = p_g * top_p / jnp.sum(top_p, axis=-1, keepdims=True)
    expert_id = g_sel[:, None] * EXPERTS_PER_GROUP + top_i

    A = T * TOP_K
    flat_e = expert_id.reshape(A)
    flat_tok = jnp.repeat(jnp.arange(T), TOP_K)
    flat_w = gate.reshape(A)
    order = jnp.argsort(flat_e)
    e_s, tok_s, w_s = flat_e[order], flat_tok[order], flat_w[order]
    counts = jnp.bincount(flat_e, length=N_EXPERTS)
    padded = (counts + MOE_BLOCK - 1) // MOE_BLOCK * MOE_BLOCK
    off = jnp.cumsum(counts) - counts
    pend = jnp.cumsum(padded)
    poff = pend - padded
    dest = poff[e_s] + jnp.arange(A) - off[e_s]
    n_rows = A + N_EXPERTS * MOE_BLOCK
    n_blocks = n_rows // MOE_BLOCK
    buf = jnp.zeros((n_rows, D), h.dtype).at[dest].set(xt[tok_s])
    block_e = jnp.clip(jnp.searchsorted(pend, jnp.arange(n_blocks) * MOE_BLOCK, side='right'),
                       0, N_EXPERTS - 1)

    def run_block(args):
        xb, e = args
        hid = jax.nn.silu(xb @ w_gate[e]) * (xb @ w_up[e])
        return hid @ w_down[e]

    yb = lax.map(run_block, (buf.reshape(n_blocks, MOE_BLOCK, D), block_e)).reshape(n_rows, D)
    y = jnp.zeros((T, D), jnp.float32).at[tok_s].add(yb[dest].astype(jnp.float32) * w_s[:, None])
    return y.reshape(B, S, D).astype(h.dtype)


def setup_inputs(seed: int = 0) -> dict:
    key = jax.random.key(seed)
    ks = jax.random.split(key, 24)
    nrm = jax.random.normal
    f32 = jnp.float32
    x = nrm(ks[0], (BATCH, SEQ, D_MODEL), f32)
    g_mix = 1.0 + 0.1 * nrm(ks[1], (DEPTH, D_MODEL), f32)
    w_in = nrm(ks[2], (DEPTH, D_MODEL, N_IN), f32) * D_MODEL ** -0.5
    b_i = 0.1 * nrm(ks[3], (DEPTH, N_HEADS), f32)
    b_f = jnp.linspace(3.0, 6.0, N_HEADS, dtype=f32)[None] + 0.1 * nrm(ks[4], (DEPTH, N_HEADS), f32)
    b_if = jnp.concatenate([b_i, b_f], axis=-1)
    conv_q = nrm(ks[5], (DEPTH, CONV_K, D_MLSTM), f32) * CONV_K ** -0.5
    conv_k = nrm(ks[6], (DEPTH, CONV_K, D_MLSTM), f32) * CONV_K ** -0.5
    g_head = 1.0 + 0.1 * nrm(ks[7], (DEPTH, D_MLSTM), f32)
    w_pool = nrm(ks[8], (DEPTH, POOL_GROUPS, POOL_GC, POOL_GC), f32) * POOL_GC ** -0.5
    pool_scale = 1.0 + 0.1 * nrm(ks[9], (DEPTH, D_POOL), f32)
    w_br_a = nrm(ks[10], (DEPTH, D_POOL, D_MODEL), f32) * D_POOL ** -0.5
    w_br_b = nrm(ks[11], (DEPTH, D_MLSTM, D_MODEL), f32) * D_MLSTM ** -0.5
    w_out = nrm(ks[12], (DEPTH, D_MODEL, D_MODEL), f32) * D_MODEL ** -0.5
    g_ffn = 1.0 + 0.1 * nrm(ks[13], (DEPTH, D_MODEL), f32)
    w_rg = nrm(ks[14], (DEPTH, D_MODEL, N_GROUPS), f32) * D_MODEL ** -0.5
    b_rg = 0.01 * nrm(ks[15], (DEPTH, N_GROUPS), f32)
    w_re = nrm(ks[16], (DEPTH, D_MODEL, N_EXPERTS), f32) * D_MODEL ** -0.5
    b_re = 0.01 * nrm(ks[17], (DEPTH, N_EXPERTS), f32)
    w_e_gate = nrm(ks[18], (DEPTH, N_EXPERTS, D_MODEL, D_EXPERT), f32) * D_MODEL ** -0.5
    w_e_up = nrm(ks[19], (DEPTH, N_EXPERTS, D_MODEL, D_EXPERT), f32) * D_MODEL ** -0.5
    w_e_down = nrm(ks[20], (DEPTH, N_EXPERTS, D_EXPERT, D_MODEL), f32) * D_EXPERT ** -0.5
    g_final = 1.0 + 0.1 * nrm(ks[21], (D_MODEL,), f32)
    return {"x": x, "g_mix": g_mix, "w_in": w_in, "b_if": b_if, "conv_q": conv_q,
            "conv_k": conv_k, "g_head": g_head, "w_pool": w_pool, "pool_scale": pool_scale,
            "w_br_a": w_br_a, "w_br_b": w_br_b, "w_out": w_out, "g_ffn": g_ffn,
            "w_rg": w_rg, "b_rg": b_rg, "w_re": w_re, "b_re": b_re, "w_e_gate": w_e_gate,
            "w_e_up": w_e_up, "w_e_down": w_e_down, "g_final": g_final}


def reference(x, g_mix, w_in, b_if, conv_q, conv_k, g_head, w_pool, pool_scale, w_br_a, w_br_b,
              w_out, g_ffn, w_rg, b_rg, w_re, b_re, w_e_gate, w_e_up, w_e_down, g_final):
    for l in range(DEPTH):
        x = x + hybrid_mixer(rmsnorm(x, g_mix[l]), w_in[l], b_if[l], conv_q[l], conv_k[l],
                             g_head[l], w_pool[l], pool_scale[l], w_br_a[l], w_br_b[l], w_out[l])
        x = x + hier_moe(rmsnorm(x, g_ffn[l]), w_rg[l], b_rg[l], w_re[l], b_re[l],
                         w_e_gate[l], w_e_up[l], w_e_down[l])
    return rmsnorm(x, g_final)
```

```python
import functools

import numpy as np
import jax
import jax.numpy as jnp
from jax import lax
from jax.experimental import pallas as pl
from jax.experimental.pallas import tpu as pltpu

F32 = jnp.float32
BF16 = jnp.bfloat16

CHUNK = 64
POOL_WINDOWS = (2, 4, 8, 16)
N_HEADS = 4
CONV_K = 4
N_GROUPS = 4
EXPERTS_PER_GROUP = 8
N_EXPERTS = N_GROUPS * EXPERTS_PER_GROUP
TOP_K = 2
EPS = 1e-6

LANES = 128
HALO = 16
ROUTER_LANE0 = N_GROUPS

INPROJ_TM = 512
MIX_TS = 256
MOE_TM = 256
DISPATCH_TT = 512
COMBINE_TT = 256
DMA_UNROLL = 8
VMEM_LIMIT = 56 * 1024 * 1024


def _cparams(n_axes):
    return pltpu.CompilerParams(dimension_semantics=("arbitrary",) * n_axes,
                                vmem_limit_bytes=VMEM_LIMIT)


def _sigmoid(v):
    return 1.0 / (1.0 + jnp.exp(-v))


def _silu(v):
    return v * _sigmoid(v)


def _log_sigmoid(v):
    return jnp.minimum(v, 0.0) - jnp.log1p(jnp.exp(-jnp.abs(v)))


def _split3(v):
    hi = v.astype(BF16)
    r1 = v - hi.astype(F32)
    mid = r1.astype(BF16)
    lo = (r1 - mid.astype(F32)).astype(BF16)
    return hi, mid, lo


def _dot(a, b):
    return jnp.dot(a, b, preferred_element_type=F32)


def _dot_nt(a, b):
    return lax.dot_general(a, b, (((1,), (1,)), ((), ())), preferred_element_type=F32)


def _dot_tn(a, b):
    return lax.dot_general(a, b, (((0,), (0,)), ((), ())), preferred_element_type=F32)


def _inproj_kernel(x_ref, g_ref, wm_ref, wg_ref, wif_ref, wift_ref,
                   zm_ref, zg_ref, zif_ref, zift_ref):
    x = x_ref[...]
    ms = jnp.mean(x * x, axis=-1, keepdims=True)
    xn = (x * lax.rsqrt(ms + EPS) * g_ref[...]).astype(BF16)
    zm_ref[...] = _dot(xn, wm_ref[...]).astype(BF16)
    zg_ref[...] = _dot(xn, wg_ref[...]).astype(BF16)
    zif_ref[...] = _dot(xn, wif_ref[...])
    zift_ref[...] = _dot_nt(wift_ref[...], xn)


def _inproj(x2d, g_mix, w_main, w_gate, w_if, w_if_t):
    T, D = x2d.shape
    tm = min(INPROJ_TM, T)
    nm, ng = w_main.shape[1], w_gate.shape[1]
    const = lambda i: (0, 0)
    return pl.pallas_call(
        _inproj_kernel,
        grid=(T // tm,),
        in_specs=[pl.BlockSpec((tm, D), lambda i: (i, 0)),
                  pl.BlockSpec((1, D), const),
                  pl.BlockSpec((D, nm), const),
                  pl.BlockSpec((D, ng), const),
                  pl.BlockSpec((D, LANES), const),
                  pl.BlockSpec((16, D), const)],
        out_specs=[pl.BlockSpec((tm, nm), lambda i: (i, 0)),
                   pl.BlockSpec((tm, ng), lambda i: (i, 0)),
                   pl.BlockSpec((tm, LANES), lambda i: (i, 0)),
                   pl.BlockSpec((16, tm), lambda i: (0, i))],
        out_shape=[jax.ShapeDtypeStruct((T, nm), BF16),
                   jax.ShapeDtypeStruct((T, ng), BF16),
                   jax.ShapeDtypeStruct((T, LANES), F32),
                   jax.ShapeDtypeStruct((16, T), F32)],
        compiler_params=_cparams(1),
        name="inproj",
    )(x2d, g_mix, w_main, w_gate, w_if, w_if_t)


def _mixer_kernel(x_ref, zm_ref, zg_ref, zif_ref, zift_ref,
                  bif_ref, bift_ref, convq_ref, convk_ref, ghead_ref, wpool_ref, pscale_ref,
                  wa_ref, wb_ref, wo_ref, gffn_ref, wrh_ref, wrl_ref, br_ref,
                  tric_ref, trir_ref, stri_ref,
                  x2_ref, xn2_ref, rti_ref, rtf_ref, cnt_ref,
                  halo_ref, q_ref, k_ref, h_ref, pool_ref, cst_ref, mst_ref, run_ref):
    ts = x_ref.shape[0]
    d_pool = wa_ref.shape[0]
    d_ml = wb_ref.shape[0]
    dh = d_ml // N_HEADS
    n_chunks = ts // CHUNK
    j = pl.program_id(1)

    @pl.when(j == 0)
    def _():
        halo_ref[...] = jnp.zeros_like(halo_ref)
        cst_ref[...] = jnp.zeros_like(cst_ref)
        mst_ref[...] = jnp.zeros_like(mst_ref)

    @pl.when((j == 0) & (pl.program_id(0) == 0))
    def _():
        run_ref[...] = jnp.zeros_like(run_ref)

    row = lax.broadcasted_iota(jnp.int32, (ts, LANES), 0)
    pos1 = (row + j * ts + 1).astype(F32)

    def extended(cg):
        cols = slice(cg * LANES, (cg + 1) * LANES)
        cur = zm_ref[:, cols].astype(F32)
        ext = jnp.concatenate([halo_ref[:, cols], cur], axis=0)
        halo_ref[:, cols] = cur[ts - HALO:, :]
        return ext

    n_pool_groups = d_pool // LANES
    for g in range(n_pool_groups):
        w = POOL_WINDOWS[g]
        ext = extended(g)
        s, span = ext, 1
        while span < w:
            s = s + pltpu.roll(s, span, axis=0)
            span *= 2
        win = s[HALO:, :]
        cnt = jnp.minimum(pos1, float(w))
        d = win / cnt - ext[HALO:, :]
        y = _dot(d.astype(BF16), wpool_ref[g]) * pscale_ref[:, g * LANES:(g + 1) * LANES]
        pool_ref[:, g * LANES:(g + 1) * LANES] = y.astype(BF16)

    n_ml_groups = d_ml // LANES
    for which, (cw_ref, dst_ref, scale) in enumerate(((convq_ref, q_ref, 1.0), (convk_ref, k_ref, dh ** -0.5))):
        for g in range(n_ml_groups):
            cols = slice(g * LANES, (g + 1) * LANES)
            ext = extended(n_pool_groups + which * n_ml_groups + g)
            acc = ext * cw_ref[CONV_K - 1:CONV_K, cols]
            for sft in range(1, CONV_K):
                acc = acc + pltpu.roll(ext, sft, axis=0) * cw_ref[CONV_K - 1 - sft:CONV_K - sft, cols]
            dst_ref[:, cols] = (_silu(acc[HALO:, :]) * scale).astype(BF16)

    zc = zif_ref[...] + bif_ref[...]
    lf_c = _log_sigmoid(zc)
    bc = sum(_dot(tric_ref[...], p) for p in _split3(lf_c))
    zr = zift_ref[...] + bift_ref[...]
    lf_r = _log_sigmoid(zr)
    br = sum(_dot(p, trir_ref[...]) for p in _split3(lf_r))

    ti = lax.broadcasted_iota(jnp.int32, (CHUNK, CHUNK), 0)
    si = lax.broadcasted_iota(jnp.int32, (CHUNK, CHUNK), 1)
    causal = si <= ti
    lane_c = lax.broadcasted_iota(jnp.int32, (CHUNK, LANES), 1)
    ones_blk = jnp.where(lane_c == 0, 1.0, 0.0).astype(BF16)
    v0 = d_pool + 2 * d_ml

    for c in range(n_chunks):
        rs = slice(c * CHUNK, (c + 1) * CHUNK)
        for h in range(N_HEADS):
            hs = slice(h * dh, (h + 1) * dh)
            q = q_ref[rs, hs]
            k = k_ref[rs, hs]
            v_aug = jnp.concatenate([zm_ref[rs, v0 + h * dh:v0 + (h + 1) * dh], ones_blk], axis=-1)
            bt = bc[rs, N_HEADS + h:N_HEADS + h + 1]
            igc = zc[rs, h:h + 1]
            r_row = zr[h:h + 1, rs] - br[N_HEADS + h:N_HEADS + h + 1, rs]
            m_prev = mst_ref[0:1, h:h + 1]
            c_prev = cst_ref[h]

            dmat = jnp.where(causal, bt + r_row, -jnp.inf)
            m_intra = jnp.max(dmat, axis=-1, keepdims=True)
            inter = bt + m_prev
            m_t = jnp.maximum(inter, m_intra)
            w_inter = jnp.exp(inter - m_t)
            smat = _dot_nt(q, k) * jnp.exp(dmat - m_t)
            num_aug = w_inter * _dot(q, c_prev.astype(BF16)) + _dot(smat.astype(BF16), v_aug)
            nq = num_aug[:, dh:dh + 1]
            den = jnp.maximum(jnp.abs(nq), jnp.exp(-m_t))
            h_ref[rs, hs] = num_aug[:, :dh] / den

            b_last = bt[CHUNK - 1:CHUNK, :]
            a_log = b_last - bt + igc
            a_max = jnp.max(a_log, axis=0, keepdims=True)
            m_new = jnp.maximum(b_last + m_prev, a_max)
            kw = (k.astype(F32) * jnp.exp(a_log - m_new)).astype(BF16)
            cst_ref[h] = jnp.exp(b_last + m_prev - m_new) * c_prev + _dot_tn(kw, v_aug)
            mst_ref[0:1, h:h + 1] = m_new

    o0 = v0 + d_ml
    for h in range(N_HEADS):
        hs = slice(h * dh, (h + 1) * dh)
        hv = h_ref[:, hs]
        mu = jnp.mean(hv, axis=-1, keepdims=True)
        hc = hv - mu
        var = jnp.mean(hc * hc, axis=-1, keepdims=True)
        hn = hc * lax.rsqrt(var + EPS) * ghead_ref[:, hs]
        og = _sigmoid(zm_ref[:, o0 + h * dh:o0 + (h + 1) * dh].astype(F32))
        q_ref[:, hs] = (og * hn).astype(BF16)
    y_a = _dot(pool_ref[...], wa_ref[...])
    y_b = _dot(q_ref[...], wb_ref[...])
    d_model = x_ref.shape[1]
    ga = _sigmoid(zg_ref[:, :d_model].astype(F32))
    gb = _sigmoid(zg_ref[:, d_model:].astype(F32))
    merged = (ga * y_a + gb * y_b).astype(BF16)
    x2 = x_ref[...] + _dot(merged, wo_ref[...])
    x2_ref[...] = x2

    ms = jnp.mean(x2 * x2, axis=-1, keepdims=True)
    xn2 = x2 * lax.rsqrt(ms + EPS) * gffn_ref[...]
    xn2_ref[...] = xn2
    xh = xn2.astype(BF16)
    xl = (xn2 - xh.astype(F32)).astype(BF16)
    lg = _dot(xh, wrh_ref[...]) + _dot(xl, wrh_ref[...]) + _dot(xh, wrl_ref[...]) + br_ref[...]

    lane = lax.broadcasted_iota(jnp.int32, (ts, LANES), 1)
    lanef = lane.astype(F32)
    big = float(4 * LANES)
    gl = jnp.where(lane < N_GROUPS, lg, -jnp.inf)
    gmax = jnp.max(gl, axis=-1, keepdims=True)
    g_sel = jnp.min(jnp.where(gl == gmax, lanef, big), axis=-1, keepdims=True)
    p_g = 1.0 / jnp.sum(jnp.exp(gl - gmax), axis=-1, keepdims=True)
    lo = ROUTER_LANE0 + EXPERTS_PER_GROUP * g_sel
    el = jnp.where((lanef >= lo) & (lanef < lo + EXPERTS_PER_GROUP), lg, -jnp.inf)
    m1 = jnp.max(el, axis=-1, keepdims=True)
    i1 = jnp.min(jnp.where(el == m1, lanef, big), axis=-1, keepdims=True)
    el2 = jnp.where(lanef == i1, -jnp.inf, el)
    m2 = jnp.max(el2, axis=-1, keepdims=True)
    i2 = jnp.min(jnp.where(el2 == m2, lanef, big), axis=-1, keepdims=True)
    e2x = jnp.exp(m2 - m1)
    gate1 = p_g / (1.0 + e2x)
    gate2 = p_g * e2x / (1.0 + e2x)

    oh1 = lanef == i1
    oh2 = lanef == i2
    ohs = jnp.where(oh1 | oh2, 1.0, 0.0)
    base = _dot(stri_ref[...], ohs.astype(BF16)) + run_ref[0:1, :]
    rank1 = jnp.sum(jnp.where(oh1, base, 0.0), axis=-1, keepdims=True)
    rank2 = jnp.sum(jnp.where(oh2, base, 0.0), axis=-1, keepdims=True)
    run_new = run_ref[0:1, :] + jnp.sum(ohs, axis=0, keepdims=True)
    run_ref[0:1, :] = run_new
    cnt_ref[...] = jnp.broadcast_to(run_new, cnt_ref.shape).astype(jnp.int32)

    rti = jnp.where(lane == 0, i1 - ROUTER_LANE0,
                    jnp.where(lane == 1, i2 - ROUTER_LANE0,
                              jnp.where(lane == 2, rank1, jnp.where(lane == 3, rank2, 0.0))))
    rti_ref[...] = rti.astype(jnp.int32)
    rtf_ref[...] = jnp.where(lane == 0, gate1, jnp.where(lane == 1, gate2, 0.0))


def _mixer(x2d, zm, zg, zif, zift, params, batch, seq):
    T, D = x2d.shape
    ts = min(MIX_TS, seq)
    nts = seq // ts
    d_pool = params["w_br_a"].shape[0]
    d_ml = params["w_br_b"].shape[0]
    dh = d_ml // N_HEADS

    idx = np.arange(ts)
    same_chunk = (idx[:, None] // CHUNK) == (idx[None, :] // CHUNK)
    tri_c = jnp.asarray((idx[None, :] <= idx[:, None]) & same_chunk, BF16)
    tri_r = jnp.asarray((idx[:, None] <= idx[None, :]) & same_chunk, BF16)
    stri = jnp.asarray(idx[None, :] < idx[:, None], BF16)

    tok = lambda b, j: (b * nts + j, 0)
    tok_t = lambda b, j: (0, b * nts + j)
    c2 = lambda b, j: (0, 0)
    c3 = lambda b, j: (0, 0, 0)
    full = lambda a: pl.BlockSpec(a.shape, c2 if a.ndim == 2 else c3)
    consts = [params[n] for n in ("b_if", "b_if_t", "conv_q", "conv_k", "g_head", "w_pool", "pool_scale",
                                  "w_br_a", "w_br_b", "w_out", "g_ffn", "w_r_hi", "w_r_lo", "b_r")]
    consts += [tri_c, tri_r, stri]
    return pl.pallas_call(
        _mixer_kernel,
        grid=(batch, nts),
        in_specs=[pl.BlockSpec((ts, D), tok),
                  pl.BlockSpec((ts, zm.shape[1]), tok),
                  pl.BlockSpec((ts, zg.shape[1]), tok),
                  pl.BlockSpec((ts, LANES), tok),
                  pl.BlockSpec((16, ts), tok_t)] + [full(a) for a in consts],
        out_specs=[pl.BlockSpec((ts, D), tok),
                   pl.BlockSpec((ts, D), tok),
                   pl.BlockSpec((ts, LANES), tok),
                   pl.BlockSpec((ts, LANES), tok),
                   pl.BlockSpec((8, LANES), c2)],
        out_shape=[jax.ShapeDtypeStruct((T, D), F32),
                   jax.ShapeDtypeStruct((T, D), F32),
                   jax.ShapeDtypeStruct((T, LANES), jnp.int32),
                   jax.ShapeDtypeStruct((T, LANES), F32),
                   jax.ShapeDtypeStruct((8, LANES), jnp.int32)],
        scratch_shapes=[pltpu.VMEM((HALO, d_pool + 2 * d_ml), F32),
                        pltpu.VMEM((ts, d_ml), BF16),
                        pltpu.VMEM((ts, d_ml), BF16),
                        pltpu.VMEM((ts, d_ml), F32),
                        pltpu.VMEM((ts, d_pool), BF16),
                        pltpu.VMEM((N_HEADS, dh, 2 * dh), F32),
                        pltpu.VMEM((8, LANES), F32),
                        pltpu.VMEM((8, LANES), F32)],
        compiler_params=_cparams(2),
        name="mixer",
    )(x2d, zm, zg, zif, zift, *consts)


def _dest_row(e_ref, rank_ref, poff_ref, a):
    return poff_ref[e_ref[a]] + rank_ref[a]


def _dispatch_kernel(e_ref, rank_ref, poff_ref, zflag_ref,
                     xn_ref, buf_ref, zeros_ref, sem, zsem):
    tt = xn_ref.shape[0]
    n_blocks = buf_ref.shape[0] // MOE_TM
    i = pl.program_id(0)

    @pl.when(i == 0)
    def _():
        zeros_ref[...] = jnp.zeros_like(zeros_ref)

        def zero_copy(b):
            return pltpu.make_async_copy(zeros_ref, buf_ref.at[pl.ds(b * MOE_TM, MOE_TM)], zsem)

        def zero_start(b, carry):
            @pl.when(zflag_ref[b] > 0)
            def _():
                zero_copy(b).start()
            return carry

        def zero_wait(b, carry):
            @pl.when(zflag_ref[b] > 0)
            def _():
                zero_copy(b).wait()
            return carry

        lax.fori_loop(0, n_blocks, zero_start, 0)
        lax.fori_loop(0, n_blocks, zero_wait, 0)

    def row_copy(t, k):
        a = (i * tt + t) * TOP_K + k
        d = _dest_row(e_ref, rank_ref, poff_ref, a)
        return pltpu.make_async_copy(xn_ref.at[pl.ds(t, 1)], buf_ref.at[pl.ds(d, 1)], sem)

    def issue(g, carry):
        for u in range(DMA_UNROLL):
            for k in range(TOP_K):
                row_copy(g * DMA_UNROLL + u, k).start()
        return carry

    def drain(g, carry):
        for u in range(DMA_UNROLL):
            for k in range(TOP_K):
                row_copy(g * DMA_UNROLL + u, k).wait()
        return carry

    lax.fori_loop(0, tt // DMA_UNROLL, issue, 0)
    lax.fori_loop(0, tt // DMA_UNROLL, drain, 0)


def _dispatch(xn3, e_flat, rank_flat, poff, zflag, n_rows):
    T, _, D = xn3.shape
    tt = min(DISPATCH_TT, T)
    return pl.pallas_call(
        _dispatch_kernel,
        grid_spec=pltpu.PrefetchScalarGridSpec(
            num_scalar_prefetch=4,
            grid=(T // tt,),
            in_specs=[pl.BlockSpec((tt, 1, D), lambda i, *_: (i, 0, 0))],
            out_specs=pl.BlockSpec(memory_space=pl.ANY),
            scratch_shapes=[pltpu.VMEM((MOE_TM, 1, D), F32),
                            pltpu.SemaphoreType.DMA(()),
                            pltpu.SemaphoreType.DMA(())]),
        out_shape=jax.ShapeDtypeStruct((n_rows, 1, D), F32),
        compiler_params=_cparams(1),
        name="dispatch",
    )(e_flat, rank_flat, poff, zflag, xn3)


def _experts_kernel(blk_e_ref, nused_ref, x_ref, wg_ref, wu_ref, wd_ref, y_ref):
    used = pl.program_id(0) < nused_ref[0]

    @pl.when(used)
    def _():
        x = x_ref[...].astype(BF16)
        hg = _dot(x, wg_ref[0])
        hu = _dot(x, wu_ref[0])
        hid = (_silu(hg) * hu).astype(BF16)
        y_ref[...] = _dot(hid, wd_ref[0])

    @pl.when(jnp.logical_not(used))
    def _():
        y_ref[...] = jnp.zeros_like(y_ref)


def _experts(buf, blk_e, nused, w_gate, w_up, w_down):
    R, D = buf.shape
    de = w_gate.shape[2]
    n_blocks = R // MOE_TM

    def blk(i, blk_e_ref, nused_ref):
        return jnp.minimum(i, nused_ref[0] - 1)

    row_map = lambda i, be, nu: (blk(i, be, nu), 0)
    w_map = lambda i, be, nu: (be[blk(i, be, nu)], 0, 0)
    return pl.pallas_call(
        _experts_kernel,
        grid_spec=pltpu.PrefetchScalarGridSpec(
            num_scalar_prefetch=2,
            grid=(n_blocks,),
            in_specs=[pl.BlockSpec((MOE_TM, D), row_map),
                      pl.BlockSpec((1, D, de), w_map),
                      pl.BlockSpec((1, D, de), w_map),
                      pl.BlockSpec((1, de, D), w_map)],
            out_specs=pl.BlockSpec((MOE_TM, D), lambda i, be, nu: (i, 0))),
        out_shape=jax.ShapeDtypeStruct((R, D), F32),
        compiler_params=_cparams(1),
        name="experts",
    )(blk_e, nused, buf, w_gate, w_up, w_down)


def _combine_kernel(e_ref, rank_ref, poff_ref, x2_ref, rtf_ref, gfin_ref, yb_ref, out_ref,
                    gbuf0_ref, gbuf1_ref, sem):
    tt = x2_ref.shape[0]
    i = pl.program_id(0)
    gbufs = (gbuf0_ref, gbuf1_ref)

    def row_copy(t, k):
        a = (i * tt + t) * TOP_K + k
        d = _dest_row(e_ref, rank_ref, poff_ref, a)
        return pltpu.make_async_copy(yb_ref.at[pl.ds(d, 1)], gbufs[k].at[pl.ds(t, 1)], sem)

    def issue(g, carry):
        for u in range(DMA_UNROLL):
            for k in range(TOP_K):
                row_copy(g * DMA_UNROLL + u, k).start()
        return carry

    def drain(g, carry):
        for u in range(DMA_UNROLL):
            for k in range(TOP_K):
                row_copy(g * DMA_UNROLL + u, k).wait()
        return carry

    lax.fori_loop(0, tt // DMA_UNROLL, issue, 0)
    lax.fori_loop(0, tt // DMA_UNROLL, drain, 0)

    rtf = rtf_ref[...]
    y = x2_ref[...] + rtf[:, :, 0:1] * gbuf0_ref[...] + rtf[:, :, 1:2] * gbuf1_ref[...]
    ms = jnp.mean(y * y, axis=-1, keepdims=True)
    out_ref[...] = y * lax.rsqrt(ms + EPS) * gfin_ref[...]


def _combine(x23, rtf3, g_final3, yb3, e_flat, rank_flat, poff):
    T, _, D = x23.shape
    tt = min(COMBINE_TT, T)
    tok = lambda i, *_: (i, 0, 0)
    return pl.pallas_call(
        _combine_kernel,
        grid_spec=pltpu.PrefetchScalarGridSpec(
            num_scalar_prefetch=3,
            grid=(T // tt,),
            in_specs=[pl.BlockSpec((tt, 1, D), tok),
                      pl.BlockSpec((tt, 1, LANES), tok),
                      pl.BlockSpec((1, 1, D), lambda i, *_: (0, 0, 0)),
                      pl.BlockSpec(memory_space=pl.ANY)],
            out_specs=pl.BlockSpec((tt, 1, D), tok),
            scratch_shapes=[pltpu.VMEM((tt, 1, D), F32),
                            pltpu.VMEM((tt, 1, D), F32),
                            pltpu.SemaphoreType.DMA(())]),
        out_shape=jax.ShapeDtypeStruct((T, 1, D), F32),
        compiler_params=_cparams(1),
        name="combine",
    )(e_flat, rank_flat, poff, x23, rtf3, g_final3, yb3)


def _pad_lanes(a, width=LANES):
    return jnp.pad(a, ((0, 0), (0, width - a.shape[1])))


def kernel(x, g_mix, w_in, b_if, conv_q, conv_k, g_head, w_pool, pool_scale, w_br_a, w_br_b, w_out,
           g_ffn, w_rg, b_rg, w_re, b_re, w_e_gate, w_e_up, w_e_down, g_final):
    B, S, D = x.shape
    T = B * S
    assert g_mix.shape[0] == 1, "single-layer block"
    assert S % MIX_TS == 0 and T % DISPATCH_TT == 0 and T % INPROJ_TM == 0
    d_pool = w_br_a.shape[1]
    d_ml = w_br_b.shape[1]
    x2d = x.reshape(T, D)

    n_main = d_pool + 4 * d_ml
    w_l = w_in[0]
    w_main = w_l[:, :n_main].astype(BF16)
    w_if = w_l[:, n_main:n_main + 2 * N_HEADS]
    w_gates = w_l[:, n_main + 2 * N_HEADS:].astype(BF16)
    w_if_c = _pad_lanes(w_if).astype(BF16)
    w_if_t = jnp.pad(w_if.T, ((0, 16 - 2 * N_HEADS), (0, 0))).astype(BF16)
    w_r = _pad_lanes(jnp.concatenate([w_rg[0], w_re[0]], axis=1))
    w_r_hi = w_r.astype(BF16)
    w_r_lo = (w_r - w_r_hi.astype(F32)).astype(BF16)
    params = {
        "b_if": _pad_lanes(b_if[0][None, :]),
        "b_if_t": jnp.pad(b_if[0][:, None], ((0, 16 - 2 * N_HEADS), (0, 0))),
        "conv_q": conv_q[0], "conv_k": conv_k[0],
        "g_head": g_head[0][None, :],
        "w_pool": w_pool[0].astype(BF16),
        "pool_scale": pool_scale[0][None, :],
        "w_br_a": w_br_a[0].astype(BF16), "w_br_b": w_br_b[0].astype(BF16),
        "w_out": w_out[0].astype(BF16),
        "g_ffn": g_ffn[0][None, :],
        "w_r_hi": w_r_hi, "w_r_lo": w_r_lo,
        "b_r": _pad_lanes(jnp.concatenate([b_rg[0], b_re[0]])[None, :]),
    }

    zm, zg, zif, zift = _inproj(x2d, g_mix[0][None, :], w_main, w_gates, w_if_c, w_if_t)
    x2, xn2, rti, rtf, cnt = _mixer(x2d, zm, zg, zif, zift, params, B, S)

    counts = cnt[0, ROUTER_LANE0:ROUTER_LANE0 + N_EXPERTS]
    padded = (counts + MOE_TM - 1) // MOE_TM * MOE_TM
    pend = jnp.cumsum(padded)
    poff = (pend - padded).astype(jnp.int32)
    n_rows = T * TOP_K + N_EXPERTS * MOE_TM
    n_blocks = n_rows // MOE_TM
    nused = (pend[-1:] // MOE_TM).astype(jnp.int32)
    blk_ids = jnp.arange(n_blocks, dtype=jnp.int32)
    blk_e = jnp.sum((pend[None, :] <= blk_ids[:, None] * MOE_TM).astype(jnp.int32), axis=1)
    blk_e = jnp.minimum(blk_e, N_EXPERTS - 1)
    zflag = (((blk_ids + 1) * MOE_TM == pend[blk_e]) | (blk_ids >= nused[0])).astype(jnp.int32)
    e_flat = rti[:, 0:TOP_K].reshape(T * TOP_K)
    rank_flat = rti[:, TOP_K:2 * TOP_K].reshape(T * TOP_K)

    buf = _dispatch(xn2.reshape(T, 1, D), e_flat, rank_flat, poff, zflag, n_rows)
    yb = _experts(buf.reshape(n_rows, D), blk_e, nused, w_e_gate[0].astype(BF16), w_e_up[0].astype(BF16),
                  w_e_down[0].astype(BF16))
    out = _combine(x2.reshape(T, 1, D), rtf.reshape(T, 1, LANES), g_final.reshape(1, 1, D),
                   yb.reshape(n_rows, 1, D), e_flat, rank_flat, poff)
    return out.reshape(B, S, D)
```

```python
import functools

import numpy as np
import jax
import jax.numpy as jnp
from jax import lax
from jax.experimental import pallas as pl
from jax.experimental.pallas import tpu as pltpu

F32 = jnp.float32
BF16 = jnp.bfloat16

CHUNK = 64
POOL_WINDOWS = (2, 4, 8, 16)
N_HEADS = 4
CONV_K = 4
N_GROUPS = 4
EXPERTS_PER_GROUP = 8
N_EXPERTS = N_GROUPS * EXPERTS_PER_GROUP
TOP_K = 2
EPS = 1e-6

LANES = 128
HALO = 16
ROUTER_LANE0 = N_GROUPS

INPROJ_TM = 512
MIX_TS = 256
MOE_TM = 256
ROW_PIECE = 8
MOE_SL = TOP_K * MIX_TS + N_EXPERTS * ROW_PIECE
SLOT_RADIX = 16
VMEM_LIMIT = 56 * 1024 * 1024


def _cparams(n_axes):
    return pltpu.CompilerParams(dimension_semantics=("arbitrary",) * n_axes,
                                vmem_limit_bytes=VMEM_LIMIT)


def _sigmoid(v):
    return 1.0 / (1.0 + jnp.exp(-v))


def _silu(v):
    return v * _sigmoid(v)


def _log_sigmoid(v):
    return jnp.minimum(v, 0.0) - jnp.log1p(jnp.exp(-jnp.abs(v)))


def _split3(v):
    hi = v.astype(BF16)
    r1 = v - hi.astype(F32)
    mid = r1.astype(BF16)
    lo = (r1 - mid.astype(F32)).astype(BF16)
    return hi, mid, lo


def _dot(a, b):
    return jnp.dot(a, b, preferred_element_type=F32)


def _dot_nt(a, b):
    return lax.dot_general(a, b, (((1,), (1,)), ((), ())), preferred_element_type=F32)


def _dot_tn(a, b):
    return lax.dot_general(a, b, (((0,), (0,)), ((), ())), preferred_element_type=F32)


def _inproj_kernel(x_ref, g_ref, wm_ref, wg_ref, wif_ref, wift_ref,
                   zm_ref, zg_ref, zif_ref, zift_ref):
    x = x_ref[...]
    ms = jnp.mean(x * x, axis=-1, keepdims=True)
    xn = (x * lax.rsqrt(ms + EPS) * g_ref[...]).astype(BF16)
    zm_ref[...] = _dot(xn, wm_ref[...]).astype(BF16)
    zg_ref[...] = _dot(xn, wg_ref[...]).astype(BF16)
    zif_ref[...] = _dot(xn, wif_ref[...])
    zift_ref[...] = _dot_nt(wift_ref[...], xn)


def _inproj(x2d, g_mix, w_main, w_gate, w_if, w_if_t):
    T, D = x2d.shape
    tm = min(INPROJ_TM, T)
    nm, ng = w_main.shape[1], w_gate.shape[1]
    const = lambda i: (0, 0)
    return pl.pallas_call(
        _inproj_kernel,
        grid=(T // tm,),
        in_specs=[pl.BlockSpec((tm, D), lambda i: (i, 0)),
                  pl.BlockSpec((1, D), const),
                  pl.BlockSpec((D, nm), const),
                  pl.BlockSpec((D, ng), const),
                  pl.BlockSpec((D, LANES), const),
                  pl.BlockSpec((16, D), const)],
        out_specs=[pl.BlockSpec((tm, nm), lambda i: (i, 0)),
                   pl.BlockSpec((tm, ng), lambda i: (i, 0)),
                   pl.BlockSpec((tm, LANES), lambda i: (i, 0)),
                   pl.BlockSpec((16, tm), lambda i: (0, i))],
        out_shape=[jax.ShapeDtypeStruct((T, nm), BF16),
                   jax.ShapeDtypeStruct((T, ng), BF16),
                   jax.ShapeDtypeStruct((T, LANES), F32),
                   jax.ShapeDtypeStruct((16, T), F32)],
        compiler_params=_cparams(1),
        name="inproj",
    )(x2d, g_mix, w_main, w_gate, w_if, w_if_t)


def _mixer_kernel(x_ref, zm_ref, zg_ref, zif_ref, zift_ref,
                  bif_ref, bift_ref, convq_ref, convk_ref, ghead_ref, wpool_ref, pscale_ref,
                  wa_ref, wb_ref, wo_ref, gffn_ref, wrh_ref, wrl_ref, br_ref,
                  tric_ref, trir_ref, stri_ref, ut_ref, sel_ref,
                  x2_ref, xn2_ref, rti_ref, rtf_ref, tstat_ref, srow_ref,
                  halo_ref, q_ref, k_ref, h_ref, pool_ref, cst_ref, mst_ref):
    ts = x_ref.shape[0]
    d_pool = wa_ref.shape[0]
    d_ml = wb_ref.shape[0]
    dh = d_ml // N_HEADS
    n_chunks = ts // CHUNK
    j = pl.program_id(1)

    @pl.when(j == 0)
    def _():
        halo_ref[...] = jnp.zeros_like(halo_ref)
        cst_ref[...] = jnp.zeros_like(cst_ref)
        mst_ref[...] = jnp.zeros_like(mst_ref)

    row = lax.broadcasted_iota(jnp.int32, (ts, LANES), 0)
    pos1 = (row + j * ts + 1).astype(F32)

    def extended(cg):
        cols = slice(cg * LANES, (cg + 1) * LANES)
        cur = zm_ref[:, cols].astype(F32)
        ext = jnp.concatenate([halo_ref[:, cols], cur], axis=0)
        halo_ref[:, cols] = cur[ts - HALO:, :]
        return ext

    n_pool_groups = d_pool // LANES
    for g in range(n_pool_groups):
        w = POOL_WINDOWS[g]
        ext = extended(g)
        s, span = ext, 1
        while span < w:
            s = s + pltpu.roll(s, span, axis=0)
            span *= 2
        win = s[HALO:, :]
        cnt = jnp.minimum(pos1, float(w))
        d = win / cnt - ext[HALO:, :]
        y = _dot(d.astype(BF16), wpool_ref[g]) * pscale_ref[:, g * LANES:(g + 1) * LANES]
        pool_ref[:, g * LANES:(g + 1) * LANES] = y.astype(BF16)

    n_ml_groups = d_ml // LANES
    for which, (cw_ref, dst_ref, scale) in enumerate(((convq_ref, q_ref, 1.0), (convk_ref, k_ref, dh ** -0.5))):
        for g in range(n_ml_groups):
            cols = slice(g * LANES, (g + 1) * LANES)
            ext = extended(n_pool_groups + which * n_ml_groups + g)
            acc = ext * cw_ref[CONV_K - 1:CONV_K, cols]
            for sft in range(1, CONV_K):
                acc = acc + pltpu.roll(ext, sft, axis=0) * cw_ref[CONV_K - 1 - sft:CONV_K - sft, cols]
            dst_ref[:, cols] = (_silu(acc[HALO:, :]) * scale).astype(BF16)

    zc = zif_ref[...] + bif_ref[...]
    lf_c = _log_sigmoid(zc)
    bc = sum(_dot(tric_ref[...], p) for p in _split3(lf_c))
    zr = zift_ref[...] + bift_ref[...]
    lf_r = _log_sigmoid(zr)
    br = sum(_dot(p, trir_ref[...]) for p in _split3(lf_r))

    ti = lax.broadcasted_iota(jnp.int32, (CHUNK, CHUNK), 0)
    si = lax.broadcasted_iota(jnp.int32, (CHUNK, CHUNK), 1)
    causal = si <= ti
    lane_c = lax.broadcasted_iota(jnp.int32, (CHUNK, LANES), 1)
    ones_blk = jnp.where(lane_c == 0, 1.0, 0.0).astype(BF16)
    v0 = d_pool + 2 * d_ml

    for c in range(n_chunks):
        rs = slice(c * CHUNK, (c + 1) * CHUNK)
        for h in range(N_HEADS):
            hs = slice(h * dh, (h + 1) * dh)
            q = q_ref[rs, hs]
            k = k_ref[rs, hs]
            v_aug = jnp.concatenate([zm_ref[rs, v0 + h * dh:v0 + (h + 1) * dh], ones_blk], axis=-1)
            bt = bc[rs, N_HEADS + h:N_HEADS + h + 1]
            igc = zc[rs, h:h + 1]
            r_row = zr[h:h + 1, rs] - br[N_HEADS + h:N_HEADS + h + 1, rs]
            m_prev = mst_ref[0:1, h:h + 1]
            c_prev = cst_ref[h]

            dmat = jnp.where(causal, bt + r_row, -jnp.inf)
            m_intra = jnp.max(dmat, axis=-1, keepdims=True)
            inter = bt + m_prev
            m_t = jnp.maximum(inter, m_intra)
            w_inter = jnp.exp(inter - m_t)
            smat = _dot_nt(q, k) * jnp.exp(dmat - m_t)
            num_aug = w_inter * _dot(q, c_prev.astype(BF16)) + _dot(smat.astype(BF16), v_aug)
            nq = num_aug[:, dh:dh + 1]
            den = jnp.maximum(jnp.abs(nq), jnp.exp(-m_t))
            h_ref[rs, hs] = num_aug[:, :dh] / den

            b_last = bt[CHUNK - 1:CHUNK, :]
            a_log = b_last - bt + igc
            a_max = jnp.max(a_log, axis=0, keepdims=True)
            m_new = jnp.maximum(b_last + m_prev, a_max)
            kw = (k.astype(F32) * jnp.exp(a_log - m_new)).astype(BF16)
            cst_ref[h] = jnp.exp(b_last + m_prev - m_new) * c_prev + _dot_tn(kw, v_aug)
            mst_ref[0:1, h:h + 1] = m_new

    o0 = v0 + d_ml
    for h in range(N_HEADS):
        hs = slice(h * dh, (h + 1) * dh)
        hv = h_ref[:, hs]
        mu = jnp.mean(hv, axis=-1, keepdims=True)
        hc = hv - mu
        var = jnp.mean(hc * hc, axis=-1, keepdims=True)
        hn = hc * lax.rsqrt(var + EPS) * ghead_ref[:, hs]
        og = _sigmoid(zm_ref[:, o0 + h * dh:o0 + (h + 1) * dh].astype(F32))
        q_ref[:, hs] = (og * hn).astype(BF16)
    y_a = _dot(pool_ref[...], wa_ref[...])
    y_b = _dot(q_ref[...], wb_ref[...])
    d_model = x_ref.shape[1]
    ga = _sigmoid(zg_ref[:, :d_model].astype(F32))
    gb = _sigmoid(zg_ref[:, d_model:].astype(F32))
    merged = (ga * y_a + gb * y_b).astype(BF16)
    x2 = x_ref[...] + _dot(merged, wo_ref[...])
    x2_ref[...] = x2

    ms = jnp.mean(x2 * x2, axis=-1, keepdims=True)
    xn2 = x2 * lax.rsqrt(ms + EPS) * gffn_ref[...]
    xh = xn2.astype(BF16)
    xn2_ref[...] = xh
    xl = (xn2 - xh.astype(F32)).astype(BF16)
    lg = _dot(xh, wrh_ref[...]) + _dot(xl, wrh_ref[...]) + _dot(xh, wrl_ref[...]) + br_ref[...]

    lane = lax.broadcasted_iota(jnp.int32, (ts, LANES), 1)
    lanef = lane.astype(F32)
    big = float(4 * LANES)
    gl = jnp.where(lane < N_GROUPS, lg, -jnp.inf)
    gmax = jnp.max(gl, axis=-1, keepdims=True)
    g_sel = jnp.min(jnp.where(gl == gmax, lanef, big), axis=-1, keepdims=True)
    p_g = 1.0 / jnp.sum(jnp.exp(gl - gmax), axis=-1, keepdims=True)
    lo = ROUTER_LANE0 + EXPERTS_PER_GROUP * g_sel
    el = jnp.where((lanef >= lo) & (lanef < lo + EXPERTS_PER_GROUP), lg, -jnp.inf)
    m1 = jnp.max(el, axis=-1, keepdims=True)
    i1 = jnp.min(jnp.where(el == m1, lanef, big), axis=-1, keepdims=True)
    el2 = jnp.where(lanef == i1, -jnp.inf, el)
    m2 = jnp.max(el2, axis=-1, keepdims=True)
    i2 = jnp.min(jnp.where(el2 == m2, lanef, big), axis=-1, keepdims=True)
    e2x = jnp.exp(m2 - m1)
    gate1 = p_g / (1.0 + e2x)
    gate2 = p_g * e2x / (1.0 + e2x)

    oh1 = lanef == i1
    oh2 = lanef == i2
    ohs = jnp.where(oh1 | oh2, 1.0, 0.0)
    n_loc = jnp.sum(ohs, axis=0, keepdims=True)
    pieces = jnp.floor((n_loc + (ROW_PIECE - 1.0)) * (1.0 / ROW_PIECE))
    piece_off = _dot(jnp.broadcast_to(pieces, (8, LANES)).astype(BF16), ut_ref[...])[0:1, :]
    base = _dot(stri_ref[...], ohs.astype(BF16)) + ROW_PIECE * piece_off
    slot1 = jnp.sum(jnp.where(oh1, base, 0.0), axis=-1, keepdims=True)
    slot2 = jnp.sum(jnp.where(oh2, base, 0.0), axis=-1, keepdims=True)
    tstat_ref[...] = jnp.broadcast_to(pieces, tstat_ref.shape).astype(jnp.int32)

    rti = jnp.where(lane == 0, i1 - ROUTER_LANE0,
                    jnp.where(lane == 1, i2 - ROUTER_LANE0,
                              jnp.where(lane == 2, slot1, jnp.where(lane == 3, slot2, 0.0))))
    rti_ref[...] = rti.astype(jnp.int32)
    rtf_ref[...] = jnp.where(lane == 0, gate1, jnp.where(lane == 1, gate2, 0.0))
    h1 = jnp.floor(slot1 * (1.0 / SLOT_RADIX))
    h2 = jnp.floor(slot2 * (1.0 / SLOT_RADIX))
    parts = jnp.where(lane == 0, h1, jnp.where(lane == 1, slot1 - SLOT_RADIX * h1,
                      jnp.where(lane == 2, h2, jnp.where(lane == 3, slot2 - SLOT_RADIX * h2, 0.0))))
    srow_ref[...] = _dot_nt(sel_ref[...], parts.astype(BF16))


def _mixer(x2d, zm, zg, zif, zift, params, batch, seq):
    T, D = x2d.shape
    ts = min(MIX_TS, seq)
    nts = seq // ts
    d_pool = params["w_br_a"].shape[0]
    d_ml = params["w_br_b"].shape[0]
    dh = d_ml // N_HEADS

    idx = np.arange(ts)
    same_chunk = (idx[:, None] // CHUNK) == (idx[None, :] // CHUNK)
    tri_c = jnp.asarray((idx[None, :] <= idx[:, None]) & same_chunk, BF16)
    tri_r = jnp.asarray((idx[:, None] <= idx[None, :]) & same_chunk, BF16)
    stri = jnp.asarray(idx[None, :] < idx[:, None], BF16)
    lane_idx = np.arange(LANES)
    ut = jnp.asarray(lane_idx[:, None] < lane_idx[None, :], BF16)
    sel = jnp.asarray(np.arange(8)[:, None] == lane_idx[None, :], BF16)

    tok = lambda b, j: (b * nts + j, 0)
    tok_t = lambda b, j: (0, b * nts + j)
    c2 = lambda b, j: (0, 0)
    c3 = lambda b, j: (0, 0, 0)
    full = lambda a: pl.BlockSpec(a.shape, c2 if a.ndim == 2 else c3)
    consts = [params[n] for n in ("b_if", "b_if_t", "conv_q", "conv_k", "g_head", "w_pool", "pool_scale",
                                  "w_br_a", "w_br_b", "w_out", "g_ffn", "w_r_hi", "w_r_lo", "b_r")]
    consts += [tri_c, tri_r, stri, ut, sel]
    n_tiles = batch * nts
    return pl.pallas_call(
        _mixer_kernel,
        grid=(batch, nts),
        in_specs=[pl.BlockSpec((ts, D), tok),
                  pl.BlockSpec((ts, zm.shape[1]), tok),
                  pl.BlockSpec((ts, zg.shape[1]), tok),
                  pl.BlockSpec((ts, LANES), tok),
                  pl.BlockSpec((16, ts), tok_t)] + [full(a) for a in consts],
        out_specs=[pl.BlockSpec((ts, D), tok),
                   pl.BlockSpec((ts, D), tok),
                   pl.BlockSpec((ts, LANES), tok),
                   pl.BlockSpec((ts, LANES), tok),
                   pl.BlockSpec((8, LANES), tok),
                   pl.BlockSpec((8, ts), tok_t)],
        out_shape=[jax.ShapeDtypeStruct((T, D), F32),
                   jax.ShapeDtypeStruct((T, D), BF16),
                   jax.ShapeDtypeStruct((T, LANES), jnp.int32),
                   jax.ShapeDtypeStruct((T, LANES), F32),
                   jax.ShapeDtypeStruct((n_tiles * 8, LANES), jnp.int32),
                   jax.ShapeDtypeStruct((8, T), F32)],
        scratch_shapes=[pltpu.VMEM((HALO, d_pool + 2 * d_ml), F32),
                        pltpu.VMEM((ts, d_ml), BF16),
                        pltpu.VMEM((ts, d_ml), BF16),
                        pltpu.VMEM((ts, d_ml), F32),
                        pltpu.VMEM((ts, d_pool), BF16),
                        pltpu.VMEM((N_HEADS, dh, 2 * dh), F32),
                        pltpu.VMEM((8, LANES), F32)],
        compiler_params=_cparams(2),
        name="mixer",
    )(x2d, zm, zg, zif, zift, *consts)


def _for_each_piece(cnt_ref, loc_ref, glob_ref, tile, fn):
    def per_expert(e, carry):
        idx = tile * N_EXPERTS + e
        loc = loc_ref[idx]
        glob = glob_ref[idx]

        def per_piece(p, c2):
            fn(pl.multiple_of((loc + p) * ROW_PIECE, ROW_PIECE), pl.multiple_of((glob + p) * ROW_PIECE, ROW_PIECE))
            return c2

        lax.fori_loop(0, cnt_ref[idx], per_piece, 0)
        return carry

    lax.fori_loop(0, N_EXPERTS, per_expert, 0)


def _dispatch_kernel(cnt_ref, loc_ref, glob_ref, zflag_ref,
                     xn_ref, srow_ref, buf_ref, rows_ref, zeros_ref, sem, zsem):
    tt = xn_ref.shape[0]
    sl = rows_ref.shape[1]
    n_blocks = buf_ref.shape[0] // MOE_TM
    i = pl.program_id(0)
    n_tiles = pl.num_programs(0)
    cur = lax.rem(i, 2)

    @pl.when(i == 0)
    def _():
        zeros_ref[...] = jnp.zeros_like(zeros_ref)

        def zero_copy(b):
            return pltpu.make_async_copy(zeros_ref, buf_ref.at[pl.ds(b * MOE_TM, MOE_TM)], zsem)

        def zero_start(b, carry):
            @pl.when(zflag_ref[b] > 0)
            def _():
                zero_copy(b).start()
            return carry

        def zero_wait(b, carry):
            @pl.when(zflag_ref[b] > 0)
            def _():
                zero_copy(b).wait()
            return carry

        lax.fori_loop(0, n_blocks, zero_start, 0)
        lax.fori_loop(0, n_blocks, zero_wait, 0)

    def piece_copy(buf_slot, local_row, global_row):
        return pltpu.make_async_copy(rows_ref.at[buf_slot, pl.ds(local_row, ROW_PIECE)],
                                     buf_ref.at[pl.ds(global_row, ROW_PIECE)], sem.at[buf_slot])

    def start_pieces(tile, buf_slot):
        _for_each_piece(cnt_ref, loc_ref, glob_ref, tile, lambda l, g: piece_copy(buf_slot, l, g).start())

    def wait_pieces(tile, buf_slot):
        _for_each_piece(cnt_ref, loc_ref, glob_ref, tile, lambda l, g: piece_copy(buf_slot, l, g).wait())

    @pl.when(i >= 2)
    def _():
        wait_pieces(i - 2, cur)

    sr = srow_ref[...]
    slot1 = SLOT_RADIX * sr[0:1, :] + sr[1:2, :]
    slot2 = SLOT_RADIX * sr[2:3, :] + sr[3:4, :]
    r = lax.broadcasted_iota(jnp.int32, (sl, tt), 0).astype(F32)
    sel = jnp.where((r == slot1) | (r == slot2), 1.0, 0.0).astype(BF16)
    rows_ref[cur] = _dot(sel, xn_ref[...])
    start_pieces(i, cur)

    @pl.when(i == n_tiles - 1)
    def _():
        @pl.when(i >= 1)
        def _():
            wait_pieces(i - 1, 1 - cur)
        wait_pieces(i, cur)


def _dispatch(xn2, srow, piece_cnt, piece_loc, piece_glob, zflag, n_rows):
    T, D = xn2.shape
    tt = MIX_TS
    return pl.pallas_call(
        _dispatch_kernel,
        grid_spec=pltpu.PrefetchScalarGridSpec(
            num_scalar_prefetch=4,
            grid=(T // tt,),
            in_specs=[pl.BlockSpec((tt, D), lambda i, *_: (i, 0)),
                      pl.BlockSpec((8, tt), lambda i, *_: (0, i))],
            out_specs=pl.BlockSpec(memory_space=pl.ANY),
            scratch_shapes=[pltpu.VMEM((2, MOE_SL, D), F32),
                            pltpu.VMEM((MOE_TM, D), F32),
                            pltpu.SemaphoreType.DMA((2,)),
                            pltpu.SemaphoreType.DMA(())]),
        out_shape=jax.ShapeDtypeStruct((n_rows, D), F32),
        compiler_params=_cparams(1),
        name="dispatch",
    )(piece_cnt, piece_loc, piece_glob, zflag, xn2, srow)


def _experts_kernel(blk_e_ref, nused_ref, x_ref, wg_ref, wu_ref, wd_ref, y_ref):
    used = pl.program_id(0) < nused_ref[0]

    @pl.when(used)
    def _():
        x = x_ref[...].astype(BF16)
        hg = _dot(x, wg_ref[0])
        hu = _dot(x, wu_ref[0])
        hid = (_silu(hg) * hu).astype(BF16)
        y_ref[...] = _dot(hid, wd_ref[0])

    @pl.when(jnp.logical_not(used))
    def _():
        y_ref[...] = jnp.zeros_like(y_ref)


def _experts(buf, blk_e, nused, w_gate, w_up, w_down):
    R, D = buf.shape
    de = w_gate.shape[2]
    n_blocks = R // MOE_TM

    def blk(i, blk_e_ref, nused_ref):
        return jnp.minimum(i, nused_ref[0] - 1)

    row_map = lambda i, be, nu: (blk(i, be, nu), 0)
    w_map = lambda i, be, nu: (be[blk(i, be, nu)], 0, 0)
    return pl.pallas_call(
        _experts_kernel,
        grid_spec=pltpu.PrefetchScalarGridSpec(
            num_scalar_prefetch=2,
            grid=(n_blocks,),
            in_specs=[pl.BlockSpec((MOE_TM, D), row_map),
                      pl.BlockSpec((1, D, de), w_map),
                      pl.BlockSpec((1, D, de), w_map),
                      pl.BlockSpec((1, de, D), w_map)],
            out_specs=pl.BlockSpec((MOE_TM, D), lambda i, be, nu: (i, 0))),
        out_shape=jax.ShapeDtypeStruct((R, D), F32),
        compiler_params=_cparams(1),
        name="experts",
    )(blk_e, nused, buf, w_gate, w_up, w_down)


def _combine_kernel(cnt_ref, loc_ref, glob_ref, x2_ref, rti_ref, rtf_ref, gfin_ref, yb_ref, out_ref,
                    rows_ref, sem):
    tt = x2_ref.shape[0]
    sl = rows_ref.shape[1]
    i = pl.program_id(0)
    n_tiles = pl.num_programs(0)
    cur = lax.rem(i, 2)

    def piece_copy(buf_slot, local_row, global_row):
        return pltpu.make_async_copy(yb_ref.at[pl.ds(global_row, ROW_PIECE)],
                                     rows_ref.at[buf_slot, pl.ds(local_row, ROW_PIECE)], sem.at[buf_slot])

    def start_pieces(tile, buf_slot):
        _for_each_piece(cnt_ref, loc_ref, glob_ref, tile, lambda l, g: piece_copy(buf_slot, l, g).start())

    def wait_pieces(tile, buf_slot):
        _for_each_piece(cnt_ref, loc_ref, glob_ref, tile, lambda l, g: piece_copy(buf_slot, l, g).wait())

    @pl.when(i == 0)
    def _():
        rows_ref[...] = jnp.zeros_like(rows_ref)
        start_pieces(0, 0)

    @pl.when(i + 1 < n_tiles)
    def _():
        start_pieces(i + 1, 1 - cur)

    wait_pieces(i, cur)

    rti = rti_ref[...]
    rtf = rtf_ref[...]
    slot1 = rti[:, 2:3]
    slot2 = rti[:, 3:4]
    lane = lax.broadcasted_iota(jnp.int32, (tt, sl), 1)
    g = jnp.where(lane == slot1, rtf[:, 0:1], jnp.where(lane == slot2, rtf[:, 1:2], 0.0)).astype(BF16)
    y = x2_ref[...] + _dot(g, rows_ref[cur].astype(BF16))
    ms = jnp.mean(y * y, axis=-1, keepdims=True)
    out_ref[...] = y * lax.rsqrt(ms + EPS) * gfin_ref[...]


def _combine(x2, rti, rtf, g_final, yb, piece_cnt, piece_loc, piece_glob):
    T, D = x2.shape
    tt = MIX_TS
    tok = lambda i, *_: (i, 0)
    return pl.pallas_call(
        _combine_kernel,
        grid_spec=pltpu.PrefetchScalarGridSpec(
            num_scalar_prefetch=3,
            grid=(T // tt,),
            in_specs=[pl.BlockSpec((tt, D), tok),
                      pl.BlockSpec((tt, LANES), tok),
                      pl.BlockSpec((tt, LANES), tok),
                      pl.BlockSpec((1, D), lambda i, *_: (0, 0)),
                      pl.BlockSpec(memory_space=pl.ANY)],
            out_specs=pl.BlockSpec((tt, D), tok),
            scratch_shapes=[pltpu.VMEM((2, MOE_SL, D), F32),
                            pltpu.SemaphoreType.DMA((2,))]),
        out_shape=jax.ShapeDtypeStruct((T, D), F32),
        compiler_params=_cparams(1),
        name="combine",
    )(piece_cnt, piece_loc, piece_glob, x2, rti, rtf, g_final, yb)


def _pad_lanes(a, width=LANES):
    return jnp.pad(a, ((0, 0), (0, width - a.shape[1])))


def kernel(x, g_mix, w_in, b_if, conv_q, conv_k, g_head, w_pool, pool_scale, w_br_a, w_br_b, w_out,
           g_ffn, w_rg, b_rg, w_re, b_re, w_e_gate, w_e_up, w_e_down, g_final):
    B, S, D = x.shape
    T = B * S
    assert g_mix.shape[0] == 1, "single-layer block"
    assert S % MIX_TS == 0 and T % INPROJ_TM == 0
    d_pool = w_br_a.shape[1]
    d_ml = w_br_b.shape[1]
    x2d = x.reshape(T, D)

    n_main = d_pool + 4 * d_ml
    w_l = w_in[0]
    w_main = w_l[:, :n_main].astype(BF16)
    w_if = w_l[:, n_main:n_main + 2 * N_HEADS]
    w_gates = w_l[:, n_main + 2 * N_HEADS:].astype(BF16)
    w_if_c = _pad_lanes(w_if).astype(BF16)
    w_if_t = jnp.pad(w_if.T, ((0, 16 - 2 * N_HEADS), (0, 0))).astype(BF16)
    w_r = _pad_lanes(jnp.concatenate([w_rg[0], w_re[0]], axis=1))
    w_r_hi = w_r.astype(BF16)
    w_r_lo = (w_r - w_r_hi.astype(F32)).astype(BF16)
    params = {
        "b_if": _pad_lanes(b_if[0][None, :]),
        "b_if_t": jnp.pad(b_if[0][:, None], ((0, 16 - 2 * N_HEADS), (0, 0))),
        "conv_q": conv_q[0], "conv_k": conv_k[0],
        "g_head": g_head[0][None, :],
        "w_pool": w_pool[0].astype(BF16),
        "pool_scale": pool_scale[0][None, :],
        "w_br_a": w_br_a[0].astype(BF16), "w_br_b": w_br_b[0].astype(BF16),
        "w_out": w_out[0].astype(BF16),
        "g_ffn": g_ffn[0][None, :],
        "w_r_hi": w_r_hi, "w_r_lo": w_r_lo,
        "b_r": _pad_lanes(jnp.concatenate([b_rg[0], b_re[0]])[None, :]),
    }

    zm, zg, zif, zift = _inproj(x2d, g_mix[0][None, :], w_main, w_gates, w_if_c, w_if_t)
    x2, xn2, rti, rtf, tstat, srow = _mixer(x2d, zm, zg, zif, zift, params, B, S)

    n_tiles = T // MIX_TS
    pcs = tstat.reshape(n_tiles, 8, LANES)[:, 0, ROUTER_LANE0:ROUTER_LANE0 + N_EXPERTS]
    piece_loc = jnp.cumsum(pcs, axis=1) - pcs
    rows_e = jnp.sum(pcs, axis=0) * ROW_PIECE
    padded = (rows_e + MOE_TM - 1) // MOE_TM * MOE_TM
    pend = jnp.cumsum(padded)
    poff = pend - padded
    piece_glob = poff[None, :] // ROW_PIECE + jnp.cumsum(pcs, axis=0) - pcs
    n_rows = n_tiles * MOE_SL + N_EXPERTS * MOE_TM
    n_blocks = n_rows // MOE_TM
    nused = (pend[-1:] // MOE_TM).astype(jnp.int32)
    blk_ids = jnp.arange(n_blocks, dtype=jnp.int32)
    blk_e = jnp.sum((pend[None, :] <= blk_ids[:, None] * MOE_TM).astype(jnp.int32), axis=1)
    blk_e = jnp.minimum(blk_e, N_EXPERTS - 1)
    zflag = (((blk_ids + 1) * MOE_TM == pend[blk_e]) | (blk_ids >= nused[0])).astype(jnp.int32)
    flat = lambda a: a.astype(jnp.int32).reshape(n_tiles * N_EXPERTS)
    piece_cnt, piece_loc, piece_glob = flat(pcs), flat(piece_loc), flat(piece_glob)

    buf = _dispatch(xn2, srow, piece_cnt, piece_loc, piece_glob, zflag, n_rows)
    yb = _experts(buf, blk_e, nused, w_e_gate[0].astype(BF16), w_e_up[0].astype(BF16), w_e_down[0].astype(BF16))
    out = _combine(x2, rti, rtf, g_final[None, :], yb, piece_cnt, piece_loc, piece_glob)
    return out.reshape(B, S, D)
```

```python
import functools

import numpy as np
import jax
import jax.numpy as jnp
from jax import lax
from jax.experimental import pallas as pl
from jax.experimental.pallas import tpu as pltpu

F32 = jnp.float32
BF16 = jnp.bfloat16

CHUNK = 64
POOL_WINDOWS = (2, 4, 8, 16)
N_HEADS = 4
CONV_K = 4
N_GROUPS = 4
EXPERTS_PER_GROUP = 8
N_EXPERTS = N_GROUPS * EXPERTS_PER_GROUP
TOP_K = 2
EPS = 1e-6

LANES = 128
HALO = 16
ROUTER_LANE0 = N_GROUPS

INPROJ_TM = 512
MIX_TS = 256
MOE_TM = 256
ROW_PIECE = 8
MOE_SL = TOP_K * MIX_TS + N_EXPERTS * ROW_PIECE
PIECES_PER_TILE = MOE_SL // ROW_PIECE
SLOT_RADIX = 16
VMEM_LIMIT = 56 * 1024 * 1024


def _cparams(n_axes):
    return pltpu.CompilerParams(dimension_semantics=("arbitrary",) * n_axes,
                                vmem_limit_bytes=VMEM_LIMIT)


def _sigmoid(v):
    return 0.5 * jnp.tanh(0.5 * v) + 0.5


def _silu(v):
    return v * _sigmoid(v)


def _log_sigmoid(v):
    return jnp.minimum(v, 0.0) - jnp.log1p(jnp.exp(-jnp.abs(v)))


def _split3(v):
    hi = v.astype(BF16)
    r1 = v - hi.astype(F32)
    mid = r1.astype(BF16)
    lo = (r1 - mid.astype(F32)).astype(BF16)
    return hi, mid, lo


def _dot(a, b):
    return jnp.dot(a, b, preferred_element_type=F32)


def _dot_nt(a, b):
    return lax.dot_general(a, b, (((1,), (1,)), ((), ())), preferred_element_type=F32)


def _dot_tn(a, b):
    return lax.dot_general(a, b, (((0,), (0,)), ((), ())), preferred_element_type=F32)


def _inproj_kernel(x_ref, g_ref, wm_ref, wg_ref, wif_ref, wift_ref,
                   zm_ref, zg_ref, zif_ref, zift_ref):
    x = x_ref[...]
    ms = jnp.mean(x * x, axis=-1, keepdims=True)
    xn = (x * lax.rsqrt(ms + EPS) * g_ref[...]).astype(BF16)
    zm_ref[...] = _dot(xn, wm_ref[...]).astype(BF16)
    zg_ref[...] = _dot(xn, wg_ref[...]).astype(BF16)
    zif_ref[...] = _dot(xn, wif_ref[...])
    zift_ref[...] = _dot_nt(wift_ref[...], xn)


def _inproj(x2d, g_mix, w_main, w_gate, w_if, w_if_t):
    T, D = x2d.shape
    tm = min(INPROJ_TM, T)
    nm, ng = w_main.shape[1], w_gate.shape[1]
    const = lambda i: (0, 0)
    return pl.pallas_call(
        _inproj_kernel,
        grid=(T // tm,),
        in_specs=[pl.BlockSpec((tm, D), lambda i: (i, 0)),
                  pl.BlockSpec((1, D), const),
                  pl.BlockSpec((D, nm), const),
                  pl.BlockSpec((D, ng), const),
                  pl.BlockSpec((D, LANES), const),
                  pl.BlockSpec((16, D), const)],
        out_specs=[pl.BlockSpec((tm, nm), lambda i: (i, 0)),
                   pl.BlockSpec((tm, ng), lambda i: (i, 0)),
                   pl.BlockSpec((tm, LANES), lambda i: (i, 0)),
                   pl.BlockSpec((16, tm), lambda i: (0, i))],
        out_shape=[jax.ShapeDtypeStruct((T, nm), BF16),
                   jax.ShapeDtypeStruct((T, ng), BF16),
                   jax.ShapeDtypeStruct((T, LANES), F32),
                   jax.ShapeDtypeStruct((16, T), F32)],
        compiler_params=_cparams(1),
        name="inproj",
    )(x2d, g_mix, w_main, w_gate, w_if, w_if_t)


def _mixer_kernel(x_ref, zm_ref, zg_ref, zif_ref, zift_ref,
                  bif_ref, bift_ref, convq_ref, convk_ref, ghead_ref, wpool_ref, pscale_ref,
                  wa_ref, wb_ref, wo_ref, gffn_ref, wrh_ref, wrl_ref, br_ref,
                  tric_ref, trir_ref, stri_ref, ut_ref, sel_ref,
                  x2_ref, xn2_ref, rti_ref, rtf_ref, tstat_ref, srow_ref,
                  halo_ref, q_ref, k_ref, h_ref, pool_ref, cst_ref, mst_ref):
    ts = x_ref.shape[0]
    d_pool = wa_ref.shape[0]
    d_ml = wb_ref.shape[0]
    dh = d_ml // N_HEADS
    n_chunks = ts // CHUNK
    j = pl.program_id(1)

    @pl.when(j == 0)
    def _():
        halo_ref[...] = jnp.zeros_like(halo_ref)
        cst_ref[...] = jnp.zeros_like(cst_ref)
        mst_ref[...] = jnp.zeros_like(mst_ref)

    row = lax.broadcasted_iota(jnp.int32, (ts, LANES), 0)
    pos1 = (row + j * ts + 1).astype(F32)

    def extended(cg):
        cols = slice(cg * LANES, (cg + 1) * LANES)
        cur = zm_ref[:, cols].astype(F32)
        ext = jnp.concatenate([halo_ref[:, cols], cur], axis=0)
        halo_ref[:, cols] = cur[ts - HALO:, :]
        return ext

    n_pool_groups = d_pool // LANES
    for g in range(n_pool_groups):
        w = POOL_WINDOWS[g]
        ext = extended(g)
        s, span = ext, 1
        while span < w:
            s = s + pltpu.roll(s, span, axis=0)
            span *= 2
        win = s[HALO:, :]
        cnt = jnp.minimum(pos1, float(w))
        d = win / cnt - ext[HALO:, :]
        y = _dot(d.astype(BF16), wpool_ref[g]) * pscale_ref[:, g * LANES:(g + 1) * LANES]
        pool_ref[:, g * LANES:(g + 1) * LANES] = y.astype(BF16)

    n_ml_groups = d_ml // LANES
    for which, (cw_ref, dst_ref, scale) in enumerate(((convq_ref, q_ref, 1.0), (convk_ref, k_ref, dh ** -0.5))):
        for g in range(n_ml_groups):
            cols = slice(g * LANES, (g + 1) * LANES)
            ext = extended(n_pool_groups + which * n_ml_groups + g)
            acc = ext * cw_ref[CONV_K - 1:CONV_K, cols]
            for sft in range(1, CONV_K):
                acc = acc + pltpu.roll(ext, sft, axis=0) * cw_ref[CONV_K - 1 - sft:CONV_K - sft, cols]
            dst_ref[:, cols] = (_silu(acc[HALO:, :]) * scale).astype(BF16)

    zc = zif_ref[...] + bif_ref[...]
    lf_c = _log_sigmoid(zc)
    bc = sum(_dot(tric_ref[...], p) for p in _split3(lf_c))
    zr = zift_ref[...] + bift_ref[...]
    lf_r = _log_sigmoid(zr)
    br = sum(_dot(p, trir_ref[...]) for p in _split3(lf_r))

    ti = lax.broadcasted_iota(jnp.int32, (CHUNK, CHUNK), 0)
    si = lax.broadcasted_iota(jnp.int32, (CHUNK, CHUNK), 1)
    causal = si <= ti
    ones_blk = jnp.ones((CHUNK, dh), BF16)
    v0 = d_pool + 2 * d_ml
    ig_rep = [jnp.broadcast_to(zc[:, h:h + 1], (ts, dh)) for h in range(N_HEADS)]
    bt_rep = [jnp.broadcast_to(bc[:, N_HEADS + h:N_HEADS + h + 1], (ts, dh)) for h in range(N_HEADS)]

    m_state = [mst_ref[h:h + 1, :] for h in range(N_HEADS)]
    c_state = [cst_ref[h] for h in range(N_HEADS)]
    for c in range(n_chunks):
        rs = slice(c * CHUNK, (c + 1) * CHUNK)
        for h in range(N_HEADS):
            hs = slice(h * dh, (h + 1) * dh)
            q = q_ref[rs, hs]
            k = k_ref[rs, hs]
            v_aug = jnp.concatenate([zm_ref[rs, v0 + h * dh:v0 + (h + 1) * dh], ones_blk], axis=-1)
            bt = bt_rep[h][rs, :]
            igc = ig_rep[h][rs, :]
            r_row = zr[h:h + 1, rs] - br[N_HEADS + h:N_HEADS + h + 1, rs]
            m_prev = m_state[h]
            c_prev = c_state[h]

            dmat = jnp.where(causal, bt[:, :CHUNK] + r_row, -jnp.inf)
            m_intra = jnp.max(dmat, axis=-1, keepdims=True)
            inter = bt + m_prev
            m_t = jnp.maximum(inter, m_intra)
            w_inter = jnp.exp(inter - m_t)
            smat = _dot_nt(q, k) * jnp.exp(dmat - m_t[:, :CHUNK])
            qc = _dot(q, c_prev.astype(BF16))
            sv = _dot(smat.astype(BF16), v_aug)
            nq = w_inter * qc[:, dh:] + sv[:, dh:]
            den = jnp.maximum(jnp.abs(nq), jnp.exp(-m_t))
            h_ref[rs, hs] = (w_inter * qc[:, :dh] + sv[:, :dh]) / den

            b_last = bt[CHUNK - 1:CHUNK, :]
            a_log = b_last - bt + igc
            a_max = jnp.max(a_log, axis=0, keepdims=True)
            m_new = jnp.maximum(b_last + m_prev, a_max)
            kw = (k.astype(F32) * jnp.exp(a_log - m_new)).astype(BF16)
            decay = jnp.exp(b_last + m_prev - m_new)
            c_state[h] = jnp.concatenate([decay, decay], axis=-1) * c_prev + _dot_tn(kw, v_aug)
            m_state[h] = m_new
    for h in range(N_HEADS):
        cst_ref[h] = c_state[h]
        mst_ref[h:h + 1, :] = m_state[h]

    o0 = v0 + d_ml
    for h in range(N_HEADS):
        hs = slice(h * dh, (h + 1) * dh)
        hv = h_ref[:, hs]
        mu = jnp.mean(hv, axis=-1, keepdims=True)
        hc = hv - mu
        var = jnp.mean(hc * hc, axis=-1, keepdims=True)
        hn = hc * lax.rsqrt(var + EPS) * ghead_ref[:, hs]
        og = _sigmoid(zm_ref[:, o0 + h * dh:o0 + (h + 1) * dh].astype(F32))
        q_ref[:, hs] = (og * hn).astype(BF16)
    y_a = _dot(pool_ref[...], wa_ref[...])
    y_b = _dot(q_ref[...], wb_ref[...])
    d_model = x_ref.shape[1]
    ga = _sigmoid(zg_ref[:, :d_model].astype(F32))
    gb = _sigmoid(zg_ref[:, d_model:].astype(F32))
    merged = (ga * y_a + gb * y_b).astype(BF16)
    x2 = x_ref[...] + _dot(merged, wo_ref[...])
    x2_ref[...] = x2

    ms = jnp.mean(x2 * x2, axis=-1, keepdims=True)
    xn2 = x2 * lax.rsqrt(ms + EPS) * gffn_ref[...]
    xh = xn2.astype(BF16)
    xn2_ref[...] = xh
    xl = (xn2 - xh.astype(F32)).astype(BF16)
    lg = _dot(xh, wrh_ref[...]) + _dot(xl, wrh_ref[...]) + _dot(xh, wrl_ref[...]) + br_ref[...]

    lane = lax.broadcasted_iota(jnp.int32, (ts, LANES), 1)
    lanef = lane.astype(F32)
    big = float(4 * LANES)
    gl = jnp.where(lane < N_GROUPS, lg, -jnp.inf)
    gmax = jnp.max(gl, axis=-1, keepdims=True)
    g_sel = jnp.min(jnp.where(gl == gmax, lanef, big), axis=-1, keepdims=True)
    p_g = 1.0 / jnp.sum(jnp.exp(gl - gmax), axis=-1, keepdims=True)
    lo = ROUTER_LANE0 + EXPERTS_PER_GROUP * g_sel
    el = jnp.where((lanef >= lo) & (lanef < lo + EXPERTS_PER_GROUP), lg, -jnp.inf)
    m1 = jnp.max(el, axis=-1, keepdims=True)
    i1 = jnp.min(jnp.where(el == m1, lanef, big), axis=-1, keepdims=True)
    el2 = jnp.where(lanef == i1, -jnp.inf, el)
    m2 = jnp.max(el2, axis=-1, keepdims=True)
    i2 = jnp.min(jnp.where(el2 == m2, lanef, big), axis=-1, keepdims=True)
    e2x = jnp.exp(m2 - m1)
    gate1 = p_g / (1.0 + e2x)
    gate2 = p_g * e2x / (1.0 + e2x)

    oh1 = lanef == i1
    oh2 = lanef == i2
    ohs = jnp.where(oh1 | oh2, 1.0, 0.0)
    n_loc = jnp.sum(ohs, axis=0, keepdims=True)
    pieces = jnp.floor((n_loc + (ROW_PIECE - 1.0)) * (1.0 / ROW_PIECE))
    piece_off = _dot(jnp.broadcast_to(pieces, (8, LANES)).astype(BF16), ut_ref[...])[0:1, :]
    base = _dot(stri_ref[...], ohs.astype(BF16)) + ROW_PIECE * piece_off
    slot1 = jnp.sum(jnp.where(oh1, base, 0.0), axis=-1, keepdims=True)
    slot2 = jnp.sum(jnp.where(oh2, base, 0.0), axis=-1, keepdims=True)
    tstat_ref[...] = jnp.broadcast_to(pieces, tstat_ref.shape).astype(jnp.int32)

    rti = jnp.where(lane == 0, i1 - ROUTER_LANE0,
                    jnp.where(lane == 1, i2 - ROUTER_LANE0,
                              jnp.where(lane == 2, slot1, jnp.where(lane == 3, slot2, 0.0))))
    rti_ref[...] = rti.astype(jnp.int32)
    rtf_ref[...] = jnp.where(lane == 0, gate1, jnp.where(lane == 1, gate2, 0.0))
    h1 = jnp.floor(slot1 * (1.0 / SLOT_RADIX))
    h2 = jnp.floor(slot2 * (1.0 / SLOT_RADIX))
    parts = jnp.where(lane == 0, h1, jnp.where(lane == 1, slot1 - SLOT_RADIX * h1,
                      jnp.where(lane == 2, h2, jnp.where(lane == 3, slot2 - SLOT_RADIX * h2, 0.0))))
    srow_ref[...] = _dot_nt(sel_ref[...], parts.astype(BF16))


def _mixer(x2d, zm, zg, zif, zift, params, batch, seq):
    T, D = x2d.shape
    ts = min(MIX_TS, seq)
    nts = seq // ts
    d_pool = params["w_br_a"].shape[0]
    d_ml = params["w_br_b"].shape[0]
    dh = d_ml // N_HEADS

    idx = np.arange(ts)
    same_chunk = (idx[:, None] // CHUNK) == (idx[None, :] // CHUNK)
    tri_c = jnp.asarray((idx[None, :] <= idx[:, None]) & same_chunk, BF16)
    tri_r = jnp.asarray((idx[:, None] <= idx[None, :]) & same_chunk, BF16)
    stri = jnp.asarray(idx[None, :] < idx[:, None], BF16)
    lane_idx = np.arange(LANES)
    ut = jnp.asarray(lane_idx[:, None] < lane_idx[None, :], BF16)
    sel = jnp.asarray(np.arange(8)[:, None] == lane_idx[None, :], BF16)

    tok = lambda b, j: (b * nts + j, 0)
    tok_t = lambda b, j: (0, b * nts + j)
    c2 = lambda b, j: (0, 0)
    c3 = lambda b, j: (0, 0, 0)
    full = lambda a: pl.BlockSpec(a.shape, c2 if a.ndim == 2 else c3)
    consts = [params[n] for n in ("b_if", "b_if_t", "conv_q", "conv_k", "g_head", "w_pool", "pool_scale",
                                  "w_br_a", "w_br_b", "w_out", "g_ffn", "w_r_hi", "w_r_lo", "b_r")]
    consts += [tri_c, tri_r, stri, ut, sel]
    n_tiles = batch * nts
    return pl.pallas_call(
        _mixer_kernel,
        grid=(batch, nts),
        in_specs=[pl.BlockSpec((ts, D), tok),
                  pl.BlockSpec((ts, zm.shape[1]), tok),
                  pl.BlockSpec((ts, zg.shape[1]), tok),
                  pl.BlockSpec((ts, LANES), tok),
                  pl.BlockSpec((16, ts), tok_t)] + [full(a) for a in consts],
        out_specs=[pl.BlockSpec((ts, D), tok),
                   pl.BlockSpec((ts, D), tok),
                   pl.BlockSpec((ts, LANES), tok),
                   pl.BlockSpec((ts, LANES), tok),
                   pl.BlockSpec((8, LANES), tok),
                   pl.BlockSpec((8, ts), tok_t)],
        out_shape=[jax.ShapeDtypeStruct((T, D), F32),
                   jax.ShapeDtypeStruct((T, D), BF16),
                   jax.ShapeDtypeStruct((T, LANES), jnp.int32),
                   jax.ShapeDtypeStruct((T, LANES), F32),
                   jax.ShapeDtypeStruct((n_tiles * 8, LANES), jnp.int32),
                   jax.ShapeDtypeStruct((8, T), F32)],
        scratch_shapes=[pltpu.VMEM((HALO, d_pool + 2 * d_ml), F32),
                        pltpu.VMEM((ts, d_ml), BF16),
                        pltpu.VMEM((ts, d_ml), BF16),
                        pltpu.VMEM((ts, d_ml), F32),
                        pltpu.VMEM((ts, d_pool), BF16),
                        pltpu.VMEM((N_HEADS, dh, 2 * dh), F32),
                        pltpu.VMEM((8, LANES), F32)],
        compiler_params=_cparams(2),
        name="mixer",
    )(x2d, zm, zg, zif, zift, *consts)


def _for_each_piece(npieces_ref, glob_ref, tile, fn):
    base = tile * PIECES_PER_TILE

    def per_piece(p, carry):
        fn(pl.multiple_of(p * ROW_PIECE, ROW_PIECE), pl.multiple_of(glob_ref[base + p] * ROW_PIECE, ROW_PIECE))
        return carry

    lax.fori_loop(0, npieces_ref[tile], per_piece, 0)


def _dispatch_kernel(npieces_ref, glob_ref, zflag_ref,
                     xn_ref, srow_ref, buf_ref, rows_ref, zeros_ref, sem, zsem):
    tt = xn_ref.shape[0]
    sl = rows_ref.shape[1]
    n_blocks = buf_ref.shape[0] // MOE_TM
    i = pl.program_id(0)
    n_tiles = pl.num_programs(0)
    cur = lax.rem(i, 2)

    @pl.when(i == 0)
    def _():
        zeros_ref[...] = jnp.zeros_like(zeros_ref)

        def zero_copy(b):
            return pltpu.make_async_copy(zeros_ref, buf_ref.at[pl.ds(b * MOE_TM, MOE_TM)], zsem)

        def zero_start(b, carry):
            @pl.when(zflag_ref[b] > 0)
            def _():
                zero_copy(b).start()
            return carry

        def zero_wait(b, carry):
            @pl.when(zflag_ref[b] > 0)
            def _():
                zero_copy(b).wait()
            return carry

        lax.fori_loop(0, n_blocks, zero_start, 0)
        lax.fori_loop(0, n_blocks, zero_wait, 0)

    def piece_copy(buf_slot, local_row, global_row):
        return pltpu.make_async_copy(rows_ref.at[buf_slot, pl.ds(local_row, ROW_PIECE)],
                                     buf_ref.at[pl.ds(global_row, ROW_PIECE)], sem.at[buf_slot])

    def start_pieces(tile, buf_slot):
        _for_each_piece(npieces_ref, glob_ref, tile, lambda l, g: piece_copy(buf_slot, l, g).start())

    def wait_pieces(tile, buf_slot):
        _for_each_piece(npieces_ref, glob_ref, tile, lambda l, g: piece_copy(buf_slot, l, g).wait())

    @pl.when(i >= 2)
    def _():
        wait_pieces(i - 2, cur)

    sr = srow_ref[...]
    slot1 = SLOT_RADIX * sr[0:1, :] + sr[1:2, :]
    slot2 = SLOT_RADIX * sr[2:3, :] + sr[3:4, :]
    r = lax.broadcasted_iota(jnp.int32, (sl, tt), 0).astype(F32)
    sel = jnp.where((r == slot1) | (r == slot2), 1.0, 0.0).astype(BF16)
    rows_ref[cur] = _dot(sel, xn_ref[...])
    start_pieces(i, cur)

    @pl.when(i == n_tiles - 1)
    def _():
        @pl.when(i >= 1)
        def _():
            wait_pieces(i - 1, 1 - cur)
        wait_pieces(i, cur)


def _dispatch(xn2, srow, npieces, piece_glob, zflag, n_rows):
    T, D = xn2.shape
    tt = MIX_TS
    return pl.pallas_call(
        _dispatch_kernel,
        grid_spec=pltpu.PrefetchScalarGridSpec(
            num_scalar_prefetch=3,
            grid=(T // tt,),
            in_specs=[pl.BlockSpec((tt, D), lambda i, *_: (i, 0)),
                      pl.BlockSpec((8, tt), lambda i, *_: (0, i))],
            out_specs=pl.BlockSpec(memory_space=pl.ANY),
            scratch_shapes=[pltpu.VMEM((2, MOE_SL, D), F32),
                            pltpu.VMEM((MOE_TM, D), F32),
                            pltpu.SemaphoreType.DMA((2,)),
                            pltpu.SemaphoreType.DMA(())]),
        out_shape=jax.ShapeDtypeStruct((n_rows, D), F32),
        compiler_params=_cparams(1),
        name="dispatch",
    )(npieces, piece_glob, zflag, xn2, srow)


def _experts_kernel(blk_e_ref, nused_ref, x_ref, wg_ref, wu_ref, wd_ref, y_ref):
    used = pl.program_id(0) < nused_ref[0]

    @pl.when(used)
    def _():
        x = x_ref[...].astype(BF16)
        hg = _dot(x, wg_ref[0])
        hu = _dot(x, wu_ref[0])
        hid = (_silu(hg) * hu).astype(BF16)
        y_ref[...] = _dot(hid, wd_ref[0])

    @pl.when(jnp.logical_not(used))
    def _():
        y_ref[...] = jnp.zeros_like(y_ref)


def _experts(buf, blk_e, nused, w_gate, w_up, w_down):
    R, D = buf.shape
    de = w_gate.shape[2]
    n_blocks = R // MOE_TM

    def blk(i, blk_e_ref, nused_ref):
        return jnp.minimum(i, nused_ref[0] - 1)

    row_map = lambda i, be, nu: (blk(i, be, nu), 0)
    w_map = lambda i, be, nu: (be[blk(i, be, nu)], 0, 0)
    return pl.pallas_call(
        _experts_kernel,
        grid_spec=pltpu.PrefetchScalarGridSpec(
            num_scalar_prefetch=2,
            grid=(n_blocks,),
            in_specs=[pl.BlockSpec((MOE_TM, D), row_map),
                      pl.BlockSpec((1, D, de), w_map),
                      pl.BlockSpec((1, D, de), w_map),
                      pl.BlockSpec((1, de, D), w_map)],
            out_specs=pl.BlockSpec((MOE_TM, D), lambda i, be, nu: (i, 0))),
        out_shape=jax.ShapeDtypeStruct((R, D), F32),
        compiler_params=_cparams(1),
        name="experts",
    )(blk_e, nused, buf, w_gate, w_up, w_down)


def _combine_kernel(npieces_ref, glob_ref, x2_ref, rti_ref, rtf_ref, gfin_ref, yb_ref, out_ref,
                    rows_ref, sem):
    tt = x2_ref.shape[0]
    sl = rows_ref.shape[1]
    i = pl.program_id(0)
    n_tiles = pl.num_programs(0)
    cur = lax.rem(i, 2)

    def piece_copy(buf_slot, local_row, global_row):
        return pltpu.make_async_copy(yb_ref.at[pl.ds(global_row, ROW_PIECE)],
                                     rows_ref.at[buf_slot, pl.ds(local_row, ROW_PIECE)], sem.at[buf_slot])

    def start_pieces(tile, buf_slot):
        _for_each_piece(npieces_ref, glob_ref, tile, lambda l, g: piece_copy(buf_slot, l, g).start())

    def wait_pieces(tile, buf_slot):
        _for_each_piece(npieces_ref, glob_ref, tile, lambda l, g: piece_copy(buf_slot, l, g).wait())

    @pl.when(i == 0)
    def _():
        rows_ref[...] = jnp.zeros_like(rows_ref)
        start_pieces(0, 0)

    @pl.when(i + 1 < n_tiles)
    def _():
        start_pieces(i + 1, 1 - cur)

    wait_pieces(i, cur)

    rti = rti_ref[...]
    rtf = rtf_ref[...]
    slot1 = rti[:, 2:3]
    slot2 = rti[:, 3:4]
    lane = lax.broadcasted_iota(jnp.int32, (tt, sl), 1)
    g = jnp.where(lane == slot1, rtf[:, 0:1], jnp.where(lane == slot2, rtf[:, 1:2], 0.0)).astype(BF16)
    y = x2_ref[...] + _dot(g, rows_ref[cur].astype(BF16))
    ms = jnp.mean(y * y, axis=-1, keepdims=True)
    out_ref[...] = y * lax.rsqrt(ms + EPS) * gfin_ref[...]


def _combine(x2, rti, rtf, g_final, yb, npieces, piece_glob):
    T, D = x2.shape
    tt = MIX_TS
    tok = lambda i, *_: (i, 0)
    return pl.pallas_call(
        _combine_kernel,
        grid_spec=pltpu.PrefetchScalarGridSpec(
            num_scalar_prefetch=2,
            grid=(T // tt,),
            in_specs=[pl.BlockSpec((tt, D), tok),
                      pl.BlockSpec((tt, LANES), tok),
                      pl.BlockSpec((tt, LANES), tok),
                      pl.BlockSpec((1, D), lambda i, *_: (0, 0)),
                      pl.BlockSpec(memory_space=pl.ANY)],
            out_specs=pl.BlockSpec((tt, D), tok),
            scratch_shapes=[pltpu.VMEM((2, MOE_SL, D), F32),
                            pltpu.SemaphoreType.DMA((2,))]),
        out_shape=jax.ShapeDtypeStruct((T, D), F32),
        compiler_params=_cparams(1),
        name="combine",
    )(npieces, piece_glob, x2, rti, rtf, g_final, yb)


def _pad_lanes(a, width=LANES):
    return jnp.pad(a, ((0, 0), (0, width - a.shape[1])))


def kernel(x, g_mix, w_in, b_if, conv_q, conv_k, g_head, w_pool, pool_scale, w_br_a, w_br_b, w_out,
           g_ffn, w_rg, b_rg, w_re, b_re, w_e_gate, w_e_up, w_e_down, g_final):
    B, S, D = x.shape
    T = B * S
    assert g_mix.shape[0] == 1, "single-layer block"
    assert S % MIX_TS == 0 and T % INPROJ_TM == 0
    d_pool = w_br_a.shape[1]
    d_ml = w_br_b.shape[1]
    x2d = x.reshape(T, D)

    n_main = d_pool + 4 * d_ml
    w_l = w_in[0]
    w_main = w_l[:, :n_main].astype(BF16)
    w_if = w_l[:, n_main:n_main + 2 * N_HEADS]
    w_gates = w_l[:, n_main + 2 * N_HEADS:].astype(BF16)
    w_if_c = _pad_lanes(w_if).astype(BF16)
    w_if_t = jnp.pad(w_if.T, ((0, 16 - 2 * N_HEADS), (0, 0))).astype(BF16)
    w_r = _pad_lanes(jnp.concatenate([w_rg[0], w_re[0]], axis=1))
    w_r_hi = w_r.astype(BF16)
    w_r_lo = (w_r - w_r_hi.astype(F32)).astype(BF16)
    params = {
        "b_if": _pad_lanes(b_if[0][None, :]),
        "b_if_t": jnp.pad(b_if[0][:, None], ((0, 16 - 2 * N_HEADS), (0, 0))),
        "conv_q": conv_q[0], "conv_k": conv_k[0],
        "g_head": g_head[0][None, :],
        "w_pool": w_pool[0].astype(BF16),
        "pool_scale": pool_scale[0][None, :],
        "w_br_a": w_br_a[0].astype(BF16), "w_br_b": w_br_b[0].astype(BF16),
        "w_out": w_out[0].astype(BF16),
        "g_ffn": g_ffn[0][None, :],
        "w_r_hi": w_r_hi, "w_r_lo": w_r_lo,
        "b_r": _pad_lanes(jnp.concatenate([b_rg[0], b_re[0]])[None, :]),
    }

    zm, zg, zif, zift = _inproj(x2d, g_mix[0][None, :], w_main, w_gates, w_if_c, w_if_t)
    x2, xn2, rti, rtf, tstat, srow = _mixer(x2d, zm, zg, zif, zift, params, B, S)

    n_tiles = T // MIX_TS
    pcs = tstat.reshape(n_tiles, 8, LANES)[:, 0, ROUTER_LANE0:ROUTER_LANE0 + N_EXPERTS]
    piece_loc = jnp.cumsum(pcs, axis=1) - pcs
    rows_e = jnp.sum(pcs, axis=0) * ROW_PIECE
    padded = (rows_e + MOE_TM - 1) // MOE_TM * MOE_TM
    pend = jnp.cumsum(padded)
    poff = pend - padded
    piece_glob = poff[None, :] // ROW_PIECE + jnp.cumsum(pcs, axis=0) - pcs
    n_rows = n_tiles * MOE_SL + N_EXPERTS * MOE_TM
    n_blocks = n_rows // MOE_TM
    nused = (pend[-1:] // MOE_TM).astype(jnp.int32)
    blk_ids = jnp.arange(n_blocks, dtype=jnp.int32)
    blk_e = jnp.sum((pend[None, :] <= blk_ids[:, None] * MOE_TM).astype(jnp.int32), axis=1)
    blk_e = jnp.minimum(blk_e, N_EXPERTS - 1)
    zflag = (((blk_ids + 1) * MOE_TM == pend[blk_e]) | (blk_ids >= nused[0])).astype(jnp.int32)
    p_ids = jnp.arange(PIECES_PER_TILE, dtype=jnp.int32)
    piece_end = piece_loc + pcs
    e_of_p = jnp.minimum(jnp.sum((piece_end[:, None, :] <= p_ids[None, :, None]).astype(jnp.int32), axis=2),
                         N_EXPERTS - 1)
    glob_of_p = (jnp.take_along_axis(piece_glob - piece_loc, e_of_p, axis=1) + p_ids[None, :]).astype(jnp.int32)
    glob_of_p = glob_of_p.reshape(n_tiles * PIECES_PER_TILE)
    npieces = jnp.sum(pcs, axis=1).astype(jnp.int32)

    buf = _dispatch(xn2, srow, npieces, glob_of_p, zflag, n_rows)
    yb = _experts(buf, blk_e, nused, w_e_gate[0].astype(BF16), w_e_up[0].astype(BF16), w_e_down[0].astype(BF16))
    out = _combine(x2, rti, rtf, g_final[None, :], yb, npieces, glob_of_p)
    return out.reshape(B, S, D)
```

```python
import functools

import numpy as np
import jax
import jax.numpy as jnp
from jax import lax
from jax.experimental import pallas as pl
from jax.experimental.pallas import tpu as pltpu

F32 = jnp.float32
BF16 = jnp.bfloat16

CHUNK = 64
POOL_WINDOWS = (2, 4, 8, 16)
N_HEADS = 4
CONV_K = 4
N_GROUPS = 4
EXPERTS_PER_GROUP = 8
N_EXPERTS = N_GROUPS * EXPERTS_PER_GROUP
TOP_K = 2
EPS = 1e-6

LANES = 128
HALO = 16
ROUTER_LANE0 = N_GROUPS

INPROJ_TM = 512
MIX_TS = 256
MOE_TM = 512
ROW_PIECE = 8
MOE_SL = TOP_K * MIX_TS + N_EXPERTS * ROW_PIECE
PIECES_PER_TILE = MOE_SL // ROW_PIECE
SLOT_RADIX = 16
VMEM_LIMIT = 56 * 1024 * 1024


def _cparams(n_axes):
    return pltpu.CompilerParams(dimension_semantics=("arbitrary",) * n_axes,
                                vmem_limit_bytes=VMEM_LIMIT)


def _sigmoid(v):
    return 0.5 * jnp.tanh(0.5 * v) + 0.5


def _silu(v):
    return v * _sigmoid(v)


def _log_sigmoid(v):
    return jnp.minimum(v, 0.0) - jnp.log1p(jnp.exp(-jnp.abs(v)))


def _split3(v):
    hi = v.astype(BF16)
    r1 = v - hi.astype(F32)
    mid = r1.astype(BF16)
    lo = (r1 - mid.astype(F32)).astype(BF16)
    return hi, mid, lo


def _dot(a, b):
    return jnp.dot(a, b, preferred_element_type=F32)


def _dot_nt(a, b):
    return lax.dot_general(a, b, (((1,), (1,)), ((), ())), preferred_element_type=F32)


def _dot_tn(a, b):
    return lax.dot_general(a, b, (((0,), (0,)), ((), ())), preferred_element_type=F32)


def _inproj_kernel(x_ref, g_ref, wm_ref, wg_ref, wif_ref, wift_ref,
                   zm_ref, zg_ref, zif_ref, zift_ref):
    x = x_ref[...]
    ms = jnp.mean(x * x, axis=-1, keepdims=True)
    xn = (x * lax.rsqrt(ms + EPS) * g_ref[...]).astype(BF16)
    zm_ref[...] = _dot(xn, wm_ref[...]).astype(BF16)
    zg_ref[...] = _dot(xn, wg_ref[...]).astype(BF16)
    zif_ref[...] = _dot(xn, wif_ref[...])
    zift_ref[...] = _dot_nt(wift_ref[...], xn)


def _inproj(x2d, g_mix, w_main, w_gate, w_if, w_if_t):
    T, D = x2d.shape
    tm = min(INPROJ_TM, T)
    nm, ng = w_main.shape[1], w_gate.shape[1]
    const = lambda i: (0, 0)
    return pl.pallas_call(
        _inproj_kernel,
        grid=(T // tm,),
        in_specs=[pl.BlockSpec((tm, D), lambda i: (i, 0)),
                  pl.BlockSpec((1, D), const),
                  pl.BlockSpec((D, nm), const),
                  pl.BlockSpec((D, ng), const),
                  pl.BlockSpec((D, LANES), const),
                  pl.BlockSpec((16, D), const)],
        out_specs=[pl.BlockSpec((tm, nm), lambda i: (i, 0)),
                   pl.BlockSpec((tm, ng), lambda i: (i, 0)),
                   pl.BlockSpec((tm, LANES), lambda i: (i, 0)),
                   pl.BlockSpec((16, tm), lambda i: (0, i))],
        out_shape=[jax.ShapeDtypeStruct((T, nm), BF16),
                   jax.ShapeDtypeStruct((T, ng), BF16),
                   jax.ShapeDtypeStruct((T, LANES), F32),
                   jax.ShapeDtypeStruct((16, T), F32)],
        compiler_params=_cparams(1),
        name="inproj",
    )(x2d, g_mix, w_main, w_gate, w_if, w_if_t)


def _mixer_kernel(x_ref, zm_ref, zg_ref, zif_ref, zift_ref,
                  bif_ref, bift_ref, convq_ref, convk_ref, ghead_ref, wpool_ref, pscale_ref,
                  wa_ref, wb_ref, wo_ref, gffn_ref, wrh_ref, wrl_ref, br_ref,
                  tric_ref, trir_ref, stri_ref, ut_ref, sel_ref,
                  x2_ref, xn2_ref, rti_ref, rtf_ref, tstat_ref, srow_ref,
                  halo_ref, q_ref, k_ref, h_ref, pool_ref, cst_ref, mst_ref):
    ts = x_ref.shape[0]
    d_pool = wa_ref.shape[0]
    d_ml = wb_ref.shape[0]
    dh = d_ml // N_HEADS
    n_chunks = ts // CHUNK
    j = pl.program_id(1)

    @pl.when(j == 0)
    def _():
        halo_ref[...] = jnp.zeros_like(halo_ref)
        cst_ref[...] = jnp.zeros_like(cst_ref)
        mst_ref[...] = jnp.zeros_like(mst_ref)

    row = lax.broadcasted_iota(jnp.int32, (ts, LANES), 0)
    pos1 = (row + j * ts + 1).astype(F32)

    def extended(cg):
        cols = slice(cg * LANES, (cg + 1) * LANES)
        cur = zm_ref[:, cols].astype(F32)
        ext = jnp.concatenate([halo_ref[:, cols], cur], axis=0)
        halo_ref[:, cols] = cur[ts - HALO:, :]
        return ext

    n_pool_groups = d_pool // LANES
    for g in range(n_pool_groups):
        w = POOL_WINDOWS[g]
        ext = extended(g)
        s, span = ext, 1
        while span < w:
            s = s + pltpu.roll(s, span, axis=0)
            span *= 2
        win = s[HALO:, :]
        cnt = jnp.minimum(pos1, float(w))
        d = win / cnt - ext[HALO:, :]
        y = _dot(d.astype(BF16), wpool_ref[g]) * pscale_ref[:, g * LANES:(g + 1) * LANES]
        pool_ref[:, g * LANES:(g + 1) * LANES] = y.astype(BF16)

    n_ml_groups = d_ml // LANES
    for which, (cw_ref, dst_ref, scale) in enumerate(((convq_ref, q_ref, 1.0), (convk_ref, k_ref, dh ** -0.5))):
        for g in range(n_ml_groups):
            cols = slice(g * LANES, (g + 1) * LANES)
            ext = extended(n_pool_groups + which * n_ml_groups + g)
            acc = ext * cw_ref[CONV_K - 1:CONV_K, cols]
            for sft in range(1, CONV_K):
                acc = acc + pltpu.roll(ext, sft, axis=0) * cw_ref[CONV_K - 1 - sft:CONV_K - sft, cols]
            dst_ref[:, cols] = (_silu(acc[HALO:, :]) * scale).astype(BF16)

    zc = zif_ref[...] + bif_ref[...]
    lf_c = _log_sigmoid(zc)
    bc = sum(_dot(tric_ref[...], p) for p in _split3(lf_c))
    zr = zift_ref[...] + bift_ref[...]
    lf_r = _log_sigmoid(zr)
    br = sum(_dot(p, trir_ref[...]) for p in _split3(lf_r))

    ti = lax.broadcasted_iota(jnp.int32, (CHUNK, CHUNK), 0)
    si = lax.broadcasted_iota(jnp.int32, (CHUNK, CHUNK), 1)
    causal = si <= ti
    ones_blk = jnp.ones((CHUNK, dh), BF16)
    v0 = d_pool + 2 * d_ml
    ig_rep = [jnp.broadcast_to(zc[:, h:h + 1], (ts, dh)) for h in range(N_HEADS)]
    bt_rep = [jnp.broadcast_to(bc[:, N_HEADS + h:N_HEADS + h + 1], (ts, dh)) for h in range(N_HEADS)]

    m_state = [mst_ref[h:h + 1, :] for h in range(N_HEADS)]
    c_state = [cst_ref[h] for h in range(N_HEADS)]
    for c in range(n_chunks):
        rs = slice(c * CHUNK, (c + 1) * CHUNK)
        for h in range(N_HEADS):
            hs = slice(h * dh, (h + 1) * dh)
            q = q_ref[rs, hs]
            k = k_ref[rs, hs]
            v_aug = jnp.concatenate([zm_ref[rs, v0 + h * dh:v0 + (h + 1) * dh], ones_blk], axis=-1)
            bt = bt_rep[h][rs, :]
            igc = ig_rep[h][rs, :]
            r_row = zr[h:h + 1, rs] - br[N_HEADS + h:N_HEADS + h + 1, rs]
            m_prev = m_state[h]
            c_prev = c_state[h]

            dmat = jnp.where(causal, bt[:, :CHUNK] + r_row, -jnp.inf)
            m_intra = jnp.max(dmat, axis=-1, keepdims=True)
            inter = bt + m_prev
            m_t = jnp.maximum(inter, m_intra)
            w_inter = jnp.exp(inter - m_t)
            smat = _dot_nt(q, k) * jnp.exp(dmat - m_t[:, :CHUNK])
            qc = _dot(q, c_prev.astype(BF16))
            sv = _dot(smat.astype(BF16), v_aug)
            nq = w_inter * qc[:, dh:] + sv[:, dh:]
            den = jnp.maximum(jnp.abs(nq), jnp.exp(-m_t))
            h_ref[rs, hs] = (w_inter * qc[:, :dh] + sv[:, :dh]) / den

            b_last = bt[CHUNK - 1:CHUNK, :]
            a_log = b_last - bt + igc
            a_max = jnp.max(a_log, axis=0, keepdims=True)
            m_new = jnp.maximum(b_last + m_prev, a_max)
            kw = (k.astype(F32) * jnp.exp(a_log - m_new)).astype(BF16)
            decay = jnp.exp(b_last + m_prev - m_new)
            c_state[h] = jnp.concatenate([decay, decay], axis=-1) * c_prev + _dot_tn(kw, v_aug)
            m_state[h] = m_new
    for h in range(N_HEADS):
        cst_ref[h] = c_state[h]
        mst_ref[h:h + 1, :] = m_state[h]

    o0 = v0 + d_ml
    for h in range(N_HEADS):
        hs = slice(h * dh, (h + 1) * dh)
        hv = h_ref[:, hs]
        mu = jnp.mean(hv, axis=-1, keepdims=True)
        hc = hv - mu
        var = jnp.mean(hc * hc, axis=-1, keepdims=True)
        hn = hc * lax.rsqrt(var + EPS) * ghead_ref[:, hs]
        og = _sigmoid(zm_ref[:, o0 + h * dh:o0 + (h + 1) * dh].astype(F32))
        q_ref[:, hs] = (og * hn).astype(BF16)
    y_a = _dot(pool_ref[...], wa_ref[...])
    y_b = _dot(q_ref[...], wb_ref[...])
    d_model = x_ref.shape[1]
    ga = _sigmoid(zg_ref[:, :d_model].astype(F32))
    gb = _sigmoid(zg_ref[:, d_model:].astype(F32))
    merged = (ga * y_a + gb * y_b).astype(BF16)
    x2 = x_ref[...] + _dot(merged, wo_ref[...])
    x2_ref[...] = x2

    ms = jnp.mean(x2 * x2, axis=-1, keepdims=True)
    xn2 = x2 * lax.rsqrt(ms + EPS) * gffn_ref[...]
    xh = xn2.astype(BF16)
    xn2_ref[...] = xh
    xl = (xn2 - xh.astype(F32)).astype(BF16)
    lg = _dot(xh, wrh_ref[...]) + _dot(xl, wrh_ref[...]) + _dot(xh, wrl_ref[...]) + br_ref[...]

    lane = lax.broadcasted_iota(jnp.int32, (ts, LANES), 1)
    lanef = lane.astype(F32)
    big = float(4 * LANES)
    gl = jnp.where(lane < N_GROUPS, lg, -jnp.inf)
    gmax = jnp.max(gl, axis=-1, keepdims=True)
    g_sel = jnp.min(jnp.where(gl == gmax, lanef, big), axis=-1, keepdims=True)
    p_g = 1.0 / jnp.sum(jnp.exp(gl - gmax), axis=-1, keepdims=True)
    lo = ROUTER_LANE0 + EXPERTS_PER_GROUP * g_sel
    el = jnp.where((lanef >= lo) & (lanef < lo + EXPERTS_PER_GROUP), lg, -jnp.inf)
    m1 = jnp.max(el, axis=-1, keepdims=True)
    i1 = jnp.min(jnp.where(el == m1, lanef, big), axis=-1, keepdims=True)
    el2 = jnp.where(lanef == i1, -jnp.inf, el)
    m2 = jnp.max(el2, axis=-1, keepdims=True)
    i2 = jnp.min(jnp.where(el2 == m2, lanef, big), axis=-1, keepdims=True)
    e2x = jnp.exp(m2 - m1)
    gate1 = p_g / (1.0 + e2x)
    gate2 = p_g * e2x / (1.0 + e2x)

    oh1 = lanef == i1
    oh2 = lanef == i2
    ohs = jnp.where(oh1 | oh2, 1.0, 0.0)
    n_loc = jnp.sum(ohs, axis=0, keepdims=True)
    pieces = jnp.floor((n_loc + (ROW_PIECE - 1.0)) * (1.0 / ROW_PIECE))
    piece_off = _dot(jnp.broadcast_to(pieces, (8, LANES)).astype(BF16), ut_ref[...])[0:1, :]
    base = _dot(stri_ref[...], ohs.astype(BF16)) + ROW_PIECE * piece_off
    slot1 = jnp.sum(jnp.where(oh1, base, 0.0), axis=-1, keepdims=True)
    slot2 = jnp.sum(jnp.where(oh2, base, 0.0), axis=-1, keepdims=True)
    tstat_ref[...] = jnp.broadcast_to(pieces, tstat_ref.shape).astype(jnp.int32)

    rti = jnp.where(lane == 0, i1 - ROUTER_LANE0,
                    jnp.where(lane == 1, i2 - ROUTER_LANE0,
                              jnp.where(lane == 2, slot1, jnp.where(lane == 3, slot2, 0.0))))
    rti_ref[...] = rti.astype(jnp.int32)
    rtf_ref[...] = jnp.where(lane == 0, gate1, jnp.where(lane == 1, gate2, 0.0))
    h1 = jnp.floor(slot1 * (1.0 / SLOT_RADIX))
    h2 = jnp.floor(slot2 * (1.0 / SLOT_RADIX))
    parts = jnp.where(lane == 0, h1, jnp.where(lane == 1, slot1 - SLOT_RADIX * h1,
                      jnp.where(lane == 2, h2, jnp.where(lane == 3, slot2 - SLOT_RADIX * h2, 0.0))))
    srow_ref[...] = _dot_nt(sel_ref[...], parts.astype(BF16))


def _mixer(x2d, zm, zg, zif, zift, params, batch, seq):
    T, D = x2d.shape
    ts = min(MIX_TS, seq)
    nts = seq // ts
    d_pool = params["w_br_a"].shape[0]
    d_ml = params["w_br_b"].shape[0]
    dh = d_ml // N_HEADS

    idx = np.arange(ts)
    same_chunk = (idx[:, None] // CHUNK) == (idx[None, :] // CHUNK)
    tri_c = jnp.asarray((idx[None, :] <= idx[:, None]) & same_chunk, BF16)
    tri_r = jnp.asarray((idx[:, None] <= idx[None, :]) & same_chunk, BF16)
    stri = jnp.asarray(idx[None, :] < idx[:, None], BF16)
    lane_idx = np.arange(LANES)
    ut = jnp.asarray(lane_idx[:, None] < lane_idx[None, :], BF16)
    sel = jnp.asarray(np.arange(8)[:, None] == lane_idx[None, :], BF16)

    tok = lambda b, j: (b * nts + j, 0)
    tok_t = lambda b, j: (0, b * nts + j)
    c2 = lambda b, j: (0, 0)
    c3 = lambda b, j: (0, 0, 0)
    full = lambda a: pl.BlockSpec(a.shape, c2 if a.ndim == 2 else c3)
    consts = [params[n] for n in ("b_if", "b_if_t", "conv_q", "conv_k", "g_head", "w_pool", "pool_scale",
                                  "w_br_a", "w_br_b", "w_out", "g_ffn", "w_r_hi", "w_r_lo", "b_r")]
    consts += [tri_c, tri_r, stri, ut, sel]
    n_tiles = batch * nts
    return pl.pallas_call(
        _mixer_kernel,
        grid=(batch, nts),
        in_specs=[pl.BlockSpec((ts, D), tok),
                  pl.BlockSpec((ts, zm.shape[1]), tok),
                  pl.BlockSpec((ts, zg.shape[1]), tok),
                  pl.BlockSpec((ts, LANES), tok),
                  pl.BlockSpec((16, ts), tok_t)] + [full(a) for a in consts],
        out_specs=[pl.BlockSpec((ts, D), tok),
                   pl.BlockSpec((ts, D), tok),
                   pl.BlockSpec((ts, LANES), tok),
                   pl.BlockSpec((ts, LANES), tok),
                   pl.BlockSpec((8, LANES), tok),
                   pl.BlockSpec((8, ts), tok_t)],
        out_shape=[jax.ShapeDtypeStruct((T, D), F32),
                   jax.ShapeDtypeStruct((T, D), BF16),
                   jax.ShapeDtypeStruct((T, LANES), jnp.int32),
                   jax.ShapeDtypeStruct((T, LANES), F32),
                   jax.ShapeDtypeStruct((n_tiles * 8, LANES), jnp.int32),
                   jax.ShapeDtypeStruct((8, T), F32)],
        scratch_shapes=[pltpu.VMEM((HALO, d_pool + 2 * d_ml), F32),
                        pltpu.VMEM((ts, d_ml), BF16),
                        pltpu.VMEM((ts, d_ml), BF16),
                        pltpu.VMEM((ts, d_ml), F32),
                        pltpu.VMEM((ts, d_pool), BF16),
                        pltpu.VMEM((N_HEADS, dh, 2 * dh), F32),
                        pltpu.VMEM((8, LANES), F32)],
        compiler_params=_cparams(2),
        name="mixer",
    )(x2d, zm, zg, zif, zift, *consts)


def _for_each_piece(npieces_ref, glob_ref, tile, fn):
    base = tile * PIECES_PER_TILE

    def per_piece(p, carry):
        fn(pl.multiple_of(p * ROW_PIECE, ROW_PIECE), pl.multiple_of(glob_ref[base + p] * ROW_PIECE, ROW_PIECE))
        return carry

    lax.fori_loop(0, npieces_ref[tile], per_piece, 0)


def _dispatch_kernel(npieces_ref, glob_ref, zflag_ref,
                     xn_ref, srow_ref, buf_ref, rows_ref, zeros_ref, sem, zsem):
    tt = xn_ref.shape[0]
    sl = rows_ref.shape[1]
    n_blocks = buf_ref.shape[0] // MOE_TM
    i = pl.program_id(0)
    n_tiles = pl.num_programs(0)
    cur = lax.rem(i, 2)

    @pl.when(i == 0)
    def _():
        zeros_ref[...] = jnp.zeros_like(zeros_ref)

        def zero_copy(b):
            return pltpu.make_async_copy(zeros_ref, buf_ref.at[pl.ds(b * MOE_TM, MOE_TM)], zsem)

        def zero_start(b, carry):
            @pl.when(zflag_ref[b] > 0)
            def _():
                zero_copy(b).start()
            return carry

        def zero_wait(b, carry):
            @pl.when(zflag_ref[b] > 0)
            def _():
                zero_copy(b).wait()
            return carry

        lax.fori_loop(0, n_blocks, zero_start, 0)
        lax.fori_loop(0, n_blocks, zero_wait, 0)

    def piece_copy(buf_slot, local_row, global_row):
        return pltpu.make_async_copy(rows_ref.at[buf_slot, pl.ds(local_row, ROW_PIECE)],
                                     buf_ref.at[pl.ds(global_row, ROW_PIECE)], sem.at[buf_slot])

    def start_pieces(tile, buf_slot):
        _for_each_piece(npieces_ref, glob_ref, tile, lambda l, g: piece_copy(buf_slot, l, g).start())

    def wait_pieces(tile, buf_slot):
        _for_each_piece(npieces_ref, glob_ref, tile, lambda l, g: piece_copy(buf_slot, l, g).wait())

    @pl.when(i >= 2)
    def _():
        wait_pieces(i - 2, cur)

    sr = srow_ref[...]
    slot1 = SLOT_RADIX * sr[0:1, :] + sr[1:2, :]
    slot2 = SLOT_RADIX * sr[2:3, :] + sr[3:4, :]
    r = lax.broadcasted_iota(jnp.int32, (sl, tt), 0).astype(F32)
    sel = jnp.where((r == slot1) | (r == slot2), 1.0, 0.0).astype(BF16)
    rows_ref[cur] = _dot(sel, xn_ref[...])
    start_pieces(i, cur)

    @pl.when(i == n_tiles - 1)
    def _():
        @pl.when(i >= 1)
        def _():
            wait_pieces(i - 1, 1 - cur)
        wait_pieces(i, cur)


def _dispatch(xn2, srow, npieces, piece_glob, zflag, n_rows):
    T, D = xn2.shape
    tt = MIX_TS
    return pl.pallas_call(
        _dispatch_kernel,
        grid_spec=pltpu.PrefetchScalarGridSpec(
            num_scalar_prefetch=3,
            grid=(T // tt,),
            in_specs=[pl.BlockSpec((tt, D), lambda i, *_: (i, 0)),
                      pl.BlockSpec((8, tt), lambda i, *_: (0, i))],
            out_specs=pl.BlockSpec(memory_space=pl.ANY),
            scratch_shapes=[pltpu.VMEM((2, MOE_SL, D), F32),
                            pltpu.VMEM((MOE_TM, D), F32),
                            pltpu.SemaphoreType.DMA((2,)),
                            pltpu.SemaphoreType.DMA(())]),
        out_shape=jax.ShapeDtypeStruct((n_rows, D), F32),
        compiler_params=_cparams(1),
        name="dispatch",
    )(npieces, piece_glob, zflag, xn2, srow)


def _experts_kernel(blk_e_ref, nused_ref, x_ref, wg_ref, wu_ref, wd_ref, y_ref, wgb_ref, wub_ref, wdb_ref):
    i = pl.program_id(0)
    used = i < nused_ref[0]

    @pl.when(used & ((i == 0) | (blk_e_ref[i] != blk_e_ref[jnp.maximum(i - 1, 0)])))
    def _():
        wgb_ref[...] = wg_ref[0].astype(BF16)
        wub_ref[...] = wu_ref[0].astype(BF16)
        wdb_ref[...] = wd_ref[0].astype(BF16)

    @pl.when(used)
    def _():
        x = x_ref[...].astype(BF16)
        hg = _dot(x, wgb_ref[...])
        hu = _dot(x, wub_ref[...])
        hid = (_silu(hg) * hu).astype(BF16)
        y_ref[...] = _dot(hid, wdb_ref[...])

    @pl.when(jnp.logical_not(used))
    def _():
        y_ref[...] = jnp.zeros_like(y_ref)


def _experts(buf, blk_e, nused, w_gate, w_up, w_down):
    R, D = buf.shape
    de = w_gate.shape[2]
    n_blocks = R // MOE_TM

    def blk(i, blk_e_ref, nused_ref):
        return jnp.minimum(i, nused_ref[0] - 1)

    row_map = lambda i, be, nu: (blk(i, be, nu), 0)
    w_map = lambda i, be, nu: (be[blk(i, be, nu)], 0, 0)
    return pl.pallas_call(
        _experts_kernel,
        grid_spec=pltpu.PrefetchScalarGridSpec(
            num_scalar_prefetch=2,
            grid=(n_blocks,),
            in_specs=[pl.BlockSpec((MOE_TM, D), row_map),
                      pl.BlockSpec((1, D, de), w_map),
                      pl.BlockSpec((1, D, de), w_map),
                      pl.BlockSpec((1, de, D), w_map)],
            out_specs=pl.BlockSpec((MOE_TM, D), lambda i, be, nu: (i, 0)),
            scratch_shapes=[pltpu.VMEM((D, de), BF16), pltpu.VMEM((D, de), BF16), pltpu.VMEM((de, D), BF16)]),
        out_shape=jax.ShapeDtypeStruct((R, D), F32),
        compiler_params=_cparams(1),
        name="experts",
    )(blk_e, nused, buf, w_gate, w_up, w_down)


def _combine_kernel(npieces_ref, glob_ref, x2_ref, rti_ref, rtf_ref, gfin_ref, yb_ref, out_ref,
                    rows_ref, sem):
    tt = x2_ref.shape[0]
    sl = rows_ref.shape[1]
    i = pl.program_id(0)
    n_tiles = pl.num_programs(0)
    cur = lax.rem(i, 2)

    def piece_copy(buf_slot, local_row, global_row):
        return pltpu.make_async_copy(yb_ref.at[pl.ds(global_row, ROW_PIECE)],
                                     rows_ref.at[buf_slot, pl.ds(local_row, ROW_PIECE)], sem.at[buf_slot])

    def start_pieces(tile, buf_slot):
        _for_each_piece(npieces_ref, glob_ref, tile, lambda l, g: piece_copy(buf_slot, l, g).start())

    def wait_pieces(tile, buf_slot):
        _for_each_piece(npieces_ref, glob_ref, tile, lambda l, g: piece_copy(buf_slot, l, g).wait())

    @pl.when(i == 0)
    def _():
        rows_ref[...] = jnp.zeros_like(rows_ref)
        start_pieces(0, 0)

    @pl.when(i + 1 < n_tiles)
    def _():
        start_pieces(i + 1, 1 - cur)

    wait_pieces(i, cur)

    rti = rti_ref[...]
    rtf = rtf_ref[...]
    slot1 = rti[:, 2:3]
    slot2 = rti[:, 3:4]
    lane = lax.broadcasted_iota(jnp.int32, (tt, sl), 1)
    g = jnp.where(lane == slot1, rtf[:, 0:1], jnp.where(lane == slot2, rtf[:, 1:2], 0.0)).astype(BF16)
    y = x2_ref[...] + _dot(g, rows_ref[cur].astype(BF16))
    ms = jnp.mean(y * y, axis=-1, keepdims=True)
    out_ref[...] = y * lax.rsqrt(ms + EPS) * gfin_ref[...]


def _combine(x2, rti, rtf, g_final, yb, npieces, piece_glob):
    T, D = x2.shape
    tt = MIX_TS
    tok = lambda i, *_: (i, 0)
    return pl.pallas_call(
        _combine_kernel,
        grid_spec=pltpu.PrefetchScalarGridSpec(
            num_scalar_prefetch=2,
            grid=(T // tt,),
            in_specs=[pl.BlockSpec((tt, D), tok),
                      pl.BlockSpec((tt, LANES), tok),
                      pl.BlockSpec((tt, LANES), tok),
                      pl.BlockSpec((1, D), lambda i, *_: (0, 0)),
                      pl.BlockSpec(memory_space=pl.ANY)],
            out_specs=pl.BlockSpec((tt, D), tok),
            scratch_shapes=[pltpu.VMEM((2, MOE_SL, D), F32),
                            pltpu.SemaphoreType.DMA((2,))]),
        out_shape=jax.ShapeDtypeStruct((T, D), F32),
        compiler_params=_cparams(1),
        name="combine",
    )(npieces, piece_glob, x2, rti, rtf, g_final, yb)


def _pad_lanes(a, width=LANES):
    return jnp.pad(a, ((0, 0), (0, width - a.shape[1])))


def kernel(x, g_mix, w_in, b_if, conv_q, conv_k, g_head, w_pool, pool_scale, w_br_a, w_br_b, w_out,
           g_ffn, w_rg, b_rg, w_re, b_re, w_e_gate, w_e_up, w_e_down, g_final):
    B, S, D = x.shape
    T = B * S
    assert g_mix.shape[0] == 1, "single-layer block"
    assert S % MIX_TS == 0 and T % INPROJ_TM == 0
    d_pool = w_br_a.shape[1]
    d_ml = w_br_b.shape[1]
    x2d = x.reshape(T, D)

    n_main = d_pool + 4 * d_ml
    w_l = w_in[0]
    w_main = w_l[:, :n_main].astype(BF16)
    w_if = w_l[:, n_main:n_main + 2 * N_HEADS]
    w_gates = w_l[:, n_main + 2 * N_HEADS:].astype(BF16)
    w_if_c = _pad_lanes(w_if).astype(BF16)
    w_if_t = jnp.pad(w_if.T, ((0, 16 - 2 * N_HEADS), (0, 0))).astype(BF16)
    w_r = _pad_lanes(jnp.concatenate([w_rg[0], w_re[0]], axis=1))
    w_r_hi = w_r.astype(BF16)
    w_r_lo = (w_r - w_r_hi.astype(F32)).astype(BF16)
    params = {
        "b_if": _pad_lanes(b_if[0][None, :]),
        "b_if_t": jnp.pad(b_if[0][:, None], ((0, 16 - 2 * N_HEADS), (0, 0))),
        "conv_q": conv_q[0], "conv_k": conv_k[0],
        "g_head": g_head[0][None, :],
        "w_pool": w_pool[0].astype(BF16),
        "pool_scale": pool_scale[0][None, :],
        "w_br_a": w_br_a[0].astype(BF16), "w_br_b": w_br_b[0].astype(BF16),
        "w_out": w_out[0].astype(BF16),
        "g_ffn": g_ffn[0][None, :],
        "w_r_hi": w_r_hi, "w_r_lo": w_r_lo,
        "b_r": _pad_lanes(jnp.concatenate([b_rg[0], b_re[0]])[None, :]),
    }

    zm, zg, zif, zift = _inproj(x2d, g_mix[0][None, :], w_main, w_gates, w_if_c, w_if_t)
    x2, xn2, rti, rtf, tstat, srow = _mixer(x2d, zm, zg, zif, zift, params, B, S)

    n_tiles = T // MIX_TS
    pcs = tstat.reshape(n_tiles, 8, LANES)[:, 0, ROUTER_LANE0:ROUTER_LANE0 + N_EXPERTS]
    piece_loc = jnp.cumsum(pcs, axis=1) - pcs
    rows_e = jnp.sum(pcs, axis=0) * ROW_PIECE
    padded = (rows_e + MOE_TM - 1) // MOE_TM * MOE_TM
    pend = jnp.cumsum(padded)
    poff = pend - padded
    piece_glob = poff[None, :] // ROW_PIECE + jnp.cumsum(pcs, axis=0) - pcs
    n_rows = n_tiles * MOE_SL + N_EXPERTS * MOE_TM
    n_blocks = n_rows // MOE_TM
    nused = (pend[-1:] // MOE_TM).astype(jnp.int32)
    blk_ids = jnp.arange(n_blocks, dtype=jnp.int32)
    blk_e = jnp.sum((pend[None, :] <= blk_ids[:, None] * MOE_TM).astype(jnp.int32), axis=1)
    blk_e = jnp.minimum(blk_e, N_EXPERTS - 1)
    zflag = (((blk_ids + 1) * MOE_TM == pend[blk_e]) | (blk_ids >= nused[0])).astype(jnp.int32)
    p_ids = jnp.arange(PIECES_PER_TILE, dtype=jnp.int32)
    piece_end = piece_loc + pcs
    e_of_p = jnp.minimum(jnp.sum((piece_end[:, None, :] <= p_ids[None, :, None]).astype(jnp.int32), axis=2),
                         N_EXPERTS - 1)
    e_ids = jnp.arange(N_EXPERTS, dtype=jnp.int32)
    shift = jnp.sum(jnp.where(e_of_p[:, :, None] == e_ids[None, None, :], (piece_glob - piece_loc)[:, None, :], 0),
                    axis=2)
    glob_of_p = (shift + p_ids[None, :]).astype(jnp.int32)
    glob_of_p = glob_of_p.reshape(n_tiles * PIECES_PER_TILE)
    npieces = jnp.sum(pcs, axis=1).astype(jnp.int32)

    buf = _dispatch(xn2, srow, npieces, glob_of_p, zflag, n_rows)
    yb = _experts(buf, blk_e, nused, w_e_gate[0], w_e_up[0], w_e_down[0])
    out = _combine(x2, rti, rtf, g_final[None, :], yb, npieces, glob_of_p)
    return out.reshape(B, S, D)
```

```python
import functools

import numpy as np
import jax
import jax.numpy as jnp
from jax import lax
from jax.experimental import pallas as pl
from jax.experimental.pallas import tpu as pltpu

F32 = jnp.float32
BF16 = jnp.bfloat16

CHUNK = 64
POOL_WINDOWS = (2, 4, 8, 16)
N_HEADS = 4
CONV_K = 4
N_GROUPS = 4
EXPERTS_PER_GROUP = 8
N_EXPERTS = N_GROUPS * EXPERTS_PER_GROUP
TOP_K = 2
EPS = 1e-6

LANES = 128
HALO = 16
ROUTER_LANE0 = N_GROUPS

INPROJ_TN = 256
MIX_TS = 256
MOE_TM = 512
ROW_PIECE = 8
MOE_SL = TOP_K * MIX_TS + N_EXPERTS * ROW_PIECE
PIECES_PER_TILE = MOE_SL // ROW_PIECE
SLOT_RADIX = 16
VMEM_LIMIT = 56 * 1024 * 1024


def _cparams(n_axes):
    return pltpu.CompilerParams(dimension_semantics=("arbitrary",) * n_axes,
                                vmem_limit_bytes=VMEM_LIMIT)


def _sigmoid(v):
    return 0.5 * jnp.tanh(0.5 * v) + 0.5


def _silu(v):
    return v * _sigmoid(v)


def _log_sigmoid(v):
    return jnp.minimum(v, 0.0) - jnp.log1p(jnp.exp(-jnp.abs(v)))


def _split3(v):
    hi = v.astype(BF16)
    r1 = v - hi.astype(F32)
    mid = r1.astype(BF16)
    lo = (r1 - mid.astype(F32)).astype(BF16)
    return hi, mid, lo


def _dot(a, b):
    return jnp.dot(a, b, preferred_element_type=F32)


def _dot_nt(a, b):
    return lax.dot_general(a, b, (((1,), (1,)), ((), ())), preferred_element_type=F32)


def _dot_tn(a, b):
    return lax.dot_general(a, b, (((0,), (0,)), ((), ())), preferred_element_type=F32)


def _inproj_steps(x_ref, g_ref, w_refs, z_refs, xn_ref):
    wm_ref, wg_ref, wif_ref, wift_ref = w_refs
    zm_ref, zg_ref, zif_ref, zift_ref = z_refs

    def norm():
        x = x_ref[...]
        ms = jnp.mean(x * x, axis=-1, keepdims=True)
        xn_ref[...] = (x * lax.rsqrt(ms + EPS) * g_ref[...]).astype(BF16)

    def block(w_ref, z_ref, c0):
        def run():
            cols = slice(c0, c0 + INPROJ_TN)
            z_ref[:, cols] = _dot(xn_ref[...], w_ref[:, cols]).astype(BF16)
        return run

    def gates():
        zif_ref[...] = _dot(xn_ref[...], wif_ref[...])
        zift_ref[...] = _dot_nt(wift_ref[...], xn_ref[...])

    steps = [norm, gates]
    steps += [block(wm_ref, zm_ref, c0) for c0 in range(0, zm_ref.shape[1], INPROJ_TN)]
    steps += [block(wg_ref, zg_ref, c0) for c0 in range(0, zg_ref.shape[1], INPROJ_TN)]
    return steps


def _mixer_kernel(x_ref, xnext_ref, gmix_ref, wm_ref, wg_ref, wif_ref, wift_ref,
                  bif_ref, bift_ref, convq_ref, convk_ref, ghead_ref, wpool_ref, pscale_ref,
                  wa_ref, wb_ref, wo_ref, gffn_ref, wrh_ref, wrl_ref, br_ref,
                  tric_ref, trir_ref, stri_ref, ut_ref, sel_ref,
                  x2_ref, xn2_ref, rti_ref, rtf_ref, tstat_ref, srow_ref,
                  zm_ref, zg_ref, zif_ref, zift_ref, zm_nxt, zg_nxt, zif_nxt, zift_nxt, xn_ref,
                  halo_ref, q_ref, k_ref, h_ref, pool_ref, cst_ref, mst_ref):
    ts = x_ref.shape[0]
    d_pool = wa_ref.shape[0]
    d_ml = wb_ref.shape[0]
    dh = d_ml // N_HEADS
    n_chunks = ts // CHUNK
    j = pl.program_id(1)
    w_in_refs = (wm_ref, wg_ref, wif_ref, wift_ref)
    z_cur = (zm_ref, zg_ref, zif_ref, zift_ref)
    z_nxt = (zm_nxt, zg_nxt, zif_nxt, zift_nxt)
    first = (j == 0) & (pl.program_id(0) == 0)

    @pl.when(first)
    def _():
        for step in _inproj_steps(x_ref, gmix_ref, w_in_refs, z_cur, xn_ref):
            step()

    @pl.when(jnp.logical_not(first))
    def _():
        for dst, src in zip(z_cur, z_nxt):
            dst[...] = src[...]

    @pl.when(j == 0)
    def _():
        halo_ref[...] = jnp.zeros_like(halo_ref)
        cst_ref[...] = jnp.zeros_like(cst_ref)
        mst_ref[...] = jnp.zeros_like(mst_ref)

    pending = _inproj_steps(xnext_ref, gmix_ref, w_in_refs, z_nxt, xn_ref)

    def project_some(n=1):
        for _ in range(min(n, len(pending))):
            pending.pop(0)()

    project_some(2)

    row = lax.broadcasted_iota(jnp.int32, (ts, LANES), 0)
    pos1 = (row + j * ts + 1).astype(F32)

    def extended(cg):
        cols = slice(cg * LANES, (cg + 1) * LANES)
        cur = zm_ref[:, cols].astype(F32)
        ext = jnp.concatenate([halo_ref[:, cols], cur], axis=0)
        halo_ref[:, cols] = cur[ts - HALO:, :]
        return ext

    n_pool_groups = d_pool // LANES
    for g in range(n_pool_groups):
        w = POOL_WINDOWS[g]
        ext = extended(g)
        s, span = ext, 1
        while span < w:
            s = s + pltpu.roll(s, span, axis=0)
            span *= 2
        win = s[HALO:, :]
        cnt = jnp.minimum(pos1, float(w))
        d = win / cnt - ext[HALO:, :]
        y = _dot(d.astype(BF16), wpool_ref[g]) * pscale_ref[:, g * LANES:(g + 1) * LANES]
        pool_ref[:, g * LANES:(g + 1) * LANES] = y.astype(BF16)
        project_some()

    n_ml_groups = d_ml // LANES
    for which, (cw_ref, dst_ref, scale) in enumerate(((convq_ref, q_ref, 1.0), (convk_ref, k_ref, dh ** -0.5))):
        for g in range(n_ml_groups):
            cols = slice(g * LANES, (g + 1) * LANES)
            ext = extended(n_pool_groups + which * n_ml_groups + g)
            acc = ext * cw_ref[CONV_K - 1:CONV_K, cols]
            for sft in range(1, CONV_K):
                acc = acc + pltpu.roll(ext, sft, axis=0) * cw_ref[CONV_K - 1 - sft:CONV_K - sft, cols]
            dst_ref[:, cols] = (_silu(acc[HALO:, :]) * scale).astype(BF16)
        project_some()

    zc = zif_ref[...] + bif_ref[...]
    lf_c = _log_sigmoid(zc)
    bc = sum(_dot(tric_ref[...], p) for p in _split3(lf_c))
    zr = zift_ref[...] + bift_ref[...]
    lf_r = _log_sigmoid(zr)
    br = sum(_dot(p, trir_ref[...]) for p in _split3(lf_r))

    ti = lax.broadcasted_iota(jnp.int32, (CHUNK, CHUNK), 0)
    si = lax.broadcasted_iota(jnp.int32, (CHUNK, CHUNK), 1)
    causal = si <= ti
    ones_blk = jnp.ones((CHUNK, dh), BF16)
    v0 = d_pool + 2 * d_ml
    ig_rep = [jnp.broadcast_to(zc[:, h:h + 1], (ts, dh)) for h in range(N_HEADS)]
    bt_rep = [jnp.broadcast_to(bc[:, N_HEADS + h:N_HEADS + h + 1], (ts, dh)) for h in range(N_HEADS)]

    m_state = [mst_ref[h:h + 1, :] for h in range(N_HEADS)]
    c_state = [cst_ref[h] for h in range(N_HEADS)]
    for c in range(n_chunks):
        rs = slice(c * CHUNK, (c + 1) * CHUNK)
        for h in range(N_HEADS):
            hs = slice(h * dh, (h + 1) * dh)
            q = q_ref[rs, hs]
            k = k_ref[rs, hs]
            v_aug = jnp.concatenate([zm_ref[rs, v0 + h * dh:v0 + (h + 1) * dh], ones_blk], axis=-1)
            bt = bt_rep[h][rs, :]
            igc = ig_rep[h][rs, :]
            r_row = zr[h:h + 1, rs] - br[N_HEADS + h:N_HEADS + h + 1, rs]
            m_prev = m_state[h]
            c_prev = c_state[h]

            dmat = jnp.where(causal, bt[:, :CHUNK] + r_row, -jnp.inf)
            m_intra = jnp.max(dmat, axis=-1, keepdims=True)
            inter = bt + m_prev
            m_t = jnp.maximum(inter, m_intra)
            w_inter = jnp.exp(inter - m_t)
            smat = _dot_nt(q, k) * jnp.exp(dmat - m_t[:, :CHUNK])
            qc = _dot(q, c_prev.astype(BF16))
            sv = _dot(smat.astype(BF16), v_aug)
            nq = w_inter * qc[:, dh:] + sv[:, dh:]
            den = jnp.maximum(jnp.abs(nq), jnp.exp(-m_t))
            h_ref[rs, hs] = (w_inter * qc[:, :dh] + sv[:, :dh]) / den

            b_last = bt[CHUNK - 1:CHUNK, :]
            a_log = b_last - bt + igc
            a_max = jnp.max(a_log, axis=0, keepdims=True)
            m_new = jnp.maximum(b_last + m_prev, a_max)
            kw = (k.astype(F32) * jnp.exp(a_log - m_new)).astype(BF16)
            decay = jnp.exp(b_last + m_prev - m_new)
            c_state[h] = jnp.concatenate([decay, decay], axis=-1) * c_prev + _dot_tn(kw, v_aug)
            m_state[h] = m_new
            if h % 2 == 1:
                project_some()
    for h in range(N_HEADS):
        cst_ref[h] = c_state[h]
        mst_ref[h:h + 1, :] = m_state[h]

    o0 = v0 + d_ml
    for h in range(N_HEADS):
        hs = slice(h * dh, (h + 1) * dh)
        hv = h_ref[:, hs]
        mu = jnp.mean(hv, axis=-1, keepdims=True)
        hc = hv - mu
        var = jnp.mean(hc * hc, axis=-1, keepdims=True)
        hn = hc * lax.rsqrt(var + EPS) * ghead_ref[:, hs]
        og = _sigmoid(zm_ref[:, o0 + h * dh:o0 + (h + 1) * dh].astype(F32))
        q_ref[:, hs] = (og * hn).astype(BF16)
    y_a = _dot(pool_ref[...], wa_ref[...])
    y_b = _dot(q_ref[...], wb_ref[...])
    d_model = x_ref.shape[1]
    ga = _sigmoid(zg_ref[:, :d_model].astype(F32))
    gb = _sigmoid(zg_ref[:, d_model:].astype(F32))
    merged = (ga * y_a + gb * y_b).astype(BF16)
    x2 = x_ref[...] + _dot(merged, wo_ref[...])
    x2_ref[...] = x2

    ms = jnp.mean(x2 * x2, axis=-1, keepdims=True)
    xn2 = x2 * lax.rsqrt(ms + EPS) * gffn_ref[...]
    xh = xn2.astype(BF16)
    xn2_ref[...] = xh
    xl = (xn2 - xh.astype(F32)).astype(BF16)
    lg = _dot(xh, wrh_ref[...]) + _dot(xl, wrh_ref[...]) + _dot(xh, wrl_ref[...]) + br_ref[...]

    lane = lax.broadcasted_iota(jnp.int32, (ts, LANES), 1)
    lanef = lane.astype(F32)
    big = float(4 * LANES)
    gl = jnp.where(lane < N_GROUPS, lg, -jnp.inf)
    gmax = jnp.max(gl, axis=-1, keepdims=True)
    g_sel = jnp.min(jnp.where(gl == gmax, lanef, big), axis=-1, keepdims=True)
    p_g = 1.0 / jnp.sum(jnp.exp(gl - gmax), axis=-1, keepdims=True)
    project_some()
    lo = ROUTER_LANE0 + EXPERTS_PER_GROUP * g_sel
    el = jnp.where((lanef >= lo) & (lanef < lo + EXPERTS_PER_GROUP), lg, -jnp.inf)
    m1 = jnp.max(el, axis=-1, keepdims=True)
    i1 = jnp.min(jnp.where(el == m1, lanef, big), axis=-1, keepdims=True)
    project_some()
    el2 = jnp.where(lanef == i1, -jnp.inf, el)
    m2 = jnp.max(el2, axis=-1, keepdims=True)
    i2 = jnp.min(jnp.where(el2 == m2, lanef, big), axis=-1, keepdims=True)
    project_some()
    e2x = jnp.exp(m2 - m1)
    gate1 = p_g / (1.0 + e2x)
    gate2 = p_g * e2x / (1.0 + e2x)

    oh1 = lanef == i1
    oh2 = lanef == i2
    ohs = jnp.where(oh1 | oh2, 1.0, 0.0)
    n_loc = jnp.sum(ohs, axis=0, keepdims=True)
    pieces = jnp.floor((n_loc + (ROW_PIECE - 1.0)) * (1.0 / ROW_PIECE))
    piece_off = _dot(jnp.broadcast_to(pieces, (8, LANES)).astype(BF16), ut_ref[...])[0:1, :]
    base = _dot(stri_ref[...], ohs.astype(BF16)) + ROW_PIECE * piece_off
    slot1 = jnp.sum(jnp.where(oh1, base, 0.0), axis=-1, keepdims=True)
    slot2 = jnp.sum(jnp.where(oh2, base, 0.0), axis=-1, keepdims=True)
    tstat_ref[...] = jnp.broadcast_to(pieces, tstat_ref.shape).astype(jnp.int32)

    rti = jnp.where(lane == 0, i1 - ROUTER_LANE0,
                    jnp.where(lane == 1, i2 - ROUTER_LANE0,
                              jnp.where(lane == 2, slot1, jnp.where(lane == 3, slot2, 0.0))))
    rti_ref[...] = rti.astype(jnp.int32)
    rtf_ref[...] = jnp.where(lane == 0, gate1, jnp.where(lane == 1, gate2, 0.0))
    h1 = jnp.floor(slot1 * (1.0 / SLOT_RADIX))
    h2 = jnp.floor(slot2 * (1.0 / SLOT_RADIX))
    parts = jnp.where(lane == 0, h1, jnp.where(lane == 1, slot1 - SLOT_RADIX * h1,
                      jnp.where(lane == 2, h2, jnp.where(lane == 3, slot2 - SLOT_RADIX * h2, 0.0))))
    srow_ref[...] = _dot_nt(sel_ref[...], parts.astype(BF16))
    project_some(len(pending))


def _mixer(x2d, g_mix, w_main, w_gates, w_if_c, w_if_t, params, batch, seq):
    T, D = x2d.shape
    ts = min(MIX_TS, seq)
    nts = seq // ts
    d_pool = params["w_br_a"].shape[0]
    d_ml = params["w_br_b"].shape[0]
    dh = d_ml // N_HEADS

    idx = np.arange(ts)
    same_chunk = (idx[:, None] // CHUNK) == (idx[None, :] // CHUNK)
    tri_c = jnp.asarray((idx[None, :] <= idx[:, None]) & same_chunk, BF16)
    tri_r = jnp.asarray((idx[:, None] <= idx[None, :]) & same_chunk, BF16)
    stri = jnp.asarray(idx[None, :] < idx[:, None], BF16)
    lane_idx = np.arange(LANES)
    ut = jnp.asarray(lane_idx[:, None] < lane_idx[None, :], BF16)
    sel = jnp.asarray(np.arange(8)[:, None] == lane_idx[None, :], BF16)

    tok = lambda b, j: (b * nts + j, 0)
    tok_t = lambda b, j: (0, b * nts + j)
    c2 = lambda b, j: (0, 0)
    c3 = lambda b, j: (0, 0, 0)
    full = lambda a: pl.BlockSpec(a.shape, c2 if a.ndim == 2 else c3)
    consts = [params[n] for n in ("b_if", "b_if_t", "conv_q", "conv_k", "g_head", "w_pool", "pool_scale",
                                  "w_br_a", "w_br_b", "w_out", "g_ffn", "w_r_hi", "w_r_lo", "b_r")]
    consts = [g_mix, w_main, w_gates, w_if_c, w_if_t] + consts + [tri_c, tri_r, stri, ut, sel]
    n_tiles = batch * nts
    nm, ng = w_main.shape[1], w_gates.shape[1]
    tok_next = lambda b, j: (jnp.minimum(b * nts + j + 1, n_tiles - 1), 0)
    z_scratch = [pltpu.VMEM((ts, nm), BF16), pltpu.VMEM((ts, ng), BF16),
                 pltpu.VMEM((ts, LANES), F32), pltpu.VMEM((16, ts), F32)]
    return pl.pallas_call(
        _mixer_kernel,
        grid=(batch, nts),
        in_specs=[pl.BlockSpec((ts, D), tok),
                  pl.BlockSpec((ts, D), tok_next)] + [full(a) for a in consts],
        out_specs=[pl.BlockSpec((ts, D), tok),
                   pl.BlockSpec((ts, D), tok),
                   pl.BlockSpec((ts, LANES), tok),
                   pl.BlockSpec((ts, LANES), tok),
                   pl.BlockSpec((8, LANES), tok),
                   pl.BlockSpec((8, ts), tok_t)],
        out_shape=[jax.ShapeDtypeStruct((T, D), F32),
                   jax.ShapeDtypeStruct((T, D), BF16),
                   jax.ShapeDtypeStruct((T, LANES), jnp.int32),
                   jax.ShapeDtypeStruct((T, LANES), F32),
                   jax.ShapeDtypeStruct((n_tiles * 8, LANES), jnp.int32),
                   jax.ShapeDtypeStruct((8, T), F32)],
        scratch_shapes=z_scratch + z_scratch + [
                        pltpu.VMEM((ts, D), BF16),
                        pltpu.VMEM((HALO, d_pool + 2 * d_ml), F32),
                        pltpu.VMEM((ts, d_ml), BF16),
                        pltpu.VMEM((ts, d_ml), BF16),
                        pltpu.VMEM((ts, d_ml), F32),
                        pltpu.VMEM((ts, d_pool), BF16),
                        pltpu.VMEM((N_HEADS, dh, 2 * dh), F32),
                        pltpu.VMEM((8, LANES), F32)],
        compiler_params=_cparams(2),
        name="mixer",
    )(x2d, x2d, *consts)


def _for_each_piece(npieces_ref, glob_ref, tile, fn):
    base = tile * PIECES_PER_TILE

    def per_piece(p, carry):
        fn(pl.multiple_of(p * ROW_PIECE, ROW_PIECE), pl.multiple_of(glob_ref[base + p] * ROW_PIECE, ROW_PIECE))
        return carry

    lax.fori_loop(0, npieces_ref[tile], per_piece, 0)


def _dispatch_kernel(npieces_ref, glob_ref, zflag_ref,
                     xn_ref, srow_ref, buf_ref, rows_ref, zeros_ref, sem, zsem):
    tt = xn_ref.shape[0]
    sl = rows_ref.shape[1]
    n_blocks = buf_ref.shape[0] // MOE_TM
    i = pl.program_id(0)
    n_tiles = pl.num_programs(0)
    cur = lax.rem(i, 2)

    @pl.when(i == 0)
    def _():
        zeros_ref[...] = jnp.zeros_like(zeros_ref)

        def zero_copy(b):
            return pltpu.make_async_copy(zeros_ref, buf_ref.at[pl.ds(b * MOE_TM, MOE_TM)], zsem)

        def zero_start(b, carry):
            @pl.when(zflag_ref[b] > 0)
            def _():
                zero_copy(b).start()
            return carry

        def zero_wait(b, carry):
            @pl.when(zflag_ref[b] > 0)
            def _():
                zero_copy(b).wait()
            return carry

        lax.fori_loop(0, n_blocks, zero_start, 0)
        lax.fori_loop(0, n_blocks, zero_wait, 0)

    def piece_copy(buf_slot, local_row, global_row):
        return pltpu.make_async_copy(rows_ref.at[buf_slot, pl.ds(local_row, ROW_PIECE)],
                                     buf_ref.at[pl.ds(global_row, ROW_PIECE)], sem.at[buf_slot])

    def start_pieces(tile, buf_slot):
        _for_each_piece(npieces_ref, glob_ref, tile, lambda l, g: piece_copy(buf_slot, l, g).start())

    def wait_pieces(tile, buf_slot):
        _for_each_piece(npieces_ref, glob_ref, tile, lambda l, g: piece_copy(buf_slot, l, g).wait())

    @pl.when(i >= 2)
    def _():
        wait_pieces(i - 2, cur)

    sr = srow_ref[...]
    slot1 = SLOT_RADIX * sr[0:1, :] + sr[1:2, :]
    slot2 = SLOT_RADIX * sr[2:3, :] + sr[3:4, :]
    r = lax.broadcasted_iota(jnp.int32, (sl, tt), 0).astype(F32)
    sel = jnp.where((r == slot1) | (r == slot2), 1.0, 0.0).astype(BF16)
    rows_ref[cur] = _dot(sel, xn_ref[...])
    start_pieces(i, cur)

    @pl.when(i == n_tiles - 1)
    def _():
        @pl.when(i >= 1)
        def _():
            wait_pieces(i - 1, 1 - cur)
        wait_pieces(i, cur)


def _dispatch(xn2, srow, npieces, piece_glob, zflag, n_rows):
    T, D = xn2.shape
    tt = MIX_TS
    return pl.pallas_call(
        _dispatch_kernel,
        grid_spec=pltpu.PrefetchScalarGridSpec(
            num_scalar_prefetch=3,
            grid=(T // tt,),
            in_specs=[pl.BlockSpec((tt, D), lambda i, *_: (i, 0)),
                      pl.BlockSpec((8, tt), lambda i, *_: (0, i))],
            out_specs=pl.BlockSpec(memory_space=pl.ANY),
            scratch_shapes=[pltpu.VMEM((2, MOE_SL, D), F32),
                            pltpu.VMEM((MOE_TM, D), F32),
                            pltpu.SemaphoreType.DMA((2,)),
                            pltpu.SemaphoreType.DMA(())]),
        out_shape=jax.ShapeDtypeStruct((n_rows, D), F32),
        compiler_params=_cparams(1),
        name="dispatch",
    )(npieces, piece_glob, zflag, xn2, srow)


def _experts_kernel(blk_e_ref, nused_ref, x_ref, wg_ref, wu_ref, wd_ref, y_ref, wgb_ref, wub_ref, wdb_ref):
    i = pl.program_id(0)
    used = i < nused_ref[0]

    @pl.when(used & ((i == 0) | (blk_e_ref[i] != blk_e_ref[jnp.maximum(i - 1, 0)])))
    def _():
        wgb_ref[...] = wg_ref[0].astype(BF16)
        wub_ref[...] = wu_ref[0].astype(BF16)
        wdb_ref[...] = wd_ref[0].astype(BF16)

    @pl.when(used)
    def _():
        x = x_ref[...].astype(BF16)
        hg = _dot(x, wgb_ref[...])
        hu = _dot(x, wub_ref[...])
        hid = (_silu(hg) * hu).astype(BF16)
        y_ref[...] = _dot(hid, wdb_ref[...])

    @pl.when(jnp.logical_not(used))
    def _():
        y_ref[...] = jnp.zeros_like(y_ref)


def _experts(buf, blk_e, nused, w_gate, w_up, w_down):
    R, D = buf.shape
    de = w_gate.shape[2]
    n_blocks = R // MOE_TM

    def blk(i, blk_e_ref, nused_ref):
        return jnp.minimum(i, nused_ref[0] - 1)

    row_map = lambda i, be, nu: (blk(i, be, nu), 0)
    w_map = lambda i, be, nu: (be[blk(i, be, nu)], 0, 0)
    return pl.pallas_call(
        _experts_kernel,
        grid_spec=pltpu.PrefetchScalarGridSpec(
            num_scalar_prefetch=2,
            grid=(n_blocks,),
            in_specs=[pl.BlockSpec((MOE_TM, D), row_map),
                      pl.BlockSpec((1, D, de), w_map),
                      pl.BlockSpec((1, D, de), w_map),
                      pl.BlockSpec((1, de, D), w_map)],
            out_specs=pl.BlockSpec((MOE_TM, D), lambda i, be, nu: (i, 0)),
            scratch_shapes=[pltpu.VMEM((D, de), BF16), pltpu.VMEM((D, de), BF16), pltpu.VMEM((de, D), BF16)]),
        out_shape=jax.ShapeDtypeStruct((R, D), F32),
        compiler_params=_cparams(1),
        name="experts",
    )(blk_e, nused, buf, w_gate, w_up, w_down)


def _combine_kernel(npieces_ref, glob_ref, x2_ref, rti_ref, rtf_ref, gfin_ref, yb_ref, out_ref,
                    rows_ref, sem):
    tt = x2_ref.shape[0]
    sl = rows_ref.shape[1]
    i = pl.program_id(0)
    n_tiles = pl.num_programs(0)
    cur = lax.rem(i, 2)

    def piece_copy(buf_slot, local_row, global_row):
        return pltpu.make_async_copy(yb_ref.at[pl.ds(global_row, ROW_PIECE)],
                                     rows_ref.at[buf_slot, pl.ds(local_row, ROW_PIECE)], sem.at[buf_slot])

    def start_pieces(tile, buf_slot):
        _for_each_piece(npieces_ref, glob_ref, tile, lambda l, g: piece_copy(buf_slot, l, g).start())

    def wait_pieces(tile, buf_slot):
        _for_each_piece(npieces_ref, glob_ref, tile, lambda l, g: piece_copy(buf_slot, l, g).wait())

    @pl.when(i == 0)
    def _():
        rows_ref[...] = jnp.zeros_like(rows_ref)
        start_pieces(0, 0)

    @pl.when(i + 1 < n_tiles)
    def _():
        start_pieces(i + 1, 1 - cur)

    wait_pieces(i, cur)

    rti = rti_ref[...]
    rtf = rtf_ref[...]
    slot1 = rti[:, 2:3]
    slot2 = rti[:, 3:4]
    lane = lax.broadcasted_iota(jnp.int32, (tt, sl), 1)
    g = jnp.where(lane == slot1, rtf[:, 0:1], jnp.where(lane == slot2, rtf[:, 1:2], 0.0)).astype(BF16)
    y = x2_ref[...] + _dot(g, rows_ref[cur].astype(BF16))
    ms = jnp.mean(y * y, axis=-1, keepdims=True)
    out_ref[...] = y * lax.rsqrt(ms + EPS) * gfin_ref[...]


def _combine(x2, rti, rtf, g_final, yb, npieces, piece_glob):
    T, D = x2.shape
    tt = MIX_TS
    tok = lambda i, *_: (i, 0)
    return pl.pallas_call(
        _combine_kernel,
        grid_spec=pltpu.PrefetchScalarGridSpec(
            num_scalar_prefetch=2,
            grid=(T // tt,),
            in_specs=[pl.BlockSpec((tt, D), tok),
                      pl.BlockSpec((tt, LANES), tok),
                      pl.BlockSpec((tt, LANES), tok),
                      pl.BlockSpec((1, D), lambda i, *_: (0, 0)),
                      pl.BlockSpec(memory_space=pl.ANY)],
            out_specs=pl.BlockSpec((tt, D), tok),
            scratch_shapes=[pltpu.VMEM((2, MOE_SL, D), F32),
                            pltpu.SemaphoreType.DMA((2,))]),
        out_shape=jax.ShapeDtypeStruct((T, D), F32),
        compiler_params=_cparams(1),
        name="combine",
    )(npieces, piece_glob, x2, rti, rtf, g_final, yb)


def _pad_lanes(a, width=LANES):
    return jnp.pad(a, ((0, 0), (0, width - a.shape[1])))


def kernel(x, g_mix, w_in, b_if, conv_q, conv_k, g_head, w_pool, pool_scale, w_br_a, w_br_b, w_out,
           g_ffn, w_rg, b_rg, w_re, b_re, w_e_gate, w_e_up, w_e_down, g_final):
    B, S, D = x.shape
    T = B * S
    assert g_mix.shape[0] == 1, "single-layer block"
    assert S % MIX_TS == 0
    d_pool = w_br_a.shape[1]
    d_ml = w_br_b.shape[1]
    x2d = x.reshape(T, D)

    n_main = d_pool + 4 * d_ml
    w_l = w_in[0]
    w_main = w_l[:, :n_main].astype(BF16)
    w_if = w_l[:, n_main:n_main + 2 * N_HEADS]
    w_gates = w_l[:, n_main + 2 * N_HEADS:].astype(BF16)
    w_if_c = _pad_lanes(w_if).astype(BF16)
    w_if_t = jnp.pad(w_if.T, ((0, 16 - 2 * N_HEADS), (0, 0))).astype(BF16)
    w_r = _pad_lanes(jnp.concatenate([w_rg[0], w_re[0]], axis=1))
    w_r_hi = w_r.astype(BF16)
    w_r_lo = (w_r - w_r_hi.astype(F32)).astype(BF16)
    params = {
        "b_if": _pad_lanes(b_if[0][None, :]),
        "b_if_t": jnp.pad(b_if[0][:, None], ((0, 16 - 2 * N_HEADS), (0, 0))),
        "conv_q": conv_q[0], "conv_k": conv_k[0],
        "g_head": g_head[0][None, :],
        "w_pool": w_pool[0].astype(BF16),
        "pool_scale": pool_scale[0][None, :],
        "w_br_a": w_br_a[0].astype(BF16), "w_br_b": w_br_b[0].astype(BF16),
        "w_out": w_out[0].astype(BF16),
        "g_ffn": g_ffn[0][None, :],
        "w_r_hi": w_r_hi, "w_r_lo": w_r_lo,
        "b_r": _pad_lanes(jnp.concatenate([b_rg[0], b_re[0]])[None, :]),
    }

    x2, xn2, rti, rtf, tstat, srow = _mixer(x2d, g_mix[0][None, :], w_main, w_gates, w_if_c, w_if_t, params, B, S)

    n_tiles = T // MIX_TS
    pcs = tstat.reshape(n_tiles, 8, LANES)[:, 0, ROUTER_LANE0:ROUTER_LANE0 + N_EXPERTS]
    piece_loc = jnp.cumsum(pcs, axis=1) - pcs
    rows_e = jnp.sum(pcs, axis=0) * ROW_PIECE
    padded = (rows_e + MOE_TM - 1) // MOE_TM * MOE_TM
    pend = jnp.cumsum(padded)
    poff = pend - padded
    piece_glob = poff[None, :] // ROW_PIECE + jnp.cumsum(pcs, axis=0) - pcs
    n_rows = n_tiles * MOE_SL + N_EXPERTS * MOE_TM
    n_blocks = n_rows // MOE_TM
    nused = (pend[-1:] // MOE_TM).astype(jnp.int32)
    blk_ids = jnp.arange(n_blocks, dtype=jnp.int32)
    blk_e = jnp.sum((pend[None, :] <= blk_ids[:, None] * MOE_TM).astype(jnp.int32), axis=1)
    blk_e = jnp.minimum(blk_e, N_EXPERTS - 1)
    zflag = (((blk_ids + 1) * MOE_TM == pend[blk_e]) | (blk_ids >= nused[0])).astype(jnp.int32)
    p_ids = jnp.arange(PIECES_PER_TILE, dtype=jnp.int32)
    piece_end = piece_loc + pcs
    e_of_p = jnp.minimum(jnp.sum((piece_end[:, None, :] <= p_ids[None, :, None]).astype(jnp.int32), axis=2),
                         N_EXPERTS - 1)
    e_ids = jnp.arange(N_EXPERTS, dtype=jnp.int32)
    shift = jnp.sum(jnp.where(e_of_p[:, :, None] == e_ids[None, None, :], (piece_glob - piece_loc)[:, None, :], 0),
                    axis=2)
    glob_of_p = (shift + p_ids[None, :]).astype(jnp.int32)
    glob_of_p = glob_of_p.reshape(n_tiles * PIECES_PER_TILE)
    npieces = jnp.sum(pcs, axis=1).astype(jnp.int32)

    buf = _dispatch(xn2, srow, npieces, glob_of_p, zflag, n_rows)
    yb = _experts(buf, blk_e, nused, w_e_gate[0], w_e_up[0], w_e_down[0])
    out = _combine(x2, rti, rtf, g_final[None, :], yb, npieces, glob_of_p)
    return out.reshape(B, S, D)
```

```python
import functools

import numpy as np
import jax
import jax.numpy as jnp
from jax import lax
from jax.experimental import pallas as pl
from jax.experimental.pallas import tpu as pltpu

F32 = jnp.float32
BF16 = jnp.bfloat16

CHUNK = 64
POOL_WINDOWS = (2, 4, 8, 16)
N_HEADS = 4
CONV_K = 4
N_GROUPS = 4
EXPERTS_PER_GROUP = 8
N_EXPERTS = N_GROUPS * EXPERTS_PER_GROUP
TOP_K = 2
EPS = 1e-6

LANES = 128
HALO = 16
ROUTER_LANE0 = N_GROUPS

INPROJ_TN = 256
MIX_TS = 256
MOE_TM = 512
ROW_PIECE = 8
MOE_SL = TOP_K * MIX_TS + N_EXPERTS * ROW_PIECE
PIECES_PER_TILE = MOE_SL // ROW_PIECE
PIECE_UNROLL = 4
EXPERT_SUB = 128
SLOT_RADIX = 16
VMEM_LIMIT = 56 * 1024 * 1024


def _cparams(n_axes):
    return pltpu.CompilerParams(dimension_semantics=("arbitrary",) * n_axes,
                                vmem_limit_bytes=VMEM_LIMIT)


def _sigmoid(v):
    return 0.5 * jnp.tanh(0.5 * v) + 0.5


def _silu(v):
    return v * _sigmoid(v)


def _log_sigmoid(v):
    return jnp.minimum(v, 0.0) - jnp.log1p(jnp.exp(-jnp.abs(v)))


def _split3(v):
    hi = v.astype(BF16)
    r1 = v - hi.astype(F32)
    mid = r1.astype(BF16)
    lo = (r1 - mid.astype(F32)).astype(BF16)
    return hi, mid, lo


def _dot(a, b):
    return jnp.dot(a, b, preferred_element_type=F32)


def _dot_nt(a, b):
    return lax.dot_general(a, b, (((1,), (1,)), ((), ())), preferred_element_type=F32)


def _dot_tn(a, b):
    return lax.dot_general(a, b, (((0,), (0,)), ((), ())), preferred_element_type=F32)


def _inproj_steps(x_ref, g_ref, w_refs, z_refs, xn_ref):
    wm_ref, wg_ref, wif_ref, wift_ref = w_refs
    zm_ref, zg_ref, zif_ref, zift_ref = z_refs

    def norm():
        x = x_ref[...]
        ms = jnp.mean(x * x, axis=-1, keepdims=True)
        xn_ref[...] = (x * lax.rsqrt(ms + EPS) * g_ref[...]).astype(BF16)

    def block(w_ref, z_ref, c0):
        def run():
            cols = slice(c0, c0 + INPROJ_TN)
            z_ref[:, cols] = _dot(xn_ref[...], w_ref[:, cols]).astype(BF16)
        return run

    def gates():
        zif_ref[...] = _dot(xn_ref[...], wif_ref[...])
        zift_ref[...] = _dot_nt(wift_ref[...], xn_ref[...])

    steps = [norm, gates]
    steps += [block(wm_ref, zm_ref, c0) for c0 in range(0, zm_ref.shape[1], INPROJ_TN)]
    steps += [block(wg_ref, zg_ref, c0) for c0 in range(0, zg_ref.shape[1], INPROJ_TN)]
    return steps


def _mixer_kernel(x_ref, xnext_ref, gmix_ref, wm_ref, wg_ref, wif_ref, wift_ref,
                  bif_ref, bift_ref, convq_ref, convk_ref, ghead_ref, wpool_ref, pscale_ref,
                  wa_ref, wb_ref, wo_ref, gffn_ref, wrh_ref, wrl_ref, br_ref,
                  tric_ref, trir_ref, stri_ref, ut_ref, sel_ref,
                  x2_ref, xn2_ref, rti_ref, rtf_ref, tstat_ref, srow_ref,
                  zm_ref, zg_ref, zif_ref, zift_ref, zm_nxt, zg_nxt, zif_nxt, zift_nxt, xn_ref,
                  ext_ref, q_ref, k_ref, h_ref, pool_ref, cst_ref, mst_ref):
    ts = x_ref.shape[0]
    d_pool = wa_ref.shape[0]
    d_ml = wb_ref.shape[0]
    dh = d_ml // N_HEADS
    n_chunks = ts // CHUNK
    j = pl.program_id(1)
    w_in_refs = (wm_ref, wg_ref, wif_ref, wift_ref)
    z_cur = (zm_ref, zg_ref, zif_ref, zift_ref)
    z_nxt = (zm_nxt, zg_nxt, zif_nxt, zift_nxt)
    first = (j == 0) & (pl.program_id(0) == 0)

    @pl.when(first)
    def _():
        for step in _inproj_steps(x_ref, gmix_ref, w_in_refs, z_cur, xn_ref):
            step()

    @pl.when(jnp.logical_not(first))
    def _():
        for dst, src in zip(z_cur, z_nxt):
            dst[...] = src[...]

    @pl.when(j == 0)
    def _():
        ext_ref[:, :HALO, :] = jnp.zeros((ext_ref.shape[0], HALO, LANES), F32)
        cst_ref[...] = jnp.zeros_like(cst_ref)
        mst_ref[...] = jnp.zeros_like(mst_ref)

    pending = _inproj_steps(xnext_ref, gmix_ref, w_in_refs, z_nxt, xn_ref)

    def project_some(n=1):
        for _ in range(min(n, len(pending))):
            pending.pop(0)()

    project_some(2)

    row = lax.broadcasted_iota(jnp.int32, (ts, LANES), 0)
    pos1 = (row + j * ts + 1).astype(F32)

    def history(cg):
        cur = zm_ref[:, cg * LANES:(cg + 1) * LANES].astype(F32)
        ext_ref[cg, HALO:, :] = cur
        return cur, lambda s: ext_ref[cg, HALO - s:HALO - s + ts, :]

    def keep_history(cg, cur):
        ext_ref[cg, :HALO, :] = cur[ts - HALO:, :]

    n_pool_groups = d_pool // LANES
    for g in range(n_pool_groups):
        w = POOL_WINDOWS[g]
        cur, shifted = history(g)
        win = cur
        for s in range(1, w):
            win = win + shifted(s)
        keep_history(g, cur)
        cnt = jnp.minimum(pos1, float(w))
        d = win / cnt - cur
        y = _dot(d.astype(BF16), wpool_ref[g]) * pscale_ref[:, g * LANES:(g + 1) * LANES]
        pool_ref[:, g * LANES:(g + 1) * LANES] = y.astype(BF16)
        project_some()

    n_ml_groups = d_ml // LANES
    for which, (cw_ref, dst_ref, scale) in enumerate(((convq_ref, q_ref, 1.0), (convk_ref, k_ref, dh ** -0.5))):
        for g in range(n_ml_groups):
            cols = slice(g * LANES, (g + 1) * LANES)
            cg = n_pool_groups + which * n_ml_groups + g
            cur, shifted = history(cg)
            acc = cur * cw_ref[CONV_K - 1:CONV_K, cols]
            for sft in range(1, CONV_K):
                acc = acc + shifted(sft) * cw_ref[CONV_K - 1 - sft:CONV_K - sft, cols]
            keep_history(cg, cur)
            dst_ref[:, cols] = (_silu(acc) * scale).astype(BF16)
        project_some()

    zc = zif_ref[...] + bif_ref[...]
    lf_c = _log_sigmoid(zc)
    bc = sum(_dot(tric_ref[...], p) for p in _split3(lf_c))
    zr = zift_ref[...] + bift_ref[...]
    lf_r = _log_sigmoid(zr)
    br = sum(_dot(p, trir_ref[...]) for p in _split3(lf_r))

    ti = lax.broadcasted_iota(jnp.int32, (CHUNK, CHUNK), 0)
    si = lax.broadcasted_iota(jnp.int32, (CHUNK, CHUNK), 1)
    causal = si <= ti
    ones_blk = jnp.ones((CHUNK, dh), BF16)
    v0 = d_pool + 2 * d_ml
    ig_rep = [jnp.broadcast_to(zc[:, h:h + 1], (ts, dh)) for h in range(N_HEADS)]
    bt_rep = [jnp.broadcast_to(bc[:, N_HEADS + h:N_HEADS + h + 1], (ts, dh)) for h in range(N_HEADS)]

    m_state = [mst_ref[h:h + 1, :] for h in range(N_HEADS)]
    c_state = [cst_ref[h] for h in range(N_HEADS)]
    for c in range(n_chunks):
        rs = slice(c * CHUNK, (c + 1) * CHUNK)
        for h in range(N_HEADS):
            hs = slice(h * dh, (h + 1) * dh)
            q = q_ref[rs, hs]
            k = k_ref[rs, hs]
            v_aug = jnp.concatenate([zm_ref[rs, v0 + h * dh:v0 + (h + 1) * dh], ones_blk], axis=-1)
            bt = bt_rep[h][rs, :]
            igc = ig_rep[h][rs, :]
            r_row = zr[h:h + 1, rs] - br[N_HEADS + h:N_HEADS + h + 1, rs]
            m_prev = m_state[h]
            c_prev = c_state[h]

            dmat = jnp.where(causal, bt[:, :CHUNK] + r_row, -jnp.inf)
            m_intra = jnp.max(dmat, axis=-1, keepdims=True)
            inter = bt + m_prev
            m_t = jnp.maximum(inter, m_intra)
            w_inter = jnp.exp(inter - m_t)
            smat = _dot_nt(q, k) * jnp.exp(dmat - m_t[:, :CHUNK])
            qc = _dot(q, c_prev.astype(BF16))
            sv = _dot(smat.astype(BF16), v_aug)
            nq = w_inter * qc[:, dh:] + sv[:, dh:]
            den = jnp.maximum(jnp.abs(nq), jnp.exp(-m_t))
            h_ref[rs, hs] = (w_inter * qc[:, :dh] + sv[:, :dh]) / den

            b_last = bt[CHUNK - 1:CHUNK, :]
            a_log = b_last - bt + igc
            a_max = jnp.max(a_log, axis=0, keepdims=True)
            m_new = jnp.maximum(b_last + m_prev, a_max)
            kw = (k.astype(F32) * jnp.exp(a_log - m_new)).astype(BF16)
            decay = jnp.exp(b_last + m_prev - m_new)
            c_state[h] = jnp.concatenate([decay, decay], axis=-1) * c_prev + _dot_tn(kw, v_aug)
            m_state[h] = m_new
            if h % 2 == 1:
                project_some()
    for h in range(N_HEADS):
        cst_ref[h] = c_state[h]
        mst_ref[h:h + 1, :] = m_state[h]

    o0 = v0 + d_ml
    for h in range(N_HEADS):
        hs = slice(h * dh, (h + 1) * dh)
        hv = h_ref[:, hs]
        mu = jnp.mean(hv, axis=-1, keepdims=True)
        hc = hv - mu
        var = jnp.mean(hc * hc, axis=-1, keepdims=True)
        hn = hc * lax.rsqrt(var + EPS) * ghead_ref[:, hs]
        og = _sigmoid(zm_ref[:, o0 + h * dh:o0 + (h + 1) * dh].astype(F32))
        q_ref[:, hs] = (og * hn).astype(BF16)
    y_a = _dot(pool_ref[...], wa_ref[...])
    y_b = _dot(q_ref[...], wb_ref[...])
    d_model = x_ref.shape[1]
    ga = _sigmoid(zg_ref[:, :d_model].astype(F32))
    gb = _sigmoid(zg_ref[:, d_model:].astype(F32))
    merged = (ga * y_a + gb * y_b).astype(BF16)
    x2 = x_ref[...] + _dot(merged, wo_ref[...])
    x2_ref[...] = x2

    ms = jnp.mean(x2 * x2, axis=-1, keepdims=True)
    xn2 = x2 * lax.rsqrt(ms + EPS) * gffn_ref[...]
    xh = xn2.astype(BF16)
    xn2_ref[...] = xh
    xl = (xn2 - xh.astype(F32)).astype(BF16)
    lg = _dot(xh, wrh_ref[...]) + _dot(xl, wrh_ref[...]) + _dot(xh, wrl_ref[...]) + br_ref[...]

    lane = lax.broadcasted_iota(jnp.int32, (ts, LANES), 1)
    lanef = lane.astype(F32)
    big = float(4 * LANES)
    gl = jnp.where(lane < N_GROUPS, lg, -jnp.inf)
    gmax = jnp.max(gl, axis=-1, keepdims=True)
    g_sel = jnp.min(jnp.where(gl == gmax, lanef, big), axis=-1, keepdims=True)
    p_g = 1.0 / jnp.sum(jnp.exp(gl - gmax), axis=-1, keepdims=True)
    project_some()
    lo = ROUTER_LANE0 + EXPERTS_PER_GROUP * g_sel
    el = jnp.where((lanef >= lo) & (lanef < lo + EXPERTS_PER_GROUP), lg, -jnp.inf)
    m1 = jnp.max(el, axis=-1, keepdims=True)
    i1 = jnp.min(jnp.where(el == m1, lanef, big), axis=-1, keepdims=True)
    project_some()
    el2 = jnp.where(lanef == i1, -jnp.inf, el)
    m2 = jnp.max(el2, axis=-1, keepdims=True)
    i2 = jnp.min(jnp.where(el2 == m2, lanef, big), axis=-1, keepdims=True)
    project_some()
    e2x = jnp.exp(m2 - m1)
    gate1 = p_g / (1.0 + e2x)
    gate2 = p_g * e2x / (1.0 + e2x)

    oh1 = lanef == i1
    oh2 = lanef == i2
    ohs = jnp.where(oh1 | oh2, 1.0, 0.0)
    n_loc = jnp.sum(ohs, axis=0, keepdims=True)
    pieces = jnp.floor((n_loc + (ROW_PIECE - 1.0)) * (1.0 / ROW_PIECE))
    piece_off = _dot(jnp.broadcast_to(pieces, (8, LANES)).astype(BF16), ut_ref[...])[0:1, :]
    base = _dot(stri_ref[...], ohs.astype(BF16)) + ROW_PIECE * piece_off
    slot1 = jnp.sum(jnp.where(oh1, base, 0.0), axis=-1, keepdims=True)
    slot2 = jnp.sum(jnp.where(oh2, base, 0.0), axis=-1, keepdims=True)
    tstat_ref[...] = jnp.broadcast_to(pieces, tstat_ref.shape).astype(jnp.int32)

    rti = jnp.where(lane == 0, i1 - ROUTER_LANE0,
                    jnp.where(lane == 1, i2 - ROUTER_LANE0,
                              jnp.where(lane == 2, slot1, jnp.where(lane == 3, slot2, 0.0))))
    rti_ref[...] = rti.astype(jnp.int32)
    rtf_ref[...] = jnp.where(lane == 0, gate1, jnp.where(lane == 1, gate2, 0.0))
    h1 = jnp.floor(slot1 * (1.0 / SLOT_RADIX))
    h2 = jnp.floor(slot2 * (1.0 / SLOT_RADIX))
    parts = jnp.where(lane == 0, h1, jnp.where(lane == 1, slot1 - SLOT_RADIX * h1,
                      jnp.where(lane == 2, h2, jnp.where(lane == 3, slot2 - SLOT_RADIX * h2, 0.0))))
    srow_ref[...] = _dot_nt(sel_ref[...], parts.astype(BF16))
    project_some(len(pending))


def _mixer(x2d, g_mix, w_main, w_gates, w_if_c, w_if_t, params, batch, seq):
    T, D = x2d.shape
    ts = min(MIX_TS, seq)
    nts = seq // ts
    d_pool = params["w_br_a"].shape[0]
    d_ml = params["w_br_b"].shape[0]
    dh = d_ml // N_HEADS

    idx = np.arange(ts)
    same_chunk = (idx[:, None] // CHUNK) == (idx[None, :] // CHUNK)
    tri_c = jnp.asarray((idx[None, :] <= idx[:, None]) & same_chunk, BF16)
    tri_r = jnp.asarray((idx[:, None] <= idx[None, :]) & same_chunk, BF16)
    stri = jnp.asarray(idx[None, :] < idx[:, None], BF16)
    lane_idx = np.arange(LANES)
    ut = jnp.asarray(lane_idx[:, None] < lane_idx[None, :], BF16)
    sel = jnp.asarray(np.arange(8)[:, None] == lane_idx[None, :], BF16)

    tok = lambda b, j: (b * nts + j, 0)
    tok_t = lambda b, j: (0, b * nts + j)
    c2 = lambda b, j: (0, 0)
    c3 = lambda b, j: (0, 0, 0)
    full = lambda a: pl.BlockSpec(a.shape, c2 if a.ndim == 2 else c3)
    consts = [params[n] for n in ("b_if", "b_if_t", "conv_q", "conv_k", "g_head", "w_pool", "pool_scale",
                                  "w_br_a", "w_br_b", "w_out", "g_ffn", "w_r_hi", "w_r_lo", "b_r")]
    consts = [g_mix, w_main, w_gates, w_if_c, w_if_t] + consts + [tri_c, tri_r, stri, ut, sel]
    n_tiles = batch * nts
    nm, ng = w_main.shape[1], w_gates.shape[1]
    tok_next = lambda b, j: (jnp.minimum(b * nts + j + 1, n_tiles - 1), 0)
    z_scratch = [pltpu.VMEM((ts, nm), BF16), pltpu.VMEM((ts, ng), BF16),
                 pltpu.VMEM((ts, LANES), F32), pltpu.VMEM((16, ts), F32)]
    return pl.pallas_call(
        _mixer_kernel,
        grid=(batch, nts),
        in_specs=[pl.BlockSpec((ts, D), tok),
                  pl.BlockSpec((ts, D), tok_next)] + [full(a) for a in consts],
        out_specs=[pl.BlockSpec((ts, D), tok),
                   pl.BlockSpec((ts, D), tok),
                   pl.BlockSpec((ts, LANES), tok),
                   pl.BlockSpec((ts, LANES), tok),
                   pl.BlockSpec((8, LANES), tok),
                   pl.BlockSpec((8, ts), tok_t)],
        out_shape=[jax.ShapeDtypeStruct((T, D), F32),
                   jax.ShapeDtypeStruct((T, D), BF16),
                   jax.ShapeDtypeStruct((T, LANES), jnp.int32),
                   jax.ShapeDtypeStruct((T, LANES), F32),
                   jax.ShapeDtypeStruct((n_tiles * 8, LANES), jnp.int32),
                   jax.ShapeDtypeStruct((8, T), F32)],
        scratch_shapes=z_scratch + z_scratch + [
                        pltpu.VMEM((ts, D), BF16),
                        pltpu.VMEM(((d_pool + 2 * d_ml) // LANES, HALO + ts, LANES), F32),
                        pltpu.VMEM((ts, d_ml), BF16),
                        pltpu.VMEM((ts, d_ml), BF16),
                        pltpu.VMEM((ts, d_ml), F32),
                        pltpu.VMEM((ts, d_pool), BF16),
                        pltpu.VMEM((N_HEADS, dh, 2 * dh), F32),
                        pltpu.VMEM((8, LANES), F32)],
        compiler_params=_cparams(2),
        name="mixer",
    )(x2d, x2d, *consts)


def _for_each_piece(npieces_ref, glob_ref, tile, fn):
    base = tile * PIECES_PER_TILE
    n = npieces_ref[tile]

    def one(p):
        fn(pl.multiple_of(p * ROW_PIECE, ROW_PIECE), pl.multiple_of(glob_ref[base + p] * ROW_PIECE, ROW_PIECE))

    def group(g, carry):
        for u in range(PIECE_UNROLL):
            one(g * PIECE_UNROLL + u)
        return carry

    n_groups = lax.div(n, jnp.int32(PIECE_UNROLL))
    lax.fori_loop(0, n_groups, group, 0)
    for u in range(PIECE_UNROLL - 1):
        @pl.when(n_groups * PIECE_UNROLL + u < n)
        def _():
            one(n_groups * PIECE_UNROLL + u)


def _dispatch_kernel(npieces_ref, glob_ref, zflag_ref,
                     xn_ref, srow_ref, buf_ref, rows_ref, zeros_ref, sem, zsem):
    tt = xn_ref.shape[0]
    sl = rows_ref.shape[1]
    n_blocks = buf_ref.shape[0] // MOE_TM
    i = pl.program_id(0)
    n_tiles = pl.num_programs(0)
    cur = lax.rem(i, 2)

    @pl.when(i == 0)
    def _():
        zeros_ref[...] = jnp.zeros_like(zeros_ref)

        def zero_copy(b):
            return pltpu.make_async_copy(zeros_ref, buf_ref.at[pl.ds(b * MOE_TM, MOE_TM)], zsem)

        def zero_start(b, carry):
            @pl.when(zflag_ref[b] > 0)
            def _():
                zero_copy(b).start()
            return carry

        def zero_wait(b, carry):
            @pl.when(zflag_ref[b] > 0)
            def _():
                zero_copy(b).wait()
            return carry

        lax.fori_loop(0, n_blocks, zero_start, 0)
        lax.fori_loop(0, n_blocks, zero_wait, 0)

    def piece_copy(buf_slot, local_row, global_row):
        return pltpu.make_async_copy(rows_ref.at[buf_slot, pl.ds(local_row, ROW_PIECE)],
                                     buf_ref.at[pl.ds(global_row, ROW_PIECE)], sem.at[buf_slot])

    def start_pieces(tile, buf_slot):
        _for_each_piece(npieces_ref, glob_ref, tile, lambda l, g: piece_copy(buf_slot, l, g).start())

    def wait_pieces(tile, buf_slot):
        _for_each_piece(npieces_ref, glob_ref, tile, lambda l, g: piece_copy(buf_slot, l, g).wait())

    @pl.when(i >= 2)
    def _():
        wait_pieces(i - 2, cur)

    sr = srow_ref[...]
    slot1 = SLOT_RADIX * sr[0:1, :] + sr[1:2, :]
    slot2 = SLOT_RADIX * sr[2:3, :] + sr[3:4, :]
    r = lax.broadcasted_iota(jnp.int32, (sl, tt), 0).astype(F32)
    sel = jnp.where((r == slot1) | (r == slot2), 1.0, 0.0).astype(BF16)
    rows_ref[cur] = _dot(sel, xn_ref[...])
    start_pieces(i, cur)

    @pl.when(i == n_tiles - 1)
    def _():
        @pl.when(i >= 1)
        def _():
            wait_pieces(i - 1, 1 - cur)
        wait_pieces(i, cur)


def _dispatch(xn2, srow, npieces, piece_glob, zflag, n_rows):
    T, D = xn2.shape
    tt = MIX_TS
    return pl.pallas_call(
        _dispatch_kernel,
        grid_spec=pltpu.PrefetchScalarGridSpec(
            num_scalar_prefetch=3,
            grid=(T // tt,),
            in_specs=[pl.BlockSpec((tt, D), lambda i, *_: (i, 0)),
                      pl.BlockSpec((8, tt), lambda i, *_: (0, i))],
            out_specs=pl.BlockSpec(memory_space=pl.ANY),
            scratch_shapes=[pltpu.VMEM((2, MOE_SL, D), F32),
                            pltpu.VMEM((MOE_TM, D), F32),
                            pltpu.SemaphoreType.DMA((2,)),
                            pltpu.SemaphoreType.DMA(())]),
        out_shape=jax.ShapeDtypeStruct((n_rows, D), F32),
        compiler_params=_cparams(1),
        name="dispatch",
    )(npieces, piece_glob, zflag, xn2, srow)


def _experts_kernel(blk_e_ref, nused_ref, nsub_ref, x_ref, wg_ref, wu_ref, wd_ref, y_ref,
                    wgb_ref, wub_ref, wdb_ref):
    i = pl.program_id(0)
    used = i < nused_ref[0]
    n_sub = nsub_ref[i]

    @pl.when(used & ((i == 0) | (blk_e_ref[i] != blk_e_ref[jnp.maximum(i - 1, 0)])))
    def _():
        wgb_ref[...] = wg_ref[0].astype(BF16)
        wub_ref[...] = wu_ref[0].astype(BF16)
        wdb_ref[...] = wd_ref[0].astype(BF16)

    for k in range(1, MOE_TM // EXPERT_SUB + 1):
        @pl.when(used & (n_sub == k))
        def _(k=k):
            m = k * EXPERT_SUB
            x = x_ref[:m, :].astype(BF16)
            hg = _dot(x, wgb_ref[...])
            hu = _dot(x, wub_ref[...])
            hid = (_silu(hg) * hu).astype(BF16)
            y_ref[:m, :] = _dot(hid, wdb_ref[...])
            if m < MOE_TM:
                y_ref[m:, :] = jnp.zeros((MOE_TM - m, y_ref.shape[1]), F32)

    @pl.when(jnp.logical_not(used))
    def _():
        y_ref[...] = jnp.zeros_like(y_ref)


def _experts(buf, blk_e, nused, nsub, w_gate, w_up, w_down):
    R, D = buf.shape
    de = w_gate.shape[2]
    n_blocks = R // MOE_TM

    def blk(i, nused_ref):
        return jnp.minimum(i, nused_ref[0] - 1)

    row_map = lambda i, be, nu, ns: (blk(i, nu), 0)
    w_map = lambda i, be, nu, ns: (be[blk(i, nu)], 0, 0)
    return pl.pallas_call(
        _experts_kernel,
        grid_spec=pltpu.PrefetchScalarGridSpec(
            num_scalar_prefetch=3,
            grid=(n_blocks,),
            in_specs=[pl.BlockSpec((MOE_TM, D), row_map),
                      pl.BlockSpec((1, D, de), w_map),
                      pl.BlockSpec((1, D, de), w_map),
                      pl.BlockSpec((1, de, D), w_map)],
            out_specs=pl.BlockSpec((MOE_TM, D), lambda i, be, nu, ns: (i, 0)),
            scratch_shapes=[pltpu.VMEM((D, de), BF16), pltpu.VMEM((D, de), BF16), pltpu.VMEM((de, D), BF16)]),
        out_shape=jax.ShapeDtypeStruct((R, D), F32),
        compiler_params=_cparams(1),
        name="experts",
    )(blk_e, nused, nsub, buf, w_gate, w_up, w_down)


def _combine_kernel(npieces_ref, glob_ref, x2_ref, rti_ref, rtf_ref, gfin_ref, yb_ref, out_ref,
                    rows_ref, sem):
    tt = x2_ref.shape[0]
    sl = rows_ref.shape[1]
    i = pl.program_id(0)
    n_tiles = pl.num_programs(0)
    cur = lax.rem(i, 2)

    def piece_copy(buf_slot, local_row, global_row):
        return pltpu.make_async_copy(yb_ref.at[pl.ds(global_row, ROW_PIECE)],
                                     rows_ref.at[buf_slot, pl.ds(local_row, ROW_PIECE)], sem.at[buf_slot])

    def start_pieces(tile, buf_slot):
        _for_each_piece(npieces_ref, glob_ref, tile, lambda l, g: piece_copy(buf_slot, l, g).start())

    def wait_pieces(tile, buf_slot):
        _for_each_piece(npieces_ref, glob_ref, tile, lambda l, g: piece_copy(buf_slot, l, g).wait())

    @pl.when(i == 0)
    def _():
        rows_ref[...] = jnp.zeros_like(rows_ref)
        start_pieces(0, 0)

    @pl.when(i + 1 < n_tiles)
    def _():
        start_pieces(i + 1, 1 - cur)

    wait_pieces(i, cur)

    rti = rti_ref[...]
    rtf = rtf_ref[...]
    slot1 = rti[:, 2:3]
    slot2 = rti[:, 3:4]
    lane = lax.broadcasted_iota(jnp.int32, (tt, sl), 1)
    g = jnp.where(lane == slot1, rtf[:, 0:1], jnp.where(lane == slot2, rtf[:, 1:2], 0.0)).astype(BF16)
    y = x2_ref[...] + _dot(g, rows_ref[cur].astype(BF16))
    ms = jnp.mean(y * y, axis=-1, keepdims=True)
    out_ref[...] = y * lax.rsqrt(ms + EPS) * gfin_ref[...]


def _combine(x2, rti, rtf, g_final, yb, npieces, piece_glob):
    T, D = x2.shape
    tt = MIX_TS
    tok = lambda i, *_: (i, 0)
    return pl.pallas_call(
        _combine_kernel,
        grid_spec=pltpu.PrefetchScalarGridSpec(
            num_scalar_prefetch=2,
            grid=(T // tt,),
            in_specs=[pl.BlockSpec((tt, D), tok),
                      pl.BlockSpec((tt, LANES), tok),
                      pl.BlockSpec((tt, LANES), tok),
                      pl.BlockSpec((1, D), lambda i, *_: (0, 0)),
                      pl.BlockSpec(memory_space=pl.ANY)],
            out_specs=pl.BlockSpec((tt, D), tok),
            scratch_shapes=[pltpu.VMEM((2, MOE_SL, D), F32),
                            pltpu.SemaphoreType.DMA((2,))]),
        out_shape=jax.ShapeDtypeStruct((T, D), F32),
        compiler_params=_cparams(1),
        name="combine",
    )(npieces, piece_glob, x2, rti, rtf, g_final, yb)


def _pad_lanes(a, width=LANES):
    return jnp.pad(a, ((0, 0), (0, width - a.shape[1])))


def kernel(x, g_mix, w_in, b_if, conv_q, conv_k, g_head, w_pool, pool_scale, w_br_a, w_br_b, w_out,
           g_ffn, w_rg, b_rg, w_re, b_re, w_e_gate, w_e_up, w_e_down, g_final):
    B, S, D = x.shape
    T = B * S
    assert g_mix.shape[0] == 1, "single-layer block"
    assert S % MIX_TS == 0
    d_pool = w_br_a.shape[1]
    d_ml = w_br_b.shape[1]
    x2d = x.reshape(T, D)

    n_main = d_pool + 4 * d_ml
    w_l = w_in[0]
    w_main = w_l[:, :n_main].astype(BF16)
    w_if = w_l[:, n_main:n_main + 2 * N_HEADS]
    w_gates = w_l[:, n_main + 2 * N_HEADS:].astype(BF16)
    w_if_c = _pad_lanes(w_if).astype(BF16)
    w_if_t = jnp.pad(w_if.T, ((0, 16 - 2 * N_HEADS), (0, 0))).astype(BF16)
    w_r = _pad_lanes(jnp.concatenate([w_rg[0], w_re[0]], axis=1))
    w_r_hi = w_r.astype(BF16)
    w_r_lo = (w_r - w_r_hi.astype(F32)).astype(BF16)
    params = {
        "b_if": _pad_lanes(b_if[0][None, :]),
        "b_if_t": jnp.pad(b_if[0][:, None], ((0, 16 - 2 * N_HEADS), (0, 0))),
        "conv_q": conv_q[0], "conv_k": conv_k[0],
        "g_head": g_head[0][None, :],
        "w_pool": w_pool[0].astype(BF16),
        "pool_scale": pool_scale[0][None, :],
        "w_br_a": w_br_a[0].astype(BF16), "w_br_b": w_br_b[0].astype(BF16),
        "w_out": w_out[0].astype(BF16),
        "g_ffn": g_ffn[0][None, :],
        "w_r_hi": w_r_hi, "w_r_lo": w_r_lo,
        "b_r": _pad_lanes(jnp.concatenate([b_rg[0], b_re[0]])[None, :]),
    }

    x2, xn2, rti, rtf, tstat, srow = _mixer(x2d, g_mix[0][None, :], w_main, w_gates, w_if_c, w_if_t, params, B, S)

    n_tiles = T // MIX_TS
    pcs = tstat.reshape(n_tiles, 8, LANES)[:, 0, ROUTER_LANE0:ROUTER_LANE0 + N_EXPERTS]
    piece_loc = jnp.cumsum(pcs, axis=1) - pcs
    rows_e = jnp.sum(pcs, axis=0) * ROW_PIECE
    padded = (rows_e + MOE_TM - 1) // MOE_TM * MOE_TM
    pend = jnp.cumsum(padded)
    poff = pend - padded
    piece_glob = poff[None, :] // ROW_PIECE + jnp.cumsum(pcs, axis=0) - pcs
    n_rows = n_tiles * MOE_SL + N_EXPERTS * MOE_TM
    n_blocks = n_rows // MOE_TM
    nused = (pend[-1:] // MOE_TM).astype(jnp.int32)
    blk_ids = jnp.arange(n_blocks, dtype=jnp.int32)
    blk_e = jnp.sum((pend[None, :] <= blk_ids[:, None] * MOE_TM).astype(jnp.int32), axis=1)
    blk_e = jnp.minimum(blk_e, N_EXPERTS - 1)
    zflag = (((blk_ids + 1) * MOE_TM == pend[blk_e]) | (blk_ids >= nused[0])).astype(jnp.int32)
    p_ids = jnp.arange(PIECES_PER_TILE, dtype=jnp.int32)
    piece_end = piece_loc + pcs
    e_of_p = jnp.minimum(jnp.sum((piece_end[:, None, :] <= p_ids[None, :, None]).astype(jnp.int32), axis=2),
                         N_EXPERTS - 1)
    e_ids = jnp.arange(N_EXPERTS, dtype=jnp.int32)
    shift = jnp.sum(jnp.where(e_of_p[:, :, None] == e_ids[None, None, :], (piece_glob - piece_loc)[:, None, :], 0),
                    axis=2)
    glob_of_p = (shift + p_ids[None, :]).astype(jnp.int32)
    glob_of_p = glob_of_p.reshape(n_tiles * PIECES_PER_TILE)
    npieces = jnp.sum(pcs, axis=1).astype(jnp.int32)

    buf = _dispatch(xn2, srow, npieces, glob_of_p, zflag, n_rows)
    data_end = jnp.sum(jnp.where(blk_e[:, None] == jnp.arange(N_EXPERTS)[None, :], (poff + rows_e)[None, :], 0), axis=1)
    rows_in_blk = jnp.clip(data_end - blk_ids * MOE_TM, 0, MOE_TM)
    nsub = ((rows_in_blk + EXPERT_SUB - 1) // EXPERT_SUB).astype(jnp.int32)
    yb = _experts(buf, blk_e, nused, nsub, w_e_gate[0], w_e_up[0], w_e_down[0])
    out = _combine(x2, rti, rtf, g_final[None, :], yb, npieces, glob_of_p)
    return out.reshape(B, S, D)
```

```python
import functools

import numpy as np
import jax
import jax.numpy as jnp
from jax import lax
from jax.experimental import pallas as pl
from jax.experimental.pallas import tpu as pltpu

F32 = jnp.float32
BF16 = jnp.bfloat16

CHUNK = 64
POOL_WINDOWS = (2, 4, 8, 16)
N_HEADS = 4
CONV_K = 4
N_GROUPS = 4
EXPERTS_PER_GROUP = 8
N_EXPERTS = N_GROUPS * EXPERTS_PER_GROUP
TOP_K = 2
EPS = 1e-6

LANES = 128
HALO = 16
ROUTER_LANE0 = N_GROUPS

INPROJ_TN = 256
MIX_TS = 256
MOE_TM = 512
ROW_PIECE = 16
MOE_SL = TOP_K * MIX_TS + N_EXPERTS * ROW_PIECE
PIECES_PER_TILE = MOE_SL // ROW_PIECE
PIECE_UNROLL = 4
EXPERT_SUB = 128
SLOT_RADIX = 16
VMEM_LIMIT = 56 * 1024 * 1024


def _cparams(n_axes):
    return pltpu.CompilerParams(dimension_semantics=("arbitrary",) * n_axes,
                                vmem_limit_bytes=VMEM_LIMIT)


def _sigmoid(v):
    return 0.5 * jnp.tanh(0.5 * v) + 0.5


def _silu(v):
    return v * _sigmoid(v)


def _log_sigmoid(v):
    return jnp.minimum(v, 0.0) - jnp.log1p(jnp.exp(-jnp.abs(v)))


def _split3(v):
    hi = v.astype(BF16)
    r1 = v - hi.astype(F32)
    mid = r1.astype(BF16)
    lo = (r1 - mid.astype(F32)).astype(BF16)
    return hi, mid, lo


def _dot(a, b):
    return jnp.dot(a, b, preferred_element_type=F32)


def _dot_nt(a, b):
    return lax.dot_general(a, b, (((1,), (1,)), ((), ())), preferred_element_type=F32)


def _dot_tn(a, b):
    return lax.dot_general(a, b, (((0,), (0,)), ((), ())), preferred_element_type=F32)


def _inproj_steps(x_ref, g_ref, w_refs, z_refs, xn_ref):
    wm_ref, wg_ref, wif_ref, wift_ref = w_refs
    zm_ref, zg_ref, zif_ref, zift_ref = z_refs

    def norm():
        x = x_ref[...]
        ms = jnp.mean(x * x, axis=-1, keepdims=True)
        xn_ref[...] = (x * lax.rsqrt(ms + EPS) * g_ref[...]).astype(BF16)

    def block(w_ref, z_ref, c0):
        def run():
            cols = slice(c0, c0 + INPROJ_TN)
            z_ref[:, cols] = _dot(xn_ref[...], w_ref[:, cols]).astype(BF16)
        return run

    def gates():
        zif_ref[...] = _dot(xn_ref[...], wif_ref[...])
        zift_ref[...] = _dot_nt(wift_ref[...], xn_ref[...])

    steps = [norm, gates]
    steps += [block(wm_ref, zm_ref, c0) for c0 in range(0, zm_ref.shape[1], INPROJ_TN)]
    steps += [block(wg_ref, zg_ref, c0) for c0 in range(0, zg_ref.shape[1], INPROJ_TN)]
    return steps


def _mixer_kernel(x_ref, xnext_ref, gmix_ref, wm_ref, wg_ref, wif_ref, wift_ref,
                  bif_ref, bift_ref, convq_ref, convk_ref, ghead_ref, wpool_ref, pscale_ref,
                  wa_ref, wb_ref, wo_ref, gffn_ref, wrh_ref, wrl_ref, br_ref,
                  tric_ref, trir_ref, stri_ref, ut_ref, sel_ref,
                  x2_ref, xn2_ref, rti_ref, rtf_ref, tstat_ref, srow_ref,
                  zm_ref, zg_ref, zif_ref, zift_ref, zm_nxt, zg_nxt, zif_nxt, zift_nxt, xn_ref,
                  ext_ref, q_ref, k_ref, h_ref, pool_ref, cst_ref, mst_ref):
    ts = x_ref.shape[0]
    d_pool = wa_ref.shape[0]
    d_ml = wb_ref.shape[0]
    dh = d_ml // N_HEADS
    n_chunks = ts // CHUNK
    j = pl.program_id(1)
    w_in_refs = (wm_ref, wg_ref, wif_ref, wift_ref)
    z_cur = (zm_ref, zg_ref, zif_ref, zift_ref)
    z_nxt = (zm_nxt, zg_nxt, zif_nxt, zift_nxt)
    first = (j == 0) & (pl.program_id(0) == 0)

    @pl.when(first)
    def _():
        for step in _inproj_steps(x_ref, gmix_ref, w_in_refs, z_cur, xn_ref):
            step()

    @pl.when(jnp.logical_not(first))
    def _():
        for dst, src in zip(z_cur, z_nxt):
            dst[...] = src[...]

    @pl.when(j == 0)
    def _():
        ext_ref[:, :HALO, :] = jnp.zeros((ext_ref.shape[0], HALO, LANES), F32)
        cst_ref[...] = jnp.zeros_like(cst_ref)
        mst_ref[...] = jnp.zeros_like(mst_ref)

    pending = _inproj_steps(xnext_ref, gmix_ref, w_in_refs, z_nxt, xn_ref)

    def project_some(n=1):
        for _ in range(min(n, len(pending))):
            pending.pop(0)()

    project_some(2)

    row = lax.broadcasted_iota(jnp.int32, (ts, LANES), 0)
    pos1 = (row + j * ts + 1).astype(F32)

    def history(cg):
        cur = zm_ref[:, cg * LANES:(cg + 1) * LANES].astype(F32)
        ext_ref[cg, HALO:, :] = cur
        return cur, lambda s: ext_ref[cg, HALO - s:HALO - s + ts, :]

    def keep_history(cg, cur):
        ext_ref[cg, :HALO, :] = cur[ts - HALO:, :]

    n_pool_groups = d_pool // LANES
    for g in range(n_pool_groups):
        w = POOL_WINDOWS[g]
        cur, shifted = history(g)
        win = cur
        for s in range(1, w):
            win = win + shifted(s)
        keep_history(g, cur)
        cnt = jnp.minimum(pos1, float(w))
        d = win / cnt - cur
        y = _dot(d.astype(BF16), wpool_ref[g]) * pscale_ref[:, g * LANES:(g + 1) * LANES]
        pool_ref[:, g * LANES:(g + 1) * LANES] = y.astype(BF16)
        project_some()

    n_ml_groups = d_ml // LANES
    for which, (cw_ref, dst_ref, scale) in enumerate(((convq_ref, q_ref, 1.0), (convk_ref, k_ref, dh ** -0.5))):
        for g in range(n_ml_groups):
            cols = slice(g * LANES, (g + 1) * LANES)
            cg = n_pool_groups + which * n_ml_groups + g
            cur, shifted = history(cg)
            acc = cur * cw_ref[CONV_K - 1:CONV_K, cols]
            for sft in range(1, CONV_K):
                acc = acc + shifted(sft) * cw_ref[CONV_K - 1 - sft:CONV_K - sft, cols]
            keep_history(cg, cur)
            dst_ref[:, cols] = (_silu(acc) * scale).astype(BF16)
        project_some()

    zc = zif_ref[...] + bif_ref[...]
    lf_c = _log_sigmoid(zc)
    bc = sum(_dot(tric_ref[...], p) for p in _split3(lf_c))
    zr = zift_ref[...] + bift_ref[...]
    lf_r = _log_sigmoid(zr)
    br = sum(_dot(p, trir_ref[...]) for p in _split3(lf_r))

    ti = lax.broadcasted_iota(jnp.int32, (CHUNK, CHUNK), 0)
    si = lax.broadcasted_iota(jnp.int32, (CHUNK, CHUNK), 1)
    causal = si <= ti
    ones_blk = jnp.ones((CHUNK, dh), BF16)
    v0 = d_pool + 2 * d_ml
    ig_rep = [jnp.broadcast_to(zc[:, h:h + 1], (ts, dh)) for h in range(N_HEADS)]
    bt_rep = [jnp.broadcast_to(bc[:, N_HEADS + h:N_HEADS + h + 1], (ts, dh)) for h in range(N_HEADS)]

    m_state = [mst_ref[h:h + 1, :] for h in range(N_HEADS)]
    c_state = [cst_ref[h] for h in range(N_HEADS)]
    for c in range(n_chunks):
        rs = slice(c * CHUNK, (c + 1) * CHUNK)
        for h in range(N_HEADS):
            hs = slice(h * dh, (h + 1) * dh)
            q = q_ref[rs, hs]
            k = k_ref[rs, hs]
            v_aug = jnp.concatenate([zm_ref[rs, v0 + h * dh:v0 + (h + 1) * dh], ones_blk], axis=-1)
            bt = bt_rep[h][rs, :]
            igc = ig_rep[h][rs, :]
            r_row = zr[h:h + 1, rs] - br[N_HEADS + h:N_HEADS + h + 1, rs]
            m_prev = m_state[h]
            c_prev = c_state[h]

            dmat = jnp.where(causal, bt[:, :CHUNK] + r_row, -jnp.inf)
            m_intra = jnp.max(dmat, axis=-1, keepdims=True)
            inter = bt + m_prev
            m_t = jnp.maximum(inter, m_intra)
            w_inter = jnp.exp(inter - m_t)
            smat = _dot_nt(q, k) * jnp.exp(dmat - m_t[:, :CHUNK])
            qc = _dot(q, c_prev.astype(BF16))
            sv = _dot(smat.astype(BF16), v_aug)
            nq = w_inter * qc[:, dh:] + sv[:, dh:]
            den = jnp.maximum(jnp.abs(nq), jnp.exp(-m_t))
            h_ref[rs, hs] = (w_inter * qc[:, :dh] + sv[:, :dh]) / den

            b_last = bt[CHUNK - 1:CHUNK, :]
            a_log = b_last - bt + igc
            a_max = jnp.max(a_log, axis=0, keepdims=True)
            m_new = jnp.maximum(b_last + m_prev, a_max)
            kw = (k.astype(F32) * jnp.exp(a_log - m_new)).astype(BF16)
            decay = jnp.exp(b_last + m_prev - m_new)
            c_state[h] = jnp.concatenate([decay, decay], axis=-1) * c_prev + _dot_tn(kw, v_aug)
            m_state[h] = m_new
            if h % 2 == 1:
                project_some()
    for h in range(N_HEADS):
        cst_ref[h] = c_state[h]
        mst_ref[h:h + 1, :] = m_state[h]

    o0 = v0 + d_ml
    for h in range(N_HEADS):
        hs = slice(h * dh, (h + 1) * dh)
        hv = h_ref[:, hs]
        mu = jnp.mean(hv, axis=-1, keepdims=True)
        hc = hv - mu
        var = jnp.mean(hc * hc, axis=-1, keepdims=True)
        hn = hc * lax.rsqrt(var + EPS) * ghead_ref[:, hs]
        og = _sigmoid(zm_ref[:, o0 + h * dh:o0 + (h + 1) * dh].astype(F32))
        q_ref[:, hs] = (og * hn).astype(BF16)
    y_a = _dot(pool_ref[...], wa_ref[...])
    y_b = _dot(q_ref[...], wb_ref[...])
    d_model = x_ref.shape[1]
    ga = _sigmoid(zg_ref[:, :d_model].astype(F32))
    gb = _sigmoid(zg_ref[:, d_model:].astype(F32))
    merged = (ga * y_a + gb * y_b).astype(BF16)
    x2 = x_ref[...] + _dot(merged, wo_ref[...])
    x2_ref[...] = x2

    ms = jnp.mean(x2 * x2, axis=-1, keepdims=True)
    xn2 = x2 * lax.rsqrt(ms + EPS) * gffn_ref[...]
    xh = xn2.astype(BF16)
    xn2_ref[...] = xh
    xl = (xn2 - xh.astype(F32)).astype(BF16)
    lg = _dot(xh, wrh_ref[...]) + _dot(xl, wrh_ref[...]) + _dot(xh, wrl_ref[...]) + br_ref[...]

    lane = lax.broadcasted_iota(jnp.int32, (ts, LANES), 1)
    lanef = lane.astype(F32)
    big = float(4 * LANES)
    gl = jnp.where(lane < N_GROUPS, lg, -jnp.inf)
    gmax = jnp.max(gl, axis=-1, keepdims=True)
    g_sel = jnp.min(jnp.where(gl == gmax, lanef, big), axis=-1, keepdims=True)
    p_g = 1.0 / jnp.sum(jnp.exp(gl - gmax), axis=-1, keepdims=True)
    project_some()
    lo = ROUTER_LANE0 + EXPERTS_PER_GROUP * g_sel
    el = jnp.where((lanef >= lo) & (lanef < lo + EXPERTS_PER_GROUP), lg, -jnp.inf)
    m1 = jnp.max(el, axis=-1, keepdims=True)
    i1 = jnp.min(jnp.where(el == m1, lanef, big), axis=-1, keepdims=True)
    project_some()
    el2 = jnp.where(lanef == i1, -jnp.inf, el)
    m2 = jnp.max(el2, axis=-1, keepdims=True)
    i2 = jnp.min(jnp.where(el2 == m2, lanef, big), axis=-1, keepdims=True)
    project_some()
    e2x = jnp.exp(m2 - m1)
    gate1 = p_g / (1.0 + e2x)
    gate2 = p_g * e2x / (1.0 + e2x)

    oh1 = lanef == i1
    oh2 = lanef == i2
    ohs = jnp.where(oh1 | oh2, 1.0, 0.0)
    n_loc = jnp.sum(ohs, axis=0, keepdims=True)
    pieces = jnp.floor((n_loc + (ROW_PIECE - 1.0)) * (1.0 / ROW_PIECE))
    piece_off = _dot(jnp.broadcast_to(pieces, (8, LANES)).astype(BF16), ut_ref[...])[0:1, :]
    base = _dot(stri_ref[...], ohs.astype(BF16)) + ROW_PIECE * piece_off
    slot1 = jnp.sum(jnp.where(oh1, base, 0.0), axis=-1, keepdims=True)
    slot2 = jnp.sum(jnp.where(oh2, base, 0.0), axis=-1, keepdims=True)
    tstat_ref[...] = jnp.broadcast_to(pieces, tstat_ref.shape).astype(jnp.int32)

    rti = jnp.where(lane == 0, i1 - ROUTER_LANE0,
                    jnp.where(lane == 1, i2 - ROUTER_LANE0,
                              jnp.where(lane == 2, slot1, jnp.where(lane == 3, slot2, 0.0))))
    rti_ref[...] = rti.astype(jnp.int32)
    rtf_ref[...] = jnp.where(lane == 0, gate1, jnp.where(lane == 1, gate2, 0.0))
    h1 = jnp.floor(slot1 * (1.0 / SLOT_RADIX))
    h2 = jnp.floor(slot2 * (1.0 / SLOT_RADIX))
    parts = jnp.where(lane == 0, h1, jnp.where(lane == 1, slot1 - SLOT_RADIX * h1,
                      jnp.where(lane == 2, h2, jnp.where(lane == 3, slot2 - SLOT_RADIX * h2, 0.0))))
    srow_ref[...] = _dot_nt(sel_ref[...], parts.astype(BF16))
    project_some(len(pending))


def _mixer(x2d, g_mix, w_main, w_gates, w_if_c, w_if_t, params, batch, seq):
    T, D = x2d.shape
    ts = min(MIX_TS, seq)
    nts = seq // ts
    d_pool = params["w_br_a"].shape[0]
    d_ml = params["w_br_b"].shape[0]
    dh = d_ml // N_HEADS

    idx = np.arange(ts)
    same_chunk = (idx[:, None] // CHUNK) == (idx[None, :] // CHUNK)
    tri_c = jnp.asarray((idx[None, :] <= idx[:, None]) & same_chunk, BF16)
    tri_r = jnp.asarray((idx[:, None] <= idx[None, :]) & same_chunk, BF16)
    stri = jnp.asarray(idx[None, :] < idx[:, None], BF16)
    lane_idx = np.arange(LANES)
    ut = jnp.asarray(lane_idx[:, None] < lane_idx[None, :], BF16)
    sel = jnp.asarray(np.arange(8)[:, None] == lane_idx[None, :], BF16)

    tok = lambda b, j: (b * nts + j, 0)
    tok_t = lambda b, j: (0, b * nts + j)
    c2 = lambda b, j: (0, 0)
    c3 = lambda b, j: (0, 0, 0)
    full = lambda a: pl.BlockSpec(a.shape, c2 if a.ndim == 2 else c3)
    consts = [params[n] for n in ("b_if", "b_if_t", "conv_q", "conv_k", "g_head", "w_pool", "pool_scale",
                                  "w_br_a", "w_br_b", "w_out", "g_ffn", "w_r_hi", "w_r_lo", "b_r")]
    consts = [g_mix, w_main, w_gates, w_if_c, w_if_t] + consts + [tri_c, tri_r, stri, ut, sel]
    n_tiles = batch * nts
    nm, ng = w_main.shape[1], w_gates.shape[1]
    tok_next = lambda b, j: (jnp.minimum(b * nts + j + 1, n_tiles - 1), 0)
    z_scratch = [pltpu.VMEM((ts, nm), BF16), pltpu.VMEM((ts, ng), BF16),
                 pltpu.VMEM((ts, LANES), F32), pltpu.VMEM((16, ts), F32)]
    return pl.pallas_call(
        _mixer_kernel,
        grid=(batch, nts),
        in_specs=[pl.BlockSpec((ts, D), tok),
                  pl.BlockSpec((ts, D), tok_next)] + [full(a) for a in consts],
        out_specs=[pl.BlockSpec((ts, D), tok),
                   pl.BlockSpec((ts, D), tok),
                   pl.BlockSpec((ts, LANES), tok),
                   pl.BlockSpec((ts, LANES), tok),
                   pl.BlockSpec((8, LANES), tok),
                   pl.BlockSpec((8, ts), tok_t)],
        out_shape=[jax.ShapeDtypeStruct((T, D), F32),
                   jax.ShapeDtypeStruct((T, D), BF16),
                   jax.ShapeDtypeStruct((T, LANES), jnp.int32),
                   jax.ShapeDtypeStruct((T, LANES), F32),
                   jax.ShapeDtypeStruct((n_tiles * 8, LANES), jnp.int32),
                   jax.ShapeDtypeStruct((8, T), F32)],
        scratch_shapes=z_scratch + z_scratch + [
                        pltpu.VMEM((ts, D), BF16),
                        pltpu.VMEM(((d_pool + 2 * d_ml) // LANES, HALO + ts, LANES), F32),
                        pltpu.VMEM((ts, d_ml), BF16),
                        pltpu.VMEM((ts, d_ml), BF16),
                        pltpu.VMEM((ts, d_ml), F32),
                        pltpu.VMEM((ts, d_pool), BF16),
                        pltpu.VMEM((N_HEADS, dh, 2 * dh), F32),
                        pltpu.VMEM((8, LANES), F32)],
        compiler_params=_cparams(2),
        name="mixer",
    )(x2d, x2d, *consts)


def _for_each_piece(npieces_ref, glob_ref, tile, fn):
    base = tile * PIECES_PER_TILE
    n = npieces_ref[tile]

    def one(p):
        fn(pl.multiple_of(p * ROW_PIECE, ROW_PIECE), pl.multiple_of(glob_ref[base + p] * ROW_PIECE, ROW_PIECE))

    def group(g, carry):
        for u in range(PIECE_UNROLL):
            one(g * PIECE_UNROLL + u)
        return carry

    n_groups = lax.div(n, jnp.int32(PIECE_UNROLL))
    lax.fori_loop(0, n_groups, group, 0)
    for u in range(PIECE_UNROLL - 1):
        @pl.when(n_groups * PIECE_UNROLL + u < n)
        def _():
            one(n_groups * PIECE_UNROLL + u)


def _dispatch_kernel(npieces_ref, glob_ref, zflag_ref,
                     xn_ref, srow_ref, buf_ref, rows_ref, zeros_ref, sem, zsem):
    tt = xn_ref.shape[0]
    sl = rows_ref.shape[1]
    n_blocks = buf_ref.shape[0] // MOE_TM
    i = pl.program_id(0)
    n_tiles = pl.num_programs(0)
    cur = lax.rem(i, 2)

    @pl.when(i == 0)
    def _():
        zeros_ref[...] = jnp.zeros_like(zeros_ref)

        def zero_copy(b):
            return pltpu.make_async_copy(zeros_ref, buf_ref.at[pl.ds(b * MOE_TM, MOE_TM)], zsem)

        def zero_start(b, carry):
            @pl.when(zflag_ref[b] > 0)
            def _():
                zero_copy(b).start()
            return carry

        def zero_wait(b, carry):
            @pl.when(zflag_ref[b] > 0)
            def _():
                zero_copy(b).wait()
            return carry

        lax.fori_loop(0, n_blocks, zero_start, 0)
        lax.fori_loop(0, n_blocks, zero_wait, 0)

    def piece_copy(buf_slot, local_row, global_row):
        return pltpu.make_async_copy(rows_ref.at[buf_slot, pl.ds(local_row, ROW_PIECE)],
                                     buf_ref.at[pl.ds(global_row, ROW_PIECE)], sem.at[buf_slot])

    def start_pieces(tile, buf_slot):
        _for_each_piece(npieces_ref, glob_ref, tile, lambda l, g: piece_copy(buf_slot, l, g).start())

    def wait_pieces(tile, buf_slot):
        _for_each_piece(npieces_ref, glob_ref, tile, lambda l, g: piece_copy(buf_slot, l, g).wait())

    @pl.when(i >= 2)
    def _():
        wait_pieces(i - 2, cur)

    sr = srow_ref[...]
    slot1 = SLOT_RADIX * sr[0:1, :] + sr[1:2, :]
    slot2 = SLOT_RADIX * sr[2:3, :] + sr[3:4, :]
    r = lax.broadcasted_iota(jnp.int32, (sl, tt), 0).astype(F32)
    sel = jnp.where((r == slot1) | (r == slot2), 1.0, 0.0).astype(BF16)
    rows_ref[cur] = _dot(sel, xn_ref[...]).astype(BF16)
    start_pieces(i, cur)

    @pl.when(i == n_tiles - 1)
    def _():
        @pl.when(i >= 1)
        def _():
            wait_pieces(i - 1, 1 - cur)
        wait_pieces(i, cur)


def _dispatch(xn2, srow, npieces, piece_glob, zflag, n_rows):
    T, D = xn2.shape
    tt = MIX_TS
    return pl.pallas_call(
        _dispatch_kernel,
        grid_spec=pltpu.PrefetchScalarGridSpec(
            num_scalar_prefetch=3,
            grid=(T // tt,),
            in_specs=[pl.BlockSpec((tt, D), lambda i, *_: (i, 0)),
                      pl.BlockSpec((8, tt), lambda i, *_: (0, i))],
            out_specs=pl.BlockSpec(memory_space=pl.ANY),
            scratch_shapes=[pltpu.VMEM((2, MOE_SL, D), BF16),
                            pltpu.VMEM((MOE_TM, D), BF16),
                            pltpu.SemaphoreType.DMA((2,)),
                            pltpu.SemaphoreType.DMA(())]),
        out_shape=jax.ShapeDtypeStruct((n_rows, D), BF16),
        compiler_params=_cparams(1),
        name="dispatch",
    )(npieces, piece_glob, zflag, xn2, srow)


def _experts_kernel(blk_e_ref, nused_ref, nsub_ref, x_ref, wg_ref, wu_ref, wd_ref, y_ref,
                    wgb_ref, wub_ref, wdb_ref):
    i = pl.program_id(0)
    used = i < nused_ref[0]
    n_sub = nsub_ref[i]

    @pl.when(used & ((i == 0) | (blk_e_ref[i] != blk_e_ref[jnp.maximum(i - 1, 0)])))
    def _():
        wgb_ref[...] = wg_ref[0].astype(BF16)
        wub_ref[...] = wu_ref[0].astype(BF16)
        wdb_ref[...] = wd_ref[0].astype(BF16)

    for k in range(1, MOE_TM // EXPERT_SUB + 1):
        @pl.when(used & (n_sub == k))
        def _(k=k):
            m = k * EXPERT_SUB
            x = x_ref[:m, :]
            hg = _dot(x, wgb_ref[...])
            hu = _dot(x, wub_ref[...])
            hid = (_silu(hg) * hu).astype(BF16)
            y_ref[:m, :] = _dot(hid, wdb_ref[...]).astype(BF16)
            if m < MOE_TM:
                y_ref[m:, :] = jnp.zeros((MOE_TM - m, y_ref.shape[1]), BF16)

    @pl.when(jnp.logical_not(used))
    def _():
        y_ref[...] = jnp.zeros_like(y_ref)


def _experts(buf, blk_e, nused, nsub, w_gate, w_up, w_down):
    R, D = buf.shape
    de = w_gate.shape[2]
    n_blocks = R // MOE_TM

    def blk(i, nused_ref):
        return jnp.minimum(i, nused_ref[0] - 1)

    row_map = lambda i, be, nu, ns: (blk(i, nu), 0)
    w_map = lambda i, be, nu, ns: (be[blk(i, nu)], 0, 0)
    return pl.pallas_call(
        _experts_kernel,
        grid_spec=pltpu.PrefetchScalarGridSpec(
            num_scalar_prefetch=3,
            grid=(n_blocks,),
            in_specs=[pl.BlockSpec((MOE_TM, D), row_map),
                      pl.BlockSpec((1, D, de), w_map),
                      pl.BlockSpec((1, D, de), w_map),
                      pl.BlockSpec((1, de, D), w_map)],
            out_specs=pl.BlockSpec((MOE_TM, D), lambda i, be, nu, ns: (i, 0)),
            scratch_shapes=[pltpu.VMEM((D, de), BF16), pltpu.VMEM((D, de), BF16), pltpu.VMEM((de, D), BF16)]),
        out_shape=jax.ShapeDtypeStruct((R, D), BF16),
        compiler_params=_cparams(1),
        name="experts",
    )(blk_e, nused, nsub, buf, w_gate, w_up, w_down)


def _combine_kernel(npieces_ref, glob_ref, x2_ref, rti_ref, rtf_ref, gfin_ref, yb_ref, out_ref,
                    rows_ref, sem):
    tt = x2_ref.shape[0]
    sl = rows_ref.shape[1]
    i = pl.program_id(0)
    n_tiles = pl.num_programs(0)
    cur = lax.rem(i, 2)

    def piece_copy(buf_slot, local_row, global_row):
        return pltpu.make_async_copy(yb_ref.at[pl.ds(global_row, ROW_PIECE)],
                                     rows_ref.at[buf_slot, pl.ds(local_row, ROW_PIECE)], sem.at[buf_slot])

    def start_pieces(tile, buf_slot):
        _for_each_piece(npieces_ref, glob_ref, tile, lambda l, g: piece_copy(buf_slot, l, g).start())

    def wait_pieces(tile, buf_slot):
        _for_each_piece(npieces_ref, glob_ref, tile, lambda l, g: piece_copy(buf_slot, l, g).wait())

    @pl.when(i == 0)
    def _():
        rows_ref[...] = jnp.zeros_like(rows_ref)
        start_pieces(0, 0)

    @pl.when(i + 1 < n_tiles)
    def _():
        start_pieces(i + 1, 1 - cur)

    wait_pieces(i, cur)

    rti = rti_ref[...]
    rtf = rtf_ref[...]
    slot1 = rti[:, 2:3]
    slot2 = rti[:, 3:4]
    lane = lax.broadcasted_iota(jnp.int32, (tt, sl), 1)
    g = jnp.where(lane == slot1, rtf[:, 0:1], jnp.where(lane == slot2, rtf[:, 1:2], 0.0)).astype(BF16)
    y = x2_ref[...] + _dot(g, rows_ref[cur])
    ms = jnp.mean(y * y, axis=-1, keepdims=True)
    out_ref[...] = y * lax.rsqrt(ms + EPS) * gfin_ref[...]


def _combine(x2, rti, rtf, g_final, yb, npieces, piece_glob):
    T, D = x2.shape
    tt = MIX_TS
    tok = lambda i, *_: (i, 0)
    return pl.pallas_call(
        _combine_kernel,
        grid_spec=pltpu.PrefetchScalarGridSpec(
            num_scalar_prefetch=2,
            grid=(T // tt,),
            in_specs=[pl.BlockSpec((tt, D), tok),
                      pl.BlockSpec((tt, LANES), tok),
                      pl.BlockSpec((tt, LANES), tok),
                      pl.BlockSpec((1, D), lambda i, *_: (0, 0)),
                      pl.BlockSpec(memory_space=pl.ANY)],
            out_specs=pl.BlockSpec((tt, D), tok),
            scratch_shapes=[pltpu.VMEM((2, MOE_SL, D), BF16),
                            pltpu.SemaphoreType.DMA((2,))]),
        out_shape=jax.ShapeDtypeStruct((T, D), F32),
        compiler_params=_cparams(1),
        name="combine",
    )(npieces, piece_glob, x2, rti, rtf, g_final, yb)


def _pad_lanes(a, width=LANES):
    return jnp.pad(a, ((0, 0), (0, width - a.shape[1])))


def kernel(x, g_mix, w_in, b_if, conv_q, conv_k, g_head, w_pool, pool_scale, w_br_a, w_br_b, w_out,
           g_ffn, w_rg, b_rg, w_re, b_re, w_e_gate, w_e_up, w_e_down, g_final):
    B, S, D = x.shape
    T = B * S
    assert g_mix.shape[0] == 1, "single-layer block"
    assert S % MIX_TS == 0
    d_pool = w_br_a.shape[1]
    d_ml = w_br_b.shape[1]
    x2d = x.reshape(T, D)

    n_main = d_pool + 4 * d_ml
    w_l = w_in[0]
    w_main = w_l[:, :n_main].astype(BF16)
    w_if = w_l[:, n_main:n_main + 2 * N_HEADS]
    w_gates = w_l[:, n_main + 2 * N_HEADS:].astype(BF16)
    w_if_c = _pad_lanes(w_if).astype(BF16)
    w_if_t = jnp.pad(w_if.T, ((0, 16 - 2 * N_HEADS), (0, 0))).astype(BF16)
    w_r = _pad_lanes(jnp.concatenate([w_rg[0], w_re[0]], axis=1))
    w_r_hi = w_r.astype(BF16)
    w_r_lo = (w_r - w_r_hi.astype(F32)).astype(BF16)
    params = {
        "b_if": _pad_lanes(b_if[0][None, :]),
        "b_if_t": jnp.pad(b_if[0][:, None], ((0, 16 - 2 * N_HEADS), (0, 0))),
        "conv_q": conv_q[0], "conv_k": conv_k[0],
        "g_head": g_head[0][None, :],
        "w_pool": w_pool[0].astype(BF16),
        "pool_scale": pool_scale[0][None, :],
        "w_br_a": w_br_a[0].astype(BF16), "w_br_b": w_br_b[0].astype(BF16),
        "w_out": w_out[0].astype(BF16),
        "g_ffn": g_ffn[0][None, :],
        "w_r_hi": w_r_hi, "w_r_lo": w_r_lo,
        "b_r": _pad_lanes(jnp.concatenate([b_rg[0], b_re[0]])[None, :]),
    }

    x2, xn2, rti, rtf, tstat, srow = _mixer(x2d, g_mix[0][None, :], w_main, w_gates, w_if_c, w_if_t, params, B, S)

    n_tiles = T // MIX_TS
    pcs = tstat.reshape(n_tiles, 8, LANES)[:, 0, ROUTER_LANE0:ROUTER_LANE0 + N_EXPERTS]
    piece_loc = jnp.cumsum(pcs, axis=1) - pcs
    rows_e = jnp.sum(pcs, axis=0) * ROW_PIECE
    padded = (rows_e + MOE_TM - 1) // MOE_TM * MOE_TM
    pend = jnp.cumsum(padded)
    poff = pend - padded
    piece_glob = poff[None, :] // ROW_PIECE + jnp.cumsum(pcs, axis=0) - pcs
    n_rows = n_tiles * MOE_SL + N_EXPERTS * MOE_TM
    n_blocks = n_rows // MOE_TM
    nused = (pend[-1:] // MOE_TM).astype(jnp.int32)
    blk_ids = jnp.arange(n_blocks, dtype=jnp.int32)
    blk_e = jnp.sum((pend[None, :] <= blk_ids[:, None] * MOE_TM).astype(jnp.int32), axis=1)
    blk_e = jnp.minimum(blk_e, N_EXPERTS - 1)
    zflag = (((blk_ids + 1) * MOE_TM == pend[blk_e]) | (blk_ids >= nused[0])).astype(jnp.int32)
    p_ids = jnp.arange(PIECES_PER_TILE, dtype=jnp.int32)
    piece_end = piece_loc + pcs
    e_of_p = jnp.minimum(jnp.sum((piece_end[:, None, :] <= p_ids[None, :, None]).astype(jnp.int32), axis=2),
                         N_EXPERTS - 1)
    e_ids = jnp.arange(N_EXPERTS, dtype=jnp.int32)
    shift = jnp.sum(jnp.where(e_of_p[:, :, None] == e_ids[None, None, :], (piece_glob - piece_loc)[:, None, :], 0),
                    axis=2)
    glob_of_p = (shift + p_ids[None, :]).astype(jnp.int32)
    glob_of_p = glob_of_p.reshape(n_tiles * PIECES_PER_TILE)
    npieces = jnp.sum(pcs, axis=1).astype(jnp.int32)

    buf = _dispatch(xn2, srow, npieces, glob_of_p, zflag, n_rows)
    data_end = jnp.sum(jnp.where(blk_e[:, None] == jnp.arange(N_EXPERTS)[None, :], (poff + rows_e)[None, :], 0), axis=1)
    rows_in_blk = jnp.clip(data_end - blk_ids * MOE_TM, 0, MOE_TM)
    nsub = ((rows_in_blk + EXPERT_SUB - 1) // EXPERT_SUB).astype(jnp.int32)
    yb = _experts(buf, blk_e, nused, nsub, w_e_gate[0], w_e_up[0], w_e_down[0])
    out = _combine(x2, rti, rtf, g_final[None, :], yb, npieces, glob_of_p)
    return out.reshape(B, S, D)
```

```python
import functools

import numpy as np
import jax
import jax.numpy as jnp
from jax import lax
from jax.experimental import pallas as pl
from jax.experimental.pallas import tpu as pltpu

F32 = jnp.float32
BF16 = jnp.bfloat16

CHUNK = 64
POOL_WINDOWS = (2, 4, 8, 16)
N_HEADS = 4
CONV_K = 4
N_GROUPS = 4
EXPERTS_PER_GROUP = 8
N_EXPERTS = N_GROUPS * EXPERTS_PER_GROUP
TOP_K = 2
EPS = 1e-6

LANES = 128
HALO = 16
ROUTER_LANE0 = N_GROUPS

INPROJ_TN = 256
MIX_TS = 256
MOE_TM = 512
ROW_PIECE = 16
MOE_SL = TOP_K * MIX_TS + N_EXPERTS * ROW_PIECE
PIECES_PER_TILE = MOE_SL // ROW_PIECE
PIECE_UNROLL = 4
EXPERT_SUB = 128
SLOT_RADIX = 16
VMEM_LIMIT = 56 * 1024 * 1024


def _cparams(n_axes):
    return pltpu.CompilerParams(dimension_semantics=("arbitrary",) * n_axes,
                                vmem_limit_bytes=VMEM_LIMIT)


def _sigmoid(v):
    return 0.5 * jnp.tanh(0.5 * v) + 0.5


def _silu(v):
    return v * _sigmoid(v)


def _log_sigmoid(v):
    return jnp.minimum(v, 0.0) - jnp.log1p(jnp.exp(-jnp.abs(v)))


def _split3(v):
    hi = v.astype(BF16)
    r1 = v - hi.astype(F32)
    mid = r1.astype(BF16)
    lo = (r1 - mid.astype(F32)).astype(BF16)
    return hi, mid, lo


def _dot(a, b):
    return jnp.dot(a, b, preferred_element_type=F32)


def _dot_nt(a, b):
    return lax.dot_general(a, b, (((1,), (1,)), ((), ())), preferred_element_type=F32)


def _dot_tn(a, b):
    return lax.dot_general(a, b, (((0,), (0,)), ((), ())), preferred_element_type=F32)


def _inproj_steps(x_ref, g_ref, w_refs, z_refs, xn_ref):
    wm_ref, wg_ref, wif_ref, wift_ref = w_refs
    zm_ref, zg_ref, zif_ref, zift_ref = z_refs

    def norm():
        x = x_ref[...]
        ms = jnp.mean(x * x, axis=-1, keepdims=True)
        xn_ref[...] = (x * lax.rsqrt(ms + EPS) * g_ref[...]).astype(BF16)

    def block(w_ref, z_ref, c0):
        def run():
            cols = slice(c0, c0 + INPROJ_TN)
            z_ref[:, cols] = _dot(xn_ref[...], w_ref[:, cols]).astype(BF16)
        return run

    def gates():
        zif_ref[...] = _dot(xn_ref[...], wif_ref[...])
        zift_ref[...] = _dot_nt(wift_ref[...], xn_ref[...])

    steps = [norm, gates]
    steps += [block(wm_ref, zm_ref, c0) for c0 in range(0, zm_ref.shape[1], INPROJ_TN)]
    steps += [block(wg_ref, zg_ref, c0) for c0 in range(0, zg_ref.shape[1], INPROJ_TN)]
    return steps


def _mixer_kernel(x_ref, xnext_ref, gmix_ref, wm_ref, wg_ref, wif_ref, wift_ref,
                  bif_ref, bift_ref, convq_ref, convk_ref, ghead_ref, wpool_ref, pscale_ref,
                  wa_ref, wb_ref, wo_ref, gffn_ref, wrh_ref, wrl_ref, br_ref,
                  tric_ref, trir_ref, stri_ref, ut_ref, sel_ref,
                  x2_ref, xn2_ref, rti_ref, rtf_ref, tstat_ref, srow_ref,
                  zm_ref, zg_ref, zif_ref, zift_ref, zm_nxt, zg_nxt, zif_nxt, zift_nxt, xn_ref,
                  ext_ref, q_ref, k_ref, h_ref, pool_ref, cst_ref, mst_ref):
    ts = x_ref.shape[0]
    d_pool = wa_ref.shape[0]
    d_ml = wb_ref.shape[0]
    dh = d_ml // N_HEADS
    n_chunks = ts // CHUNK
    j = pl.program_id(1)
    w_in_refs = (wm_ref, wg_ref, wif_ref, wift_ref)
    z_cur = (zm_ref, zg_ref, zif_ref, zift_ref)
    z_nxt = (zm_nxt, zg_nxt, zif_nxt, zift_nxt)
    first = (j == 0) & (pl.program_id(0) == 0)

    @pl.when(first)
    def _():
        for step in _inproj_steps(x_ref, gmix_ref, w_in_refs, z_cur, xn_ref):
            step()

    @pl.when(jnp.logical_not(first))
    def _():
        for dst, src in zip(z_cur, z_nxt):
            dst[...] = src[...]

    @pl.when(j == 0)
    def _():
        ext_ref[:, :HALO, :] = jnp.zeros((ext_ref.shape[0], HALO, LANES), F32)
        cst_ref[...] = jnp.zeros_like(cst_ref)
        mst_ref[...] = jnp.zeros_like(mst_ref)

    pending = _inproj_steps(xnext_ref, gmix_ref, w_in_refs, z_nxt, xn_ref)

    def project_some(n=1):
        for _ in range(min(n, len(pending))):
            pending.pop(0)()

    project_some(2)

    row = lax.broadcasted_iota(jnp.int32, (ts, LANES), 0)
    pos1 = (row + j * ts + 1).astype(F32)

    def history(cg):
        cur = zm_ref[:, cg * LANES:(cg + 1) * LANES].astype(F32)
        ext_ref[cg, HALO:, :] = cur
        return cur, lambda s: ext_ref[cg, HALO - s:HALO - s + ts, :]

    def keep_history(cg, cur):
        ext_ref[cg, :HALO, :] = cur[ts - HALO:, :]

    n_pool_groups = d_pool // LANES
    for g in range(n_pool_groups):
        w = POOL_WINDOWS[g]
        cur, shifted = history(g)
        win = cur
        for s in range(1, w):
            win = win + shifted(s)
        keep_history(g, cur)
        cnt = jnp.minimum(pos1, float(w))
        d = win / cnt - cur
        y = _dot(d.astype(BF16), wpool_ref[g]) * pscale_ref[:, g * LANES:(g + 1) * LANES]
        pool_ref[:, g * LANES:(g + 1) * LANES] = y.astype(BF16)
        project_some()

    n_ml_groups = d_ml // LANES
    for which, (cw_ref, dst_ref, scale) in enumerate(((convq_ref, q_ref, 1.0), (convk_ref, k_ref, dh ** -0.5))):
        for g in range(n_ml_groups):
            cols = slice(g * LANES, (g + 1) * LANES)
            cg = n_pool_groups + which * n_ml_groups + g
            cur, shifted = history(cg)
            acc = cur * cw_ref[CONV_K - 1:CONV_K, cols]
            for sft in range(1, CONV_K):
                acc = acc + shifted(sft) * cw_ref[CONV_K - 1 - sft:CONV_K - sft, cols]
            keep_history(cg, cur)
            dst_ref[:, cols] = (_silu(acc) * scale).astype(BF16)
        project_some()

    zc = zif_ref[...] + bif_ref[...]
    lf_c = _log_sigmoid(zc)
    bc = sum(_dot(tric_ref[...], p) for p in _split3(lf_c))
    zr = zift_ref[...] + bift_ref[...]
    lf_r = _log_sigmoid(zr)
    br = sum(_dot(p, trir_ref[...]) for p in _split3(lf_r))

    ti = lax.broadcasted_iota(jnp.int32, (CHUNK, CHUNK), 0)
    si = lax.broadcasted_iota(jnp.int32, (CHUNK, CHUNK), 1)
    causal = si <= ti
    ones_blk = jnp.ones((CHUNK, dh), BF16)
    v0 = d_pool + 2 * d_ml
    ig_rep = [jnp.broadcast_to(zc[:, h:h + 1], (ts, dh)) for h in range(N_HEADS)]
    bt_rep = [jnp.broadcast_to(bc[:, N_HEADS + h:N_HEADS + h + 1], (ts, dh)) for h in range(N_HEADS)]

    m_state = [mst_ref[h:h + 1, :] for h in range(N_HEADS)]
    c_state = [cst_ref[h] for h in range(N_HEADS)]
    def stage_scores(c):
        rs = slice(c * CHUNK, (c + 1) * CHUNK)
        out = []
        for h in range(N_HEADS):
            hs = slice(h * dh, (h + 1) * dh)
            q = q_ref[rs, hs]
            k = k_ref[rs, hs]
            bt = bt_rep[h][rs, :]
            r_row = zr[h:h + 1, rs] - br[N_HEADS + h:N_HEADS + h + 1, rs]
            dmat = jnp.where(causal, bt[:, :CHUNK] + r_row, -jnp.inf)
            out.append(dict(q=q, k=k, bt=bt, dmat=dmat, qk=_dot_nt(q, k),
                            m_intra=jnp.max(dmat, axis=-1, keepdims=True)))
        return out

    def stage_state(c, st):
        rs = slice(c * CHUNK, (c + 1) * CHUNK)
        for h in range(N_HEADS):
            s = st[h]
            bt, k = s["bt"], s["k"]
            m_prev, c_prev = m_state[h], c_state[h]
            v_aug = jnp.concatenate([zm_ref[rs, v0 + h * dh:v0 + (h + 1) * dh], ones_blk], axis=-1)
            igc = ig_rep[h][rs, :]
            b_last = bt[CHUNK - 1:CHUNK, :]
            a_log = b_last - bt + igc
            a_max = jnp.max(a_log, axis=0, keepdims=True)
            m_new = jnp.maximum(b_last + m_prev, a_max)
            kw = (k.astype(F32) * jnp.exp(a_log - m_new)).astype(BF16)
            decay = jnp.exp(b_last + m_prev - m_new)
            s.update(v_aug=v_aug, m_prev=m_prev, qc=_dot(s["q"], c_prev.astype(BF16)))
            c_state[h] = jnp.concatenate([decay, decay], axis=-1) * c_prev + _dot_tn(kw, v_aug)
            m_state[h] = m_new

    def stage_values(c, st):
        rs = slice(c * CHUNK, (c + 1) * CHUNK)
        for h in range(N_HEADS):
            s = st[h]
            hs = slice(h * dh, (h + 1) * dh)
            inter = s["bt"] + s["m_prev"]
            m_t = jnp.maximum(inter, s["m_intra"])
            w_inter = jnp.exp(inter - m_t)
            smat = s["qk"] * jnp.exp(s["dmat"] - m_t[:, :CHUNK])
            sv = _dot(smat.astype(BF16), s["v_aug"])
            qc = s["qc"]
            nq = w_inter * qc[:, dh:] + sv[:, dh:]
            den = jnp.maximum(jnp.abs(nq), jnp.exp(-m_t))
            h_ref[rs, hs] = (w_inter * qc[:, :dh] + sv[:, :dh]) / den

    staged = stage_scores(0)
    for c in range(n_chunks):
        stage_state(c, staged)
        project_some()
        nxt = stage_scores(c + 1) if c + 1 < n_chunks else None
        project_some()
        stage_values(c, staged)
        staged = nxt
    for h in range(N_HEADS):
        cst_ref[h] = c_state[h]
        mst_ref[h:h + 1, :] = m_state[h]

    o0 = v0 + d_ml
    for h in range(N_HEADS):
        hs = slice(h * dh, (h + 1) * dh)
        hv = h_ref[:, hs]
        mu = jnp.mean(hv, axis=-1, keepdims=True)
        hc = hv - mu
        var = jnp.mean(hc * hc, axis=-1, keepdims=True)
        hn = hc * lax.rsqrt(var + EPS) * ghead_ref[:, hs]
        og = _sigmoid(zm_ref[:, o0 + h * dh:o0 + (h + 1) * dh].astype(F32))
        q_ref[:, hs] = (og * hn).astype(BF16)
    y_a = _dot(pool_ref[...], wa_ref[...])
    y_b = _dot(q_ref[...], wb_ref[...])
    d_model = x_ref.shape[1]
    ga = _sigmoid(zg_ref[:, :d_model].astype(F32))
    gb = _sigmoid(zg_ref[:, d_model:].astype(F32))
    merged = (ga * y_a + gb * y_b).astype(BF16)
    x2 = x_ref[...] + _dot(merged, wo_ref[...])
    x2_ref[...] = x2

    ms = jnp.mean(x2 * x2, axis=-1, keepdims=True)
    xn2 = x2 * lax.rsqrt(ms + EPS) * gffn_ref[...]
    xh = xn2.astype(BF16)
    xn2_ref[...] = xh
    xl = (xn2 - xh.astype(F32)).astype(BF16)
    lg = _dot(xh, wrh_ref[...]) + _dot(xl, wrh_ref[...]) + _dot(xh, wrl_ref[...]) + br_ref[...]

    lane = lax.broadcasted_iota(jnp.int32, (ts, LANES), 1)
    lanef = lane.astype(F32)
    big = float(4 * LANES)
    gl = jnp.where(lane < N_GROUPS, lg, -jnp.inf)
    gmax = jnp.max(gl, axis=-1, keepdims=True)
    g_sel = jnp.min(jnp.where(gl == gmax, lanef, big), axis=-1, keepdims=True)
    p_g = 1.0 / jnp.sum(jnp.exp(gl - gmax), axis=-1, keepdims=True)
    project_some()
    lo = ROUTER_LANE0 + EXPERTS_PER_GROUP * g_sel
    el = jnp.where((lanef >= lo) & (lanef < lo + EXPERTS_PER_GROUP), lg, -jnp.inf)
    m1 = jnp.max(el, axis=-1, keepdims=True)
    i1 = jnp.min(jnp.where(el == m1, lanef, big), axis=-1, keepdims=True)
    project_some()
    el2 = jnp.where(lanef == i1, -jnp.inf, el)
    m2 = jnp.max(el2, axis=-1, keepdims=True)
    i2 = jnp.min(jnp.where(el2 == m2, lanef, big), axis=-1, keepdims=True)
    project_some()
    e2x = jnp.exp(m2 - m1)
    gate1 = p_g / (1.0 + e2x)
    gate2 = p_g * e2x / (1.0 + e2x)

    oh1 = lanef == i1
    oh2 = lanef == i2
    ohs = jnp.where(oh1 | oh2, 1.0, 0.0)
    n_loc = jnp.sum(ohs, axis=0, keepdims=True)
    pieces = jnp.floor((n_loc + (ROW_PIECE - 1.0)) * (1.0 / ROW_PIECE))
    piece_off = _dot(jnp.broadcast_to(pieces, (8, LANES)).astype(BF16), ut_ref[...])[0:1, :]
    base = _dot(stri_ref[...], ohs.astype(BF16)) + ROW_PIECE * piece_off
    slot1 = jnp.sum(jnp.where(oh1, base, 0.0), axis=-1, keepdims=True)
    slot2 = jnp.sum(jnp.where(oh2, base, 0.0), axis=-1, keepdims=True)
    tstat_ref[...] = jnp.broadcast_to(pieces, tstat_ref.shape).astype(jnp.int32)

    rti = jnp.where(lane == 0, i1 - ROUTER_LANE0,
                    jnp.where(lane == 1, i2 - ROUTER_LANE0,
                              jnp.where(lane == 2, slot1, jnp.where(lane == 3, slot2, 0.0))))
    rti_ref[...] = rti.astype(jnp.int32)
    rtf_ref[...] = jnp.where(lane == 0, gate1, jnp.where(lane == 1, gate2, 0.0))
    h1 = jnp.floor(slot1 * (1.0 / SLOT_RADIX))
    h2 = jnp.floor(slot2 * (1.0 / SLOT_RADIX))
    parts = jnp.where(lane == 0, h1, jnp.where(lane == 1, slot1 - SLOT_RADIX * h1,
                      jnp.where(lane == 2, h2, jnp.where(lane == 3, slot2 - SLOT_RADIX * h2, 0.0))))
    srow_ref[...] = _dot_nt(sel_ref[...], parts.astype(BF16))
    project_some(len(pending))


def _mixer(x2d, g_mix, w_main, w_gates, w_if_c, w_if_t, params, batch, seq):
    T, D = x2d.shape
    ts = min(MIX_TS, seq)
    nts = seq // ts
    d_pool = params["w_br_a"].shape[0]
    d_ml = params["w_br_b"].shape[0]
    dh = d_ml // N_HEADS

    idx = np.arange(ts)
    same_chunk = (idx[:, None] // CHUNK) == (idx[None, :] // CHUNK)
    tri_c = jnp.asarray((idx[None, :] <= idx[:, None]) & same_chunk, BF16)
    tri_r = jnp.asarray((idx[:, None] <= idx[None, :]) & same_chunk, BF16)
    stri = jnp.asarray(idx[None, :] < idx[:, None], BF16)
    lane_idx = np.arange(LANES)
    ut = jnp.asarray(lane_idx[:, None] < lane_idx[None, :], BF16)
    sel = jnp.asarray(np.arange(8)[:, None] == lane_idx[None, :], BF16)

    tok = lambda b, j: (b * nts + j, 0)
    tok_t = lambda b, j: (0, b * nts + j)
    c2 = lambda b, j: (0, 0)
    c3 = lambda b, j: (0, 0, 0)
    full = lambda a: pl.BlockSpec(a.shape, c2 if a.ndim == 2 else c3)
    consts = [params[n] for n in ("b_if", "b_if_t", "conv_q", "conv_k", "g_head", "w_pool", "pool_scale",
                                  "w_br_a", "w_br_b", "w_out", "g_ffn", "w_r_hi", "w_r_lo", "b_r")]
    consts = [g_mix, w_main, w_gates, w_if_c, w_if_t] + consts + [tri_c, tri_r, stri, ut, sel]
    n_tiles = batch * nts
    nm, ng = w_main.shape[1], w_gates.shape[1]
    tok_next = lambda b, j: (jnp.minimum(b * nts + j + 1, n_tiles - 1), 0)
    z_scratch = [pltpu.VMEM((ts, nm), BF16), pltpu.VMEM((ts, ng), BF16),
                 pltpu.VMEM((ts, LANES), F32), pltpu.VMEM((16, ts), F32)]
    return pl.pallas_call(
        _mixer_kernel,
        grid=(batch, nts),
        in_specs=[pl.BlockSpec((ts, D), tok),
                  pl.BlockSpec((ts, D), tok_next)] + [full(a) for a in consts],
        out_specs=[pl.BlockSpec((ts, D), tok),
                   pl.BlockSpec((ts, D), tok),
                   pl.BlockSpec((ts, LANES), tok),
                   pl.BlockSpec((ts, LANES), tok),
                   pl.BlockSpec((8, LANES), tok),
                   pl.BlockSpec((8, ts), tok_t)],
        out_shape=[jax.ShapeDtypeStruct((T, D), F32),
                   jax.ShapeDtypeStruct((T, D), BF16),
                   jax.ShapeDtypeStruct((T, LANES), jnp.int32),
                   jax.ShapeDtypeStruct((T, LANES), F32),
                   jax.ShapeDtypeStruct((n_tiles * 8, LANES), jnp.int32),
                   jax.ShapeDtypeStruct((8, T), F32)],
        scratch_shapes=z_scratch + z_scratch + [
                        pltpu.VMEM((ts, D), BF16),
                        pltpu.VMEM(((d_pool + 2 * d_ml) // LANES, HALO + ts, LANES), F32),
                        pltpu.VMEM((ts, d_ml), BF16),
                        pltpu.VMEM((ts, d_ml), BF16),
                        pltpu.VMEM((ts, d_ml), F32),
                        pltpu.VMEM((ts, d_pool), BF16),
                        pltpu.VMEM((N_HEADS, dh, 2 * dh), F32),
                        pltpu.VMEM((8, LANES), F32)],
        compiler_params=_cparams(2),
        name="mixer",
    )(x2d, x2d, *consts)


def _for_each_piece(npieces_ref, glob_ref, tile, fn):
    base = tile * PIECES_PER_TILE
    n = npieces_ref[tile]

    def one(p):
        fn(pl.multiple_of(p * ROW_PIECE, ROW_PIECE), pl.multiple_of(glob_ref[base + p] * ROW_PIECE, ROW_PIECE))

    def group(g, carry):
        for u in range(PIECE_UNROLL):
            one(g * PIECE_UNROLL + u)
        return carry

    n_groups = lax.div(n, jnp.int32(PIECE_UNROLL))
    lax.fori_loop(0, n_groups, group, 0)
    for u in range(PIECE_UNROLL - 1):
        @pl.when(n_groups * PIECE_UNROLL + u < n)
        def _():
            one(n_groups * PIECE_UNROLL + u)


def _dispatch_kernel(npieces_ref, glob_ref, zflag_ref,
                     xn_ref, srow_ref, buf_ref, rows_ref, zeros_ref, sem, zsem):
    tt = xn_ref.shape[0]
    sl = rows_ref.shape[1]
    n_blocks = buf_ref.shape[0] // MOE_TM
    i = pl.program_id(0)
    n_tiles = pl.num_programs(0)
    cur = lax.rem(i, 2)

    @pl.when(i == 0)
    def _():
        zeros_ref[...] = jnp.zeros_like(zeros_ref)

        def zero_copy(b):
            return pltpu.make_async_copy(zeros_ref, buf_ref.at[pl.ds(b * MOE_TM, MOE_TM)], zsem)

        def zero_start(b, carry):
            @pl.when(zflag_ref[b] > 0)
            def _():
                zero_copy(b).start()
            return carry

        def zero_wait(b, carry):
            @pl.when(zflag_ref[b] > 0)
            def _():
                zero_copy(b).wait()
            return carry

        lax.fori_loop(0, n_blocks, zero_start, 0)
        lax.fori_loop(0, n_blocks, zero_wait, 0)

    def piece_copy(buf_slot, local_row, global_row):
        return pltpu.make_async_copy(rows_ref.at[buf_slot, pl.ds(local_row, ROW_PIECE)],
                                     buf_ref.at[pl.ds(global_row, ROW_PIECE)], sem.at[buf_slot])

    def start_pieces(tile, buf_slot):
        _for_each_piece(npieces_ref, glob_ref, tile, lambda l, g: piece_copy(buf_slot, l, g).start())

    def wait_pieces(tile, buf_slot):
        _for_each_piece(npieces_ref, glob_ref, tile, lambda l, g: piece_copy(buf_slot, l, g).wait())

    @pl.when(i >= 2)
    def _():
        wait_pieces(i - 2, cur)

    sr = srow_ref[...]
    slot1 = SLOT_RADIX * sr[0:1, :] + sr[1:2, :]
    slot2 = SLOT_RADIX * sr[2:3, :] + sr[3:4, :]
    r = lax.broadcasted_iota(jnp.int32, (sl, tt), 0).astype(F32)
    sel = jnp.where((r == slot1) | (r == slot2), 1.0, 0.0).astype(BF16)
    rows_ref[cur] = _dot(sel, xn_ref[...]).astype(BF16)
    start_pieces(i, cur)

    @pl.when(i == n_tiles - 1)
    def _():
        @pl.when(i >= 1)
        def _():
            wait_pieces(i - 1, 1 - cur)
        wait_pieces(i, cur)


def _dispatch(xn2, srow, npieces, piece_glob, zflag, n_rows):
    T, D = xn2.shape
    tt = MIX_TS
    return pl.pallas_call(
        _dispatch_kernel,
        grid_spec=pltpu.PrefetchScalarGridSpec(
            num_scalar_prefetch=3,
            grid=(T // tt,),
            in_specs=[pl.BlockSpec((tt, D), lambda i, *_: (i, 0)),
                      pl.BlockSpec((8, tt), lambda i, *_: (0, i))],
            out_specs=pl.BlockSpec(memory_space=pl.ANY),
            scratch_shapes=[pltpu.VMEM((2, MOE_SL, D), BF16),
                            pltpu.VMEM((MOE_TM, D), BF16),
                            pltpu.SemaphoreType.DMA((2,)),
                            pltpu.SemaphoreType.DMA(())]),
        out_shape=jax.ShapeDtypeStruct((n_rows, D), BF16),
        compiler_params=_cparams(1),
        name="dispatch",
    )(npieces, piece_glob, zflag, xn2, srow)


def _experts_kernel(blk_e_ref, nused_ref, nsub_ref, x_ref, wg_ref, wu_ref, wd_ref, y_ref,
                    wgb_ref, wub_ref, wdb_ref):
    i = pl.program_id(0)
    used = i < nused_ref[0]
    n_sub = nsub_ref[i]

    @pl.when(used & ((i == 0) | (blk_e_ref[i] != blk_e_ref[jnp.maximum(i - 1, 0)])))
    def _():
        wgb_ref[...] = wg_ref[0].astype(BF16)
        wub_ref[...] = wu_ref[0].astype(BF16)
        wdb_ref[...] = wd_ref[0].astype(BF16)

    for k in range(1, MOE_TM // EXPERT_SUB + 1):
        @pl.when(used & (n_sub == k))
        def _(k=k):
            m = k * EXPERT_SUB
            x = x_ref[:m, :]
            hg = _dot(x, wgb_ref[...])
            hu = _dot(x, wub_ref[...])
            hid = (_silu(hg) * hu).astype(BF16)
            y_ref[:m, :] = _dot(hid, wdb_ref[...]).astype(BF16)
            if m < MOE_TM:
                y_ref[m:, :] = jnp.zeros((MOE_TM - m, y_ref.shape[1]), BF16)

    @pl.when(jnp.logical_not(used))
    def _():
        y_ref[...] = jnp.zeros_like(y_ref)


def _experts(buf, blk_e, nused, nsub, w_gate, w_up, w_down):
    R, D = buf.shape
    de = w_gate.shape[2]
    n_blocks = R // MOE_TM

    def blk(i, nused_ref):
        return jnp.minimum(i, nused_ref[0] - 1)

    row_map = lambda i, be, nu, ns: (blk(i, nu), 0)
    w_map = lambda i, be, nu, ns: (be[blk(i, nu)], 0, 0)
    return pl.pallas_call(
        _experts_kernel,
        grid_spec=pltpu.PrefetchScalarGridSpec(
            num_scalar_prefetch=3,
            grid=(n_blocks,),
            in_specs=[pl.BlockSpec((MOE_TM, D), row_map),
                      pl.BlockSpec((1, D, de), w_map),
                      pl.BlockSpec((1, D, de), w_map),
                      pl.BlockSpec((1, de, D), w_map)],
            out_specs=pl.BlockSpec((MOE_TM, D), lambda i, be, nu, ns: (i, 0)),
            scratch_shapes=[pltpu.VMEM((D, de), BF16), pltpu.VMEM((D, de), BF16), pltpu.VMEM((de, D), BF16)]),
        out_shape=jax.ShapeDtypeStruct((R, D), BF16),
        compiler_params=_cparams(1),
        name="experts",
    )(blk_e, nused, nsub, buf, w_gate, w_up, w_down)


def _combine_kernel(npieces_ref, glob_ref, x2_ref, rti_ref, rtf_ref, gfin_ref, yb_ref, out_ref,
                    rows_ref, sem):
    tt = x2_ref.shape[0]
    sl = rows_ref.shape[1]
    i = pl.program_id(0)
    n_tiles = pl.num_programs(0)
    cur = lax.rem(i, 2)

    def piece_copy(buf_slot, local_row, global_row):
        return pltpu.make_async_copy(yb_ref.at[pl.ds(global_row, ROW_PIECE)],
                                     rows_ref.at[buf_slot, pl.ds(local_row, ROW_PIECE)], sem.at[buf_slot])

    def start_pieces(tile, buf_slot):
        _for_each_piece(npieces_ref, glob_ref, tile, lambda l, g: piece_copy(buf_slot, l, g).start())

    def wait_pieces(tile, buf_slot):
        _for_each_piece(npieces_ref, glob_ref, tile, lambda l, g: piece_copy(buf_slot, l, g).wait())

    @pl.when(i == 0)
    def _():
        rows_ref[...] = jnp.zeros_like(rows_ref)
        start_pieces(0, 0)

    @pl.when(i + 1 < n_tiles)
    def _():
        start_pieces(i + 1, 1 - cur)

    wait_pieces(i, cur)

    rti = rti_ref[...]
    rtf = rtf_ref[...]
    slot1 = rti[:, 2:3]
    slot2 = rti[:, 3:4]
    lane = lax.broadcasted_iota(jnp.int32, (tt, sl), 1)
    g = jnp.where(lane == slot1, rtf[:, 0:1], jnp.where(lane == slot2, rtf[:, 1:2], 0.0)).astype(BF16)
    y = x2_ref[...] + _dot(g, rows_ref[cur])
    ms = jnp.mean(y * y, axis=-1, keepdims=True)
    out_ref[...] = y * lax.rsqrt(ms + EPS) * gfin_ref[...]


def _combine(x2, rti, rtf, g_final, yb, npieces, piece_glob):
    T, D = x2.shape
    tt = MIX_TS
    tok = lambda i, *_: (i, 0)
    return pl.pallas_call(
        _combine_kernel,
        grid_spec=pltpu.PrefetchScalarGridSpec(
            num_scalar_prefetch=2,
            grid=(T // tt,),
            in_specs=[pl.BlockSpec((tt, D), tok),
                      pl.BlockSpec((tt, LANES), tok),
                      pl.BlockSpec((tt, LANES), tok),
                      pl.BlockSpec((1, D), lambda i, *_: (0, 0)),
                      pl.BlockSpec(memory_space=pl.ANY)],
            out_specs=pl.BlockSpec((tt, D), tok),
            scratch_shapes=[pltpu.VMEM((2, MOE_SL, D), BF16),
                            pltpu.SemaphoreType.DMA((2,))]),
        out_shape=jax.ShapeDtypeStruct((T, D), F32),
        compiler_params=_cparams(1),
        name="combine",
    )(npieces, piece_glob, x2, rti, rtf, g_final, yb)


def _pad_lanes(a, width=LANES):
    return jnp.pad(a, ((0, 0), (0, width - a.shape[1])))


def kernel(x, g_mix, w_in, b_if, conv_q, conv_k, g_head, w_pool, pool_scale, w_br_a, w_br_b, w_out,
           g_ffn, w_rg, b_rg, w_re, b_re, w_e_gate, w_e_up, w_e_down, g_final):
    B, S, D = x.shape
    T = B * S
    assert g_mix.shape[0] == 1, "single-layer block"
    assert S % MIX_TS == 0
    d_pool = w_br_a.shape[1]
    d_ml = w_br_b.shape[1]
    x2d = x.reshape(T, D)

    n_main = d_pool + 4 * d_ml
    w_l = w_in[0]
    w_main = w_l[:, :n_main].astype(BF16)
    w_if = w_l[:, n_main:n_main + 2 * N_HEADS]
    w_gates = w_l[:, n_main + 2 * N_HEADS:].astype(BF16)
    w_if_c = _pad_lanes(w_if).astype(BF16)
    w_if_t = jnp.pad(w_if.T, ((0, 16 - 2 * N_HEADS), (0, 0))).astype(BF16)
    w_r = _pad_lanes(jnp.concatenate([w_rg[0], w_re[0]], axis=1))
    w_r_hi = w_r.astype(BF16)
    w_r_lo = (w_r - w_r_hi.astype(F32)).astype(BF16)
    params = {
        "b_if": _pad_lanes(b_if[0][None, :]),
        "b_if_t": jnp.pad(b_if[0][:, None], ((0, 16 - 2 * N_HEADS), (0, 0))),
        "conv_q": conv_q[0], "conv_k": conv_k[0],
        "g_head": g_head[0][None, :],
        "w_pool": w_pool[0].astype(BF16),
        "pool_scale": pool_scale[0][None, :],
        "w_br_a": w_br_a[0].astype(BF16), "w_br_b": w_br_b[0].astype(BF16),
        "w_out": w_out[0].astype(BF16),
        "g_ffn": g_ffn[0][None, :],
        "w_r_hi": w_r_hi, "w_r_lo": w_r_lo,
        "b_r": _pad_lanes(jnp.concatenate([b_rg[0], b_re[0]])[None, :]),
    }

    x2, xn2, rti, rtf, tstat, srow = _mixer(x2d, g_mix[0][None, :], w_main, w_gates, w_if_c, w_if_t, params, B, S)

    n_tiles = T // MIX_TS
    pcs = tstat.reshape(n_tiles, 8, LANES)[:, 0, ROUTER_LANE0:ROUTER_LANE0 + N_EXPERTS]
    piece_loc = jnp.cumsum(pcs, axis=1) - pcs
    rows_e = jnp.sum(pcs, axis=0) * ROW_PIECE
    padded = (rows_e + MOE_TM - 1) // MOE_TM * MOE_TM
    pend = jnp.cumsum(padded)
    poff = pend - padded
    piece_glob = poff[None, :] // ROW_PIECE + jnp.cumsum(pcs, axis=0) - pcs
    n_rows = n_tiles * MOE_SL + N_EXPERTS * MOE_TM
    n_blocks = n_rows // MOE_TM
    nused = (pend[-1:] // MOE_TM).astype(jnp.int32)
    blk_ids = jnp.arange(n_blocks, dtype=jnp.int32)
    blk_e = jnp.sum((pend[None, :] <= blk_ids[:, None] * MOE_TM).astype(jnp.int32), axis=1)
    blk_e = jnp.minimum(blk_e, N_EXPERTS - 1)
    zflag = (((blk_ids + 1) * MOE_TM == pend[blk_e]) | (blk_ids >= nused[0])).astype(jnp.int32)
    p_ids = jnp.arange(PIECES_PER_TILE, dtype=jnp.int32)
    piece_end = piece_loc + pcs
    e_of_p = jnp.minimum(jnp.sum((piece_end[:, None, :] <= p_ids[None, :, None]).astype(jnp.int32), axis=2),
                         N_EXPERTS - 1)
    e_ids = jnp.arange(N_EXPERTS, dtype=jnp.int32)
    shift = jnp.sum(jnp.where(e_of_p[:, :, None] == e_ids[None, None, :], (piece_glob - piece_loc)[:, None, :], 0),
                    axis=2)
    glob_of_p = (shift + p_ids[None, :]).astype(jnp.int32)
    glob_of_p = glob_of_p.reshape(n_tiles * PIECES_PER_TILE)
    npieces = jnp.sum(pcs, axis=1).astype(jnp.int32)

    buf = _dispatch(xn2, srow, npieces, glob_of_p, zflag, n_rows)
    data_end = jnp.sum(jnp.where(blk_e[:, None] == jnp.arange(N_EXPERTS)[None, :], (poff + rows_e)[None, :], 0), axis=1)
    rows_in_blk = jnp.clip(data_end - blk_ids * MOE_TM, 0, MOE_TM)
    nsub = ((rows_in_blk + EXPERT_SUB - 1) // EXPERT_SUB).astype(jnp.int32)
    yb = _experts(buf, blk_e, nused, nsub, w_e_gate[0], w_e_up[0], w_e_down[0])
    out = _combine(x2, rti, rtf, g_final[None, :], yb, npieces, glob_of_p)
    return out.reshape(B, S, D)
```

```python
import functools

import numpy as np
import jax
import jax.numpy as jnp
from jax import lax
from jax.experimental import pallas as pl
from jax.experimental.pallas import tpu as pltpu

F32 = jnp.float32
BF16 = jnp.bfloat16

CHUNK = 64
POOL_WINDOWS = (2, 4, 8, 16)
N_HEADS = 4
CONV_K = 4
N_GROUPS = 4
EXPERTS_PER_GROUP = 8
N_EXPERTS = N_GROUPS * EXPERTS_PER_GROUP
TOP_K = 2
EPS = 1e-6

LANES = 128
HALO = 16
ROUTER_LANE0 = N_GROUPS

INPROJ_TN = 256
MIX_TS = 256
MOE_TM = 512
ROW_PIECE = 16
MOE_SL = TOP_K * MIX_TS + N_EXPERTS * ROW_PIECE
PIECES_PER_TILE = MOE_SL // ROW_PIECE
PIECE_UNROLL = 4
EXPERT_SUB = 128
SLOT_RADIX = 16
VMEM_LIMIT = 56 * 1024 * 1024


def _cparams(n_axes):
    return pltpu.CompilerParams(dimension_semantics=("arbitrary",) * n_axes,
                                vmem_limit_bytes=VMEM_LIMIT)


def _sigmoid(v):
    return 0.5 * jnp.tanh(0.5 * v) + 0.5


def _silu(v):
    return v * _sigmoid(v)


def _log_sigmoid(v):
    return jnp.minimum(v, 0.0) - jnp.log1p(jnp.exp(-jnp.abs(v)))


def _split3(v):
    hi = v.astype(BF16)
    r1 = v - hi.astype(F32)
    mid = r1.astype(BF16)
    lo = (r1 - mid.astype(F32)).astype(BF16)
    return hi, mid, lo


def _dot(a, b):
    return jnp.dot(a, b, preferred_element_type=F32)


def _dot_nt(a, b):
    return lax.dot_general(a, b, (((1,), (1,)), ((), ())), preferred_element_type=F32)


def _dot_tn(a, b):
    return lax.dot_general(a, b, (((0,), (0,)), ((), ())), preferred_element_type=F32)


def _inproj_steps(x_ref, g_ref, w_refs, z_refs, xn_ref):
    wm_ref, wg_ref, wif_ref, wift_ref = w_refs
    zm_ref, zg_ref, zif_ref, zift_ref = z_refs

    def norm():
        x = x_ref[...]
        ms = jnp.mean(x * x, axis=-1, keepdims=True)
        xn_ref[...] = (x * lax.rsqrt(ms + EPS) * g_ref[...]).astype(BF16)

    def block(w_ref, z_ref, c0):
        def run():
            cols = slice(c0, c0 + INPROJ_TN)
            z_ref[:, cols] = _dot(xn_ref[...], w_ref[:, cols]).astype(BF16)
        return run

    def gates():
        zif_ref[...] = _dot(xn_ref[...], wif_ref[...])
        zift_ref[...] = _dot_nt(wift_ref[...], xn_ref[...])

    steps = [norm, gates]
    steps += [block(wm_ref, zm_ref, c0) for c0 in range(0, zm_ref.shape[1], INPROJ_TN)]
    steps += [block(wg_ref, zg_ref, c0) for c0 in range(0, zg_ref.shape[1], INPROJ_TN)]
    return steps


def _mixer_kernel(x_ref, xnext_ref, gmix_ref, wm_ref, wg_ref, wif_ref, wift_ref,
                  bif_ref, bift_ref, convq_ref, convk_ref, ghead_ref, wpool_ref, pscale_ref,
                  wa_ref, wb_ref, wo_ref, gffn_ref, wrh_ref, wrl_ref, br_ref,
                  tric_ref, trir_ref, stri_ref, ut_ref, sel_ref,
                  x2_ref, xn2_ref, rti_ref, rtf_ref, tstat_ref, srow_ref,
                  zm_ref, zg_ref, zif_ref, zift_ref, zm_nxt, zg_nxt, zif_nxt, zift_nxt, xn_ref,
                  ext_ref, q_ref, k_ref, h_ref, pool_ref, cst_ref, mst_ref):
    ts = x_ref.shape[0]
    d_pool = wa_ref.shape[0]
    d_ml = wb_ref.shape[0]
    dh = d_ml // N_HEADS
    n_chunks = ts // CHUNK
    j = pl.program_id(1)
    w_in_refs = (wm_ref, wg_ref, wif_ref, wift_ref)
    z_cur = (zm_ref, zg_ref, zif_ref, zift_ref)
    z_nxt = (zm_nxt, zg_nxt, zif_nxt, zift_nxt)
    first = (j == 0) & (pl.program_id(0) == 0)

    @pl.when(first)
    def _():
        for step in _inproj_steps(x_ref, gmix_ref, w_in_refs, z_cur, xn_ref):
            step()

    @pl.when(jnp.logical_not(first))
    def _():
        for dst, src in zip(z_cur, z_nxt):
            dst[...] = src[...]

    @pl.when(j == 0)
    def _():
        ext_ref[:, :HALO, :] = jnp.zeros((ext_ref.shape[0], HALO, LANES), F32)
        cst_ref[...] = jnp.zeros_like(cst_ref)
        mst_ref[...] = jnp.zeros_like(mst_ref)

    pending = _inproj_steps(xnext_ref, gmix_ref, w_in_refs, z_nxt, xn_ref)

    def project_some(n=1):
        for _ in range(min(n, len(pending))):
            pending.pop(0)()

    project_some(2)

    row = lax.broadcasted_iota(jnp.int32, (ts, LANES), 0)
    pos1 = (row + j * ts + 1).astype(F32)

    def history(cg):
        cur = zm_ref[:, cg * LANES:(cg + 1) * LANES].astype(F32)
        ext_ref[cg, HALO:, :] = cur
        return cur, lambda s: ext_ref[cg, HALO - s:HALO - s + ts, :]

    def keep_history(cg, cur):
        ext_ref[cg, :HALO, :] = cur[ts - HALO:, :]

    n_pool_groups = d_pool // LANES
    for g in range(n_pool_groups):
        w = POOL_WINDOWS[g]
        cur, shifted = history(g)
        win = cur
        for s in range(1, w):
            win = win + shifted(s)
        keep_history(g, cur)
        cnt = jnp.minimum(pos1, float(w))
        d = win / cnt - cur
        y = _dot(d.astype(BF16), wpool_ref[g]) * pscale_ref[:, g * LANES:(g + 1) * LANES]
        pool_ref[:, g * LANES:(g + 1) * LANES] = y.astype(BF16)
        project_some()

    n_ml_groups = d_ml // LANES
    for which, (cw_ref, dst_ref, scale) in enumerate(((convq_ref, q_ref, 1.0), (convk_ref, k_ref, dh ** -0.5))):
        for g in range(n_ml_groups):
            cols = slice(g * LANES, (g + 1) * LANES)
            cg = n_pool_groups + which * n_ml_groups + g
            cur, shifted = history(cg)
            acc = cur * cw_ref[CONV_K - 1:CONV_K, cols]
            for sft in range(1, CONV_K):
                acc = acc + shifted(sft) * cw_ref[CONV_K - 1 - sft:CONV_K - sft, cols]
            keep_history(cg, cur)
            dst_ref[:, cols] = (_silu(acc) * scale).astype(BF16)
        project_some()

    zc = zif_ref[...] + bif_ref[...]
    lf_c = _log_sigmoid(zc)
    bc = sum(_dot(tric_ref[...], p) for p in _split3(lf_c))
    zr = zift_ref[...] + bift_ref[...]
    lf_r = _log_sigmoid(zr)
    br = sum(_dot(p, trir_ref[...]) for p in _split3(lf_r))

    ti = lax.broadcasted_iota(jnp.int32, (CHUNK, CHUNK), 0)
    si = lax.broadcasted_iota(jnp.int32, (CHUNK, CHUNK), 1)
    causal = si <= ti
    ones_blk = jnp.ones((CHUNK, dh), BF16)
    v0 = d_pool + 2 * d_ml
    ig_rep = [jnp.broadcast_to(zc[:, h:h + 1], (ts, dh)) for h in range(N_HEADS)]
    bt_rep = [jnp.broadcast_to(bc[:, N_HEADS + h:N_HEADS + h + 1], (ts, dh)) for h in range(N_HEADS)]

    m_state = [mst_ref[h:h + 1, :] for h in range(N_HEADS)]
    c_state = [cst_ref[h] for h in range(N_HEADS)]
    def stage_scores(c):
        rs = slice(c * CHUNK, (c + 1) * CHUNK)
        out = []
        for h in range(N_HEADS):
            hs = slice(h * dh, (h + 1) * dh)
            q = q_ref[rs, hs]
            k = k_ref[rs, hs]
            bt = bt_rep[h][rs, :]
            r_row = zr[h:h + 1, rs] - br[N_HEADS + h:N_HEADS + h + 1, rs]
            dmat = jnp.where(causal, bt[:, :CHUNK] + r_row, -jnp.inf)
            out.append(dict(q=q, k=k, bt=bt, dmat=dmat, qk=_dot_nt(q, k),
                            m_intra=jnp.max(dmat, axis=-1, keepdims=True)))
        return out

    def stage_state(c, st):
        rs = slice(c * CHUNK, (c + 1) * CHUNK)
        for h in range(N_HEADS):
            s = st[h]
            bt, k = s["bt"], s["k"]
            m_prev, c_prev = m_state[h], c_state[h]
            v_aug = jnp.concatenate([zm_ref[rs, v0 + h * dh:v0 + (h + 1) * dh], ones_blk], axis=-1)
            igc = ig_rep[h][rs, :]
            b_last = bt[CHUNK - 1:CHUNK, :]
            a_log = b_last - bt + igc
            a_max = jnp.max(a_log, axis=0, keepdims=True)
            m_new = jnp.maximum(b_last + m_prev, a_max)
            kw = (k.astype(F32) * jnp.exp(a_log - m_new)).astype(BF16)
            decay = jnp.exp(b_last + m_prev - m_new)
            s.update(v_aug=v_aug, m_prev=m_prev, qc=_dot(s["q"], c_prev.astype(BF16)))
            c_state[h] = jnp.concatenate([decay, decay], axis=-1) * c_prev + _dot_tn(kw, v_aug)
            m_state[h] = m_new

    def stage_values(c, st):
        rs = slice(c * CHUNK, (c + 1) * CHUNK)
        for h in range(N_HEADS):
            s = st[h]
            hs = slice(h * dh, (h + 1) * dh)
            inter = s["bt"] + s["m_prev"]
            m_t = jnp.maximum(inter, s["m_intra"])
            w_inter = jnp.exp(inter - m_t)
            smat = s["qk"] * jnp.exp(s["dmat"] - m_t[:, :CHUNK])
            sv = _dot(smat.astype(BF16), s["v_aug"])
            qc = s["qc"]
            nq = w_inter * qc[:, dh:] + sv[:, dh:]
            den = jnp.maximum(jnp.abs(nq), jnp.exp(-m_t))
            h_ref[rs, hs] = (w_inter * qc[:, :dh] + sv[:, :dh]) / den

    staged = stage_scores(0)
    for c in range(n_chunks):
        stage_state(c, staged)
        project_some()
        nxt = stage_scores(c + 1) if c + 1 < n_chunks else None
        project_some()
        stage_values(c, staged)
        staged = nxt
    for h in range(N_HEADS):
        cst_ref[h] = c_state[h]
        mst_ref[h:h + 1, :] = m_state[h]

    o0 = v0 + d_ml
    for h in range(N_HEADS):
        hs = slice(h * dh, (h + 1) * dh)
        hv = h_ref[:, hs]
        mu = jnp.mean(hv, axis=-1, keepdims=True)
        hc = hv - mu
        var = jnp.mean(hc * hc, axis=-1, keepdims=True)
        hn = hc * lax.rsqrt(var + EPS) * ghead_ref[:, hs]
        og = _sigmoid(zm_ref[:, o0 + h * dh:o0 + (h + 1) * dh].astype(F32))
        q_ref[:, hs] = (og * hn).astype(BF16)
    y_a = _dot(pool_ref[...], wa_ref[...])
    y_b = _dot(q_ref[...], wb_ref[...])
    d_model = x_ref.shape[1]
    ga = _sigmoid(zg_ref[:, :d_model].astype(F32))
    gb = _sigmoid(zg_ref[:, d_model:].astype(F32))
    merged = (ga * y_a + gb * y_b).astype(BF16)
    x2 = x_ref[...] + _dot(merged, wo_ref[...])
    x2_ref[...] = x2

    ms = jnp.mean(x2 * x2, axis=-1, keepdims=True)
    xn2 = x2 * lax.rsqrt(ms + EPS) * gffn_ref[...]
    xh = xn2.astype(BF16)
    xn2_ref[...] = xh
    xl = (xn2 - xh.astype(F32)).astype(BF16)
    lg = _dot(xh, wrh_ref[...]) + _dot(xl, wrh_ref[...]) + _dot(xh, wrl_ref[...]) + br_ref[...]

    lane = lax.broadcasted_iota(jnp.int32, (ts, LANES), 1)
    lanef = lane.astype(F32)
    big = float(4 * LANES)
    gl = jnp.where(lane < N_GROUPS, lg, -jnp.inf)
    gmax = jnp.max(gl, axis=-1, keepdims=True)
    g_sel = jnp.min(jnp.where(gl == gmax, lanef, big), axis=-1, keepdims=True)
    p_g = 1.0 / jnp.sum(jnp.exp(gl - gmax), axis=-1, keepdims=True)
    project_some()
    lo = ROUTER_LANE0 + EXPERTS_PER_GROUP * g_sel
    el = jnp.where((lanef >= lo) & (lanef < lo + EXPERTS_PER_GROUP), lg, -jnp.inf)
    m1 = jnp.max(el, axis=-1, keepdims=True)
    i1 = jnp.min(jnp.where(el == m1, lanef, big), axis=-1, keepdims=True)
    project_some()
    el2 = jnp.where(lanef == i1, -jnp.inf, el)
    m2 = jnp.max(el2, axis=-1, keepdims=True)
    i2 = jnp.min(jnp.where(el2 == m2, lanef, big), axis=-1, keepdims=True)
    project_some()
    e2x = jnp.exp(m2 - m1)
    gate1 = p_g / (1.0 + e2x)
    gate2 = p_g * e2x / (1.0 + e2x)

    oh1 = lanef == i1
    oh2 = lanef == i2
    ohs = jnp.where(oh1 | oh2, 1.0, 0.0)
    n_loc = jnp.sum(ohs, axis=0, keepdims=True)
    pieces = jnp.floor((n_loc + (ROW_PIECE - 1.0)) * (1.0 / ROW_PIECE))
    piece_off = _dot(jnp.broadcast_to(pieces, (8, LANES)).astype(BF16), ut_ref[...])[0:1, :]
    base = _dot(stri_ref[...], ohs.astype(BF16)) + ROW_PIECE * piece_off
    slot1 = jnp.sum(jnp.where(oh1, base, 0.0), axis=-1, keepdims=True)
    slot2 = jnp.sum(jnp.where(oh2, base, 0.0), axis=-1, keepdims=True)
    tstat_ref[...] = jnp.broadcast_to(pieces, tstat_ref.shape).astype(jnp.int32)

    rti = jnp.where(lane == 0, i1 - ROUTER_LANE0,
                    jnp.where(lane == 1, i2 - ROUTER_LANE0,
                              jnp.where(lane == 2, slot1, jnp.where(lane == 3, slot2, 0.0))))
    rti_ref[...] = rti.astype(jnp.int32)
    rtf_ref[...] = jnp.where(lane == 0, gate1, jnp.where(lane == 1, gate2, 0.0))
    h1 = jnp.floor(slot1 * (1.0 / SLOT_RADIX))
    h2 = jnp.floor(slot2 * (1.0 / SLOT_RADIX))
    parts = jnp.where(lane == 0, h1, jnp.where(lane == 1, slot1 - SLOT_RADIX * h1,
                      jnp.where(lane == 2, h2, jnp.where(lane == 3, slot2 - SLOT_RADIX * h2, 0.0))))
    srow_ref[...] = _dot_nt(sel_ref[...], parts.astype(BF16))
    project_some(len(pending))


def _mixer(x2d, g_mix, w_main, w_gates, w_if_c, w_if_t, params, batch, seq):
    T, D = x2d.shape
    ts = min(MIX_TS, seq)
    nts = seq // ts
    d_pool = params["w_br_a"].shape[0]
    d_ml = params["w_br_b"].shape[0]
    dh = d_ml // N_HEADS

    idx = np.arange(ts)
    same_chunk = (idx[:, None] // CHUNK) == (idx[None, :] // CHUNK)
    tri_c = jnp.asarray((idx[None, :] <= idx[:, None]) & same_chunk, BF16)
    tri_r = jnp.asarray((idx[:, None] <= idx[None, :]) & same_chunk, BF16)
    stri = jnp.asarray(idx[None, :] < idx[:, None], BF16)
    lane_idx = np.arange(LANES)
    ut = jnp.asarray(lane_idx[:, None] < lane_idx[None, :], BF16)
    sel = jnp.asarray(np.arange(8)[:, None] == lane_idx[None, :], BF16)

    tok = lambda b, j: (b * nts + j, 0)
    tok_t = lambda b, j: (0, b * nts + j)
    c2 = lambda b, j: (0, 0)
    c3 = lambda b, j: (0, 0, 0)
    full = lambda a: pl.BlockSpec(a.shape, c2 if a.ndim == 2 else c3)
    consts = [params[n] for n in ("b_if", "b_if_t", "conv_q", "conv_k", "g_head", "w_pool", "pool_scale",
                                  "w_br_a", "w_br_b", "w_out", "g_ffn", "w_r_hi", "w_r_lo", "b_r")]
    consts = [g_mix, w_main, w_gates, w_if_c, w_if_t] + consts + [tri_c, tri_r, stri, ut, sel]
    n_tiles = batch * nts
    nm, ng = w_main.shape[1], w_gates.shape[1]
    tok_next = lambda b, j: (jnp.minimum(b * nts + j + 1, n_tiles - 1), 0)
    z_scratch = [pltpu.VMEM((ts, nm), BF16), pltpu.VMEM((ts, ng), BF16),
                 pltpu.VMEM((ts, LANES), F32), pltpu.VMEM((16, ts), F32)]
    return pl.pallas_call(
        _mixer_kernel,
        grid=(batch, nts),
        in_specs=[pl.BlockSpec((ts, D), tok),
                  pl.BlockSpec((ts, D), tok_next)] + [full(a) for a in consts],
        out_specs=[pl.BlockSpec((ts, D), tok),
                   pl.BlockSpec((ts, D), tok),
                   pl.BlockSpec((ts, LANES), tok),
                   pl.BlockSpec((ts, LANES), tok),
                   pl.BlockSpec((8, LANES), tok),
                   pl.BlockSpec((8, ts), tok_t)],
        out_shape=[jax.ShapeDtypeStruct((T, D), F32),
                   jax.ShapeDtypeStruct((T, D), BF16),
                   jax.ShapeDtypeStruct((T, LANES), jnp.int32),
                   jax.ShapeDtypeStruct((T, LANES), F32),
                   jax.ShapeDtypeStruct((n_tiles * 8, LANES), jnp.int32),
                   jax.ShapeDtypeStruct((8, T), F32)],
        scratch_shapes=z_scratch + z_scratch + [
                        pltpu.VMEM((ts, D), BF16),
                        pltpu.VMEM(((d_pool + 2 * d_ml) // LANES, HALO + ts, LANES), F32),
                        pltpu.VMEM((ts, d_ml), BF16),
                        pltpu.VMEM((ts, d_ml), BF16),
                        pltpu.VMEM((ts, d_ml), F32),
                        pltpu.VMEM((ts, d_pool), BF16),
                        pltpu.VMEM((N_HEADS, dh, 2 * dh), F32),
                        pltpu.VMEM((8, LANES), F32)],
        compiler_params=_cparams(2),
        name="mixer",
    )(x2d, x2d, *consts)


def _for_each_piece(npieces_ref, glob_ref, tile, fn):
    base = tile * PIECES_PER_TILE
    n = npieces_ref[tile]

    def one(p):
        fn(pl.multiple_of(p * ROW_PIECE, ROW_PIECE), pl.multiple_of(glob_ref[base + p] * ROW_PIECE, ROW_PIECE))

    def group(g, carry):
        for u in range(PIECE_UNROLL):
            one(g * PIECE_UNROLL + u)
        return carry

    n_groups = lax.div(n, jnp.int32(PIECE_UNROLL))
    lax.fori_loop(0, n_groups, group, 0)
    for u in range(PIECE_UNROLL - 1):
        @pl.when(n_groups * PIECE_UNROLL + u < n)
        def _():
            one(n_groups * PIECE_UNROLL + u)


def _dispatch_kernel(npieces_ref, glob_ref, zflag_ref,
                     xn_ref, srow_ref, buf_ref, rows_ref, zeros_ref, sem, zsem):
    tt = xn_ref.shape[0]
    sl = rows_ref.shape[1]
    n_blocks = buf_ref.shape[0] // MOE_TM
    i = pl.program_id(0)
    n_tiles = pl.num_programs(0)
    cur = lax.rem(i, 2)

    @pl.when(i == 0)
    def _():
        zeros_ref[...] = jnp.zeros_like(zeros_ref)

        def zero_copy(b):
            return pltpu.make_async_copy(zeros_ref, buf_ref.at[pl.ds(b * MOE_TM, MOE_TM)], zsem)

        def zero_start(b, carry):
            @pl.when(zflag_ref[b] > 0)
            def _():
                zero_copy(b).start()
            return carry

        def zero_wait(b, carry):
            @pl.when(zflag_ref[b] > 0)
            def _():
                zero_copy(b).wait()
            return carry

        lax.fori_loop(0, n_blocks, zero_start, 0)
        lax.fori_loop(0, n_blocks, zero_wait, 0)

    def piece_copy(buf_slot, local_row, global_row):
        return pltpu.make_async_copy(rows_ref.at[buf_slot, pl.ds(local_row, ROW_PIECE)],
                                     buf_ref.at[pl.ds(global_row, ROW_PIECE)], sem.at[buf_slot])

    def start_pieces(tile, buf_slot):
        _for_each_piece(npieces_ref, glob_ref, tile, lambda l, g: piece_copy(buf_slot, l, g).start())

    def wait_pieces(tile, buf_slot):
        _for_each_piece(npieces_ref, glob_ref, tile, lambda l, g: piece_copy(buf_slot, l, g).wait())

    @pl.when(i >= 2)
    def _():
        wait_pieces(i - 2, cur)

    sr = srow_ref[...]
    slot1 = SLOT_RADIX * sr[0:1, :] + sr[1:2, :]
    slot2 = SLOT_RADIX * sr[2:3, :] + sr[3:4, :]
    r = lax.broadcasted_iota(jnp.int32, (sl, tt), 0).astype(F32)
    sel = jnp.where((r == slot1) | (r == slot2), 1.0, 0.0).astype(BF16)
    rows_ref[cur] = _dot(sel, xn_ref[...]).astype(BF16)
    start_pieces(i, cur)

    @pl.when(i == n_tiles - 1)
    def _():
        @pl.when(i >= 1)
        def _():
            wait_pieces(i - 1, 1 - cur)
        wait_pieces(i, cur)


def _dispatch(xn2, srow, npieces, piece_glob, zflag, n_rows):
    T, D = xn2.shape
    tt = MIX_TS
    return pl.pallas_call(
        _dispatch_kernel,
        grid_spec=pltpu.PrefetchScalarGridSpec(
            num_scalar_prefetch=3,
            grid=(T // tt,),
            in_specs=[pl.BlockSpec((tt, D), lambda i, *_: (i, 0)),
                      pl.BlockSpec((8, tt), lambda i, *_: (0, i))],
            out_specs=pl.BlockSpec(memory_space=pl.ANY),
            scratch_shapes=[pltpu.VMEM((2, MOE_SL, D), BF16),
                            pltpu.VMEM((MOE_TM, D), BF16),
                            pltpu.SemaphoreType.DMA((2,)),
                            pltpu.SemaphoreType.DMA(())]),
        out_shape=jax.ShapeDtypeStruct((n_rows, D), BF16),
        compiler_params=_cparams(1),
        name="dispatch",
    )(npieces, piece_glob, zflag, xn2, srow)


def _experts_kernel(blk_e_ref, nused_ref, nsub_ref, first_ref, next_e_ref, slot_ref,
                    x_ref, wg_hbm, wu_hbm, wd_hbm, y_ref,
                    wg32_ref, wu32_ref, wd32_ref, wgb_ref, wub_ref, wdb_ref, sem):
    i = pl.program_id(0)
    used = i < nused_ref[0]
    n_sub = nsub_ref[i]
    landing = ((wg_hbm, wg32_ref), (wu_hbm, wu32_ref), (wd_hbm, wd32_ref))

    def weight_copies(e, s):
        return [pltpu.make_async_copy(hbm.at[e], vmem.at[s], sem.at[s, n]) for n, (hbm, vmem) in enumerate(landing)]

    @pl.when(used & (i == 0))
    def _():
        for cp in weight_copies(blk_e_ref[0], 0):
            cp.start()

    @pl.when(used & (first_ref[i] > 0))
    def _():
        s = slot_ref[i]
        for cp in weight_copies(blk_e_ref[i], s):
            cp.wait()
        wgb_ref[...] = wg32_ref[s].astype(BF16)
        wub_ref[...] = wu32_ref[s].astype(BF16)
        wdb_ref[...] = wd32_ref[s].astype(BF16)

        @pl.when(next_e_ref[i] >= 0)
        def _():
            for cp in weight_copies(next_e_ref[i], 1 - s):
                cp.start()

    for k in range(1, MOE_TM // EXPERT_SUB + 1):
        @pl.when(used & (n_sub == k))
        def _(k=k):
            m = k * EXPERT_SUB
            x = x_ref[:m, :]
            hg = _dot(x, wgb_ref[...])
            hu = _dot(x, wub_ref[...])
            hid = (_silu(hg) * hu).astype(BF16)
            y_ref[:m, :] = _dot(hid, wdb_ref[...]).astype(BF16)
            if m < MOE_TM:
                y_ref[m:, :] = jnp.zeros((MOE_TM - m, y_ref.shape[1]), BF16)

    @pl.when(jnp.logical_not(used))
    def _():
        y_ref[...] = jnp.zeros_like(y_ref)


def _experts(buf, blk_e, nused, nsub, run_first, next_e, run_slot, w_gate, w_up, w_down):
    R, D = buf.shape
    de = w_gate.shape[2]
    n_blocks = R // MOE_TM
    row_map = lambda i, be, nu, *_: (jnp.minimum(i, nu[0] - 1), 0)
    return pl.pallas_call(
        _experts_kernel,
        grid_spec=pltpu.PrefetchScalarGridSpec(
            num_scalar_prefetch=6,
            grid=(n_blocks,),
            in_specs=[pl.BlockSpec((MOE_TM, D), row_map),
                      pl.BlockSpec(memory_space=pl.ANY),
                      pl.BlockSpec(memory_space=pl.ANY),
                      pl.BlockSpec(memory_space=pl.ANY)],
            out_specs=pl.BlockSpec((MOE_TM, D), lambda i, *_: (i, 0)),
            scratch_shapes=[pltpu.VMEM((2, D, de), F32), pltpu.VMEM((2, D, de), F32), pltpu.VMEM((2, de, D), F32),
                            pltpu.VMEM((D, de), BF16), pltpu.VMEM((D, de), BF16), pltpu.VMEM((de, D), BF16),
                            pltpu.SemaphoreType.DMA((2, 3))]),
        out_shape=jax.ShapeDtypeStruct((R, D), BF16),
        compiler_params=_cparams(1),
        name="experts",
    )(blk_e, nused, nsub, run_first, next_e, run_slot, buf, w_gate, w_up, w_down)


def _combine_kernel(npieces_ref, glob_ref, x2_ref, rti_ref, rtf_ref, gfin_ref, yb_ref, out_ref,
                    rows_ref, sem):
    tt = x2_ref.shape[0]
    sl = rows_ref.shape[1]
    i = pl.program_id(0)
    n_tiles = pl.num_programs(0)
    cur = lax.rem(i, 2)

    def piece_copy(buf_slot, local_row, global_row):
        return pltpu.make_async_copy(yb_ref.at[pl.ds(global_row, ROW_PIECE)],
                                     rows_ref.at[buf_slot, pl.ds(local_row, ROW_PIECE)], sem.at[buf_slot])

    def start_pieces(tile, buf_slot):
        _for_each_piece(npieces_ref, glob_ref, tile, lambda l, g: piece_copy(buf_slot, l, g).start())

    def wait_pieces(tile, buf_slot):
        _for_each_piece(npieces_ref, glob_ref, tile, lambda l, g: piece_copy(buf_slot, l, g).wait())

    @pl.when(i == 0)
    def _():
        rows_ref[...] = jnp.zeros_like(rows_ref)
        start_pieces(0, 0)

    @pl.when(i + 1 < n_tiles)
    def _():
        start_pieces(i + 1, 1 - cur)

    wait_pieces(i, cur)

    rti = rti_ref[...]
    rtf = rtf_ref[...]
    slot1 = rti[:, 2:3]
    slot2 = rti[:, 3:4]
    lane = lax.broadcasted_iota(jnp.int32, (tt, sl), 1)
    g = jnp.where(lane == slot1, rtf[:, 0:1], jnp.where(lane == slot2, rtf[:, 1:2], 0.0)).astype(BF16)
    y = x2_ref[...] + _dot(g, rows_ref[cur])
    ms = jnp.mean(y * y, axis=-1, keepdims=True)
    out_ref[...] = y * lax.rsqrt(ms + EPS) * gfin_ref[...]


def _combine(x2, rti, rtf, g_final, yb, npieces, piece_glob):
    T, D = x2.shape
    tt = MIX_TS
    tok = lambda i, *_: (i, 0)
    return pl.pallas_call(
        _combine_kernel,
        grid_spec=pltpu.PrefetchScalarGridSpec(
            num_scalar_prefetch=2,
            grid=(T // tt,),
            in_specs=[pl.BlockSpec((tt, D), tok),
                      pl.BlockSpec((tt, LANES), tok),
                      pl.BlockSpec((tt, LANES), tok),
                      pl.BlockSpec((1, D), lambda i, *_: (0, 0)),
                      pl.BlockSpec(memory_space=pl.ANY)],
            out_specs=pl.BlockSpec((tt, D), tok),
            scratch_shapes=[pltpu.VMEM((2, MOE_SL, D), BF16),
                            pltpu.SemaphoreType.DMA((2,))]),
        out_shape=jax.ShapeDtypeStruct((T, D), F32),
        compiler_params=_cparams(1),
        name="combine",
    )(npieces, piece_glob, x2, rti, rtf, g_final, yb)


def _pad_lanes(a, width=LANES):
    return jnp.pad(a, ((0, 0), (0, width - a.shape[1])))


def kernel(x, g_mix, w_in, b_if, conv_q, conv_k, g_head, w_pool, pool_scale, w_br_a, w_br_b, w_out,
           g_ffn, w_rg, b_rg, w_re, b_re, w_e_gate, w_e_up, w_e_down, g_final):
    B, S, D = x.shape
    T = B * S
    assert g_mix.shape[0] == 1, "single-layer block"
    assert S % MIX_TS == 0
    d_pool = w_br_a.shape[1]
    d_ml = w_br_b.shape[1]
    x2d = x.reshape(T, D)

    n_main = d_pool + 4 * d_ml
    w_l = w_in[0]
    w_main = w_l[:, :n_main].astype(BF16)
    w_if = w_l[:, n_main:n_main + 2 * N_HEADS]
    w_gates = w_l[:, n_main + 2 * N_HEADS:].astype(BF16)
    w_if_c = _pad_lanes(w_if).astype(BF16)
    w_if_t = jnp.pad(w_if.T, ((0, 16 - 2 * N_HEADS), (0, 0))).astype(BF16)
    w_r = _pad_lanes(jnp.concatenate([w_rg[0], w_re[0]], axis=1))
    w_r_hi = w_r.astype(BF16)
    w_r_lo = (w_r - w_r_hi.astype(F32)).astype(BF16)
    params = {
        "b_if": _pad_lanes(b_if[0][None, :]),
        "b_if_t": jnp.pad(b_if[0][:, None], ((0, 16 - 2 * N_HEADS), (0, 0))),
        "conv_q": conv_q[0], "conv_k": conv_k[0],
        "g_head": g_head[0][None, :],
        "w_pool": w_pool[0].astype(BF16),
        "pool_scale": pool_scale[0][None, :],
        "w_br_a": w_br_a[0].astype(BF16), "w_br_b": w_br_b[0].astype(BF16),
        "w_out": w_out[0].astype(BF16),
        "g_ffn": g_ffn[0][None, :],
        "w_r_hi": w_r_hi, "w_r_lo": w_r_lo,
        "b_r": _pad_lanes(jnp.concatenate([b_rg[0], b_re[0]])[None, :]),
    }

    x2, xn2, rti, rtf, tstat, srow = _mixer(x2d, g_mix[0][None, :], w_main, w_gates, w_if_c, w_if_t, params, B, S)

    n_tiles = T // MIX_TS
    pcs = tstat.reshape(n_tiles, 8, LANES)[:, 0, ROUTER_LANE0:ROUTER_LANE0 + N_EXPERTS]
    piece_loc = jnp.cumsum(pcs, axis=1) - pcs
    rows_e = jnp.sum(pcs, axis=0) * ROW_PIECE
    padded = (rows_e + MOE_TM - 1) // MOE_TM * MOE_TM
    pend = jnp.cumsum(padded)
    poff = pend - padded
    piece_glob = poff[None, :] // ROW_PIECE + jnp.cumsum(pcs, axis=0) - pcs
    n_rows = n_tiles * MOE_SL + N_EXPERTS * MOE_TM
    n_blocks = n_rows // MOE_TM
    nused = (pend[-1:] // MOE_TM).astype(jnp.int32)
    blk_ids = jnp.arange(n_blocks, dtype=jnp.int32)
    blk_e = jnp.sum((pend[None, :] <= blk_ids[:, None] * MOE_TM).astype(jnp.int32), axis=1)
    blk_e = jnp.minimum(blk_e, N_EXPERTS - 1)
    zflag = (((blk_ids + 1) * MOE_TM == pend[blk_e]) | (blk_ids >= nused[0])).astype(jnp.int32)
    p_ids = jnp.arange(PIECES_PER_TILE, dtype=jnp.int32)
    piece_end = piece_loc + pcs
    e_of_p = jnp.minimum(jnp.sum((piece_end[:, None, :] <= p_ids[None, :, None]).astype(jnp.int32), axis=2),
                         N_EXPERTS - 1)
    e_ids = jnp.arange(N_EXPERTS, dtype=jnp.int32)
    shift = jnp.sum(jnp.where(e_of_p[:, :, None] == e_ids[None, None, :], (piece_glob - piece_loc)[:, None, :], 0),
                    axis=2)
    glob_of_p = (shift + p_ids[None, :]).astype(jnp.int32)
    glob_of_p = glob_of_p.reshape(n_tiles * PIECES_PER_TILE)
    npieces = jnp.sum(pcs, axis=1).astype(jnp.int32)

    buf = _dispatch(xn2, srow, npieces, glob_of_p, zflag, n_rows)
    data_end = jnp.sum(jnp.where(blk_e[:, None] == jnp.arange(N_EXPERTS)[None, :], (poff + rows_e)[None, :], 0), axis=1)
    rows_in_blk = jnp.clip(data_end - blk_ids * MOE_TM, 0, MOE_TM)
    nsub = ((rows_in_blk + EXPERT_SUB - 1) // EXPERT_SUB).astype(jnp.int32)
    is_used = blk_ids < nused[0]
    run_first = (is_used & ((blk_ids == 0) | (blk_e != jnp.roll(blk_e, 1)))).astype(jnp.int32)
    run_slot = ((jnp.cumsum(run_first) - 1) % 2).astype(jnp.int32)
    e_ids32 = jnp.arange(N_EXPERTS, dtype=jnp.int32)
    later_nonempty = (e_ids32[None, :] > e_ids32[:, None]) & (rows_e[None, :] > 0)
    next_of_e = jnp.min(jnp.where(later_nonempty, e_ids32[None, :], N_EXPERTS), axis=1)
    next_of_e = jnp.where(next_of_e == N_EXPERTS, -1, next_of_e)
    next_e = jnp.sum(jnp.where(blk_e[:, None] == e_ids32[None, :], next_of_e[None, :], 0), axis=1).astype(jnp.int32)
    yb = _experts(buf, blk_e, nused, nsub, run_first, next_e, run_slot, w_e_gate[0], w_e_up[0], w_e_down[0])
    out = _combine(x2, rti, rtf, g_final[None, :], yb, npieces, glob_of_p)
    return out.reshape(B, S, D)
```

```python
import functools

import numpy as np
import jax
import jax.numpy as jnp
from jax import lax
from jax.experimental import pallas as pl
from jax.experimental.pallas import tpu as pltpu

F32 = jnp.float32
BF16 = jnp.bfloat16

CHUNK = 64
POOL_WINDOWS = (2, 4, 8, 16)
N_HEADS = 4
CONV_K = 4
N_GROUPS = 4
EXPERTS_PER_GROUP = 8
N_EXPERTS = N_GROUPS * EXPERTS_PER_GROUP
TOP_K = 2
EPS = 1e-6

LANES = 128
HALO = 16
ROUTER_LANE0 = N_GROUPS

INPROJ_TN = 256
MIX_TS = 256
MOE_TM = 512
ROW_PIECE = 16
MOE_SL = TOP_K * MIX_TS + N_EXPERTS * ROW_PIECE
PIECES_PER_TILE = MOE_SL // ROW_PIECE
PIECE_UNROLL = 4
EXPERT_SUB = 128
SLOT_RADIX = 16
VMEM_LIMIT = 56 * 1024 * 1024


def _cparams(n_axes):
    return pltpu.CompilerParams(dimension_semantics=("arbitrary",) * n_axes,
                                vmem_limit_bytes=VMEM_LIMIT)


def _sigmoid(v):
    return 0.5 * jnp.tanh(0.5 * v) + 0.5


def _silu(v):
    return v * _sigmoid(v)


def _log_sigmoid(v):
    return jnp.minimum(v, 0.0) - jnp.log1p(jnp.exp(-jnp.abs(v)))


def _split3(v):
    hi = v.astype(BF16)
    r1 = v - hi.astype(F32)
    mid = r1.astype(BF16)
    lo = (r1 - mid.astype(F32)).astype(BF16)
    return hi, mid, lo


def _dot(a, b):
    return jnp.dot(a, b, preferred_element_type=F32)


def _dot_nt(a, b):
    return lax.dot_general(a, b, (((1,), (1,)), ((), ())), preferred_element_type=F32)


def _dot_tn(a, b):
    return lax.dot_general(a, b, (((0,), (0,)), ((), ())), preferred_element_type=F32)


def _inproj_steps(x_ref, g_ref, w_refs, z_refs, xn_ref):
    wm_ref, wg_ref, wif_ref, wift_ref = w_refs
    zm_ref, zg_ref, zif_ref, zift_ref = z_refs

    def norm():
        x = x_ref[...]
        ms = jnp.mean(x * x, axis=-1, keepdims=True)
        xn_ref[...] = (x * lax.rsqrt(ms + EPS) * g_ref[...]).astype(BF16)

    def block(w_ref, z_ref, c0):
        def run():
            cols = slice(c0, c0 + INPROJ_TN)
            z_ref[:, cols] = _dot(xn_ref[...], w_ref[:, cols]).astype(BF16)
        return run

    def gates():
        zif_ref[...] = _dot(xn_ref[...], wif_ref[...])
        zift_ref[...] = _dot_nt(wift_ref[...], xn_ref[...])

    steps = [norm, gates]
    steps += [block(wm_ref, zm_ref, c0) for c0 in range(0, zm_ref.shape[1], INPROJ_TN)]
    steps += [block(wg_ref, zg_ref, c0) for c0 in range(0, zg_ref.shape[1], INPROJ_TN)]
    return steps


def _mixer_kernel(x_ref, xnext_ref, gmix_ref, wm_ref, wg_ref, wif_ref, wift_ref,
                  bif_ref, bift_ref, convq_ref, convk_ref, ghead_ref, wpool_ref, pscale_ref,
                  wa_ref, wb_ref, wo_ref, gffn_ref, wrh_ref, wrl_ref, br_ref,
                  tric_ref, trir_ref, stri_ref, ut_ref, sel_ref,
                  x2_ref, xn2_ref, rti_ref, rtf_ref, tstat_ref, srow_ref,
                  zm_ref, zg_ref, zif_ref, zift_ref, zm_nxt, zg_nxt, zif_nxt, zift_nxt, xn_ref,
                  ext_ref, q_ref, k_ref, h_ref, pool_ref, cst_ref, mst_ref, lg_ref, *, tiles_per_seq):
    ts = x_ref.shape[0]
    d_pool = wa_ref.shape[0]
    d_ml = wb_ref.shape[0]
    dh = d_ml // N_HEADS
    n_chunks = ts // CHUNK
    g_step = pl.program_id(0)
    j = lax.rem(g_step, tiles_per_seq)
    w_in_refs = (wm_ref, wg_ref, wif_ref, wift_ref)
    z_cur = (zm_ref, zg_ref, zif_ref, zift_ref)
    z_nxt = (zm_nxt, zg_nxt, zif_nxt, zift_nxt)
    first = g_step == 0

    @pl.when(first)
    def _():
        for step in _inproj_steps(x_ref, gmix_ref, w_in_refs, z_cur, xn_ref):
            step()
        lg_ref[...] = jnp.zeros_like(lg_ref)

    @pl.when(jnp.logical_not(first))
    def _():
        for dst, src in zip(z_cur, z_nxt):
            dst[...] = src[...]

    @pl.when(j == 0)
    def _():
        ext_ref[:, :HALO, :] = jnp.zeros((ext_ref.shape[0], HALO, LANES), F32)
        cst_ref[...] = jnp.zeros_like(cst_ref)
        mst_ref[...] = jnp.zeros_like(mst_ref)

    pending = _inproj_steps(xnext_ref, gmix_ref, w_in_refs, z_nxt, xn_ref)

    def project_some(n=1):
        for _ in range(min(n, len(pending))):
            pending.pop(0)()

    project_some(2)

    routed = _route_select(lg_ref[...])

    row = lax.broadcasted_iota(jnp.int32, (ts, LANES), 0)
    pos1 = (row + j * ts + 1).astype(F32)

    def history(cg):
        cur = zm_ref[:, cg * LANES:(cg + 1) * LANES].astype(F32)
        ext_ref[cg, HALO:, :] = cur
        return cur, lambda s: ext_ref[cg, HALO - s:HALO - s + ts, :]

    def keep_history(cg, cur):
        ext_ref[cg, :HALO, :] = cur[ts - HALO:, :]

    n_pool_groups = d_pool // LANES
    for g in range(n_pool_groups):
        w = POOL_WINDOWS[g]
        cur, shifted = history(g)
        win = cur
        for s in range(1, w):
            win = win + shifted(s)
        keep_history(g, cur)
        cnt = jnp.minimum(pos1, float(w))
        d = win / cnt - cur
        y = _dot(d.astype(BF16), wpool_ref[g]) * pscale_ref[:, g * LANES:(g + 1) * LANES]
        pool_ref[:, g * LANES:(g + 1) * LANES] = y.astype(BF16)
        project_some()

    n_ml_groups = d_ml // LANES
    for which, (cw_ref, dst_ref, scale) in enumerate(((convq_ref, q_ref, 1.0), (convk_ref, k_ref, dh ** -0.5))):
        for g in range(n_ml_groups):
            cols = slice(g * LANES, (g + 1) * LANES)
            cg = n_pool_groups + which * n_ml_groups + g
            cur, shifted = history(cg)
            acc = cur * cw_ref[CONV_K - 1:CONV_K, cols]
            for sft in range(1, CONV_K):
                acc = acc + shifted(sft) * cw_ref[CONV_K - 1 - sft:CONV_K - sft, cols]
            keep_history(cg, cur)
            dst_ref[:, cols] = (_silu(acc) * scale).astype(BF16)
        project_some()

    _route_slots(routed, stri_ref, ut_ref, sel_ref, rti_ref, rtf_ref, tstat_ref, srow_ref)
    project_some()

    zc = zif_ref[...] + bif_ref[...]
    lf_c = _log_sigmoid(zc)
    bc = sum(_dot(tric_ref[...], p) for p in _split3(lf_c))
    zr = zift_ref[...] + bift_ref[...]
    lf_r = _log_sigmoid(zr)
    br = sum(_dot(p, trir_ref[...]) for p in _split3(lf_r))
    project_some(2)

    ti = lax.broadcasted_iota(jnp.int32, (CHUNK, CHUNK), 0)
    si = lax.broadcasted_iota(jnp.int32, (CHUNK, CHUNK), 1)
    causal = si <= ti
    ones_blk = jnp.ones((CHUNK, dh), BF16)
    v0 = d_pool + 2 * d_ml
    ig_rep = [jnp.broadcast_to(zc[:, h:h + 1], (ts, dh)) for h in range(N_HEADS)]
    bt_rep = [jnp.broadcast_to(bc[:, N_HEADS + h:N_HEADS + h + 1], (ts, dh)) for h in range(N_HEADS)]

    m_state = [mst_ref[h:h + 1, :] for h in range(N_HEADS)]
    c_state = [cst_ref[h] for h in range(N_HEADS)]
    def stage_scores(c):
        rs = slice(c * CHUNK, (c + 1) * CHUNK)
        out = []
        for h in range(N_HEADS):
            hs = slice(h * dh, (h + 1) * dh)
            q = q_ref[rs, hs]
            k = k_ref[rs, hs]
            bt = bt_rep[h][rs, :]
            r_row = zr[h:h + 1, rs] - br[N_HEADS + h:N_HEADS + h + 1, rs]
            dmat = jnp.where(causal, bt[:, :CHUNK] + r_row, -jnp.inf)
            out.append(dict(q=q, k=k, bt=bt, dmat=dmat, qk=_dot_nt(q, k),
                            m_intra=jnp.max(dmat, axis=-1, keepdims=True)))
        return out

    def stage_state(c, st):
        rs = slice(c * CHUNK, (c + 1) * CHUNK)
        for h in range(N_HEADS):
            s = st[h]
            bt, k = s["bt"], s["k"]
            m_prev, c_prev = m_state[h], c_state[h]
            v_aug = jnp.concatenate([zm_ref[rs, v0 + h * dh:v0 + (h + 1) * dh], ones_blk], axis=-1)
            igc = ig_rep[h][rs, :]
            b_last = bt[CHUNK - 1:CHUNK, :]
            a_log = b_last - bt + igc
            a_max = jnp.max(a_log, axis=0, keepdims=True)
            m_new = jnp.maximum(b_last + m_prev, a_max)
            kw = (k.astype(F32) * jnp.exp(a_log - m_new)).astype(BF16)
            decay = jnp.exp(b_last + m_prev - m_new)
            s.update(v_aug=v_aug, m_prev=m_prev, qc=_dot(s["q"], c_prev.astype(BF16)))
            c_state[h] = jnp.concatenate([decay, decay], axis=-1) * c_prev + _dot_tn(kw, v_aug)
            m_state[h] = m_new

    def stage_values(c, st):
        rs = slice(c * CHUNK, (c + 1) * CHUNK)
        for h in range(N_HEADS):
            s = st[h]
            hs = slice(h * dh, (h + 1) * dh)
            inter = s["bt"] + s["m_prev"]
            m_t = jnp.maximum(inter, s["m_intra"])
            w_inter = jnp.exp(inter - m_t)
            smat = s["qk"] * jnp.exp(s["dmat"] - m_t[:, :CHUNK])
            sv = _dot(smat.astype(BF16), s["v_aug"])
            qc = s["qc"]
            nq = w_inter * qc[:, dh:] + sv[:, dh:]
            den = jnp.maximum(jnp.abs(nq), jnp.exp(-m_t))
            h_ref[rs, hs] = (w_inter * qc[:, :dh] + sv[:, :dh]) / den

    staged = stage_scores(0)
    for c in range(n_chunks):
        stage_state(c, staged)
        project_some()
        nxt = stage_scores(c + 1) if c + 1 < n_chunks else None
        project_some()
        stage_values(c, staged)
        staged = nxt
    for h in range(N_HEADS):
        cst_ref[h] = c_state[h]
        mst_ref[h:h + 1, :] = m_state[h]

    o0 = v0 + d_ml
    for h in range(N_HEADS):
        hs = slice(h * dh, (h + 1) * dh)
        hv = h_ref[:, hs]
        mu = jnp.mean(hv, axis=-1, keepdims=True)
        hc = hv - mu
        var = jnp.mean(hc * hc, axis=-1, keepdims=True)
        hn = hc * lax.rsqrt(var + EPS) * ghead_ref[:, hs]
        og = _sigmoid(zm_ref[:, o0 + h * dh:o0 + (h + 1) * dh].astype(F32))
        q_ref[:, hs] = (og * hn).astype(BF16)
    y_a = _dot(pool_ref[...], wa_ref[...])
    y_b = _dot(q_ref[...], wb_ref[...])
    d_model = x_ref.shape[1]
    ga = _sigmoid(zg_ref[:, :d_model].astype(F32))
    gb = _sigmoid(zg_ref[:, d_model:].astype(F32))
    merged = (ga * y_a + gb * y_b).astype(BF16)
    x2 = x_ref[...] + _dot(merged, wo_ref[...])
    x2_ref[...] = x2
    project_some(len(pending))

    ms = jnp.mean(x2 * x2, axis=-1, keepdims=True)
    xn2 = x2 * lax.rsqrt(ms + EPS) * gffn_ref[...]
    xh = xn2.astype(BF16)
    xn2_ref[...] = xh
    xl = (xn2 - xh.astype(F32)).astype(BF16)
    lg_ref[...] = _dot(xh, wrh_ref[...]) + _dot(xl, wrh_ref[...]) + _dot(xh, wrl_ref[...]) + br_ref[...]
    project_some(len(pending))


def _route_select(lg):
    ts = lg.shape[0]
    lane = lax.broadcasted_iota(jnp.int32, (ts, LANES), 1)
    lanef = lane.astype(F32)
    big = float(4 * LANES)
    gl = jnp.where(lane < N_GROUPS, lg, -jnp.inf)
    gmax = jnp.max(gl, axis=-1, keepdims=True)
    g_sel = jnp.min(jnp.where(gl == gmax, lanef, big), axis=-1, keepdims=True)
    p_g = 1.0 / jnp.sum(jnp.exp(gl - gmax), axis=-1, keepdims=True)
    lo = ROUTER_LANE0 + EXPERTS_PER_GROUP * g_sel
    el = jnp.where((lanef >= lo) & (lanef < lo + EXPERTS_PER_GROUP), lg, -jnp.inf)
    m1 = jnp.max(el, axis=-1, keepdims=True)
    i1 = jnp.min(jnp.where(el == m1, lanef, big), axis=-1, keepdims=True)
    el2 = jnp.where(lanef == i1, -jnp.inf, el)
    m2 = jnp.max(el2, axis=-1, keepdims=True)
    i2 = jnp.min(jnp.where(el2 == m2, lanef, big), axis=-1, keepdims=True)
    e2x = jnp.exp(m2 - m1)
    gate1 = p_g / (1.0 + e2x)
    gate2 = p_g * e2x / (1.0 + e2x)
    return dict(lane=lane, i1=i1, i2=i2, gate1=gate1, gate2=gate2, oh1=lanef == i1, oh2=lanef == i2)


def _route_slots(r, stri_ref, ut_ref, sel_ref, rti_ref, rtf_ref, tstat_ref, srow_ref):
    lane, oh1, oh2, i1, i2 = r["lane"], r["oh1"], r["oh2"], r["i1"], r["i2"]
    ohs = jnp.where(oh1 | oh2, 1.0, 0.0)
    n_loc = jnp.sum(ohs, axis=0, keepdims=True)
    pieces = jnp.floor((n_loc + (ROW_PIECE - 1.0)) * (1.0 / ROW_PIECE))
    piece_off = _dot(jnp.broadcast_to(pieces, (8, LANES)).astype(BF16), ut_ref[...])[0:1, :]
    base = _dot(stri_ref[...], ohs.astype(BF16)) + ROW_PIECE * piece_off
    slot1 = jnp.sum(jnp.where(oh1, base, 0.0), axis=-1, keepdims=True)
    slot2 = jnp.sum(jnp.where(oh2, base, 0.0), axis=-1, keepdims=True)
    tstat_ref[...] = jnp.broadcast_to(pieces, tstat_ref.shape).astype(jnp.int32)

    rti = jnp.where(lane == 0, i1 - ROUTER_LANE0,
                    jnp.where(lane == 1, i2 - ROUTER_LANE0,
                              jnp.where(lane == 2, slot1, jnp.where(lane == 3, slot2, 0.0))))
    rti_ref[...] = rti.astype(jnp.int32)
    rtf_ref[...] = jnp.where(lane == 0, r["gate1"], jnp.where(lane == 1, r["gate2"], 0.0))
    h1 = jnp.floor(slot1 * (1.0 / SLOT_RADIX))
    h2 = jnp.floor(slot2 * (1.0 / SLOT_RADIX))
    parts = jnp.where(lane == 0, h1, jnp.where(lane == 1, slot1 - SLOT_RADIX * h1,
                      jnp.where(lane == 2, h2, jnp.where(lane == 3, slot2 - SLOT_RADIX * h2, 0.0))))
    srow_ref[...] = _dot_nt(sel_ref[...], parts.astype(BF16))


def _mixer(x2d, g_mix, w_main, w_gates, w_if_c, w_if_t, params, batch, seq):
    T, D = x2d.shape
    ts = min(MIX_TS, seq)
    nts = seq // ts
    d_pool = params["w_br_a"].shape[0]
    d_ml = params["w_br_b"].shape[0]
    dh = d_ml // N_HEADS

    idx = np.arange(ts)
    same_chunk = (idx[:, None] // CHUNK) == (idx[None, :] // CHUNK)
    tri_c = jnp.asarray((idx[None, :] <= idx[:, None]) & same_chunk, BF16)
    tri_r = jnp.asarray((idx[:, None] <= idx[None, :]) & same_chunk, BF16)
    stri = jnp.asarray(idx[None, :] < idx[:, None], BF16)
    lane_idx = np.arange(LANES)
    ut = jnp.asarray(lane_idx[:, None] < lane_idx[None, :], BF16)
    sel = jnp.asarray(np.arange(8)[:, None] == lane_idx[None, :], BF16)

    n_tiles = batch * nts
    tok = lambda g: (g, 0)
    tok_in = lambda g: (jnp.minimum(g, n_tiles - 1), 0)
    tok_next = lambda g: (jnp.minimum(g + 1, n_tiles - 1), 0)
    tok_prev = lambda g: (jnp.maximum(g - 1, 0), 0)
    tok_prev_t = lambda g: (0, jnp.maximum(g - 1, 0))
    c2 = lambda g: (0, 0)
    c3 = lambda g: (0, 0, 0)
    full = lambda a: pl.BlockSpec(a.shape, c2 if a.ndim == 2 else c3)
    consts = [params[n] for n in ("b_if", "b_if_t", "conv_q", "conv_k", "g_head", "w_pool", "pool_scale",
                                  "w_br_a", "w_br_b", "w_out", "g_ffn", "w_r_hi", "w_r_lo", "b_r")]
    consts = [g_mix, w_main, w_gates, w_if_c, w_if_t] + consts + [tri_c, tri_r, stri, ut, sel]
    nm, ng = w_main.shape[1], w_gates.shape[1]
    z_scratch = [pltpu.VMEM((ts, nm), BF16), pltpu.VMEM((ts, ng), BF16),
                 pltpu.VMEM((ts, LANES), F32), pltpu.VMEM((16, ts), F32)]
    return pl.pallas_call(
        functools.partial(_mixer_kernel, tiles_per_seq=nts),
        grid=(n_tiles + 1,),
        in_specs=[pl.BlockSpec((ts, D), tok_in),
                  pl.BlockSpec((ts, D), tok_next)] + [full(a) for a in consts],
        out_specs=[pl.BlockSpec((ts, D), tok),
                   pl.BlockSpec((ts, D), tok),
                   pl.BlockSpec((ts, LANES), tok_prev),
                   pl.BlockSpec((ts, LANES), tok_prev),
                   pl.BlockSpec((8, LANES), tok_prev),
                   pl.BlockSpec((8, ts), tok_prev_t)],
        out_shape=[jax.ShapeDtypeStruct((T + ts, D), F32),
                   jax.ShapeDtypeStruct((T + ts, D), BF16),
                   jax.ShapeDtypeStruct((T, LANES), jnp.int32),
                   jax.ShapeDtypeStruct((T, LANES), F32),
                   jax.ShapeDtypeStruct((n_tiles * 8, LANES), jnp.int32),
                   jax.ShapeDtypeStruct((8, T), F32)],
        scratch_shapes=z_scratch + z_scratch + [
                        pltpu.VMEM((ts, D), BF16),
                        pltpu.VMEM(((d_pool + 2 * d_ml) // LANES, HALO + ts, LANES), F32),
                        pltpu.VMEM((ts, d_ml), BF16),
                        pltpu.VMEM((ts, d_ml), BF16),
                        pltpu.VMEM((ts, d_ml), F32),
                        pltpu.VMEM((ts, d_pool), BF16),
                        pltpu.VMEM((N_HEADS, dh, 2 * dh), F32),
                        pltpu.VMEM((8, LANES), F32),
                        pltpu.VMEM((ts, LANES), F32)],
        compiler_params=_cparams(1),
        name="mixer",
    )(x2d, x2d, *consts)


def _for_each_piece(npieces_ref, glob_ref, tile, fn):
    base = tile * PIECES_PER_TILE
    n = npieces_ref[tile]

    def one(p):
        fn(pl.multiple_of(p * ROW_PIECE, ROW_PIECE), pl.multiple_of(glob_ref[base + p] * ROW_PIECE, ROW_PIECE))

    def group(g, carry):
        for u in range(PIECE_UNROLL):
            one(g * PIECE_UNROLL + u)
        return carry

    n_groups = lax.div(n, jnp.int32(PIECE_UNROLL))
    lax.fori_loop(0, n_groups, group, 0)
    for u in range(PIECE_UNROLL - 1):
        @pl.when(n_groups * PIECE_UNROLL + u < n)
        def _():
            one(n_groups * PIECE_UNROLL + u)


def _dispatch_kernel(npieces_ref, glob_ref, zflag_ref,
                     xn_ref, srow_ref, buf_ref, rows_ref, zeros_ref, sem, zsem):
    tt = xn_ref.shape[0]
    sl = rows_ref.shape[1]
    n_blocks = buf_ref.shape[0] // MOE_TM
    i = pl.program_id(0)
    n_tiles = pl.num_programs(0)
    cur = lax.rem(i, 2)

    @pl.when(i == 0)
    def _():
        zeros_ref[...] = jnp.zeros_like(zeros_ref)

        def zero_copy(b):
            return pltpu.make_async_copy(zeros_ref, buf_ref.at[pl.ds(b * MOE_TM, MOE_TM)], zsem)

        def zero_start(b, carry):
            @pl.when(zflag_ref[b] > 0)
            def _():
                zero_copy(b).start()
            return carry

        def zero_wait(b, carry):
            @pl.when(zflag_ref[b] > 0)
            def _():
                zero_copy(b).wait()
            return carry

        lax.fori_loop(0, n_blocks, zero_start, 0)
        lax.fori_loop(0, n_blocks, zero_wait, 0)

    def piece_copy(buf_slot, local_row, global_row):
        return pltpu.make_async_copy(rows_ref.at[buf_slot, pl.ds(local_row, ROW_PIECE)],
                                     buf_ref.at[pl.ds(global_row, ROW_PIECE)], sem.at[buf_slot])

    def start_pieces(tile, buf_slot):
        _for_each_piece(npieces_ref, glob_ref, tile, lambda l, g: piece_copy(buf_slot, l, g).start())

    def wait_pieces(tile, buf_slot):
        _for_each_piece(npieces_ref, glob_ref, tile, lambda l, g: piece_copy(buf_slot, l, g).wait())

    @pl.when(i >= 2)
    def _():
        wait_pieces(i - 2, cur)

    sr = srow_ref[...]
    slot1 = SLOT_RADIX * sr[0:1, :] + sr[1:2, :]
    slot2 = SLOT_RADIX * sr[2:3, :] + sr[3:4, :]
    r = lax.broadcasted_iota(jnp.int32, (sl, tt), 0).astype(F32)
    sel = jnp.where((r == slot1) | (r == slot2), 1.0, 0.0).astype(BF16)
    rows_ref[cur] = _dot(sel, xn_ref[...]).astype(BF16)
    start_pieces(i, cur)

    @pl.when(i == n_tiles - 1)
    def _():
        @pl.when(i >= 1)
        def _():
            wait_pieces(i - 1, 1 - cur)
        wait_pieces(i, cur)


def _dispatch(xn2, srow, npieces, piece_glob, zflag, n_rows):
    T, D = srow.shape[1], xn2.shape[1]
    tt = MIX_TS
    return pl.pallas_call(
        _dispatch_kernel,
        grid_spec=pltpu.PrefetchScalarGridSpec(
            num_scalar_prefetch=3,
            grid=(T // tt,),
            in_specs=[pl.BlockSpec((tt, D), lambda i, *_: (i, 0)),
                      pl.BlockSpec((8, tt), lambda i, *_: (0, i))],
            out_specs=pl.BlockSpec(memory_space=pl.ANY),
            scratch_shapes=[pltpu.VMEM((2, MOE_SL, D), BF16),
                            pltpu.VMEM((MOE_TM, D), BF16),
                            pltpu.SemaphoreType.DMA((2,)),
                            pltpu.SemaphoreType.DMA(())]),
        out_shape=jax.ShapeDtypeStruct((n_rows, D), BF16),
        compiler_params=_cparams(1),
        name="dispatch",
    )(npieces, piece_glob, zflag, xn2, srow)


def _experts_kernel(blk_e_ref, nused_ref, nsub_ref, first_ref, next_e_ref, slot_ref,
                    x_ref, wg_hbm, wu_hbm, wd_hbm, y_ref,
                    wg32_ref, wu32_ref, wd32_ref, wgb_ref, wub_ref, wdb_ref, sem):
    i = pl.program_id(0)
    used = i < nused_ref[0]
    n_sub = nsub_ref[i]
    landing = ((wg_hbm, wg32_ref), (wu_hbm, wu32_ref), (wd_hbm, wd32_ref))

    def weight_copies(e, s):
        return [pltpu.make_async_copy(hbm.at[e], vmem.at[s], sem.at[s, n]) for n, (hbm, vmem) in enumerate(landing)]

    @pl.when(used & (i == 0))
    def _():
        for cp in weight_copies(blk_e_ref[0], 0):
            cp.start()

    @pl.when(used & (first_ref[i] > 0))
    def _():
        s = slot_ref[i]
        for cp in weight_copies(blk_e_ref[i], s):
            cp.wait()
        wgb_ref[...] = wg32_ref[s].astype(BF16)
        wub_ref[...] = wu32_ref[s].astype(BF16)
        wdb_ref[...] = wd32_ref[s].astype(BF16)

        @pl.when(next_e_ref[i] >= 0)
        def _():
            for cp in weight_copies(next_e_ref[i], 1 - s):
                cp.start()

    for k in range(1, MOE_TM // EXPERT_SUB + 1):
        @pl.when(used & (n_sub == k))
        def _(k=k):
            m = k * EXPERT_SUB
            x = x_ref[:m, :]
            hg = _dot(x, wgb_ref[...])
            hu = _dot(x, wub_ref[...])
            hid = (_silu(hg) * hu).astype(BF16)
            y_ref[:m, :] = _dot(hid, wdb_ref[...]).astype(BF16)
            if m < MOE_TM:
                y_ref[m:, :] = jnp.zeros((MOE_TM - m, y_ref.shape[1]), BF16)

    @pl.when(jnp.logical_not(used))
    def _():
        y_ref[...] = jnp.zeros_like(y_ref)


def _experts(buf, blk_e, nused, nsub, run_first, next_e, run_slot, w_gate, w_up, w_down):
    R, D = buf.shape
    de = w_gate.shape[2]
    n_blocks = R // MOE_TM
    row_map = lambda i, be, nu, *_: (jnp.minimum(i, nu[0] - 1), 0)
    return pl.pallas_call(
        _experts_kernel,
        grid_spec=pltpu.PrefetchScalarGridSpec(
            num_scalar_prefetch=6,
            grid=(n_blocks,),
            in_specs=[pl.BlockSpec((MOE_TM, D), row_map),
                      pl.BlockSpec(memory_space=pl.ANY),
                      pl.BlockSpec(memory_space=pl.ANY),
                      pl.BlockSpec(memory_space=pl.ANY)],
            out_specs=pl.BlockSpec((MOE_TM, D), lambda i, *_: (i, 0)),
            scratch_shapes=[pltpu.VMEM((2, D, de), F32), pltpu.VMEM((2, D, de), F32), pltpu.VMEM((2, de, D), F32),
                            pltpu.VMEM((D, de), BF16), pltpu.VMEM((D, de), BF16), pltpu.VMEM((de, D), BF16),
                            pltpu.SemaphoreType.DMA((2, 3))]),
        out_shape=jax.ShapeDtypeStruct((R, D), BF16),
        compiler_params=_cparams(1),
        name="experts",
    )(blk_e, nused, nsub, run_first, next_e, run_slot, buf, w_gate, w_up, w_down)


def _combine_kernel(npieces_ref, glob_ref, x2_ref, rti_ref, rtf_ref, gfin_ref, yb_ref, out_ref,
                    rows_ref, sem):
    tt = x2_ref.shape[0]
    sl = rows_ref.shape[1]
    i = pl.program_id(0)
    n_tiles = pl.num_programs(0)
    cur = lax.rem(i, 2)

    def piece_copy(buf_slot, local_row, global_row):
        return pltpu.make_async_copy(yb_ref.at[pl.ds(global_row, ROW_PIECE)],
                                     rows_ref.at[buf_slot, pl.ds(local_row, ROW_PIECE)], sem.at[buf_slot])

    def start_pieces(tile, buf_slot):
        _for_each_piece(npieces_ref, glob_ref, tile, lambda l, g: piece_copy(buf_slot, l, g).start())

    def wait_pieces(tile, buf_slot):
        _for_each_piece(npieces_ref, glob_ref, tile, lambda l, g: piece_copy(buf_slot, l, g).wait())

    @pl.when(i == 0)
    def _():
        rows_ref[...] = jnp.zeros_like(rows_ref)
        start_pieces(0, 0)

    @pl.when(i + 1 < n_tiles)
    def _():
        start_pieces(i + 1, 1 - cur)

    wait_pieces(i, cur)

    rti = rti_ref[...]
    rtf = rtf_ref[...]
    slot1 = rti[:, 2:3]
    slot2 = rti[:, 3:4]
    lane = lax.broadcasted_iota(jnp.int32, (tt, sl), 1)
    g = jnp.where(lane == slot1, rtf[:, 0:1], jnp.where(lane == slot2, rtf[:, 1:2], 0.0)).astype(BF16)
    y = x2_ref[...] + _dot(g, rows_ref[cur])
    ms = jnp.mean(y * y, axis=-1, keepdims=True)
    out_ref[...] = y * lax.rsqrt(ms + EPS) * gfin_ref[...]


def _combine(x2, rti, rtf, g_final, yb, npieces, piece_glob):
    T, D = rti.shape[0], x2.shape[1]
    tt = MIX_TS
    tok = lambda i, *_: (i, 0)
    return pl.pallas_call(
        _combine_kernel,
        grid_spec=pltpu.PrefetchScalarGridSpec(
            num_scalar_prefetch=2,
            grid=(T // tt,),
            in_specs=[pl.BlockSpec((tt, D), tok),
                      pl.BlockSpec((tt, LANES), tok),
                      pl.BlockSpec((tt, LANES), tok),
                      pl.BlockSpec((1, D), lambda i, *_: (0, 0)),
                      pl.BlockSpec(memory_space=pl.ANY)],
            out_specs=pl.BlockSpec((tt, D), tok),
            scratch_shapes=[pltpu.VMEM((2, MOE_SL, D), BF16),
                            pltpu.SemaphoreType.DMA((2,))]),
        out_shape=jax.ShapeDtypeStruct((T, D), F32),
        compiler_params=_cparams(1),
        name="combine",
    )(npieces, piece_glob, x2, rti, rtf, g_final, yb)


def _pad_lanes(a, width=LANES):
    return jnp.pad(a, ((0, 0), (0, width - a.shape[1])))


def kernel(x, g_mix, w_in, b_if, conv_q, conv_k, g_head, w_pool, pool_scale, w_br_a, w_br_b, w_out,
           g_ffn, w_rg, b_rg, w_re, b_re, w_e_gate, w_e_up, w_e_down, g_final):
    B, S, D = x.shape
    T = B * S
    assert g_mix.shape[0] == 1, "single-layer block"
    assert S % MIX_TS == 0
    d_pool = w_br_a.shape[1]
    d_ml = w_br_b.shape[1]
    x2d = x.reshape(T, D)

    n_main = d_pool + 4 * d_ml
    w_l = w_in[0]
    w_main = w_l[:, :n_main].astype(BF16)
    w_if = w_l[:, n_main:n_main + 2 * N_HEADS]
    w_gates = w_l[:, n_main + 2 * N_HEADS:].astype(BF16)
    w_if_c = _pad_lanes(w_if).astype(BF16)
    w_if_t = jnp.pad(w_if.T, ((0, 16 - 2 * N_HEADS), (0, 0))).astype(BF16)
    w_r = _pad_lanes(jnp.concatenate([w_rg[0], w_re[0]], axis=1))
    w_r_hi = w_r.astype(BF16)
    w_r_lo = (w_r - w_r_hi.astype(F32)).astype(BF16)
    params = {
        "b_if": _pad_lanes(b_if[0][None, :]),
        "b_if_t": jnp.pad(b_if[0][:, None], ((0, 16 - 2 * N_HEADS), (0, 0))),
        "conv_q": conv_q[0], "conv_k": conv_k[0],
        "g_head": g_head[0][None, :],
        "w_pool": w_pool[0].astype(BF16),
        "pool_scale": pool_scale[0][None, :],
        "w_br_a": w_br_a[0].astype(BF16), "w_br_b": w_br_b[0].astype(BF16),
        "w_out": w_out[0].astype(BF16),
        "g_ffn": g_ffn[0][None, :],
        "w_r_hi": w_r_hi, "w_r_lo": w_r_lo,
        "b_r": _pad_lanes(jnp.concatenate([b_rg[0], b_re[0]])[None, :]),
    }

    x2, xn2, rti, rtf, tstat, srow = _mixer(x2d, g_mix[0][None, :], w_main, w_gates, w_if_c, w_if_t, params, B, S)

    n_tiles = T // MIX_TS
    pcs = tstat.reshape(n_tiles, 8, LANES)[:, 0, ROUTER_LANE0:ROUTER_LANE0 + N_EXPERTS]
    piece_loc = jnp.cumsum(pcs, axis=1) - pcs
    rows_e = jnp.sum(pcs, axis=0) * ROW_PIECE
    padded = (rows_e + MOE_TM - 1) // MOE_TM * MOE_TM
    pend = jnp.cumsum(padded)
    poff = pend - padded
    piece_glob = poff[None, :] // ROW_PIECE + jnp.cumsum(pcs, axis=0) - pcs
    n_rows = n_tiles * MOE_SL + N_EXPERTS * MOE_TM
    n_blocks = n_rows // MOE_TM
    nused = (pend[-1:] // MOE_TM).astype(jnp.int32)
    blk_ids = jnp.arange(n_blocks, dtype=jnp.int32)
    blk_e = jnp.sum((pend[None, :] <= blk_ids[:, None] * MOE_TM).astype(jnp.int32), axis=1)
    blk_e = jnp.minimum(blk_e, N_EXPERTS - 1)
    zflag = (((blk_ids + 1) * MOE_TM == pend[blk_e]) | (blk_ids >= nused[0])).astype(jnp.int32)
    p_ids = jnp.arange(PIECES_PER_TILE, dtype=jnp.int32)
    piece_end = piece_loc + pcs
    e_of_p = jnp.minimum(jnp.sum((piece_end[:, None, :] <= p_ids[None, :, None]).astype(jnp.int32), axis=2),
                         N_EXPERTS - 1)
    e_ids = jnp.arange(N_EXPERTS, dtype=jnp.int32)
    shift = jnp.sum(jnp.where(e_of_p[:, :, None] == e_ids[None, None, :], (piece_glob - piece_loc)[:, None, :], 0),
                    axis=2)
    glob_of_p = (shift + p_ids[None, :]).astype(jnp.int32)
    glob_of_p = glob_of_p.reshape(n_tiles * PIECES_PER_TILE)
    npieces = jnp.sum(pcs, axis=1).astype(jnp.int32)

    buf = _dispatch(xn2, srow, npieces, glob_of_p, zflag, n_rows)
    data_end = jnp.sum(jnp.where(blk_e[:, None] == jnp.arange(N_EXPERTS)[None, :], (poff + rows_e)[None, :], 0), axis=1)
    rows_in_blk = jnp.clip(data_end - blk_ids * MOE_TM, 0, MOE_TM)
    nsub = ((rows_in_blk + EXPERT_SUB - 1) // EXPERT_SUB).astype(jnp.int32)
    is_used = blk_ids < nused[0]
    run_first = (is_used & ((blk_ids == 0) | (blk_e != jnp.roll(blk_e, 1)))).astype(jnp.int32)
    run_slot = ((jnp.cumsum(run_first) - 1) % 2).astype(jnp.int32)
    e_ids32 = jnp.arange(N_EXPERTS, dtype=jnp.int32)
    later_nonempty = (e_ids32[None, :] > e_ids32[:, None]) & (rows_e[None, :] > 0)
    next_of_e = jnp.min(jnp.where(later_nonempty, e_ids32[None, :], N_EXPERTS), axis=1)
    next_of_e = jnp.where(next_of_e == N_EXPERTS, -1, next_of_e)
    next_e = jnp.sum(jnp.where(blk_e[:, None] == e_ids32[None, :], next_of_e[None, :], 0), axis=1).astype(jnp.int32)
    yb = _experts(buf, blk_e, nused, nsub, run_first, next_e, run_slot, w_e_gate[0], w_e_up[0], w_e_down[0])
    out = _combine(x2, rti, rtf, g_final[None, :], yb, npieces, glob_of_p)
    return out.reshape(B, S, D)
```

```python
import functools

import numpy as np
import jax
import jax.numpy as jnp
from jax import lax
from jax.experimental import pallas as pl
from jax.experimental.pallas import tpu as pltpu

F32 = jnp.float32
BF16 = jnp.bfloat16

CHUNK = 64
POOL_WINDOWS = (2, 4, 8, 16)
N_HEADS = 4
CONV_K = 4
N_GROUPS = 4
EXPERTS_PER_GROUP = 8
N_EXPERTS = N_GROUPS * EXPERTS_PER_GROUP
TOP_K = 2
EPS = 1e-6

LANES = 128
HALO = 16
ROUTER_LANE0 = N_GROUPS

INPROJ_TN = 256
MIX_TS = 256
MOE_TM = 512
ROW_PIECE = 16
MOE_SL = TOP_K * MIX_TS + N_EXPERTS * ROW_PIECE
PIECES_PER_TILE = MOE_SL // ROW_PIECE
SLOT_SUB = 128
PIECE_UNROLL = 4
EXPERT_SUB = 128
SLOT_RADIX = 16
VMEM_LIMIT = 56 * 1024 * 1024


def _cparams(n_axes):
    return pltpu.CompilerParams(dimension_semantics=("arbitrary",) * n_axes,
                                vmem_limit_bytes=VMEM_LIMIT)


def _sigmoid(v):
    return 0.5 * jnp.tanh(0.5 * v) + 0.5


def _silu(v):
    return v * _sigmoid(v)


def _log_sigmoid(v):
    return jnp.minimum(v, 0.0) - jnp.log1p(jnp.exp(-jnp.abs(v)))


def _split3(v):
    hi = v.astype(BF16)
    r1 = v - hi.astype(F32)
    mid = r1.astype(BF16)
    lo = (r1 - mid.astype(F32)).astype(BF16)
    return hi, mid, lo


def _dot(a, b):
    return jnp.dot(a, b, preferred_element_type=F32)


def _dot_nt(a, b):
    return lax.dot_general(a, b, (((1,), (1,)), ((), ())), preferred_element_type=F32)


def _dot_tn(a, b):
    return lax.dot_general(a, b, (((0,), (0,)), ((), ())), preferred_element_type=F32)


def _inproj_steps(x_ref, g_ref, w_refs, z_refs, xn_ref):
    wm_ref, wg_ref, wif_ref, wift_ref = w_refs
    zm_ref, zg_ref, zif_ref, zift_ref = z_refs

    def norm():
        x = x_ref[...]
        ms = jnp.mean(x * x, axis=-1, keepdims=True)
        xn_ref[...] = (x * lax.rsqrt(ms + EPS) * g_ref[...]).astype(BF16)

    def block(w_ref, z_ref, c0):
        def run():
            cols = slice(c0, c0 + INPROJ_TN)
            z_ref[:, cols] = _dot(xn_ref[...], w_ref[:, cols]).astype(BF16)
        return run

    def gates():
        zif_ref[...] = _dot(xn_ref[...], wif_ref[...])
        zift_ref[...] = _dot_nt(wift_ref[...], xn_ref[...])

    steps = [norm, gates]
    steps += [block(wm_ref, zm_ref, c0) for c0 in range(0, zm_ref.shape[1], INPROJ_TN)]
    steps += [block(wg_ref, zg_ref, c0) for c0 in range(0, zg_ref.shape[1], INPROJ_TN)]
    return steps


def _mixer_kernel(x_ref, xnext_ref, gmix_ref, wm_ref, wg_ref, wif_ref, wift_ref,
                  bif_ref, bift_ref, convq_ref, convk_ref, ghead_ref, wpool_ref, pscale_ref,
                  wa_ref, wb_ref, wo_ref, gffn_ref, wrh_ref, wrl_ref, br_ref,
                  tric_ref, trir_ref, stri_ref, ut_ref, sel_ref,
                  x2_ref, xn2_ref, rti_ref, rtf_ref, tstat_ref, srow_ref,
                  zm_ref, zg_ref, zif_ref, zift_ref, zm_nxt, zg_nxt, zif_nxt, zift_nxt, xn_ref,
                  ext_ref, q_ref, k_ref, h_ref, pool_ref, cst_ref, mst_ref, lg_ref, *, tiles_per_seq):
    ts = x_ref.shape[0]
    d_pool = wa_ref.shape[0]
    d_ml = wb_ref.shape[0]
    dh = d_ml // N_HEADS
    n_chunks = ts // CHUNK
    g_step = pl.program_id(0)
    j = lax.rem(g_step, tiles_per_seq)
    w_in_refs = (wm_ref, wg_ref, wif_ref, wift_ref)
    z_cur = (zm_ref, zg_ref, zif_ref, zift_ref)
    z_nxt = (zm_nxt, zg_nxt, zif_nxt, zift_nxt)
    first = g_step == 0

    @pl.when(first)
    def _():
        for step in _inproj_steps(x_ref, gmix_ref, w_in_refs, z_cur, xn_ref):
            step()
        lg_ref[...] = jnp.zeros_like(lg_ref)

    @pl.when(jnp.logical_not(first))
    def _():
        for dst, src in zip(z_cur, z_nxt):
            dst[...] = src[...]

    @pl.when(j == 0)
    def _():
        ext_ref[:, :HALO, :] = jnp.zeros((ext_ref.shape[0], HALO, LANES), F32)
        cst_ref[...] = jnp.zeros_like(cst_ref)
        mst_ref[...] = jnp.zeros_like(mst_ref)

    pending = _inproj_steps(xnext_ref, gmix_ref, w_in_refs, z_nxt, xn_ref)

    def project_some(n=1):
        for _ in range(min(n, len(pending))):
            pending.pop(0)()

    project_some(2)

    routed = _route_select(lg_ref[...])

    row = lax.broadcasted_iota(jnp.int32, (ts, LANES), 0)
    pos1 = (row + j * ts + 1).astype(F32)

    def history(cg):
        cur = zm_ref[:, cg * LANES:(cg + 1) * LANES].astype(F32)
        ext_ref[cg, HALO:, :] = cur
        return cur, lambda s: ext_ref[cg, HALO - s:HALO - s + ts, :]

    def keep_history(cg, cur):
        ext_ref[cg, :HALO, :] = cur[ts - HALO:, :]

    n_pool_groups = d_pool // LANES
    for g in range(n_pool_groups):
        w = POOL_WINDOWS[g]
        cur, shifted = history(g)
        win = cur
        for s in range(1, w):
            win = win + shifted(s)
        keep_history(g, cur)
        cnt = jnp.minimum(pos1, float(w))
        d = win / cnt - cur
        y = _dot(d.astype(BF16), wpool_ref[g]) * pscale_ref[:, g * LANES:(g + 1) * LANES]
        pool_ref[:, g * LANES:(g + 1) * LANES] = y.astype(BF16)
        project_some()

    n_ml_groups = d_ml // LANES
    for which, (cw_ref, dst_ref, scale) in enumerate(((convq_ref, q_ref, 1.0), (convk_ref, k_ref, dh ** -0.5))):
        for g in range(n_ml_groups):
            cols = slice(g * LANES, (g + 1) * LANES)
            cg = n_pool_groups + which * n_ml_groups + g
            cur, shifted = history(cg)
            acc = cur * cw_ref[CONV_K - 1:CONV_K, cols]
            for sft in range(1, CONV_K):
                acc = acc + shifted(sft) * cw_ref[CONV_K - 1 - sft:CONV_K - sft, cols]
            keep_history(cg, cur)
            dst_ref[:, cols] = (_silu(acc) * scale).astype(BF16)
        project_some()

    _route_slots(routed, stri_ref, ut_ref, sel_ref, rti_ref, rtf_ref, tstat_ref, srow_ref)
    project_some()

    zc = zif_ref[...] + bif_ref[...]
    lf_c = _log_sigmoid(zc)
    bc = sum(_dot(tric_ref[...], p) for p in _split3(lf_c))
    zr = zift_ref[...] + bift_ref[...]
    lf_r = _log_sigmoid(zr)
    br = sum(_dot(p, trir_ref[...]) for p in _split3(lf_r))
    project_some(2)

    ti = lax.broadcasted_iota(jnp.int32, (CHUNK, CHUNK), 0)
    si = lax.broadcasted_iota(jnp.int32, (CHUNK, CHUNK), 1)
    causal = si <= ti
    ones_blk = jnp.ones((CHUNK, dh), BF16)
    v0 = d_pool + 2 * d_ml
    ig_rep = [jnp.broadcast_to(zc[:, h:h + 1], (ts, dh)) for h in range(N_HEADS)]
    bt_rep = [jnp.broadcast_to(bc[:, N_HEADS + h:N_HEADS + h + 1], (ts, dh)) for h in range(N_HEADS)]

    m_state = [mst_ref[h:h + 1, :] for h in range(N_HEADS)]
    c_state = [cst_ref[h] for h in range(N_HEADS)]
    def stage_scores(c):
        rs = slice(c * CHUNK, (c + 1) * CHUNK)
        out = []
        for h in range(N_HEADS):
            hs = slice(h * dh, (h + 1) * dh)
            q = q_ref[rs, hs]
            k = k_ref[rs, hs]
            bt = bt_rep[h][rs, :]
            r_row = zr[h:h + 1, rs] - br[N_HEADS + h:N_HEADS + h + 1, rs]
            dmat = jnp.where(causal, bt[:, :CHUNK] + r_row, -jnp.inf)
            out.append(dict(q=q, k=k, bt=bt, dmat=dmat, qk=_dot_nt(q, k),
                            m_intra=jnp.max(dmat, axis=-1, keepdims=True)))
        return out

    def stage_state(c, st):
        rs = slice(c * CHUNK, (c + 1) * CHUNK)
        for h in range(N_HEADS):
            s = st[h]
            bt, k = s["bt"], s["k"]
            m_prev, c_prev = m_state[h], c_state[h]
            v_aug = jnp.concatenate([zm_ref[rs, v0 + h * dh:v0 + (h + 1) * dh], ones_blk], axis=-1)
            igc = ig_rep[h][rs, :]
            b_last = bt[CHUNK - 1:CHUNK, :]
            a_log = b_last - bt + igc
            a_max = jnp.max(a_log, axis=0, keepdims=True)
            m_new = jnp.maximum(b_last + m_prev, a_max)
            kw = (k.astype(F32) * jnp.exp(a_log - m_new)).astype(BF16)
            decay = jnp.exp(b_last + m_prev - m_new)
            s.update(v_aug=v_aug, m_prev=m_prev, qc=_dot(s["q"], c_prev.astype(BF16)))
            c_state[h] = jnp.concatenate([decay, decay], axis=-1) * c_prev + _dot_tn(kw, v_aug)
            m_state[h] = m_new

    def stage_values(c, st):
        rs = slice(c * CHUNK, (c + 1) * CHUNK)
        for h in range(N_HEADS):
            s = st[h]
            hs = slice(h * dh, (h + 1) * dh)
            inter = s["bt"] + s["m_prev"]
            m_t = jnp.maximum(inter, s["m_intra"])
            w_inter = jnp.exp(inter - m_t)
            smat = s["qk"] * jnp.exp(s["dmat"] - m_t[:, :CHUNK])
            sv = _dot(smat.astype(BF16), s["v_aug"])
            qc = s["qc"]
            nq = w_inter * qc[:, dh:] + sv[:, dh:]
            den = jnp.maximum(jnp.abs(nq), jnp.exp(-m_t))
            h_ref[rs, hs] = (w_inter * qc[:, :dh] + sv[:, :dh]) / den

    staged = stage_scores(0)
    for c in range(n_chunks):
        stage_state(c, staged)
        project_some()
        nxt = stage_scores(c + 1) if c + 1 < n_chunks else None
        project_some()
        stage_values(c, staged)
        staged = nxt
    for h in range(N_HEADS):
        cst_ref[h] = c_state[h]
        mst_ref[h:h + 1, :] = m_state[h]

    o0 = v0 + d_ml
    for h in range(N_HEADS):
        hs = slice(h * dh, (h + 1) * dh)
        hv = h_ref[:, hs]
        mu = jnp.mean(hv, axis=-1, keepdims=True)
        hc = hv - mu
        var = jnp.mean(hc * hc, axis=-1, keepdims=True)
        hn = hc * lax.rsqrt(var + EPS) * ghead_ref[:, hs]
        og = _sigmoid(zm_ref[:, o0 + h * dh:o0 + (h + 1) * dh].astype(F32))
        q_ref[:, hs] = (og * hn).astype(BF16)
    y_a = _dot(pool_ref[...], wa_ref[...])
    y_b = _dot(q_ref[...], wb_ref[...])
    d_model = x_ref.shape[1]
    ga = _sigmoid(zg_ref[:, :d_model].astype(F32))
    gb = _sigmoid(zg_ref[:, d_model:].astype(F32))
    merged = (ga * y_a + gb * y_b).astype(BF16)
    x2 = x_ref[...] + _dot(merged, wo_ref[...])
    x2_ref[...] = x2
    project_some(len(pending))

    ms = jnp.mean(x2 * x2, axis=-1, keepdims=True)
    xn2 = x2 * lax.rsqrt(ms + EPS) * gffn_ref[...]
    xh = xn2.astype(BF16)
    xn2_ref[...] = xh
    xl = (xn2 - xh.astype(F32)).astype(BF16)
    lg_ref[...] = _dot(xh, wrh_ref[...]) + _dot(xl, wrh_ref[...]) + _dot(xh, wrl_ref[...]) + br_ref[...]
    project_some(len(pending))


def _route_select(lg):
    ts = lg.shape[0]
    lane = lax.broadcasted_iota(jnp.int32, (ts, LANES), 1)
    lanef = lane.astype(F32)
    big = float(4 * LANES)
    gl = jnp.where(lane < N_GROUPS, lg, -jnp.inf)
    gmax = jnp.max(gl, axis=-1, keepdims=True)
    g_sel = jnp.min(jnp.where(gl == gmax, lanef, big), axis=-1, keepdims=True)
    p_g = 1.0 / jnp.sum(jnp.exp(gl - gmax), axis=-1, keepdims=True)
    lo = ROUTER_LANE0 + EXPERTS_PER_GROUP * g_sel
    el = jnp.where((lanef >= lo) & (lanef < lo + EXPERTS_PER_GROUP), lg, -jnp.inf)
    m1 = jnp.max(el, axis=-1, keepdims=True)
    i1 = jnp.min(jnp.where(el == m1, lanef, big), axis=-1, keepdims=True)
    el2 = jnp.where(lanef == i1, -jnp.inf, el)
    m2 = jnp.max(el2, axis=-1, keepdims=True)
    i2 = jnp.min(jnp.where(el2 == m2, lanef, big), axis=-1, keepdims=True)
    e2x = jnp.exp(m2 - m1)
    gate1 = p_g / (1.0 + e2x)
    gate2 = p_g * e2x / (1.0 + e2x)
    return dict(lane=lane, i1=i1, i2=i2, gate1=gate1, gate2=gate2, oh1=lanef == i1, oh2=lanef == i2)


def _route_slots(r, stri_ref, ut_ref, sel_ref, rti_ref, rtf_ref, tstat_ref, srow_ref):
    lane, oh1, oh2, i1, i2 = r["lane"], r["oh1"], r["oh2"], r["i1"], r["i2"]
    ohs = jnp.where(oh1 | oh2, 1.0, 0.0)
    n_loc = jnp.sum(ohs, axis=0, keepdims=True)
    pieces = jnp.floor((n_loc + (ROW_PIECE - 1.0)) * (1.0 / ROW_PIECE))
    piece_off = _dot(jnp.broadcast_to(pieces, (8, LANES)).astype(BF16), ut_ref[...])[0:1, :]
    base = _dot(stri_ref[...], ohs.astype(BF16)) + ROW_PIECE * piece_off
    slot1 = jnp.sum(jnp.where(oh1, base, 0.0), axis=-1, keepdims=True)
    slot2 = jnp.sum(jnp.where(oh2, base, 0.0), axis=-1, keepdims=True)
    tstat_ref[...] = jnp.broadcast_to(pieces, tstat_ref.shape).astype(jnp.int32)

    rti = jnp.where(lane == 0, i1 - ROUTER_LANE0,
                    jnp.where(lane == 1, i2 - ROUTER_LANE0,
                              jnp.where(lane == 2, slot1, jnp.where(lane == 3, slot2, 0.0))))
    rti_ref[...] = rti.astype(jnp.int32)
    rtf_ref[...] = jnp.where(lane == 0, r["gate1"], jnp.where(lane == 1, r["gate2"], 0.0))
    h1 = jnp.floor(slot1 * (1.0 / SLOT_RADIX))
    h2 = jnp.floor(slot2 * (1.0 / SLOT_RADIX))
    parts = jnp.where(lane == 0, h1, jnp.where(lane == 1, slot1 - SLOT_RADIX * h1,
                      jnp.where(lane == 2, h2, jnp.where(lane == 3, slot2 - SLOT_RADIX * h2, 0.0))))
    srow_ref[...] = _dot_nt(sel_ref[...], parts.astype(BF16))


def _mixer(x2d, g_mix, w_main, w_gates, w_if_c, w_if_t, params, batch, seq):
    T, D = x2d.shape
    ts = min(MIX_TS, seq)
    nts = seq // ts
    d_pool = params["w_br_a"].shape[0]
    d_ml = params["w_br_b"].shape[0]
    dh = d_ml // N_HEADS

    idx = np.arange(ts)
    same_chunk = (idx[:, None] // CHUNK) == (idx[None, :] // CHUNK)
    tri_c = jnp.asarray((idx[None, :] <= idx[:, None]) & same_chunk, BF16)
    tri_r = jnp.asarray((idx[:, None] <= idx[None, :]) & same_chunk, BF16)
    stri = jnp.asarray(idx[None, :] < idx[:, None], BF16)
    lane_idx = np.arange(LANES)
    ut = jnp.asarray(lane_idx[:, None] < lane_idx[None, :], BF16)
    sel = jnp.asarray(np.arange(8)[:, None] == lane_idx[None, :], BF16)

    n_tiles = batch * nts
    tok = lambda g: (g, 0)
    tok_in = lambda g: (jnp.minimum(g, n_tiles - 1), 0)
    tok_next = lambda g: (jnp.minimum(g + 1, n_tiles - 1), 0)
    tok_prev = lambda g: (jnp.maximum(g - 1, 0), 0)
    tok_prev_t = lambda g: (0, jnp.maximum(g - 1, 0))
    c2 = lambda g: (0, 0)
    c3 = lambda g: (0, 0, 0)
    full = lambda a: pl.BlockSpec(a.shape, c2 if a.ndim == 2 else c3)
    consts = [params[n] for n in ("b_if", "b_if_t", "conv_q", "conv_k", "g_head", "w_pool", "pool_scale",
                                  "w_br_a", "w_br_b", "w_out", "g_ffn", "w_r_hi", "w_r_lo", "b_r")]
    consts = [g_mix, w_main, w_gates, w_if_c, w_if_t] + consts + [tri_c, tri_r, stri, ut, sel]
    nm, ng = w_main.shape[1], w_gates.shape[1]
    z_scratch = [pltpu.VMEM((ts, nm), BF16), pltpu.VMEM((ts, ng), BF16),
                 pltpu.VMEM((ts, LANES), F32), pltpu.VMEM((16, ts), F32)]
    return pl.pallas_call(
        functools.partial(_mixer_kernel, tiles_per_seq=nts),
        grid=(n_tiles + 1,),
        in_specs=[pl.BlockSpec((ts, D), tok_in),
                  pl.BlockSpec((ts, D), tok_next)] + [full(a) for a in consts],
        out_specs=[pl.BlockSpec((ts, D), tok),
                   pl.BlockSpec((ts, D), tok),
                   pl.BlockSpec((ts, LANES), tok_prev),
                   pl.BlockSpec((ts, LANES), tok_prev),
                   pl.BlockSpec((8, LANES), tok_prev),
                   pl.BlockSpec((8, ts), tok_prev_t)],
        out_shape=[jax.ShapeDtypeStruct((T + ts, D), F32),
                   jax.ShapeDtypeStruct((T + ts, D), BF16),
                   jax.ShapeDtypeStruct((T, LANES), jnp.int32),
                   jax.ShapeDtypeStruct((T, LANES), F32),
                   jax.ShapeDtypeStruct((n_tiles * 8, LANES), jnp.int32),
                   jax.ShapeDtypeStruct((8, T), F32)],
        scratch_shapes=z_scratch + z_scratch + [
                        pltpu.VMEM((ts, D), BF16),
                        pltpu.VMEM(((d_pool + 2 * d_ml) // LANES, HALO + ts, LANES), F32),
                        pltpu.VMEM((ts, d_ml), BF16),
                        pltpu.VMEM((ts, d_ml), BF16),
                        pltpu.VMEM((ts, d_ml), F32),
                        pltpu.VMEM((ts, d_pool), BF16),
                        pltpu.VMEM((N_HEADS, dh, 2 * dh), F32),
                        pltpu.VMEM((8, LANES), F32),
                        pltpu.VMEM((ts, LANES), F32)],
        compiler_params=_cparams(1),
        name="mixer",
    )(x2d, x2d, *consts)


def _for_each_piece(npieces_ref, glob_ref, tile, fn):
    base = tile * PIECES_PER_TILE
    n = npieces_ref[tile]

    def one(p):
        fn(pl.multiple_of(p * ROW_PIECE, ROW_PIECE), pl.multiple_of(glob_ref[base + p] * ROW_PIECE, ROW_PIECE))

    def group(g, carry):
        for u in range(PIECE_UNROLL):
            one(g * PIECE_UNROLL + u)
        return carry

    n_groups = lax.div(n, jnp.int32(PIECE_UNROLL))
    lax.fori_loop(0, n_groups, group, 0)
    for u in range(PIECE_UNROLL - 1):
        @pl.when(n_groups * PIECE_UNROLL + u < n)
        def _():
            one(n_groups * PIECE_UNROLL + u)


def _used_slot_groups(n_pieces):
    return lax.div(n_pieces * ROW_PIECE + (SLOT_SUB - 1), jnp.int32(SLOT_SUB))


def _dispatch_kernel(npieces_ref, glob_ref, zflag_ref,
                     xn_ref, srow_ref, buf_ref, rows_ref, zeros_ref, sem, zsem):
    tt = xn_ref.shape[0]
    sl = rows_ref.shape[1]
    n_blocks = buf_ref.shape[0] // MOE_TM
    i = pl.program_id(0)
    n_tiles = pl.num_programs(0)
    cur = lax.rem(i, 2)

    @pl.when(i == 0)
    def _():
        zeros_ref[...] = jnp.zeros_like(zeros_ref)

        def zero_copy(b):
            return pltpu.make_async_copy(zeros_ref, buf_ref.at[pl.ds(b * MOE_TM, MOE_TM)], zsem)

        def zero_start(b, carry):
            @pl.when(zflag_ref[b] > 0)
            def _():
                zero_copy(b).start()
            return carry

        def zero_wait(b, carry):
            @pl.when(zflag_ref[b] > 0)
            def _():
                zero_copy(b).wait()
            return carry

        lax.fori_loop(0, n_blocks, zero_start, 0)
        lax.fori_loop(0, n_blocks, zero_wait, 0)

    def piece_copy(buf_slot, local_row, global_row):
        return pltpu.make_async_copy(rows_ref.at[buf_slot, pl.ds(local_row, ROW_PIECE)],
                                     buf_ref.at[pl.ds(global_row, ROW_PIECE)], sem.at[buf_slot])

    def start_pieces(tile, buf_slot):
        _for_each_piece(npieces_ref, glob_ref, tile, lambda l, g: piece_copy(buf_slot, l, g).start())

    def wait_pieces(tile, buf_slot):
        _for_each_piece(npieces_ref, glob_ref, tile, lambda l, g: piece_copy(buf_slot, l, g).wait())

    @pl.when(i >= 2)
    def _():
        wait_pieces(i - 2, cur)

    sr = srow_ref[...]
    slot1 = SLOT_RADIX * sr[0:1, :] + sr[1:2, :]
    slot2 = SLOT_RADIX * sr[2:3, :] + sr[3:4, :]
    n_sub = _used_slot_groups(npieces_ref[i])
    for k in range(TOP_K * tt // SLOT_SUB, sl // SLOT_SUB + 1):
        @pl.when(n_sub == k)
        def _(k=k):
            m = k * SLOT_SUB
            r = lax.broadcasted_iota(jnp.int32, (m, tt), 0).astype(F32)
            sel = jnp.where((r == slot1) | (r == slot2), 1.0, 0.0).astype(BF16)
            rows_ref[cur, :m, :] = _dot(sel, xn_ref[...]).astype(BF16)
    start_pieces(i, cur)

    @pl.when(i == n_tiles - 1)
    def _():
        @pl.when(i >= 1)
        def _():
            wait_pieces(i - 1, 1 - cur)
        wait_pieces(i, cur)


def _dispatch(xn2, srow, npieces, piece_glob, zflag, n_rows):
    T, D = srow.shape[1], xn2.shape[1]
    tt = MIX_TS
    return pl.pallas_call(
        _dispatch_kernel,
        grid_spec=pltpu.PrefetchScalarGridSpec(
            num_scalar_prefetch=3,
            grid=(T // tt,),
            in_specs=[pl.BlockSpec((tt, D), lambda i, *_: (i, 0)),
                      pl.BlockSpec((8, tt), lambda i, *_: (0, i))],
            out_specs=pl.BlockSpec(memory_space=pl.ANY),
            scratch_shapes=[pltpu.VMEM((2, MOE_SL, D), BF16),
                            pltpu.VMEM((MOE_TM, D), BF16),
                            pltpu.SemaphoreType.DMA((2,)),
                            pltpu.SemaphoreType.DMA(())]),
        out_shape=jax.ShapeDtypeStruct((n_rows, D), BF16),
        compiler_params=_cparams(1),
        name="dispatch",
    )(npieces, piece_glob, zflag, xn2, srow)


def _experts_kernel(blk_e_ref, nused_ref, nsub_ref, first_ref, next_e_ref, slot_ref,
                    x_ref, wg_hbm, wu_hbm, wd_hbm, y_ref,
                    wg32_ref, wu32_ref, wd32_ref, wgb_ref, wub_ref, wdb_ref, sem):
    i = pl.program_id(0)
    used = i < nused_ref[0]
    n_sub = nsub_ref[i]
    landing = ((wg_hbm, wg32_ref), (wu_hbm, wu32_ref), (wd_hbm, wd32_ref))

    def weight_copies(e, s):
        return [pltpu.make_async_copy(hbm.at[e], vmem.at[s], sem.at[s, n]) for n, (hbm, vmem) in enumerate(landing)]

    @pl.when(used & (i == 0))
    def _():
        for cp in weight_copies(blk_e_ref[0], 0):
            cp.start()

    @pl.when(used & (first_ref[i] > 0))
    def _():
        s = slot_ref[i]
        for cp in weight_copies(blk_e_ref[i], s):
            cp.wait()
        wgb_ref[...] = wg32_ref[s].astype(BF16)
        wub_ref[...] = wu32_ref[s].astype(BF16)
        wdb_ref[...] = wd32_ref[s].astype(BF16)

        @pl.when(next_e_ref[i] >= 0)
        def _():
            for cp in weight_copies(next_e_ref[i], 1 - s):
                cp.start()

    for k in range(1, MOE_TM // EXPERT_SUB + 1):
        @pl.when(used & (n_sub == k))
        def _(k=k):
            m = k * EXPERT_SUB
            x = x_ref[:m, :]
            hg = _dot(x, wgb_ref[...])
            hu = _dot(x, wub_ref[...])
            hid = (_silu(hg) * hu).astype(BF16)
            y_ref[:m, :] = _dot(hid, wdb_ref[...]).astype(BF16)
            if m < MOE_TM:
                y_ref[m:, :] = jnp.zeros((MOE_TM - m, y_ref.shape[1]), BF16)

    @pl.when(jnp.logical_not(used))
    def _():
        y_ref[...] = jnp.zeros_like(y_ref)


def _experts(buf, blk_e, nused, nsub, run_first, next_e, run_slot, w_gate, w_up, w_down):
    R, D = buf.shape
    de = w_gate.shape[2]
    n_blocks = R // MOE_TM
    row_map = lambda i, be, nu, *_: (jnp.minimum(i, nu[0] - 1), 0)
    return pl.pallas_call(
        _experts_kernel,
        grid_spec=pltpu.PrefetchScalarGridSpec(
            num_scalar_prefetch=6,
            grid=(n_blocks,),
            in_specs=[pl.BlockSpec((MOE_TM, D), row_map),
                      pl.BlockSpec(memory_space=pl.ANY),
                      pl.BlockSpec(memory_space=pl.ANY),
                      pl.BlockSpec(memory_space=pl.ANY)],
            out_specs=pl.BlockSpec((MOE_TM, D), lambda i, *_: (i, 0)),
            scratch_shapes=[pltpu.VMEM((2, D, de), F32), pltpu.VMEM((2, D, de), F32), pltpu.VMEM((2, de, D), F32),
                            pltpu.VMEM((D, de), BF16), pltpu.VMEM((D, de), BF16), pltpu.VMEM((de, D), BF16),
                            pltpu.SemaphoreType.DMA((2, 3))]),
        out_shape=jax.ShapeDtypeStruct((R, D), BF16),
        compiler_params=_cparams(1),
        name="experts",
    )(blk_e, nused, nsub, run_first, next_e, run_slot, buf, w_gate, w_up, w_down)


def _combine_kernel(npieces_ref, glob_ref, x2_ref, rti_ref, rtf_ref, gfin_ref, yb_ref, out_ref,
                    rows_ref, sem):
    tt = x2_ref.shape[0]
    sl = rows_ref.shape[1]
    i = pl.program_id(0)
    n_tiles = pl.num_programs(0)
    cur = lax.rem(i, 2)

    def piece_copy(buf_slot, local_row, global_row):
        return pltpu.make_async_copy(yb_ref.at[pl.ds(global_row, ROW_PIECE)],
                                     rows_ref.at[buf_slot, pl.ds(local_row, ROW_PIECE)], sem.at[buf_slot])

    def start_pieces(tile, buf_slot):
        _for_each_piece(npieces_ref, glob_ref, tile, lambda l, g: piece_copy(buf_slot, l, g).start())

    def wait_pieces(tile, buf_slot):
        _for_each_piece(npieces_ref, glob_ref, tile, lambda l, g: piece_copy(buf_slot, l, g).wait())

    @pl.when(i == 0)
    def _():
        rows_ref[...] = jnp.zeros_like(rows_ref)
        start_pieces(0, 0)

    @pl.when(i + 1 < n_tiles)
    def _():
        start_pieces(i + 1, 1 - cur)

    wait_pieces(i, cur)

    rti = rti_ref[...]
    rtf = rtf_ref[...]
    slot1 = rti[:, 2:3]
    slot2 = rti[:, 3:4]
    n_sub = _used_slot_groups(npieces_ref[i])
    for k in range(TOP_K * tt // SLOT_SUB, sl // SLOT_SUB + 1):
        @pl.when(n_sub == k)
        def _(k=k):
            m = k * SLOT_SUB
            lane = lax.broadcasted_iota(jnp.int32, (tt, m), 1)
            g = jnp.where(lane == slot1, rtf[:, 0:1], jnp.where(lane == slot2, rtf[:, 1:2], 0.0)).astype(BF16)
            y = x2_ref[...] + _dot(g, rows_ref[cur, :m, :])
            ms = jnp.mean(y * y, axis=-1, keepdims=True)
            out_ref[...] = y * lax.rsqrt(ms + EPS) * gfin_ref[...]


def _combine(x2, rti, rtf, g_final, yb, npieces, piece_glob):
    T, D = rti.shape[0], x2.shape[1]
    tt = MIX_TS
    tok = lambda i, *_: (i, 0)
    return pl.pallas_call(
        _combine_kernel,
        grid_spec=pltpu.PrefetchScalarGridSpec(
            num_scalar_prefetch=2,
            grid=(T // tt,),
            in_specs=[pl.BlockSpec((tt, D), tok),
                      pl.BlockSpec((tt, LANES), tok),
                      pl.BlockSpec((tt, LANES), tok),
                      pl.BlockSpec((1, D), lambda i, *_: (0, 0)),
                      pl.BlockSpec(memory_space=pl.ANY)],
            out_specs=pl.BlockSpec((tt, D), tok),
            scratch_shapes=[pltpu.VMEM((2, MOE_SL, D), BF16),
                            pltpu.SemaphoreType.DMA((2,))]),
        out_shape=jax.ShapeDtypeStruct((T, D), F32),
        compiler_params=_cparams(1),
        name="combine",
    )(npieces, piece_glob, x2, rti, rtf, g_final, yb)


def _pad_lanes(a, width=LANES):
    return jnp.pad(a, ((0, 0), (0, width - a.shape[1])))


def kernel(x, g_mix, w_in, b_if, conv_q, conv_k, g_head, w_pool, pool_scale, w_br_a, w_br_b, w_out,
           g_ffn, w_rg, b_rg, w_re, b_re, w_e_gate, w_e_up, w_e_down, g_final):
    B, S, D = x.shape
    T = B * S
    assert g_mix.shape[0] == 1, "single-layer block"
    assert S % MIX_TS == 0
    d_pool = w_br_a.shape[1]
    d_ml = w_br_b.shape[1]
    x2d = x.reshape(T, D)

    n_main = d_pool + 4 * d_ml
    w_l = w_in[0]
    w_main = w_l[:, :n_main].astype(BF16)
    w_if = w_l[:, n_main:n_main + 2 * N_HEADS]
    w_gates = w_l[:, n_main + 2 * N_HEADS:].astype(BF16)
    w_if_c = _pad_lanes(w_if).astype(BF16)
    w_if_t = jnp.pad(w_if.T, ((0, 16 - 2 * N_HEADS), (0, 0))).astype(BF16)
    w_r = _pad_lanes(jnp.concatenate([w_rg[0], w_re[0]], axis=1))
    w_r_hi = w_r.astype(BF16)
    w_r_lo = (w_r - w_r_hi.astype(F32)).astype(BF16)
    params = {
        "b_if": _pad_lanes(b_if[0][None, :]),
        "b_if_t": jnp.pad(b_if[0][:, None], ((0, 16 - 2 * N_HEADS), (0, 0))),
        "conv_q": conv_q[0], "conv_k": conv_k[0],
        "g_head": g_head[0][None, :],
        "w_pool": w_pool[0].astype(BF16),
        "pool_scale": pool_scale[0][None, :],
        "w_br_a": w_br_a[0].astype(BF16), "w_br_b": w_br_b[0].astype(BF16),
        "w_out": w_out[0].astype(BF16),
        "g_ffn": g_ffn[0][None, :],
        "w_r_hi": w_r_hi, "w_r_lo": w_r_lo,
        "b_r": _pad_lanes(jnp.concatenate([b_rg[0], b_re[0]])[None, :]),
    }

    x2, xn2, rti, rtf, tstat, srow = _mixer(x2d, g_mix[0][None, :], w_main, w_gates, w_if_c, w_if_t, params, B, S)

    n_tiles = T // MIX_TS
    pcs = tstat.reshape(n_tiles, 8, LANES)[:, 0, ROUTER_LANE0:ROUTER_LANE0 + N_EXPERTS]
    piece_loc = jnp.cumsum(pcs, axis=1) - pcs
    rows_e = jnp.sum(pcs, axis=0) * ROW_PIECE
    padded = (rows_e + MOE_TM - 1) // MOE_TM * MOE_TM
    pend = jnp.cumsum(padded)
    poff = pend - padded
    piece_glob = poff[None, :] // ROW_PIECE + jnp.cumsum(pcs, axis=0) - pcs
    n_rows = n_tiles * MOE_SL + N_EXPERTS * MOE_TM
    n_blocks = n_rows // MOE_TM
    nused = (pend[-1:] // MOE_TM).astype(jnp.int32)
    blk_ids = jnp.arange(n_blocks, dtype=jnp.int32)
    blk_e = jnp.sum((pend[None, :] <= blk_ids[:, None] * MOE_TM).astype(jnp.int32), axis=1)
    blk_e = jnp.minimum(blk_e, N_EXPERTS - 1)
    zflag = (((blk_ids + 1) * MOE_TM == pend[blk_e]) | (blk_ids >= nused[0])).astype(jnp.int32)
    p_ids = jnp.arange(PIECES_PER_TILE, dtype=jnp.int32)
    piece_end = piece_loc + pcs
    e_of_p = jnp.minimum(jnp.sum((piece_end[:, None, :] <= p_ids[None, :, None]).astype(jnp.int32), axis=2),
                         N_EXPERTS - 1)
    e_ids = jnp.arange(N_EXPERTS, dtype=jnp.int32)
    shift = jnp.sum(jnp.where(e_of_p[:, :, None] == e_ids[None, None, :], (piece_glob - piece_loc)[:, None, :], 0),
                    axis=2)
    glob_of_p = (shift + p_ids[None, :]).astype(jnp.int32)
    glob_of_p = glob_of_p.reshape(n_tiles * PIECES_PER_TILE)
    npieces = jnp.sum(pcs, axis=1).astype(jnp.int32)

    buf = _dispatch(xn2, srow, npieces, glob_of_p, zflag, n_rows)
    data_end = jnp.sum(jnp.where(blk_e[:, None] == jnp.arange(N_EXPERTS)[None, :], (poff + rows_e)[None, :], 0), axis=1)
    rows_in_blk = jnp.clip(data_end - blk_ids * MOE_TM, 0, MOE_TM)
    nsub = ((rows_in_blk + EXPERT_SUB - 1) // EXPERT_SUB).astype(jnp.int32)
    is_used = blk_ids < nused[0]
    run_first = (is_used & ((blk_ids == 0) | (blk_e != jnp.roll(blk_e, 1)))).astype(jnp.int32)
    run_slot = ((jnp.cumsum(run_first) - 1) % 2).astype(jnp.int32)
    e_ids32 = jnp.arange(N_EXPERTS, dtype=jnp.int32)
    later_nonempty = (e_ids32[None, :] > e_ids32[:, None]) & (rows_e[None, :] > 0)
    next_of_e = jnp.min(jnp.where(later_nonempty, e_ids32[None, :], N_EXPERTS), axis=1)
    next_of_e = jnp.where(next_of_e == N_EXPERTS, -1, next_of_e)
    next_e = jnp.sum(jnp.where(blk_e[:, None] == e_ids32[None, :], next_of_e[None, :], 0), axis=1).astype(jnp.int32)
    yb = _experts(buf, blk_e, nused, nsub, run_first, next_e, run_slot, w_e_gate[0], w_e_up[0], w_e_down[0])
    out = _combine(x2, rti, rtf, g_final[None, :], yb, npieces, glob_of_p)
    return out.reshape(B, S, D)
```

```python
import functools

import numpy as np
import jax
import jax.numpy as jnp
from jax import lax
from jax.experimental import pallas as pl
from jax.experimental.pallas import tpu as pltpu

F32 = jnp.float32
BF16 = jnp.bfloat16

CHUNK = 64
POOL_WINDOWS = (2, 4, 8, 16)
N_HEADS = 4
CONV_K = 4
N_GROUPS = 4
EXPERTS_PER_GROUP = 8
N_EXPERTS = N_GROUPS * EXPERTS_PER_GROUP
TOP_K = 2
EPS = 1e-6

LANES = 128
HALO = 16
ROUTER_LANE0 = N_GROUPS

INPROJ_TN = 256
MIX_TS = 256
MOE_TM = 512
ROW_PIECE = 16
MOE_SL = TOP_K * MIX_TS + N_EXPERTS * ROW_PIECE
PIECES_PER_TILE = MOE_SL // ROW_PIECE
SLOT_SUB = 128
PIECE_UNROLL = 4
EXPERT_SUB = 128
SLOT_RADIX = 16
VMEM_LIMIT = 56 * 1024 * 1024


def _cparams(n_axes):
    return pltpu.CompilerParams(dimension_semantics=("arbitrary",) * n_axes,
                                vmem_limit_bytes=VMEM_LIMIT)


def _sigmoid(v):
    return 0.5 * jnp.tanh(0.5 * v) + 0.5


def _silu(v):
    return v * _sigmoid(v)


def _log_sigmoid(v):
    return jnp.minimum(v, 0.0) - jnp.log1p(jnp.exp(-jnp.abs(v)))


def _split3(v):
    hi = v.astype(BF16)
    r1 = v - hi.astype(F32)
    mid = r1.astype(BF16)
    lo = (r1 - mid.astype(F32)).astype(BF16)
    return hi, mid, lo


def _dot(a, b):
    return jnp.dot(a, b, preferred_element_type=F32)


def _dot_nt(a, b):
    return lax.dot_general(a, b, (((1,), (1,)), ((), ())), preferred_element_type=F32)


def _dot_tn(a, b):
    return lax.dot_general(a, b, (((0,), (0,)), ((), ())), preferred_element_type=F32)


def _inproj_steps(x_ref, g_ref, w_refs, z_refs, xn_ref):
    wm_ref, wg_ref, wif_ref, wift_ref = w_refs
    zm_ref, zg_ref, zif_ref, zift_ref = z_refs

    def norm():
        x = x_ref[...]
        ms = jnp.mean(x * x, axis=-1, keepdims=True)
        xn_ref[...] = (x * lax.rsqrt(ms + EPS) * g_ref[...]).astype(BF16)

    def block(w_ref, z_ref, c0):
        def run():
            cols = slice(c0, c0 + INPROJ_TN)
            z_ref[:, cols] = _dot(xn_ref[...], w_ref[:, cols]).astype(BF16)
        return run

    def gates():
        zif_ref[...] = _dot(xn_ref[...], wif_ref[...])
        zift_ref[...] = _dot_nt(wift_ref[...], xn_ref[...])

    steps = [norm, gates]
    steps += [block(wm_ref, zm_ref, c0) for c0 in range(0, zm_ref.shape[1], INPROJ_TN)]
    steps += [block(wg_ref, zg_ref, c0) for c0 in range(0, zg_ref.shape[1], INPROJ_TN)]
    return steps


def _mixer_kernel(x_ref, xnext_ref, gmix_ref, wm_ref, wg_ref, wif_ref, wift_ref,
                  bif_ref, bift_ref, convq_ref, convk_ref, ghead_ref, wpool_ref, pscale_ref,
                  wa_ref, wb_ref, wo_ref, gffn_ref, wrh_ref, wrl_ref, br_ref,
                  tric_ref, trir_ref, stri_ref, ut_ref, sel_ref,
                  x2_ref, xn2_ref, rti_ref, rtf_ref, tstat_ref, srow_ref,
                  zm_ref, zg_ref, zif_ref, zift_ref, zm_nxt, zg_nxt, zif_nxt, zift_nxt, xn_ref,
                  ext_ref, q_ref, k_ref, h_ref, pool_ref, cst_ref, mst_ref, lg_ref, *, tiles_per_seq):
    ts = x_ref.shape[0]
    d_pool = wa_ref.shape[0]
    d_ml = wb_ref.shape[0]
    dh = d_ml // N_HEADS
    n_chunks = ts // CHUNK
    g_step = pl.program_id(0)
    j = lax.rem(g_step, tiles_per_seq)
    w_in_refs = (wm_ref, wg_ref, wif_ref, wift_ref)
    z_cur = (zm_ref, zg_ref, zif_ref, zift_ref)
    z_nxt = (zm_nxt, zg_nxt, zif_nxt, zift_nxt)
    first = g_step == 0

    @pl.when(first)
    def _():
        for step in _inproj_steps(x_ref, gmix_ref, w_in_refs, z_cur, xn_ref):
            step()
        lg_ref[...] = jnp.zeros_like(lg_ref)

    @pl.when(jnp.logical_not(first))
    def _():
        for dst, src in zip(z_cur, z_nxt):
            dst[...] = src[...]

    @pl.when(j == 0)
    def _():
        ext_ref[:, :HALO, :] = jnp.zeros((ext_ref.shape[0], HALO, LANES), F32)
        cst_ref[...] = jnp.zeros_like(cst_ref)
        mst_ref[...] = jnp.zeros_like(mst_ref)

    pending = _inproj_steps(xnext_ref, gmix_ref, w_in_refs, z_nxt, xn_ref)

    def project_some(n=1):
        for _ in range(min(n, len(pending))):
            pending.pop(0)()

    project_some(2)

    routed = _route_select(lg_ref[...])

    row = lax.broadcasted_iota(jnp.int32, (ts, LANES), 0)
    pos1 = (row + j * ts + 1).astype(F32)

    def history(cg):
        cur = zm_ref[:, cg * LANES:(cg + 1) * LANES].astype(F32)
        ext_ref[cg, HALO:, :] = cur
        return cur, lambda s: ext_ref[cg, HALO - s:HALO - s + ts, :]

    def keep_history(cg, cur):
        ext_ref[cg, :HALO, :] = cur[ts - HALO:, :]

    n_pool_groups = d_pool // LANES
    for g in range(n_pool_groups):
        w = POOL_WINDOWS[g]
        cur, shifted = history(g)
        win = cur
        for s in range(1, w):
            win = win + shifted(s)
        keep_history(g, cur)
        cnt = jnp.minimum(pos1, float(w))
        d = win / cnt - cur
        y = _dot(d.astype(BF16), wpool_ref[g]) * pscale_ref[:, g * LANES:(g + 1) * LANES]
        pool_ref[:, g * LANES:(g + 1) * LANES] = y.astype(BF16)
        project_some()

    n_ml_groups = d_ml // LANES
    for which, (cw_ref, dst_ref, scale) in enumerate(((convq_ref, q_ref, 1.0), (convk_ref, k_ref, dh ** -0.5))):
        for g in range(n_ml_groups):
            cols = slice(g * LANES, (g + 1) * LANES)
            cg = n_pool_groups + which * n_ml_groups + g
            cur, shifted = history(cg)
            acc = cur * cw_ref[CONV_K - 1:CONV_K, cols]
            for sft in range(1, CONV_K):
                acc = acc + shifted(sft) * cw_ref[CONV_K - 1 - sft:CONV_K - sft, cols]
            keep_history(cg, cur)
            dst_ref[:, cols] = (_silu(acc) * scale).astype(BF16)
        project_some()

    _route_slots(routed, stri_ref, ut_ref, sel_ref, rti_ref, rtf_ref, tstat_ref, srow_ref)
    project_some()

    zc = zif_ref[...] + bif_ref[...]
    lf_c = _log_sigmoid(zc)
    bc = sum(_dot(tric_ref[...], p) for p in _split3(lf_c))
    zr = zift_ref[...] + bift_ref[...]
    lf_r = _log_sigmoid(zr)
    br = sum(_dot(p, trir_ref[...]) for p in _split3(lf_r))
    project_some(2)

    ti = lax.broadcasted_iota(jnp.int32, (CHUNK, CHUNK), 0)
    si = lax.broadcasted_iota(jnp.int32, (CHUNK, CHUNK), 1)
    causal = si <= ti
    ones_blk = jnp.ones((CHUNK, dh), BF16)
    v0 = d_pool + 2 * d_ml
    ig_rep = [jnp.broadcast_to(zc[:, h:h + 1], (ts, dh)) for h in range(N_HEADS)]
    bt_rep = [jnp.broadcast_to(bc[:, N_HEADS + h:N_HEADS + h + 1], (ts, dh)) for h in range(N_HEADS)]

    m_state = [mst_ref[h:h + 1, :] for h in range(N_HEADS)]
    c_state = [cst_ref[h] for h in range(N_HEADS)]
    def stage_scores(c):
        rs = slice(c * CHUNK, (c + 1) * CHUNK)
        out = []
        for h in range(N_HEADS):
            hs = slice(h * dh, (h + 1) * dh)
            q = q_ref[rs, hs]
            k = k_ref[rs, hs]
            bt = bt_rep[h][rs, :]
            r_row = zr[h:h + 1, rs] - br[N_HEADS + h:N_HEADS + h + 1, rs]
            dmat = jnp.where(causal, bt[:, :CHUNK] + r_row, -jnp.inf)
            out.append(dict(q=q, k=k, bt=bt, dmat=dmat, qk=_dot_nt(q, k),
                            m_intra=jnp.max(dmat, axis=-1, keepdims=True)))
        return out

    def stage_state(c, st):
        rs = slice(c * CHUNK, (c + 1) * CHUNK)
        for h in range(N_HEADS):
            s = st[h]
            bt, k = s["bt"], s["k"]
            m_prev, c_prev = m_state[h], c_state[h]
            v_aug = jnp.concatenate([zm_ref[rs, v0 + h * dh:v0 + (h + 1) * dh], ones_blk], axis=-1)
            igc = ig_rep[h][rs, :]
            b_last = bt[CHUNK - 1:CHUNK, :]
            a_log = b_last - bt + igc
            a_max = jnp.max(a_log, axis=0, keepdims=True)
            m_new = jnp.maximum(b_last + m_prev, a_max)
            kw = (k.astype(F32) * jnp.exp(a_log - m_new)).astype(BF16)
            decay = jnp.exp(b_last + m_prev - m_new)
            s.update(v_aug=v_aug, m_prev=m_prev, qc=_dot(s["q"], c_prev.astype(BF16)))
            c_state[h] = jnp.concatenate([decay, decay], axis=-1) * c_prev + _dot_tn(kw, v_aug)
            m_state[h] = m_new

    def stage_values(c, st):
        rs = slice(c * CHUNK, (c + 1) * CHUNK)
        for h in range(N_HEADS):
            s = st[h]
            hs = slice(h * dh, (h + 1) * dh)
            inter = s["bt"] + s["m_prev"]
            m_t = jnp.maximum(inter, s["m_intra"])
            w_inter = jnp.exp(inter - m_t)
            smat = s["qk"] * jnp.exp(s["dmat"] - m_t[:, :CHUNK])
            sv = _dot(smat.astype(BF16), s["v_aug"])
            qc = s["qc"]
            nq = w_inter * qc[:, dh:] + sv[:, dh:]
            den = jnp.maximum(jnp.abs(nq), jnp.exp(-m_t))
            h_ref[rs, hs] = (w_inter * qc[:, :dh] + sv[:, :dh]) / den

    staged = stage_scores(0)
    for c in range(n_chunks):
        stage_state(c, staged)
        project_some()
        nxt = stage_scores(c + 1) if c + 1 < n_chunks else None
        project_some()
        stage_values(c, staged)
        staged = nxt
    for h in range(N_HEADS):
        cst_ref[h] = c_state[h]
        mst_ref[h:h + 1, :] = m_state[h]

    o0 = v0 + d_ml
    for h in range(N_HEADS):
        hs = slice(h * dh, (h + 1) * dh)
        hv = h_ref[:, hs]
        mu = jnp.mean(hv, axis=-1, keepdims=True)
        hc = hv - mu
        var = jnp.mean(hc * hc, axis=-1, keepdims=True)
        hn = hc * lax.rsqrt(var + EPS) * ghead_ref[:, hs]
        og = _sigmoid(zm_ref[:, o0 + h * dh:o0 + (h + 1) * dh].astype(F32))
        q_ref[:, hs] = (og * hn).astype(BF16)
    y_a = _dot(pool_ref[...], wa_ref[...])
    y_b = _dot(q_ref[...], wb_ref[...])
    d_model = x_ref.shape[1]
    ga = _sigmoid(zg_ref[:, :d_model].astype(F32))
    gb = _sigmoid(zg_ref[:, d_model:].astype(F32))
    merged = (ga * y_a + gb * y_b).astype(BF16)
    x2 = x_ref[...] + _dot(merged, wo_ref[...])
    x2_ref[...] = x2
    project_some(len(pending))

    ms = jnp.mean(x2 * x2, axis=-1, keepdims=True)
    xn2 = x2 * lax.rsqrt(ms + EPS) * gffn_ref[...]
    xh = xn2.astype(BF16)
    xn2_ref[...] = xh
    xl = (xn2 - xh.astype(F32)).astype(BF16)
    lg_ref[...] = _dot(xh, wrh_ref[...]) + _dot(xl, wrh_ref[...]) + _dot(xh, wrl_ref[...]) + br_ref[...]
    project_some(len(pending))


def _route_select(lg):
    ts = lg.shape[0]
    lane = lax.broadcasted_iota(jnp.int32, (ts, LANES), 1)
    lanef = lane.astype(F32)
    big = float(4 * LANES)
    gl = jnp.where(lane < N_GROUPS, lg, -jnp.inf)
    gmax = jnp.max(gl, axis=-1, keepdims=True)
    g_sel = jnp.min(jnp.where(gl == gmax, lanef, big), axis=-1, keepdims=True)
    p_g = 1.0 / jnp.sum(jnp.exp(gl - gmax), axis=-1, keepdims=True)
    lo = ROUTER_LANE0 + EXPERTS_PER_GROUP * g_sel
    el = jnp.where((lanef >= lo) & (lanef < lo + EXPERTS_PER_GROUP), lg, -jnp.inf)
    m1 = jnp.max(el, axis=-1, keepdims=True)
    i1 = jnp.min(jnp.where(el == m1, lanef, big), axis=-1, keepdims=True)
    el2 = jnp.where(lanef == i1, -jnp.inf, el)
    m2 = jnp.max(el2, axis=-1, keepdims=True)
    i2 = jnp.min(jnp.where(el2 == m2, lanef, big), axis=-1, keepdims=True)
    e2x = jnp.exp(m2 - m1)
    gate1 = p_g / (1.0 + e2x)
    gate2 = p_g * e2x / (1.0 + e2x)
    return dict(lane=lane, i1=i1, i2=i2, gate1=gate1, gate2=gate2, oh1=lanef == i1, oh2=lanef == i2)


def _route_slots(r, stri_ref, ut_ref, sel_ref, rti_ref, rtf_ref, tstat_ref, srow_ref):
    lane, oh1, oh2, i1, i2 = r["lane"], r["oh1"], r["oh2"], r["i1"], r["i2"]
    ohs = jnp.where(oh1 | oh2, 1.0, 0.0)
    n_loc = jnp.sum(ohs, axis=0, keepdims=True)
    pieces = jnp.floor((n_loc + (ROW_PIECE - 1.0)) * (1.0 / ROW_PIECE))
    piece_off = _dot(jnp.broadcast_to(pieces, (8, LANES)).astype(BF16), ut_ref[...])[0:1, :]
    base = _dot(stri_ref[...], ohs.astype(BF16)) + ROW_PIECE * piece_off
    slot1 = jnp.sum(jnp.where(oh1, base, 0.0), axis=-1, keepdims=True)
    slot2 = jnp.sum(jnp.where(oh2, base, 0.0), axis=-1, keepdims=True)
    tstat_ref[...] = jnp.broadcast_to(pieces, tstat_ref.shape).astype(jnp.int32)

    rti = jnp.where(lane == 0, i1 - ROUTER_LANE0,
                    jnp.where(lane == 1, i2 - ROUTER_LANE0,
                              jnp.where(lane == 2, slot1, jnp.where(lane == 3, slot2, 0.0))))
    rti_ref[...] = rti.astype(jnp.int32)
    rtf_ref[...] = jnp.where(lane == 0, r["gate1"], jnp.where(lane == 1, r["gate2"], 0.0))
    h1 = jnp.floor(slot1 * (1.0 / SLOT_RADIX))
    h2 = jnp.floor(slot2 * (1.0 / SLOT_RADIX))
    parts = jnp.where(lane == 0, h1, jnp.where(lane == 1, slot1 - SLOT_RADIX * h1,
                      jnp.where(lane == 2, h2, jnp.where(lane == 3, slot2 - SLOT_RADIX * h2, 0.0))))
    srow_ref[...] = _dot_nt(sel_ref[...], parts.astype(BF16))


def _mixer(x2d, g_mix, w_main, w_gates, w_if_c, w_if_t, params, batch, seq):
    T, D = x2d.shape
    ts = min(MIX_TS, seq)
    nts = seq // ts
    d_pool = params["w_br_a"].shape[0]
    d_ml = params["w_br_b"].shape[0]
    dh = d_ml // N_HEADS

    idx = np.arange(ts)
    same_chunk = (idx[:, None] // CHUNK) == (idx[None, :] // CHUNK)
    tri_c = jnp.asarray((idx[None, :] <= idx[:, None]) & same_chunk, BF16)
    tri_r = jnp.asarray((idx[:, None] <= idx[None, :]) & same_chunk, BF16)
    stri = jnp.asarray(idx[None, :] < idx[:, None], BF16)
    lane_idx = np.arange(LANES)
    ut = jnp.asarray(lane_idx[:, None] < lane_idx[None, :], BF16)
    sel = jnp.asarray(np.arange(8)[:, None] == lane_idx[None, :], BF16)

    n_tiles = batch * nts
    tok = lambda g: (g, 0)
    tok_in = lambda g: (jnp.minimum(g, n_tiles - 1), 0)
    tok_next = lambda g: (jnp.minimum(g + 1, n_tiles - 1), 0)
    tok_prev = lambda g: (jnp.maximum(g - 1, 0), 0)
    tok_prev_t = lambda g: (0, jnp.maximum(g - 1, 0))
    c2 = lambda g: (0, 0)
    c3 = lambda g: (0, 0, 0)
    full = lambda a: pl.BlockSpec(a.shape, c2 if a.ndim == 2 else c3)
    consts = [params[n] for n in ("b_if", "b_if_t", "conv_q", "conv_k", "g_head", "w_pool", "pool_scale",
                                  "w_br_a", "w_br_b", "w_out", "g_ffn", "w_r_hi", "w_r_lo", "b_r")]
    consts = [g_mix, w_main, w_gates, w_if_c, w_if_t] + consts + [tri_c, tri_r, stri, ut, sel]
    nm, ng = w_main.shape[1], w_gates.shape[1]
    z_scratch = [pltpu.VMEM((ts, nm), BF16), pltpu.VMEM((ts, ng), BF16),
                 pltpu.VMEM((ts, LANES), F32), pltpu.VMEM((16, ts), F32)]
    return pl.pallas_call(
        functools.partial(_mixer_kernel, tiles_per_seq=nts),
        grid=(n_tiles + 1,),
        in_specs=[pl.BlockSpec((ts, D), tok_in),
                  pl.BlockSpec((ts, D), tok_next)] + [full(a) for a in consts],
        out_specs=[pl.BlockSpec((ts, D), tok),
                   pl.BlockSpec((ts, D), tok),
                   pl.BlockSpec((ts, LANES), tok_prev),
                   pl.BlockSpec((ts, LANES), tok_prev),
                   pl.BlockSpec((8, LANES), tok_prev),
                   pl.BlockSpec((8, ts), tok_prev_t)],
        out_shape=[jax.ShapeDtypeStruct((T + ts, D), F32),
                   jax.ShapeDtypeStruct((T + ts, D), BF16),
                   jax.ShapeDtypeStruct((T, LANES), jnp.int32),
                   jax.ShapeDtypeStruct((T, LANES), F32),
                   jax.ShapeDtypeStruct((n_tiles * 8, LANES), jnp.int32),
                   jax.ShapeDtypeStruct((8, T), F32)],
        scratch_shapes=z_scratch + z_scratch + [
                        pltpu.VMEM((ts, D), BF16),
                        pltpu.VMEM(((d_pool + 2 * d_ml) // LANES, HALO + ts, LANES), F32),
                        pltpu.VMEM((ts, d_ml), BF16),
                        pltpu.VMEM((ts, d_ml), BF16),
                        pltpu.VMEM((ts, d_ml), F32),
                        pltpu.VMEM((ts, d_pool), BF16),
                        pltpu.VMEM((N_HEADS, dh, 2 * dh), F32),
                        pltpu.VMEM((8, LANES), F32),
                        pltpu.VMEM((ts, LANES), F32)],
        compiler_params=_cparams(1),
        name="mixer",
    )(x2d, x2d, *consts)


def _for_each_piece(npieces_ref, glob_ref, tile, fn):
    base = tile * PIECES_PER_TILE
    n = npieces_ref[tile]

    def one(p):
        fn(pl.multiple_of(p * ROW_PIECE, ROW_PIECE), pl.multiple_of(glob_ref[base + p] * ROW_PIECE, ROW_PIECE))

    def group(g, carry):
        for u in range(PIECE_UNROLL):
            one(g * PIECE_UNROLL + u)
        return carry

    n_groups = lax.div(n, jnp.int32(PIECE_UNROLL))
    lax.fori_loop(0, n_groups, group, 0)
    for u in range(PIECE_UNROLL - 1):
        @pl.when(n_groups * PIECE_UNROLL + u < n)
        def _():
            one(n_groups * PIECE_UNROLL + u)


def _used_slot_groups(n_pieces):
    return lax.div(n_pieces * ROW_PIECE + (SLOT_SUB - 1), jnp.int32(SLOT_SUB))


def _dispatch_kernel(npieces_ref, glob_ref, zflag_ref,
                     xn_ref, srow_ref, buf_ref, rows_ref, zeros_ref, sem, zsem, tsem):
    tt = xn_ref.shape[0]
    sl = rows_ref.shape[1]
    n_blocks = buf_ref.shape[0] // MOE_TM
    i = pl.program_id(0)
    n_tiles = pl.num_programs(0)
    cur = lax.rem(i, 2)

    def zero_copy(b, flag):
        return pltpu.make_async_copy(zeros_ref, buf_ref.at[pl.ds(b * MOE_TM, MOE_TM)], zsem if flag == 1 else tsem)

    def for_flagged(flag, fn):
        def body(b, carry):
            @pl.when(zflag_ref[b] == flag)
            def _():
                fn(zero_copy(b, flag))
            return carry
        lax.fori_loop(0, n_blocks, body, 0)

    @pl.when(i == 0)
    def _():
        zeros_ref[...] = jnp.zeros_like(zeros_ref)
        for_flagged(1, lambda cp: cp.start())
        for_flagged(2, lambda cp: cp.start())
        for_flagged(1, lambda cp: cp.wait())

    def piece_copy(buf_slot, local_row, global_row):
        return pltpu.make_async_copy(rows_ref.at[buf_slot, pl.ds(local_row, ROW_PIECE)],
                                     buf_ref.at[pl.ds(global_row, ROW_PIECE)], sem.at[buf_slot])

    def start_pieces(tile, buf_slot):
        _for_each_piece(npieces_ref, glob_ref, tile, lambda l, g: piece_copy(buf_slot, l, g).start())

    def wait_pieces(tile, buf_slot):
        _for_each_piece(npieces_ref, glob_ref, tile, lambda l, g: piece_copy(buf_slot, l, g).wait())

    @pl.when(i >= 2)
    def _():
        wait_pieces(i - 2, cur)

    sr = srow_ref[...]
    slot1 = SLOT_RADIX * sr[0:1, :] + sr[1:2, :]
    slot2 = SLOT_RADIX * sr[2:3, :] + sr[3:4, :]
    n_sub = _used_slot_groups(npieces_ref[i])
    for k in range(TOP_K * tt // SLOT_SUB, sl // SLOT_SUB + 1):
        @pl.when(n_sub == k)
        def _(k=k):
            m = k * SLOT_SUB
            r = lax.broadcasted_iota(jnp.int32, (m, tt), 0).astype(F32)
            sel = jnp.where((r == slot1) | (r == slot2), 1.0, 0.0).astype(BF16)
            rows_ref[cur, :m, :] = _dot(sel, xn_ref[...]).astype(BF16)
    start_pieces(i, cur)

    @pl.when(i == n_tiles - 1)
    def _():
        @pl.when(i >= 1)
        def _():
            wait_pieces(i - 1, 1 - cur)
        wait_pieces(i, cur)
        for_flagged(2, lambda cp: cp.wait())


def _dispatch(xn2, srow, npieces, piece_glob, zflag, n_rows):
    T, D = srow.shape[1], xn2.shape[1]
    tt = MIX_TS
    return pl.pallas_call(
        _dispatch_kernel,
        grid_spec=pltpu.PrefetchScalarGridSpec(
            num_scalar_prefetch=3,
            grid=(T // tt,),
            in_specs=[pl.BlockSpec((tt, D), lambda i, *_: (i, 0)),
                      pl.BlockSpec((8, tt), lambda i, *_: (0, i))],
            out_specs=pl.BlockSpec(memory_space=pl.ANY),
            scratch_shapes=[pltpu.VMEM((2, MOE_SL, D), BF16),
                            pltpu.VMEM((MOE_TM, D), BF16),
                            pltpu.SemaphoreType.DMA((2,)),
                            pltpu.SemaphoreType.DMA(()),
                            pltpu.SemaphoreType.DMA(())]),
        out_shape=jax.ShapeDtypeStruct((n_rows, D), BF16),
        compiler_params=_cparams(1),
        name="dispatch",
    )(npieces, piece_glob, zflag, xn2, srow)


def _experts_kernel(blk_e_ref, nused_ref, nsub_ref, first_ref, next_e_ref, slot_ref,
                    x_ref, wg_hbm, wu_hbm, wd_hbm, y_ref,
                    wg32_ref, wu32_ref, wd32_ref, wgb_ref, wub_ref, wdb_ref, sem):
    i = pl.program_id(0)
    used = i < nused_ref[0]
    n_sub = nsub_ref[i]
    landing = ((wg_hbm, wg32_ref), (wu_hbm, wu32_ref), (wd_hbm, wd32_ref))

    def weight_copies(e, s):
        return [pltpu.make_async_copy(hbm.at[e], vmem.at[s], sem.at[s, n]) for n, (hbm, vmem) in enumerate(landing)]

    @pl.when(used & (i == 0))
    def _():
        for cp in weight_copies(blk_e_ref[0], 0):
            cp.start()

    @pl.when(used & (first_ref[i] > 0))
    def _():
        s = slot_ref[i]
        for cp in weight_copies(blk_e_ref[i], s):
            cp.wait()
        wgb_ref[...] = wg32_ref[s].astype(BF16)
        wub_ref[...] = wu32_ref[s].astype(BF16)
        wdb_ref[...] = wd32_ref[s].astype(BF16)

        @pl.when(next_e_ref[i] >= 0)
        def _():
            for cp in weight_copies(next_e_ref[i], 1 - s):
                cp.start()

    for k in range(1, MOE_TM // EXPERT_SUB + 1):
        @pl.when(used & (n_sub == k))
        def _(k=k):
            m = k * EXPERT_SUB
            x = x_ref[:m, :]
            hg = _dot(x, wgb_ref[...])
            hu = _dot(x, wub_ref[...])
            hid = (_silu(hg) * hu).astype(BF16)
            y_ref[:m, :] = _dot(hid, wdb_ref[...]).astype(BF16)
            if m < MOE_TM:
                y_ref[m:, :] = jnp.zeros((MOE_TM - m, y_ref.shape[1]), BF16)


def _experts(buf, blk_e, nused, nsub, run_first, next_e, run_slot, w_gate, w_up, w_down):
    R, D = buf.shape
    de = w_gate.shape[2]
    n_blocks = R // MOE_TM
    row_map = lambda i, be, nu, *_: (jnp.minimum(i, nu[0] - 1), 0)
    return pl.pallas_call(
        _experts_kernel,
        grid_spec=pltpu.PrefetchScalarGridSpec(
            num_scalar_prefetch=6,
            grid=(n_blocks,),
            in_specs=[pl.BlockSpec((MOE_TM, D), row_map),
                      pl.BlockSpec(memory_space=pl.ANY),
                      pl.BlockSpec(memory_space=pl.ANY),
                      pl.BlockSpec(memory_space=pl.ANY)],
            out_specs=pl.BlockSpec((MOE_TM, D), row_map),
            scratch_shapes=[pltpu.VMEM((2, D, de), F32), pltpu.VMEM((2, D, de), F32), pltpu.VMEM((2, de, D), F32),
                            pltpu.VMEM((D, de), BF16), pltpu.VMEM((D, de), BF16), pltpu.VMEM((de, D), BF16),
                            pltpu.SemaphoreType.DMA((2, 3))]),
        out_shape=jax.ShapeDtypeStruct((R, D), BF16),
        input_output_aliases={6: 0},
        compiler_params=_cparams(1),
        name="experts",
    )(blk_e, nused, nsub, run_first, next_e, run_slot, buf, w_gate, w_up, w_down)


def _combine_kernel(npieces_ref, glob_ref, x2_ref, rti_ref, rtf_ref, gfin_ref, yb_ref, out_ref,
                    rows_ref, sem):
    tt = x2_ref.shape[0]
    sl = rows_ref.shape[1]
    i = pl.program_id(0)
    n_tiles = pl.num_programs(0)
    cur = lax.rem(i, 2)

    def piece_copy(buf_slot, local_row, global_row):
        return pltpu.make_async_copy(yb_ref.at[pl.ds(global_row, ROW_PIECE)],
                                     rows_ref.at[buf_slot, pl.ds(local_row, ROW_PIECE)], sem.at[buf_slot])

    def start_pieces(tile, buf_slot):
        _for_each_piece(npieces_ref, glob_ref, tile, lambda l, g: piece_copy(buf_slot, l, g).start())

    def wait_pieces(tile, buf_slot):
        _for_each_piece(npieces_ref, glob_ref, tile, lambda l, g: piece_copy(buf_slot, l, g).wait())

    @pl.when(i == 0)
    def _():
        rows_ref[...] = jnp.zeros_like(rows_ref)
        start_pieces(0, 0)

    @pl.when(i + 1 < n_tiles)
    def _():
        start_pieces(i + 1, 1 - cur)

    wait_pieces(i, cur)

    rti = rti_ref[...]
    rtf = rtf_ref[...]
    slot1 = rti[:, 2:3]
    slot2 = rti[:, 3:4]
    n_sub = _used_slot_groups(npieces_ref[i])
    for k in range(TOP_K * tt // SLOT_SUB, sl // SLOT_SUB + 1):
        @pl.when(n_sub == k)
        def _(k=k):
            m = k * SLOT_SUB
            lane = lax.broadcasted_iota(jnp.int32, (tt, m), 1)
            g = jnp.where(lane == slot1, rtf[:, 0:1], jnp.where(lane == slot2, rtf[:, 1:2], 0.0)).astype(BF16)
            y = x2_ref[...] + _dot(g, rows_ref[cur, :m, :])
            ms = jnp.mean(y * y, axis=-1, keepdims=True)
            out_ref[...] = y * lax.rsqrt(ms + EPS) * gfin_ref[...]


def _combine(x2, rti, rtf, g_final, yb, npieces, piece_glob):
    T, D = rti.shape[0], x2.shape[1]
    tt = MIX_TS
    tok = lambda i, *_: (i, 0)
    return pl.pallas_call(
        _combine_kernel,
        grid_spec=pltpu.PrefetchScalarGridSpec(
            num_scalar_prefetch=2,
            grid=(T // tt,),
            in_specs=[pl.BlockSpec((tt, D), tok),
                      pl.BlockSpec((tt, LANES), tok),
                      pl.BlockSpec((tt, LANES), tok),
                      pl.BlockSpec((1, D), lambda i, *_: (0, 0)),
                      pl.BlockSpec(memory_space=pl.ANY)],
            out_specs=pl.BlockSpec((tt, D), tok),
            scratch_shapes=[pltpu.VMEM((2, MOE_SL, D), BF16),
                            pltpu.SemaphoreType.DMA((2,))]),
        out_shape=jax.ShapeDtypeStruct((T, D), F32),
        compiler_params=_cparams(1),
        name="combine",
    )(npieces, piece_glob, x2, rti, rtf, g_final, yb)


def _pad_lanes(a, width=LANES):
    return jnp.pad(a, ((0, 0), (0, width - a.shape[1])))


def kernel(x, g_mix, w_in, b_if, conv_q, conv_k, g_head, w_pool, pool_scale, w_br_a, w_br_b, w_out,
           g_ffn, w_rg, b_rg, w_re, b_re, w_e_gate, w_e_up, w_e_down, g_final):
    B, S, D = x.shape
    T = B * S
    assert g_mix.shape[0] == 1, "single-layer block"
    assert S % MIX_TS == 0
    d_pool = w_br_a.shape[1]
    d_ml = w_br_b.shape[1]
    x2d = x.reshape(T, D)

    n_main = d_pool + 4 * d_ml
    w_l = w_in[0]
    w_main = w_l[:, :n_main].astype(BF16)
    w_if = w_l[:, n_main:n_main + 2 * N_HEADS]
    w_gates = w_l[:, n_main + 2 * N_HEADS:].astype(BF16)
    w_if_c = _pad_lanes(w_if).astype(BF16)
    w_if_t = jnp.pad(w_if.T, ((0, 16 - 2 * N_HEADS), (0, 0))).astype(BF16)
    w_r = _pad_lanes(jnp.concatenate([w_rg[0], w_re[0]], axis=1))
    w_r_hi = w_r.astype(BF16)
    w_r_lo = (w_r - w_r_hi.astype(F32)).astype(BF16)
    params = {
        "b_if": _pad_lanes(b_if[0][None, :]),
        "b_if_t": jnp.pad(b_if[0][:, None], ((0, 16 - 2 * N_HEADS), (0, 0))),
        "conv_q": conv_q[0], "conv_k": conv_k[0],
        "g_head": g_head[0][None, :],
        "w_pool": w_pool[0].astype(BF16),
        "pool_scale": pool_scale[0][None, :],
        "w_br_a": w_br_a[0].astype(BF16), "w_br_b": w_br_b[0].astype(BF16),
        "w_out": w_out[0].astype(BF16),
        "g_ffn": g_ffn[0][None, :],
        "w_r_hi": w_r_hi, "w_r_lo": w_r_lo,
        "b_r": _pad_lanes(jnp.concatenate([b_rg[0], b_re[0]])[None, :]),
    }

    x2, xn2, rti, rtf, tstat, srow = _mixer(x2d, g_mix[0][None, :], w_main, w_gates, w_if_c, w_if_t, params, B, S)

    n_tiles = T // MIX_TS
    pcs = tstat.reshape(n_tiles, 8, LANES)[:, 0, ROUTER_LANE0:ROUTER_LANE0 + N_EXPERTS]
    piece_loc = jnp.cumsum(pcs, axis=1) - pcs
    rows_e = jnp.sum(pcs, axis=0) * ROW_PIECE
    padded = (rows_e + MOE_TM - 1) // MOE_TM * MOE_TM
    pend = jnp.cumsum(padded)
    poff = pend - padded
    piece_glob = poff[None, :] // ROW_PIECE + jnp.cumsum(pcs, axis=0) - pcs
    n_rows = n_tiles * MOE_SL + N_EXPERTS * MOE_TM
    n_blocks = n_rows // MOE_TM
    nused = (pend[-1:] // MOE_TM).astype(jnp.int32)
    blk_ids = jnp.arange(n_blocks, dtype=jnp.int32)
    blk_e = jnp.sum((pend[None, :] <= blk_ids[:, None] * MOE_TM).astype(jnp.int32), axis=1)
    blk_e = jnp.minimum(blk_e, N_EXPERTS - 1)
    zflag = jnp.where(blk_ids >= nused[0], 2, ((blk_ids + 1) * MOE_TM == pend[blk_e]).astype(jnp.int32)).astype(jnp.int32)
    p_ids = jnp.arange(PIECES_PER_TILE, dtype=jnp.int32)
    piece_end = piece_loc + pcs
    e_of_p = jnp.minimum(jnp.sum((piece_end[:, None, :] <= p_ids[None, :, None]).astype(jnp.int32), axis=2),
                         N_EXPERTS - 1)
    e_ids = jnp.arange(N_EXPERTS, dtype=jnp.int32)
    shift = jnp.sum(jnp.where(e_of_p[:, :, None] == e_ids[None, None, :], (piece_glob - piece_loc)[:, None, :], 0),
                    axis=2)
    glob_of_p = (shift + p_ids[None, :]).astype(jnp.int32)
    glob_of_p = glob_of_p.reshape(n_tiles * PIECES_PER_TILE)
    npieces = jnp.sum(pcs, axis=1).astype(jnp.int32)

    buf = _dispatch(xn2, srow, npieces, glob_of_p, zflag, n_rows)
    data_end = jnp.sum(jnp.where(blk_e[:, None] == jnp.arange(N_EXPERTS)[None, :], (poff + rows_e)[None, :], 0), axis=1)
    rows_in_blk = jnp.clip(data_end - blk_ids * MOE_TM, 0, MOE_TM)
    nsub = ((rows_in_blk + EXPERT_SUB - 1) // EXPERT_SUB).astype(jnp.int32)
    is_used = blk_ids < nused[0]
    run_first = (is_used & ((blk_ids == 0) | (blk_e != jnp.roll(blk_e, 1)))).astype(jnp.int32)
    run_slot = ((jnp.cumsum(run_first) - 1) % 2).astype(jnp.int32)
    e_ids32 = jnp.arange(N_EXPERTS, dtype=jnp.int32)
    later_nonempty = (e_ids32[None, :] > e_ids32[:, None]) & (rows_e[None, :] > 0)
    next_of_e = jnp.min(jnp.where(later_nonempty, e_ids32[None, :], N_EXPERTS), axis=1)
    next_of_e = jnp.where(next_of_e == N_EXPERTS, -1, next_of_e)
    next_e = jnp.sum(jnp.where(blk_e[:, None] == e_ids32[None, :], next_of_e[None, :], 0), axis=1).astype(jnp.int32)
    yb = _experts(buf, blk_e, nused, nsub, run_first, next_e, run_slot, w_e_gate[0], w_e_up[0], w_e_down[0])
    out = _combine(x2, rti, rtf, g_final[None, :], yb, npieces, glob_of_p)
    return out.reshape(B, S, D)
```

```python
import functools

import numpy as np
import jax
import jax.numpy as jnp
from jax import lax
from jax.experimental import pallas as pl
from jax.experimental.pallas import tpu as pltpu

F32 = jnp.float32
BF16 = jnp.bfloat16

CHUNK = 64
POOL_WINDOWS = (2, 4, 8, 16)
N_HEADS = 4
CONV_K = 4
N_GROUPS = 4
EXPERTS_PER_GROUP = 8
N_EXPERTS = N_GROUPS * EXPERTS_PER_GROUP
TOP_K = 2
EPS = 1e-6

LANES = 128
HALO = 16
ROUTER_LANE0 = N_GROUPS

INPROJ_TN = 256
MIX_TS = 256
MOE_TM = 512
ROW_PIECE = 16
MOE_SL = TOP_K * MIX_TS + N_EXPERTS * ROW_PIECE
PIECES_PER_TILE = MOE_SL // ROW_PIECE
MOE_STEP_TILES = 2
SLOT_SUB = 128
PIECE_UNROLL = 4
EXPERT_SUB = 128
SLOT_RADIX = 16
VMEM_LIMIT = 56 * 1024 * 1024


def _cparams(n_axes):
    return pltpu.CompilerParams(dimension_semantics=("arbitrary",) * n_axes,
                                vmem_limit_bytes=VMEM_LIMIT)


def _sigmoid(v):
    return 0.5 * jnp.tanh(0.5 * v) + 0.5


def _silu(v):
    return v * _sigmoid(v)


def _log_sigmoid(v):
    return jnp.minimum(v, 0.0) - jnp.log1p(jnp.exp(-jnp.abs(v)))


def _split3(v):
    hi = v.astype(BF16)
    r1 = v - hi.astype(F32)
    mid = r1.astype(BF16)
    lo = (r1 - mid.astype(F32)).astype(BF16)
    return hi, mid, lo


def _dot(a, b):
    return jnp.dot(a, b, preferred_element_type=F32)


def _dot_nt(a, b):
    return lax.dot_general(a, b, (((1,), (1,)), ((), ())), preferred_element_type=F32)


def _dot_tn(a, b):
    return lax.dot_general(a, b, (((0,), (0,)), ((), ())), preferred_element_type=F32)


def _inproj_steps(x_ref, g_ref, w_refs, z_refs, xn_ref):
    wm_ref, wg_ref, wif_ref, wift_ref = w_refs
    zm_ref, zg_ref, zif_ref, zift_ref = z_refs

    def norm():
        x = x_ref[...]
        ms = jnp.mean(x * x, axis=-1, keepdims=True)
        xn_ref[...] = (x * lax.rsqrt(ms + EPS) * g_ref[...]).astype(BF16)

    def block(w_ref, z_ref, c0):
        def run():
            cols = slice(c0, c0 + INPROJ_TN)
            z_ref[:, cols] = _dot(xn_ref[...], w_ref[:, cols]).astype(BF16)
        return run

    def gates():
        zif_ref[...] = _dot(xn_ref[...], wif_ref[...])
        zift_ref[...] = _dot_nt(wift_ref[...], xn_ref[...])

    steps = [norm, gates]
    steps += [block(wm_ref, zm_ref, c0) for c0 in range(0, zm_ref.shape[1], INPROJ_TN)]
    steps += [block(wg_ref, zg_ref, c0) for c0 in range(0, zg_ref.shape[1], INPROJ_TN)]
    return steps


def _mixer_kernel(x_ref, xnext_ref, gmix_ref, wm_ref, wg_ref, wif_ref, wift_ref,
                  bif_ref, bift_ref, convq_ref, convk_ref, ghead_ref, wpool_ref, pscale_ref,
                  wa_ref, wb_ref, wo_ref, gffn_ref, wrh_ref, wrl_ref, br_ref,
                  tric_ref, trir_ref, stri_ref, ut_ref, sel_ref,
                  x2_ref, xn2_ref, rti_ref, rtf_ref, tstat_ref, srow_ref,
                  zm_ref, zg_ref, zif_ref, zift_ref, zm_nxt, zg_nxt, zif_nxt, zift_nxt, xn_ref,
                  ext_ref, q_ref, k_ref, h_ref, pool_ref, cst_ref, mst_ref, lg_ref, *, tiles_per_seq):
    ts = x_ref.shape[0]
    d_pool = wa_ref.shape[0]
    d_ml = wb_ref.shape[0]
    dh = d_ml // N_HEADS
    n_chunks = ts // CHUNK
    g_step = pl.program_id(0)
    j = lax.rem(g_step, tiles_per_seq)
    w_in_refs = (wm_ref, wg_ref, wif_ref, wift_ref)
    z_cur = (zm_ref, zg_ref, zif_ref, zift_ref)
    z_nxt = (zm_nxt, zg_nxt, zif_nxt, zift_nxt)
    first = g_step == 0

    @pl.when(first)
    def _():
        for step in _inproj_steps(x_ref, gmix_ref, w_in_refs, z_cur, xn_ref):
            step()
        lg_ref[...] = jnp.zeros_like(lg_ref)

    @pl.when(jnp.logical_not(first))
    def _():
        for dst, src in zip(z_cur, z_nxt):
            dst[...] = src[...]

    @pl.when(j == 0)
    def _():
        ext_ref[:, :HALO, :] = jnp.zeros((ext_ref.shape[0], HALO, LANES), F32)
        cst_ref[...] = jnp.zeros_like(cst_ref)
        mst_ref[...] = jnp.zeros_like(mst_ref)

    pending = _inproj_steps(xnext_ref, gmix_ref, w_in_refs, z_nxt, xn_ref)

    def project_some(n=1):
        for _ in range(min(n, len(pending))):
            pending.pop(0)()

    project_some(2)

    routed = _route_select(lg_ref[...])

    row = lax.broadcasted_iota(jnp.int32, (ts, LANES), 0)
    pos1 = (row + j * ts + 1).astype(F32)

    def history(cg):
        cur = zm_ref[:, cg * LANES:(cg + 1) * LANES].astype(F32)
        ext_ref[cg, HALO:, :] = cur
        return cur, lambda s: ext_ref[cg, HALO - s:HALO - s + ts, :]

    def keep_history(cg, cur):
        ext_ref[cg, :HALO, :] = cur[ts - HALO:, :]

    n_pool_groups = d_pool // LANES
    for g in range(n_pool_groups):
        w = POOL_WINDOWS[g]
        cur, shifted = history(g)
        win = cur
        for s in range(1, w):
            win = win + shifted(s)
        keep_history(g, cur)
        cnt = jnp.minimum(pos1, float(w))
        d = win / cnt - cur
        y = _dot(d.astype(BF16), wpool_ref[g]) * pscale_ref[:, g * LANES:(g + 1) * LANES]
        pool_ref[:, g * LANES:(g + 1) * LANES] = y.astype(BF16)
        project_some()

    n_ml_groups = d_ml // LANES
    for which, (cw_ref, dst_ref, scale) in enumerate(((convq_ref, q_ref, 1.0), (convk_ref, k_ref, dh ** -0.5))):
        for g in range(n_ml_groups):
            cols = slice(g * LANES, (g + 1) * LANES)
            cg = n_pool_groups + which * n_ml_groups + g
            cur, shifted = history(cg)
            acc = cur * cw_ref[CONV_K - 1:CONV_K, cols]
            for sft in range(1, CONV_K):
                acc = acc + shifted(sft) * cw_ref[CONV_K - 1 - sft:CONV_K - sft, cols]
            keep_history(cg, cur)
            dst_ref[:, cols] = (_silu(acc) * scale).astype(BF16)
        project_some()

    _route_slots(routed, stri_ref, ut_ref, sel_ref, rti_ref, rtf_ref, tstat_ref, srow_ref)
    project_some()

    zc = zif_ref[...] + bif_ref[...]
    lf_c = _log_sigmoid(zc)
    bc = sum(_dot(tric_ref[...], p) for p in _split3(lf_c))
    zr = zift_ref[...] + bift_ref[...]
    lf_r = _log_sigmoid(zr)
    br = sum(_dot(p, trir_ref[...]) for p in _split3(lf_r))
    project_some(2)

    ti = lax.broadcasted_iota(jnp.int32, (CHUNK, CHUNK), 0)
    si = lax.broadcasted_iota(jnp.int32, (CHUNK, CHUNK), 1)
    causal = si <= ti
    ones_blk = jnp.ones((CHUNK, dh), BF16)
    v0 = d_pool + 2 * d_ml
    ig_rep = [jnp.broadcast_to(zc[:, h:h + 1], (ts, dh)) for h in range(N_HEADS)]
    bt_rep = [jnp.broadcast_to(bc[:, N_HEADS + h:N_HEADS + h + 1], (ts, dh)) for h in range(N_HEADS)]

    m_state = [mst_ref[h:h + 1, :] for h in range(N_HEADS)]
    c_state = [cst_ref[h] for h in range(N_HEADS)]
    def stage_scores(c):
        rs = slice(c * CHUNK, (c + 1) * CHUNK)
        out = []
        for h in range(N_HEADS):
            hs = slice(h * dh, (h + 1) * dh)
            q = q_ref[rs, hs]
            k = k_ref[rs, hs]
            bt = bt_rep[h][rs, :]
            r_row = zr[h:h + 1, rs] - br[N_HEADS + h:N_HEADS + h + 1, rs]
            dmat = jnp.where(causal, bt[:, :CHUNK] + r_row, -jnp.inf)
            out.append(dict(q=q, k=k, bt=bt, dmat=dmat, qk=_dot_nt(q, k),
                            m_intra=jnp.max(dmat, axis=-1, keepdims=True)))
        return out

    def stage_state(c, st):
        rs = slice(c * CHUNK, (c + 1) * CHUNK)
        for h in range(N_HEADS):
            s = st[h]
            bt, k = s["bt"], s["k"]
            m_prev, c_prev = m_state[h], c_state[h]
            v_aug = jnp.concatenate([zm_ref[rs, v0 + h * dh:v0 + (h + 1) * dh], ones_blk], axis=-1)
            igc = ig_rep[h][rs, :]
            b_last = bt[CHUNK - 1:CHUNK, :]
            a_log = b_last - bt + igc
            a_max = jnp.max(a_log, axis=0, keepdims=True)
            m_new = jnp.maximum(b_last + m_prev, a_max)
            kw = (k.astype(F32) * jnp.exp(a_log - m_new)).astype(BF16)
            decay = jnp.exp(b_last + m_prev - m_new)
            s.update(v_aug=v_aug, m_prev=m_prev, qc=_dot(s["q"], c_prev.astype(BF16)))
            c_state[h] = jnp.concatenate([decay, decay], axis=-1) * c_prev + _dot_tn(kw, v_aug)
            m_state[h] = m_new

    def stage_values(c, st):
        rs = slice(c * CHUNK, (c + 1) * CHUNK)
        for h in range(N_HEADS):
            s = st[h]
            hs = slice(h * dh, (h + 1) * dh)
            inter = s["bt"] + s["m_prev"]
            m_t = jnp.maximum(inter, s["m_intra"])
            w_inter = jnp.exp(inter - m_t)
            smat = s["qk"] * jnp.exp(s["dmat"] - m_t[:, :CHUNK])
            sv = _dot(smat.astype(BF16), s["v_aug"])
            qc = s["qc"]
            nq = w_inter * qc[:, dh:] + sv[:, dh:]
            den = jnp.maximum(jnp.abs(nq), jnp.exp(-m_t))
            h_ref[rs, hs] = (w_inter * qc[:, :dh] + sv[:, :dh]) / den

    staged = stage_scores(0)
    for c in range(n_chunks):
        stage_state(c, staged)
        project_some()
        nxt = stage_scores(c + 1) if c + 1 < n_chunks else None
        project_some()
        stage_values(c, staged)
        staged = nxt
    for h in range(N_HEADS):
        cst_ref[h] = c_state[h]
        mst_ref[h:h + 1, :] = m_state[h]

    o0 = v0 + d_ml
    for h in range(N_HEADS):
        hs = slice(h * dh, (h + 1) * dh)
        hv = h_ref[:, hs]
        mu = jnp.mean(hv, axis=-1, keepdims=True)
        hc = hv - mu
        var = jnp.mean(hc * hc, axis=-1, keepdims=True)
        hn = hc * lax.rsqrt(var + EPS) * ghead_ref[:, hs]
        og = _sigmoid(zm_ref[:, o0 + h * dh:o0 + (h + 1) * dh].astype(F32))
        q_ref[:, hs] = (og * hn).astype(BF16)
    y_a = _dot(pool_ref[...], wa_ref[...])
    y_b = _dot(q_ref[...], wb_ref[...])
    d_model = x_ref.shape[1]
    ga = _sigmoid(zg_ref[:, :d_model].astype(F32))
    gb = _sigmoid(zg_ref[:, d_model:].astype(F32))
    merged = (ga * y_a + gb * y_b).astype(BF16)
    x2 = x_ref[...] + _dot(merged, wo_ref[...])
    x2_ref[...] = x2
    project_some(len(pending))

    ms = jnp.mean(x2 * x2, axis=-1, keepdims=True)
    xn2 = x2 * lax.rsqrt(ms + EPS) * gffn_ref[...]
    xh = xn2.astype(BF16)
    xn2_ref[...] = xh
    xl = (xn2 - xh.astype(F32)).astype(BF16)
    lg_ref[...] = _dot(xh, wrh_ref[...]) + _dot(xl, wrh_ref[...]) + _dot(xh, wrl_ref[...]) + br_ref[...]
    project_some(len(pending))


def _route_select(lg):
    ts = lg.shape[0]
    lane = lax.broadcasted_iota(jnp.int32, (ts, LANES), 1)
    lanef = lane.astype(F32)
    big = float(4 * LANES)
    gl = jnp.where(lane < N_GROUPS, lg, -jnp.inf)
    gmax = jnp.max(gl, axis=-1, keepdims=True)
    g_sel = jnp.min(jnp.where(gl == gmax, lanef, big), axis=-1, keepdims=True)
    p_g = 1.0 / jnp.sum(jnp.exp(gl - gmax), axis=-1, keepdims=True)
    lo = ROUTER_LANE0 + EXPERTS_PER_GROUP * g_sel
    el = jnp.where((lanef >= lo) & (lanef < lo + EXPERTS_PER_GROUP), lg, -jnp.inf)
    m1 = jnp.max(el, axis=-1, keepdims=True)
    i1 = jnp.min(jnp.where(el == m1, lanef, big), axis=-1, keepdims=True)
    el2 = jnp.where(lanef == i1, -jnp.inf, el)
    m2 = jnp.max(el2, axis=-1, keepdims=True)
    i2 = jnp.min(jnp.where(el2 == m2, lanef, big), axis=-1, keepdims=True)
    e2x = jnp.exp(m2 - m1)
    gate1 = p_g / (1.0 + e2x)
    gate2 = p_g * e2x / (1.0 + e2x)
    return dict(lane=lane, i1=i1, i2=i2, gate1=gate1, gate2=gate2, oh1=lanef == i1, oh2=lanef == i2)


def _route_slots(r, stri_ref, ut_ref, sel_ref, rti_ref, rtf_ref, tstat_ref, srow_ref):
    lane, oh1, oh2, i1, i2 = r["lane"], r["oh1"], r["oh2"], r["i1"], r["i2"]
    ohs = jnp.where(oh1 | oh2, 1.0, 0.0)
    n_loc = jnp.sum(ohs, axis=0, keepdims=True)
    pieces = jnp.floor((n_loc + (ROW_PIECE - 1.0)) * (1.0 / ROW_PIECE))
    piece_off = _dot(jnp.broadcast_to(pieces, (8, LANES)).astype(BF16), ut_ref[...])[0:1, :]
    base = _dot(stri_ref[...], ohs.astype(BF16)) + ROW_PIECE * piece_off
    slot1 = jnp.sum(jnp.where(oh1, base, 0.0), axis=-1, keepdims=True)
    slot2 = jnp.sum(jnp.where(oh2, base, 0.0), axis=-1, keepdims=True)
    tstat_ref[...] = jnp.broadcast_to(pieces, tstat_ref.shape).astype(jnp.int32)

    rti = jnp.where(lane == 0, i1 - ROUTER_LANE0,
                    jnp.where(lane == 1, i2 - ROUTER_LANE0,
                              jnp.where(lane == 2, slot1, jnp.where(lane == 3, slot2, 0.0))))
    rti_ref[...] = rti.astype(jnp.int32)
    rtf_ref[...] = jnp.where(lane == 0, r["gate1"], jnp.where(lane == 1, r["gate2"], 0.0))
    h1 = jnp.floor(slot1 * (1.0 / SLOT_RADIX))
    h2 = jnp.floor(slot2 * (1.0 / SLOT_RADIX))
    parts = jnp.where(lane == 0, h1, jnp.where(lane == 1, slot1 - SLOT_RADIX * h1,
                      jnp.where(lane == 2, h2, jnp.where(lane == 3, slot2 - SLOT_RADIX * h2, 0.0))))
    srow_ref[...] = _dot_nt(sel_ref[...], parts.astype(BF16))


def _mixer(x2d, g_mix, w_main, w_gates, w_if_c, w_if_t, params, batch, seq):
    T, D = x2d.shape
    ts = min(MIX_TS, seq)
    nts = seq // ts
    d_pool = params["w_br_a"].shape[0]
    d_ml = params["w_br_b"].shape[0]
    dh = d_ml // N_HEADS

    idx = np.arange(ts)
    same_chunk = (idx[:, None] // CHUNK) == (idx[None, :] // CHUNK)
    tri_c = jnp.asarray((idx[None, :] <= idx[:, None]) & same_chunk, BF16)
    tri_r = jnp.asarray((idx[:, None] <= idx[None, :]) & same_chunk, BF16)
    stri = jnp.asarray(idx[None, :] < idx[:, None], BF16)
    lane_idx = np.arange(LANES)
    ut = jnp.asarray(lane_idx[:, None] < lane_idx[None, :], BF16)
    sel = jnp.asarray(np.arange(8)[:, None] == lane_idx[None, :], BF16)

    n_tiles = batch * nts
    tok = lambda g: (g, 0)
    tok_in = lambda g: (jnp.minimum(g, n_tiles - 1), 0)
    tok_next = lambda g: (jnp.minimum(g + 1, n_tiles - 1), 0)
    tok_prev = lambda g: (jnp.maximum(g - 1, 0), 0)
    tok_prev_t = lambda g: (0, jnp.maximum(g - 1, 0))
    c2 = lambda g: (0, 0)
    c3 = lambda g: (0, 0, 0)
    full = lambda a: pl.BlockSpec(a.shape, c2 if a.ndim == 2 else c3)
    consts = [params[n] for n in ("b_if", "b_if_t", "conv_q", "conv_k", "g_head", "w_pool", "pool_scale",
                                  "w_br_a", "w_br_b", "w_out", "g_ffn", "w_r_hi", "w_r_lo", "b_r")]
    consts = [g_mix, w_main, w_gates, w_if_c, w_if_t] + consts + [tri_c, tri_r, stri, ut, sel]
    nm, ng = w_main.shape[1], w_gates.shape[1]
    z_scratch = [pltpu.VMEM((ts, nm), BF16), pltpu.VMEM((ts, ng), BF16),
                 pltpu.VMEM((ts, LANES), F32), pltpu.VMEM((16, ts), F32)]
    return pl.pallas_call(
        functools.partial(_mixer_kernel, tiles_per_seq=nts),
        grid=(n_tiles + 1,),
        in_specs=[pl.BlockSpec((ts, D), tok_in),
                  pl.BlockSpec((ts, D), tok_next)] + [full(a) for a in consts],
        out_specs=[pl.BlockSpec((ts, D), tok),
                   pl.BlockSpec((ts, D), tok),
                   pl.BlockSpec((ts, LANES), tok_prev),
                   pl.BlockSpec((ts, LANES), tok_prev),
                   pl.BlockSpec((8, LANES), tok_prev),
                   pl.BlockSpec((8, ts), tok_prev_t)],
        out_shape=[jax.ShapeDtypeStruct((T + ts, D), F32),
                   jax.ShapeDtypeStruct((T + ts, D), BF16),
                   jax.ShapeDtypeStruct((T, LANES), jnp.int32),
                   jax.ShapeDtypeStruct((T, LANES), F32),
                   jax.ShapeDtypeStruct((n_tiles * 8, LANES), jnp.int32),
                   jax.ShapeDtypeStruct((8, T), F32)],
        scratch_shapes=z_scratch + z_scratch + [
                        pltpu.VMEM((ts, D), BF16),
                        pltpu.VMEM(((d_pool + 2 * d_ml) // LANES, HALO + ts, LANES), F32),
                        pltpu.VMEM((ts, d_ml), BF16),
                        pltpu.VMEM((ts, d_ml), BF16),
                        pltpu.VMEM((ts, d_ml), F32),
                        pltpu.VMEM((ts, d_pool), BF16),
                        pltpu.VMEM((N_HEADS, dh, 2 * dh), F32),
                        pltpu.VMEM((8, LANES), F32),
                        pltpu.VMEM((ts, LANES), F32)],
        compiler_params=_cparams(1),
        name="mixer",
    )(x2d, x2d, *consts)


def _for_each_piece(npieces_ref, glob_ref, tile, fn):
    base = tile * PIECES_PER_TILE
    n = npieces_ref[tile]

    def one(p):
        fn(pl.multiple_of(p * ROW_PIECE, ROW_PIECE), pl.multiple_of(glob_ref[base + p] * ROW_PIECE, ROW_PIECE))

    def group(g, carry):
        for u in range(PIECE_UNROLL):
            one(g * PIECE_UNROLL + u)
        return carry

    n_groups = lax.div(n, jnp.int32(PIECE_UNROLL))
    lax.fori_loop(0, n_groups, group, 0)
    for u in range(PIECE_UNROLL - 1):
        @pl.when(n_groups * PIECE_UNROLL + u < n)
        def _():
            one(n_groups * PIECE_UNROLL + u)


def _used_slot_groups(n_pieces):
    return lax.div(n_pieces * ROW_PIECE + (SLOT_SUB - 1), jnp.int32(SLOT_SUB))


def _dispatch_kernel(npieces_ref, glob_ref, zflag_ref,
                     xn_ref, srow_ref, buf_ref, rows_ref, zeros_ref, sem, zsem, tsem):
    tt = xn_ref.shape[0] // MOE_STEP_TILES
    sl = rows_ref.shape[1] // MOE_STEP_TILES
    n_blocks = buf_ref.shape[0] // MOE_TM
    i = pl.program_id(0)
    n_steps = pl.num_programs(0)
    cur = lax.rem(i, 2)

    def zero_copy(b, flag):
        return pltpu.make_async_copy(zeros_ref, buf_ref.at[pl.ds(b * MOE_TM, MOE_TM)], zsem if flag == 1 else tsem)

    def for_flagged(flag, fn):
        def body(b, carry):
            @pl.when(zflag_ref[b] == flag)
            def _():
                fn(zero_copy(b, flag))
            return carry
        lax.fori_loop(0, n_blocks, body, 0)

    @pl.when(i == 0)
    def _():
        zeros_ref[...] = jnp.zeros_like(zeros_ref)
        for_flagged(1, lambda cp: cp.start())
        for_flagged(2, lambda cp: cp.start())
        for_flagged(1, lambda cp: cp.wait())

    def piece_copy(buf_slot, sub, local_row, global_row):
        return pltpu.make_async_copy(rows_ref.at[buf_slot, pl.ds(sub * sl + local_row, ROW_PIECE)],
                                     buf_ref.at[pl.ds(global_row, ROW_PIECE)], sem.at[buf_slot])

    def for_step_pieces(step, buf_slot, act):
        for sub in range(MOE_STEP_TILES):
            _for_each_piece(npieces_ref, glob_ref, step * MOE_STEP_TILES + sub,
                            lambda l, g, sub=sub: act(piece_copy(buf_slot, sub, l, g)))

    @pl.when(i >= 2)
    def _():
        for_step_pieces(i - 2, cur, lambda cp: cp.wait())

    for sub in range(MOE_STEP_TILES):
        sr = srow_ref[:, sub * tt:(sub + 1) * tt]
        slot1 = SLOT_RADIX * sr[0:1, :] + sr[1:2, :]
        slot2 = SLOT_RADIX * sr[2:3, :] + sr[3:4, :]
        n_sub = _used_slot_groups(npieces_ref[i * MOE_STEP_TILES + sub])
        for k in range(TOP_K * tt // SLOT_SUB, sl // SLOT_SUB + 1):
            @pl.when(n_sub == k)
            def _(k=k, sub=sub, slot1=slot1, slot2=slot2):
                m = k * SLOT_SUB
                r = lax.broadcasted_iota(jnp.int32, (m, tt), 0).astype(F32)
                sel = jnp.where((r == slot1) | (r == slot2), 1.0, 0.0).astype(BF16)
                rows_ref[cur, sub * sl:sub * sl + m, :] = _dot(sel, xn_ref[sub * tt:(sub + 1) * tt, :]).astype(BF16)
    for_step_pieces(i, cur, lambda cp: cp.start())

    @pl.when(i == n_steps - 1)
    def _():
        @pl.when(i >= 1)
        def _():
            for_step_pieces(i - 1, 1 - cur, lambda cp: cp.wait())
        for_step_pieces(i, cur, lambda cp: cp.wait())
        for_flagged(2, lambda cp: cp.wait())


def _dispatch(xn2, srow, npieces, piece_glob, zflag, n_rows):
    T, D = srow.shape[1], xn2.shape[1]
    tt = MOE_STEP_TILES * MIX_TS
    return pl.pallas_call(
        _dispatch_kernel,
        grid_spec=pltpu.PrefetchScalarGridSpec(
            num_scalar_prefetch=3,
            grid=(T // tt,),
            in_specs=[pl.BlockSpec((tt, D), lambda i, *_: (i, 0)),
                      pl.BlockSpec((8, tt), lambda i, *_: (0, i))],
            out_specs=pl.BlockSpec(memory_space=pl.ANY),
            scratch_shapes=[pltpu.VMEM((2, MOE_STEP_TILES * MOE_SL, D), BF16),
                            pltpu.VMEM((MOE_TM, D), BF16),
                            pltpu.SemaphoreType.DMA((2,)),
                            pltpu.SemaphoreType.DMA(()),
                            pltpu.SemaphoreType.DMA(())]),
        out_shape=jax.ShapeDtypeStruct((n_rows, D), BF16),
        compiler_params=_cparams(1),
        name="dispatch",
    )(npieces, piece_glob, zflag, xn2, srow)


def _experts_kernel(blk_e_ref, nused_ref, nsub_ref, first_ref, next_e_ref, slot_ref,
                    x_ref, wg_hbm, wu_hbm, wd_hbm, y_ref,
                    wg32_ref, wu32_ref, wd32_ref, wgb_ref, wub_ref, wdb_ref, sem):
    i = pl.program_id(0)
    used = i < nused_ref[0]
    n_sub = nsub_ref[i]
    landing = ((wg_hbm, wg32_ref), (wu_hbm, wu32_ref), (wd_hbm, wd32_ref))

    def weight_copies(e, s):
        return [pltpu.make_async_copy(hbm.at[e], vmem.at[s], sem.at[s, n]) for n, (hbm, vmem) in enumerate(landing)]

    @pl.when(used & (i == 0))
    def _():
        for cp in weight_copies(blk_e_ref[0], 0):
            cp.start()

    @pl.when(used & (first_ref[i] > 0))
    def _():
        s = slot_ref[i]
        for cp in weight_copies(blk_e_ref[i], s):
            cp.wait()
        wgb_ref[...] = wg32_ref[s].astype(BF16)
        wub_ref[...] = wu32_ref[s].astype(BF16)
        wdb_ref[...] = wd32_ref[s].astype(BF16)

        @pl.when(next_e_ref[i] >= 0)
        def _():
            for cp in weight_copies(next_e_ref[i], 1 - s):
                cp.start()

    for k in range(1, MOE_TM // EXPERT_SUB + 1):
        @pl.when(used & (n_sub == k))
        def _(k=k):
            m = k * EXPERT_SUB
            x = x_ref[:m, :]
            hg = _dot(x, wgb_ref[...])
            hu = _dot(x, wub_ref[...])
            hid = (_silu(hg) * hu).astype(BF16)
            y_ref[:m, :] = _dot(hid, wdb_ref[...]).astype(BF16)
            if m < MOE_TM:
                y_ref[m:, :] = jnp.zeros((MOE_TM - m, y_ref.shape[1]), BF16)


def _experts(buf, blk_e, nused, nsub, run_first, next_e, run_slot, w_gate, w_up, w_down):
    R, D = buf.shape
    de = w_gate.shape[2]
    n_blocks = R // MOE_TM
    row_map = lambda i, be, nu, *_: (jnp.minimum(i, nu[0] - 1), 0)
    return pl.pallas_call(
        _experts_kernel,
        grid_spec=pltpu.PrefetchScalarGridSpec(
            num_scalar_prefetch=6,
            grid=(n_blocks,),
            in_specs=[pl.BlockSpec((MOE_TM, D), row_map),
                      pl.BlockSpec(memory_space=pl.ANY),
                      pl.BlockSpec(memory_space=pl.ANY),
                      pl.BlockSpec(memory_space=pl.ANY)],
            out_specs=pl.BlockSpec((MOE_TM, D), row_map),
            scratch_shapes=[pltpu.VMEM((2, D, de), F32), pltpu.VMEM((2, D, de), F32), pltpu.VMEM((2, de, D), F32),
                            pltpu.VMEM((D, de), BF16), pltpu.VMEM((D, de), BF16), pltpu.VMEM((de, D), BF16),
                            pltpu.SemaphoreType.DMA((2, 3))]),
        out_shape=jax.ShapeDtypeStruct((R, D), BF16),
        input_output_aliases={6: 0},
        compiler_params=_cparams(1),
        name="experts",
    )(blk_e, nused, nsub, run_first, next_e, run_slot, buf, w_gate, w_up, w_down)


def _combine_kernel(npieces_ref, glob_ref, x2_ref, rti_ref, rtf_ref, gfin_ref, yb_ref, out_ref,
                    rows_ref, sem):
    tt = x2_ref.shape[0] // MOE_STEP_TILES
    sl = rows_ref.shape[1] // MOE_STEP_TILES
    i = pl.program_id(0)
    n_steps = pl.num_programs(0)
    cur = lax.rem(i, 2)

    def piece_copy(buf_slot, sub, local_row, global_row):
        return pltpu.make_async_copy(yb_ref.at[pl.ds(global_row, ROW_PIECE)],
                                     rows_ref.at[buf_slot, pl.ds(sub * sl + local_row, ROW_PIECE)], sem.at[buf_slot])

    def for_step_pieces(step, buf_slot, act):
        for sub in range(MOE_STEP_TILES):
            _for_each_piece(npieces_ref, glob_ref, step * MOE_STEP_TILES + sub,
                            lambda l, g, sub=sub: act(piece_copy(buf_slot, sub, l, g)))

    @pl.when(i == 0)
    def _():
        rows_ref[...] = jnp.zeros_like(rows_ref)
        for_step_pieces(0, 0, lambda cp: cp.start())

    @pl.when(i + 1 < n_steps)
    def _():
        for_step_pieces(i + 1, 1 - cur, lambda cp: cp.start())

    for_step_pieces(i, cur, lambda cp: cp.wait())

    for sub in range(MOE_STEP_TILES):
        ts_rows = slice(sub * tt, (sub + 1) * tt)
        rti = rti_ref[ts_rows, :]
        rtf = rtf_ref[ts_rows, :]
        slot1 = rti[:, 2:3]
        slot2 = rti[:, 3:4]
        n_sub = _used_slot_groups(npieces_ref[i * MOE_STEP_TILES + sub])
        for k in range(TOP_K * tt // SLOT_SUB, sl // SLOT_SUB + 1):
            @pl.when(n_sub == k)
            def _(k=k, sub=sub, ts_rows=ts_rows, rtf=rtf, slot1=slot1, slot2=slot2):
                m = k * SLOT_SUB
                lane = lax.broadcasted_iota(jnp.int32, (tt, m), 1)
                g = jnp.where(lane == slot1, rtf[:, 0:1], jnp.where(lane == slot2, rtf[:, 1:2], 0.0)).astype(BF16)
                y = x2_ref[ts_rows, :] + _dot(g, rows_ref[cur, sub * sl:sub * sl + m, :])
                ms = jnp.mean(y * y, axis=-1, keepdims=True)
                out_ref[ts_rows, :] = y * lax.rsqrt(ms + EPS) * gfin_ref[...]


def _combine(x2, rti, rtf, g_final, yb, npieces, piece_glob):
    T, D = rti.shape[0], x2.shape[1]
    tt = MOE_STEP_TILES * MIX_TS
    tok = lambda i, *_: (i, 0)
    return pl.pallas_call(
        _combine_kernel,
        grid_spec=pltpu.PrefetchScalarGridSpec(
            num_scalar_prefetch=2,
            grid=(T // tt,),
            in_specs=[pl.BlockSpec((tt, D), tok),
                      pl.BlockSpec((tt, LANES), tok),
                      pl.BlockSpec((tt, LANES), tok),
                      pl.BlockSpec((1, D), lambda i, *_: (0, 0)),
                      pl.BlockSpec(memory_space=pl.ANY)],
            out_specs=pl.BlockSpec((tt, D), tok),
            scratch_shapes=[pltpu.VMEM((2, MOE_STEP_TILES * MOE_SL, D), BF16),
                            pltpu.SemaphoreType.DMA((2,))]),
        out_shape=jax.ShapeDtypeStruct((T, D), F32),
        compiler_params=_cparams(1),
        name="combine",
    )(npieces, piece_glob, x2, rti, rtf, g_final, yb)


def _pad_lanes(a, width=LANES):
    return jnp.pad(a, ((0, 0), (0, width - a.shape[1])))


def kernel(x, g_mix, w_in, b_if, conv_q, conv_k, g_head, w_pool, pool_scale, w_br_a, w_br_b, w_out,
           g_ffn, w_rg, b_rg, w_re, b_re, w_e_gate, w_e_up, w_e_down, g_final):
    B, S, D = x.shape
    T = B * S
    assert g_mix.shape[0] == 1, "single-layer block"
    assert S % MIX_TS == 0 and (T // MIX_TS) % MOE_STEP_TILES == 0
    d_pool = w_br_a.shape[1]
    d_ml = w_br_b.shape[1]
    x2d = x.reshape(T, D)

    n_main = d_pool + 4 * d_ml
    w_l = w_in[0]
    w_main = w_l[:, :n_main].astype(BF16)
    w_if = w_l[:, n_main:n_main + 2 * N_HEADS]
    w_gates = w_l[:, n_main + 2 * N_HEADS:].astype(BF16)
    w_if_c = _pad_lanes(w_if).astype(BF16)
    w_if_t = jnp.pad(w_if.T, ((0, 16 - 2 * N_HEADS), (0, 0))).astype(BF16)
    w_r = _pad_lanes(jnp.concatenate([w_rg[0], w_re[0]], axis=1))
    w_r_hi = w_r.astype(BF16)
    w_r_lo = (w_r - w_r_hi.astype(F32)).astype(BF16)
    params = {
        "b_if": _pad_lanes(b_if[0][None, :]),
        "b_if_t": jnp.pad(b_if[0][:, None], ((0, 16 - 2 * N_HEADS), (0, 0))),
        "conv_q": conv_q[0], "conv_k": conv_k[0],
        "g_head": g_head[0][None, :],
        "w_pool": w_pool[0].astype(BF16),
        "pool_scale": pool_scale[0][None, :],
        "w_br_a": w_br_a[0].astype(BF16), "w_br_b": w_br_b[0].astype(BF16),
        "w_out": w_out[0].astype(BF16),
        "g_ffn": g_ffn[0][None, :],
        "w_r_hi": w_r_hi, "w_r_lo": w_r_lo,
        "b_r": _pad_lanes(jnp.concatenate([b_rg[0], b_re[0]])[None, :]),
    }

    x2, xn2, rti, rtf, tstat, srow = _mixer(x2d, g_mix[0][None, :], w_main, w_gates, w_if_c, w_if_t, params, B, S)

    n_tiles = T // MIX_TS
    pcs = tstat.reshape(n_tiles, 8, LANES)[:, 0, ROUTER_LANE0:ROUTER_LANE0 + N_EXPERTS]
    piece_loc = jnp.cumsum(pcs, axis=1) - pcs
    rows_e = jnp.sum(pcs, axis=0) * ROW_PIECE
    padded = (rows_e + MOE_TM - 1) // MOE_TM * MOE_TM
    pend = jnp.cumsum(padded)
    poff = pend - padded
    piece_glob = poff[None, :] // ROW_PIECE + jnp.cumsum(pcs, axis=0) - pcs
    n_rows = n_tiles * MOE_SL + N_EXPERTS * MOE_TM
    n_blocks = n_rows // MOE_TM
    nused = (pend[-1:] // MOE_TM).astype(jnp.int32)
    blk_ids = jnp.arange(n_blocks, dtype=jnp.int32)
    blk_e = jnp.sum((pend[None, :] <= blk_ids[:, None] * MOE_TM).astype(jnp.int32), axis=1)
    blk_e = jnp.minimum(blk_e, N_EXPERTS - 1)
    zflag = jnp.where(blk_ids >= nused[0], 2, ((blk_ids + 1) * MOE_TM == pend[blk_e]).astype(jnp.int32)).astype(jnp.int32)
    p_ids = jnp.arange(PIECES_PER_TILE, dtype=jnp.int32)
    piece_end = piece_loc + pcs
    e_of_p = jnp.minimum(jnp.sum((piece_end[:, None, :] <= p_ids[None, :, None]).astype(jnp.int32), axis=2),
                         N_EXPERTS - 1)
    e_ids = jnp.arange(N_EXPERTS, dtype=jnp.int32)
    shift = jnp.sum(jnp.where(e_of_p[:, :, None] == e_ids[None, None, :], (piece_glob - piece_loc)[:, None, :], 0),
                    axis=2)
    glob_of_p = (shift + p_ids[None, :]).astype(jnp.int32)
    glob_of_p = glob_of_p.reshape(n_tiles * PIECES_PER_TILE)
    npieces = jnp.sum(pcs, axis=1).astype(jnp.int32)

    buf = _dispatch(xn2, srow, npieces, glob_of_p, zflag, n_rows)
    data_end = jnp.sum(jnp.where(blk_e[:, None] == jnp.arange(N_EXPERTS)[None, :], (poff + rows_e)[None, :], 0), axis=1)
    rows_in_blk = jnp.clip(data_end - blk_ids * MOE_TM, 0, MOE_TM)
    nsub = ((rows_in_blk + EXPERT_SUB - 1) // EXPERT_SUB).astype(jnp.int32)
    is_used = blk_ids < nused[0]
    run_first = (is_used & ((blk_ids == 0) | (blk_e != jnp.roll(blk_e, 1)))).astype(jnp.int32)
    run_slot = ((jnp.cumsum(run_first) - 1) % 2).astype(jnp.int32)
    e_ids32 = jnp.arange(N_EXPERTS, dtype=jnp.int32)
    later_nonempty = (e_ids32[None, :] > e_ids32[:, None]) & (rows_e[None, :] > 0)
    next_of_e = jnp.min(jnp.where(later_nonempty, e_ids32[None, :], N_EXPERTS), axis=1)
    next_of_e = jnp.where(next_of_e == N_EXPERTS, -1, next_of_e)
    next_e = jnp.sum(jnp.where(blk_e[:, None] == e_ids32[None, :], next_of_e[None, :], 0), axis=1).astype(jnp.int32)
    yb = _experts(buf, blk_e, nused, nsub, run_first, next_e, run_slot, w_e_gate[0], w_e_up[0], w_e_down[0])
    out = _combine(x2, rti, rtf, g_final[None, :], yb, npieces, glob_of_p)
    return out.reshape(B, S, D)
```

```python
import functools

import numpy as np
import jax
import jax.numpy as jnp
from jax import lax
from jax.experimental import pallas as pl
from jax.experimental.pallas import tpu as pltpu

F32 = jnp.float32
BF16 = jnp.bfloat16

CHUNK = 64
POOL_WINDOWS = (2, 4, 8, 16)
N_HEADS = 4
CONV_K = 4
N_GROUPS = 4
EXPERTS_PER_GROUP = 8
N_EXPERTS = N_GROUPS * EXPERTS_PER_GROUP
TOP_K = 2
EPS = 1e-6

LANES = 128
HALO = 16
ROUTER_LANE0 = N_GROUPS

INPROJ_TN = 256
MIX_TS = 256
MOE_TM = 512
ROW_PIECE = 16
MOE_SL = TOP_K * MIX_TS + N_EXPERTS * ROW_PIECE
PIECES_PER_TILE = MOE_SL // ROW_PIECE
MOE_STEP_TILES = 4
SLOT_SUB = 128
PIECE_UNROLL = 4
EXPERT_SUB = 128
SLOT_RADIX = 16
VMEM_LIMIT = 56 * 1024 * 1024


def _cparams(n_axes):
    return pltpu.CompilerParams(dimension_semantics=("arbitrary",) * n_axes,
                                vmem_limit_bytes=VMEM_LIMIT)


def _sigmoid(v):
    return 0.5 * jnp.tanh(0.5 * v) + 0.5


def _silu(v):
    return v * _sigmoid(v)


def _log_sigmoid(v):
    return jnp.minimum(v, 0.0) - jnp.log1p(jnp.exp(-jnp.abs(v)))


def _split3(v):
    hi = v.astype(BF16)
    r1 = v - hi.astype(F32)
    mid = r1.astype(BF16)
    lo = (r1 - mid.astype(F32)).astype(BF16)
    return hi, mid, lo


def _dot(a, b):
    return jnp.dot(a, b, preferred_element_type=F32)


def _dot_nt(a, b):
    return lax.dot_general(a, b, (((1,), (1,)), ((), ())), preferred_element_type=F32)


def _dot_tn(a, b):
    return lax.dot_general(a, b, (((0,), (0,)), ((), ())), preferred_element_type=F32)


def _inproj_steps(x_ref, g_ref, w_refs, z_refs, xn_ref):
    wm_ref, wg_ref, wif_ref, wift_ref = w_refs
    zm_ref, zg_ref, zif_ref, zift_ref = z_refs

    def norm():
        x = x_ref[...]
        ms = jnp.mean(x * x, axis=-1, keepdims=True)
        xn_ref[...] = (x * lax.rsqrt(ms + EPS) * g_ref[...]).astype(BF16)

    def block(w_ref, z_ref, c0):
        def run():
            cols = slice(c0, c0 + INPROJ_TN)
            z_ref[:, cols] = _dot(xn_ref[...], w_ref[:, cols]).astype(BF16)
        return run

    def gates():
        zif_ref[...] = _dot(xn_ref[...], wif_ref[...])
        zift_ref[...] = _dot_nt(wift_ref[...], xn_ref[...])

    steps = [norm, gates]
    steps += [block(wm_ref, zm_ref, c0) for c0 in range(0, zm_ref.shape[1], INPROJ_TN)]
    steps += [block(wg_ref, zg_ref, c0) for c0 in range(0, zg_ref.shape[1], INPROJ_TN)]
    return steps


def _mixer_kernel(x_ref, xnext_ref, gmix_ref, wm_ref, wg_ref, wif_ref, wift_ref,
                  bif_ref, bift_ref, convq_ref, convk_ref, ghead_ref, wpool_ref, pscale_ref,
                  wa_ref, wb_ref, wo_ref, gffn_ref, wrh_ref, wrl_ref, br_ref,
                  tric_ref, trir_ref, stri_ref, ut_ref, sel_ref,
                  x2_ref, xn2_ref, rti_ref, rtf_ref, tstat_ref, srow_ref,
                  zm_ref, zg_ref, zif_ref, zift_ref, zm_nxt, zg_nxt, zif_nxt, zift_nxt, xn_ref,
                  ext_ref, q_ref, k_ref, h_ref, pool_ref, cst_ref, mst_ref, lg_ref, *, tiles_per_seq):
    ts = x_ref.shape[0]
    d_pool = wa_ref.shape[0]
    d_ml = wb_ref.shape[0]
    dh = d_ml // N_HEADS
    n_chunks = ts // CHUNK
    g_step = pl.program_id(0)
    j = lax.rem(g_step, tiles_per_seq)
    w_in_refs = (wm_ref, wg_ref, wif_ref, wift_ref)
    z_cur = (zm_ref, zg_ref, zif_ref, zift_ref)
    z_nxt = (zm_nxt, zg_nxt, zif_nxt, zift_nxt)
    first = g_step == 0

    @pl.when(first)
    def _():
        for step in _inproj_steps(x_ref, gmix_ref, w_in_refs, z_cur, xn_ref):
            step()
        lg_ref[...] = jnp.zeros_like(lg_ref)

    @pl.when(jnp.logical_not(first))
    def _():
        for dst, src in zip(z_cur, z_nxt):
            dst[...] = src[...]

    @pl.when(j == 0)
    def _():
        ext_ref[:, :HALO, :] = jnp.zeros((ext_ref.shape[0], HALO, LANES), F32)
        cst_ref[...] = jnp.zeros_like(cst_ref)
        mst_ref[...] = jnp.zeros_like(mst_ref)

    pending = _inproj_steps(xnext_ref, gmix_ref, w_in_refs, z_nxt, xn_ref)

    def project_some(n=1):
        for _ in range(min(n, len(pending))):
            pending.pop(0)()

    project_some(2)

    routed = _route_select(lg_ref[...])

    row = lax.broadcasted_iota(jnp.int32, (ts, LANES), 0)
    pos1 = (row + j * ts + 1).astype(F32)

    def history(cg):
        cur = zm_ref[:, cg * LANES:(cg + 1) * LANES].astype(F32)
        ext_ref[cg, HALO:, :] = cur
        return cur, lambda s: ext_ref[cg, HALO - s:HALO - s + ts, :]

    def keep_history(cg, cur):
        ext_ref[cg, :HALO, :] = cur[ts - HALO:, :]

    n_pool_groups = d_pool // LANES
    for g in range(n_pool_groups):
        w = POOL_WINDOWS[g]
        cur, shifted = history(g)
        win = cur
        for s in range(1, w):
            win = win + shifted(s)
        keep_history(g, cur)
        cnt = jnp.minimum(pos1, float(w))
        d = win / cnt - cur
        y = _dot(d.astype(BF16), wpool_ref[g]) * pscale_ref[:, g * LANES:(g + 1) * LANES]
        pool_ref[:, g * LANES:(g + 1) * LANES] = y.astype(BF16)
        project_some()

    n_ml_groups = d_ml // LANES
    for which, (cw_ref, dst_ref, scale) in enumerate(((convq_ref, q_ref, 1.0), (convk_ref, k_ref, dh ** -0.5))):
        for g in range(n_ml_groups):
            cols = slice(g * LANES, (g + 1) * LANES)
            cg = n_pool_groups + which * n_ml_groups + g
            cur, shifted = history(cg)
            acc = cur * cw_ref[CONV_K - 1:CONV_K, cols]
            for sft in range(1, CONV_K):
                acc = acc + shifted(sft) * cw_ref[CONV_K - 1 - sft:CONV_K - sft, cols]
            keep_history(cg, cur)
            dst_ref[:, cols] = (_silu(acc) * scale).astype(BF16)
        project_some()

    _route_slots(routed, stri_ref, ut_ref, sel_ref, rti_ref, rtf_ref, tstat_ref, srow_ref)
    project_some()

    zc = zif_ref[...] + bif_ref[...]
    lf_c = _log_sigmoid(zc)
    bc = sum(_dot(tric_ref[...], p) for p in _split3(lf_c))
    zr = zift_ref[...] + bift_ref[...]
    lf_r = _log_sigmoid(zr)
    br = sum(_dot(p, trir_ref[...]) for p in _split3(lf_r))
    project_some(2)

    ti = lax.broadcasted_iota(jnp.int32, (CHUNK, CHUNK), 0)
    si = lax.broadcasted_iota(jnp.int32, (CHUNK, CHUNK), 1)
    causal = si <= ti
    ones_blk = jnp.ones((CHUNK, dh), BF16)
    v0 = d_pool + 2 * d_ml
    ig_rep = [jnp.broadcast_to(zc[:, h:h + 1], (ts, dh)) for h in range(N_HEADS)]
    bt_rep = [jnp.broadcast_to(bc[:, N_HEADS + h:N_HEADS + h + 1], (ts, dh)) for h in range(N_HEADS)]

    m_state = [mst_ref[h:h + 1, :] for h in range(N_HEADS)]
    c_state = [cst_ref[h] for h in range(N_HEADS)]
    def stage_scores(c):
        rs = slice(c * CHUNK, (c + 1) * CHUNK)
        out = []
        for h in range(N_HEADS):
            hs = slice(h * dh, (h + 1) * dh)
            q = q_ref[rs, hs]
            k = k_ref[rs, hs]
            bt = bt_rep[h][rs, :]
            r_row = zr[h:h + 1, rs] - br[N_HEADS + h:N_HEADS + h + 1, rs]
            dmat = jnp.where(causal, bt[:, :CHUNK] + r_row, -jnp.inf)
            out.append(dict(q=q, k=k, bt=bt, dmat=dmat, qk=_dot_nt(q, k),
                            m_intra=jnp.max(dmat, axis=-1, keepdims=True)))
        return out

    def stage_state(c, st):
        rs = slice(c * CHUNK, (c + 1) * CHUNK)
        for h in range(N_HEADS):
            s = st[h]
            bt, k = s["bt"], s["k"]
            m_prev, c_prev = m_state[h], c_state[h]
            v_aug = jnp.concatenate([zm_ref[rs, v0 + h * dh:v0 + (h + 1) * dh], ones_blk], axis=-1)
            igc = ig_rep[h][rs, :]
            b_last = bt[CHUNK - 1:CHUNK, :]
            a_log = b_last - bt + igc
            a_max = jnp.max(a_log, axis=0, keepdims=True)
            m_new = jnp.maximum(b_last + m_prev, a_max)
            kw = (k.astype(F32) * jnp.exp(a_log - m_new)).astype(BF16)
            decay = jnp.exp(b_last + m_prev - m_new)
            s.update(v_aug=v_aug, m_prev=m_prev, qc=_dot(s["q"], c_prev.astype(BF16)))
            c_state[h] = jnp.concatenate([decay, decay], axis=-1) * c_prev + _dot_tn(kw, v_aug)
            m_state[h] = m_new

    def stage_values(c, st):
        rs = slice(c * CHUNK, (c + 1) * CHUNK)
        for h in range(N_HEADS):
            s = st[h]
            hs = slice(h * dh, (h + 1) * dh)
            inter = s["bt"] + s["m_prev"]
            m_t = jnp.maximum(inter, s["m_intra"])
            w_inter = jnp.exp(inter - m_t)
            smat = s["qk"] * jnp.exp(s["dmat"] - m_t[:, :CHUNK])
            sv = _dot(smat.astype(BF16), s["v_aug"])
            qc = s["qc"]
            nq = w_inter * qc[:, dh:] + sv[:, dh:]
            den = jnp.maximum(jnp.abs(nq), jnp.exp(-m_t))
            h_ref[rs, hs] = (w_inter * qc[:, :dh] + sv[:, :dh]) / den

    staged = stage_scores(0)
    for c in range(n_chunks):
        stage_state(c, staged)
        project_some()
        nxt = stage_scores(c + 1) if c + 1 < n_chunks else None
        project_some()
        stage_values(c, staged)
        staged = nxt
    for h in range(N_HEADS):
        cst_ref[h] = c_state[h]
        mst_ref[h:h + 1, :] = m_state[h]

    o0 = v0 + d_ml
    for h in range(N_HEADS):
        hs = slice(h * dh, (h + 1) * dh)
        hv = h_ref[:, hs]
        mu = jnp.mean(hv, axis=-1, keepdims=True)
        hc = hv - mu
        var = jnp.mean(hc * hc, axis=-1, keepdims=True)
        hn = hc * lax.rsqrt(var + EPS) * ghead_ref[:, hs]
        og = _sigmoid(zm_ref[:, o0 + h * dh:o0 + (h + 1) * dh].astype(F32))
        q_ref[:, hs] = (og * hn).astype(BF16)
    y_a = _dot(pool_ref[...], wa_ref[...])
    y_b = _dot(q_ref[...], wb_ref[...])
    d_model = x_ref.shape[1]
    ga = _sigmoid(zg_ref[:, :d_model].astype(F32))
    gb = _sigmoid(zg_ref[:, d_model:].astype(F32))
    merged = (ga * y_a + gb * y_b).astype(BF16)
    x2 = x_ref[...] + _dot(merged, wo_ref[...])
    x2_ref[...] = x2
    project_some(len(pending))

    ms = jnp.mean(x2 * x2, axis=-1, keepdims=True)
    xn2 = x2 * lax.rsqrt(ms + EPS) * gffn_ref[...]
    xh = xn2.astype(BF16)
    xn2_ref[...] = xh
    xl = (xn2 - xh.astype(F32)).astype(BF16)
    lg_ref[...] = _dot(xh, wrh_ref[...]) + _dot(xl, wrh_ref[...]) + _dot(xh, wrl_ref[...]) + br_ref[...]
    project_some(len(pending))


def _route_select(lg):
    ts = lg.shape[0]
    lane = lax.broadcasted_iota(jnp.int32, (ts, LANES), 1)
    lanef = lane.astype(F32)
    big = float(4 * LANES)
    gl = jnp.where(lane < N_GROUPS, lg, -jnp.inf)
    gmax = jnp.max(gl, axis=-1, keepdims=True)
    g_sel = jnp.min(jnp.where(gl == gmax, lanef, big), axis=-1, keepdims=True)
    p_g = 1.0 / jnp.sum(jnp.exp(gl - gmax), axis=-1, keepdims=True)
    lo = ROUTER_LANE0 + EXPERTS_PER_GROUP * g_sel
    el = jnp.where((lanef >= lo) & (lanef < lo + EXPERTS_PER_GROUP), lg, -jnp.inf)
    m1 = jnp.max(el, axis=-1, keepdims=True)
    i1 = jnp.min(jnp.where(el == m1, lanef, big), axis=-1, keepdims=True)
    el2 = jnp.where(lanef == i1, -jnp.inf, el)
    m2 = jnp.max(el2, axis=-1, keepdims=True)
    i2 = jnp.min(jnp.where(el2 == m2, lanef, big), axis=-1, keepdims=True)
    e2x = jnp.exp(m2 - m1)
    gate1 = p_g / (1.0 + e2x)
    gate2 = p_g * e2x / (1.0 + e2x)
    return dict(lane=lane, i1=i1, i2=i2, gate1=gate1, gate2=gate2, oh1=lanef == i1, oh2=lanef == i2)


def _route_slots(r, stri_ref, ut_ref, sel_ref, rti_ref, rtf_ref, tstat_ref, srow_ref):
    lane, oh1, oh2, i1, i2 = r["lane"], r["oh1"], r["oh2"], r["i1"], r["i2"]
    ohs = jnp.where(oh1 | oh2, 1.0, 0.0)
    n_loc = jnp.sum(ohs, axis=0, keepdims=True)
    pieces = jnp.floor((n_loc + (ROW_PIECE - 1.0)) * (1.0 / ROW_PIECE))
    piece_off = _dot(jnp.broadcast_to(pieces, (8, LANES)).astype(BF16), ut_ref[...])[0:1, :]
    base = _dot(stri_ref[...], ohs.astype(BF16)) + ROW_PIECE * piece_off
    slot1 = jnp.sum(jnp.where(oh1, base, 0.0), axis=-1, keepdims=True)
    slot2 = jnp.sum(jnp.where(oh2, base, 0.0), axis=-1, keepdims=True)
    tstat_ref[...] = jnp.broadcast_to(pieces, tstat_ref.shape).astype(jnp.int32)

    rti = jnp.where(lane == 0, i1 - ROUTER_LANE0,
                    jnp.where(lane == 1, i2 - ROUTER_LANE0,
                              jnp.where(lane == 2, slot1, jnp.where(lane == 3, slot2, 0.0))))
    rti_ref[...] = rti.astype(jnp.int32)
    rtf_ref[...] = jnp.where(lane == 0, r["gate1"], jnp.where(lane == 1, r["gate2"], 0.0))
    h1 = jnp.floor(slot1 * (1.0 / SLOT_RADIX))
    h2 = jnp.floor(slot2 * (1.0 / SLOT_RADIX))
    parts = jnp.where(lane == 0, h1, jnp.where(lane == 1, slot1 - SLOT_RADIX * h1,
                      jnp.where(lane == 2, h2, jnp.where(lane == 3, slot2 - SLOT_RADIX * h2, 0.0))))
    srow_ref[...] = _dot_nt(sel_ref[...], parts.astype(BF16))


def _mixer(x2d, g_mix, w_main, w_gates, w_if_c, w_if_t, params, batch, seq):
    T, D = x2d.shape
    ts = min(MIX_TS, seq)
    nts = seq // ts
    d_pool = params["w_br_a"].shape[0]
    d_ml = params["w_br_b"].shape[0]
    dh = d_ml // N_HEADS

    idx = np.arange(ts)
    same_chunk = (idx[:, None] // CHUNK) == (idx[None, :] // CHUNK)
    tri_c = jnp.asarray((idx[None, :] <= idx[:, None]) & same_chunk, BF16)
    tri_r = jnp.asarray((idx[:, None] <= idx[None, :]) & same_chunk, BF16)
    stri = jnp.asarray(idx[None, :] < idx[:, None], BF16)
    lane_idx = np.arange(LANES)
    ut = jnp.asarray(lane_idx[:, None] < lane_idx[None, :], BF16)
    sel = jnp.asarray(np.arange(8)[:, None] == lane_idx[None, :], BF16)

    n_tiles = batch * nts
    tok = lambda g: (g, 0)
    tok_in = lambda g: (jnp.minimum(g, n_tiles - 1), 0)
    tok_next = lambda g: (jnp.minimum(g + 1, n_tiles - 1), 0)
    tok_prev = lambda g: (jnp.maximum(g - 1, 0), 0)
    tok_prev_t = lambda g: (0, jnp.maximum(g - 1, 0))
    c2 = lambda g: (0, 0)
    c3 = lambda g: (0, 0, 0)
    full = lambda a: pl.BlockSpec(a.shape, c2 if a.ndim == 2 else c3)
    consts = [params[n] for n in ("b_if", "b_if_t", "conv_q", "conv_k", "g_head", "w_pool", "pool_scale",
                                  "w_br_a", "w_br_b", "w_out", "g_ffn", "w_r_hi", "w_r_lo", "b_r")]
    consts = [g_mix, w_main, w_gates, w_if_c, w_if_t] + consts + [tri_c, tri_r, stri, ut, sel]
    nm, ng = w_main.shape[1], w_gates.shape[1]
    z_scratch = [pltpu.VMEM((ts, nm), BF16), pltpu.VMEM((ts, ng), BF16),
                 pltpu.VMEM((ts, LANES), F32), pltpu.VMEM((16, ts), F32)]
    return pl.pallas_call(
        functools.partial(_mixer_kernel, tiles_per_seq=nts),
        grid=(n_tiles + 1,),
        in_specs=[pl.BlockSpec((ts, D), tok_in),
                  pl.BlockSpec((ts, D), tok_next)] + [full(a) for a in consts],
        out_specs=[pl.BlockSpec((ts, D), tok),
                   pl.BlockSpec((ts, D), tok),
                   pl.BlockSpec((ts, LANES), tok_prev),
                   pl.BlockSpec((ts, LANES), tok_prev),
                   pl.BlockSpec((8, LANES), tok_prev),
                   pl.BlockSpec((8, ts), tok_prev_t)],
        out_shape=[jax.ShapeDtypeStruct((T + ts, D), F32),
                   jax.ShapeDtypeStruct((T + ts, D), BF16),
                   jax.ShapeDtypeStruct((T, LANES), jnp.int32),
                   jax.ShapeDtypeStruct((T, LANES), F32),
                   jax.ShapeDtypeStruct((n_tiles * 8, LANES), jnp.int32),
                   jax.ShapeDtypeStruct((8, T), F32)],
        scratch_shapes=z_scratch + z_scratch + [
                        pltpu.VMEM((ts, D), BF16),
                        pltpu.VMEM(((d_pool + 2 * d_ml) // LANES, HALO + ts, LANES), F32),
                        pltpu.VMEM((ts, d_ml), BF16),
                        pltpu.VMEM((ts, d_ml), BF16),
                        pltpu.VMEM((ts, d_ml), F32),
                        pltpu.VMEM((ts, d_pool), BF16),
                        pltpu.VMEM((N_HEADS, dh, 2 * dh), F32),
                        pltpu.VMEM((8, LANES), F32),
                        pltpu.VMEM((ts, LANES), F32)],
        compiler_params=_cparams(1),
        name="mixer",
    )(x2d, x2d, *consts)


def _for_each_piece(npieces_ref, glob_ref, tile, fn):
    base = tile * PIECES_PER_TILE
    n = npieces_ref[tile]

    def one(p):
        fn(pl.multiple_of(p * ROW_PIECE, ROW_PIECE), pl.multiple_of(glob_ref[base + p] * ROW_PIECE, ROW_PIECE))

    def group(g, carry):
        for u in range(PIECE_UNROLL):
            one(g * PIECE_UNROLL + u)
        return carry

    n_groups = lax.div(n, jnp.int32(PIECE_UNROLL))
    lax.fori_loop(0, n_groups, group, 0)
    for u in range(PIECE_UNROLL - 1):
        @pl.when(n_groups * PIECE_UNROLL + u < n)
        def _():
            one(n_groups * PIECE_UNROLL + u)


def _used_slot_groups(n_pieces):
    return lax.div(n_pieces * ROW_PIECE + (SLOT_SUB - 1), jnp.int32(SLOT_SUB))


def _dispatch_kernel(npieces_ref, glob_ref, zflag_ref,
                     xn_ref, srow_ref, buf_ref, rows_ref, zeros_ref, sem, zsem, tsem):
    tt = xn_ref.shape[0] // MOE_STEP_TILES
    sl = rows_ref.shape[1] // MOE_STEP_TILES
    n_blocks = buf_ref.shape[0] // MOE_TM
    i = pl.program_id(0)
    n_steps = pl.num_programs(0)
    cur = lax.rem(i, 2)

    def zero_copy(b, flag):
        return pltpu.make_async_copy(zeros_ref, buf_ref.at[pl.ds(b * MOE_TM, MOE_TM)], zsem if flag == 1 else tsem)

    def for_flagged(flag, fn):
        def body(b, carry):
            @pl.when(zflag_ref[b] == flag)
            def _():
                fn(zero_copy(b, flag))
            return carry
        lax.fori_loop(0, n_blocks, body, 0)

    @pl.when(i == 0)
    def _():
        zeros_ref[...] = jnp.zeros_like(zeros_ref)
        for_flagged(1, lambda cp: cp.start())
        for_flagged(2, lambda cp: cp.start())
        for_flagged(1, lambda cp: cp.wait())

    def piece_copy(buf_slot, sub, local_row, global_row):
        return pltpu.make_async_copy(rows_ref.at[buf_slot, pl.ds(sub * sl + local_row, ROW_PIECE)],
                                     buf_ref.at[pl.ds(global_row, ROW_PIECE)], sem.at[buf_slot])

    def for_step_pieces(step, buf_slot, act):
        for sub in range(MOE_STEP_TILES):
            _for_each_piece(npieces_ref, glob_ref, step * MOE_STEP_TILES + sub,
                            lambda l, g, sub=sub: act(piece_copy(buf_slot, sub, l, g)))

    @pl.when(i >= 2)
    def _():
        for_step_pieces(i - 2, cur, lambda cp: cp.wait())

    for sub in range(MOE_STEP_TILES):
        sr = srow_ref[:, sub * tt:(sub + 1) * tt]
        slot1 = SLOT_RADIX * sr[0:1, :] + sr[1:2, :]
        slot2 = SLOT_RADIX * sr[2:3, :] + sr[3:4, :]
        n_sub = _used_slot_groups(npieces_ref[i * MOE_STEP_TILES + sub])
        for k in range(TOP_K * tt // SLOT_SUB, sl // SLOT_SUB + 1):
            @pl.when(n_sub == k)
            def _(k=k, sub=sub, slot1=slot1, slot2=slot2):
                m = k * SLOT_SUB
                r = lax.broadcasted_iota(jnp.int32, (m, tt), 0).astype(F32)
                sel = jnp.where((r == slot1) | (r == slot2), 1.0, 0.0).astype(BF16)
                rows_ref[cur, sub * sl:sub * sl + m, :] = _dot(sel, xn_ref[sub * tt:(sub + 1) * tt, :]).astype(BF16)
    for_step_pieces(i, cur, lambda cp: cp.start())

    @pl.when(i == n_steps - 1)
    def _():
        @pl.when(i >= 1)
        def _():
            for_step_pieces(i - 1, 1 - cur, lambda cp: cp.wait())
        for_step_pieces(i, cur, lambda cp: cp.wait())
        for_flagged(2, lambda cp: cp.wait())


def _dispatch(xn2, srow, npieces, piece_glob, zflag, n_rows):
    T, D = srow.shape[1], xn2.shape[1]
    tt = MOE_STEP_TILES * MIX_TS
    return pl.pallas_call(
        _dispatch_kernel,
        grid_spec=pltpu.PrefetchScalarGridSpec(
            num_scalar_prefetch=3,
            grid=(T // tt,),
            in_specs=[pl.BlockSpec((tt, D), lambda i, *_: (i, 0)),
                      pl.BlockSpec((8, tt), lambda i, *_: (0, i))],
            out_specs=pl.BlockSpec(memory_space=pl.ANY),
            scratch_shapes=[pltpu.VMEM((2, MOE_STEP_TILES * MOE_SL, D), BF16),
                            pltpu.VMEM((MOE_TM, D), BF16),
                            pltpu.SemaphoreType.DMA((2,)),
                            pltpu.SemaphoreType.DMA(()),
                            pltpu.SemaphoreType.DMA(())]),
        out_shape=jax.ShapeDtypeStruct((n_rows, D), BF16),
        compiler_params=_cparams(1),
        name="dispatch",
    )(npieces, piece_glob, zflag, xn2, srow)


def _experts_kernel(blk_e_ref, nused_ref, nsub_ref, first_ref, next_e_ref, slot_ref,
                    x_ref, wg_hbm, wu_hbm, wd_hbm, y_ref,
                    wg32_ref, wu32_ref, wd32_ref, wgb_ref, wub_ref, wdb_ref, sem):
    i = pl.program_id(0)
    used = i < nused_ref[0]
    n_sub = nsub_ref[i]
    landing = ((wg_hbm, wg32_ref), (wu_hbm, wu32_ref), (wd_hbm, wd32_ref))

    def weight_copies(e, s):
        return [pltpu.make_async_copy(hbm.at[e], vmem.at[s], sem.at[s, n]) for n, (hbm, vmem) in enumerate(landing)]

    @pl.when(used & (i == 0))
    def _():
        for cp in weight_copies(blk_e_ref[0], 0):
            cp.start()

    @pl.when(used & (first_ref[i] > 0))
    def _():
        s = slot_ref[i]
        for cp in weight_copies(blk_e_ref[i], s):
            cp.wait()
        wgb_ref[...] = wg32_ref[s].astype(BF16)
        wub_ref[...] = wu32_ref[s].astype(BF16)
        wdb_ref[...] = wd32_ref[s].astype(BF16)

        @pl.when(next_e_ref[i] >= 0)
        def _():
            for cp in weight_copies(next_e_ref[i], 1 - s):
                cp.start()

    for k in range(1, MOE_TM // EXPERT_SUB + 1):
        @pl.when(used & (n_sub == k))
        def _(k=k):
            m = k * EXPERT_SUB
            x = x_ref[:m, :]
            hg = _dot(x, wgb_ref[...])
            hu = _dot(x, wub_ref[...])
            hid = (_silu(hg) * hu).astype(BF16)
            y_ref[:m, :] = _dot(hid, wdb_ref[...]).astype(BF16)
            if m < MOE_TM:
                y_ref[m:, :] = jnp.zeros((MOE_TM - m, y_ref.shape[1]), BF16)


def _experts(buf, blk_e, nused, nsub, run_first, next_e, run_slot, w_gate, w_up, w_down):
    R, D = buf.shape
    de = w_gate.shape[2]
    n_blocks = R // MOE_TM
    row_map = lambda i, be, nu, *_: (jnp.minimum(i, nu[0] - 1), 0)
    return pl.pallas_call(
        _experts_kernel,
        grid_spec=pltpu.PrefetchScalarGridSpec(
            num_scalar_prefetch=6,
            grid=(n_blocks,),
            in_specs=[pl.BlockSpec((MOE_TM, D), row_map),
                      pl.BlockSpec(memory_space=pl.ANY),
                      pl.BlockSpec(memory_space=pl.ANY),
                      pl.BlockSpec(memory_space=pl.ANY)],
            out_specs=pl.BlockSpec((MOE_TM, D), row_map),
            scratch_shapes=[pltpu.VMEM((2, D, de), F32), pltpu.VMEM((2, D, de), F32), pltpu.VMEM((2, de, D), F32),
                            pltpu.VMEM((D, de), BF16), pltpu.VMEM((D, de), BF16), pltpu.VMEM((de, D), BF16),
                            pltpu.SemaphoreType.DMA((2, 3))]),
        out_shape=jax.ShapeDtypeStruct((R, D), BF16),
        input_output_aliases={6: 0},
        compiler_params=_cparams(1),
        name="experts",
    )(blk_e, nused, nsub, run_first, next_e, run_slot, buf, w_gate, w_up, w_down)


def _combine_kernel(npieces_ref, glob_ref, x2_ref, rti_ref, rtf_ref, gfin_ref, yb_ref, out_ref,
                    rows_ref, sem):
    tt = x2_ref.shape[0] // MOE_STEP_TILES
    sl = rows_ref.shape[1] // MOE_STEP_TILES
    i = pl.program_id(0)
    n_steps = pl.num_programs(0)
    cur = lax.rem(i, 2)

    def piece_copy(buf_slot, sub, local_row, global_row):
        return pltpu.make_async_copy(yb_ref.at[pl.ds(global_row, ROW_PIECE)],
                                     rows_ref.at[buf_slot, pl.ds(sub * sl + local_row, ROW_PIECE)], sem.at[buf_slot])

    def for_step_pieces(step, buf_slot, act):
        for sub in range(MOE_STEP_TILES):
            _for_each_piece(npieces_ref, glob_ref, step * MOE_STEP_TILES + sub,
                            lambda l, g, sub=sub: act(piece_copy(buf_slot, sub, l, g)))

    @pl.when(i == 0)
    def _():
        rows_ref[...] = jnp.zeros_like(rows_ref)
        for_step_pieces(0, 0, lambda cp: cp.start())

    @pl.when(i + 1 < n_steps)
    def _():
        for_step_pieces(i + 1, 1 - cur, lambda cp: cp.start())

    for_step_pieces(i, cur, lambda cp: cp.wait())

    for sub in range(MOE_STEP_TILES):
        ts_rows = slice(sub * tt, (sub + 1) * tt)
        rti = rti_ref[ts_rows, :]
        rtf = rtf_ref[ts_rows, :]
        slot1 = rti[:, 2:3]
        slot2 = rti[:, 3:4]
        n_sub = _used_slot_groups(npieces_ref[i * MOE_STEP_TILES + sub])
        for k in range(TOP_K * tt // SLOT_SUB, sl // SLOT_SUB + 1):
            @pl.when(n_sub == k)
            def _(k=k, sub=sub, ts_rows=ts_rows, rtf=rtf, slot1=slot1, slot2=slot2):
                m = k * SLOT_SUB
                lane = lax.broadcasted_iota(jnp.int32, (tt, m), 1)
                g = jnp.where(lane == slot1, rtf[:, 0:1], jnp.where(lane == slot2, rtf[:, 1:2], 0.0)).astype(BF16)
                y = x2_ref[ts_rows, :] + _dot(g, rows_ref[cur, sub * sl:sub * sl + m, :])
                ms = jnp.mean(y * y, axis=-1, keepdims=True)
                out_ref[ts_rows, :] = y * lax.rsqrt(ms + EPS) * gfin_ref[...]


def _combine(x2, rti, rtf, g_final, yb, npieces, piece_glob):
    T, D = rti.shape[0], x2.shape[1]
    tt = MOE_STEP_TILES * MIX_TS
    tok = lambda i, *_: (i, 0)
    return pl.pallas_call(
        _combine_kernel,
        grid_spec=pltpu.PrefetchScalarGridSpec(
            num_scalar_prefetch=2,
            grid=(T // tt,),
            in_specs=[pl.BlockSpec((tt, D), tok),
                      pl.BlockSpec((tt, LANES), tok),
                      pl.BlockSpec((tt, LANES), tok),
                      pl.BlockSpec((1, D), lambda i, *_: (0, 0)),
                      pl.BlockSpec(memory_space=pl.ANY)],
            out_specs=pl.BlockSpec((tt, D), tok),
            scratch_shapes=[pltpu.VMEM((2, MOE_STEP_TILES * MOE_SL, D), BF16),
                            pltpu.SemaphoreType.DMA((2,))]),
        out_shape=jax.ShapeDtypeStruct((T, D), F32),
        compiler_params=_cparams(1),
        name="combine",
    )(npieces, piece_glob, x2, rti, rtf, g_final, yb)


def _pad_lanes(a, width=LANES):
    return jnp.pad(a, ((0, 0), (0, width - a.shape[1])))


def kernel(x, g_mix, w_in, b_if, conv_q, conv_k, g_head, w_pool, pool_scale, w_br_a, w_br_b, w_out,
           g_ffn, w_rg, b_rg, w_re, b_re, w_e_gate, w_e_up, w_e_down, g_final):
    B, S, D = x.shape
    T = B * S
    assert g_mix.shape[0] == 1, "single-layer block"
    assert S % MIX_TS == 0 and (T // MIX_TS) % MOE_STEP_TILES == 0
    d_pool = w_br_a.shape[1]
    d_ml = w_br_b.shape[1]
    x2d = x.reshape(T, D)

    n_main = d_pool + 4 * d_ml
    w_l = w_in[0]
    w_main = w_l[:, :n_main].astype(BF16)
    w_if = w_l[:, n_main:n_main + 2 * N_HEADS]
    w_gates = w_l[:, n_main + 2 * N_HEADS:].astype(BF16)
    w_if_c = _pad_lanes(w_if).astype(BF16)
    w_if_t = jnp.pad(w_if.T, ((0, 16 - 2 * N_HEADS), (0, 0))).astype(BF16)
    w_r = _pad_lanes(jnp.concatenate([w_rg[0], w_re[0]], axis=1))
    w_r_hi = w_r.astype(BF16)
    w_r_lo = (w_r - w_r_hi.astype(F32)).astype(BF16)
    params = {
        "b_if": _pad_lanes(b_if[0][None, :]),
        "b_if_t": jnp.pad(b_if[0][:, None], ((0, 16 - 2 * N_HEADS), (0, 0))),
        "conv_q": conv_q[0], "conv_k": conv_k[0],
        "g_head": g_head[0][None, :],
        "w_pool": w_pool[0].astype(BF16),
        "pool_scale": pool_scale[0][None, :],
        "w_br_a": w_br_a[0].astype(BF16), "w_br_b": w_br_b[0].astype(BF16),
        "w_out": w_out[0].astype(BF16),
        "g_ffn": g_ffn[0][None, :],
        "w_r_hi": w_r_hi, "w_r_lo": w_r_lo,
        "b_r": _pad_lanes(jnp.concatenate([b_rg[0], b_re[0]])[None, :]),
    }

    x2, xn2, rti, rtf, tstat, srow = _mixer(x2d, g_mix[0][None, :], w_main, w_gates, w_if_c, w_if_t, params, B, S)

    n_tiles = T // MIX_TS
    pcs = tstat.reshape(n_tiles, 8, LANES)[:, 0, ROUTER_LANE0:ROUTER_LANE0 + N_EXPERTS]
    piece_loc = jnp.cumsum(pcs, axis=1) - pcs
    rows_e = jnp.sum(pcs, axis=0) * ROW_PIECE
    padded = (rows_e + MOE_TM - 1) // MOE_TM * MOE_TM
    pend = jnp.cumsum(padded)
    poff = pend - padded
    piece_glob = poff[None, :] // ROW_PIECE + jnp.cumsum(pcs, axis=0) - pcs
    n_rows = n_tiles * MOE_SL + N_EXPERTS * MOE_TM
    n_blocks = n_rows // MOE_TM
    nused = (pend[-1:] // MOE_TM).astype(jnp.int32)
    blk_ids = jnp.arange(n_blocks, dtype=jnp.int32)
    blk_e = jnp.sum((pend[None, :] <= blk_ids[:, None] * MOE_TM).astype(jnp.int32), axis=1)
    blk_e = jnp.minimum(blk_e, N_EXPERTS - 1)
    zflag = jnp.where(blk_ids >= nused[0], 2, ((blk_ids + 1) * MOE_TM == pend[blk_e]).astype(jnp.int32)).astype(jnp.int32)
    p_ids = jnp.arange(PIECES_PER_TILE, dtype=jnp.int32)
    piece_end = piece_loc + pcs
    e_of_p = jnp.minimum(jnp.sum((piece_end[:, None, :] <= p_ids[None, :, None]).astype(jnp.int32), axis=2),
                         N_EXPERTS - 1)
    e_ids = jnp.arange(N_EXPERTS, dtype=jnp.int32)
    shift = jnp.sum(jnp.where(e_of_p[:, :, None] == e_ids[None, None, :], (piece_glob - piece_loc)[:, None, :], 0),
                    axis=2)
    glob_of_p = (shift + p_ids[None, :]).astype(jnp.int32)
    glob_of_p = glob_of_p.reshape(n_tiles * PIECES_PER_TILE)
    npieces = jnp.sum(pcs, axis=1).astype(jnp.int32)

    buf = _dispatch(xn2, srow, npieces, glob_of_p, zflag, n_rows)
    data_end = jnp.sum(jnp.where(blk_e[:, None] == jnp.arange(N_EXPERTS)[None, :], (poff + rows_e)[None, :], 0), axis=1)
    rows_in_blk = jnp.clip(data_end - blk_ids * MOE_TM, 0, MOE_TM)
    nsub = ((rows_in_blk + EXPERT_SUB - 1) // EXPERT_SUB).astype(jnp.int32)
    is_used = blk_ids < nused[0]
    run_first = (is_used & ((blk_ids == 0) | (blk_e != jnp.roll(blk_e, 1)))).astype(jnp.int32)
    run_slot = ((jnp.cumsum(run_first) - 1) % 2).astype(jnp.int32)
    e_ids32 = jnp.arange(N_EXPERTS, dtype=jnp.int32)
    later_nonempty = (e_ids32[None, :] > e_ids32[:, None]) & (rows_e[None, :] > 0)
    next_of_e = jnp.min(jnp.where(later_nonempty, e_ids32[None, :], N_EXPERTS), axis=1)
    next_of_e = jnp.where(next_of_e == N_EXPERTS, -1, next_of_e)
    next_e = jnp.sum(jnp.where(blk_e[:, None] == e_ids32[None, :], next_of_e[None, :], 0), axis=1).astype(jnp.int32)
    yb = _experts(buf, blk_e, nused, nsub, run_first, next_e, run_slot, w_e_gate[0], w_e_up[0], w_e_down[0])
    out = _combine(x2, rti, rtf, g_final[None, :], yb, npieces, glob_of_p)
    return out.reshape(B, S, D)
```

```python
import functools

import numpy as np
import jax
import jax.numpy as jnp
from jax import lax
from jax.experimental import pallas as pl
from jax.experimental.pallas import tpu as pltpu

F32 = jnp.float32
BF16 = jnp.bfloat16

CHUNK = 64
POOL_WINDOWS = (2, 4, 8, 16)
N_HEADS = 4
CONV_K = 4
N_GROUPS = 4
EXPERTS_PER_GROUP = 8
N_EXPERTS = N_GROUPS * EXPERTS_PER_GROUP
TOP_K = 2
EPS = 1e-6

LANES = 128
HALO = 16
ROUTER_LANE0 = N_GROUPS

INPROJ_TN = 256
MIX_TS = 256
MOE_TM = 512
ROW_PIECE = 16
MOE_SL = TOP_K * MIX_TS + N_EXPERTS * ROW_PIECE
PIECES_PER_TILE = MOE_SL // ROW_PIECE
MOE_STEP_TILES = 2
SLOT_SUB = 128
PIECE_UNROLL = 4
EXPERT_SUB = 128
SLOT_RADIX = 16
VMEM_LIMIT = 56 * 1024 * 1024


def _cparams(n_axes):
    return pltpu.CompilerParams(dimension_semantics=("arbitrary",) * n_axes,
                                vmem_limit_bytes=VMEM_LIMIT)


def _sigmoid(v):
    return 0.5 * jnp.tanh(0.5 * v) + 0.5


def _silu(v):
    return v * _sigmoid(v)


def _log_sigmoid(v):
    return jnp.minimum(v, 0.0) - jnp.log1p(jnp.exp(-jnp.abs(v)))


def _split3(v):
    hi = v.astype(BF16)
    r1 = v - hi.astype(F32)
    mid = r1.astype(BF16)
    lo = (r1 - mid.astype(F32)).astype(BF16)
    return hi, mid, lo


def _dot(a, b):
    return jnp.dot(a, b, preferred_element_type=F32)


def _dot_nt(a, b):
    return lax.dot_general(a, b, (((1,), (1,)), ((), ())), preferred_element_type=F32)


def _dot_tn(a, b):
    return lax.dot_general(a, b, (((0,), (0,)), ((), ())), preferred_element_type=F32)


def _inproj_steps(x_ref, g_ref, w_refs, z_refs, xn_ref):
    wm_ref, wg_ref, wif_ref, wift_ref = w_refs
    zm_ref, zg_ref, zif_ref, zift_ref = z_refs

    def norm():
        x = x_ref[...]
        ms = jnp.mean(x * x, axis=-1, keepdims=True)
        xn_ref[...] = (x * lax.rsqrt(ms + EPS) * g_ref[...]).astype(BF16)

    def block(w_ref, z_ref, c0):
        def run():
            cols = slice(c0, c0 + INPROJ_TN)
            z_ref[:, cols] = _dot(xn_ref[...], w_ref[:, cols]).astype(BF16)
        return run

    def gates():
        zif_ref[...] = _dot(xn_ref[...], wif_ref[...])
        zift_ref[...] = _dot_nt(wift_ref[...], xn_ref[...])

    steps = [norm, gates]
    steps += [block(wm_ref, zm_ref, c0) for c0 in range(0, zm_ref.shape[1], INPROJ_TN)]
    steps += [block(wg_ref, zg_ref, c0) for c0 in range(0, zg_ref.shape[1], INPROJ_TN)]
    return steps


def _mixer_kernel(x_ref, xnext_ref, gmix_ref, wm_ref, wg_ref, wif_ref, wift_ref,
                  bif_ref, bift_ref, convq_ref, convk_ref, ghead_ref, wpool_ref, pscale_ref,
                  wa_ref, wb_ref, wo_ref, gffn_ref, wrh_ref, wrl_ref, br_ref,
                  tric_ref, trir_ref, stri_ref, ut_ref, sel_ref,
                  x2_ref, xn2_ref, rti_ref, rtf_ref, tstat_ref, srow_ref,
                  zm_ref, zg_ref, zif_ref, zift_ref, zm_nxt, zg_nxt, zif_nxt, zift_nxt, xn_ref,
                  ext_ref, q_ref, k_ref, h_ref, pool_ref, cst_ref, mst_ref, lg_ref, *, tiles_per_seq):
    ts = x_ref.shape[0]
    d_pool = wa_ref.shape[0]
    d_ml = wb_ref.shape[0]
    dh = d_ml // N_HEADS
    n_chunks = ts // CHUNK
    g_step = pl.program_id(0)
    j = lax.rem(g_step, tiles_per_seq)
    w_in_refs = (wm_ref, wg_ref, wif_ref, wift_ref)
    z_cur = (zm_ref, zg_ref, zif_ref, zift_ref)
    z_nxt = (zm_nxt, zg_nxt, zif_nxt, zift_nxt)
    first = g_step == 0

    @pl.when(first)
    def _():
        for step in _inproj_steps(x_ref, gmix_ref, w_in_refs, z_cur, xn_ref):
            step()
        lg_ref[...] = jnp.zeros_like(lg_ref)

    @pl.when(jnp.logical_not(first))
    def _():
        for dst, src in zip(z_cur, z_nxt):
            dst[...] = src[...]

    @pl.when(j == 0)
    def _():
        ext_ref[:, :HALO, :] = jnp.zeros((ext_ref.shape[0], HALO, LANES), F32)
        cst_ref[...] = jnp.zeros_like(cst_ref)
        mst_ref[...] = jnp.zeros_like(mst_ref)

    pending = _inproj_steps(xnext_ref, gmix_ref, w_in_refs, z_nxt, xn_ref)

    def project_some(n=1):
        for _ in range(min(n, len(pending))):
            pending.pop(0)()

    project_some(2)

    routed = _route_select(lg_ref[...])

    row = lax.broadcasted_iota(jnp.int32, (ts, LANES), 0)
    pos1 = (row + j * ts + 1).astype(F32)

    def history(cg):
        cur = zm_ref[:, cg * LANES:(cg + 1) * LANES].astype(F32)
        ext_ref[cg, HALO:, :] = cur
        return cur, lambda s: ext_ref[cg, HALO - s:HALO - s + ts, :]

    def keep_history(cg, cur):
        ext_ref[cg, :HALO, :] = cur[ts - HALO:, :]

    n_pool_groups = d_pool // LANES
    for g in range(n_pool_groups):
        w = POOL_WINDOWS[g]
        cur, shifted = history(g)
        win = cur
        for s in range(1, w):
            win = win + shifted(s)
        keep_history(g, cur)
        cnt = jnp.minimum(pos1, float(w))
        d = win / cnt - cur
        y = _dot(d.astype(BF16), wpool_ref[g]) * pscale_ref[:, g * LANES:(g + 1) * LANES]
        pool_ref[:, g * LANES:(g + 1) * LANES] = y.astype(BF16)
        project_some()

    n_ml_groups = d_ml // LANES
    for which, (cw_ref, dst_ref, scale) in enumerate(((convq_ref, q_ref, 1.0), (convk_ref, k_ref, dh ** -0.5))):
        for g in range(n_ml_groups):
            cols = slice(g * LANES, (g + 1) * LANES)
            cg = n_pool_groups + which * n_ml_groups + g
            cur, shifted = history(cg)
            acc = cur * cw_ref[CONV_K - 1:CONV_K, cols]
            for sft in range(1, CONV_K):
                acc = acc + shifted(sft) * cw_ref[CONV_K - 1 - sft:CONV_K - sft, cols]
            keep_history(cg, cur)
            dst_ref[:, cols] = (_silu(acc) * scale).astype(BF16)
        project_some()

    _route_slots(routed, stri_ref, ut_ref, sel_ref, rti_ref, rtf_ref, tstat_ref, srow_ref)
    project_some()

    zc = zif_ref[...] + bif_ref[...]
    lf_c = _log_sigmoid(zc)
    bc = sum(_dot(tric_ref[...], p) for p in _split3(lf_c))
    zr = zift_ref[...] + bift_ref[...]
    lf_r = _log_sigmoid(zr)
    br = sum(_dot(p, trir_ref[...]) for p in _split3(lf_r))
    project_some(2)

    ti = lax.broadcasted_iota(jnp.int32, (CHUNK, CHUNK), 0)
    si = lax.broadcasted_iota(jnp.int32, (CHUNK, CHUNK), 1)
    causal = si <= ti
    ones_blk = jnp.ones((CHUNK, dh), BF16)
    v0 = d_pool + 2 * d_ml
    ig_rep = [jnp.broadcast_to(zc[:, h:h + 1], (ts, dh)) for h in range(N_HEADS)]
    bt_rep = [jnp.broadcast_to(bc[:, N_HEADS + h:N_HEADS + h + 1], (ts, dh)) for h in range(N_HEADS)]

    m_state = [mst_ref[h:h + 1, :] for h in range(N_HEADS)]
    c_state = [cst_ref[h] for h in range(N_HEADS)]
    def stage_scores(c):
        rs = slice(c * CHUNK, (c + 1) * CHUNK)
        out = []
        for h in range(N_HEADS):
            hs = slice(h * dh, (h + 1) * dh)
            q = q_ref[rs, hs]
            k = k_ref[rs, hs]
            bt = bt_rep[h][rs, :]
            r_row = zr[h:h + 1, rs] - br[N_HEADS + h:N_HEADS + h + 1, rs]
            dmat = jnp.where(causal, bt[:, :CHUNK] + r_row, -jnp.inf)
            out.append(dict(q=q, k=k, bt=bt, dmat=dmat, qk=_dot_nt(q, k),
                            m_intra=jnp.max(dmat, axis=-1, keepdims=True)))
        return out

    def stage_state(c, st):
        rs = slice(c * CHUNK, (c + 1) * CHUNK)
        for h in range(N_HEADS):
            s = st[h]
            bt, k = s["bt"], s["k"]
            m_prev, c_prev = m_state[h], c_state[h]
            v_aug = jnp.concatenate([zm_ref[rs, v0 + h * dh:v0 + (h + 1) * dh], ones_blk], axis=-1)
            igc = ig_rep[h][rs, :]
            b_last = bt[CHUNK - 1:CHUNK, :]
            a_log = b_last - bt + igc
            a_max = jnp.max(a_log, axis=0, keepdims=True)
            m_new = jnp.maximum(b_last + m_prev, a_max)
            kw = (k.astype(F32) * jnp.exp(a_log - m_new)).astype(BF16)
            decay = jnp.exp(b_last + m_prev - m_new)
            s.update(v_aug=v_aug, m_prev=m_prev, qc=_dot(s["q"], c_prev.astype(BF16)))
            c_state[h] = jnp.concatenate([decay, decay], axis=-1) * c_prev + _dot_tn(kw, v_aug)
            m_state[h] = m_new

    def stage_values(c, st):
        rs = slice(c * CHUNK, (c + 1) * CHUNK)
        for h in range(N_HEADS):
            s = st[h]
            hs = slice(h * dh, (h + 1) * dh)
            inter = s["bt"] + s["m_prev"]
            m_t = jnp.maximum(inter, s["m_intra"])
            w_inter = jnp.exp(inter - m_t)
            smat = s["qk"] * jnp.exp(s["dmat"] - m_t[:, :CHUNK])
            sv = _dot(smat.astype(BF16), s["v_aug"])
            qc = s["qc"]
            nq = w_inter * qc[:, dh:] + sv[:, dh:]
            den = jnp.maximum(jnp.abs(nq), jnp.exp(-m_t))
            h_ref[rs, hs] = (w_inter * qc[:, :dh] + sv[:, :dh]) / den

    staged = stage_scores(0)
    for c in range(n_chunks):
        stage_state(c, staged)
        project_some()
        nxt = stage_scores(c + 1) if c + 1 < n_chunks else None
        project_some()
        stage_values(c, staged)
        staged = nxt
    for h in range(N_HEADS):
        cst_ref[h] = c_state[h]
        mst_ref[h:h + 1, :] = m_state[h]

    o0 = v0 + d_ml
    for h in range(N_HEADS):
        hs = slice(h * dh, (h + 1) * dh)
        hv = h_ref[:, hs]
        mu = jnp.mean(hv, axis=-1, keepdims=True)
        hc = hv - mu
        var = jnp.mean(hc * hc, axis=-1, keepdims=True)
        hn = hc * lax.rsqrt(var + EPS) * ghead_ref[:, hs]
        og = _sigmoid(zm_ref[:, o0 + h * dh:o0 + (h + 1) * dh].astype(F32))
        q_ref[:, hs] = (og * hn).astype(BF16)
    y_a = _dot(pool_ref[...], wa_ref[...])
    y_b = _dot(q_ref[...], wb_ref[...])
    d_model = x_ref.shape[1]
    ga = _sigmoid(zg_ref[:, :d_model].astype(F32))
    gb = _sigmoid(zg_ref[:, d_model:].astype(F32))
    merged = (ga * y_a + gb * y_b).astype(BF16)
    x2 = x_ref[...] + _dot(merged, wo_ref[...])
    x2_ref[...] = x2
    project_some(len(pending))

    ms = jnp.mean(x2 * x2, axis=-1, keepdims=True)
    xn2 = x2 * lax.rsqrt(ms + EPS) * gffn_ref[...]
    xh = xn2.astype(BF16)
    xn2_ref[...] = xh
    xl = (xn2 - xh.astype(F32)).astype(BF16)
    lg_ref[...] = _dot(xh, wrh_ref[...]) + _dot(xl, wrh_ref[...]) + _dot(xh, wrl_ref[...]) + br_ref[...]
    project_some(len(pending))


def _route_select(lg):
    ts = lg.shape[0]
    lane = lax.broadcasted_iota(jnp.int32, (ts, LANES), 1)
    lanef = lane.astype(F32)
    big = float(4 * LANES)
    gl = jnp.where(lane < N_GROUPS, lg, -jnp.inf)
    gmax = jnp.max(gl, axis=-1, keepdims=True)
    g_sel = jnp.min(jnp.where(gl == gmax, lanef, big), axis=-1, keepdims=True)
    p_g = 1.0 / jnp.sum(jnp.exp(gl - gmax), axis=-1, keepdims=True)
    lo = ROUTER_LANE0 + EXPERTS_PER_GROUP * g_sel
    el = jnp.where((lanef >= lo) & (lanef < lo + EXPERTS_PER_GROUP), lg, -jnp.inf)
    m1 = jnp.max(el, axis=-1, keepdims=True)
    i1 = jnp.min(jnp.where(el == m1, lanef, big), axis=-1, keepdims=True)
    el2 = jnp.where(lanef == i1, -jnp.inf, el)
    m2 = jnp.max(el2, axis=-1, keepdims=True)
    i2 = jnp.min(jnp.where(el2 == m2, lanef, big), axis=-1, keepdims=True)
    e2x = jnp.exp(m2 - m1)
    gate1 = p_g / (1.0 + e2x)
    gate2 = p_g * e2x / (1.0 + e2x)
    return dict(lane=lane, i1=i1, i2=i2, gate1=gate1, gate2=gate2, oh1=lanef == i1, oh2=lanef == i2)


def _route_slots(r, stri_ref, ut_ref, sel_ref, rti_ref, rtf_ref, tstat_ref, srow_ref):
    lane, oh1, oh2, i1, i2 = r["lane"], r["oh1"], r["oh2"], r["i1"], r["i2"]
    ohs = jnp.where(oh1 | oh2, 1.0, 0.0)
    n_loc = jnp.sum(ohs, axis=0, keepdims=True)
    pieces = jnp.floor((n_loc + (ROW_PIECE - 1.0)) * (1.0 / ROW_PIECE))
    piece_off = _dot(jnp.broadcast_to(pieces, (8, LANES)).astype(BF16), ut_ref[...])[0:1, :]
    base = _dot(stri_ref[...], ohs.astype(BF16)) + ROW_PIECE * piece_off
    slot1 = jnp.sum(jnp.where(oh1, base, 0.0), axis=-1, keepdims=True)
    slot2 = jnp.sum(jnp.where(oh2, base, 0.0), axis=-1, keepdims=True)
    tstat_ref[...] = jnp.broadcast_to(pieces, tstat_ref.shape).astype(jnp.int32)

    rti = jnp.where(lane == 0, i1 - ROUTER_LANE0,
                    jnp.where(lane == 1, i2 - ROUTER_LANE0,
                              jnp.where(lane == 2, slot1, jnp.where(lane == 3, slot2, 0.0))))
    rti_ref[...] = rti.astype(jnp.int32)
    rtf_ref[...] = jnp.where(lane == 0, r["gate1"], jnp.where(lane == 1, r["gate2"], 0.0))
    h1 = jnp.floor(slot1 * (1.0 / SLOT_RADIX))
    h2 = jnp.floor(slot2 * (1.0 / SLOT_RADIX))
    parts = jnp.where(lane == 0, h1, jnp.where(lane == 1, slot1 - SLOT_RADIX * h1,
                      jnp.where(lane == 2, h2, jnp.where(lane == 3, slot2 - SLOT_RADIX * h2, 0.0))))
    srow_ref[...] = _dot_nt(sel_ref[...], parts.astype(BF16))


def _mixer(x2d, g_mix, w_main, w_gates, w_if_c, w_if_t, params, batch, seq):
    T, D = x2d.shape
    ts = min(MIX_TS, seq)
    nts = seq // ts
    d_pool = params["w_br_a"].shape[0]
    d_ml = params["w_br_b"].shape[0]
    dh = d_ml // N_HEADS

    idx = np.arange(ts)
    same_chunk = (idx[:, None] // CHUNK) == (idx[None, :] // CHUNK)
    tri_c = jnp.asarray((idx[None, :] <= idx[:, None]) & same_chunk, BF16)
    tri_r = jnp.asarray((idx[:, None] <= idx[None, :]) & same_chunk, BF16)
    stri = jnp.asarray(idx[None, :] < idx[:, None], BF16)
    lane_idx = np.arange(LANES)
    ut = jnp.asarray(lane_idx[:, None] < lane_idx[None, :], BF16)
    sel = jnp.asarray(np.arange(8)[:, None] == lane_idx[None, :], BF16)

    n_tiles = batch * nts
    tok = lambda g: (g, 0)
    tok_in = lambda g: (jnp.minimum(g, n_tiles - 1), 0)
    tok_next = lambda g: (jnp.minimum(g + 1, n_tiles - 1), 0)
    tok_prev = lambda g: (jnp.maximum(g - 1, 0), 0)
    tok_prev_t = lambda g: (0, jnp.maximum(g - 1, 0))
    c2 = lambda g: (0, 0)
    c3 = lambda g: (0, 0, 0)
    full = lambda a: pl.BlockSpec(a.shape, c2 if a.ndim == 2 else c3)
    consts = [params[n] for n in ("b_if", "b_if_t", "conv_q", "conv_k", "g_head", "w_pool", "pool_scale",
                                  "w_br_a", "w_br_b", "w_out", "g_ffn", "w_r_hi", "w_r_lo", "b_r")]
    consts = [g_mix, w_main, w_gates, w_if_c, w_if_t] + consts + [tri_c, tri_r, stri, ut, sel]
    nm, ng = w_main.shape[1], w_gates.shape[1]
    z_scratch = [pltpu.VMEM((ts, nm), BF16), pltpu.VMEM((ts, ng), BF16),
                 pltpu.VMEM((ts, LANES), F32), pltpu.VMEM((16, ts), F32)]
    return pl.pallas_call(
        functools.partial(_mixer_kernel, tiles_per_seq=nts),
        grid=(n_tiles + 1,),
        in_specs=[pl.BlockSpec((ts, D), tok_in),
                  pl.BlockSpec((ts, D), tok_next)] + [full(a) for a in consts],
        out_specs=[pl.BlockSpec((ts, D), tok),
                   pl.BlockSpec((ts, D), tok),
                   pl.BlockSpec((ts, LANES), tok_prev),
                   pl.BlockSpec((ts, LANES), tok_prev),
                   pl.BlockSpec((8, LANES), tok_prev),
                   pl.BlockSpec((8, ts), tok_prev_t)],
        out_shape=[jax.ShapeDtypeStruct((T + ts, D), F32),
                   jax.ShapeDtypeStruct((T + ts, D), BF16),
                   jax.ShapeDtypeStruct((T, LANES), jnp.int32),
                   jax.ShapeDtypeStruct((T, LANES), F32),
                   jax.ShapeDtypeStruct((n_tiles * 8, LANES), jnp.int32),
                   jax.ShapeDtypeStruct((8, T), F32)],
        scratch_shapes=z_scratch + z_scratch + [
                        pltpu.VMEM((ts, D), BF16),
                        pltpu.VMEM(((d_pool + 2 * d_ml) // LANES, HALO + ts, LANES), F32),
                        pltpu.VMEM((ts, d_ml), BF16),
                        pltpu.VMEM((ts, d_ml), BF16),
                        pltpu.VMEM((ts, d_ml), F32),
                        pltpu.VMEM((ts, d_pool), BF16),
                        pltpu.VMEM((N_HEADS, dh, 2 * dh), F32),
                        pltpu.VMEM((8, LANES), F32),
                        pltpu.VMEM((ts, LANES), F32)],
        compiler_params=_cparams(1),
        name="mixer",
    )(x2d, x2d, *consts)


def _for_each_piece(npieces_ref, glob_ref, tile, fn):
    base = tile * PIECES_PER_TILE
    n = npieces_ref[tile]

    def one(p):
        fn(pl.multiple_of(p * ROW_PIECE, ROW_PIECE), pl.multiple_of(glob_ref[base + p] * ROW_PIECE, ROW_PIECE))

    def group(g, carry):
        for u in range(PIECE_UNROLL):
            one(g * PIECE_UNROLL + u)
        return carry

    n_groups = lax.div(n, jnp.int32(PIECE_UNROLL))
    lax.fori_loop(0, n_groups, group, 0)
    for u in range(PIECE_UNROLL - 1):
        @pl.when(n_groups * PIECE_UNROLL + u < n)
        def _():
            one(n_groups * PIECE_UNROLL + u)


def _used_slot_groups(n_pieces):
    return lax.div(n_pieces * ROW_PIECE + (SLOT_SUB - 1), jnp.int32(SLOT_SUB))


def _dispatch_kernel(npieces_ref, glob_ref, zflag_ref,
                     xn_ref, srow_ref, buf_ref, rows_ref, zeros_ref, sem, zsem, tsem):
    tt = xn_ref.shape[0] // MOE_STEP_TILES
    sl = rows_ref.shape[1] // MOE_STEP_TILES
    n_blocks = buf_ref.shape[0] // MOE_TM
    i = pl.program_id(0)
    n_steps = pl.num_programs(0)
    cur = lax.rem(i, 2)

    def zero_copy(b, flag):
        return pltpu.make_async_copy(zeros_ref, buf_ref.at[pl.ds(b * MOE_TM, MOE_TM)], zsem if flag == 1 else tsem)

    def for_flagged(flag, fn):
        def body(b, carry):
            @pl.when(zflag_ref[b] == flag)
            def _():
                fn(zero_copy(b, flag))
            return carry
        lax.fori_loop(0, n_blocks, body, 0)

    @pl.when(i == 0)
    def _():
        zeros_ref[...] = jnp.zeros_like(zeros_ref)
        for_flagged(1, lambda cp: cp.start())
        for_flagged(2, lambda cp: cp.start())
        for_flagged(1, lambda cp: cp.wait())

    def piece_copy(buf_slot, sub, local_row, global_row):
        return pltpu.make_async_copy(rows_ref.at[buf_slot, pl.ds(sub * sl + local_row, ROW_PIECE)],
                                     buf_ref.at[pl.ds(global_row, ROW_PIECE)], sem.at[buf_slot])

    def for_step_pieces(step, buf_slot, act):
        for sub in range(MOE_STEP_TILES):
            _for_each_piece(npieces_ref, glob_ref, step * MOE_STEP_TILES + sub,
                            lambda l, g, sub=sub: act(piece_copy(buf_slot, sub, l, g)))

    @pl.when(i >= 2)
    def _():
        for_step_pieces(i - 2, cur, lambda cp: cp.wait())

    for sub in range(MOE_STEP_TILES):
        sr = srow_ref[:, sub * tt:(sub + 1) * tt]
        slot1 = SLOT_RADIX * sr[0:1, :] + sr[1:2, :]
        slot2 = SLOT_RADIX * sr[2:3, :] + sr[3:4, :]
        n_sub = _used_slot_groups(npieces_ref[i * MOE_STEP_TILES + sub])
        for k in range(TOP_K * tt // SLOT_SUB, sl // SLOT_SUB + 1):
            @pl.when(n_sub == k)
            def _(k=k, sub=sub, slot1=slot1, slot2=slot2):
                m = k * SLOT_SUB
                r = lax.broadcasted_iota(jnp.int32, (m, tt), 0).astype(F32)
                sel = jnp.where((r == slot1) | (r == slot2), 1.0, 0.0).astype(BF16)
                rows_ref[cur, sub * sl:sub * sl + m, :] = _dot(sel, xn_ref[sub * tt:(sub + 1) * tt, :]).astype(BF16)
    for_step_pieces(i, cur, lambda cp: cp.start())

    @pl.when(i == n_steps - 1)
    def _():
        @pl.when(i >= 1)
        def _():
            for_step_pieces(i - 1, 1 - cur, lambda cp: cp.wait())
        for_step_pieces(i, cur, lambda cp: cp.wait())
        for_flagged(2, lambda cp: cp.wait())


def _dispatch(xn2, srow, npieces, piece_glob, zflag, n_rows):
    T, D = srow.shape[1], xn2.shape[1]
    tt = MOE_STEP_TILES * MIX_TS
    return pl.pallas_call(
        _dispatch_kernel,
        grid_spec=pltpu.PrefetchScalarGridSpec(
            num_scalar_prefetch=3,
            grid=(T // tt,),
            in_specs=[pl.BlockSpec((tt, D), lambda i, *_: (i, 0)),
                      pl.BlockSpec((8, tt), lambda i, *_: (0, i))],
            out_specs=pl.BlockSpec(memory_space=pl.ANY),
            scratch_shapes=[pltpu.VMEM((2, MOE_STEP_TILES * MOE_SL, D), BF16),
                            pltpu.VMEM((MOE_TM, D), BF16),
                            pltpu.SemaphoreType.DMA((2,)),
                            pltpu.SemaphoreType.DMA(()),
                            pltpu.SemaphoreType.DMA(())]),
        out_shape=jax.ShapeDtypeStruct((n_rows, D), BF16),
        compiler_params=_cparams(1),
        name="dispatch",
    )(npieces, piece_glob, zflag, xn2, srow)


def _experts_kernel(blk_e_ref, nused_ref, nsub_ref, first_ref, next_e_ref, slot_ref,
                    x_ref, wg_hbm, wu_hbm, wd_hbm, y_ref,
                    wg32_ref, wu32_ref, wd32_ref, wgb_ref, wub_ref, wdb_ref, sem):
    i = pl.program_id(0)
    used = i < nused_ref[0]
    n_sub = nsub_ref[i]
    landing = ((wg_hbm, wg32_ref), (wu_hbm, wu32_ref), (wd_hbm, wd32_ref))

    def weight_copies(e, s):
        return [pltpu.make_async_copy(hbm.at[e], vmem.at[s], sem.at[s, n]) for n, (hbm, vmem) in enumerate(landing)]

    @pl.when(used & (i == 0))
    def _():
        for cp in weight_copies(blk_e_ref[0], 0):
            cp.start()

    @pl.when(used & (first_ref[i] > 0))
    def _():
        s = slot_ref[i]
        for cp in weight_copies(blk_e_ref[i], s):
            cp.wait()
        wgb_ref[...] = wg32_ref[s].astype(BF16)
        wub_ref[...] = wu32_ref[s].astype(BF16)
        wdb_ref[...] = wd32_ref[s].astype(BF16)

        @pl.when(next_e_ref[i] >= 0)
        def _():
            for cp in weight_copies(next_e_ref[i], 1 - s):
                cp.start()

    for k in range(1, MOE_TM // EXPERT_SUB + 1):
        @pl.when(used & (n_sub == k))
        def _(k=k):
            m = k * EXPERT_SUB
            x = x_ref[:m, :]
            hg = _dot(x, wgb_ref[...])
            hu = _dot(x, wub_ref[...])
            hid = (_silu(hg) * hu).astype(BF16)
            y_ref[:m, :] = _dot(hid, wdb_ref[...]).astype(BF16)
            if m < MOE_TM:
                y_ref[m:, :] = jnp.zeros((MOE_TM - m, y_ref.shape[1]), BF16)


def _experts(buf, blk_e, nused, nsub, run_first, next_e, run_slot, w_gate, w_up, w_down):
    R, D = buf.shape
    de = w_gate.shape[2]
    n_blocks = R // MOE_TM
    row_map = lambda i, be, nu, *_: (jnp.minimum(i, nu[0] - 1), 0)
    return pl.pallas_call(
        _experts_kernel,
        grid_spec=pltpu.PrefetchScalarGridSpec(
            num_scalar_prefetch=6,
            grid=(n_blocks,),
            in_specs=[pl.BlockSpec((MOE_TM, D), row_map),
                      pl.BlockSpec(memory_space=pl.ANY),
                      pl.BlockSpec(memory_space=pl.ANY),
                      pl.BlockSpec(memory_space=pl.ANY)],
            out_specs=pl.BlockSpec((MOE_TM, D), row_map),
            scratch_shapes=[pltpu.VMEM((2, D, de), F32), pltpu.VMEM((2, D, de), F32), pltpu.VMEM((2, de, D), F32),
                            pltpu.VMEM((D, de), BF16), pltpu.VMEM((D, de), BF16), pltpu.VMEM((de, D), BF16),
                            pltpu.SemaphoreType.DMA((2, 3))]),
        out_shape=jax.ShapeDtypeStruct((R, D), BF16),
        input_output_aliases={6: 0},
        compiler_params=_cparams(1),
        name="experts",
    )(blk_e, nused, nsub, run_first, next_e, run_slot, buf, w_gate, w_up, w_down)


def _combine_kernel(npieces_ref, glob_ref, x2_ref, rti_ref, rtf_ref, gfin_ref, yb_ref, out_ref,
                    rows_ref, sem):
    tt = x2_ref.shape[0] // MOE_STEP_TILES
    sl = rows_ref.shape[1] // MOE_STEP_TILES
    i = pl.program_id(0)
    n_steps = pl.num_programs(0)
    cur = lax.rem(i, 2)

    def piece_copy(buf_slot, sub, local_row, global_row):
        return pltpu.make_async_copy(yb_ref.at[pl.ds(global_row, ROW_PIECE)],
                                     rows_ref.at[buf_slot, pl.ds(sub * sl + local_row, ROW_PIECE)], sem.at[buf_slot])

    def for_step_pieces(step, buf_slot, act):
        for sub in range(MOE_STEP_TILES):
            _for_each_piece(npieces_ref, glob_ref, step * MOE_STEP_TILES + sub,
                            lambda l, g, sub=sub: act(piece_copy(buf_slot, sub, l, g)))

    @pl.when(i == 0)
    def _():
        rows_ref[...] = jnp.zeros_like(rows_ref)
        for_step_pieces(0, 0, lambda cp: cp.start())

    @pl.when(i + 1 < n_steps)
    def _():
        for_step_pieces(i + 1, 1 - cur, lambda cp: cp.start())

    for_step_pieces(i, cur, lambda cp: cp.wait())

    for sub in range(MOE_STEP_TILES):
        ts_rows = slice(sub * tt, (sub + 1) * tt)
        rti = rti_ref[ts_rows, :]
        rtf = rtf_ref[ts_rows, :]
        slot1 = rti[:, 2:3]
        slot2 = rti[:, 3:4]
        n_sub = _used_slot_groups(npieces_ref[i * MOE_STEP_TILES + sub])
        for k in range(TOP_K * tt // SLOT_SUB, sl // SLOT_SUB + 1):
            @pl.when(n_sub == k)
            def _(k=k, sub=sub, ts_rows=ts_rows, rtf=rtf, slot1=slot1, slot2=slot2):
                m = k * SLOT_SUB
                lane = lax.broadcasted_iota(jnp.int32, (tt, m), 1)
                g = jnp.where(lane == slot1, rtf[:, 0:1], jnp.where(lane == slot2, rtf[:, 1:2], 0.0)).astype(BF16)
                y = x2_ref[ts_rows, :] + _dot(g, rows_ref[cur, sub * sl:sub * sl + m, :])
                ms = jnp.mean(y * y, axis=-1, keepdims=True)
                out_ref[ts_rows, :] = y * lax.rsqrt(ms + EPS) * gfin_ref[...]


def _combine(x2, rti, rtf, g_final, yb, npieces, piece_glob):
    T, D = rti.shape[0], x2.shape[1]
    tt = MOE_STEP_TILES * MIX_TS
    tok = lambda i, *_: (i, 0)
    return pl.pallas_call(
        _combine_kernel,
        grid_spec=pltpu.PrefetchScalarGridSpec(
            num_scalar_prefetch=2,
            grid=(T // tt,),
            in_specs=[pl.BlockSpec((tt, D), tok),
                      pl.BlockSpec((tt, LANES), tok),
                      pl.BlockSpec((tt, LANES), tok),
                      pl.BlockSpec((1, D), lambda i, *_: (0, 0)),
                      pl.BlockSpec(memory_space=pl.ANY)],
            out_specs=pl.BlockSpec((tt, D), tok),
            scratch_shapes=[pltpu.VMEM((2, MOE_STEP_TILES * MOE_SL, D), BF16),
                            pltpu.SemaphoreType.DMA((2,))]),
        out_shape=jax.ShapeDtypeStruct((T, D), F32),
        compiler_params=_cparams(1),
        name="combine",
    )(npieces, piece_glob, x2, rti, rtf, g_final, yb)


def _pad_lanes(a, width=LANES):
    return jnp.pad(a, ((0, 0), (0, width - a.shape[1])))


def kernel(x, g_mix, w_in, b_if, conv_q, conv_k, g_head, w_pool, pool_scale, w_br_a, w_br_b, w_out,
           g_ffn, w_rg, b_rg, w_re, b_re, w_e_gate, w_e_up, w_e_down, g_final):
    B, S, D = x.shape
    T = B * S
    assert g_mix.shape[0] == 1, "single-layer block"
    assert S % MIX_TS == 0 and (T // MIX_TS) % MOE_STEP_TILES == 0
    d_pool = w_br_a.shape[1]
    d_ml = w_br_b.shape[1]
    x2d = x.reshape(T, D)

    n_main = d_pool + 4 * d_ml
    w_l = w_in[0]
    w_main = w_l[:, :n_main].astype(BF16)
    w_if = w_l[:, n_main:n_main + 2 * N_HEADS]
    w_gates = w_l[:, n_main + 2 * N_HEADS:].astype(BF16)
    w_if_c = _pad_lanes(w_if).astype(BF16)
    w_if_t = jnp.pad(w_if.T, ((0, 16 - 2 * N_HEADS), (0, 0))).astype(BF16)
    w_r = _pad_lanes(jnp.concatenate([w_rg[0], w_re[0]], axis=1))
    w_r_hi = w_r.astype(BF16)
    w_r_lo = (w_r - w_r_hi.astype(F32)).astype(BF16)
    params = {
        "b_if": _pad_lanes(b_if[0][None, :]),
        "b_if_t": jnp.pad(b_if[0][:, None], ((0, 16 - 2 * N_HEADS), (0, 0))),
        "conv_q": conv_q[0], "conv_k": conv_k[0],
        "g_head": g_head[0][None, :],
        "w_pool": w_pool[0].astype(BF16),
        "pool_scale": pool_scale[0][None, :],
        "w_br_a": w_br_a[0].astype(BF16), "w_br_b": w_br_b[0].astype(BF16),
        "w_out": w_out[0].astype(BF16),
        "g_ffn": g_ffn[0][None, :],
        "w_r_hi": w_r_hi, "w_r_lo": w_r_lo,
        "b_r": _pad_lanes(jnp.concatenate([b_rg[0], b_re[0]])[None, :]),
    }

    x2, xn2, rti, rtf, tstat, srow = _mixer(x2d, g_mix[0][None, :], w_main, w_gates, w_if_c, w_if_t, params, B, S)

    n_tiles = T // MIX_TS
    n_rows = n_tiles * MOE_SL + N_EXPERTS * MOE_TM
    n_blocks = n_rows // MOE_TM
    i32 = lambda a: a.astype(jnp.int32)
    mm = lambda a, b: jnp.round(jnp.dot(a, b, precision=lax.Precision.HIGHEST, preferred_element_type=F32))
    e_ids = np.arange(N_EXPERTS)
    t_ids = np.arange(n_tiles)
    b_ids = np.arange(n_blocks)
    cum_e = jnp.asarray(e_ids[:, None] <= e_ids[None, :], F32)
    cum_t = jnp.asarray(t_ids[:, None] >= t_ids[None, :], F32)
    cum_b = jnp.asarray(b_ids[:, None] <= b_ids[None, :], F32)
    e_row = jnp.asarray(e_ids[None, :], F32)

    pcs = tstat.reshape(n_tiles, 8, LANES)[:, 0, ROUTER_LANE0:ROUTER_LANE0 + N_EXPERTS].astype(F32)
    piece_end = mm(pcs, cum_e)
    piece_loc = piece_end - pcs
    tile_cum = mm(cum_t, pcs)
    rows_e = tile_cum[-1:, :] * ROW_PIECE
    padded = jnp.floor((rows_e + (MOE_TM - 1)) * (1.0 / MOE_TM)) * MOE_TM
    pend = mm(padded, cum_e)
    poff = pend - padded
    piece_glob = poff * (1.0 / ROW_PIECE) + tile_cum - pcs
    nused_f = pend[0, -1] * (1.0 / MOE_TM)
    blk_start = jnp.asarray(b_ids[:, None] * MOE_TM, F32)
    blk_e_f = jnp.minimum(jnp.sum((pend <= blk_start).astype(F32), axis=1, keepdims=True), N_EXPERTS - 1.0)
    blk_oh = (blk_e_f == e_row).astype(F32)
    later_nonempty = (e_ids[None, :] > e_ids[:, None]) & (rows_e > 0)
    next_of_e = jnp.min(jnp.where(later_nonempty, e_row, float(N_EXPERTS)), axis=1)
    next_of_e = jnp.where(next_of_e == N_EXPERTS, -1.0, next_of_e)
    per_blk = mm(blk_oh, jnp.stack([pend[0], (poff + rows_e)[0], next_of_e], axis=1))
    blk_used = blk_start < pend[0, -1]
    zflag = jnp.where(blk_used, (blk_start + MOE_TM == per_blk[:, 0:1]).astype(F32), 2.0)
    rows_in_blk = jnp.clip(per_blk[:, 1:2] - blk_start, 0.0, float(MOE_TM))
    nsub = jnp.floor((rows_in_blk + (EXPERT_SUB - 1)) * (1.0 / EXPERT_SUB))
    prev_e = jnp.concatenate([jnp.full((1, 1), -1.0, F32), blk_e_f[:-1]], axis=0)
    run_first = (blk_used & (blk_e_f != prev_e)).astype(F32)
    run_idx = mm(run_first.reshape(1, n_blocks), cum_b) - 1.0
    run_slot = run_idx - 2.0 * jnp.floor(run_idx * 0.5)
    p_ids = jnp.asarray(np.arange(PIECES_PER_TILE), F32)
    e_of_p = jnp.minimum(jnp.sum((piece_end[:, None, :] <= p_ids[None, :, None]).astype(F32), axis=2),
                         N_EXPERTS - 1.0)
    shift = jnp.sum(jnp.where(e_of_p[:, :, None] == e_row[None], (piece_glob - piece_loc)[:, None, :], 0.0), axis=2)
    glob_of_p = i32(shift + p_ids[None, :]).reshape(n_tiles * PIECES_PER_TILE)
    npieces = i32(piece_end[:, -1])
    flat_b = lambda a: i32(a).reshape(n_blocks)
    blk_e, nused = flat_b(blk_e_f), i32(nused_f).reshape(1)

    buf = _dispatch(xn2, srow, npieces, glob_of_p, flat_b(zflag), n_rows)
    yb = _experts(buf, blk_e, nused, flat_b(nsub), flat_b(run_first), flat_b(per_blk[:, 2:3]), flat_b(run_slot),
                  w_e_gate[0], w_e_up[0], w_e_down[0])
    out = _combine(x2, rti, rtf, g_final[None, :], yb, npieces, glob_of_p)
    return out.reshape(B, S, D)
```

```python
import functools

import numpy as np
import jax
import jax.numpy as jnp
from jax import lax
from jax.experimental import pallas as pl
from jax.experimental.pallas import tpu as pltpu

F32 = jnp.float32
BF16 = jnp.bfloat16

CHUNK = 64
POOL_WINDOWS = (2, 4, 8, 16)
N_HEADS = 4
CONV_K = 4
N_GROUPS = 4
EXPERTS_PER_GROUP = 8
N_EXPERTS = N_GROUPS * EXPERTS_PER_GROUP
TOP_K = 2
EPS = 1e-6

LANES = 128
HALO = 16
ROUTER_LANE0 = N_GROUPS

INPROJ_TN = 256
MIX_TS = 256
MOE_TM = 512
ROW_PIECE = 16
MOE_SL = TOP_K * MIX_TS + N_EXPERTS * ROW_PIECE
PIECES_PER_TILE = MOE_SL // ROW_PIECE
MOE_STEP_TILES = 2
SLOT_SUB = 128
PIECE_UNROLL = 4
EXPERT_SUB = 128
SLOT_RADIX = 16
VMEM_LIMIT = 56 * 1024 * 1024


def _cparams(n_axes):
    return pltpu.CompilerParams(dimension_semantics=("arbitrary",) * n_axes,
                                vmem_limit_bytes=VMEM_LIMIT)


def _sigmoid(v):
    return 0.5 * jnp.tanh(0.5 * v) + 0.5


def _silu(v):
    return v * _sigmoid(v)


def _log_sigmoid(v):
    return jnp.minimum(v, 0.0) - jnp.log1p(jnp.exp(-jnp.abs(v)))


def _split3(v):
    hi = v.astype(BF16)
    r1 = v - hi.astype(F32)
    mid = r1.astype(BF16)
    lo = (r1 - mid.astype(F32)).astype(BF16)
    return hi, mid, lo


def _dot(a, b):
    return jnp.dot(a, b, preferred_element_type=F32)


def _dot_nt(a, b):
    return lax.dot_general(a, b, (((1,), (1,)), ((), ())), preferred_element_type=F32)


def _dot_tn(a, b):
    return lax.dot_general(a, b, (((0,), (0,)), ((), ())), preferred_element_type=F32)


def _inproj_steps(x_ref, g_ref, w_refs, z_refs, xn_ref):
    wm_ref, wg_ref, wif_ref = w_refs
    zm_ref, zg_ref, zif_ref, zift_ref = z_refs

    def norm():
        x = x_ref[...]
        ms = jnp.mean(x * x, axis=-1, keepdims=True)
        xn_ref[...] = (x * lax.rsqrt(ms + EPS) * g_ref[...]).astype(BF16)

    def block(w_ref, z_ref, c0):
        def run():
            cols = slice(c0, c0 + INPROJ_TN)
            z_ref[:, cols] = _dot(xn_ref[...], w_ref[:, cols]).astype(BF16)
        return run

    def gates():
        zif = _dot(xn_ref[...], wif_ref[...])
        zif_ref[...] = zif
        zift_ref[...] = zif.T[:zift_ref.shape[0], :]

    steps = [norm, gates]
    steps += [block(wm_ref, zm_ref, c0) for c0 in range(0, zm_ref.shape[1], INPROJ_TN)]
    steps += [block(wg_ref, zg_ref, c0) for c0 in range(0, zg_ref.shape[1], INPROJ_TN)]
    return steps


def _mixer_kernel(x_ref, xnext_ref, gmix_ref, wm_ref, wg_ref, wif_ref,
                  bif_ref, bift_ref, convq_ref, convk_ref, ghead_ref, wpool_ref, pscale_ref,
                  wa_ref, wb_ref, wo_ref, gffn_ref, wr_ref, br_ref,
                  tric_ref, trir_ref, stri_ref, ut_ref, sel_ref,
                  x2_ref, xn2_ref, rti_ref, rtf_ref, tstat_ref, srow_ref,
                  zm_ref, zg_ref, zif_ref, zift_ref, zm_nxt, zg_nxt, zif_nxt, zift_nxt, xn_ref,
                  ext_ref, q_ref, k_ref, h_ref, pool_ref, cst_ref, mst_ref, lg_ref, *, tiles_per_seq):
    ts = x_ref.shape[0]
    d_pool = wa_ref.shape[0]
    d_ml = wb_ref.shape[0]
    dh = d_ml // N_HEADS
    n_chunks = ts // CHUNK
    g_step = pl.program_id(0)
    j = lax.rem(g_step, tiles_per_seq)
    w_in_refs = (wm_ref, wg_ref, wif_ref)
    z_cur = (zm_ref, zg_ref, zif_ref, zift_ref)
    z_nxt = (zm_nxt, zg_nxt, zif_nxt, zift_nxt)
    first = g_step == 0

    @pl.when(first)
    def _():
        for step in _inproj_steps(x_ref, gmix_ref, w_in_refs, z_cur, xn_ref):
            step()
        lg_ref[...] = jnp.zeros_like(lg_ref)

    @pl.when(jnp.logical_not(first))
    def _():
        for dst, src in zip(z_cur, z_nxt):
            dst[...] = src[...]

    @pl.when(j == 0)
    def _():
        ext_ref[:, :HALO, :] = jnp.zeros((ext_ref.shape[0], HALO, LANES), F32)
        cst_ref[...] = jnp.zeros_like(cst_ref)
        mst_ref[...] = jnp.zeros_like(mst_ref)

    pending = _inproj_steps(xnext_ref, gmix_ref, w_in_refs, z_nxt, xn_ref)

    def project_some(n=1):
        for _ in range(min(n, len(pending))):
            pending.pop(0)()

    project_some(2)

    routed = _route_select(lg_ref[...])

    row = lax.broadcasted_iota(jnp.int32, (ts, LANES), 0)
    pos1 = (row + j * ts + 1).astype(F32)

    def history(cg):
        cur = zm_ref[:, cg * LANES:(cg + 1) * LANES].astype(F32)
        ext_ref[cg, HALO:, :] = cur
        return cur, lambda s: ext_ref[cg, HALO - s:HALO - s + ts, :]

    def keep_history(cg, cur):
        ext_ref[cg, :HALO, :] = cur[ts - HALO:, :]

    n_pool_groups = d_pool // LANES
    for g in range(n_pool_groups):
        w = POOL_WINDOWS[g]
        cur, shifted = history(g)
        win = cur
        for s in range(1, w):
            win = win + shifted(s)
        keep_history(g, cur)
        cnt = jnp.minimum(pos1, float(w))
        d = win / cnt - cur
        y = _dot(d.astype(BF16), wpool_ref[g]) * pscale_ref[:, g * LANES:(g + 1) * LANES]
        pool_ref[:, g * LANES:(g + 1) * LANES] = y.astype(BF16)
        project_some()

    n_ml_groups = d_ml // LANES
    for which, (cw_ref, dst_ref, scale) in enumerate(((convq_ref, q_ref, 1.0), (convk_ref, k_ref, dh ** -0.5))):
        for g in range(n_ml_groups):
            cols = slice(g * LANES, (g + 1) * LANES)
            cg = n_pool_groups + which * n_ml_groups + g
            cur, shifted = history(cg)
            acc = cur * cw_ref[CONV_K - 1:CONV_K, cols]
            for sft in range(1, CONV_K):
                acc = acc + shifted(sft) * cw_ref[CONV_K - 1 - sft:CONV_K - sft, cols]
            keep_history(cg, cur)
            dst_ref[:, cols] = (_silu(acc) * scale).astype(BF16)
        project_some()

    _route_slots(routed, stri_ref, ut_ref, sel_ref, rti_ref, rtf_ref, tstat_ref, srow_ref)
    project_some()

    zc = zif_ref[...] + bif_ref[...]
    lf_c = _log_sigmoid(zc)
    bc = sum(_dot(tric_ref[...], p) for p in _split3(lf_c))
    zr = zift_ref[...] + bift_ref[...]
    lf_r = _log_sigmoid(zr)
    br = sum(_dot(p, trir_ref[...]) for p in _split3(lf_r))
    project_some(2)

    ti = lax.broadcasted_iota(jnp.int32, (CHUNK, CHUNK), 0)
    si = lax.broadcasted_iota(jnp.int32, (CHUNK, CHUNK), 1)
    causal = si <= ti
    ones_blk = jnp.ones((CHUNK, dh), BF16)
    v0 = d_pool + 2 * d_ml
    ig_rep = [jnp.broadcast_to(zc[:, h:h + 1], (ts, dh)) for h in range(N_HEADS)]
    bt_rep = [jnp.broadcast_to(bc[:, N_HEADS + h:N_HEADS + h + 1], (ts, dh)) for h in range(N_HEADS)]

    m_state = [mst_ref[h:h + 1, :] for h in range(N_HEADS)]
    c_state = [cst_ref[h] for h in range(N_HEADS)]
    def stage_scores(c):
        rs = slice(c * CHUNK, (c + 1) * CHUNK)
        out = []
        for h in range(N_HEADS):
            hs = slice(h * dh, (h + 1) * dh)
            q = q_ref[rs, hs]
            k = k_ref[rs, hs]
            bt = bt_rep[h][rs, :]
            r_row = zr[h:h + 1, rs] - br[N_HEADS + h:N_HEADS + h + 1, rs]
            dmat = jnp.where(causal, bt[:, :CHUNK] + r_row, -jnp.inf)
            out.append(dict(q=q, k=k, bt=bt, dmat=dmat, qk=_dot_nt(q, k),
                            m_intra=jnp.max(dmat, axis=-1, keepdims=True)))
        return out

    def stage_state(c, st):
        rs = slice(c * CHUNK, (c + 1) * CHUNK)
        for h in range(N_HEADS):
            s = st[h]
            bt, k = s["bt"], s["k"]
            m_prev, c_prev = m_state[h], c_state[h]
            v_aug = jnp.concatenate([zm_ref[rs, v0 + h * dh:v0 + (h + 1) * dh], ones_blk], axis=-1)
            igc = ig_rep[h][rs, :]
            b_last = bt[CHUNK - 1:CHUNK, :]
            a_log = b_last - bt + igc
            a_max = jnp.max(a_log, axis=0, keepdims=True)
            m_new = jnp.maximum(b_last + m_prev, a_max)
            kw = (k.astype(F32) * jnp.exp(a_log - m_new)).astype(BF16)
            decay = jnp.exp(b_last + m_prev - m_new)
            s.update(v_aug=v_aug, m_prev=m_prev, qc=_dot(s["q"], c_prev.astype(BF16)))
            c_state[h] = jnp.concatenate([decay, decay], axis=-1) * c_prev + _dot_tn(kw, v_aug)
            m_state[h] = m_new

    def stage_values(c, st):
        rs = slice(c * CHUNK, (c + 1) * CHUNK)
        for h in range(N_HEADS):
            s = st[h]
            hs = slice(h * dh, (h + 1) * dh)
            inter = s["bt"] + s["m_prev"]
            m_t = jnp.maximum(inter, s["m_intra"])
            w_inter = jnp.exp(inter - m_t)
            smat = s["qk"] * jnp.exp(s["dmat"] - m_t[:, :CHUNK])
            sv = _dot(smat.astype(BF16), s["v_aug"])
            qc = s["qc"]
            nq = w_inter * qc[:, dh:] + sv[:, dh:]
            den = jnp.maximum(jnp.abs(nq), jnp.exp(-m_t))
            h_ref[rs, hs] = (w_inter * qc[:, :dh] + sv[:, :dh]) / den

    staged = stage_scores(0)
    for c in range(n_chunks):
        stage_state(c, staged)
        project_some()
        nxt = stage_scores(c + 1) if c + 1 < n_chunks else None
        project_some()
        stage_values(c, staged)
        staged = nxt
    for h in range(N_HEADS):
        cst_ref[h] = c_state[h]
        mst_ref[h:h + 1, :] = m_state[h]

    o0 = v0 + d_ml
    for h in range(N_HEADS):
        hs = slice(h * dh, (h + 1) * dh)
        hv = h_ref[:, hs]
        mu = jnp.mean(hv, axis=-1, keepdims=True)
        hc = hv - mu
        var = jnp.mean(hc * hc, axis=-1, keepdims=True)
        hn = hc * lax.rsqrt(var + EPS) * ghead_ref[:, hs]
        og = _sigmoid(zm_ref[:, o0 + h * dh:o0 + (h + 1) * dh].astype(F32))
        q_ref[:, hs] = (og * hn).astype(BF16)
    y_a = _dot(pool_ref[...], wa_ref[...])
    y_b = _dot(q_ref[...], wb_ref[...])
    d_model = x_ref.shape[1]
    ga = _sigmoid(zg_ref[:, :d_model].astype(F32))
    gb = _sigmoid(zg_ref[:, d_model:].astype(F32))
    merged = (ga * y_a + gb * y_b).astype(BF16)
    x2 = x_ref[...] + _dot(merged, wo_ref[...])
    x2_ref[...] = x2
    project_some(len(pending))

    ms = jnp.mean(x2 * x2, axis=-1, keepdims=True)
    xn2 = x2 * lax.rsqrt(ms + EPS) * gffn_ref[...]
    xh = xn2.astype(BF16)
    xn2_ref[...] = xh
    lg_ref[...] = _dot(xh, wr_ref[...]) + br_ref[...]
    project_some(len(pending))


def _route_select(lg):
    ts = lg.shape[0]
    lane = lax.broadcasted_iota(jnp.int32, (ts, LANES), 1)
    lanef = lane.astype(F32)
    big = float(4 * LANES)
    gl = jnp.where(lane < N_GROUPS, lg, -jnp.inf)
    gmax = jnp.max(gl, axis=-1, keepdims=True)
    g_sel = jnp.min(jnp.where(gl == gmax, lanef, big), axis=-1, keepdims=True)
    p_g = 1.0 / jnp.sum(jnp.exp(gl - gmax), axis=-1, keepdims=True)
    lo = ROUTER_LANE0 + EXPERTS_PER_GROUP * g_sel
    el = jnp.where((lanef >= lo) & (lanef < lo + EXPERTS_PER_GROUP), lg, -jnp.inf)
    m1 = jnp.max(el, axis=-1, keepdims=True)
    i1 = jnp.min(jnp.where(el == m1, lanef, big), axis=-1, keepdims=True)
    el2 = jnp.where(lanef == i1, -jnp.inf, el)
    m2 = jnp.max(el2, axis=-1, keepdims=True)
    i2 = jnp.min(jnp.where(el2 == m2, lanef, big), axis=-1, keepdims=True)
    e2x = jnp.exp(m2 - m1)
    gate1 = p_g / (1.0 + e2x)
    gate2 = p_g * e2x / (1.0 + e2x)
    return dict(lane=lane, i1=i1, i2=i2, gate1=gate1, gate2=gate2, oh1=lanef == i1, oh2=lanef == i2)


def _route_slots(r, stri_ref, ut_ref, sel_ref, rti_ref, rtf_ref, tstat_ref, srow_ref):
    lane, oh1, oh2, i1, i2 = r["lane"], r["oh1"], r["oh2"], r["i1"], r["i2"]
    ohs = jnp.where(oh1 | oh2, 1.0, 0.0)
    n_loc = jnp.sum(ohs, axis=0, keepdims=True)
    pieces = jnp.floor((n_loc + (ROW_PIECE - 1.0)) * (1.0 / ROW_PIECE))
    piece_off = _dot(jnp.broadcast_to(pieces, (8, LANES)).astype(BF16), ut_ref[...])[0:1, :]
    base = _dot(stri_ref[...], ohs.astype(BF16)) + ROW_PIECE * piece_off
    slot1 = jnp.sum(jnp.where(oh1, base, 0.0), axis=-1, keepdims=True)
    slot2 = jnp.sum(jnp.where(oh2, base, 0.0), axis=-1, keepdims=True)
    tstat_ref[...] = jnp.broadcast_to(pieces, tstat_ref.shape).astype(jnp.int32)

    rti = jnp.where(lane == 0, i1 - ROUTER_LANE0,
                    jnp.where(lane == 1, i2 - ROUTER_LANE0,
                              jnp.where(lane == 2, slot1, jnp.where(lane == 3, slot2, 0.0))))
    rti_ref[...] = rti.astype(jnp.int32)
    rtf_ref[...] = jnp.where(lane == 0, r["gate1"], jnp.where(lane == 1, r["gate2"], 0.0))
    h1 = jnp.floor(slot1 * (1.0 / SLOT_RADIX))
    h2 = jnp.floor(slot2 * (1.0 / SLOT_RADIX))
    parts = jnp.where(lane == 0, h1, jnp.where(lane == 1, slot1 - SLOT_RADIX * h1,
                      jnp.where(lane == 2, h2, jnp.where(lane == 3, slot2 - SLOT_RADIX * h2, 0.0))))
    srow_ref[...] = _dot_nt(sel_ref[...], parts.astype(BF16))


def _mixer(x2d, g_mix, w_main, w_gates, w_if_c, params, batch, seq):
    T, D = x2d.shape
    ts = min(MIX_TS, seq)
    nts = seq // ts
    d_pool = params["w_br_a"].shape[0]
    d_ml = params["w_br_b"].shape[0]
    dh = d_ml // N_HEADS

    idx = np.arange(ts)
    same_chunk = (idx[:, None] // CHUNK) == (idx[None, :] // CHUNK)
    tri_c = jnp.asarray((idx[None, :] <= idx[:, None]) & same_chunk, BF16)
    tri_r = jnp.asarray((idx[:, None] <= idx[None, :]) & same_chunk, BF16)
    stri = jnp.asarray(idx[None, :] < idx[:, None], BF16)
    lane_idx = np.arange(LANES)
    ut = jnp.asarray(lane_idx[:, None] < lane_idx[None, :], BF16)
    sel = jnp.asarray(np.arange(8)[:, None] == lane_idx[None, :], BF16)

    n_tiles = batch * nts
    tok = lambda g: (g, 0)
    tok_in = lambda g: (jnp.minimum(g, n_tiles - 1), 0)
    tok_next = lambda g: (jnp.minimum(g + 1, n_tiles - 1), 0)
    tok_prev = lambda g: (jnp.maximum(g - 1, 0), 0)
    tok_prev_t = lambda g: (0, jnp.maximum(g - 1, 0))
    c2 = lambda g: (0, 0)
    c3 = lambda g: (0, 0, 0)
    full = lambda a: pl.BlockSpec(a.shape, c2 if a.ndim == 2 else c3)
    consts = [params[n] for n in ("b_if", "b_if_t", "conv_q", "conv_k", "g_head", "w_pool", "pool_scale",
                                  "w_br_a", "w_br_b", "w_out", "g_ffn", "w_r", "b_r")]
    consts = [g_mix, w_main, w_gates, w_if_c] + consts + [tri_c, tri_r, stri, ut, sel]
    nm, ng = w_main.shape[1], w_gates.shape[1]
    z_scratch = [pltpu.VMEM((ts, nm), BF16), pltpu.VMEM((ts, ng), BF16),
                 pltpu.VMEM((ts, LANES), F32), pltpu.VMEM((16, ts), F32)]
    return pl.pallas_call(
        functools.partial(_mixer_kernel, tiles_per_seq=nts),
        grid=(n_tiles + 1,),
        in_specs=[pl.BlockSpec((ts, D), tok_in),
                  pl.BlockSpec((ts, D), tok_next)] + [full(a) for a in consts],
        out_specs=[pl.BlockSpec((ts, D), tok),
                   pl.BlockSpec((ts, D), tok),
                   pl.BlockSpec((ts, LANES), tok_prev),
                   pl.BlockSpec((ts, LANES), tok_prev),
                   pl.BlockSpec((8, LANES), tok_prev),
                   pl.BlockSpec((8, ts), tok_prev_t)],
        out_shape=[jax.ShapeDtypeStruct((T + ts, D), F32),
                   jax.ShapeDtypeStruct((T + ts, D), BF16),
                   jax.ShapeDtypeStruct((T, LANES), jnp.int32),
                   jax.ShapeDtypeStruct((T, LANES), F32),
                   jax.ShapeDtypeStruct((n_tiles * 8, LANES), jnp.int32),
                   jax.ShapeDtypeStruct((8, T), F32)],
        scratch_shapes=z_scratch + z_scratch + [
                        pltpu.VMEM((ts, D), BF16),
                        pltpu.VMEM(((d_pool + 2 * d_ml) // LANES, HALO + ts, LANES), F32),
                        pltpu.VMEM((ts, d_ml), BF16),
                        pltpu.VMEM((ts, d_ml), BF16),
                        pltpu.VMEM((ts, d_ml), F32),
                        pltpu.VMEM((ts, d_pool), BF16),
                        pltpu.VMEM((N_HEADS, dh, 2 * dh), F32),
                        pltpu.VMEM((8, LANES), F32),
                        pltpu.VMEM((ts, LANES), F32)],
        compiler_params=_cparams(1),
        name="mixer",
    )(x2d, x2d, *consts)


def _for_each_piece(npieces_ref, glob_ref, tile, fn):
    base = tile * PIECES_PER_TILE
    n = npieces_ref[tile]

    def one(p):
        fn(pl.multiple_of(p * ROW_PIECE, ROW_PIECE), pl.multiple_of(glob_ref[base + p] * ROW_PIECE, ROW_PIECE))

    def group(g, carry):
        for u in range(PIECE_UNROLL):
            one(g * PIECE_UNROLL + u)
        return carry

    n_groups = lax.div(n, jnp.int32(PIECE_UNROLL))
    lax.fori_loop(0, n_groups, group, 0)
    for u in range(PIECE_UNROLL - 1):
        @pl.when(n_groups * PIECE_UNROLL + u < n)
        def _():
            one(n_groups * PIECE_UNROLL + u)


def _used_slot_groups(n_pieces):
    return lax.div(n_pieces * ROW_PIECE + (SLOT_SUB - 1), jnp.int32(SLOT_SUB))


def _dispatch_kernel(npieces_ref, glob_ref, zflag_ref,
                     xn_ref, srow_ref, buf_ref, rows_ref, zeros_ref, sem, zsem, tsem):
    tt = xn_ref.shape[0] // MOE_STEP_TILES
    sl = rows_ref.shape[1] // MOE_STEP_TILES
    n_blocks = buf_ref.shape[0] // MOE_TM
    i = pl.program_id(0)
    n_steps = pl.num_programs(0)
    cur = lax.rem(i, 2)

    def zero_copy(b, flag):
        return pltpu.make_async_copy(zeros_ref, buf_ref.at[pl.ds(b * MOE_TM, MOE_TM)], zsem if flag == 1 else tsem)

    def for_flagged(flag, fn):
        def body(b, carry):
            @pl.when(zflag_ref[b] == flag)
            def _():
                fn(zero_copy(b, flag))
            return carry
        lax.fori_loop(0, n_blocks, body, 0)

    @pl.when(i == 0)
    def _():
        zeros_ref[...] = jnp.zeros_like(zeros_ref)
        for_flagged(1, lambda cp: cp.start())
        for_flagged(2, lambda cp: cp.start())
        for_flagged(1, lambda cp: cp.wait())

    def piece_copy(buf_slot, sub, local_row, global_row):
        return pltpu.make_async_copy(rows_ref.at[buf_slot, pl.ds(sub * sl + local_row, ROW_PIECE)],
                                     buf_ref.at[pl.ds(global_row, ROW_PIECE)], sem.at[buf_slot])

    def for_step_pieces(step, buf_slot, act):
        for sub in range(MOE_STEP_TILES):
            _for_each_piece(npieces_ref, glob_ref, step * MOE_STEP_TILES + sub,
                            lambda l, g, sub=sub: act(piece_copy(buf_slot, sub, l, g)))

    @pl.when(i >= 2)
    def _():
        for_step_pieces(i - 2, cur, lambda cp: cp.wait())

    for sub in range(MOE_STEP_TILES):
        sr = srow_ref[:, sub * tt:(sub + 1) * tt]
        slot1 = SLOT_RADIX * sr[0:1, :] + sr[1:2, :]
        slot2 = SLOT_RADIX * sr[2:3, :] + sr[3:4, :]
        n_sub = _used_slot_groups(npieces_ref[i * MOE_STEP_TILES + sub])
        for k in range(TOP_K * tt // SLOT_SUB, sl // SLOT_SUB + 1):
            @pl.when(n_sub == k)
            def _(k=k, sub=sub, slot1=slot1, slot2=slot2):
                m = k * SLOT_SUB
                r = lax.broadcasted_iota(jnp.int32, (m, tt), 0).astype(F32)
                sel = jnp.where((r == slot1) | (r == slot2), 1.0, 0.0).astype(BF16)
                rows_ref[cur, sub * sl:sub * sl + m, :] = _dot(sel, xn_ref[sub * tt:(sub + 1) * tt, :]).astype(BF16)
    for_step_pieces(i, cur, lambda cp: cp.start())

    @pl.when(i == n_steps - 1)
    def _():
        @pl.when(i >= 1)
        def _():
            for_step_pieces(i - 1, 1 - cur, lambda cp: cp.wait())
        for_step_pieces(i, cur, lambda cp: cp.wait())
        for_flagged(2, lambda cp: cp.wait())


def _dispatch(xn2, srow, npieces, piece_glob, zflag, n_rows):
    T, D = srow.shape[1], xn2.shape[1]
    tt = MOE_STEP_TILES * MIX_TS
    return pl.pallas_call(
        _dispatch_kernel,
        grid_spec=pltpu.PrefetchScalarGridSpec(
            num_scalar_prefetch=3,
            grid=(T // tt,),
            in_specs=[pl.BlockSpec((tt, D), lambda i, *_: (i, 0)),
                      pl.BlockSpec((8, tt), lambda i, *_: (0, i))],
            out_specs=pl.BlockSpec(memory_space=pl.ANY),
            scratch_shapes=[pltpu.VMEM((2, MOE_STEP_TILES * MOE_SL, D), BF16),
                            pltpu.VMEM((MOE_TM, D), BF16),
                            pltpu.SemaphoreType.DMA((2,)),
                            pltpu.SemaphoreType.DMA(()),
                            pltpu.SemaphoreType.DMA(())]),
        out_shape=jax.ShapeDtypeStruct((n_rows, D), BF16),
        compiler_params=_cparams(1),
        name="dispatch",
    )(npieces, piece_glob, zflag, xn2, srow)


def _experts_kernel(blk_e_ref, nused_ref, nsub_ref, first_ref, next_e_ref, slot_ref,
                    x_ref, wg_hbm, wu_hbm, wd_hbm, y_ref,
                    wg32_ref, wu32_ref, wd32_ref, wgb_ref, wub_ref, wdb_ref, sem):
    i = pl.program_id(0)
    used = i < nused_ref[0]
    n_sub = nsub_ref[i]
    landing = ((wg_hbm, wg32_ref), (wu_hbm, wu32_ref), (wd_hbm, wd32_ref))

    def weight_copies(e, s):
        return [pltpu.make_async_copy(hbm.at[e], vmem.at[s], sem.at[s, n]) for n, (hbm, vmem) in enumerate(landing)]

    @pl.when(used & (i == 0))
    def _():
        for cp in weight_copies(blk_e_ref[0], 0):
            cp.start()

    @pl.when(used & (first_ref[i] > 0))
    def _():
        s = slot_ref[i]
        for cp in weight_copies(blk_e_ref[i], s):
            cp.wait()
        wgb_ref[...] = wg32_ref[s].astype(BF16)
        wub_ref[...] = wu32_ref[s].astype(BF16)
        wdb_ref[...] = wd32_ref[s].astype(BF16)

        @pl.when(next_e_ref[i] >= 0)
        def _():
            for cp in weight_copies(next_e_ref[i], 1 - s):
                cp.start()

    for k in range(1, MOE_TM // EXPERT_SUB + 1):
        @pl.when(used & (n_sub == k))
        def _(k=k):
            m = k * EXPERT_SUB
            x = x_ref[:m, :]
            hg = _dot(x, wgb_ref[...])
            hu = _dot(x, wub_ref[...])
            hid = (_silu(hg) * hu).astype(BF16)
            y_ref[:m, :] = _dot(hid, wdb_ref[...]).astype(BF16)
            if m < MOE_TM:
                y_ref[m:, :] = jnp.zeros((MOE_TM - m, y_ref.shape[1]), BF16)


def _experts(buf, blk_e, nused, nsub, run_first, next_e, run_slot, w_gate, w_up, w_down):
    R, D = buf.shape
    de = w_gate.shape[2]
    n_blocks = R // MOE_TM
    row_map = lambda i, be, nu, *_: (jnp.minimum(i, nu[0] - 1), 0)
    return pl.pallas_call(
        _experts_kernel,
        grid_spec=pltpu.PrefetchScalarGridSpec(
            num_scalar_prefetch=6,
            grid=(n_blocks,),
            in_specs=[pl.BlockSpec((MOE_TM, D), row_map),
                      pl.BlockSpec(memory_space=pl.ANY),
                      pl.BlockSpec(memory_space=pl.ANY),
                      pl.BlockSpec(memory_space=pl.ANY)],
            out_specs=pl.BlockSpec((MOE_TM, D), row_map),
            scratch_shapes=[pltpu.VMEM((2, D, de), F32), pltpu.VMEM((2, D, de), F32), pltpu.VMEM((2, de, D), F32),
                            pltpu.VMEM((D, de), BF16), pltpu.VMEM((D, de), BF16), pltpu.VMEM((de, D), BF16),
                            pltpu.SemaphoreType.DMA((2, 3))]),
        out_shape=jax.ShapeDtypeStruct((R, D), BF16),
        input_output_aliases={6: 0},
        compiler_params=_cparams(1),
        name="experts",
    )(blk_e, nused, nsub, run_first, next_e, run_slot, buf, w_gate, w_up, w_down)


def _combine_kernel(npieces_ref, glob_ref, x2_ref, rti_ref, rtf_ref, gfin_ref, yb_ref, out_ref,
                    rows_ref, sem):
    tt = x2_ref.shape[0] // MOE_STEP_TILES
    sl = rows_ref.shape[1] // MOE_STEP_TILES
    i = pl.program_id(0)
    n_steps = pl.num_programs(0)
    cur = lax.rem(i, 2)

    def piece_copy(buf_slot, sub, local_row, global_row):
        return pltpu.make_async_copy(yb_ref.at[pl.ds(global_row, ROW_PIECE)],
                                     rows_ref.at[buf_slot, pl.ds(sub * sl + local_row, ROW_PIECE)], sem.at[buf_slot])

    def for_step_pieces(step, buf_slot, act):
        for sub in range(MOE_STEP_TILES):
            _for_each_piece(npieces_ref, glob_ref, step * MOE_STEP_TILES + sub,
                            lambda l, g, sub=sub: act(piece_copy(buf_slot, sub, l, g)))

    @pl.when(i == 0)
    def _():
        rows_ref[...] = jnp.zeros_like(rows_ref)
        for_step_pieces(0, 0, lambda cp: cp.start())

    @pl.when(i + 1 < n_steps)
    def _():
        for_step_pieces(i + 1, 1 - cur, lambda cp: cp.start())

    for_step_pieces(i, cur, lambda cp: cp.wait())

    for sub in range(MOE_STEP_TILES):
        ts_rows = slice(sub * tt, (sub + 1) * tt)
        rti = rti_ref[ts_rows, :]
        rtf = rtf_ref[ts_rows, :]
        slot1 = rti[:, 2:3]
        slot2 = rti[:, 3:4]
        n_sub = _used_slot_groups(npieces_ref[i * MOE_STEP_TILES + sub])
        for k in range(TOP_K * tt // SLOT_SUB, sl // SLOT_SUB + 1):
            @pl.when(n_sub == k)
            def _(k=k, sub=sub, ts_rows=ts_rows, rtf=rtf, slot1=slot1, slot2=slot2):
                m = k * SLOT_SUB
                lane = lax.broadcasted_iota(jnp.int32, (tt, m), 1)
                g = jnp.where(lane == slot1, rtf[:, 0:1], jnp.where(lane == slot2, rtf[:, 1:2], 0.0)).astype(BF16)
                y = x2_ref[ts_rows, :] + _dot(g, rows_ref[cur, sub * sl:sub * sl + m, :])
                ms = jnp.mean(y * y, axis=-1, keepdims=True)
                out_ref[ts_rows, :] = y * lax.rsqrt(ms + EPS) * gfin_ref[...]


def _combine(x2, rti, rtf, g_final, yb, npieces, piece_glob):
    T, D = rti.shape[0], x2.shape[1]
    tt = MOE_STEP_TILES * MIX_TS
    tok = lambda i, *_: (i, 0)
    return pl.pallas_call(
        _combine_kernel,
        grid_spec=pltpu.PrefetchScalarGridSpec(
            num_scalar_prefetch=2,
            grid=(T // tt,),
            in_specs=[pl.BlockSpec((tt, D), tok),
                      pl.BlockSpec((tt, LANES), tok),
                      pl.BlockSpec((tt, LANES), tok),
                      pl.BlockSpec((1, D), lambda i, *_: (0, 0)),
                      pl.BlockSpec(memory_space=pl.ANY)],
            out_specs=pl.BlockSpec((tt, D), tok),
            scratch_shapes=[pltpu.VMEM((2, MOE_STEP_TILES * MOE_SL, D), BF16),
                            pltpu.SemaphoreType.DMA((2,))]),
        out_shape=jax.ShapeDtypeStruct((T, D), F32),
        compiler_params=_cparams(1),
        name="combine",
    )(npieces, piece_glob, x2, rti, rtf, g_final, yb)


def _pad_lanes(a, width=LANES):
    return jnp.pad(a, ((0, 0), (0, width - a.shape[1])))


def kernel(x, g_mix, w_in, b_if, conv_q, conv_k, g_head, w_pool, pool_scale, w_br_a, w_br_b, w_out,
           g_ffn, w_rg, b_rg, w_re, b_re, w_e_gate, w_e_up, w_e_down, g_final):
    B, S, D = x.shape
    T = B * S
    assert g_mix.shape[0] == 1, "single-layer block"
    assert S % MIX_TS == 0 and (T // MIX_TS) % MOE_STEP_TILES == 0
    d_pool = w_br_a.shape[1]
    d_ml = w_br_b.shape[1]
    x2d = x.reshape(T, D)

    n_main = d_pool + 4 * d_ml
    w_l = w_in[0]
    w_main = w_l[:, :n_main].astype(BF16)
    w_if = w_l[:, n_main:n_main + 2 * N_HEADS]
    w_gates = w_l[:, n_main + 2 * N_HEADS:].astype(BF16)
    w_if_c = _pad_lanes(w_if).astype(BF16)
    params = {
        "b_if": _pad_lanes(b_if[0][None, :]),
        "b_if_t": jnp.pad(b_if[0][:, None], ((0, 16 - 2 * N_HEADS), (0, 0))),
        "conv_q": conv_q[0], "conv_k": conv_k[0],
        "g_head": g_head[0][None, :],
        "w_pool": w_pool[0].astype(BF16),
        "pool_scale": pool_scale[0][None, :],
        "w_br_a": w_br_a[0].astype(BF16), "w_br_b": w_br_b[0].astype(BF16),
        "w_out": w_out[0].astype(BF16),
        "g_ffn": g_ffn[0][None, :],
        "w_r": _pad_lanes(jnp.concatenate([w_rg[0], w_re[0]], axis=1)).astype(BF16),
        "b_r": _pad_lanes(jnp.concatenate([b_rg[0], b_re[0]])[None, :]),
    }

    x2, xn2, rti, rtf, tstat, srow = _mixer(x2d, g_mix[0][None, :], w_main, w_gates, w_if_c, params, B, S)

    n_tiles = T // MIX_TS
    n_rows = n_tiles * MOE_SL + N_EXPERTS * MOE_TM
    n_blocks = n_rows // MOE_TM
    i32 = lambda a: a.astype(jnp.int32)
    mm = lambda a, b: jnp.round(jnp.dot(a, b, precision=lax.Precision.HIGHEST, preferred_element_type=F32))
    e_ids = np.arange(N_EXPERTS)
    t_ids = np.arange(n_tiles)
    b_ids = np.arange(n_blocks)
    cum_e = jnp.asarray(e_ids[:, None] <= e_ids[None, :], F32)
    cum_t = jnp.asarray(t_ids[:, None] >= t_ids[None, :], F32)
    cum_b = jnp.asarray(b_ids[:, None] <= b_ids[None, :], F32)
    e_row = jnp.asarray(e_ids[None, :], F32)

    pcs = tstat.reshape(n_tiles, 8, LANES)[:, 0, ROUTER_LANE0:ROUTER_LANE0 + N_EXPERTS].astype(F32)
    piece_end = mm(pcs, cum_e)
    piece_loc = piece_end - pcs
    tile_cum = mm(cum_t, pcs)
    rows_e = tile_cum[-1:, :] * ROW_PIECE
    padded = jnp.floor((rows_e + (MOE_TM - 1)) * (1.0 / MOE_TM)) * MOE_TM
    pend = mm(padded, cum_e)
    poff = pend - padded
    piece_glob = poff * (1.0 / ROW_PIECE) + tile_cum - pcs
    nused_f = pend[0, -1] * (1.0 / MOE_TM)
    blk_start = jnp.asarray(b_ids[:, None] * MOE_TM, F32)
    blk_e_f = jnp.minimum(jnp.sum((pend <= blk_start).astype(F32), axis=1, keepdims=True), N_EXPERTS - 1.0)
    blk_oh = (blk_e_f == e_row).astype(F32)
    later_nonempty = (e_ids[None, :] > e_ids[:, None]) & (rows_e > 0)
    next_of_e = jnp.min(jnp.where(later_nonempty, e_row, float(N_EXPERTS)), axis=1)
    next_of_e = jnp.where(next_of_e == N_EXPERTS, -1.0, next_of_e)
    per_blk = mm(blk_oh, jnp.stack([pend[0], (poff + rows_e)[0], next_of_e], axis=1))
    blk_used = blk_start < pend[0, -1]
    zflag = jnp.where(blk_used, (blk_start + MOE_TM == per_blk[:, 0:1]).astype(F32), 2.0)
    rows_in_blk = jnp.clip(per_blk[:, 1:2] - blk_start, 0.0, float(MOE_TM))
    nsub = jnp.floor((rows_in_blk + (EXPERT_SUB - 1)) * (1.0 / EXPERT_SUB))
    prev_e = jnp.concatenate([jnp.full((1, 1), -1.0, F32), blk_e_f[:-1]], axis=0)
    run_first = (blk_used & (blk_e_f != prev_e)).astype(F32)
    run_idx = mm(run_first.reshape(1, n_blocks), cum_b) - 1.0
    run_slot = run_idx - 2.0 * jnp.floor(run_idx * 0.5)
    p_ids = jnp.asarray(np.arange(PIECES_PER_TILE), F32)
    e_of_p = jnp.minimum(jnp.sum((piece_end[:, None, :] <= p_ids[None, :, None]).astype(F32), axis=2),
                         N_EXPERTS - 1.0)
    shift = jnp.sum(jnp.where(e_of_p[:, :, None] == e_row[None], (piece_glob - piece_loc)[:, None, :], 0.0), axis=2)
    glob_of_p = i32(shift + p_ids[None, :]).reshape(n_tiles * PIECES_PER_TILE)
    npieces = i32(piece_end[:, -1])
    flat_b = lambda a: i32(a).reshape(n_blocks)
    blk_e, nused = flat_b(blk_e_f), i32(nused_f).reshape(1)

    buf = _dispatch(xn2, srow, npieces, glob_of_p, flat_b(zflag), n_rows)
    yb = _experts(buf, blk_e, nused, flat_b(nsub), flat_b(run_first), flat_b(per_blk[:, 2:3]), flat_b(run_slot),
                  w_e_gate[0], w_e_up[0], w_e_down[0])
    out = _combine(x2, rti, rtf, g_final[None, :], yb, npieces, glob_of_p)
    return out.reshape(B, S, D)
```

```python
import functools

import numpy as np
import jax
import jax.numpy as jnp
from jax import lax
from jax.experimental import pallas as pl
from jax.experimental.pallas import tpu as pltpu

F32 = jnp.float32
BF16 = jnp.bfloat16

CHUNK = 64
POOL_WINDOWS = (2, 4, 8, 16)
N_HEADS = 4
CONV_K = 4
N_GROUPS = 4
EXPERTS_PER_GROUP = 8
N_EXPERTS = N_GROUPS * EXPERTS_PER_GROUP
TOP_K = 2
EPS = 1e-6

LANES = 128
HALO = 16
ROUTER_LANE0 = N_GROUPS

INPROJ_TN = 256
MIX_TS = 256
MOE_TM = 512
ROW_PIECE = 16
MOE_SL = TOP_K * MIX_TS + N_EXPERTS * ROW_PIECE
PIECES_PER_TILE = MOE_SL // ROW_PIECE
MOE_STEP_TILES = 2
SLOT_SUB = 128
PIECE_UNROLL = 4
EXPERT_SUB = 128
SLOT_RADIX = 16
VMEM_LIMIT = 56 * 1024 * 1024


def _cparams(n_axes):
    return pltpu.CompilerParams(dimension_semantics=("arbitrary",) * n_axes,
                                vmem_limit_bytes=VMEM_LIMIT)


def _sigmoid(v):
    return 0.5 * jnp.tanh(0.5 * v) + 0.5


def _silu(v):
    return v * _sigmoid(v)


def _log_sigmoid(v):
    return jnp.minimum(v, 0.0) - jnp.log1p(jnp.exp(-jnp.abs(v)))


def _split3(v):
    hi = v.astype(BF16)
    r1 = v - hi.astype(F32)
    mid = r1.astype(BF16)
    lo = (r1 - mid.astype(F32)).astype(BF16)
    return hi, mid, lo


def _dot(a, b):
    return jnp.dot(a, b, preferred_element_type=F32)


def _dot_nt(a, b):
    return lax.dot_general(a, b, (((1,), (1,)), ((), ())), preferred_element_type=F32)


def _dot_tn(a, b):
    return lax.dot_general(a, b, (((0,), (0,)), ((), ())), preferred_element_type=F32)


def _inproj_steps(x_ref, g_ref, w_refs, z_refs, xn_ref):
    wm_ref, wg_ref, wif_ref = w_refs
    zm_ref, zg_ref, zif_ref, zift_ref = z_refs

    def norm():
        x = x_ref[...]
        ms = jnp.mean(x * x, axis=-1, keepdims=True)
        xn_ref[...] = (x * lax.rsqrt(ms + EPS) * g_ref[...]).astype(BF16)

    def block(w_ref, z_ref, c0):
        def run():
            cols = slice(c0, c0 + INPROJ_TN)
            z_ref[:, cols] = _dot(xn_ref[...], w_ref[:, cols]).astype(BF16)
        return run

    def gates():
        zif = _dot(xn_ref[...], wif_ref[...])
        zif_ref[...] = zif
        zift_ref[...] = zif.T[:zift_ref.shape[0], :]

    steps = [norm, gates]
    steps += [block(wm_ref, zm_ref, c0) for c0 in range(0, zm_ref.shape[1], INPROJ_TN)]
    steps += [block(wg_ref, zg_ref, c0) for c0 in range(0, zg_ref.shape[1], INPROJ_TN)]
    return steps


def _mixer_kernel(x_ref, xnext_ref, gmix_ref, wm_ref, wg_ref, wif_ref,
                  bif_ref, bift_ref, convq_ref, convk_ref, ghead_ref, wpool_ref, pscale_ref,
                  wa_ref, wb_ref, wo_ref, gffn_ref, wr_ref, br_ref,
                  tric_ref, trir_ref, stri_ref, ut_ref, sel_ref,
                  x2_ref, xn2_ref, rti_ref, rtf_ref, tstat_ref, srow_ref,
                  zm_ref, zg_ref, zif_ref, zift_ref, zm_nxt, zg_nxt, zif_nxt, zift_nxt, xn_ref,
                  ext_ref, q_ref, k_ref, h_ref, pool_ref, cst_ref, mst_ref, lg_ref, *, tiles_per_seq):
    ts = x_ref.shape[0]
    d_pool = wa_ref.shape[0]
    d_ml = wb_ref.shape[0]
    dh = d_ml // N_HEADS
    n_chunks = ts // CHUNK
    g_step = pl.program_id(0)
    j = lax.rem(g_step, tiles_per_seq)
    w_in_refs = (wm_ref, wg_ref, wif_ref)
    z_cur = (zm_ref, zg_ref, zif_ref, zift_ref)
    z_nxt = (zm_nxt, zg_nxt, zif_nxt, zift_nxt)
    first = g_step == 0

    @pl.when(first)
    def _():
        for step in _inproj_steps(x_ref, gmix_ref, w_in_refs, z_cur, xn_ref):
            step()
        lg_ref[...] = jnp.zeros_like(lg_ref)

    @pl.when(jnp.logical_not(first))
    def _():
        for dst, src in zip(z_cur, z_nxt):
            dst[...] = src[...]

    @pl.when(j == 0)
    def _():
        ext_ref[:, :HALO, :] = jnp.zeros((ext_ref.shape[0], HALO, LANES), F32)
        cst_ref[...] = jnp.zeros_like(cst_ref)
        mst_ref[...] = jnp.zeros_like(mst_ref)

    pending = _inproj_steps(xnext_ref, gmix_ref, w_in_refs, z_nxt, xn_ref)

    def project_some(n=1):
        for _ in range(min(n, len(pending))):
            pending.pop(0)()

    project_some(2)

    routed = _route_select(lg_ref[...])

    row = lax.broadcasted_iota(jnp.int32, (ts, LANES), 0)
    pos1 = (row + j * ts + 1).astype(F32)

    def history(cg):
        cur = zm_ref[:, cg * LANES:(cg + 1) * LANES].astype(F32)
        ext_ref[cg, HALO:, :] = cur
        return cur, lambda s: ext_ref[cg, HALO - s:HALO - s + ts, :]

    def keep_history(cg, cur):
        ext_ref[cg, :HALO, :] = cur[ts - HALO:, :]

    n_pool_groups = d_pool // LANES
    for g in range(n_pool_groups):
        w = POOL_WINDOWS[g]
        cur, shifted = history(g)
        win = cur
        for s in range(1, w):
            win = win + shifted(s)
        keep_history(g, cur)
        cnt = jnp.minimum(pos1, float(w))
        d = win / cnt - cur
        y = _dot(d.astype(BF16), wpool_ref[g]) * pscale_ref[:, g * LANES:(g + 1) * LANES]
        pool_ref[:, g * LANES:(g + 1) * LANES] = y.astype(BF16)
        project_some()

    n_ml_groups = d_ml // LANES
    for which, (cw_ref, dst_ref, scale) in enumerate(((convq_ref, q_ref, 1.0), (convk_ref, k_ref, dh ** -0.5))):
        for g in range(n_ml_groups):
            cols = slice(g * LANES, (g + 1) * LANES)
            cg = n_pool_groups + which * n_ml_groups + g
            cur, shifted = history(cg)
            acc = cur * cw_ref[CONV_K - 1:CONV_K, cols]
            for sft in range(1, CONV_K):
                acc = acc + shifted(sft) * cw_ref[CONV_K - 1 - sft:CONV_K - sft, cols]
            keep_history(cg, cur)
            dst_ref[:, cols] = (_silu(acc) * scale).astype(BF16)
        project_some()

    _route_slots(routed, stri_ref, ut_ref, sel_ref, rti_ref, rtf_ref, tstat_ref, srow_ref)
    project_some()

    zc = zif_ref[...] + bif_ref[...]
    lf_c = _log_sigmoid(zc)
    bc = sum(_dot(tric_ref[...], p) for p in _split3(lf_c))
    zr = zift_ref[...] + bift_ref[...]
    lf_r = _log_sigmoid(zr)
    br = sum(_dot(p, trir_ref[...]) for p in _split3(lf_r))
    project_some(2)

    ti = lax.broadcasted_iota(jnp.int32, (CHUNK, CHUNK), 0)
    si = lax.broadcasted_iota(jnp.int32, (CHUNK, CHUNK), 1)
    causal = si <= ti
    ones_blk = jnp.ones((CHUNK, dh), BF16)
    v0 = d_pool + 2 * d_ml
    ig_rep = [jnp.broadcast_to(zc[:, h:h + 1], (ts, dh)) for h in range(N_HEADS)]
    bt_rep = [jnp.broadcast_to(bc[:, N_HEADS + h:N_HEADS + h + 1], (ts, dh)) for h in range(N_HEADS)]

    m_state = [mst_ref[h:h + 1, :] for h in range(N_HEADS)]
    c_state = [cst_ref[h] for h in range(N_HEADS)]
    def stage_scores(c):
        rs = slice(c * CHUNK, (c + 1) * CHUNK)
        out = []
        for h in range(N_HEADS):
            hs = slice(h * dh, (h + 1) * dh)
            q = q_ref[rs, hs]
            k = k_ref[rs, hs]
            bt = bt_rep[h][rs, :]
            r_row = zr[h:h + 1, rs] - br[N_HEADS + h:N_HEADS + h + 1, rs]
            dmat = jnp.where(causal, bt[:, :CHUNK] + r_row, -jnp.inf)
            out.append(dict(q=q, k=k, bt=bt, dmat=dmat, qk=_dot_nt(q, k),
                            m_intra=jnp.max(dmat, axis=-1, keepdims=True)))
        return out

    def stage_state(c, st):
        rs = slice(c * CHUNK, (c + 1) * CHUNK)
        for h in range(N_HEADS):
            s = st[h]
            bt, k = s["bt"], s["k"]
            m_prev, c_prev = m_state[h], c_state[h]
            v_aug = jnp.concatenate([zm_ref[rs, v0 + h * dh:v0 + (h + 1) * dh], ones_blk], axis=-1)
            igc = ig_rep[h][rs, :]
            b_last = bt[CHUNK - 1:CHUNK, :]
            a_log = b_last - bt + igc
            a_max = jnp.max(a_log, axis=0, keepdims=True)
            m_new = jnp.maximum(b_last + m_prev, a_max)
            kw = (k.astype(F32) * jnp.exp(a_log - m_new)).astype(BF16)
            decay = jnp.exp(b_last + m_prev - m_new)
            s.update(v_aug=v_aug, m_prev=m_prev, qc=_dot(s["q"], c_prev.astype(BF16)))
            c_state[h] = jnp.concatenate([decay, decay], axis=-1) * c_prev + _dot_tn(kw, v_aug)
            m_state[h] = m_new

    def stage_values(c, st):
        rs = slice(c * CHUNK, (c + 1) * CHUNK)
        for h in range(N_HEADS):
            s = st[h]
            hs = slice(h * dh, (h + 1) * dh)
            inter = s["bt"] + s["m_prev"]
            m_t = jnp.maximum(inter, s["m_intra"])
            w_inter = jnp.exp(inter - m_t)
            smat = s["qk"] * jnp.exp(s["dmat"] - m_t[:, :CHUNK])
            sv = _dot(smat.astype(BF16), s["v_aug"])
            qc = s["qc"]
            nq = w_inter * qc[:, dh:] + sv[:, dh:]
            den = jnp.maximum(jnp.abs(nq), jnp.exp(-m_t))
            h_ref[rs, hs] = (w_inter * qc[:, :dh] + sv[:, :dh]) / den

    staged = stage_scores(0)
    for c in range(n_chunks):
        stage_state(c, staged)
        project_some()
        nxt = stage_scores(c + 1) if c + 1 < n_chunks else None
        project_some()
        stage_values(c, staged)
        staged = nxt
    for h in range(N_HEADS):
        cst_ref[h] = c_state[h]
        mst_ref[h:h + 1, :] = m_state[h]

    o0 = v0 + d_ml
    for h in range(N_HEADS):
        hs = slice(h * dh, (h + 1) * dh)
        hv = h_ref[:, hs]
        mu = jnp.mean(hv, axis=-1, keepdims=True)
        hc = hv - mu
        var = jnp.mean(hc * hc, axis=-1, keepdims=True)
        hn = hc * lax.rsqrt(var + EPS) * ghead_ref[:, hs]
        og = _sigmoid(zm_ref[:, o0 + h * dh:o0 + (h + 1) * dh].astype(F32))
        q_ref[:, hs] = (og * hn).astype(BF16)
    y_a = _dot(pool_ref[...], wa_ref[...])
    y_b = _dot(q_ref[...], wb_ref[...])
    d_model = x_ref.shape[1]
    ga = _sigmoid(zg_ref[:, :d_model].astype(F32))
    gb = _sigmoid(zg_ref[:, d_model:].astype(F32))
    merged = (ga * y_a + gb * y_b).astype(BF16)
    x2 = x_ref[...] + _dot(merged, wo_ref[...])
    x2_ref[...] = x2
    project_some(len(pending))

    ms = jnp.mean(x2 * x2, axis=-1, keepdims=True)
    xn2 = x2 * lax.rsqrt(ms + EPS) * gffn_ref[...]
    xh = xn2.astype(BF16)
    xn2_ref[...] = xh
    lg_ref[...] = _dot(xh, wr_ref[...]) + br_ref[...]
    project_some(len(pending))


def _route_select(lg):
    ts = lg.shape[0]
    lane = lax.broadcasted_iota(jnp.int32, (ts, LANES), 1)
    lanef = lane.astype(F32)
    big = float(4 * LANES)
    gl = jnp.where(lane < N_GROUPS, lg, -jnp.inf)
    gmax = jnp.max(gl, axis=-1, keepdims=True)
    g_sel = jnp.min(jnp.where(gl == gmax, lanef, big), axis=-1, keepdims=True)
    p_g = 1.0 / jnp.sum(jnp.exp(gl - gmax), axis=-1, keepdims=True)
    lo = ROUTER_LANE0 + EXPERTS_PER_GROUP * g_sel
    el = jnp.where((lanef >= lo) & (lanef < lo + EXPERTS_PER_GROUP), lg, -jnp.inf)
    m1 = jnp.max(el, axis=-1, keepdims=True)
    i1 = jnp.min(jnp.where(el == m1, lanef, big), axis=-1, keepdims=True)
    el2 = jnp.where(lanef == i1, -jnp.inf, el)
    m2 = jnp.max(el2, axis=-1, keepdims=True)
    i2 = jnp.min(jnp.where(el2 == m2, lanef, big), axis=-1, keepdims=True)
    e2x = jnp.exp(m2 - m1)
    gate1 = p_g / (1.0 + e2x)
    gate2 = p_g * e2x / (1.0 + e2x)
    return dict(lane=lane, i1=i1, i2=i2, gate1=gate1, gate2=gate2, oh1=lanef == i1, oh2=lanef == i2)


def _route_slots(r, stri_ref, ut_ref, sel_ref, rti_ref, rtf_ref, tstat_ref, srow_ref):
    lane, oh1, oh2, i1, i2 = r["lane"], r["oh1"], r["oh2"], r["i1"], r["i2"]
    ohs = jnp.where(oh1 | oh2, 1.0, 0.0)
    n_loc = jnp.sum(ohs, axis=0, keepdims=True)
    pieces = jnp.floor((n_loc + (ROW_PIECE - 1.0)) * (1.0 / ROW_PIECE))
    piece_off = _dot(jnp.broadcast_to(pieces, (8, LANES)).astype(BF16), ut_ref[...])[0:1, :]
    base = _dot(stri_ref[...], ohs.astype(BF16)) + ROW_PIECE * piece_off
    slot1 = jnp.sum(jnp.where(oh1, base, 0.0), axis=-1, keepdims=True)
    slot2 = jnp.sum(jnp.where(oh2, base, 0.0), axis=-1, keepdims=True)
    tstat_ref[...] = jnp.broadcast_to(pieces, tstat_ref.shape).astype(jnp.int32)

    rti = jnp.where(lane == 0, i1 - ROUTER_LANE0,
                    jnp.where(lane == 1, i2 - ROUTER_LANE0,
                              jnp.where(lane == 2, slot1, jnp.where(lane == 3, slot2, 0.0))))
    rti_ref[...] = rti.astype(jnp.int32)
    rtf_ref[...] = jnp.where(lane == 0, r["gate1"], jnp.where(lane == 1, r["gate2"], 0.0))
    h1 = jnp.floor(slot1 * (1.0 / SLOT_RADIX))
    h2 = jnp.floor(slot2 * (1.0 / SLOT_RADIX))
    parts = jnp.where(lane == 0, h1, jnp.where(lane == 1, slot1 - SLOT_RADIX * h1,
                      jnp.where(lane == 2, h2, jnp.where(lane == 3, slot2 - SLOT_RADIX * h2, 0.0))))
    srow_ref[...] = _dot_nt(sel_ref[...], parts.astype(BF16))


def _mixer(x2d, g_mix, w_main, w_gates, w_if_c, params, batch, seq):
    T, D = x2d.shape
    ts = min(MIX_TS, seq)
    nts = seq // ts
    d_pool = params["w_br_a"].shape[0]
    d_ml = params["w_br_b"].shape[0]
    dh = d_ml // N_HEADS

    idx = np.arange(ts)
    same_chunk = (idx[:, None] // CHUNK) == (idx[None, :] // CHUNK)
    tri_c = jnp.asarray((idx[None, :] <= idx[:, None]) & same_chunk, BF16)
    tri_r = jnp.asarray((idx[:, None] <= idx[None, :]) & same_chunk, BF16)
    stri = jnp.asarray(idx[None, :] < idx[:, None], BF16)
    lane_idx = np.arange(LANES)
    ut = jnp.asarray(lane_idx[:, None] < lane_idx[None, :], BF16)
    sel = jnp.asarray(np.arange(8)[:, None] == lane_idx[None, :], BF16)

    n_tiles = batch * nts
    tok = lambda g: (g, 0)
    tok_in = lambda g: (jnp.minimum(g, n_tiles - 1), 0)
    tok_next = lambda g: (jnp.minimum(g + 1, n_tiles - 1), 0)
    tok_prev = lambda g: (jnp.maximum(g - 1, 0), 0)
    tok_prev_t = lambda g: (0, jnp.maximum(g - 1, 0))
    c2 = lambda g: (0, 0)
    c3 = lambda g: (0, 0, 0)
    full = lambda a: pl.BlockSpec(a.shape, c2 if a.ndim == 2 else c3)
    consts = [params[n] for n in ("b_if", "b_if_t", "conv_q", "conv_k", "g_head", "w_pool", "pool_scale",
                                  "w_br_a", "w_br_b", "w_out", "g_ffn", "w_r", "b_r")]
    consts = [g_mix, w_main, w_gates, w_if_c] + consts + [tri_c, tri_r, stri, ut, sel]
    nm, ng = w_main.shape[1], w_gates.shape[1]
    z_scratch = [pltpu.VMEM((ts, nm), BF16), pltpu.VMEM((ts, ng), BF16),
                 pltpu.VMEM((ts, LANES), F32), pltpu.VMEM((16, ts), F32)]
    return pl.pallas_call(
        functools.partial(_mixer_kernel, tiles_per_seq=nts),
        grid=(n_tiles + 1,),
        in_specs=[pl.BlockSpec((ts, D), tok_in),
                  pl.BlockSpec((ts, D), tok_next)] + [full(a) for a in consts],
        out_specs=[pl.BlockSpec((ts, D), tok),
                   pl.BlockSpec((ts, D), tok),
                   pl.BlockSpec((ts, LANES), tok_prev),
                   pl.BlockSpec((ts, LANES), tok_prev),
                   pl.BlockSpec((8, LANES), tok_prev),
                   pl.BlockSpec((8, ts), tok_prev_t)],
        out_shape=[jax.ShapeDtypeStruct((T + ts, D), F32),
                   jax.ShapeDtypeStruct((T + ts, D), BF16),
                   jax.ShapeDtypeStruct((T, LANES), jnp.int32),
                   jax.ShapeDtypeStruct((T, LANES), F32),
                   jax.ShapeDtypeStruct((n_tiles * 8, LANES), jnp.int32),
                   jax.ShapeDtypeStruct((8, T), F32)],
        scratch_shapes=z_scratch + z_scratch + [
                        pltpu.VMEM((ts, D), BF16),
                        pltpu.VMEM(((d_pool + 2 * d_ml) // LANES, HALO + ts, LANES), F32),
                        pltpu.VMEM((ts, d_ml), BF16),
                        pltpu.VMEM((ts, d_ml), BF16),
                        pltpu.VMEM((ts, d_ml), F32),
                        pltpu.VMEM((ts, d_pool), BF16),
                        pltpu.VMEM((N_HEADS, dh, 2 * dh), F32),
                        pltpu.VMEM((8, LANES), F32),
                        pltpu.VMEM((ts, LANES), F32)],
        compiler_params=_cparams(1),
        name="mixer",
    )(x2d, x2d, *consts)


def _for_each_piece(npieces_ref, glob_ref, tile, fn):
    base = tile * PIECES_PER_TILE
    n = npieces_ref[tile]

    def one(p):
        fn(pl.multiple_of(p * ROW_PIECE, ROW_PIECE), pl.multiple_of(glob_ref[base + p] * ROW_PIECE, ROW_PIECE))

    def group(g, carry):
        for u in range(PIECE_UNROLL):
            one(g * PIECE_UNROLL + u)
        return carry

    n_groups = lax.div(n, jnp.int32(PIECE_UNROLL))
    lax.fori_loop(0, n_groups, group, 0)
    for u in range(PIECE_UNROLL - 1):
        @pl.when(n_groups * PIECE_UNROLL + u < n)
        def _():
            one(n_groups * PIECE_UNROLL + u)


def _used_slot_groups(n_pieces):
    return lax.div(n_pieces * ROW_PIECE + (SLOT_SUB - 1), jnp.int32(SLOT_SUB))


def _dispatch_kernel(npieces_ref, glob_ref, zflag_ref,
                     xn_ref, srow_ref, buf_ref, rows_ref, zeros_ref, sem, zsem, tsem):
    tt = xn_ref.shape[0] // MOE_STEP_TILES
    sl = rows_ref.shape[1] // MOE_STEP_TILES
    n_blocks = buf_ref.shape[0] // MOE_TM
    i = pl.program_id(0)
    n_steps = pl.num_programs(0)
    cur = lax.rem(i, 2)

    def zero_copy(b, flag):
        return pltpu.make_async_copy(zeros_ref, buf_ref.at[pl.ds(b * MOE_TM, MOE_TM)], zsem if flag == 1 else tsem)

    def for_flagged(flag, fn):
        def body(b, carry):
            @pl.when(zflag_ref[b] == flag)
            def _():
                fn(zero_copy(b, flag))
            return carry
        lax.fori_loop(0, n_blocks, body, 0)

    @pl.when(i == 0)
    def _():
        zeros_ref[...] = jnp.zeros_like(zeros_ref)
        for_flagged(1, lambda cp: cp.start())
        for_flagged(2, lambda cp: cp.start())
        for_flagged(1, lambda cp: cp.wait())

    def piece_copy(buf_slot, sub, local_row, global_row):
        return pltpu.make_async_copy(rows_ref.at[buf_slot, pl.ds(sub * sl + local_row, ROW_PIECE)],
                                     buf_ref.at[pl.ds(global_row, ROW_PIECE)], sem.at[buf_slot])

    def for_step_pieces(step, buf_slot, act):
        for sub in range(MOE_STEP_TILES):
            _for_each_piece(npieces_ref, glob_ref, step * MOE_STEP_TILES + sub,
                            lambda l, g, sub=sub: act(piece_copy(buf_slot, sub, l, g)))

    @pl.when(i >= 2)
    def _():
        for_step_pieces(i - 2, cur, lambda cp: cp.wait())

    for sub in range(MOE_STEP_TILES):
        sr = srow_ref[:, sub * tt:(sub + 1) * tt]
        slot1 = SLOT_RADIX * sr[0:1, :] + sr[1:2, :]
        slot2 = SLOT_RADIX * sr[2:3, :] + sr[3:4, :]
        n_sub = _used_slot_groups(npieces_ref[i * MOE_STEP_TILES + sub])
        for k in range(TOP_K * tt // SLOT_SUB, sl // SLOT_SUB + 1):
            @pl.when(n_sub == k)
            def _(k=k, sub=sub, slot1=slot1, slot2=slot2):
                m = k * SLOT_SUB
                r = lax.broadcasted_iota(jnp.int32, (m, tt), 0).astype(F32)
                sel = jnp.where((r == slot1) | (r == slot2), 1.0, 0.0).astype(BF16)
                rows_ref[cur, sub * sl:sub * sl + m, :] = _dot(sel, xn_ref[sub * tt:(sub + 1) * tt, :]).astype(BF16)
    for_step_pieces(i, cur, lambda cp: cp.start())

    @pl.when(i == n_steps - 1)
    def _():
        @pl.when(i >= 1)
        def _():
            for_step_pieces(i - 1, 1 - cur, lambda cp: cp.wait())
        for_step_pieces(i, cur, lambda cp: cp.wait())
        for_flagged(2, lambda cp: cp.wait())


def _dispatch(xn2, srow, npieces, piece_glob, zflag, n_rows):
    T, D = srow.shape[1], xn2.shape[1]
    tt = MOE_STEP_TILES * MIX_TS
    return pl.pallas_call(
        _dispatch_kernel,
        grid_spec=pltpu.PrefetchScalarGridSpec(
            num_scalar_prefetch=3,
            grid=(T // tt,),
            in_specs=[pl.BlockSpec((tt, D), lambda i, *_: (i, 0)),
                      pl.BlockSpec((8, tt), lambda i, *_: (0, i))],
            out_specs=pl.BlockSpec(memory_space=pl.ANY),
            scratch_shapes=[pltpu.VMEM((2, MOE_STEP_TILES * MOE_SL, D), BF16),
                            pltpu.VMEM((MOE_TM, D), BF16),
                            pltpu.SemaphoreType.DMA((2,)),
                            pltpu.SemaphoreType.DMA(()),
                            pltpu.SemaphoreType.DMA(())]),
        out_shape=jax.ShapeDtypeStruct((n_rows, D), BF16),
        compiler_params=_cparams(1),
        name="dispatch",
    )(npieces, piece_glob, zflag, xn2, srow)


def _experts_kernel(blk_e_ref, nused_ref, nsub_ref, first_ref, next_e_ref, slot_ref,
                    x_ref, wg_hbm, wu_hbm, wd_hbm, y_ref,
                    wg32_ref, wu32_ref, wd32_ref, wgb_ref, wub_ref, wdb_ref, sem):
    i = pl.program_id(0)
    used = i < nused_ref[0]
    n_sub = nsub_ref[i]
    landing = ((wg_hbm, wg32_ref), (wu_hbm, wu32_ref), (wd_hbm, wd32_ref))

    def weight_copies(e, s):
        return [pltpu.make_async_copy(hbm.at[e], vmem.at[s], sem.at[s, n]) for n, (hbm, vmem) in enumerate(landing)]

    @pl.when(used & (i == 0))
    def _():
        for cp in weight_copies(blk_e_ref[0], 0):
            cp.start()

    run_start = used & (first_ref[i] > 0)
    s = slot_ref[i]

    @pl.when(run_start)
    def _():
        for cp in weight_copies(blk_e_ref[i], s):
            cp.wait()

        @pl.when(next_e_ref[i] >= 0)
        def _():
            for cp in weight_copies(next_e_ref[i], 1 - s):
                cp.start()

    def swiglu(m, wg, wu, wd):
        x = x_ref[:m, :]
        hg = _dot(x, wg)
        hu = _dot(x, wu)
        hid = (_silu(hg) * hu).astype(BF16)
        y_ref[:m, :] = _dot(hid, wd).astype(BF16)
        if m < MOE_TM:
            y_ref[m:, :] = jnp.zeros((MOE_TM - m, y_ref.shape[1]), BF16)

    for k in range(1, MOE_TM // EXPERT_SUB + 1):
        @pl.when(run_start & (n_sub == k))
        def _(k=k):
            wg = wg32_ref[s].astype(BF16)
            wu = wu32_ref[s].astype(BF16)
            wd = wd32_ref[s].astype(BF16)
            wgb_ref[...] = wg
            wub_ref[...] = wu
            wdb_ref[...] = wd
            swiglu(k * EXPERT_SUB, wg, wu, wd)

        @pl.when(used & jnp.logical_not(run_start) & (n_sub == k))
        def _(k=k):
            swiglu(k * EXPERT_SUB, wgb_ref[...], wub_ref[...], wdb_ref[...])


def _experts(buf, blk_e, nused, nsub, run_first, next_e, run_slot, w_gate, w_up, w_down):
    R, D = buf.shape
    de = w_gate.shape[2]
    n_blocks = R // MOE_TM
    row_map = lambda i, be, nu, *_: (jnp.minimum(i, nu[0] - 1), 0)
    return pl.pallas_call(
        _experts_kernel,
        grid_spec=pltpu.PrefetchScalarGridSpec(
            num_scalar_prefetch=6,
            grid=(n_blocks,),
            in_specs=[pl.BlockSpec((MOE_TM, D), row_map),
                      pl.BlockSpec(memory_space=pl.ANY),
                      pl.BlockSpec(memory_space=pl.ANY),
                      pl.BlockSpec(memory_space=pl.ANY)],
            out_specs=pl.BlockSpec((MOE_TM, D), row_map),
            scratch_shapes=[pltpu.VMEM((2, D, de), F32), pltpu.VMEM((2, D, de), F32), pltpu.VMEM((2, de, D), F32),
                            pltpu.VMEM((D, de), BF16), pltpu.VMEM((D, de), BF16), pltpu.VMEM((de, D), BF16),
                            pltpu.SemaphoreType.DMA((2, 3))]),
        out_shape=jax.ShapeDtypeStruct((R, D), BF16),
        input_output_aliases={6: 0},
        compiler_params=_cparams(1),
        name="experts",
    )(blk_e, nused, nsub, run_first, next_e, run_slot, buf, w_gate, w_up, w_down)


def _combine_kernel(npieces_ref, glob_ref, x2_ref, rti_ref, rtf_ref, gfin_ref, yb_ref, out_ref,
                    rows_ref, sem):
    tt = x2_ref.shape[0] // MOE_STEP_TILES
    sl = rows_ref.shape[1] // MOE_STEP_TILES
    i = pl.program_id(0)
    n_steps = pl.num_programs(0)
    cur = lax.rem(i, 2)

    def piece_copy(buf_slot, sub, local_row, global_row):
        return pltpu.make_async_copy(yb_ref.at[pl.ds(global_row, ROW_PIECE)],
                                     rows_ref.at[buf_slot, pl.ds(sub * sl + local_row, ROW_PIECE)], sem.at[buf_slot])

    def for_step_pieces(step, buf_slot, act):
        for sub in range(MOE_STEP_TILES):
            _for_each_piece(npieces_ref, glob_ref, step * MOE_STEP_TILES + sub,
                            lambda l, g, sub=sub: act(piece_copy(buf_slot, sub, l, g)))

    @pl.when(i == 0)
    def _():
        rows_ref[...] = jnp.zeros_like(rows_ref)
        for_step_pieces(0, 0, lambda cp: cp.start())

    @pl.when(i + 1 < n_steps)
    def _():
        for_step_pieces(i + 1, 1 - cur, lambda cp: cp.start())

    for_step_pieces(i, cur, lambda cp: cp.wait())

    for sub in range(MOE_STEP_TILES):
        ts_rows = slice(sub * tt, (sub + 1) * tt)
        rti = rti_ref[ts_rows, :]
        rtf = rtf_ref[ts_rows, :]
        slot1 = rti[:, 2:3]
        slot2 = rti[:, 3:4]
        n_sub = _used_slot_groups(npieces_ref[i * MOE_STEP_TILES + sub])
        for k in range(TOP_K * tt // SLOT_SUB, sl // SLOT_SUB + 1):
            @pl.when(n_sub == k)
            def _(k=k, sub=sub, ts_rows=ts_rows, rtf=rtf, slot1=slot1, slot2=slot2):
                m = k * SLOT_SUB
                lane = lax.broadcasted_iota(jnp.int32, (tt, m), 1)
                g = jnp.where(lane == slot1, rtf[:, 0:1], jnp.where(lane == slot2, rtf[:, 1:2], 0.0)).astype(BF16)
                y = x2_ref[ts_rows, :] + _dot(g, rows_ref[cur, sub * sl:sub * sl + m, :])
                ms = jnp.mean(y * y, axis=-1, keepdims=True)
                out_ref[ts_rows, :] = y * lax.rsqrt(ms + EPS) * gfin_ref[...]


def _combine(x2, rti, rtf, g_final, yb, npieces, piece_glob):
    T, D = rti.shape[0], x2.shape[1]
    tt = MOE_STEP_TILES * MIX_TS
    tok = lambda i, *_: (i, 0)
    return pl.pallas_call(
        _combine_kernel,
        grid_spec=pltpu.PrefetchScalarGridSpec(
            num_scalar_prefetch=2,
            grid=(T // tt,),
            in_specs=[pl.BlockSpec((tt, D), tok),
                      pl.BlockSpec((tt, LANES), tok),
                      pl.BlockSpec((tt, LANES), tok),
                      pl.BlockSpec((1, D), lambda i, *_: (0, 0)),
                      pl.BlockSpec(memory_space=pl.ANY)],
            out_specs=pl.BlockSpec((tt, D), tok),
            scratch_shapes=[pltpu.VMEM((2, MOE_STEP_TILES * MOE_SL, D), BF16),
                            pltpu.SemaphoreType.DMA((2,))]),
        out_shape=jax.ShapeDtypeStruct((T, D), F32),
        compiler_params=_cparams(1),
        name="combine",
    )(npieces, piece_glob, x2, rti, rtf, g_final, yb)


def _pad_lanes(a, width=LANES):
    return jnp.pad(a, ((0, 0), (0, width - a.shape[1])))


def kernel(x, g_mix, w_in, b_if, conv_q, conv_k, g_head, w_pool, pool_scale, w_br_a, w_br_b, w_out,
           g_ffn, w_rg, b_rg, w_re, b_re, w_e_gate, w_e_up, w_e_down, g_final):
    B, S, D = x.shape
    T = B * S
    assert g_mix.shape[0] == 1, "single-layer block"
    assert S % MIX_TS == 0 and (T // MIX_TS) % MOE_STEP_TILES == 0
    d_pool = w_br_a.shape[1]
    d_ml = w_br_b.shape[1]
    x2d = x.reshape(T, D)

    n_main = d_pool + 4 * d_ml
    w_l = w_in[0]
    w_main = w_l[:, :n_main].astype(BF16)
    w_if = w_l[:, n_main:n_main + 2 * N_HEADS]
    w_gates = w_l[:, n_main + 2 * N_HEADS:].astype(BF16)
    w_if_c = _pad_lanes(w_if).astype(BF16)
    params = {
        "b_if": _pad_lanes(b_if[0][None, :]),
        "b_if_t": jnp.pad(b_if[0][:, None], ((0, 16 - 2 * N_HEADS), (0, 0))),
        "conv_q": conv_q[0], "conv_k": conv_k[0],
        "g_head": g_head[0][None, :],
        "w_pool": w_pool[0].astype(BF16),
        "pool_scale": pool_scale[0][None, :],
        "w_br_a": w_br_a[0].astype(BF16), "w_br_b": w_br_b[0].astype(BF16),
        "w_out": w_out[0].astype(BF16),
        "g_ffn": g_ffn[0][None, :],
        "w_r": _pad_lanes(jnp.concatenate([w_rg[0], w_re[0]], axis=1)).astype(BF16),
        "b_r": _pad_lanes(jnp.concatenate([b_rg[0], b_re[0]])[None, :]),
    }

    x2, xn2, rti, rtf, tstat, srow = _mixer(x2d, g_mix[0][None, :], w_main, w_gates, w_if_c, params, B, S)

    n_tiles = T // MIX_TS
    n_rows = n_tiles * MOE_SL + N_EXPERTS * MOE_TM
    n_blocks = n_rows // MOE_TM
    i32 = lambda a: a.astype(jnp.int32)
    mm = lambda a, b: jnp.round(jnp.dot(a, b, precision=lax.Precision.HIGHEST, preferred_element_type=F32))
    e_ids = np.arange(N_EXPERTS)
    t_ids = np.arange(n_tiles)
    b_ids = np.arange(n_blocks)
    cum_e = jnp.asarray(e_ids[:, None] <= e_ids[None, :], F32)
    cum_t = jnp.asarray(t_ids[:, None] >= t_ids[None, :], F32)
    cum_b = jnp.asarray(b_ids[:, None] <= b_ids[None, :], F32)
    e_row = jnp.asarray(e_ids[None, :], F32)

    pcs = tstat.reshape(n_tiles, 8, LANES)[:, 0, ROUTER_LANE0:ROUTER_LANE0 + N_EXPERTS].astype(F32)
    piece_end = mm(pcs, cum_e)
    piece_loc = piece_end - pcs
    tile_cum = mm(cum_t, pcs)
    rows_e = tile_cum[-1:, :] * ROW_PIECE
    padded = jnp.floor((rows_e + (MOE_TM - 1)) * (1.0 / MOE_TM)) * MOE_TM
    pend = mm(padded, cum_e)
    poff = pend - padded
    piece_glob = poff * (1.0 / ROW_PIECE) + tile_cum - pcs
    nused_f = pend[0, -1] * (1.0 / MOE_TM)
    blk_start = jnp.asarray(b_ids[:, None] * MOE_TM, F32)
    blk_e_f = jnp.minimum(jnp.sum((pend <= blk_start).astype(F32), axis=1, keepdims=True), N_EXPERTS - 1.0)
    blk_oh = (blk_e_f == e_row).astype(F32)
    later_nonempty = (e_ids[None, :] > e_ids[:, None]) & (rows_e > 0)
    next_of_e = jnp.min(jnp.where(later_nonempty, e_row, float(N_EXPERTS)), axis=1)
    next_of_e = jnp.where(next_of_e == N_EXPERTS, -1.0, next_of_e)
    per_blk = mm(blk_oh, jnp.stack([pend[0], (poff + rows_e)[0], next_of_e], axis=1))
    blk_used = blk_start < pend[0, -1]
    zflag = jnp.where(blk_used, (blk_start + MOE_TM == per_blk[:, 0:1]).astype(F32), 2.0)
    rows_in_blk = jnp.clip(per_blk[:, 1:2] - blk_start, 0.0, float(MOE_TM))
    nsub = jnp.floor((rows_in_blk + (EXPERT_SUB - 1)) * (1.0 / EXPERT_SUB))
    prev_e = jnp.concatenate([jnp.full((1, 1), -1.0, F32), blk_e_f[:-1]], axis=0)
    run_first = (blk_used & (blk_e_f != prev_e)).astype(F32)
    run_idx = mm(run_first.reshape(1, n_blocks), cum_b) - 1.0
    run_slot = run_idx - 2.0 * jnp.floor(run_idx * 0.5)
    p_ids = jnp.asarray(np.arange(PIECES_PER_TILE), F32)
    e_of_p = jnp.minimum(jnp.sum((piece_end[:, None, :] <= p_ids[None, :, None]).astype(F32), axis=2),
                         N_EXPERTS - 1.0)
    shift = jnp.sum(jnp.where(e_of_p[:, :, None] == e_row[None], (piece_glob - piece_loc)[:, None, :], 0.0), axis=2)
    glob_of_p = i32(shift + p_ids[None, :]).reshape(n_tiles * PIECES_PER_TILE)
    npieces = i32(piece_end[:, -1])
    flat_b = lambda a: i32(a).reshape(n_blocks)
    blk_e, nused = flat_b(blk_e_f), i32(nused_f).reshape(1)

    buf = _dispatch(xn2, srow, npieces, glob_of_p, flat_b(zflag), n_rows)
    yb = _experts(buf, blk_e, nused, flat_b(nsub), flat_b(run_first), flat_b(per_blk[:, 2:3]), flat_b(run_slot),
                  w_e_gate[0], w_e_up[0], w_e_down[0])
    out = _combine(x2, rti, rtf, g_final[None, :], yb, npieces, glob_of_p)
    return out.reshape(B, S, D)
```

```python
import functools

import numpy as np
import jax
import jax.numpy as jnp
from jax import lax
from jax.experimental import pallas as pl
from jax.experimental.pallas import tpu as pltpu

F32 = jnp.float32
BF16 = jnp.bfloat16

CHUNK = 64
POOL_WINDOWS = (2, 4, 8, 16)
N_HEADS = 4
CONV_K = 4
N_GROUPS = 4
EXPERTS_PER_GROUP = 8
N_EXPERTS = N_GROUPS * EXPERTS_PER_GROUP
TOP_K = 2
EPS = 1e-6

LANES = 128
HALO = 16
ROUTER_LANE0 = N_GROUPS

INPROJ_TN = 256
MIX_TS = 256
MOE_TM = 1024
ROW_PIECE = 16
MOE_SL = TOP_K * MIX_TS + N_EXPERTS * ROW_PIECE
PIECES_PER_TILE = MOE_SL // ROW_PIECE
MOE_STEP_TILES = 2
SLOT_SUB = 128
PIECE_UNROLL = 4
EXPERT_SUB = 128
SLOT_RADIX = 16
VMEM_LIMIT = 56 * 1024 * 1024


def _cparams(n_axes):
    return pltpu.CompilerParams(dimension_semantics=("arbitrary",) * n_axes,
                                vmem_limit_bytes=VMEM_LIMIT)


def _sigmoid(v):
    return 0.5 * jnp.tanh(0.5 * v) + 0.5


def _silu(v):
    return v * _sigmoid(v)


def _log_sigmoid(v):
    return jnp.minimum(v, 0.0) - jnp.log1p(jnp.exp(-jnp.abs(v)))


def _split3(v):
    hi = v.astype(BF16)
    r1 = v - hi.astype(F32)
    mid = r1.astype(BF16)
    lo = (r1 - mid.astype(F32)).astype(BF16)
    return hi, mid, lo


def _dot(a, b):
    return jnp.dot(a, b, preferred_element_type=F32)


def _dot_nt(a, b):
    return lax.dot_general(a, b, (((1,), (1,)), ((), ())), preferred_element_type=F32)


def _dot_tn(a, b):
    return lax.dot_general(a, b, (((0,), (0,)), ((), ())), preferred_element_type=F32)


def _inproj_steps(x_ref, g_ref, w_refs, z_refs, xn_ref):
    wm_ref, wg_ref, wif_ref = w_refs
    zm_ref, zg_ref, zif_ref, zift_ref = z_refs

    def norm():
        x = x_ref[...]
        ms = jnp.mean(x * x, axis=-1, keepdims=True)
        xn_ref[...] = (x * lax.rsqrt(ms + EPS) * g_ref[...]).astype(BF16)

    def block(w_ref, z_ref, c0):
        def run():
            cols = slice(c0, c0 + INPROJ_TN)
            z_ref[:, cols] = _dot(xn_ref[...], w_ref[:, cols]).astype(BF16)
        return run

    def gates():
        zif = _dot(xn_ref[...], wif_ref[...])
        zif_ref[...] = zif
        zift_ref[...] = zif.T[:zift_ref.shape[0], :]

    steps = [norm, gates]
    steps += [block(wm_ref, zm_ref, c0) for c0 in range(0, zm_ref.shape[1], INPROJ_TN)]
    steps += [block(wg_ref, zg_ref, c0) for c0 in range(0, zg_ref.shape[1], INPROJ_TN)]
    return steps


def _mixer_kernel(x_ref, xnext_ref, gmix_ref, wm_ref, wg_ref, wif_ref,
                  bif_ref, bift_ref, convq_ref, convk_ref, ghead_ref, wpool_ref, pscale_ref,
                  wa_ref, wb_ref, wo_ref, gffn_ref, wr_ref, br_ref,
                  tric_ref, trir_ref, stri_ref, ut_ref, sel_ref,
                  x2_ref, xn2_ref, rti_ref, rtf_ref, tstat_ref, srow_ref,
                  zm_ref, zg_ref, zif_ref, zift_ref, zm_nxt, zg_nxt, zif_nxt, zift_nxt, xn_ref,
                  ext_ref, q_ref, k_ref, h_ref, pool_ref, cst_ref, mst_ref, lg_ref, *, tiles_per_seq):
    ts = x_ref.shape[0]
    d_pool = wa_ref.shape[0]
    d_ml = wb_ref.shape[0]
    dh = d_ml // N_HEADS
    n_chunks = ts // CHUNK
    g_step = pl.program_id(0)
    j = lax.rem(g_step, tiles_per_seq)
    w_in_refs = (wm_ref, wg_ref, wif_ref)
    z_cur = (zm_ref, zg_ref, zif_ref, zift_ref)
    z_nxt = (zm_nxt, zg_nxt, zif_nxt, zift_nxt)
    first = g_step == 0

    @pl.when(first)
    def _():
        for step in _inproj_steps(x_ref, gmix_ref, w_in_refs, z_cur, xn_ref):
            step()
        lg_ref[...] = jnp.zeros_like(lg_ref)

    @pl.when(jnp.logical_not(first))
    def _():
        for dst, src in zip(z_cur, z_nxt):
            dst[...] = src[...]

    @pl.when(j == 0)
    def _():
        ext_ref[:, :HALO, :] = jnp.zeros((ext_ref.shape[0], HALO, LANES), F32)
        cst_ref[...] = jnp.zeros_like(cst_ref)
        mst_ref[...] = jnp.zeros_like(mst_ref)

    pending = _inproj_steps(xnext_ref, gmix_ref, w_in_refs, z_nxt, xn_ref)

    def project_some(n=1):
        for _ in range(min(n, len(pending))):
            pending.pop(0)()

    project_some(2)

    routed = _route_select(lg_ref[...])

    row = lax.broadcasted_iota(jnp.int32, (ts, LANES), 0)
    pos1 = (row + j * ts + 1).astype(F32)

    def history(cg):
        cur = zm_ref[:, cg * LANES:(cg + 1) * LANES].astype(F32)
        ext_ref[cg, HALO:, :] = cur
        return cur, lambda s: ext_ref[cg, HALO - s:HALO - s + ts, :]

    def keep_history(cg, cur):
        ext_ref[cg, :HALO, :] = cur[ts - HALO:, :]

    n_pool_groups = d_pool // LANES
    for g in range(n_pool_groups):
        w = POOL_WINDOWS[g]
        cur, shifted = history(g)
        win = cur
        for s in range(1, w):
            win = win + shifted(s)
        keep_history(g, cur)
        cnt = jnp.minimum(pos1, float(w))
        d = win / cnt - cur
        y = _dot(d.astype(BF16), wpool_ref[g]) * pscale_ref[:, g * LANES:(g + 1) * LANES]
        pool_ref[:, g * LANES:(g + 1) * LANES] = y.astype(BF16)
        project_some()

    n_ml_groups = d_ml // LANES
    for which, (cw_ref, dst_ref, scale) in enumerate(((convq_ref, q_ref, 1.0), (convk_ref, k_ref, dh ** -0.5))):
        for g in range(n_ml_groups):
            cols = slice(g * LANES, (g + 1) * LANES)
            cg = n_pool_groups + which * n_ml_groups + g
            cur, shifted = history(cg)
            acc = cur * cw_ref[CONV_K - 1:CONV_K, cols]
            for sft in range(1, CONV_K):
                acc = acc + shifted(sft) * cw_ref[CONV_K - 1 - sft:CONV_K - sft, cols]
            keep_history(cg, cur)
            dst_ref[:, cols] = (_silu(acc) * scale).astype(BF16)
        project_some()

    _route_slots(routed, stri_ref, ut_ref, sel_ref, rti_ref, rtf_ref, tstat_ref, srow_ref)
    project_some()

    zc = zif_ref[...] + bif_ref[...]
    lf_c = _log_sigmoid(zc)
    bc = sum(_dot(tric_ref[...], p) for p in _split3(lf_c))
    zr = zift_ref[...] + bift_ref[...]
    lf_r = _log_sigmoid(zr)
    br = sum(_dot(p, trir_ref[...]) for p in _split3(lf_r))
    project_some(2)

    ti = lax.broadcasted_iota(jnp.int32, (CHUNK, CHUNK), 0)
    si = lax.broadcasted_iota(jnp.int32, (CHUNK, CHUNK), 1)
    causal = si <= ti
    ones_blk = jnp.ones((CHUNK, dh), BF16)
    v0 = d_pool + 2 * d_ml
    ig_rep = [jnp.broadcast_to(zc[:, h:h + 1], (ts, dh)) for h in range(N_HEADS)]
    bt_rep = [jnp.broadcast_to(bc[:, N_HEADS + h:N_HEADS + h + 1], (ts, dh)) for h in range(N_HEADS)]

    m_state = [mst_ref[h:h + 1, :] for h in range(N_HEADS)]
    c_state = [cst_ref[h] for h in range(N_HEADS)]
    def stage_scores(c):
        rs = slice(c * CHUNK, (c + 1) * CHUNK)
        out = []
        for h in range(N_HEADS):
            hs = slice(h * dh, (h + 1) * dh)
            q = q_ref[rs, hs]
            k = k_ref[rs, hs]
            bt = bt_rep[h][rs, :]
            r_row = zr[h:h + 1, rs] - br[N_HEADS + h:N_HEADS + h + 1, rs]
            dmat = jnp.where(causal, bt[:, :CHUNK] + r_row, -jnp.inf)
            out.append(dict(q=q, k=k, bt=bt, dmat=dmat, qk=_dot_nt(q, k),
                            m_intra=jnp.max(dmat, axis=-1, keepdims=True)))
        return out

    def stage_state(c, st):
        rs = slice(c * CHUNK, (c + 1) * CHUNK)
        for h in range(N_HEADS):
            s = st[h]
            bt, k = s["bt"], s["k"]
            m_prev, c_prev = m_state[h], c_state[h]
            v_aug = jnp.concatenate([zm_ref[rs, v0 + h * dh:v0 + (h + 1) * dh], ones_blk], axis=-1)
            igc = ig_rep[h][rs, :]
            b_last = bt[CHUNK - 1:CHUNK, :]
            a_log = b_last - bt + igc
            a_max = jnp.max(a_log, axis=0, keepdims=True)
            m_new = jnp.maximum(b_last + m_prev, a_max)
            kw = (k.astype(F32) * jnp.exp(a_log - m_new)).astype(BF16)
            decay = jnp.exp(b_last + m_prev - m_new)
            s.update(v_aug=v_aug, m_prev=m_prev, qc=_dot(s["q"], c_prev.astype(BF16)))
            c_state[h] = jnp.concatenate([decay, decay], axis=-1) * c_prev + _dot_tn(kw, v_aug)
            m_state[h] = m_new

    def stage_values(c, st):
        rs = slice(c * CHUNK, (c + 1) * CHUNK)
        for h in range(N_HEADS):
            s = st[h]
            hs = slice(h * dh, (h + 1) * dh)
            inter = s["bt"] + s["m_prev"]
            m_t = jnp.maximum(inter, s["m_intra"])
            w_inter = jnp.exp(inter - m_t)
            smat = s["qk"] * jnp.exp(s["dmat"] - m_t[:, :CHUNK])
            sv = _dot(smat.astype(BF16), s["v_aug"])
            qc = s["qc"]
            nq = w_inter * qc[:, dh:] + sv[:, dh:]
            den = jnp.maximum(jnp.abs(nq), jnp.exp(-m_t))
            h_ref[rs, hs] = (w_inter * qc[:, :dh] + sv[:, :dh]) / den

    staged = stage_scores(0)
    for c in range(n_chunks):
        stage_state(c, staged)
        project_some()
        nxt = stage_scores(c + 1) if c + 1 < n_chunks else None
        project_some()
        stage_values(c, staged)
        staged = nxt
    for h in range(N_HEADS):
        cst_ref[h] = c_state[h]
        mst_ref[h:h + 1, :] = m_state[h]

    o0 = v0 + d_ml
    for h in range(N_HEADS):
        hs = slice(h * dh, (h + 1) * dh)
        hv = h_ref[:, hs]
        mu = jnp.mean(hv, axis=-1, keepdims=True)
        hc = hv - mu
        var = jnp.mean(hc * hc, axis=-1, keepdims=True)
        hn = hc * lax.rsqrt(var + EPS) * ghead_ref[:, hs]
        og = _sigmoid(zm_ref[:, o0 + h * dh:o0 + (h + 1) * dh].astype(F32))
        q_ref[:, hs] = (og * hn).astype(BF16)
    y_a = _dot(pool_ref[...], wa_ref[...])
    y_b = _dot(q_ref[...], wb_ref[...])
    d_model = x_ref.shape[1]
    ga = _sigmoid(zg_ref[:, :d_model].astype(F32))
    gb = _sigmoid(zg_ref[:, d_model:].astype(F32))
    merged = (ga * y_a + gb * y_b).astype(BF16)
    x2 = x_ref[...] + _dot(merged, wo_ref[...])
    x2_ref[...] = x2
    project_some(len(pending))

    ms = jnp.mean(x2 * x2, axis=-1, keepdims=True)
    xn2 = x2 * lax.rsqrt(ms + EPS) * gffn_ref[...]
    xh = xn2.astype(BF16)
    xn2_ref[...] = xh
    lg_ref[...] = _dot(xh, wr_ref[...]) + br_ref[...]
    project_some(len(pending))


def _route_select(lg):
    ts = lg.shape[0]
    lane = lax.broadcasted_iota(jnp.int32, (ts, LANES), 1)
    lanef = lane.astype(F32)
    big = float(4 * LANES)
    gl = jnp.where(lane < N_GROUPS, lg, -jnp.inf)
    gmax = jnp.max(gl, axis=-1, keepdims=True)
    g_sel = jnp.min(jnp.where(gl == gmax, lanef, big), axis=-1, keepdims=True)
    p_g = 1.0 / jnp.sum(jnp.exp(gl - gmax), axis=-1, keepdims=True)
    lo = ROUTER_LANE0 + EXPERTS_PER_GROUP * g_sel
    el = jnp.where((lanef >= lo) & (lanef < lo + EXPERTS_PER_GROUP), lg, -jnp.inf)
    m1 = jnp.max(el, axis=-1, keepdims=True)
    i1 = jnp.min(jnp.where(el == m1, lanef, big), axis=-1, keepdims=True)
    el2 = jnp.where(lanef == i1, -jnp.inf, el)
    m2 = jnp.max(el2, axis=-1, keepdims=True)
    i2 = jnp.min(jnp.where(el2 == m2, lanef, big), axis=-1, keepdims=True)
    e2x = jnp.exp(m2 - m1)
    gate1 = p_g / (1.0 + e2x)
    gate2 = p_g * e2x / (1.0 + e2x)
    return dict(lane=lane, i1=i1, i2=i2, gate1=gate1, gate2=gate2, oh1=lanef == i1, oh2=lanef == i2)


def _route_slots(r, stri_ref, ut_ref, sel_ref, rti_ref, rtf_ref, tstat_ref, srow_ref):
    lane, oh1, oh2, i1, i2 = r["lane"], r["oh1"], r["oh2"], r["i1"], r["i2"]
    ohs = jnp.where(oh1 | oh2, 1.0, 0.0)
    n_loc = jnp.sum(ohs, axis=0, keepdims=True)
    pieces = jnp.floor((n_loc + (ROW_PIECE - 1.0)) * (1.0 / ROW_PIECE))
    piece_off = _dot(jnp.broadcast_to(pieces, (8, LANES)).astype(BF16), ut_ref[...])[0:1, :]
    base = _dot(stri_ref[...], ohs.astype(BF16)) + ROW_PIECE * piece_off
    slot1 = jnp.sum(jnp.where(oh1, base, 0.0), axis=-1, keepdims=True)
    slot2 = jnp.sum(jnp.where(oh2, base, 0.0), axis=-1, keepdims=True)
    tstat_ref[...] = jnp.broadcast_to(pieces, tstat_ref.shape).astype(jnp.int32)

    rti = jnp.where(lane == 0, i1 - ROUTER_LANE0,
                    jnp.where(lane == 1, i2 - ROUTER_LANE0,
                              jnp.where(lane == 2, slot1, jnp.where(lane == 3, slot2, 0.0))))
    rti_ref[...] = rti.astype(jnp.int32)
    rtf_ref[...] = jnp.where(lane == 0, r["gate1"], jnp.where(lane == 1, r["gate2"], 0.0))
    h1 = jnp.floor(slot1 * (1.0 / SLOT_RADIX))
    h2 = jnp.floor(slot2 * (1.0 / SLOT_RADIX))
    parts = jnp.where(lane == 0, h1, jnp.where(lane == 1, slot1 - SLOT_RADIX * h1,
                      jnp.where(lane == 2, h2, jnp.where(lane == 3, slot2 - SLOT_RADIX * h2, 0.0))))
    srow_ref[...] = _dot_nt(sel_ref[...], parts.astype(BF16))


def _mixer(x2d, g_mix, w_main, w_gates, w_if_c, params, batch, seq):
    T, D = x2d.shape
    ts = min(MIX_TS, seq)
    nts = seq // ts
    d_pool = params["w_br_a"].shape[0]
    d_ml = params["w_br_b"].shape[0]
    dh = d_ml // N_HEADS

    idx = np.arange(ts)
    same_chunk = (idx[:, None] // CHUNK) == (idx[None, :] // CHUNK)
    tri_c = jnp.asarray((idx[None, :] <= idx[:, None]) & same_chunk, BF16)
    tri_r = jnp.asarray((idx[:, None] <= idx[None, :]) & same_chunk, BF16)
    stri = jnp.asarray(idx[None, :] < idx[:, None], BF16)
    lane_idx = np.arange(LANES)
    ut = jnp.asarray(lane_idx[:, None] < lane_idx[None, :], BF16)
    sel = jnp.asarray(np.arange(8)[:, None] == lane_idx[None, :], BF16)

    n_tiles = batch * nts
    tok = lambda g: (g, 0)
    tok_in = lambda g: (jnp.minimum(g, n_tiles - 1), 0)
    tok_next = lambda g: (jnp.minimum(g + 1, n_tiles - 1), 0)
    tok_prev = lambda g: (jnp.maximum(g - 1, 0), 0)
    tok_prev_t = lambda g: (0, jnp.maximum(g - 1, 0))
    c2 = lambda g: (0, 0)
    c3 = lambda g: (0, 0, 0)
    full = lambda a: pl.BlockSpec(a.shape, c2 if a.ndim == 2 else c3)
    consts = [params[n] for n in ("b_if", "b_if_t", "conv_q", "conv_k", "g_head", "w_pool", "pool_scale",
                                  "w_br_a", "w_br_b", "w_out", "g_ffn", "w_r", "b_r")]
    consts = [g_mix, w_main, w_gates, w_if_c] + consts + [tri_c, tri_r, stri, ut, sel]
    nm, ng = w_main.shape[1], w_gates.shape[1]
    z_scratch = [pltpu.VMEM((ts, nm), BF16), pltpu.VMEM((ts, ng), BF16),
                 pltpu.VMEM((ts, LANES), F32), pltpu.VMEM((16, ts), F32)]
    return pl.pallas_call(
        functools.partial(_mixer_kernel, tiles_per_seq=nts),
        grid=(n_tiles + 1,),
        in_specs=[pl.BlockSpec((ts, D), tok_in),
                  pl.BlockSpec((ts, D), tok_next)] + [full(a) for a in consts],
        out_specs=[pl.BlockSpec((ts, D), tok),
                   pl.BlockSpec((ts, D), tok),
                   pl.BlockSpec((ts, LANES), tok_prev),
                   pl.BlockSpec((ts, LANES), tok_prev),
                   pl.BlockSpec((8, LANES), tok_prev),
                   pl.BlockSpec((8, ts), tok_prev_t)],
        out_shape=[jax.ShapeDtypeStruct((T + ts, D), F32),
                   jax.ShapeDtypeStruct((T + ts, D), BF16),
                   jax.ShapeDtypeStruct((T, LANES), jnp.int32),
                   jax.ShapeDtypeStruct((T, LANES), F32),
                   jax.ShapeDtypeStruct((n_tiles * 8, LANES), jnp.int32),
                   jax.ShapeDtypeStruct((8, T), F32)],
        scratch_shapes=z_scratch + z_scratch + [
                        pltpu.VMEM((ts, D), BF16),
                        pltpu.VMEM(((d_pool + 2 * d_ml) // LANES, HALO + ts, LANES), F32),
                        pltpu.VMEM((ts, d_ml), BF16),
                        pltpu.VMEM((ts, d_ml), BF16),
                        pltpu.VMEM((ts, d_ml), F32),
                        pltpu.VMEM((ts, d_pool), BF16),
                        pltpu.VMEM((N_HEADS, dh, 2 * dh), F32),
                        pltpu.VMEM((8, LANES), F32),
                        pltpu.VMEM((ts, LANES), F32)],
        compiler_params=_cparams(1),
        name="mixer",
    )(x2d, x2d, *consts)


def _for_each_piece(npieces_ref, glob_ref, tile, fn):
    base = tile * PIECES_PER_TILE
    n = npieces_ref[tile]

    def one(p):
        fn(pl.multiple_of(p * ROW_PIECE, ROW_PIECE), pl.multiple_of(glob_ref[base + p] * ROW_PIECE, ROW_PIECE))

    def group(g, carry):
        for u in range(PIECE_UNROLL):
            one(g * PIECE_UNROLL + u)
        return carry

    n_groups = lax.div(n, jnp.int32(PIECE_UNROLL))
    lax.fori_loop(0, n_groups, group, 0)
    for u in range(PIECE_UNROLL - 1):
        @pl.when(n_groups * PIECE_UNROLL + u < n)
        def _():
            one(n_groups * PIECE_UNROLL + u)


def _used_slot_groups(n_pieces):
    return lax.div(n_pieces * ROW_PIECE + (SLOT_SUB - 1), jnp.int32(SLOT_SUB))


def _dispatch_kernel(npieces_ref, glob_ref, zflag_ref,
                     xn_ref, srow_ref, buf_ref, rows_ref, zeros_ref, sem, zsem, tsem):
    tt = xn_ref.shape[0] // MOE_STEP_TILES
    sl = rows_ref.shape[1] // MOE_STEP_TILES
    n_blocks = buf_ref.shape[0] // MOE_TM
    i = pl.program_id(0)
    n_steps = pl.num_programs(0)
    cur = lax.rem(i, 2)

    def zero_copy(b, flag):
        return pltpu.make_async_copy(zeros_ref, buf_ref.at[pl.ds(b * MOE_TM, MOE_TM)], zsem if flag == 1 else tsem)

    def for_flagged(flag, fn):
        def body(b, carry):
            @pl.when(zflag_ref[b] == flag)
            def _():
                fn(zero_copy(b, flag))
            return carry
        lax.fori_loop(0, n_blocks, body, 0)

    @pl.when(i == 0)
    def _():
        zeros_ref[...] = jnp.zeros_like(zeros_ref)
        for_flagged(1, lambda cp: cp.start())
        for_flagged(2, lambda cp: cp.start())
        for_flagged(1, lambda cp: cp.wait())

    def piece_copy(buf_slot, sub, local_row, global_row):
        return pltpu.make_async_copy(rows_ref.at[buf_slot, pl.ds(sub * sl + local_row, ROW_PIECE)],
                                     buf_ref.at[pl.ds(global_row, ROW_PIECE)], sem.at[buf_slot])

    def for_step_pieces(step, buf_slot, act):
        for sub in range(MOE_STEP_TILES):
            _for_each_piece(npieces_ref, glob_ref, step * MOE_STEP_TILES + sub,
                            lambda l, g, sub=sub: act(piece_copy(buf_slot, sub, l, g)))

    @pl.when(i >= 2)
    def _():
        for_step_pieces(i - 2, cur, lambda cp: cp.wait())

    for sub in range(MOE_STEP_TILES):
        sr = srow_ref[:, sub * tt:(sub + 1) * tt]
        slot1 = SLOT_RADIX * sr[0:1, :] + sr[1:2, :]
        slot2 = SLOT_RADIX * sr[2:3, :] + sr[3:4, :]
        n_sub = _used_slot_groups(npieces_ref[i * MOE_STEP_TILES + sub])
        for k in range(TOP_K * tt // SLOT_SUB, sl // SLOT_SUB + 1):
            @pl.when(n_sub == k)
            def _(k=k, sub=sub, slot1=slot1, slot2=slot2):
                m = k * SLOT_SUB
                r = lax.broadcasted_iota(jnp.int32, (m, tt), 0).astype(F32)
                sel = jnp.where((r == slot1) | (r == slot2), 1.0, 0.0).astype(BF16)
                rows_ref[cur, sub * sl:sub * sl + m, :] = _dot(sel, xn_ref[sub * tt:(sub + 1) * tt, :]).astype(BF16)
    for_step_pieces(i, cur, lambda cp: cp.start())

    @pl.when(i == n_steps - 1)
    def _():
        @pl.when(i >= 1)
        def _():
            for_step_pieces(i - 1, 1 - cur, lambda cp: cp.wait())
        for_step_pieces(i, cur, lambda cp: cp.wait())
        for_flagged(2, lambda cp: cp.wait())


def _dispatch(xn2, srow, npieces, piece_glob, zflag, n_rows):
    T, D = srow.shape[1], xn2.shape[1]
    tt = MOE_STEP_TILES * MIX_TS
    return pl.pallas_call(
        _dispatch_kernel,
        grid_spec=pltpu.PrefetchScalarGridSpec(
            num_scalar_prefetch=3,
            grid=(T // tt,),
            in_specs=[pl.BlockSpec((tt, D), lambda i, *_: (i, 0)),
                      pl.BlockSpec((8, tt), lambda i, *_: (0, i))],
            out_specs=pl.BlockSpec(memory_space=pl.ANY),
            scratch_shapes=[pltpu.VMEM((2, MOE_STEP_TILES * MOE_SL, D), BF16),
                            pltpu.VMEM((MOE_TM, D), BF16),
                            pltpu.SemaphoreType.DMA((2,)),
                            pltpu.SemaphoreType.DMA(()),
                            pltpu.SemaphoreType.DMA(())]),
        out_shape=jax.ShapeDtypeStruct((n_rows, D), BF16),
        compiler_params=_cparams(1),
        name="dispatch",
    )(npieces, piece_glob, zflag, xn2, srow)


def _experts_kernel(blk_e_ref, nused_ref, nsub_ref, first_ref, next_e_ref, slot_ref,
                    x_ref, wg_hbm, wu_hbm, wd_hbm, y_ref,
                    wg32_ref, wu32_ref, wd32_ref, wgb_ref, wub_ref, wdb_ref, sem):
    i = pl.program_id(0)
    used = i < nused_ref[0]
    n_sub = nsub_ref[i]
    landing = ((wg_hbm, wg32_ref), (wu_hbm, wu32_ref), (wd_hbm, wd32_ref))

    def weight_copies(e, s):
        return [pltpu.make_async_copy(hbm.at[e], vmem.at[s], sem.at[s, n]) for n, (hbm, vmem) in enumerate(landing)]

    @pl.when(used & (i == 0))
    def _():
        for cp in weight_copies(blk_e_ref[0], 0):
            cp.start()

    run_start = used & (first_ref[i] > 0)
    s = slot_ref[i]

    @pl.when(run_start)
    def _():
        for cp in weight_copies(blk_e_ref[i], s):
            cp.wait()

        @pl.when(next_e_ref[i] >= 0)
        def _():
            for cp in weight_copies(next_e_ref[i], 1 - s):
                cp.start()

    def swiglu(m, wg, wu, wd):
        x = x_ref[:m, :]
        hg = _dot(x, wg)
        hu = _dot(x, wu)
        hid = (_silu(hg) * hu).astype(BF16)
        y_ref[:m, :] = _dot(hid, wd).astype(BF16)
        if m < MOE_TM:
            y_ref[m:, :] = jnp.zeros((MOE_TM - m, y_ref.shape[1]), BF16)

    for k in range(1, MOE_TM // EXPERT_SUB + 1):
        @pl.when(run_start & (n_sub == k))
        def _(k=k):
            wg = wg32_ref[s].astype(BF16)
            wu = wu32_ref[s].astype(BF16)
            wd = wd32_ref[s].astype(BF16)
            wgb_ref[...] = wg
            wub_ref[...] = wu
            wdb_ref[...] = wd
            swiglu(k * EXPERT_SUB, wg, wu, wd)

        @pl.when(used & jnp.logical_not(run_start) & (n_sub == k))
        def _(k=k):
            swiglu(k * EXPERT_SUB, wgb_ref[...], wub_ref[...], wdb_ref[...])


def _experts(buf, blk_e, nused, nsub, run_first, next_e, run_slot, w_gate, w_up, w_down):
    R, D = buf.shape
    de = w_gate.shape[2]
    n_blocks = R // MOE_TM
    row_map = lambda i, be, nu, *_: (jnp.minimum(i, nu[0] - 1), 0)
    return pl.pallas_call(
        _experts_kernel,
        grid_spec=pltpu.PrefetchScalarGridSpec(
            num_scalar_prefetch=6,
            grid=(n_blocks,),
            in_specs=[pl.BlockSpec((MOE_TM, D), row_map),
                      pl.BlockSpec(memory_space=pl.ANY),
                      pl.BlockSpec(memory_space=pl.ANY),
                      pl.BlockSpec(memory_space=pl.ANY)],
            out_specs=pl.BlockSpec((MOE_TM, D), row_map),
            scratch_shapes=[pltpu.VMEM((2, D, de), F32), pltpu.VMEM((2, D, de), F32), pltpu.VMEM((2, de, D), F32),
                            pltpu.VMEM((D, de), BF16), pltpu.VMEM((D, de), BF16), pltpu.VMEM((de, D), BF16),
                            pltpu.SemaphoreType.DMA((2, 3))]),
        out_shape=jax.ShapeDtypeStruct((R, D), BF16),
        input_output_aliases={6: 0},
        compiler_params=_cparams(1),
        name="experts",
    )(blk_e, nused, nsub, run_first, next_e, run_slot, buf, w_gate, w_up, w_down)


def _combine_kernel(npieces_ref, glob_ref, x2_ref, rti_ref, rtf_ref, gfin_ref, yb_ref, out_ref,
                    rows_ref, sem):
    tt = x2_ref.shape[0] // MOE_STEP_TILES
    sl = rows_ref.shape[1] // MOE_STEP_TILES
    i = pl.program_id(0)
    n_steps = pl.num_programs(0)
    cur = lax.rem(i, 2)

    def piece_copy(buf_slot, sub, local_row, global_row):
        return pltpu.make_async_copy(yb_ref.at[pl.ds(global_row, ROW_PIECE)],
                                     rows_ref.at[buf_slot, pl.ds(sub * sl + local_row, ROW_PIECE)], sem.at[buf_slot])

    def for_step_pieces(step, buf_slot, act):
        for sub in range(MOE_STEP_TILES):
            _for_each_piece(npieces_ref, glob_ref, step * MOE_STEP_TILES + sub,
                            lambda l, g, sub=sub: act(piece_copy(buf_slot, sub, l, g)))

    @pl.when(i == 0)
    def _():
        rows_ref[...] = jnp.zeros_like(rows_ref)
        for_step_pieces(0, 0, lambda cp: cp.start())

    @pl.when(i + 1 < n_steps)
    def _():
        for_step_pieces(i + 1, 1 - cur, lambda cp: cp.start())

    for_step_pieces(i, cur, lambda cp: cp.wait())

    for sub in range(MOE_STEP_TILES):
        ts_rows = slice(sub * tt, (sub + 1) * tt)
        rti = rti_ref[ts_rows, :]
        rtf = rtf_ref[ts_rows, :]
        slot1 = rti[:, 2:3]
        slot2 = rti[:, 3:4]
        n_sub = _used_slot_groups(npieces_ref[i * MOE_STEP_TILES + sub])
        for k in range(TOP_K * tt // SLOT_SUB, sl // SLOT_SUB + 1):
            @pl.when(n_sub == k)
            def _(k=k, sub=sub, ts_rows=ts_rows, rtf=rtf, slot1=slot1, slot2=slot2):
                m = k * SLOT_SUB
                lane = lax.broadcasted_iota(jnp.int32, (tt, m), 1)
                g = jnp.where(lane == slot1, rtf[:, 0:1], jnp.where(lane == slot2, rtf[:, 1:2], 0.0)).astype(BF16)
                y = x2_ref[ts_rows, :] + _dot(g, rows_ref[cur, sub * sl:sub * sl + m, :])
                ms = jnp.mean(y * y, axis=-1, keepdims=True)
                out_ref[ts_rows, :] = y * lax.rsqrt(ms + EPS) * gfin_ref[...]


def _combine(x2, rti, rtf, g_final, yb, npieces, piece_glob):
    T, D = rti.shape[0], x2.shape[1]
    tt = MOE_STEP_TILES * MIX_TS
    tok = lambda i, *_: (i, 0)
    return pl.pallas_call(
        _combine_kernel,
        grid_spec=pltpu.PrefetchScalarGridSpec(
            num_scalar_prefetch=2,
            grid=(T // tt,),
            in_specs=[pl.BlockSpec((tt, D), tok),
                      pl.BlockSpec((tt, LANES), tok),
                      pl.BlockSpec((tt, LANES), tok),
                      pl.BlockSpec((1, D), lambda i, *_: (0, 0)),
                      pl.BlockSpec(memory_space=pl.ANY)],
            out_specs=pl.BlockSpec((tt, D), tok),
            scratch_shapes=[pltpu.VMEM((2, MOE_STEP_TILES * MOE_SL, D), BF16),
                            pltpu.SemaphoreType.DMA((2,))]),
        out_shape=jax.ShapeDtypeStruct((T, D), F32),
        compiler_params=_cparams(1),
        name="combine",
    )(npieces, piece_glob, x2, rti, rtf, g_final, yb)


def _pad_lanes(a, width=LANES):
    return jnp.pad(a, ((0, 0), (0, width - a.shape[1])))


def kernel(x, g_mix, w_in, b_if, conv_q, conv_k, g_head, w_pool, pool_scale, w_br_a, w_br_b, w_out,
           g_ffn, w_rg, b_rg, w_re, b_re, w_e_gate, w_e_up, w_e_down, g_final):
    B, S, D = x.shape
    T = B * S
    assert g_mix.shape[0] == 1, "single-layer block"
    assert S % MIX_TS == 0 and (T // MIX_TS) % MOE_STEP_TILES == 0
    d_pool = w_br_a.shape[1]
    d_ml = w_br_b.shape[1]
    x2d = x.reshape(T, D)

    n_main = d_pool + 4 * d_ml
    w_l = w_in[0]
    w_main = w_l[:, :n_main].astype(BF16)
    w_if = w_l[:, n_main:n_main + 2 * N_HEADS]
    w_gates = w_l[:, n_main + 2 * N_HEADS:].astype(BF16)
    w_if_c = _pad_lanes(w_if).astype(BF16)
    params = {
        "b_if": _pad_lanes(b_if[0][None, :]),
        "b_if_t": jnp.pad(b_if[0][:, None], ((0, 16 - 2 * N_HEADS), (0, 0))),
        "conv_q": conv_q[0], "conv_k": conv_k[0],
        "g_head": g_head[0][None, :],
        "w_pool": w_pool[0].astype(BF16),
        "pool_scale": pool_scale[0][None, :],
        "w_br_a": w_br_a[0].astype(BF16), "w_br_b": w_br_b[0].astype(BF16),
        "w_out": w_out[0].astype(BF16),
        "g_ffn": g_ffn[0][None, :],
        "w_r": _pad_lanes(jnp.concatenate([w_rg[0], w_re[0]], axis=1)).astype(BF16),
        "b_r": _pad_lanes(jnp.concatenate([b_rg[0], b_re[0]])[None, :]),
    }

    x2, xn2, rti, rtf, tstat, srow = _mixer(x2d, g_mix[0][None, :], w_main, w_gates, w_if_c, params, B, S)

    n_tiles = T // MIX_TS
    n_rows = n_tiles * MOE_SL + N_EXPERTS * MOE_TM
    n_blocks = n_rows // MOE_TM
    i32 = lambda a: a.astype(jnp.int32)
    mm = lambda a, b: jnp.round(jnp.dot(a, b, precision=lax.Precision.HIGHEST, preferred_element_type=F32))
    e_ids = np.arange(N_EXPERTS)
    t_ids = np.arange(n_tiles)
    b_ids = np.arange(n_blocks)
    cum_e = jnp.asarray(e_ids[:, None] <= e_ids[None, :], F32)
    cum_t = jnp.asarray(t_ids[:, None] >= t_ids[None, :], F32)
    cum_b = jnp.asarray(b_ids[:, None] <= b_ids[None, :], F32)
    e_row = jnp.asarray(e_ids[None, :], F32)

    pcs = tstat.reshape(n_tiles, 8, LANES)[:, 0, ROUTER_LANE0:ROUTER_LANE0 + N_EXPERTS].astype(F32)
    piece_end = mm(pcs, cum_e)
    piece_loc = piece_end - pcs
    tile_cum = mm(cum_t, pcs)
    rows_e = tile_cum[-1:, :] * ROW_PIECE
    padded = jnp.floor((rows_e + (MOE_TM - 1)) * (1.0 / MOE_TM)) * MOE_TM
    pend = mm(padded, cum_e)
    poff = pend - padded
    piece_glob = poff * (1.0 / ROW_PIECE) + tile_cum - pcs
    nused_f = pend[0, -1] * (1.0 / MOE_TM)
    blk_start = jnp.asarray(b_ids[:, None] * MOE_TM, F32)
    blk_e_f = jnp.minimum(jnp.sum((pend <= blk_start).astype(F32), axis=1, keepdims=True), N_EXPERTS - 1.0)
    blk_oh = (blk_e_f == e_row).astype(F32)
    later_nonempty = (e_ids[None, :] > e_ids[:, None]) & (rows_e > 0)
    next_of_e = jnp.min(jnp.where(later_nonempty, e_row, float(N_EXPERTS)), axis=1)
    next_of_e = jnp.where(next_of_e == N_EXPERTS, -1.0, next_of_e)
    per_blk = mm(blk_oh, jnp.stack([pend[0], (poff + rows_e)[0], next_of_e], axis=1))
    blk_used = blk_start < pend[0, -1]
    zflag = jnp.where(blk_used, (blk_start + MOE_TM == per_blk[:, 0:1]).astype(F32), 2.0)
    rows_in_blk = jnp.clip(per_blk[:, 1:2] - blk_start, 0.0, float(MOE_TM))
    nsub = jnp.floor((rows_in_blk + (EXPERT_SUB - 1)) * (1.0 / EXPERT_SUB))
    prev_e = jnp.concatenate([jnp.full((1, 1), -1.0, F32), blk_e_f[:-1]], axis=0)
    run_first = (blk_used & (blk_e_f != prev_e)).astype(F32)
    run_idx = mm(run_first.reshape(1, n_blocks), cum_b) - 1.0
    run_slot = run_idx - 2.0 * jnp.floor(run_idx * 0.5)
    p_ids = jnp.asarray(np.arange(PIECES_PER_TILE), F32)
    e_of_p = jnp.minimum(jnp.sum((piece_end[:, None, :] <= p_ids[None, :, None]).astype(F32), axis=2),
                         N_EXPERTS - 1.0)
    shift = jnp.sum(jnp.where(e_of_p[:, :, None] == e_row[None], (piece_glob - piece_loc)[:, None, :], 0.0), axis=2)
    glob_of_p = i32(shift + p_ids[None, :]).reshape(n_tiles * PIECES_PER_TILE)
    npieces = i32(piece_end[:, -1])
    flat_b = lambda a: i32(a).reshape(n_blocks)
    blk_e, nused = flat_b(blk_e_f), i32(nused_f).reshape(1)

    buf = _dispatch(xn2, srow, npieces, glob_of_p, flat_b(zflag), n_rows)
    yb = _experts(buf, blk_e, nused, flat_b(nsub), flat_b(run_first), flat_b(per_blk[:, 2:3]), flat_b(run_slot),
                  w_e_gate[0], w_e_up[0], w_e_down[0])
    out = _combine(x2, rti, rtf, g_final[None, :], yb, npieces, glob_of_p)
    return out.reshape(B, S, D)
```

```python
import functools

import numpy as np
import jax
import jax.numpy as jnp
from jax import lax
from jax.experimental import pallas as pl
from jax.experimental.pallas import tpu as pltpu

F32 = jnp.float32
BF16 = jnp.bfloat16

CHUNK = 64
POOL_WINDOWS = (2, 4, 8, 16)
N_HEADS = 4
CONV_K = 4
N_GROUPS = 4
EXPERTS_PER_GROUP = 8
N_EXPERTS = N_GROUPS * EXPERTS_PER_GROUP
TOP_K = 2
EPS = 1e-6

LANES = 128
HALO = 16
ROUTER_LANE0 = N_GROUPS

INPROJ_TN = 256
MIX_TS = 256
MOE_TM = 512
ROW_PIECE = 16
MOE_SL = TOP_K * MIX_TS + N_EXPERTS * ROW_PIECE
PIECES_PER_TILE = MOE_SL // ROW_PIECE
MOE_STEP_TILES = 2
SLOT_SUB = 128
PIECE_UNROLL = 4
EXPERT_SUB = 128
SLOT_RADIX = 16
VMEM_LIMIT = 56 * 1024 * 1024


def _cparams(n_axes):
    return pltpu.CompilerParams(dimension_semantics=("arbitrary",) * n_axes,
                                vmem_limit_bytes=VMEM_LIMIT)


def _sigmoid(v):
    return 0.5 * jnp.tanh(0.5 * v) + 0.5


def _silu(v):
    return v * _sigmoid(v)


def _log_sigmoid(v):
    return jnp.minimum(v, 0.0) - jnp.log1p(jnp.exp(-jnp.abs(v)))


def _split3(v):
    hi = v.astype(BF16)
    r1 = v - hi.astype(F32)
    mid = r1.astype(BF16)
    lo = (r1 - mid.astype(F32)).astype(BF16)
    return hi, mid, lo


def _dot(a, b):
    return jnp.dot(a, b, preferred_element_type=F32)


def _dot_nt(a, b):
    return lax.dot_general(a, b, (((1,), (1,)), ((), ())), preferred_element_type=F32)


def _dot_tn(a, b):
    return lax.dot_general(a, b, (((0,), (0,)), ((), ())), preferred_element_type=F32)


def _inproj_steps(x_ref, g_ref, w_refs, z_refs, xn_ref):
    w_ref_all, wg_ref = w_refs
    zm_ref, zg_ref, zif_ref, zift_ref = z_refs
    n_main = zm_ref.shape[1]

    def norm():
        x = x_ref[...]
        ms = jnp.mean(x * x, axis=-1, keepdims=True)
        xn_ref[...] = (x * lax.rsqrt(ms + EPS) * g_ref[...]).astype(BF16)

    def block(w_ref, z_ref, c0):
        def run():
            cols = slice(c0, c0 + INPROJ_TN)
            z_ref[:, cols] = _dot(xn_ref[...], w_ref[:, cols]).astype(BF16)
        return run

    def gates():
        zif = _dot(xn_ref[...], w_ref_all[:, n_main:n_main + LANES])
        zif_ref[...] = zif
        zift_ref[...] = zif.T[:zift_ref.shape[0], :]

    steps = [norm, gates]
    steps += [block(w_ref_all, zm_ref, c0) for c0 in range(0, n_main, INPROJ_TN)]
    steps += [block(wg_ref, zg_ref, c0) for c0 in range(0, zg_ref.shape[1], INPROJ_TN)]
    return steps


def _mixer_kernel(x_ref, xnext_ref, gmix_ref, win_ref,
                  bif_ref, bift_ref, convq_ref, convk_ref, ghead_ref, wpool_ref, pscale_ref,
                  wa_ref, wb_ref, wo_ref, gffn_ref, wr_ref, br_ref,
                  tric_ref, trir_ref, stri_ref, ut_ref, sel_ref,
                  x2_ref, xn2_ref, rti_ref, rtf_ref, tstat_ref, srow_ref,
                  zm_ref, zg_ref, zif_ref, zift_ref, zm_nxt, zg_nxt, zif_nxt, zift_nxt, xn_ref,
                  ext_ref, q_ref, k_ref, h_ref, pool_ref, cst_ref, mst_ref, lg_ref, wg_ref, *, tiles_per_seq):
    ts = x_ref.shape[0]
    d_pool = wa_ref.shape[0]
    d_ml = wb_ref.shape[0]
    dh = d_ml // N_HEADS
    n_chunks = ts // CHUNK
    g_step = pl.program_id(0)
    j = lax.rem(g_step, tiles_per_seq)
    w_in_refs = (win_ref, wg_ref)
    z_cur = (zm_ref, zg_ref, zif_ref, zift_ref)
    z_nxt = (zm_nxt, zg_nxt, zif_nxt, zift_nxt)
    first = g_step == 0

    @pl.when(first)
    def _():
        g0 = zm_ref.shape[1] + 2 * N_HEADS
        wg_ref[...] = win_ref[:, g0:g0 + wg_ref.shape[1]]
        for step in _inproj_steps(x_ref, gmix_ref, w_in_refs, z_cur, xn_ref):
            step()
        lg_ref[...] = jnp.zeros_like(lg_ref)

    @pl.when(jnp.logical_not(first))
    def _():
        for dst, src in zip(z_cur, z_nxt):
            dst[...] = src[...]

    @pl.when(j == 0)
    def _():
        ext_ref[:, :HALO, :] = jnp.zeros((ext_ref.shape[0], HALO, LANES), F32)
        cst_ref[...] = jnp.zeros_like(cst_ref)
        mst_ref[...] = jnp.zeros_like(mst_ref)

    pending = _inproj_steps(xnext_ref, gmix_ref, w_in_refs, z_nxt, xn_ref)

    def project_some(n=1):
        for _ in range(min(n, len(pending))):
            pending.pop(0)()

    project_some(2)

    routed = _route_select(lg_ref[...])

    row = lax.broadcasted_iota(jnp.int32, (ts, LANES), 0)
    pos1 = (row + j * ts + 1).astype(F32)

    def history(cg):
        cur = zm_ref[:, cg * LANES:(cg + 1) * LANES].astype(F32)
        ext_ref[cg, HALO:, :] = cur
        return cur, lambda s: ext_ref[cg, HALO - s:HALO - s + ts, :]

    def keep_history(cg, cur):
        ext_ref[cg, :HALO, :] = cur[ts - HALO:, :]

    n_pool_groups = d_pool // LANES
    for g in range(n_pool_groups):
        w = POOL_WINDOWS[g]
        cur, shifted = history(g)
        win = cur
        for s in range(1, w):
            win = win + shifted(s)
        keep_history(g, cur)
        cnt = jnp.minimum(pos1, float(w))
        d = win / cnt - cur
        y = _dot(d.astype(BF16), wpool_ref[g]) * pscale_ref[:, g * LANES:(g + 1) * LANES]
        pool_ref[:, g * LANES:(g + 1) * LANES] = y.astype(BF16)
        project_some()

    n_ml_groups = d_ml // LANES
    for which, (cw_ref, dst_ref, scale) in enumerate(((convq_ref, q_ref, 1.0), (convk_ref, k_ref, dh ** -0.5))):
        for g in range(n_ml_groups):
            cols = slice(g * LANES, (g + 1) * LANES)
            cg = n_pool_groups + which * n_ml_groups + g
            cur, shifted = history(cg)
            acc = cur * cw_ref[CONV_K - 1:CONV_K, cols]
            for sft in range(1, CONV_K):
                acc = acc + shifted(sft) * cw_ref[CONV_K - 1 - sft:CONV_K - sft, cols]
            keep_history(cg, cur)
            dst_ref[:, cols] = (_silu(acc) * scale).astype(BF16)
        project_some()

    _route_slots(routed, stri_ref, ut_ref, sel_ref, rti_ref, rtf_ref, tstat_ref, srow_ref)
    project_some()

    zc = zif_ref[...] + bif_ref[...]
    lf_c = _log_sigmoid(zc)
    bc = sum(_dot(tric_ref[...], p) for p in _split3(lf_c))
    zr = zift_ref[...] + bift_ref[...]
    lf_r = _log_sigmoid(zr)
    br = sum(_dot(p, trir_ref[...]) for p in _split3(lf_r))
    project_some(2)

    ti = lax.broadcasted_iota(jnp.int32, (CHUNK, CHUNK), 0)
    si = lax.broadcasted_iota(jnp.int32, (CHUNK, CHUNK), 1)
    causal = si <= ti
    ones_blk = jnp.ones((CHUNK, dh), BF16)
    v0 = d_pool + 2 * d_ml
    ig_rep = [jnp.broadcast_to(zc[:, h:h + 1], (ts, dh)) for h in range(N_HEADS)]
    bt_rep = [jnp.broadcast_to(bc[:, N_HEADS + h:N_HEADS + h + 1], (ts, dh)) for h in range(N_HEADS)]

    m_state = [mst_ref[h:h + 1, :] for h in range(N_HEADS)]
    c_state = [cst_ref[h] for h in range(N_HEADS)]
    def stage_scores(c):
        rs = slice(c * CHUNK, (c + 1) * CHUNK)
        out = []
        for h in range(N_HEADS):
            hs = slice(h * dh, (h + 1) * dh)
            q = q_ref[rs, hs]
            k = k_ref[rs, hs]
            bt = bt_rep[h][rs, :]
            r_row = zr[h:h + 1, rs] - br[N_HEADS + h:N_HEADS + h + 1, rs]
            dmat = jnp.where(causal, bt[:, :CHUNK] + r_row, -jnp.inf)
            out.append(dict(q=q, k=k, bt=bt, dmat=dmat, qk=_dot_nt(q, k),
                            m_intra=jnp.max(dmat, axis=-1, keepdims=True)))
        return out

    def stage_state(c, st):
        rs = slice(c * CHUNK, (c + 1) * CHUNK)
        for h in range(N_HEADS):
            s = st[h]
            bt, k = s["bt"], s["k"]
            m_prev, c_prev = m_state[h], c_state[h]
            v_aug = jnp.concatenate([zm_ref[rs, v0 + h * dh:v0 + (h + 1) * dh], ones_blk], axis=-1)
            igc = ig_rep[h][rs, :]
            b_last = bt[CHUNK - 1:CHUNK, :]
            a_log = b_last - bt + igc
            a_max = jnp.max(a_log, axis=0, keepdims=True)
            m_new = jnp.maximum(b_last + m_prev, a_max)
            kw = (k.astype(F32) * jnp.exp(a_log - m_new)).astype(BF16)
            decay = jnp.exp(b_last + m_prev - m_new)
            s.update(v_aug=v_aug, m_prev=m_prev, qc=_dot(s["q"], c_prev.astype(BF16)))
            c_state[h] = jnp.concatenate([decay, decay], axis=-1) * c_prev + _dot_tn(kw, v_aug)
            m_state[h] = m_new

    def stage_values(c, st):
        rs = slice(c * CHUNK, (c + 1) * CHUNK)
        for h in range(N_HEADS):
            s = st[h]
            hs = slice(h * dh, (h + 1) * dh)
            inter = s["bt"] + s["m_prev"]
            m_t = jnp.maximum(inter, s["m_intra"])
            w_inter = jnp.exp(inter - m_t)
            smat = s["qk"] * jnp.exp(s["dmat"] - m_t[:, :CHUNK])
            sv = _dot(smat.astype(BF16), s["v_aug"])
            qc = s["qc"]
            nq = w_inter * qc[:, dh:] + sv[:, dh:]
            den = jnp.maximum(jnp.abs(nq), jnp.exp(-m_t))
            h_ref[rs, hs] = (w_inter * qc[:, :dh] + sv[:, :dh]) / den

    staged = stage_scores(0)
    for c in range(n_chunks):
        stage_state(c, staged)
        project_some()
        nxt = stage_scores(c + 1) if c + 1 < n_chunks else None
        project_some()
        stage_values(c, staged)
        staged = nxt
    for h in range(N_HEADS):
        cst_ref[h] = c_state[h]
        mst_ref[h:h + 1, :] = m_state[h]

    o0 = v0 + d_ml
    for h in range(N_HEADS):
        hs = slice(h * dh, (h + 1) * dh)
        hv = h_ref[:, hs]
        mu = jnp.mean(hv, axis=-1, keepdims=True)
        hc = hv - mu
        var = jnp.mean(hc * hc, axis=-1, keepdims=True)
        hn = hc * lax.rsqrt(var + EPS) * ghead_ref[:, hs]
        og = _sigmoid(zm_ref[:, o0 + h * dh:o0 + (h + 1) * dh].astype(F32))
        q_ref[:, hs] = (og * hn).astype(BF16)
    y_a = _dot(pool_ref[...], wa_ref[...])
    y_b = _dot(q_ref[...], wb_ref[...])
    d_model = x_ref.shape[1]
    ga = _sigmoid(zg_ref[:, :d_model].astype(F32))
    gb = _sigmoid(zg_ref[:, d_model:].astype(F32))
    merged = (ga * y_a + gb * y_b).astype(BF16)
    x2 = x_ref[...] + _dot(merged, wo_ref[...])
    x2_ref[...] = x2
    project_some(len(pending))

    ms = jnp.mean(x2 * x2, axis=-1, keepdims=True)
    xn2 = x2 * lax.rsqrt(ms + EPS) * gffn_ref[...]
    xh = xn2.astype(BF16)
    xn2_ref[...] = xh
    lg_ref[...] = _dot(xh, wr_ref[...]) + br_ref[...]
    project_some(len(pending))


def _route_select(lg):
    ts = lg.shape[0]
    lane = lax.broadcasted_iota(jnp.int32, (ts, LANES), 1)
    lanef = lane.astype(F32)
    big = float(4 * LANES)
    gl = jnp.where(lane < N_GROUPS, lg, -jnp.inf)
    gmax = jnp.max(gl, axis=-1, keepdims=True)
    g_sel = jnp.min(jnp.where(gl == gmax, lanef, big), axis=-1, keepdims=True)
    p_g = 1.0 / jnp.sum(jnp.exp(gl - gmax), axis=-1, keepdims=True)
    lo = ROUTER_LANE0 + EXPERTS_PER_GROUP * g_sel
    el = jnp.where((lanef >= lo) & (lanef < lo + EXPERTS_PER_GROUP), lg, -jnp.inf)
    m1 = jnp.max(el, axis=-1, keepdims=True)
    i1 = jnp.min(jnp.where(el == m1, lanef, big), axis=-1, keepdims=True)
    el2 = jnp.where(lanef == i1, -jnp.inf, el)
    m2 = jnp.max(el2, axis=-1, keepdims=True)
    i2 = jnp.min(jnp.where(el2 == m2, lanef, big), axis=-1, keepdims=True)
    e2x = jnp.exp(m2 - m1)
    gate1 = p_g / (1.0 + e2x)
    gate2 = p_g * e2x / (1.0 + e2x)
    return dict(lane=lane, i1=i1, i2=i2, gate1=gate1, gate2=gate2, oh1=lanef == i1, oh2=lanef == i2)


def _route_slots(r, stri_ref, ut_ref, sel_ref, rti_ref, rtf_ref, tstat_ref, srow_ref):
    lane, oh1, oh2, i1, i2 = r["lane"], r["oh1"], r["oh2"], r["i1"], r["i2"]
    ohs = jnp.where(oh1 | oh2, 1.0, 0.0)
    n_loc = jnp.sum(ohs, axis=0, keepdims=True)
    pieces = jnp.floor((n_loc + (ROW_PIECE - 1.0)) * (1.0 / ROW_PIECE))
    piece_off = _dot(jnp.broadcast_to(pieces, (8, LANES)).astype(BF16), ut_ref[...])[0:1, :]
    base = _dot(stri_ref[...], ohs.astype(BF16)) + ROW_PIECE * piece_off
    slot1 = jnp.sum(jnp.where(oh1, base, 0.0), axis=-1, keepdims=True)
    slot2 = jnp.sum(jnp.where(oh2, base, 0.0), axis=-1, keepdims=True)
    tstat_ref[...] = jnp.broadcast_to(pieces, tstat_ref.shape).astype(jnp.int32)

    rti = jnp.where(lane == 0, i1 - ROUTER_LANE0,
                    jnp.where(lane == 1, i2 - ROUTER_LANE0,
                              jnp.where(lane == 2, slot1, jnp.where(lane == 3, slot2, 0.0))))
    rti_ref[...] = rti.astype(jnp.int32)
    rtf_ref[...] = jnp.where(lane == 0, r["gate1"], jnp.where(lane == 1, r["gate2"], 0.0))
    h1 = jnp.floor(slot1 * (1.0 / SLOT_RADIX))
    h2 = jnp.floor(slot2 * (1.0 / SLOT_RADIX))
    parts = jnp.where(lane == 0, h1, jnp.where(lane == 1, slot1 - SLOT_RADIX * h1,
                      jnp.where(lane == 2, h2, jnp.where(lane == 3, slot2 - SLOT_RADIX * h2, 0.0))))
    srow_ref[...] = _dot_nt(sel_ref[...], parts.astype(BF16))


def _mixer(x2d, g_mix, w_all, params, batch, seq):
    T, D = x2d.shape
    ts = min(MIX_TS, seq)
    nts = seq // ts
    d_pool = params["w_br_a"].shape[0]
    d_ml = params["w_br_b"].shape[0]
    dh = d_ml // N_HEADS

    idx = np.arange(ts)
    same_chunk = (idx[:, None] // CHUNK) == (idx[None, :] // CHUNK)
    tri_c = jnp.asarray((idx[None, :] <= idx[:, None]) & same_chunk, BF16)
    tri_r = jnp.asarray((idx[:, None] <= idx[None, :]) & same_chunk, BF16)
    stri = jnp.asarray(idx[None, :] < idx[:, None], BF16)
    lane_idx = np.arange(LANES)
    ut = jnp.asarray(lane_idx[:, None] < lane_idx[None, :], BF16)
    sel = jnp.asarray(np.arange(8)[:, None] == lane_idx[None, :], BF16)

    n_tiles = batch * nts
    tok = lambda g: (g, 0)
    tok_in = lambda g: (jnp.minimum(g, n_tiles - 1), 0)
    tok_next = lambda g: (jnp.minimum(g + 1, n_tiles - 1), 0)
    tok_prev = lambda g: (jnp.maximum(g - 1, 0), 0)
    tok_prev_t = lambda g: (0, jnp.maximum(g - 1, 0))
    c2 = lambda g: (0, 0)
    c3 = lambda g: (0, 0, 0)
    full = lambda a: pl.BlockSpec(a.shape, c2 if a.ndim == 2 else c3)
    consts = [params[n] for n in ("b_if", "b_if_t", "conv_q", "conv_k", "g_head", "w_pool", "pool_scale",
                                  "w_br_a", "w_br_b", "w_out", "g_ffn", "w_r", "b_r")]
    consts = [g_mix, w_all] + consts + [tri_c, tri_r, stri, ut, sel]
    nm, ng = d_pool + 4 * d_ml, 2 * D
    z_scratch = [pltpu.VMEM((ts, nm), BF16), pltpu.VMEM((ts, ng), BF16),
                 pltpu.VMEM((ts, LANES), F32), pltpu.VMEM((16, ts), F32)]
    return pl.pallas_call(
        functools.partial(_mixer_kernel, tiles_per_seq=nts),
        grid=(n_tiles + 1,),
        in_specs=[pl.BlockSpec((ts, D), tok_in),
                  pl.BlockSpec((ts, D), tok_next)] + [full(a) for a in consts],
        out_specs=[pl.BlockSpec((ts, D), tok),
                   pl.BlockSpec((ts, D), tok),
                   pl.BlockSpec((ts, LANES), tok_prev),
                   pl.BlockSpec((ts, LANES), tok_prev),
                   pl.BlockSpec((8, LANES), tok_prev),
                   pl.BlockSpec((8, ts), tok_prev_t)],
        out_shape=[jax.ShapeDtypeStruct((T + ts, D), F32),
                   jax.ShapeDtypeStruct((T + ts, D), BF16),
                   jax.ShapeDtypeStruct((T, LANES), jnp.int32),
                   jax.ShapeDtypeStruct((T, LANES), F32),
                   jax.ShapeDtypeStruct((n_tiles * 8, LANES), jnp.int32),
                   jax.ShapeDtypeStruct((8, T), F32)],
        scratch_shapes=z_scratch + z_scratch + [
                        pltpu.VMEM((ts, D), BF16),
                        pltpu.VMEM(((d_pool + 2 * d_ml) // LANES, HALO + ts, LANES), F32),
                        pltpu.VMEM((ts, d_ml), BF16),
                        pltpu.VMEM((ts, d_ml), BF16),
                        pltpu.VMEM((ts, d_ml), F32),
                        pltpu.VMEM((ts, d_pool), BF16),
                        pltpu.VMEM((N_HEADS, dh, 2 * dh), F32),
                        pltpu.VMEM((8, LANES), F32),
                        pltpu.VMEM((ts, LANES), F32),
                        pltpu.VMEM((D, ng), BF16)],
        compiler_params=_cparams(1),
        name="mixer",
    )(x2d, x2d, *consts)


def _for_each_piece(npieces_ref, glob_ref, tile, fn):
    base = tile * PIECES_PER_TILE
    n = npieces_ref[tile]

    def one(p):
        fn(pl.multiple_of(p * ROW_PIECE, ROW_PIECE), pl.multiple_of(glob_ref[base + p] * ROW_PIECE, ROW_PIECE))

    def group(g, carry):
        for u in range(PIECE_UNROLL):
            one(g * PIECE_UNROLL + u)
        return carry

    n_groups = lax.div(n, jnp.int32(PIECE_UNROLL))
    lax.fori_loop(0, n_groups, group, 0)
    for u in range(PIECE_UNROLL - 1):
        @pl.when(n_groups * PIECE_UNROLL + u < n)
        def _():
            one(n_groups * PIECE_UNROLL + u)


def _used_slot_groups(n_pieces):
    return lax.div(n_pieces * ROW_PIECE + (SLOT_SUB - 1), jnp.int32(SLOT_SUB))


def _dispatch_kernel(npieces_ref, glob_ref, zflag_ref,
                     xn_ref, srow_ref, buf_ref, rows_ref, zeros_ref, sem, zsem, tsem):
    tt = xn_ref.shape[0] // MOE_STEP_TILES
    sl = rows_ref.shape[1] // MOE_STEP_TILES
    n_blocks = buf_ref.shape[0] // MOE_TM
    i = pl.program_id(0)
    n_steps = pl.num_programs(0)
    cur = lax.rem(i, 2)

    def zero_copy(b, flag):
        return pltpu.make_async_copy(zeros_ref, buf_ref.at[pl.ds(b * MOE_TM, MOE_TM)], zsem if flag == 1 else tsem)

    def for_flagged(flag, fn):
        def body(b, carry):
            @pl.when(zflag_ref[b] == flag)
            def _():
                fn(zero_copy(b, flag))
            return carry
        lax.fori_loop(0, n_blocks, body, 0)

    @pl.when(i == 0)
    def _():
        zeros_ref[...] = jnp.zeros_like(zeros_ref)
        for_flagged(1, lambda cp: cp.start())
        for_flagged(2, lambda cp: cp.start())
        for_flagged(1, lambda cp: cp.wait())

    def piece_copy(buf_slot, sub, local_row, global_row):
        return pltpu.make_async_copy(rows_ref.at[buf_slot, pl.ds(sub * sl + local_row, ROW_PIECE)],
                                     buf_ref.at[pl.ds(global_row, ROW_PIECE)], sem.at[buf_slot])

    def for_step_pieces(step, buf_slot, act):
        for sub in range(MOE_STEP_TILES):
            _for_each_piece(npieces_ref, glob_ref, step * MOE_STEP_TILES + sub,
                            lambda l, g, sub=sub: act(piece_copy(buf_slot, sub, l, g)))

    @pl.when(i >= 2)
    def _():
        for_step_pieces(i - 2, cur, lambda cp: cp.wait())

    for sub in range(MOE_STEP_TILES):
        sr = srow_ref[:, sub * tt:(sub + 1) * tt]
        slot1 = SLOT_RADIX * sr[0:1, :] + sr[1:2, :]
        slot2 = SLOT_RADIX * sr[2:3, :] + sr[3:4, :]
        n_sub = _used_slot_groups(npieces_ref[i * MOE_STEP_TILES + sub])
        for k in range(TOP_K * tt // SLOT_SUB, sl // SLOT_SUB + 1):
            @pl.when(n_sub == k)
            def _(k=k, sub=sub, slot1=slot1, slot2=slot2):
                m = k * SLOT_SUB
                r = lax.broadcasted_iota(jnp.int32, (m, tt), 0).astype(F32)
                sel = jnp.where((r == slot1) | (r == slot2), 1.0, 0.0).astype(BF16)
                rows_ref[cur, sub * sl:sub * sl + m, :] = _dot(sel, xn_ref[sub * tt:(sub + 1) * tt, :]).astype(BF16)
    for_step_pieces(i, cur, lambda cp: cp.start())

    @pl.when(i == n_steps - 1)
    def _():
        @pl.when(i >= 1)
        def _():
            for_step_pieces(i - 1, 1 - cur, lambda cp: cp.wait())
        for_step_pieces(i, cur, lambda cp: cp.wait())
        for_flagged(2, lambda cp: cp.wait())


def _dispatch(xn2, srow, npieces, piece_glob, zflag, n_rows):
    T, D = srow.shape[1], xn2.shape[1]
    tt = MOE_STEP_TILES * MIX_TS
    return pl.pallas_call(
        _dispatch_kernel,
        grid_spec=pltpu.PrefetchScalarGridSpec(
            num_scalar_prefetch=3,
            grid=(T // tt,),
            in_specs=[pl.BlockSpec((tt, D), lambda i, *_: (i, 0)),
                      pl.BlockSpec((8, tt), lambda i, *_: (0, i))],
            out_specs=pl.BlockSpec(memory_space=pl.ANY),
            scratch_shapes=[pltpu.VMEM((2, MOE_STEP_TILES * MOE_SL, D), BF16),
                            pltpu.VMEM((MOE_TM, D), BF16),
                            pltpu.SemaphoreType.DMA((2,)),
                            pltpu.SemaphoreType.DMA(()),
                            pltpu.SemaphoreType.DMA(())]),
        out_shape=jax.ShapeDtypeStruct((n_rows, D), BF16),
        compiler_params=_cparams(1),
        name="dispatch",
    )(npieces, piece_glob, zflag, xn2, srow)


def _experts_kernel(blk_e_ref, nused_ref, nsub_ref, first_ref, next_e_ref, slot_ref,
                    x_ref, wg_hbm, wu_hbm, wd_hbm, y_ref,
                    wg32_ref, wu32_ref, wd32_ref, wgb_ref, wub_ref, wdb_ref, sem):
    i = pl.program_id(0)
    used = i < nused_ref[0]
    n_sub = nsub_ref[i]
    landing = ((wg_hbm, wg32_ref), (wu_hbm, wu32_ref), (wd_hbm, wd32_ref))

    def weight_copies(e, s):
        return [pltpu.make_async_copy(hbm.at[e], vmem.at[s], sem.at[s, n]) for n, (hbm, vmem) in enumerate(landing)]

    @pl.when(used & (i == 0))
    def _():
        for cp in weight_copies(blk_e_ref[0], 0):
            cp.start()

    run_start = used & (first_ref[i] > 0)
    s = slot_ref[i]

    @pl.when(run_start)
    def _():
        for cp in weight_copies(blk_e_ref[i], s):
            cp.wait()

        @pl.when(next_e_ref[i] >= 0)
        def _():
            for cp in weight_copies(next_e_ref[i], 1 - s):
                cp.start()

    def swiglu(m, wg, wu, wd):
        x = x_ref[:m, :]
        hg = _dot(x, wg)
        hu = _dot(x, wu)
        hid = (_silu(hg) * hu).astype(BF16)
        y_ref[:m, :] = _dot(hid, wd).astype(BF16)
        if m < MOE_TM:
            y_ref[m:, :] = jnp.zeros((MOE_TM - m, y_ref.shape[1]), BF16)

    for k in range(1, MOE_TM // EXPERT_SUB + 1):
        @pl.when(run_start & (n_sub == k))
        def _(k=k):
            wg = wg32_ref[s].astype(BF16)
            wu = wu32_ref[s].astype(BF16)
            wd = wd32_ref[s].astype(BF16)
            wgb_ref[...] = wg
            wub_ref[...] = wu
            wdb_ref[...] = wd
            swiglu(k * EXPERT_SUB, wg, wu, wd)

        @pl.when(used & jnp.logical_not(run_start) & (n_sub == k))
        def _(k=k):
            swiglu(k * EXPERT_SUB, wgb_ref[...], wub_ref[...], wdb_ref[...])


def _experts(buf, blk_e, nused, nsub, run_first, next_e, run_slot, w_gate, w_up, w_down):
    R, D = buf.shape
    de = w_gate.shape[2]
    n_blocks = R // MOE_TM
    row_map = lambda i, be, nu, *_: (jnp.minimum(i, nu[0] - 1), 0)
    return pl.pallas_call(
        _experts_kernel,
        grid_spec=pltpu.PrefetchScalarGridSpec(
            num_scalar_prefetch=6,
            grid=(n_blocks,),
            in_specs=[pl.BlockSpec((MOE_TM, D), row_map),
                      pl.BlockSpec(memory_space=pl.ANY),
                      pl.BlockSpec(memory_space=pl.ANY),
                      pl.BlockSpec(memory_space=pl.ANY)],
            out_specs=pl.BlockSpec((MOE_TM, D), row_map),
            scratch_shapes=[pltpu.VMEM((2, D, de), F32), pltpu.VMEM((2, D, de), F32), pltpu.VMEM((2, de, D), F32),
                            pltpu.VMEM((D, de), BF16), pltpu.VMEM((D, de), BF16), pltpu.VMEM((de, D), BF16),
                            pltpu.SemaphoreType.DMA((2, 3))]),
        out_shape=jax.ShapeDtypeStruct((R, D), BF16),
        input_output_aliases={6: 0},
        compiler_params=_cparams(1),
        name="experts",
    )(blk_e, nused, nsub, run_first, next_e, run_slot, buf, w_gate, w_up, w_down)


def _combine_kernel(npieces_ref, glob_ref, x2_ref, rti_ref, rtf_ref, gfin_ref, yb_ref, out_ref,
                    rows_ref, sem):
    tt = x2_ref.shape[0] // MOE_STEP_TILES
    sl = rows_ref.shape[1] // MOE_STEP_TILES
    i = pl.program_id(0)
    n_steps = pl.num_programs(0)
    cur = lax.rem(i, 2)

    def piece_copy(buf_slot, sub, local_row, global_row):
        return pltpu.make_async_copy(yb_ref.at[pl.ds(global_row, ROW_PIECE)],
                                     rows_ref.at[buf_slot, pl.ds(sub * sl + local_row, ROW_PIECE)], sem.at[buf_slot])

    def for_step_pieces(step, buf_slot, act):
        for sub in range(MOE_STEP_TILES):
            _for_each_piece(npieces_ref, glob_ref, step * MOE_STEP_TILES + sub,
                            lambda l, g, sub=sub: act(piece_copy(buf_slot, sub, l, g)))

    @pl.when(i == 0)
    def _():
        rows_ref[...] = jnp.zeros_like(rows_ref)
        for_step_pieces(0, 0, lambda cp: cp.start())

    @pl.when(i + 1 < n_steps)
    def _():
        for_step_pieces(i + 1, 1 - cur, lambda cp: cp.start())

    for_step_pieces(i, cur, lambda cp: cp.wait())

    for sub in range(MOE_STEP_TILES):
        ts_rows = slice(sub * tt, (sub + 1) * tt)
        rti = rti_ref[ts_rows, :]
        rtf = rtf_ref[ts_rows, :]
        slot1 = rti[:, 2:3]
        slot2 = rti[:, 3:4]
        n_sub = _used_slot_groups(npieces_ref[i * MOE_STEP_TILES + sub])
        for k in range(TOP_K * tt // SLOT_SUB, sl // SLOT_SUB + 1):
            @pl.when(n_sub == k)
            def _(k=k, sub=sub, ts_rows=ts_rows, rtf=rtf, slot1=slot1, slot2=slot2):
                m = k * SLOT_SUB
                lane = lax.broadcasted_iota(jnp.int32, (tt, m), 1)
                g = jnp.where(lane == slot1, rtf[:, 0:1], jnp.where(lane == slot2, rtf[:, 1:2], 0.0)).astype(BF16)
                y = x2_ref[ts_rows, :] + _dot(g, rows_ref[cur, sub * sl:sub * sl + m, :])
                ms = jnp.mean(y * y, axis=-1, keepdims=True)
                out_ref[ts_rows, :] = y * lax.rsqrt(ms + EPS) * gfin_ref[...]


def _combine(x2, rti, rtf, g_final, yb, npieces, piece_glob):
    T, D = rti.shape[0], x2.shape[1]
    tt = MOE_STEP_TILES * MIX_TS
    tok = lambda i, *_: (i, 0)
    return pl.pallas_call(
        _combine_kernel,
        grid_spec=pltpu.PrefetchScalarGridSpec(
            num_scalar_prefetch=2,
            grid=(T // tt,),
            in_specs=[pl.BlockSpec((tt, D), tok),
                      pl.BlockSpec((tt, LANES), tok),
                      pl.BlockSpec((tt, LANES), tok),
                      pl.BlockSpec((1, D), lambda i, *_: (0, 0)),
                      pl.BlockSpec(memory_space=pl.ANY)],
            out_specs=pl.BlockSpec((tt, D), tok),
            scratch_shapes=[pltpu.VMEM((2, MOE_STEP_TILES * MOE_SL, D), BF16),
                            pltpu.SemaphoreType.DMA((2,))]),
        out_shape=jax.ShapeDtypeStruct((T, D), F32),
        compiler_params=_cparams(1),
        name="combine",
    )(npieces, piece_glob, x2, rti, rtf, g_final, yb)


def _pad_lanes(a, width=LANES):
    return jnp.pad(a, ((0, 0), (0, width - a.shape[1])))


def kernel(x, g_mix, w_in, b_if, conv_q, conv_k, g_head, w_pool, pool_scale, w_br_a, w_br_b, w_out,
           g_ffn, w_rg, b_rg, w_re, b_re, w_e_gate, w_e_up, w_e_down, g_final):
    B, S, D = x.shape
    T = B * S
    assert g_mix.shape[0] == 1, "single-layer block"
    assert S % MIX_TS == 0 and (T // MIX_TS) % MOE_STEP_TILES == 0
    d_pool = w_br_a.shape[1]
    d_ml = w_br_b.shape[1]
    x2d = x.reshape(T, D)

    params = {
        "b_if": _pad_lanes(b_if[0][None, :]),
        "b_if_t": jnp.pad(b_if[0][:, None], ((0, 16 - 2 * N_HEADS), (0, 0))),
        "conv_q": conv_q[0], "conv_k": conv_k[0],
        "g_head": g_head[0][None, :],
        "w_pool": w_pool[0].astype(BF16),
        "pool_scale": pool_scale[0][None, :],
        "w_br_a": w_br_a[0].astype(BF16), "w_br_b": w_br_b[0].astype(BF16),
        "w_out": w_out[0].astype(BF16),
        "g_ffn": g_ffn[0][None, :],
        "w_r": _pad_lanes(jnp.concatenate([w_rg[0], w_re[0]], axis=1)).astype(BF16),
        "b_r": _pad_lanes(jnp.concatenate([b_rg[0], b_re[0]])[None, :]),
    }

    x2, xn2, rti, rtf, tstat, srow = _mixer(x2d, g_mix[0][None, :], w_in[0].astype(BF16), params, B, S)

    n_tiles = T // MIX_TS
    n_rows = n_tiles * MOE_SL + N_EXPERTS * MOE_TM
    n_blocks = n_rows // MOE_TM
    i32 = lambda a: a.astype(jnp.int32)
    mm = lambda a, b: jnp.round(jnp.dot(a, b, precision=lax.Precision.HIGHEST, preferred_element_type=F32))
    e_ids = np.arange(N_EXPERTS)
    t_ids = np.arange(n_tiles)
    b_ids = np.arange(n_blocks)
    cum_e = jnp.asarray(e_ids[:, None] <= e_ids[None, :], F32)
    cum_t = jnp.asarray(t_ids[:, None] >= t_ids[None, :], F32)
    cum_b = jnp.asarray(b_ids[:, None] <= b_ids[None, :], F32)
    e_row = jnp.asarray(e_ids[None, :], F32)

    pcs = tstat.reshape(n_tiles, 8, LANES)[:, 0, ROUTER_LANE0:ROUTER_LANE0 + N_EXPERTS].astype(F32)
    piece_end = mm(pcs, cum_e)
    piece_loc = piece_end - pcs
    tile_cum = mm(cum_t, pcs)
    rows_e = tile_cum[-1:, :] * ROW_PIECE
    padded = jnp.floor((rows_e + (MOE_TM - 1)) * (1.0 / MOE_TM)) * MOE_TM
    pend = mm(padded, cum_e)
    poff = pend - padded
    piece_glob = poff * (1.0 / ROW_PIECE) + tile_cum - pcs
    nused_f = pend[0, -1] * (1.0 / MOE_TM)
    blk_start = jnp.asarray(b_ids[:, None] * MOE_TM, F32)
    blk_e_f = jnp.minimum(jnp.sum((pend <= blk_start).astype(F32), axis=1, keepdims=True), N_EXPERTS - 1.0)
    blk_oh = (blk_e_f == e_row).astype(F32)
    later_nonempty = (e_ids[None, :] > e_ids[:, None]) & (rows_e > 0)
    next_of_e = jnp.min(jnp.where(later_nonempty, e_row, float(N_EXPERTS)), axis=1)
    next_of_e = jnp.where(next_of_e == N_EXPERTS, -1.0, next_of_e)
    per_blk = mm(blk_oh, jnp.stack([pend[0], (poff + rows_e)[0], next_of_e], axis=1))
    blk_used = blk_start < pend[0, -1]
    zflag = jnp.where(blk_used, (blk_start + MOE_TM == per_blk[:, 0:1]).astype(F32), 2.0)
    rows_in_blk = jnp.clip(per_blk[:, 1:2] - blk_start, 0.0, float(MOE_TM))
    nsub = jnp.floor((rows_in_blk + (EXPERT_SUB - 1)) * (1.0 / EXPERT_SUB))
    prev_e = jnp.concatenate([jnp.full((1, 1), -1.0, F32), blk_e_f[:-1]], axis=0)
    run_first = (blk_used & (blk_e_f != prev_e)).astype(F32)
    run_idx = mm(run_first.reshape(1, n_blocks), cum_b) - 1.0
    run_slot = run_idx - 2.0 * jnp.floor(run_idx * 0.5)
    p_ids = jnp.asarray(np.arange(PIECES_PER_TILE), F32)
    e_of_p = jnp.minimum(jnp.sum((piece_end[:, None, :] <= p_ids[None, :, None]).astype(F32), axis=2),
                         N_EXPERTS - 1.0)
    shift = jnp.sum(jnp.where(e_of_p[:, :, None] == e_row[None], (piece_glob - piece_loc)[:, None, :], 0.0), axis=2)
    glob_of_p = i32(shift + p_ids[None, :]).reshape(n_tiles * PIECES_PER_TILE)
    npieces = i32(piece_end[:, -1])
    flat_b = lambda a: i32(a).reshape(n_blocks)
    blk_e, nused = flat_b(blk_e_f), i32(nused_f).reshape(1)

    buf = _dispatch(xn2, srow, npieces, glob_of_p, flat_b(zflag), n_rows)
    yb = _experts(buf, blk_e, nused, flat_b(nsub), flat_b(run_first), flat_b(per_blk[:, 2:3]), flat_b(run_slot),
                  w_e_gate[0], w_e_up[0], w_e_down[0])
    out = _combine(x2, rti, rtf, g_final[None, :], yb, npieces, glob_of_p)
    return out.reshape(B, S, D)
```

```python
import functools

import numpy as np
import jax
import jax.numpy as jnp
from jax import lax
from jax.experimental import pallas as pl
from jax.experimental.pallas import tpu as pltpu

F32 = jnp.float32
BF16 = jnp.bfloat16

CHUNK = 64
POOL_WINDOWS = (2, 4, 8, 16)
N_HEADS = 4
CONV_K = 4
N_GROUPS = 4
EXPERTS_PER_GROUP = 8
N_EXPERTS = N_GROUPS * EXPERTS_PER_GROUP
TOP_K = 2
EPS = 1e-6

LANES = 128
HALO = 16
ROUTER_LANE0 = N_GROUPS

INPROJ_TN = 256
MIX_TS = 256
MOE_TM = 512
ROW_PIECE = 16
MOE_SL = TOP_K * MIX_TS + N_EXPERTS * ROW_PIECE
PIECES_PER_TILE = MOE_SL // ROW_PIECE
MOE_STEP_TILES = 2
SLOT_SUB = 128
PIECE_UNROLL = 4
EXPERT_SUB = 128
SLOT_RADIX = 16
VMEM_LIMIT = 56 * 1024 * 1024


def _cparams(n_axes):
    return pltpu.CompilerParams(dimension_semantics=("arbitrary",) * n_axes,
                                vmem_limit_bytes=VMEM_LIMIT)


def _sigmoid(v):
    return 0.5 * jnp.tanh(0.5 * v) + 0.5


def _silu(v):
    return v * _sigmoid(v)


def _log_sigmoid(v):
    return jnp.minimum(v, 0.0) - jnp.log1p(jnp.exp(-jnp.abs(v)))


def _split3(v):
    hi = v.astype(BF16)
    r1 = v - hi.astype(F32)
    mid = r1.astype(BF16)
    lo = (r1 - mid.astype(F32)).astype(BF16)
    return hi, mid, lo


def _dot(a, b):
    return jnp.dot(a, b, preferred_element_type=F32)


def _dot_nt(a, b):
    return lax.dot_general(a, b, (((1,), (1,)), ((), ())), preferred_element_type=F32)


def _dot_tn(a, b):
    return lax.dot_general(a, b, (((0,), (0,)), ((), ())), preferred_element_type=F32)


def _inproj_steps(x_ref, g_ref, w_refs, z_refs, xn_ref):
    w_ref_all, wg_ref = w_refs
    zm_ref, zg_ref, zif_ref, zift_ref = z_refs
    n_main = zm_ref.shape[1]

    def norm():
        x = x_ref[...]
        ms = jnp.mean(x * x, axis=-1, keepdims=True)
        xn_ref[...] = (x * lax.rsqrt(ms + EPS) * g_ref[...]).astype(BF16)

    def block(w_ref, z_ref, c0):
        def run():
            cols = slice(c0, c0 + INPROJ_TN)
            z_ref[:, cols] = _dot(xn_ref[...], w_ref[:, cols]).astype(BF16)
        return run

    def gates():
        zif = _dot(xn_ref[...], w_ref_all[:, n_main:n_main + LANES])
        zif_ref[...] = zif
        zift_ref[...] = zif.T[:zift_ref.shape[0], :]

    steps = [norm, gates]
    steps += [block(w_ref_all, zm_ref, c0) for c0 in range(0, n_main, INPROJ_TN)]
    steps += [block(wg_ref, zg_ref, c0) for c0 in range(0, zg_ref.shape[1], INPROJ_TN)]
    return steps


def _mixer_kernel(x_ref, xnext_ref, gmix_ref, win_ref,
                  bif_ref, bift_ref, convq_ref, convk_ref, ghead_ref, wpool32_ref, pscale_ref,
                  wa32_ref, wb32_ref, wo32_ref, gffn_ref, wr_ref, br_ref,
                  tric_ref, trir_ref, stri_ref, ut_ref, sel_ref,
                  x2_ref, xn2_ref, rti_ref, rtf_ref, tstat_ref, srow_ref,
                  zm_ref, zg_ref, zif_ref, zift_ref, zm_nxt, zg_nxt, zif_nxt, zift_nxt, xn_ref,
                  ext_ref, q_ref, k_ref, h_ref, pool_ref, cst_ref, mst_ref, lg_ref, wg_ref,
                  wpool_ref, wa_ref, wb_ref, wo_ref, *, tiles_per_seq):
    ts = x_ref.shape[0]
    d_pool = wa_ref.shape[0]
    d_ml = wb_ref.shape[0]
    dh = d_ml // N_HEADS
    n_chunks = ts // CHUNK
    g_step = pl.program_id(0)
    j = lax.rem(g_step, tiles_per_seq)
    w_in_refs = (win_ref, wg_ref)
    z_cur = (zm_ref, zg_ref, zif_ref, zift_ref)
    z_nxt = (zm_nxt, zg_nxt, zif_nxt, zift_nxt)
    first = g_step == 0

    @pl.when(first)
    def _():
        g0 = zm_ref.shape[1] + 2 * N_HEADS
        wg_ref[...] = win_ref[:, g0:g0 + wg_ref.shape[1]]
        for dst, src in ((wpool_ref, wpool32_ref), (wa_ref, wa32_ref), (wb_ref, wb32_ref), (wo_ref, wo32_ref)):
            dst[...] = src[...].astype(BF16)
        for step in _inproj_steps(x_ref, gmix_ref, w_in_refs, z_cur, xn_ref):
            step()
        lg_ref[...] = jnp.zeros_like(lg_ref)

    @pl.when(jnp.logical_not(first))
    def _():
        for dst, src in zip(z_cur, z_nxt):
            dst[...] = src[...]

    @pl.when(j == 0)
    def _():
        ext_ref[:, :HALO, :] = jnp.zeros((ext_ref.shape[0], HALO, LANES), F32)
        cst_ref[...] = jnp.zeros_like(cst_ref)
        mst_ref[...] = jnp.zeros_like(mst_ref)

    pending = _inproj_steps(xnext_ref, gmix_ref, w_in_refs, z_nxt, xn_ref)

    def project_some(n=1):
        for _ in range(min(n, len(pending))):
            pending.pop(0)()

    project_some(2)

    routed = _route_select(lg_ref[...])

    row = lax.broadcasted_iota(jnp.int32, (ts, LANES), 0)
    pos1 = (row + j * ts + 1).astype(F32)

    def history(cg):
        cur = zm_ref[:, cg * LANES:(cg + 1) * LANES].astype(F32)
        ext_ref[cg, HALO:, :] = cur
        return cur, lambda s: ext_ref[cg, HALO - s:HALO - s + ts, :]

    def keep_history(cg, cur):
        ext_ref[cg, :HALO, :] = cur[ts - HALO:, :]

    n_pool_groups = d_pool // LANES
    for g in range(n_pool_groups):
        w = POOL_WINDOWS[g]
        cur, shifted = history(g)
        win = cur
        for s in range(1, w):
            win = win + shifted(s)
        keep_history(g, cur)
        cnt = jnp.minimum(pos1, float(w))
        d = win / cnt - cur
        y = _dot(d.astype(BF16), wpool_ref[g]) * pscale_ref[:, g * LANES:(g + 1) * LANES]
        pool_ref[:, g * LANES:(g + 1) * LANES] = y.astype(BF16)
        project_some()

    n_ml_groups = d_ml // LANES
    for which, (cw_ref, dst_ref, scale) in enumerate(((convq_ref, q_ref, 1.0), (convk_ref, k_ref, dh ** -0.5))):
        for g in range(n_ml_groups):
            cols = slice(g * LANES, (g + 1) * LANES)
            cg = n_pool_groups + which * n_ml_groups + g
            cur, shifted = history(cg)
            acc = cur * cw_ref[CONV_K - 1:CONV_K, cols]
            for sft in range(1, CONV_K):
                acc = acc + shifted(sft) * cw_ref[CONV_K - 1 - sft:CONV_K - sft, cols]
            keep_history(cg, cur)
            dst_ref[:, cols] = (_silu(acc) * scale).astype(BF16)
        project_some()

    _route_slots(routed, stri_ref, ut_ref, sel_ref, rti_ref, rtf_ref, tstat_ref, srow_ref)
    project_some()

    zc = zif_ref[...] + bif_ref[...]
    lf_c = _log_sigmoid(zc)
    bc = sum(_dot(tric_ref[...], p) for p in _split3(lf_c))
    zr = zift_ref[...] + bift_ref[...]
    lf_r = _log_sigmoid(zr)
    br = sum(_dot(p, trir_ref[...]) for p in _split3(lf_r))
    project_some(2)

    ti = lax.broadcasted_iota(jnp.int32, (CHUNK, CHUNK), 0)
    si = lax.broadcasted_iota(jnp.int32, (CHUNK, CHUNK), 1)
    causal = si <= ti
    ones_blk = jnp.ones((CHUNK, dh), BF16)
    v0 = d_pool + 2 * d_ml
    ig_rep = [jnp.broadcast_to(zc[:, h:h + 1], (ts, dh)) for h in range(N_HEADS)]
    bt_rep = [jnp.broadcast_to(bc[:, N_HEADS + h:N_HEADS + h + 1], (ts, dh)) for h in range(N_HEADS)]

    m_state = [mst_ref[h:h + 1, :] for h in range(N_HEADS)]
    c_state = [cst_ref[h] for h in range(N_HEADS)]
    def stage_scores(c):
        rs = slice(c * CHUNK, (c + 1) * CHUNK)
        out = []
        for h in range(N_HEADS):
            hs = slice(h * dh, (h + 1) * dh)
            q = q_ref[rs, hs]
            k = k_ref[rs, hs]
            bt = bt_rep[h][rs, :]
            r_row = zr[h:h + 1, rs] - br[N_HEADS + h:N_HEADS + h + 1, rs]
            dmat = jnp.where(causal, bt[:, :CHUNK] + r_row, -jnp.inf)
            out.append(dict(q=q, k=k, bt=bt, dmat=dmat, qk=_dot_nt(q, k),
                            m_intra=jnp.max(dmat, axis=-1, keepdims=True)))
        return out

    def stage_state(c, st):
        rs = slice(c * CHUNK, (c + 1) * CHUNK)
        for h in range(N_HEADS):
            s = st[h]
            bt, k = s["bt"], s["k"]
            m_prev, c_prev = m_state[h], c_state[h]
            v_aug = jnp.concatenate([zm_ref[rs, v0 + h * dh:v0 + (h + 1) * dh], ones_blk], axis=-1)
            igc = ig_rep[h][rs, :]
            b_last = bt[CHUNK - 1:CHUNK, :]
            a_log = b_last - bt + igc
            a_max = jnp.max(a_log, axis=0, keepdims=True)
            m_new = jnp.maximum(b_last + m_prev, a_max)
            kw = (k.astype(F32) * jnp.exp(a_log - m_new)).astype(BF16)
            decay = jnp.exp(b_last + m_prev - m_new)
            s.update(v_aug=v_aug, m_prev=m_prev, qc=_dot(s["q"], c_prev.astype(BF16)))
            c_state[h] = jnp.concatenate([decay, decay], axis=-1) * c_prev + _dot_tn(kw, v_aug)
            m_state[h] = m_new

    def stage_values(c, st):
        rs = slice(c * CHUNK, (c + 1) * CHUNK)
        for h in range(N_HEADS):
            s = st[h]
            hs = slice(h * dh, (h + 1) * dh)
            inter = s["bt"] + s["m_prev"]
            m_t = jnp.maximum(inter, s["m_intra"])
            w_inter = jnp.exp(inter - m_t)
            smat = s["qk"] * jnp.exp(s["dmat"] - m_t[:, :CHUNK])
            sv = _dot(smat.astype(BF16), s["v_aug"])
            qc = s["qc"]
            nq = w_inter * qc[:, dh:] + sv[:, dh:]
            den = jnp.maximum(jnp.abs(nq), jnp.exp(-m_t))
            h_ref[rs, hs] = (w_inter * qc[:, :dh] + sv[:, :dh]) / den

    staged = stage_scores(0)
    for c in range(n_chunks):
        stage_state(c, staged)
        project_some()
        nxt = stage_scores(c + 1) if c + 1 < n_chunks else None
        project_some()
        stage_values(c, staged)
        staged = nxt
    for h in range(N_HEADS):
        cst_ref[h] = c_state[h]
        mst_ref[h:h + 1, :] = m_state[h]

    o0 = v0 + d_ml
    for h in range(N_HEADS):
        hs = slice(h * dh, (h + 1) * dh)
        hv = h_ref[:, hs]
        mu = jnp.mean(hv, axis=-1, keepdims=True)
        hc = hv - mu
        var = jnp.mean(hc * hc, axis=-1, keepdims=True)
        hn = hc * lax.rsqrt(var + EPS) * ghead_ref[:, hs]
        og = _sigmoid(zm_ref[:, o0 + h * dh:o0 + (h + 1) * dh].astype(F32))
        q_ref[:, hs] = (og * hn).astype(BF16)
    y_a = _dot(pool_ref[...], wa_ref[...])
    y_b = _dot(q_ref[...], wb_ref[...])
    d_model = x_ref.shape[1]
    ga = _sigmoid(zg_ref[:, :d_model].astype(F32))
    gb = _sigmoid(zg_ref[:, d_model:].astype(F32))
    merged = (ga * y_a + gb * y_b).astype(BF16)
    x2 = x_ref[...] + _dot(merged, wo_ref[...])
    x2_ref[...] = x2
    project_some(len(pending))

    ms = jnp.mean(x2 * x2, axis=-1, keepdims=True)
    xn2 = x2 * lax.rsqrt(ms + EPS) * gffn_ref[...]
    xh = xn2.astype(BF16)
    xn2_ref[...] = xh
    lg_ref[...] = _dot(xh, wr_ref[...]) + br_ref[...]
    project_some(len(pending))


def _route_select(lg):
    ts = lg.shape[0]
    lane = lax.broadcasted_iota(jnp.int32, (ts, LANES), 1)
    lanef = lane.astype(F32)
    big = float(4 * LANES)
    gl = jnp.where(lane < N_GROUPS, lg, -jnp.inf)
    gmax = jnp.max(gl, axis=-1, keepdims=True)
    g_sel = jnp.min(jnp.where(gl == gmax, lanef, big), axis=-1, keepdims=True)
    p_g = 1.0 / jnp.sum(jnp.exp(gl - gmax), axis=-1, keepdims=True)
    lo = ROUTER_LANE0 + EXPERTS_PER_GROUP * g_sel
    el = jnp.where((lanef >= lo) & (lanef < lo + EXPERTS_PER_GROUP), lg, -jnp.inf)
    m1 = jnp.max(el, axis=-1, keepdims=True)
    i1 = jnp.min(jnp.where(el == m1, lanef, big), axis=-1, keepdims=True)
    el2 = jnp.where(lanef == i1, -jnp.inf, el)
    m2 = jnp.max(el2, axis=-1, keepdims=True)
    i2 = jnp.min(jnp.where(el2 == m2, lanef, big), axis=-1, keepdims=True)
    e2x = jnp.exp(m2 - m1)
    gate1 = p_g / (1.0 + e2x)
    gate2 = p_g * e2x / (1.0 + e2x)
    return dict(lane=lane, i1=i1, i2=i2, gate1=gate1, gate2=gate2, oh1=lanef == i1, oh2=lanef == i2)


def _route_slots(r, stri_ref, ut_ref, sel_ref, rti_ref, rtf_ref, tstat_ref, srow_ref):
    lane, oh1, oh2, i1, i2 = r["lane"], r["oh1"], r["oh2"], r["i1"], r["i2"]
    ohs = jnp.where(oh1 | oh2, 1.0, 0.0)
    n_loc = jnp.sum(ohs, axis=0, keepdims=True)
    pieces = jnp.floor((n_loc + (ROW_PIECE - 1.0)) * (1.0 / ROW_PIECE))
    piece_off = _dot(jnp.broadcast_to(pieces, (8, LANES)).astype(BF16), ut_ref[...])[0:1, :]
    base = _dot(stri_ref[...], ohs.astype(BF16)) + ROW_PIECE * piece_off
    slot1 = jnp.sum(jnp.where(oh1, base, 0.0), axis=-1, keepdims=True)
    slot2 = jnp.sum(jnp.where(oh2, base, 0.0), axis=-1, keepdims=True)
    tstat_ref[...] = jnp.broadcast_to(pieces, tstat_ref.shape).astype(jnp.int32)

    rti = jnp.where(lane == 0, i1 - ROUTER_LANE0,
                    jnp.where(lane == 1, i2 - ROUTER_LANE0,
                              jnp.where(lane == 2, slot1, jnp.where(lane == 3, slot2, 0.0))))
    rti_ref[...] = rti.astype(jnp.int32)
    rtf_ref[...] = jnp.where(lane == 0, r["gate1"], jnp.where(lane == 1, r["gate2"], 0.0))
    h1 = jnp.floor(slot1 * (1.0 / SLOT_RADIX))
    h2 = jnp.floor(slot2 * (1.0 / SLOT_RADIX))
    parts = jnp.where(lane == 0, h1, jnp.where(lane == 1, slot1 - SLOT_RADIX * h1,
                      jnp.where(lane == 2, h2, jnp.where(lane == 3, slot2 - SLOT_RADIX * h2, 0.0))))
    srow_ref[...] = _dot_nt(sel_ref[...], parts.astype(BF16))


def _mixer(x2d, g_mix, w_all, params, batch, seq):
    T, D = x2d.shape
    ts = min(MIX_TS, seq)
    nts = seq // ts
    d_pool = params["w_br_a"].shape[0]
    d_ml = params["w_br_b"].shape[0]
    dh = d_ml // N_HEADS

    idx = np.arange(ts)
    same_chunk = (idx[:, None] // CHUNK) == (idx[None, :] // CHUNK)
    tri_c = jnp.asarray((idx[None, :] <= idx[:, None]) & same_chunk, BF16)
    tri_r = jnp.asarray((idx[:, None] <= idx[None, :]) & same_chunk, BF16)
    stri = jnp.asarray(idx[None, :] < idx[:, None], BF16)
    lane_idx = np.arange(LANES)
    ut = jnp.asarray(lane_idx[:, None] < lane_idx[None, :], BF16)
    sel = jnp.asarray(np.arange(8)[:, None] == lane_idx[None, :], BF16)

    n_tiles = batch * nts
    tok = lambda g: (g, 0)
    tok_in = lambda g: (jnp.minimum(g, n_tiles - 1), 0)
    tok_next = lambda g: (jnp.minimum(g + 1, n_tiles - 1), 0)
    tok_prev = lambda g: (jnp.maximum(g - 1, 0), 0)
    tok_prev_t = lambda g: (0, jnp.maximum(g - 1, 0))
    c2 = lambda g: (0, 0)
    c3 = lambda g: (0, 0, 0)
    full = lambda a: pl.BlockSpec(a.shape, c2 if a.ndim == 2 else c3)
    consts = [params[n] for n in ("b_if", "b_if_t", "conv_q", "conv_k", "g_head", "w_pool", "pool_scale",
                                  "w_br_a", "w_br_b", "w_out", "g_ffn", "w_r", "b_r")]
    consts = [g_mix, w_all] + consts + [tri_c, tri_r, stri, ut, sel]
    nm, ng = d_pool + 4 * d_ml, 2 * D
    z_scratch = [pltpu.VMEM((ts, nm), BF16), pltpu.VMEM((ts, ng), BF16),
                 pltpu.VMEM((ts, LANES), F32), pltpu.VMEM((16, ts), F32)]
    return pl.pallas_call(
        functools.partial(_mixer_kernel, tiles_per_seq=nts),
        grid=(n_tiles + 1,),
        in_specs=[pl.BlockSpec((ts, D), tok_in),
                  pl.BlockSpec((ts, D), tok_next)] + [full(a) for a in consts],
        out_specs=[pl.BlockSpec((ts, D), tok),
                   pl.BlockSpec((ts, D), tok),
                   pl.BlockSpec((ts, LANES), tok_prev),
                   pl.BlockSpec((ts, LANES), tok_prev),
                   pl.BlockSpec((8, LANES), tok_prev),
                   pl.BlockSpec((8, ts), tok_prev_t)],
        out_shape=[jax.ShapeDtypeStruct((T + ts, D), F32),
                   jax.ShapeDtypeStruct((T + ts, D), BF16),
                   jax.ShapeDtypeStruct((T, LANES), jnp.int32),
                   jax.ShapeDtypeStruct((T, LANES), F32),
                   jax.ShapeDtypeStruct((n_tiles * 8, LANES), jnp.int32),
                   jax.ShapeDtypeStruct((8, T), F32)],
        scratch_shapes=z_scratch + z_scratch + [
                        pltpu.VMEM((ts, D), BF16),
                        pltpu.VMEM(((d_pool + 2 * d_ml) // LANES, HALO + ts, LANES), F32),
                        pltpu.VMEM((ts, d_ml), BF16),
                        pltpu.VMEM((ts, d_ml), BF16),
                        pltpu.VMEM((ts, d_ml), F32),
                        pltpu.VMEM((ts, d_pool), BF16),
                        pltpu.VMEM((N_HEADS, dh, 2 * dh), F32),
                        pltpu.VMEM((8, LANES), F32),
                        pltpu.VMEM((ts, LANES), F32),
                        pltpu.VMEM((D, ng), BF16),
                        pltpu.VMEM(params["w_pool"].shape, BF16),
                        pltpu.VMEM((d_pool, D), BF16),
                        pltpu.VMEM((d_ml, D), BF16),
                        pltpu.VMEM((D, D), BF16)],
        compiler_params=_cparams(1),
        name="mixer",
    )(x2d, x2d, *consts)


def _for_each_piece(npieces_ref, glob_ref, tile, fn):
    base = tile * PIECES_PER_TILE
    n = npieces_ref[tile]

    def one(p):
        fn(pl.multiple_of(p * ROW_PIECE, ROW_PIECE), pl.multiple_of(glob_ref[base + p] * ROW_PIECE, ROW_PIECE))

    def group(g, carry):
        for u in range(PIECE_UNROLL):
            one(g * PIECE_UNROLL + u)
        return carry

    n_groups = lax.div(n, jnp.int32(PIECE_UNROLL))
    lax.fori_loop(0, n_groups, group, 0)
    for u in range(PIECE_UNROLL - 1):
        @pl.when(n_groups * PIECE_UNROLL + u < n)
        def _():
            one(n_groups * PIECE_UNROLL + u)


def _used_slot_groups(n_pieces):
    return lax.div(n_pieces * ROW_PIECE + (SLOT_SUB - 1), jnp.int32(SLOT_SUB))


def _dispatch_kernel(npieces_ref, glob_ref, zflag_ref,
                     xn_ref, srow_ref, buf_ref, rows_ref, zeros_ref, sem, zsem, tsem):
    tt = xn_ref.shape[0] // MOE_STEP_TILES
    sl = rows_ref.shape[1] // MOE_STEP_TILES
    n_blocks = buf_ref.shape[0] // MOE_TM
    i = pl.program_id(0)
    n_steps = pl.num_programs(0)
    cur = lax.rem(i, 2)

    def zero_copy(b, flag):
        return pltpu.make_async_copy(zeros_ref, buf_ref.at[pl.ds(b * MOE_TM, MOE_TM)], zsem if flag == 1 else tsem)

    def for_flagged(flag, fn):
        def body(b, carry):
            @pl.when(zflag_ref[b] == flag)
            def _():
                fn(zero_copy(b, flag))
            return carry
        lax.fori_loop(0, n_blocks, body, 0)

    @pl.when(i == 0)
    def _():
        zeros_ref[...] = jnp.zeros_like(zeros_ref)
        for_flagged(1, lambda cp: cp.start())
        for_flagged(2, lambda cp: cp.start())
        for_flagged(1, lambda cp: cp.wait())

    def piece_copy(buf_slot, sub, local_row, global_row):
        return pltpu.make_async_copy(rows_ref.at[buf_slot, pl.ds(sub * sl + local_row, ROW_PIECE)],
                                     buf_ref.at[pl.ds(global_row, ROW_PIECE)], sem.at[buf_slot])

    def for_step_pieces(step, buf_slot, act):
        for sub in range(MOE_STEP_TILES):
            _for_each_piece(npieces_ref, glob_ref, step * MOE_STEP_TILES + sub,
                            lambda l, g, sub=sub: act(piece_copy(buf_slot, sub, l, g)))

    @pl.when(i >= 2)
    def _():
        for_step_pieces(i - 2, cur, lambda cp: cp.wait())

    for sub in range(MOE_STEP_TILES):
        sr = srow_ref[:, sub * tt:(sub + 1) * tt]
        slot1 = SLOT_RADIX * sr[0:1, :] + sr[1:2, :]
        slot2 = SLOT_RADIX * sr[2:3, :] + sr[3:4, :]
        n_sub = _used_slot_groups(npieces_ref[i * MOE_STEP_TILES + sub])
        for k in range(TOP_K * tt // SLOT_SUB, sl // SLOT_SUB + 1):
            @pl.when(n_sub == k)
            def _(k=k, sub=sub, slot1=slot1, slot2=slot2):
                m = k * SLOT_SUB
                r = lax.broadcasted_iota(jnp.int32, (m, tt), 0).astype(F32)
                sel = jnp.where((r == slot1) | (r == slot2), 1.0, 0.0).astype(BF16)
                rows_ref[cur, sub * sl:sub * sl + m, :] = _dot(sel, xn_ref[sub * tt:(sub + 1) * tt, :]).astype(BF16)
    for_step_pieces(i, cur, lambda cp: cp.start())

    @pl.when(i == n_steps - 1)
    def _():
        @pl.when(i >= 1)
        def _():
            for_step_pieces(i - 1, 1 - cur, lambda cp: cp.wait())
        for_step_pieces(i, cur, lambda cp: cp.wait())
        for_flagged(2, lambda cp: cp.wait())


def _dispatch(xn2, srow, npieces, piece_glob, zflag, n_rows):
    T, D = srow.shape[1], xn2.shape[1]
    tt = MOE_STEP_TILES * MIX_TS
    return pl.pallas_call(
        _dispatch_kernel,
        grid_spec=pltpu.PrefetchScalarGridSpec(
            num_scalar_prefetch=3,
            grid=(T // tt,),
            in_specs=[pl.BlockSpec((tt, D), lambda i, *_: (i, 0)),
                      pl.BlockSpec((8, tt), lambda i, *_: (0, i))],
            out_specs=pl.BlockSpec(memory_space=pl.ANY),
            scratch_shapes=[pltpu.VMEM((2, MOE_STEP_TILES * MOE_SL, D), BF16),
                            pltpu.VMEM((MOE_TM, D), BF16),
                            pltpu.SemaphoreType.DMA((2,)),
                            pltpu.SemaphoreType.DMA(()),
                            pltpu.SemaphoreType.DMA(())]),
        out_shape=jax.ShapeDtypeStruct((n_rows, D), BF16),
        compiler_params=_cparams(1),
        name="dispatch",
    )(npieces, piece_glob, zflag, xn2, srow)


def _experts_kernel(blk_e_ref, nused_ref, nsub_ref, first_ref, next_e_ref, slot_ref,
                    x_ref, wg_hbm, wu_hbm, wd_hbm, y_ref,
                    wg32_ref, wu32_ref, wd32_ref, wgb_ref, wub_ref, wdb_ref, sem):
    i = pl.program_id(0)
    used = i < nused_ref[0]
    n_sub = nsub_ref[i]
    landing = ((wg_hbm, wg32_ref), (wu_hbm, wu32_ref), (wd_hbm, wd32_ref))

    def weight_copies(e, s):
        return [pltpu.make_async_copy(hbm.at[e], vmem.at[s], sem.at[s, n]) for n, (hbm, vmem) in enumerate(landing)]

    @pl.when(used & (i == 0))
    def _():
        for cp in weight_copies(blk_e_ref[0], 0):
            cp.start()

    run_start = used & (first_ref[i] > 0)
    s = slot_ref[i]

    @pl.when(run_start)
    def _():
        for cp in weight_copies(blk_e_ref[i], s):
            cp.wait()

        @pl.when(next_e_ref[i] >= 0)
        def _():
            for cp in weight_copies(next_e_ref[i], 1 - s):
                cp.start()

    def swiglu(m, wg, wu, wd):
        x = x_ref[:m, :]
        hg = _dot(x, wg)
        hu = _dot(x, wu)
        hid = (_silu(hg) * hu).astype(BF16)
        y_ref[:m, :] = _dot(hid, wd).astype(BF16)
        if m < MOE_TM:
            y_ref[m:, :] = jnp.zeros((MOE_TM - m, y_ref.shape[1]), BF16)

    for k in range(1, MOE_TM // EXPERT_SUB + 1):
        @pl.when(run_start & (n_sub == k))
        def _(k=k):
            wg = wg32_ref[s].astype(BF16)
            wu = wu32_ref[s].astype(BF16)
            wd = wd32_ref[s].astype(BF16)
            wgb_ref[...] = wg
            wub_ref[...] = wu
            wdb_ref[...] = wd
            swiglu(k * EXPERT_SUB, wg, wu, wd)

        @pl.when(used & jnp.logical_not(run_start) & (n_sub == k))
        def _(k=k):
            swiglu(k * EXPERT_SUB, wgb_ref[...], wub_ref[...], wdb_ref[...])


def _experts(buf, blk_e, nused, nsub, run_first, next_e, run_slot, w_gate, w_up, w_down):
    R, D = buf.shape
    de = w_gate.shape[2]
    n_blocks = R // MOE_TM
    row_map = lambda i, be, nu, *_: (jnp.minimum(i, nu[0] - 1), 0)
    return pl.pallas_call(
        _experts_kernel,
        grid_spec=pltpu.PrefetchScalarGridSpec(
            num_scalar_prefetch=6,
            grid=(n_blocks,),
            in_specs=[pl.BlockSpec((MOE_TM, D), row_map),
                      pl.BlockSpec(memory_space=pl.ANY),
                      pl.BlockSpec(memory_space=pl.ANY),
                      pl.BlockSpec(memory_space=pl.ANY)],
            out_specs=pl.BlockSpec((MOE_TM, D), row_map),
            scratch_shapes=[pltpu.VMEM((2, D, de), F32), pltpu.VMEM((2, D, de), F32), pltpu.VMEM((2, de, D), F32),
                            pltpu.VMEM((D, de), BF16), pltpu.VMEM((D, de), BF16), pltpu.VMEM((de, D), BF16),
                            pltpu.SemaphoreType.DMA((2, 3))]),
        out_shape=jax.ShapeDtypeStruct((R, D), BF16),
        input_output_aliases={6: 0},
        compiler_params=_cparams(1),
        name="experts",
    )(blk_e, nused, nsub, run_first, next_e, run_slot, buf, w_gate, w_up, w_down)


def _combine_kernel(npieces_ref, glob_ref, x2_ref, rti_ref, rtf_ref, gfin_ref, yb_ref, out_ref,
                    rows_ref, sem):
    tt = x2_ref.shape[0] // MOE_STEP_TILES
    sl = rows_ref.shape[1] // MOE_STEP_TILES
    i = pl.program_id(0)
    n_steps = pl.num_programs(0)
    cur = lax.rem(i, 2)

    def piece_copy(buf_slot, sub, local_row, global_row):
        return pltpu.make_async_copy(yb_ref.at[pl.ds(global_row, ROW_PIECE)],
                                     rows_ref.at[buf_slot, pl.ds(sub * sl + local_row, ROW_PIECE)], sem.at[buf_slot])

    def for_step_pieces(step, buf_slot, act):
        for sub in range(MOE_STEP_TILES):
            _for_each_piece(npieces_ref, glob_ref, step * MOE_STEP_TILES + sub,
                            lambda l, g, sub=sub: act(piece_copy(buf_slot, sub, l, g)))

    @pl.when(i == 0)
    def _():
        rows_ref[...] = jnp.zeros_like(rows_ref)
        for_step_pieces(0, 0, lambda cp: cp.start())

    @pl.when(i + 1 < n_steps)
    def _():
        for_step_pieces(i + 1, 1 - cur, lambda cp: cp.start())

    for_step_pieces(i, cur, lambda cp: cp.wait())

    for sub in range(MOE_STEP_TILES):
        ts_rows = slice(sub * tt, (sub + 1) * tt)
        rti = rti_ref[ts_rows, :]
        rtf = rtf_ref[ts_rows, :]
        slot1 = rti[:, 2:3]
        slot2 = rti[:, 3:4]
        n_sub = _used_slot_groups(npieces_ref[i * MOE_STEP_TILES + sub])
        for k in range(TOP_K * tt // SLOT_SUB, sl // SLOT_SUB + 1):
            @pl.when(n_sub == k)
            def _(k=k, sub=sub, ts_rows=ts_rows, rtf=rtf, slot1=slot1, slot2=slot2):
                m = k * SLOT_SUB
                lane = lax.broadcasted_iota(jnp.int32, (tt, m), 1)
                g = jnp.where(lane == slot1, rtf[:, 0:1], jnp.where(lane == slot2, rtf[:, 1:2], 0.0)).astype(BF16)
                y = x2_ref[ts_rows, :] + _dot(g, rows_ref[cur, sub * sl:sub * sl + m, :])
                ms = jnp.mean(y * y, axis=-1, keepdims=True)
                out_ref[ts_rows, :] = y * lax.rsqrt(ms + EPS) * gfin_ref[...]


def _combine(x2, rti, rtf, g_final, yb, npieces, piece_glob):
    T, D = rti.shape[0], x2.shape[1]
    tt = MOE_STEP_TILES * MIX_TS
    tok = lambda i, *_: (i, 0)
    return pl.pallas_call(
        _combine_kernel,
        grid_spec=pltpu.PrefetchScalarGridSpec(
            num_scalar_prefetch=2,
            grid=(T // tt,),
            in_specs=[pl.BlockSpec((tt, D), tok),
                      pl.BlockSpec((tt, LANES), tok),
                      pl.BlockSpec((tt, LANES), tok),
                      pl.BlockSpec((1, D), lambda i, *_: (0, 0)),
                      pl.BlockSpec(memory_space=pl.ANY)],
            out_specs=pl.BlockSpec((tt, D), tok),
            scratch_shapes=[pltpu.VMEM((2, MOE_STEP_TILES * MOE_SL, D), BF16),
                            pltpu.SemaphoreType.DMA((2,))]),
        out_shape=jax.ShapeDtypeStruct((T, D), F32),
        compiler_params=_cparams(1),
        name="combine",
    )(npieces, piece_glob, x2, rti, rtf, g_final, yb)


def _pad_lanes(a, width=LANES):
    return jnp.pad(a, ((0, 0), (0, width - a.shape[1])))


def kernel(x, g_mix, w_in, b_if, conv_q, conv_k, g_head, w_pool, pool_scale, w_br_a, w_br_b, w_out,
           g_ffn, w_rg, b_rg, w_re, b_re, w_e_gate, w_e_up, w_e_down, g_final):
    B, S, D = x.shape
    T = B * S
    assert g_mix.shape[0] == 1, "single-layer block"
    assert S % MIX_TS == 0 and (T // MIX_TS) % MOE_STEP_TILES == 0
    d_pool = w_br_a.shape[1]
    d_ml = w_br_b.shape[1]
    x2d = x.reshape(T, D)

    params = {
        "b_if": _pad_lanes(b_if[0][None, :]),
        "b_if_t": jnp.pad(b_if[0][:, None], ((0, 16 - 2 * N_HEADS), (0, 0))),
        "conv_q": conv_q[0], "conv_k": conv_k[0],
        "g_head": g_head[0][None, :],
        "w_pool": w_pool[0],
        "pool_scale": pool_scale[0][None, :],
        "w_br_a": w_br_a[0], "w_br_b": w_br_b[0],
        "w_out": w_out[0],
        "g_ffn": g_ffn[0][None, :],
        "w_r": _pad_lanes(jnp.concatenate([w_rg[0], w_re[0]], axis=1)).astype(BF16),
        "b_r": _pad_lanes(jnp.concatenate([b_rg[0], b_re[0]])[None, :]),
    }

    x2, xn2, rti, rtf, tstat, srow = _mixer(x2d, g_mix[0][None, :], w_in[0].astype(BF16), params, B, S)

    n_tiles = T // MIX_TS
    n_rows = n_tiles * MOE_SL + N_EXPERTS * MOE_TM
    n_blocks = n_rows // MOE_TM
    i32 = lambda a: a.astype(jnp.int32)
    mm = lambda a, b: jnp.round(jnp.dot(a, b, precision=lax.Precision.HIGHEST, preferred_element_type=F32))
    e_ids = np.arange(N_EXPERTS)
    t_ids = np.arange(n_tiles)
    b_ids = np.arange(n_blocks)
    cum_e = jnp.asarray(e_ids[:, None] <= e_ids[None, :], F32)
    cum_t = jnp.asarray(t_ids[:, None] >= t_ids[None, :], F32)
    cum_b = jnp.asarray(b_ids[:, None] <= b_ids[None, :], F32)
    e_row = jnp.asarray(e_ids[None, :], F32)

    pcs = tstat.reshape(n_tiles, 8, LANES)[:, 0, ROUTER_LANE0:ROUTER_LANE0 + N_EXPERTS].astype(F32)
    piece_end = mm(pcs, cum_e)
    piece_loc = piece_end - pcs
    tile_cum = mm(cum_t, pcs)
    rows_e = tile_cum[-1:, :] * ROW_PIECE
    padded = jnp.floor((rows_e + (MOE_TM - 1)) * (1.0 / MOE_TM)) * MOE_TM
    pend = mm(padded, cum_e)
    poff = pend - padded
    piece_glob = poff * (1.0 / ROW_PIECE) + tile_cum - pcs
    nused_f = pend[0, -1] * (1.0 / MOE_TM)
    blk_start = jnp.asarray(b_ids[:, None] * MOE_TM, F32)
    blk_e_f = jnp.minimum(jnp.sum((pend <= blk_start).astype(F32), axis=1, keepdims=True), N_EXPERTS - 1.0)
    blk_oh = (blk_e_f == e_row).astype(F32)
    later_nonempty = (e_ids[None, :] > e_ids[:, None]) & (rows_e > 0)
    next_of_e = jnp.min(jnp.where(later_nonempty, e_row, float(N_EXPERTS)), axis=1)
    next_of_e = jnp.where(next_of_e == N_EXPERTS, -1.0, next_of_e)
    per_blk = mm(blk_oh, jnp.stack([pend[0], (poff + rows_e)[0], next_of_e], axis=1))
    blk_used = blk_start < pend[0, -1]
    zflag = jnp.where(blk_used, (blk_start + MOE_TM == per_blk[:, 0:1]).astype(F32), 2.0)
    rows_in_blk = jnp.clip(per_blk[:, 1:2] - blk_start, 0.0, float(MOE_TM))
    nsub = jnp.floor((rows_in_blk + (EXPERT_SUB - 1)) * (1.0 / EXPERT_SUB))
    prev_e = jnp.concatenate([jnp.full((1, 1), -1.0, F32), blk_e_f[:-1]], axis=0)
    run_first = (blk_used & (blk_e_f != prev_e)).astype(F32)
    run_idx = mm(run_first.reshape(1, n_blocks), cum_b) - 1.0
    run_slot = run_idx - 2.0 * jnp.floor(run_idx * 0.5)
    p_ids = jnp.asarray(np.arange(PIECES_PER_TILE), F32)
    e_of_p = jnp.minimum(jnp.sum((piece_end[:, None, :] <= p_ids[None, :, None]).astype(F32), axis=2),
                         N_EXPERTS - 1.0)
    shift = jnp.sum(jnp.where(e_of_p[:, :, None] == e_row[None], (piece_glob - piece_loc)[:, None, :], 0.0), axis=2)
    glob_of_p = i32(shift + p_ids[None, :]).reshape(n_tiles * PIECES_PER_TILE)
    npieces = i32(piece_end[:, -1])
    flat_b = lambda a: i32(a).reshape(n_blocks)
    blk_e, nused = flat_b(blk_e_f), i32(nused_f).reshape(1)

    buf = _dispatch(xn2, srow, npieces, glob_of_p, flat_b(zflag), n_rows)
    yb = _experts(buf, blk_e, nused, flat_b(nsub), flat_b(run_first), flat_b(per_blk[:, 2:3]), flat_b(run_slot),
                  w_e_gate[0], w_e_up[0], w_e_down[0])
    out = _combine(x2, rti, rtf, g_final[None, :], yb, npieces, glob_of_p)
    return out.reshape(B, S, D)
```

```python
import functools

import numpy as np
import jax
import jax.numpy as jnp
from jax import lax
from jax.experimental import pallas as pl
from jax.experimental.pallas import tpu as pltpu

F32 = jnp.float32
BF16 = jnp.bfloat16

CHUNK = 64
POOL_WINDOWS = (2, 4, 8, 16)
N_HEADS = 4
CONV_K = 4
N_GROUPS = 4
EXPERTS_PER_GROUP = 8
N_EXPERTS = N_GROUPS * EXPERTS_PER_GROUP
TOP_K = 2
EPS = 1e-6

LANES = 128
HALO = 16
ROUTER_LANE0 = N_GROUPS

INPROJ_TN = 256
MIX_TS = 256
MOE_TM = 512
ROW_PIECE = 16
MOE_SL = TOP_K * MIX_TS + N_EXPERTS * ROW_PIECE
PIECES_PER_TILE = MOE_SL // ROW_PIECE
MOE_STEP_TILES = 2
SLOT_SUB = 128
PIECE_UNROLL = 4
EXPERT_SUB = 128
SLOT_RADIX = 16
VMEM_LIMIT = 56 * 1024 * 1024


def _cparams(n_axes):
    return pltpu.CompilerParams(dimension_semantics=("arbitrary",) * n_axes,
                                vmem_limit_bytes=VMEM_LIMIT)


def _sigmoid(v):
    return 0.5 * jnp.tanh(0.5 * v) + 0.5


def _silu(v):
    return v * _sigmoid(v)


def _log_sigmoid(v):
    return jnp.minimum(v, 0.0) - jnp.log1p(jnp.exp(-jnp.abs(v)))


def _split3(v):
    hi = v.astype(BF16)
    r1 = v - hi.astype(F32)
    mid = r1.astype(BF16)
    lo = (r1 - mid.astype(F32)).astype(BF16)
    return hi, mid, lo


def _dot(a, b):
    return jnp.dot(a, b, preferred_element_type=F32)


def _dot_nt(a, b):
    return lax.dot_general(a, b, (((1,), (1,)), ((), ())), preferred_element_type=F32)


def _dot_tn(a, b):
    return lax.dot_general(a, b, (((0,), (0,)), ((), ())), preferred_element_type=F32)


def _inproj_steps(x_ref, g_ref, w_refs, z_refs, xn_ref):
    w_ref_all, wg_ref = w_refs
    zm_ref, zg_ref, zif_ref, zift_ref = z_refs
    n_main = zm_ref.shape[1]

    def norm():
        x = x_ref[...]
        ms = jnp.mean(x * x, axis=-1, keepdims=True)
        xn_ref[...] = (x * lax.rsqrt(ms + EPS) * g_ref[...]).astype(BF16)

    def block(w_ref, z_ref, c0):
        def run():
            cols = slice(c0, c0 + INPROJ_TN)
            z_ref[:, cols] = _dot(xn_ref[...], w_ref[:, cols]).astype(BF16)
        return run

    def gates():
        zif = _dot(xn_ref[...], w_ref_all[:, n_main:n_main + LANES])
        zif_ref[...] = zif
        zift_ref[...] = zif.T[:zift_ref.shape[0], :]

    steps = [norm, gates]
    steps += [block(w_ref_all, zm_ref, c0) for c0 in range(0, n_main, INPROJ_TN)]
    steps += [block(wg_ref, zg_ref, c0) for c0 in range(0, zg_ref.shape[1], INPROJ_TN)]
    return steps


def _mixer_kernel(x_ref, xnext_ref, gmix_ref, win_ref,
                  bif_ref, bift_ref, convq_ref, convk_ref, ghead_ref, wpool32_ref, pscale_ref,
                  wa32_ref, wb32_ref, wo32_ref, gffn_ref, wr_ref, br_ref,
                  tric_ref, trir_ref, stri_ref, ut_ref, sel_ref,
                  x2_ref, xn2_ref, rti_ref, rtf_ref, tstat_ref, srow_ref,
                  zm_ref, zg_ref, zif_ref, zift_ref, zm_nxt, zg_nxt, zif_nxt, zift_nxt, xn_ref,
                  ext_ref, q_ref, k_ref, h_ref, pool_ref, cst_ref, mst_ref, lg_ref, wg_ref,
                  wpool_ref, wa_ref, wb_ref, wo_ref, *, tiles_per_seq):
    ts = x_ref.shape[0]
    d_pool = wa_ref.shape[0]
    d_ml = wb_ref.shape[0]
    dh = d_ml // N_HEADS
    n_chunks = ts // CHUNK
    g_step = pl.program_id(0)
    j = lax.rem(g_step, tiles_per_seq)
    w_in_refs = (win_ref, wg_ref)
    z_cur = (zm_ref, zg_ref, zif_ref, zift_ref)
    z_nxt = (zm_nxt, zg_nxt, zif_nxt, zift_nxt)
    first = g_step == 0

    @pl.when(first)
    def _():
        g0 = zm_ref.shape[1] + 2 * N_HEADS
        wg_ref[...] = win_ref[:, g0:g0 + wg_ref.shape[1]]
        for dst, src in ((wpool_ref, wpool32_ref), (wa_ref, wa32_ref), (wb_ref, wb32_ref), (wo_ref, wo32_ref)):
            dst[...] = src[...].astype(BF16)
        for step in _inproj_steps(x_ref, gmix_ref, w_in_refs, z_cur, xn_ref):
            step()
        lg_ref[...] = jnp.zeros_like(lg_ref)

    @pl.when(jnp.logical_not(first))
    def _():
        for dst, src in zip(z_cur, z_nxt):
            dst[...] = src[...]

    @pl.when(j == 0)
    def _():
        ext_ref[:, :HALO, :] = jnp.zeros((ext_ref.shape[0], HALO, LANES), F32)
        cst_ref[...] = jnp.zeros_like(cst_ref)
        mst_ref[...] = jnp.zeros_like(mst_ref)

    pending = _inproj_steps(xnext_ref, gmix_ref, w_in_refs, z_nxt, xn_ref)

    def project_some(n=1):
        for _ in range(min(n, len(pending))):
            pending.pop(0)()

    project_some(2)

    routed = _route_select(lg_ref[...])

    row = lax.broadcasted_iota(jnp.int32, (ts, LANES), 0)
    pos1 = (row + j * ts + 1).astype(F32)

    def history(cg):
        cur = zm_ref[:, cg * LANES:(cg + 1) * LANES].astype(F32)
        ext_ref[cg, HALO:, :] = cur
        return cur, lambda s: ext_ref[cg, HALO - s:HALO - s + ts, :]

    def keep_history(cg, cur):
        ext_ref[cg, :HALO, :] = cur[ts - HALO:, :]

    n_pool_groups = d_pool // LANES
    for g in range(n_pool_groups):
        w = POOL_WINDOWS[g]
        cur, shifted = history(g)
        win = cur
        for s in range(1, w):
            win = win + shifted(s)
        keep_history(g, cur)
        cnt = jnp.minimum(pos1, float(w))
        d = win / cnt - cur
        y = _dot(d.astype(BF16), wpool_ref[g]) * pscale_ref[:, g * LANES:(g + 1) * LANES]
        pool_ref[:, g * LANES:(g + 1) * LANES] = y.astype(BF16)
        project_some()

    n_ml_groups = d_ml // LANES
    for which, (cw_ref, dst_ref, scale) in enumerate(((convq_ref, q_ref, 1.0), (convk_ref, k_ref, dh ** -0.5))):
        for g in range(n_ml_groups):
            cols = slice(g * LANES, (g + 1) * LANES)
            cg = n_pool_groups + which * n_ml_groups + g
            cur, shifted = history(cg)
            acc = cur * cw_ref[CONV_K - 1:CONV_K, cols]
            for sft in range(1, CONV_K):
                acc = acc + shifted(sft) * cw_ref[CONV_K - 1 - sft:CONV_K - sft, cols]
            keep_history(cg, cur)
            dst_ref[:, cols] = (_silu(acc) * scale).astype(BF16)
        project_some()

    _route_slots(routed, stri_ref, ut_ref, sel_ref, rti_ref, rtf_ref, tstat_ref, srow_ref)
    project_some()

    zc = zif_ref[...] + bif_ref[...]
    lf_c = _log_sigmoid(zc)
    bc = sum(_dot(tric_ref[...], p) for p in _split3(lf_c))
    zr = zift_ref[...] + bift_ref[...]
    lf_r = _log_sigmoid(zr)
    br = sum(_dot(p, trir_ref[...]) for p in _split3(lf_r))
    project_some(2)

    ti = lax.broadcasted_iota(jnp.int32, (CHUNK, CHUNK), 0)
    si = lax.broadcasted_iota(jnp.int32, (CHUNK, CHUNK), 1)
    causal = si <= ti
    ones_blk = jnp.ones((CHUNK, dh), BF16)
    v0 = d_pool + 2 * d_ml
    ig_rep = [jnp.broadcast_to(zc[:, h:h + 1], (ts, dh)) for h in range(N_HEADS)]
    bt_rep = [jnp.broadcast_to(bc[:, N_HEADS + h:N_HEADS + h + 1], (ts, dh)) for h in range(N_HEADS)]

    m_state = [mst_ref[h:h + 1, :] for h in range(N_HEADS)]
    c_state = [cst_ref[h] for h in range(N_HEADS)]
    def stage_scores(c):
        rs = slice(c * CHUNK, (c + 1) * CHUNK)
        out = []
        for h in range(N_HEADS):
            hs = slice(h * dh, (h + 1) * dh)
            q = q_ref[rs, hs]
            k = k_ref[rs, hs]
            bt = bt_rep[h][rs, :]
            r_row = zr[h:h + 1, rs] - br[N_HEADS + h:N_HEADS + h + 1, rs]
            dmat = jnp.where(causal, bt[:, :CHUNK] + r_row, -jnp.inf)
            out.append(dict(q=q, k=k, bt=bt, dmat=dmat, qk=_dot_nt(q, k),
                            m_intra=jnp.max(dmat, axis=-1, keepdims=True)))
        return out

    def stage_state(c, st):
        rs = slice(c * CHUNK, (c + 1) * CHUNK)
        for h in range(N_HEADS):
            s = st[h]
            bt, k = s["bt"], s["k"]
            m_prev, c_prev = m_state[h], c_state[h]
            v_aug = jnp.concatenate([zm_ref[rs, v0 + h * dh:v0 + (h + 1) * dh], ones_blk], axis=-1)
            igc = ig_rep[h][rs, :]
            b_last = bt[CHUNK - 1:CHUNK, :]
            a_log = b_last - bt + igc
            a_max = jnp.max(a_log, axis=0, keepdims=True)
            m_new = jnp.maximum(b_last + m_prev, a_max)
            kw = (k.astype(F32) * jnp.exp(a_log - m_new)).astype(BF16)
            decay = jnp.exp(b_last + m_prev - m_new)
            s.update(v_aug=v_aug, m_prev=m_prev, qc=_dot(s["q"], c_prev.astype(BF16)))
            c_state[h] = jnp.concatenate([decay, decay], axis=-1) * c_prev + _dot_tn(kw, v_aug)
            m_state[h] = m_new

    def stage_values(c, st):
        rs = slice(c * CHUNK, (c + 1) * CHUNK)
        for h in range(N_HEADS):
            s = st[h]
            hs = slice(h * dh, (h + 1) * dh)
            inter = s["bt"] + s["m_prev"]
            m_t = jnp.maximum(inter, s["m_intra"])
            w_inter = jnp.exp(inter - m_t)
            smat = s["qk"] * jnp.exp(s["dmat"] - m_t[:, :CHUNK])
            sv = _dot(smat.astype(BF16), s["v_aug"])
            qc = s["qc"]
            nq = w_inter * qc[:, dh:] + sv[:, dh:]
            den = jnp.maximum(jnp.abs(nq), jnp.exp(-m_t))
            h_ref[rs, hs] = (w_inter * qc[:, :dh] + sv[:, :dh]) / den

    staged = stage_scores(0)
    for c in range(n_chunks):
        stage_state(c, staged)
        project_some()
        nxt = stage_scores(c + 1) if c + 1 < n_chunks else None
        project_some()
        stage_values(c, staged)
        staged = nxt
    for h in range(N_HEADS):
        cst_ref[h] = c_state[h]
        mst_ref[h:h + 1, :] = m_state[h]

    o0 = v0 + d_ml
    for h in range(N_HEADS):
        hs = slice(h * dh, (h + 1) * dh)
        hv = h_ref[:, hs]
        mu = jnp.mean(hv, axis=-1, keepdims=True)
        hc = hv - mu
        var = jnp.mean(hc * hc, axis=-1, keepdims=True)
        hn = hc * lax.rsqrt(var + EPS) * ghead_ref[:, hs]
        og = _sigmoid(zm_ref[:, o0 + h * dh:o0 + (h + 1) * dh].astype(F32))
        q_ref[:, hs] = (og * hn).astype(BF16)
    y_a = _dot(pool_ref[...], wa_ref[...])
    y_b = _dot(q_ref[...], wb_ref[...])
    d_model = x_ref.shape[1]
    ga = _sigmoid(zg_ref[:, :d_model].astype(F32))
    gb = _sigmoid(zg_ref[:, d_model:].astype(F32))
    merged = (ga * y_a + gb * y_b).astype(BF16)
    x2 = x_ref[...] + _dot(merged, wo_ref[...])
    x2_ref[...] = x2
    project_some(len(pending))

    ms = jnp.mean(x2 * x2, axis=-1, keepdims=True)
    xn2 = x2 * lax.rsqrt(ms + EPS) * gffn_ref[...]
    xh = xn2.astype(BF16)
    xn2_ref[...] = xh
    lg_ref[...] = _dot(xh, wr_ref[...]) + br_ref[...]
    project_some(len(pending))


def _route_select(lg):
    ts = lg.shape[0]
    lane = lax.broadcasted_iota(jnp.int32, (ts, LANES), 1)
    lanef = lane.astype(F32)
    big = float(4 * LANES)
    gl = jnp.where(lane < N_GROUPS, lg, -jnp.inf)
    gmax = jnp.max(gl, axis=-1, keepdims=True)
    g_sel = jnp.min(jnp.where(gl == gmax, lanef, big), axis=-1, keepdims=True)
    p_g = 1.0 / jnp.sum(jnp.exp(gl - gmax), axis=-1, keepdims=True)
    lo = ROUTER_LANE0 + EXPERTS_PER_GROUP * g_sel
    el = jnp.where((lanef >= lo) & (lanef < lo + EXPERTS_PER_GROUP), lg, -jnp.inf)
    m1 = jnp.max(el, axis=-1, keepdims=True)
    i1 = jnp.min(jnp.where(el == m1, lanef, big), axis=-1, keepdims=True)
    el2 = jnp.where(lanef == i1, -jnp.inf, el)
    m2 = jnp.max(el2, axis=-1, keepdims=True)
    i2 = jnp.min(jnp.where(el2 == m2, lanef, big), axis=-1, keepdims=True)
    e2x = jnp.exp(m2 - m1)
    gate1 = p_g / (1.0 + e2x)
    gate2 = p_g * e2x / (1.0 + e2x)
    return dict(lane=lane, i1=i1, i2=i2, gate1=gate1, gate2=gate2, oh1=lanef == i1, oh2=lanef == i2)


def _route_slots(r, stri_ref, ut_ref, sel_ref, rti_ref, rtf_ref, tstat_ref, srow_ref):
    lane, oh1, oh2, i1, i2 = r["lane"], r["oh1"], r["oh2"], r["i1"], r["i2"]
    ohs = jnp.where(oh1 | oh2, 1.0, 0.0)
    n_loc = jnp.sum(ohs, axis=0, keepdims=True)
    pieces = jnp.floor((n_loc + (ROW_PIECE - 1.0)) * (1.0 / ROW_PIECE))
    piece_off = _dot(jnp.broadcast_to(pieces, (8, LANES)).astype(BF16), ut_ref[...])[0:1, :]
    base = _dot(stri_ref[...], ohs.astype(BF16)) + ROW_PIECE * piece_off
    slot1 = jnp.sum(jnp.where(oh1, base, 0.0), axis=-1, keepdims=True)
    slot2 = jnp.sum(jnp.where(oh2, base, 0.0), axis=-1, keepdims=True)
    tstat_ref[...] = jnp.broadcast_to(pieces, tstat_ref.shape).astype(jnp.int32)

    rti = jnp.where(lane == 0, i1 - ROUTER_LANE0,
                    jnp.where(lane == 1, i2 - ROUTER_LANE0,
                              jnp.where(lane == 2, slot1, jnp.where(lane == 3, slot2, 0.0))))
    rti_ref[...] = rti.astype(jnp.int32)
    rtf_ref[...] = jnp.where(lane == 0, r["gate1"], jnp.where(lane == 1, r["gate2"], 0.0))
    h1 = jnp.floor(slot1 * (1.0 / SLOT_RADIX))
    h2 = jnp.floor(slot2 * (1.0 / SLOT_RADIX))
    parts = jnp.where(lane == 0, h1, jnp.where(lane == 1, slot1 - SLOT_RADIX * h1,
                      jnp.where(lane == 2, h2, jnp.where(lane == 3, slot2 - SLOT_RADIX * h2, 0.0))))
    srow_ref[...] = _dot_nt(sel_ref[...], parts.astype(BF16))


def _mixer(x2d, g_mix, w_all, params, batch, seq):
    T, D = x2d.shape
    ts = min(MIX_TS, seq)
    nts = seq // ts
    d_pool = params["w_br_a"].shape[0]
    d_ml = params["w_br_b"].shape[0]
    dh = d_ml // N_HEADS

    idx = np.arange(ts)
    same_chunk = (idx[:, None] // CHUNK) == (idx[None, :] // CHUNK)
    tri_c = jnp.asarray((idx[None, :] <= idx[:, None]) & same_chunk, BF16)
    tri_r = jnp.asarray((idx[:, None] <= idx[None, :]) & same_chunk, BF16)
    stri = jnp.asarray(idx[None, :] < idx[:, None], BF16)
    lane_idx = np.arange(LANES)
    ut = jnp.asarray(lane_idx[:, None] < lane_idx[None, :], BF16)
    sel = jnp.asarray(np.arange(8)[:, None] == lane_idx[None, :], BF16)

    n_tiles = batch * nts
    tok = lambda g: (g, 0)
    tok_in = lambda g: (jnp.minimum(g, n_tiles - 1), 0)
    tok_next = lambda g: (jnp.minimum(g + 1, n_tiles - 1), 0)
    tok_prev = lambda g: (jnp.maximum(g - 1, 0), 0)
    tok_prev_t = lambda g: (0, jnp.maximum(g - 1, 0))
    c2 = lambda g: (0, 0)
    c3 = lambda g: (0, 0, 0)
    full = lambda a: pl.BlockSpec(a.shape, c2 if a.ndim == 2 else c3)
    consts = [params[n] for n in ("b_if", "b_if_t", "conv_q", "conv_k", "g_head", "w_pool", "pool_scale",
                                  "w_br_a", "w_br_b", "w_out", "g_ffn", "w_r", "b_r")]
    consts = [g_mix, w_all] + consts + [tri_c, tri_r, stri, ut, sel]
    nm, ng = d_pool + 4 * d_ml, 2 * D
    z_scratch = [pltpu.VMEM((ts, nm), BF16), pltpu.VMEM((ts, ng), BF16),
                 pltpu.VMEM((ts, LANES), F32), pltpu.VMEM((16, ts), F32)]
    return pl.pallas_call(
        functools.partial(_mixer_kernel, tiles_per_seq=nts),
        grid=(n_tiles + 1,),
        in_specs=[pl.BlockSpec((ts, D), tok_in),
                  pl.BlockSpec((ts, D), tok_next)] + [full(a) for a in consts],
        out_specs=[pl.BlockSpec((ts, D), tok),
                   pl.BlockSpec((ts, D), tok),
                   pl.BlockSpec((ts, LANES), tok_prev),
                   pl.BlockSpec((ts, LANES), tok_prev),
                   pl.BlockSpec((8, LANES), tok_prev),
                   pl.BlockSpec((8, ts), tok_prev_t)],
        out_shape=[jax.ShapeDtypeStruct((T + ts, D), F32),
                   jax.ShapeDtypeStruct((T + ts, D), BF16),
                   jax.ShapeDtypeStruct((T, LANES), jnp.int32),
                   jax.ShapeDtypeStruct((T, LANES), F32),
                   jax.ShapeDtypeStruct((n_tiles * 8, LANES), jnp.int32),
                   jax.ShapeDtypeStruct((8, T), F32)],
        scratch_shapes=z_scratch + z_scratch + [
                        pltpu.VMEM((ts, D), BF16),
                        pltpu.VMEM(((d_pool + 2 * d_ml) // LANES, HALO + ts, LANES), F32),
                        pltpu.VMEM((ts, d_ml), BF16),
                        pltpu.VMEM((ts, d_ml), BF16),
                        pltpu.VMEM((ts, d_ml), F32),
                        pltpu.VMEM((ts, d_pool), BF16),
                        pltpu.VMEM((N_HEADS, dh, 2 * dh), F32),
                        pltpu.VMEM((8, LANES), F32),
                        pltpu.VMEM((ts, LANES), F32),
                        pltpu.VMEM((D, ng), BF16),
                        pltpu.VMEM(params["w_pool"].shape, BF16),
                        pltpu.VMEM((d_pool, D), BF16),
                        pltpu.VMEM((d_ml, D), BF16),
                        pltpu.VMEM((D, D), BF16)],
        compiler_params=_cparams(1),
        name="mixer",
    )(x2d, x2d, *consts)


def _for_each_piece(npieces_ref, glob_ref, tile, fn):
    base = tile * PIECES_PER_TILE
    n = npieces_ref[tile]

    def one(p):
        fn(pl.multiple_of(p * ROW_PIECE, ROW_PIECE), pl.multiple_of(glob_ref[base + p] * ROW_PIECE, ROW_PIECE))

    def group(g, carry):
        for u in range(PIECE_UNROLL):
            one(g * PIECE_UNROLL + u)
        return carry

    n_groups = lax.div(n, jnp.int32(PIECE_UNROLL))
    lax.fori_loop(0, n_groups, group, 0)
    for u in range(PIECE_UNROLL - 1):
        @pl.when(n_groups * PIECE_UNROLL + u < n)
        def _():
            one(n_groups * PIECE_UNROLL + u)


def _used_slot_groups(n_pieces):
    return lax.div(n_pieces * ROW_PIECE + (SLOT_SUB - 1), jnp.int32(SLOT_SUB))


def _dispatch_kernel(npieces_ref, glob_ref, pad_start_ref, pad_rows_ref, nused_ref,
                     xn_ref, srow_ref, buf_ref, rows_ref, zeros_ref, sem, zsem, tsem, *, n_steps):
    tt = xn_ref.shape[0] // MOE_STEP_TILES
    sl = rows_ref.shape[1] // MOE_STEP_TILES
    n_blocks = buf_ref.shape[0] // MOE_TM
    i = pl.program_id(0)
    cur = lax.rem(i, 2)

    def zero_fill_share(step, act):
        for k in range(-(-N_EXPERTS // n_steps)):
            e = step + k * n_steps

            @pl.when(e < N_EXPERTS)
            def _(e=e):
                e_c = jnp.minimum(e, N_EXPERTS - 1)
                rows = pad_rows_ref[e_c]
                off = pad_start_ref[e_c]
                size = MOE_TM // 2
                while size >= ROW_PIECE:
                    has = (rows & size) != 0

                    @pl.when(has)
                    def _(off=off, size=size):
                        act(pltpu.make_async_copy(zeros_ref.at[pl.ds(0, size)],
                                                  buf_ref.at[pl.ds(pl.multiple_of(off, ROW_PIECE), size)], zsem))
                    off = off + jnp.where(has, size, 0)
                    size //= 2
        for k in range(-(-n_blocks // n_steps)):
            b = step + k * n_steps

            @pl.when((b >= nused_ref[0]) & (b < n_blocks))
            def _(b=b):
                act(pltpu.make_async_copy(zeros_ref, buf_ref.at[pl.ds(b * MOE_TM, MOE_TM)], tsem))

    @pl.when(i == 0)
    def _():
        zeros_ref[...] = jnp.zeros_like(zeros_ref)

    zero_fill_share(i, lambda cp: cp.start())

    def piece_copy(buf_slot, sub, local_row, global_row):
        return pltpu.make_async_copy(rows_ref.at[buf_slot, pl.ds(sub * sl + local_row, ROW_PIECE)],
                                     buf_ref.at[pl.ds(global_row, ROW_PIECE)], sem.at[buf_slot])

    def for_step_pieces(step, buf_slot, act):
        for sub in range(MOE_STEP_TILES):
            _for_each_piece(npieces_ref, glob_ref, step * MOE_STEP_TILES + sub,
                            lambda l, g, sub=sub: act(piece_copy(buf_slot, sub, l, g)))

    @pl.when(i >= 2)
    def _():
        for_step_pieces(i - 2, cur, lambda cp: cp.wait())

    for sub in range(MOE_STEP_TILES):
        sr = srow_ref[:, sub * tt:(sub + 1) * tt]
        slot1 = SLOT_RADIX * sr[0:1, :] + sr[1:2, :]
        slot2 = SLOT_RADIX * sr[2:3, :] + sr[3:4, :]
        n_sub = _used_slot_groups(npieces_ref[i * MOE_STEP_TILES + sub])
        for k in range(TOP_K * tt // SLOT_SUB, sl // SLOT_SUB + 1):
            @pl.when(n_sub == k)
            def _(k=k, sub=sub, slot1=slot1, slot2=slot2):
                m = k * SLOT_SUB
                r = lax.broadcasted_iota(jnp.int32, (m, tt), 0).astype(F32)
                sel = jnp.where((r == slot1) | (r == slot2), 1.0, 0.0).astype(BF16)
                rows_ref[cur, sub * sl:sub * sl + m, :] = _dot(sel, xn_ref[sub * tt:(sub + 1) * tt, :]).astype(BF16)
    for_step_pieces(i, cur, lambda cp: cp.start())

    @pl.when(i == n_steps - 1)
    def _():
        @pl.when(i >= 1)
        def _():
            for_step_pieces(i - 1, 1 - cur, lambda cp: cp.wait())
        for_step_pieces(i, cur, lambda cp: cp.wait())

        def wait_share(step, carry):
            zero_fill_share(step, lambda cp: cp.wait())
            return carry

        lax.fori_loop(0, n_steps, wait_share, 0)


def _dispatch(xn2, srow, npieces, piece_glob, pad_start, pad_rows, nused, n_rows):
    T, D = srow.shape[1], xn2.shape[1]
    tt = MOE_STEP_TILES * MIX_TS
    return pl.pallas_call(
        functools.partial(_dispatch_kernel, n_steps=T // tt),
        grid_spec=pltpu.PrefetchScalarGridSpec(
            num_scalar_prefetch=5,
            grid=(T // tt,),
            in_specs=[pl.BlockSpec((tt, D), lambda i, *_: (i, 0)),
                      pl.BlockSpec((8, tt), lambda i, *_: (0, i))],
            out_specs=pl.BlockSpec(memory_space=pl.ANY),
            scratch_shapes=[pltpu.VMEM((2, MOE_STEP_TILES * MOE_SL, D), BF16),
                            pltpu.VMEM((MOE_TM, D), BF16),
                            pltpu.SemaphoreType.DMA((2,)),
                            pltpu.SemaphoreType.DMA(()),
                            pltpu.SemaphoreType.DMA(())]),
        out_shape=jax.ShapeDtypeStruct((n_rows, D), BF16),
        compiler_params=_cparams(1),
        name="dispatch",
    )(npieces, piece_glob, pad_start, pad_rows, nused, xn2, srow)


def _experts_kernel(blk_e_ref, nused_ref, nsub_ref, first_ref, next_e_ref, slot_ref,
                    x_ref, wg_hbm, wu_hbm, wd_hbm, y_ref,
                    wg32_ref, wu32_ref, wd32_ref, wgb_ref, wub_ref, wdb_ref, sem):
    i = pl.program_id(0)
    used = i < nused_ref[0]
    n_sub = nsub_ref[i]
    landing = ((wg_hbm, wg32_ref), (wu_hbm, wu32_ref), (wd_hbm, wd32_ref))

    def weight_copies(e, s):
        return [pltpu.make_async_copy(hbm.at[e], vmem.at[s], sem.at[s, n]) for n, (hbm, vmem) in enumerate(landing)]

    @pl.when(used & (i == 0))
    def _():
        for cp in weight_copies(blk_e_ref[0], 0):
            cp.start()

    run_start = used & (first_ref[i] > 0)
    s = slot_ref[i]

    @pl.when(run_start)
    def _():
        for cp in weight_copies(blk_e_ref[i], s):
            cp.wait()

        @pl.when(next_e_ref[i] >= 0)
        def _():
            for cp in weight_copies(next_e_ref[i], 1 - s):
                cp.start()

    def swiglu(m, wg, wu, wd):
        x = x_ref[:m, :]
        hg = _dot(x, wg)
        hu = _dot(x, wu)
        hid = (_silu(hg) * hu).astype(BF16)
        y_ref[:m, :] = _dot(hid, wd).astype(BF16)
        if m < MOE_TM:
            y_ref[m:, :] = jnp.zeros((MOE_TM - m, y_ref.shape[1]), BF16)

    for k in range(1, MOE_TM // EXPERT_SUB + 1):
        @pl.when(run_start & (n_sub == k))
        def _(k=k):
            wg = wg32_ref[s].astype(BF16)
            wu = wu32_ref[s].astype(BF16)
            wd = wd32_ref[s].astype(BF16)
            wgb_ref[...] = wg
            wub_ref[...] = wu
            wdb_ref[...] = wd
            swiglu(k * EXPERT_SUB, wg, wu, wd)

        @pl.when(used & jnp.logical_not(run_start) & (n_sub == k))
        def _(k=k):
            swiglu(k * EXPERT_SUB, wgb_ref[...], wub_ref[...], wdb_ref[...])


def _experts(buf, blk_e, nused, nsub, run_first, next_e, run_slot, w_gate, w_up, w_down):
    R, D = buf.shape
    de = w_gate.shape[2]
    n_blocks = R // MOE_TM
    row_map = lambda i, be, nu, *_: (jnp.minimum(i, nu[0] - 1), 0)
    return pl.pallas_call(
        _experts_kernel,
        grid_spec=pltpu.PrefetchScalarGridSpec(
            num_scalar_prefetch=6,
            grid=(n_blocks,),
            in_specs=[pl.BlockSpec((MOE_TM, D), row_map),
                      pl.BlockSpec(memory_space=pl.ANY),
                      pl.BlockSpec(memory_space=pl.ANY),
                      pl.BlockSpec(memory_space=pl.ANY)],
            out_specs=pl.BlockSpec((MOE_TM, D), row_map),
            scratch_shapes=[pltpu.VMEM((2, D, de), F32), pltpu.VMEM((2, D, de), F32), pltpu.VMEM((2, de, D), F32),
                            pltpu.VMEM((D, de), BF16), pltpu.VMEM((D, de), BF16), pltpu.VMEM((de, D), BF16),
                            pltpu.SemaphoreType.DMA((2, 3))]),
        out_shape=jax.ShapeDtypeStruct((R, D), BF16),
        input_output_aliases={6: 0},
        compiler_params=_cparams(1),
        name="experts",
    )(blk_e, nused, nsub, run_first, next_e, run_slot, buf, w_gate, w_up, w_down)


def _combine_kernel(npieces_ref, glob_ref, x2_ref, rti_ref, rtf_ref, gfin_ref, yb_ref, out_ref,
                    rows_ref, sem):
    tt = x2_ref.shape[0] // MOE_STEP_TILES
    sl = rows_ref.shape[1] // MOE_STEP_TILES
    i = pl.program_id(0)
    n_steps = pl.num_programs(0)
    cur = lax.rem(i, 2)

    def piece_copy(buf_slot, sub, local_row, global_row):
        return pltpu.make_async_copy(yb_ref.at[pl.ds(global_row, ROW_PIECE)],
                                     rows_ref.at[buf_slot, pl.ds(sub * sl + local_row, ROW_PIECE)], sem.at[buf_slot])

    def for_step_pieces(step, buf_slot, act):
        for sub in range(MOE_STEP_TILES):
            _for_each_piece(npieces_ref, glob_ref, step * MOE_STEP_TILES + sub,
                            lambda l, g, sub=sub: act(piece_copy(buf_slot, sub, l, g)))

    @pl.when(i == 0)
    def _():
        rows_ref[...] = jnp.zeros_like(rows_ref)
        for_step_pieces(0, 0, lambda cp: cp.start())

    @pl.when(i + 1 < n_steps)
    def _():
        for_step_pieces(i + 1, 1 - cur, lambda cp: cp.start())

    for_step_pieces(i, cur, lambda cp: cp.wait())

    for sub in range(MOE_STEP_TILES):
        ts_rows = slice(sub * tt, (sub + 1) * tt)
        rti = rti_ref[ts_rows, :]
        rtf = rtf_ref[ts_rows, :]
        slot1 = rti[:, 2:3]
        slot2 = rti[:, 3:4]
        n_sub = _used_slot_groups(npieces_ref[i * MOE_STEP_TILES + sub])
        for k in range(TOP_K * tt // SLOT_SUB, sl // SLOT_SUB + 1):
            @pl.when(n_sub == k)
            def _(k=k, sub=sub, ts_rows=ts_rows, rtf=rtf, slot1=slot1, slot2=slot2):
                m = k * SLOT_SUB
                lane = lax.broadcasted_iota(jnp.int32, (tt, m), 1)
                g = jnp.where(lane == slot1, rtf[:, 0:1], jnp.where(lane == slot2, rtf[:, 1:2], 0.0)).astype(BF16)
                y = x2_ref[ts_rows, :] + _dot(g, rows_ref[cur, sub * sl:sub * sl + m, :])
                ms = jnp.mean(y * y, axis=-1, keepdims=True)
                out_ref[ts_rows, :] = y * lax.rsqrt(ms + EPS) * gfin_ref[...]


def _combine(x2, rti, rtf, g_final, yb, npieces, piece_glob):
    T, D = rti.shape[0], x2.shape[1]
    tt = MOE_STEP_TILES * MIX_TS
    tok = lambda i, *_: (i, 0)
    return pl.pallas_call(
        _combine_kernel,
        grid_spec=pltpu.PrefetchScalarGridSpec(
            num_scalar_prefetch=2,
            grid=(T // tt,),
            in_specs=[pl.BlockSpec((tt, D), tok),
                      pl.BlockSpec((tt, LANES), tok),
                      pl.BlockSpec((tt, LANES), tok),
                      pl.BlockSpec((1, D), lambda i, *_: (0, 0)),
                      pl.BlockSpec(memory_space=pl.ANY)],
            out_specs=pl.BlockSpec((tt, D), tok),
            scratch_shapes=[pltpu.VMEM((2, MOE_STEP_TILES * MOE_SL, D), BF16),
                            pltpu.SemaphoreType.DMA((2,))]),
        out_shape=jax.ShapeDtypeStruct((T, D), F32),
        compiler_params=_cparams(1),
        name="combine",
    )(npieces, piece_glob, x2, rti, rtf, g_final, yb)


def _pad_lanes(a, width=LANES):
    return jnp.pad(a, ((0, 0), (0, width - a.shape[1])))


def kernel(x, g_mix, w_in, b_if, conv_q, conv_k, g_head, w_pool, pool_scale, w_br_a, w_br_b, w_out,
           g_ffn, w_rg, b_rg, w_re, b_re, w_e_gate, w_e_up, w_e_down, g_final):
    B, S, D = x.shape
    T = B * S
    assert g_mix.shape[0] == 1, "single-layer block"
    assert S % MIX_TS == 0 and (T // MIX_TS) % MOE_STEP_TILES == 0
    d_pool = w_br_a.shape[1]
    d_ml = w_br_b.shape[1]
    x2d = x.reshape(T, D)

    params = {
        "b_if": _pad_lanes(b_if[0][None, :]),
        "b_if_t": jnp.pad(b_if[0][:, None], ((0, 16 - 2 * N_HEADS), (0, 0))),
        "conv_q": conv_q[0], "conv_k": conv_k[0],
        "g_head": g_head[0][None, :],
        "w_pool": w_pool[0],
        "pool_scale": pool_scale[0][None, :],
        "w_br_a": w_br_a[0], "w_br_b": w_br_b[0],
        "w_out": w_out[0],
        "g_ffn": g_ffn[0][None, :],
        "w_r": _pad_lanes(jnp.concatenate([w_rg[0], w_re[0]], axis=1)).astype(BF16),
        "b_r": _pad_lanes(jnp.concatenate([b_rg[0], b_re[0]])[None, :]),
    }

    x2, xn2, rti, rtf, tstat, srow = _mixer(x2d, g_mix[0][None, :], w_in[0].astype(BF16), params, B, S)

    n_tiles = T // MIX_TS
    n_rows = n_tiles * MOE_SL + N_EXPERTS * MOE_TM
    n_blocks = n_rows // MOE_TM
    i32 = lambda a: a.astype(jnp.int32)
    mm = lambda a, b: jnp.round(jnp.dot(a, b, precision=lax.Precision.HIGHEST, preferred_element_type=F32))
    e_ids = np.arange(N_EXPERTS)
    t_ids = np.arange(n_tiles)
    b_ids = np.arange(n_blocks)
    cum_e = jnp.asarray(e_ids[:, None] <= e_ids[None, :], F32)
    cum_t = jnp.asarray(t_ids[:, None] >= t_ids[None, :], F32)
    cum_b = jnp.asarray(b_ids[:, None] <= b_ids[None, :], F32)
    e_row = jnp.asarray(e_ids[None, :], F32)

    pcs = tstat.reshape(n_tiles, 8, LANES)[:, 0, ROUTER_LANE0:ROUTER_LANE0 + N_EXPERTS].astype(F32)
    piece_end = mm(pcs, cum_e)
    piece_loc = piece_end - pcs
    tile_cum = mm(cum_t, pcs)
    rows_e = tile_cum[-1:, :] * ROW_PIECE
    padded = jnp.floor((rows_e + (MOE_TM - 1)) * (1.0 / MOE_TM)) * MOE_TM
    pend = mm(padded, cum_e)
    poff = pend - padded
    piece_glob = poff * (1.0 / ROW_PIECE) + tile_cum - pcs
    nused_f = pend[0, -1] * (1.0 / MOE_TM)
    blk_start = jnp.asarray(b_ids[:, None] * MOE_TM, F32)
    blk_e_f = jnp.minimum(jnp.sum((pend <= blk_start).astype(F32), axis=1, keepdims=True), N_EXPERTS - 1.0)
    blk_oh = (blk_e_f == e_row).astype(F32)
    later_nonempty = (e_ids[None, :] > e_ids[:, None]) & (rows_e > 0)
    next_of_e = jnp.min(jnp.where(later_nonempty, e_row, float(N_EXPERTS)), axis=1)
    next_of_e = jnp.where(next_of_e == N_EXPERTS, -1.0, next_of_e)
    per_blk = mm(blk_oh, jnp.stack([pend[0], (poff + rows_e)[0], next_of_e], axis=1))
    blk_used = blk_start < pend[0, -1]
    rows_in_blk = jnp.clip(per_blk[:, 1:2] - blk_start, 0.0, float(MOE_TM))
    nsub = jnp.floor((rows_in_blk + (EXPERT_SUB - 1)) * (1.0 / EXPERT_SUB))
    prev_e = jnp.concatenate([jnp.full((1, 1), -1.0, F32), blk_e_f[:-1]], axis=0)
    run_first = (blk_used & (blk_e_f != prev_e)).astype(F32)
    run_idx = mm(run_first.reshape(1, n_blocks), cum_b) - 1.0
    run_slot = run_idx - 2.0 * jnp.floor(run_idx * 0.5)
    p_ids = jnp.asarray(np.arange(PIECES_PER_TILE), F32)
    e_of_p = jnp.minimum(jnp.sum((piece_end[:, None, :] <= p_ids[None, :, None]).astype(F32), axis=2),
                         N_EXPERTS - 1.0)
    shift = jnp.sum(jnp.where(e_of_p[:, :, None] == e_row[None], (piece_glob - piece_loc)[:, None, :], 0.0), axis=2)
    glob_of_p = i32(shift + p_ids[None, :]).reshape(n_tiles * PIECES_PER_TILE)
    npieces = i32(piece_end[:, -1])
    flat_b = lambda a: i32(a).reshape(n_blocks)
    blk_e, nused = flat_b(blk_e_f), i32(nused_f).reshape(1)

    flat_e = lambda a: i32(a).reshape(N_EXPERTS)
    buf = _dispatch(xn2, srow, npieces, glob_of_p, flat_e(poff + rows_e), flat_e(padded - rows_e), nused, n_rows)
    yb = _experts(buf, blk_e, nused, flat_b(nsub), flat_b(run_first), flat_b(per_blk[:, 2:3]), flat_b(run_slot),
                  w_e_gate[0], w_e_up[0], w_e_down[0])
    out = _combine(x2, rti, rtf, g_final[None, :], yb, npieces, glob_of_p)
    return out.reshape(B, S, D)
```

```python
import functools

import numpy as np
import jax
import jax.numpy as jnp
from jax import lax
from jax.experimental import pallas as pl
from jax.experimental.pallas import tpu as pltpu

F32 = jnp.float32
BF16 = jnp.bfloat16

CHUNK = 64
POOL_WINDOWS = (2, 4, 8, 16)
N_HEADS = 4
CONV_K = 4
N_GROUPS = 4
EXPERTS_PER_GROUP = 8
N_EXPERTS = N_GROUPS * EXPERTS_PER_GROUP
TOP_K = 2
EPS = 1e-6

LANES = 128
HALO = 16
ROUTER_LANE0 = N_GROUPS

INPROJ_TN = 256
MIX_TS = 256
MOE_TM = 512
ROW_PIECE = 16
MOE_SL = TOP_K * MIX_TS + N_EXPERTS * ROW_PIECE
PIECES_PER_TILE = MOE_SL // ROW_PIECE
MOE_STEP_TILES = 2
SLOT_SUB = 128
PIECE_UNROLL = 4
EXPERT_SUB = 128
SLOT_RADIX = 16
VMEM_LIMIT = 56 * 1024 * 1024


def _cparams(n_axes):
    return pltpu.CompilerParams(dimension_semantics=("arbitrary",) * n_axes,
                                vmem_limit_bytes=VMEM_LIMIT)


def _sigmoid(v):
    return 0.5 * jnp.tanh(0.5 * v) + 0.5


def _silu(v):
    return v * _sigmoid(v)


def _log_sigmoid(v):
    return jnp.minimum(v, 0.0) - jnp.log1p(jnp.exp(-jnp.abs(v)))


def _split3(v):
    hi = v.astype(BF16)
    r1 = v - hi.astype(F32)
    mid = r1.astype(BF16)
    lo = (r1 - mid.astype(F32)).astype(BF16)
    return hi, mid, lo


def _dot(a, b):
    return jnp.dot(a, b, preferred_element_type=F32)


def _dot_nt(a, b):
    return lax.dot_general(a, b, (((1,), (1,)), ((), ())), preferred_element_type=F32)


def _dot_tn(a, b):
    return lax.dot_general(a, b, (((0,), (0,)), ((), ())), preferred_element_type=F32)


def _inproj_steps(x_ref, g_ref, w_refs, z_refs, xn_ref):
    w_ref_all, wg_ref = w_refs
    zm_ref, zg_ref, zif_ref, zift_ref = z_refs
    n_main = zm_ref.shape[1]

    def norm():
        x = x_ref[...]
        ms = jnp.mean(x * x, axis=-1, keepdims=True)
        xn_ref[...] = (x * lax.rsqrt(ms + EPS) * g_ref[...]).astype(BF16)

    def block(w_ref, z_ref, c0):
        def run():
            cols = slice(c0, c0 + INPROJ_TN)
            z_ref[:, cols] = _dot(xn_ref[...], w_ref[:, cols]).astype(BF16)
        return run

    def gates():
        zif = _dot(xn_ref[...], w_ref_all[:, n_main:n_main + LANES])
        zif_ref[...] = zif
        zift_ref[...] = zif.T[:zift_ref.shape[0], :]

    steps = [norm, gates]
    steps += [block(w_ref_all, zm_ref, c0) for c0 in range(0, n_main, INPROJ_TN)]
    steps += [block(wg_ref, zg_ref, c0) for c0 in range(0, zg_ref.shape[1], INPROJ_TN)]
    return steps


def _mixer_kernel(x_ref, xnext_ref, gmix_ref, win_ref,
                  bif_ref, bift_ref, convq_ref, convk_ref, ghead_ref, wpool32_ref, pscale_ref,
                  wa32_ref, wb32_ref, wo32_ref, gffn_ref, wr_ref, br_ref,
                  tric_ref, trir_ref, stri_ref, ut_ref, sel_ref,
                  x2_ref, xn2_ref, rti_ref, rtf_ref, tstat_ref, srow_ref,
                  zm_ref, zg_ref, zif_ref, zift_ref, zm_nxt, zg_nxt, zif_nxt, zift_nxt, xn_ref,
                  ext_ref, q_ref, k_ref, h_ref, pool_ref, cst_ref, mst_ref, lg_ref, wg_ref,
                  wpool_ref, wa_ref, wb_ref, wo_ref, *, tiles_per_seq):
    ts = x_ref.shape[0]
    d_pool = wa_ref.shape[0]
    d_ml = wb_ref.shape[0]
    dh = d_ml // N_HEADS
    n_chunks = ts // CHUNK
    g_step = pl.program_id(0)
    j = lax.rem(g_step, tiles_per_seq)
    w_in_refs = (win_ref, wg_ref)
    z_cur = (zm_ref, zg_ref, zif_ref, zift_ref)
    z_nxt = (zm_nxt, zg_nxt, zif_nxt, zift_nxt)
    first = g_step == 0

    @pl.when(first)
    def _():
        g0 = zm_ref.shape[1] + 2 * N_HEADS
        wg_ref[...] = win_ref[:, g0:g0 + wg_ref.shape[1]]
        for dst, src in ((wpool_ref, wpool32_ref), (wa_ref, wa32_ref), (wb_ref, wb32_ref), (wo_ref, wo32_ref)):
            dst[...] = src[...].astype(BF16)
        for step in _inproj_steps(x_ref, gmix_ref, w_in_refs, z_cur, xn_ref):
            step()
        lg_ref[...] = jnp.zeros_like(lg_ref)

    @pl.when(jnp.logical_not(first))
    def _():
        for dst, src in zip(z_cur, z_nxt):
            dst[...] = src[...]

    @pl.when(j == 0)
    def _():
        ext_ref[:, :HALO, :] = jnp.zeros((ext_ref.shape[0], HALO, LANES), F32)
        cst_ref[...] = jnp.zeros_like(cst_ref)
        mst_ref[...] = jnp.zeros_like(mst_ref)

    pending = _inproj_steps(xnext_ref, gmix_ref, w_in_refs, z_nxt, xn_ref)

    def project_some(n=1):
        for _ in range(min(n, len(pending))):
            pending.pop(0)()

    project_some(2)

    routed = _route_select(lg_ref[...])

    row = lax.broadcasted_iota(jnp.int32, (ts, LANES), 0)
    pos1 = (row + j * ts + 1).astype(F32)

    def history(cg):
        cur = zm_ref[:, cg * LANES:(cg + 1) * LANES].astype(F32)
        ext_ref[cg, HALO:, :] = cur
        return cur, lambda s: ext_ref[cg, HALO - s:HALO - s + ts, :]

    def keep_history(cg, cur):
        ext_ref[cg, :HALO, :] = cur[ts - HALO:, :]

    n_pool_groups = d_pool // LANES
    for g in range(n_pool_groups):
        w = POOL_WINDOWS[g]
        cur, shifted = history(g)
        win = cur
        for s in range(1, w):
            win = win + shifted(s)
        keep_history(g, cur)
        cnt = jnp.minimum(pos1, float(w))
        d = win / cnt - cur
        y = _dot(d.astype(BF16), wpool_ref[g]) * pscale_ref[:, g * LANES:(g + 1) * LANES]
        pool_ref[:, g * LANES:(g + 1) * LANES] = y.astype(BF16)
        project_some()

    n_ml_groups = d_ml // LANES
    for which, (cw_ref, dst_ref, scale) in enumerate(((convq_ref, q_ref, 1.0), (convk_ref, k_ref, dh ** -0.5))):
        for g in range(n_ml_groups):
            cols = slice(g * LANES, (g + 1) * LANES)
            cg = n_pool_groups + which * n_ml_groups + g
            cur, shifted = history(cg)
            acc = cur * cw_ref[CONV_K - 1:CONV_K, cols]
            for sft in range(1, CONV_K):
                acc = acc + shifted(sft) * cw_ref[CONV_K - 1 - sft:CONV_K - sft, cols]
            keep_history(cg, cur)
            dst_ref[:, cols] = (_silu(acc) * scale).astype(BF16)
        project_some()

    _route_slots(routed, stri_ref, ut_ref, sel_ref, rti_ref, rtf_ref, tstat_ref, srow_ref)
    project_some()

    zc = zif_ref[...] + bif_ref[...]
    lf_c = _log_sigmoid(zc)
    bc = sum(_dot(tric_ref[...], p) for p in _split3(lf_c))
    zr = zift_ref[...] + bift_ref[...]
    lf_r = _log_sigmoid(zr)
    br = sum(_dot(p, trir_ref[...]) for p in _split3(lf_r))
    project_some(2)

    ti = lax.broadcasted_iota(jnp.int32, (CHUNK, CHUNK), 0)
    si = lax.broadcasted_iota(jnp.int32, (CHUNK, CHUNK), 1)
    causal = si <= ti
    ones_blk = jnp.ones((CHUNK, dh), BF16)
    v0 = d_pool + 2 * d_ml
    ig_rep = [jnp.broadcast_to(zc[:, h:h + 1], (ts, dh)) for h in range(N_HEADS)]
    bt_rep = [jnp.broadcast_to(bc[:, N_HEADS + h:N_HEADS + h + 1], (ts, dh)) for h in range(N_HEADS)]

    m_state = [mst_ref[h:h + 1, :] for h in range(N_HEADS)]
    c_state = [cst_ref[h] for h in range(N_HEADS)]
    def stage_scores(c):
        rs = slice(c * CHUNK, (c + 1) * CHUNK)
        out = []
        for h in range(N_HEADS):
            hs = slice(h * dh, (h + 1) * dh)
            q = q_ref[rs, hs]
            k = k_ref[rs, hs]
            bt = bt_rep[h][rs, :]
            r_row = zr[h:h + 1, rs] - br[N_HEADS + h:N_HEADS + h + 1, rs]
            dmat = jnp.where(causal, bt[:, :CHUNK] + r_row, -jnp.inf)
            out.append(dict(q=q, k=k, bt=bt, dmat=dmat, qk=_dot_nt(q, k),
                            m_intra=jnp.max(dmat, axis=-1, keepdims=True)))
        return out

    def stage_state(c, st):
        rs = slice(c * CHUNK, (c + 1) * CHUNK)
        for h in range(N_HEADS):
            s = st[h]
            bt, k = s["bt"], s["k"]
            m_prev, c_prev = m_state[h], c_state[h]
            v_aug = jnp.concatenate([zm_ref[rs, v0 + h * dh:v0 + (h + 1) * dh], ones_blk], axis=-1)
            igc = ig_rep[h][rs, :]
            b_last = bt[CHUNK - 1:CHUNK, :]
            a_log = b_last - bt + igc
            a_max = jnp.max(a_log, axis=0, keepdims=True)
            m_new = jnp.maximum(b_last + m_prev, a_max)
            kw = (k.astype(F32) * jnp.exp(a_log - m_new)).astype(BF16)
            decay = jnp.exp(b_last + m_prev - m_new)
            s.update(v_aug=v_aug, m_prev=m_prev, qc=_dot(s["q"], c_prev.astype(BF16)))
            c_state[h] = jnp.concatenate([decay, decay], axis=-1) * c_prev + _dot_tn(kw, v_aug)
            m_state[h] = m_new

    def stage_values(c, st):
        rs = slice(c * CHUNK, (c + 1) * CHUNK)
        for h in range(N_HEADS):
            s = st[h]
            hs = slice(h * dh, (h + 1) * dh)
            inter = s["bt"] + s["m_prev"]
            m_t = jnp.maximum(inter, s["m_intra"])
            w_inter = jnp.exp(inter - m_t)
            smat = s["qk"] * jnp.exp(s["dmat"] - m_t[:, :CHUNK])
            sv = _dot(smat.astype(BF16), s["v_aug"])
            qc = s["qc"]
            nq = w_inter * qc[:, dh:] + sv[:, dh:]
            den = jnp.maximum(jnp.abs(nq), jnp.exp(-m_t))
            h_ref[rs, hs] = (w_inter * qc[:, :dh] + sv[:, :dh]) / den

    staged = stage_scores(0)
    for c in range(n_chunks):
        stage_state(c, staged)
        project_some()
        nxt = stage_scores(c + 1) if c + 1 < n_chunks else None
        project_some()
        stage_values(c, staged)
        staged = nxt
    for h in range(N_HEADS):
        cst_ref[h] = c_state[h]
        mst_ref[h:h + 1, :] = m_state[h]

    o0 = v0 + d_ml
    for h in range(N_HEADS):
        hs = slice(h * dh, (h + 1) * dh)
        hv = h_ref[:, hs]
        mu = jnp.mean(hv, axis=-1, keepdims=True)
        hc = hv - mu
        var = jnp.mean(hc * hc, axis=-1, keepdims=True)
        hn = hc * lax.rsqrt(var + EPS) * ghead_ref[:, hs]
        og = _sigmoid(zm_ref[:, o0 + h * dh:o0 + (h + 1) * dh].astype(F32))
        q_ref[:, hs] = (og * hn).astype(BF16)
    y_a = _dot(pool_ref[...], wa_ref[...])
    y_b = _dot(q_ref[...], wb_ref[...])
    d_model = x_ref.shape[1]
    ga = _sigmoid(zg_ref[:, :d_model].astype(F32))
    gb = _sigmoid(zg_ref[:, d_model:].astype(F32))
    merged = (ga * y_a + gb * y_b).astype(BF16)
    x2 = x_ref[...] + _dot(merged, wo_ref[...])
    x2_ref[...] = x2.astype(BF16)
    project_some(len(pending))

    ms = jnp.mean(x2 * x2, axis=-1, keepdims=True)
    xn2 = x2 * lax.rsqrt(ms + EPS) * gffn_ref[...]
    xh = xn2.astype(BF16)
    xn2_ref[...] = xh
    lg_ref[...] = _dot(xh, wr_ref[...]) + br_ref[...]
    project_some(len(pending))


def _route_select(lg):
    ts = lg.shape[0]
    lane = lax.broadcasted_iota(jnp.int32, (ts, LANES), 1)
    lanef = lane.astype(F32)
    big = float(4 * LANES)
    gl = jnp.where(lane < N_GROUPS, lg, -jnp.inf)
    gmax = jnp.max(gl, axis=-1, keepdims=True)
    g_sel = jnp.min(jnp.where(gl == gmax, lanef, big), axis=-1, keepdims=True)
    p_g = 1.0 / jnp.sum(jnp.exp(gl - gmax), axis=-1, keepdims=True)
    lo = ROUTER_LANE0 + EXPERTS_PER_GROUP * g_sel
    el = jnp.where((lanef >= lo) & (lanef < lo + EXPERTS_PER_GROUP), lg, -jnp.inf)
    m1 = jnp.max(el, axis=-1, keepdims=True)
    i1 = jnp.min(jnp.where(el == m1, lanef, big), axis=-1, keepdims=True)
    el2 = jnp.where(lanef == i1, -jnp.inf, el)
    m2 = jnp.max(el2, axis=-1, keepdims=True)
    i2 = jnp.min(jnp.where(el2 == m2, lanef, big), axis=-1, keepdims=True)
    e2x = jnp.exp(m2 - m1)
    gate1 = p_g / (1.0 + e2x)
    gate2 = p_g * e2x / (1.0 + e2x)
    return dict(lane=lane, i1=i1, i2=i2, gate1=gate1, gate2=gate2, oh1=lanef == i1, oh2=lanef == i2)


def _route_slots(r, stri_ref, ut_ref, sel_ref, rti_ref, rtf_ref, tstat_ref, srow_ref):
    lane, oh1, oh2, i1, i2 = r["lane"], r["oh1"], r["oh2"], r["i1"], r["i2"]
    ohs = jnp.where(oh1 | oh2, 1.0, 0.0)
    n_loc = jnp.sum(ohs, axis=0, keepdims=True)
    pieces = jnp.floor((n_loc + (ROW_PIECE - 1.0)) * (1.0 / ROW_PIECE))
    piece_off = _dot(jnp.broadcast_to(pieces, (8, LANES)).astype(BF16), ut_ref[...])[0:1, :]
    base = _dot(stri_ref[...], ohs.astype(BF16)) + ROW_PIECE * piece_off
    slot1 = jnp.sum(jnp.where(oh1, base, 0.0), axis=-1, keepdims=True)
    slot2 = jnp.sum(jnp.where(oh2, base, 0.0), axis=-1, keepdims=True)
    tstat_ref[...] = jnp.broadcast_to(pieces, tstat_ref.shape).astype(jnp.int32)

    rti = jnp.where(lane == 0, i1 - ROUTER_LANE0,
                    jnp.where(lane == 1, i2 - ROUTER_LANE0,
                              jnp.where(lane == 2, slot1, jnp.where(lane == 3, slot2, 0.0))))
    rti_ref[...] = rti.astype(jnp.int32)
    rtf_ref[...] = jnp.where(lane == 0, r["gate1"], jnp.where(lane == 1, r["gate2"], 0.0))
    h1 = jnp.floor(slot1 * (1.0 / SLOT_RADIX))
    h2 = jnp.floor(slot2 * (1.0 / SLOT_RADIX))
    parts = jnp.where(lane == 0, h1, jnp.where(lane == 1, slot1 - SLOT_RADIX * h1,
                      jnp.where(lane == 2, h2, jnp.where(lane == 3, slot2 - SLOT_RADIX * h2, 0.0))))
    srow_ref[...] = _dot_nt(sel_ref[...], parts.astype(BF16))


def _mixer(x2d, g_mix, w_all, params, batch, seq):
    T, D = x2d.shape
    ts = min(MIX_TS, seq)
    nts = seq // ts
    d_pool = params["w_br_a"].shape[0]
    d_ml = params["w_br_b"].shape[0]
    dh = d_ml // N_HEADS

    idx = np.arange(ts)
    same_chunk = (idx[:, None] // CHUNK) == (idx[None, :] // CHUNK)
    tri_c = jnp.asarray((idx[None, :] <= idx[:, None]) & same_chunk, BF16)
    tri_r = jnp.asarray((idx[:, None] <= idx[None, :]) & same_chunk, BF16)
    stri = jnp.asarray(idx[None, :] < idx[:, None], BF16)
    lane_idx = np.arange(LANES)
    ut = jnp.asarray(lane_idx[:, None] < lane_idx[None, :], BF16)
    sel = jnp.asarray(np.arange(8)[:, None] == lane_idx[None, :], BF16)

    n_tiles = batch * nts
    tok = lambda g: (g, 0)
    tok_in = lambda g: (jnp.minimum(g, n_tiles - 1), 0)
    tok_next = lambda g: (jnp.minimum(g + 1, n_tiles - 1), 0)
    tok_prev = lambda g: (jnp.maximum(g - 1, 0), 0)
    tok_prev_t = lambda g: (0, jnp.maximum(g - 1, 0))
    c2 = lambda g: (0, 0)
    c3 = lambda g: (0, 0, 0)
    full = lambda a: pl.BlockSpec(a.shape, c2 if a.ndim == 2 else c3)
    consts = [params[n] for n in ("b_if", "b_if_t", "conv_q", "conv_k", "g_head", "w_pool", "pool_scale",
                                  "w_br_a", "w_br_b", "w_out", "g_ffn", "w_r", "b_r")]
    consts = [g_mix, w_all] + consts + [tri_c, tri_r, stri, ut, sel]
    nm, ng = d_pool + 4 * d_ml, 2 * D
    z_scratch = [pltpu.VMEM((ts, nm), BF16), pltpu.VMEM((ts, ng), BF16),
                 pltpu.VMEM((ts, LANES), F32), pltpu.VMEM((16, ts), F32)]
    return pl.pallas_call(
        functools.partial(_mixer_kernel, tiles_per_seq=nts),
        grid=(n_tiles + 1,),
        in_specs=[pl.BlockSpec((ts, D), tok_in),
                  pl.BlockSpec((ts, D), tok_next)] + [full(a) for a in consts],
        out_specs=[pl.BlockSpec((ts, D), tok),
                   pl.BlockSpec((ts, D), tok),
                   pl.BlockSpec((ts, LANES), tok_prev),
                   pl.BlockSpec((ts, LANES), tok_prev),
                   pl.BlockSpec((8, LANES), tok_prev),
                   pl.BlockSpec((8, ts), tok_prev_t)],
        out_shape=[jax.ShapeDtypeStruct((T + ts, D), BF16),
                   jax.ShapeDtypeStruct((T + ts, D), BF16),
                   jax.ShapeDtypeStruct((T, LANES), jnp.int32),
                   jax.ShapeDtypeStruct((T, LANES), F32),
                   jax.ShapeDtypeStruct((n_tiles * 8, LANES), jnp.int32),
                   jax.ShapeDtypeStruct((8, T), F32)],
        scratch_shapes=z_scratch + z_scratch + [
                        pltpu.VMEM((ts, D), BF16),
                        pltpu.VMEM(((d_pool + 2 * d_ml) // LANES, HALO + ts, LANES), F32),
                        pltpu.VMEM((ts, d_ml), BF16),
                        pltpu.VMEM((ts, d_ml), BF16),
                        pltpu.VMEM((ts, d_ml), F32),
                        pltpu.VMEM((ts, d_pool), BF16),
                        pltpu.VMEM((N_HEADS, dh, 2 * dh), F32),
                        pltpu.VMEM((8, LANES), F32),
                        pltpu.VMEM((ts, LANES), F32),
                        pltpu.VMEM((D, ng), BF16),
                        pltpu.VMEM(params["w_pool"].shape, BF16),
                        pltpu.VMEM((d_pool, D), BF16),
                        pltpu.VMEM((d_ml, D), BF16),
                        pltpu.VMEM((D, D), BF16)],
        compiler_params=_cparams(1),
        name="mixer",
    )(x2d, x2d, *consts)


def _for_each_piece(npieces_ref, glob_ref, tile, fn):
    base = tile * PIECES_PER_TILE
    n = npieces_ref[tile]

    def one(p):
        fn(pl.multiple_of(p * ROW_PIECE, ROW_PIECE), pl.multiple_of(glob_ref[base + p] * ROW_PIECE, ROW_PIECE))

    def group(g, carry):
        for u in range(PIECE_UNROLL):
            one(g * PIECE_UNROLL + u)
        return carry

    n_groups = lax.div(n, jnp.int32(PIECE_UNROLL))
    lax.fori_loop(0, n_groups, group, 0)
    for u in range(PIECE_UNROLL - 1):
        @pl.when(n_groups * PIECE_UNROLL + u < n)
        def _():
            one(n_groups * PIECE_UNROLL + u)


def _used_slot_groups(n_pieces):
    return lax.div(n_pieces * ROW_PIECE + (SLOT_SUB - 1), jnp.int32(SLOT_SUB))


def _dispatch_kernel(npieces_ref, glob_ref, pad_start_ref, pad_rows_ref, nused_ref,
                     xn_ref, srow_ref, buf_ref, rows_ref, zeros_ref, sem, zsem, tsem, *, n_steps):
    tt = xn_ref.shape[0] // MOE_STEP_TILES
    sl = rows_ref.shape[1] // MOE_STEP_TILES
    n_blocks = buf_ref.shape[0] // MOE_TM
    i = pl.program_id(0)
    cur = lax.rem(i, 2)

    def zero_fill_share(step, act):
        for k in range(-(-N_EXPERTS // n_steps)):
            e = step + k * n_steps

            @pl.when(e < N_EXPERTS)
            def _(e=e):
                e_c = jnp.minimum(e, N_EXPERTS - 1)
                rows = pad_rows_ref[e_c]
                off = pad_start_ref[e_c]
                size = MOE_TM // 2
                while size >= ROW_PIECE:
                    has = (rows & size) != 0

                    @pl.when(has)
                    def _(off=off, size=size):
                        act(pltpu.make_async_copy(zeros_ref.at[pl.ds(0, size)],
                                                  buf_ref.at[pl.ds(pl.multiple_of(off, ROW_PIECE), size)], zsem))
                    off = off + jnp.where(has, size, 0)
                    size //= 2
        for k in range(-(-n_blocks // n_steps)):
            b = step + k * n_steps

            @pl.when((b >= nused_ref[0]) & (b < n_blocks))
            def _(b=b):
                act(pltpu.make_async_copy(zeros_ref, buf_ref.at[pl.ds(b * MOE_TM, MOE_TM)], tsem))

    @pl.when(i == 0)
    def _():
        zeros_ref[...] = jnp.zeros_like(zeros_ref)

    zero_fill_share(i, lambda cp: cp.start())

    def piece_copy(buf_slot, sub, local_row, global_row):
        return pltpu.make_async_copy(rows_ref.at[buf_slot, pl.ds(sub * sl + local_row, ROW_PIECE)],
                                     buf_ref.at[pl.ds(global_row, ROW_PIECE)], sem.at[buf_slot])

    def for_step_pieces(step, buf_slot, act):
        for sub in range(MOE_STEP_TILES):
            _for_each_piece(npieces_ref, glob_ref, step * MOE_STEP_TILES + sub,
                            lambda l, g, sub=sub: act(piece_copy(buf_slot, sub, l, g)))

    @pl.when(i >= 2)
    def _():
        for_step_pieces(i - 2, cur, lambda cp: cp.wait())

    for sub in range(MOE_STEP_TILES):
        sr = srow_ref[:, sub * tt:(sub + 1) * tt]
        slot1 = SLOT_RADIX * sr[0:1, :] + sr[1:2, :]
        slot2 = SLOT_RADIX * sr[2:3, :] + sr[3:4, :]
        n_sub = _used_slot_groups(npieces_ref[i * MOE_STEP_TILES + sub])
        for k in range(TOP_K * tt // SLOT_SUB, sl // SLOT_SUB + 1):
            @pl.when(n_sub == k)
            def _(k=k, sub=sub, slot1=slot1, slot2=slot2):
                m = k * SLOT_SUB
                r = lax.broadcasted_iota(jnp.int32, (m, tt), 0).astype(F32)
                sel = jnp.where((r == slot1) | (r == slot2), 1.0, 0.0).astype(BF16)
                rows_ref[cur, sub * sl:sub * sl + m, :] = _dot(sel, xn_ref[sub * tt:(sub + 1) * tt, :]).astype(BF16)
    for_step_pieces(i, cur, lambda cp: cp.start())

    @pl.when(i == n_steps - 1)
    def _():
        @pl.when(i >= 1)
        def _():
            for_step_pieces(i - 1, 1 - cur, lambda cp: cp.wait())
        for_step_pieces(i, cur, lambda cp: cp.wait())

        def wait_share(step, carry):
            zero_fill_share(step, lambda cp: cp.wait())
            return carry

        lax.fori_loop(0, n_steps, wait_share, 0)


def _dispatch(xn2, srow, npieces, piece_glob, pad_start, pad_rows, nused, n_rows):
    T, D = srow.shape[1], xn2.shape[1]
    tt = MOE_STEP_TILES * MIX_TS
    return pl.pallas_call(
        functools.partial(_dispatch_kernel, n_steps=T // tt),
        grid_spec=pltpu.PrefetchScalarGridSpec(
            num_scalar_prefetch=5,
            grid=(T // tt,),
            in_specs=[pl.BlockSpec((tt, D), lambda i, *_: (i, 0)),
                      pl.BlockSpec((8, tt), lambda i, *_: (0, i))],
            out_specs=pl.BlockSpec(memory_space=pl.ANY),
            scratch_shapes=[pltpu.VMEM((2, MOE_STEP_TILES * MOE_SL, D), BF16),
                            pltpu.VMEM((MOE_TM, D), BF16),
                            pltpu.SemaphoreType.DMA((2,)),
                            pltpu.SemaphoreType.DMA(()),
                            pltpu.SemaphoreType.DMA(())]),
        out_shape=jax.ShapeDtypeStruct((n_rows, D), BF16),
        compiler_params=_cparams(1),
        name="dispatch",
    )(npieces, piece_glob, pad_start, pad_rows, nused, xn2, srow)


def _experts_kernel(blk_e_ref, nused_ref, nsub_ref, first_ref, next_e_ref, slot_ref,
                    x_ref, wg_hbm, wu_hbm, wd_hbm, y_ref,
                    wg32_ref, wu32_ref, wd32_ref, wgb_ref, wub_ref, wdb_ref, sem):
    i = pl.program_id(0)
    used = i < nused_ref[0]
    n_sub = nsub_ref[i]
    landing = ((wg_hbm, wg32_ref), (wu_hbm, wu32_ref), (wd_hbm, wd32_ref))

    def weight_copies(e, s):
        return [pltpu.make_async_copy(hbm.at[e], vmem.at[s], sem.at[s, n]) for n, (hbm, vmem) in enumerate(landing)]

    @pl.when(used & (i == 0))
    def _():
        for cp in weight_copies(blk_e_ref[0], 0):
            cp.start()

    run_start = used & (first_ref[i] > 0)
    s = slot_ref[i]

    @pl.when(run_start)
    def _():
        for cp in weight_copies(blk_e_ref[i], s):
            cp.wait()

        @pl.when(next_e_ref[i] >= 0)
        def _():
            for cp in weight_copies(next_e_ref[i], 1 - s):
                cp.start()

    def swiglu(m, wg, wu, wd):
        x = x_ref[:m, :]
        hg = _dot(x, wg)
        hu = _dot(x, wu)
        hid = (_silu(hg) * hu).astype(BF16)
        y_ref[:m, :] = _dot(hid, wd).astype(BF16)
        if m < MOE_TM:
            y_ref[m:, :] = jnp.zeros((MOE_TM - m, y_ref.shape[1]), BF16)

    for k in range(1, MOE_TM // EXPERT_SUB + 1):
        @pl.when(run_start & (n_sub == k))
        def _(k=k):
            wg = wg32_ref[s].astype(BF16)
            wu = wu32_ref[s].astype(BF16)
            wd = wd32_ref[s].astype(BF16)
            wgb_ref[...] = wg
            wub_ref[...] = wu
            wdb_ref[...] = wd
            swiglu(k * EXPERT_SUB, wg, wu, wd)

        @pl.when(used & jnp.logical_not(run_start) & (n_sub == k))
        def _(k=k):
            swiglu(k * EXPERT_SUB, wgb_ref[...], wub_ref[...], wdb_ref[...])


def _experts(buf, blk_e, nused, nsub, run_first, next_e, run_slot, w_gate, w_up, w_down):
    R, D = buf.shape
    de = w_gate.shape[2]
    n_blocks = R // MOE_TM
    row_map = lambda i, be, nu, *_: (jnp.minimum(i, nu[0] - 1), 0)
    return pl.pallas_call(
        _experts_kernel,
        grid_spec=pltpu.PrefetchScalarGridSpec(
            num_scalar_prefetch=6,
            grid=(n_blocks,),
            in_specs=[pl.BlockSpec((MOE_TM, D), row_map),
                      pl.BlockSpec(memory_space=pl.ANY),
                      pl.BlockSpec(memory_space=pl.ANY),
                      pl.BlockSpec(memory_space=pl.ANY)],
            out_specs=pl.BlockSpec((MOE_TM, D), row_map),
            scratch_shapes=[pltpu.VMEM((2, D, de), F32), pltpu.VMEM((2, D, de), F32), pltpu.VMEM((2, de, D), F32),
                            pltpu.VMEM((D, de), BF16), pltpu.VMEM((D, de), BF16), pltpu.VMEM((de, D), BF16),
                            pltpu.SemaphoreType.DMA((2, 3))]),
        out_shape=jax.ShapeDtypeStruct((R, D), BF16),
        input_output_aliases={6: 0},
        compiler_params=_cparams(1),
        name="experts",
    )(blk_e, nused, nsub, run_first, next_e, run_slot, buf, w_gate, w_up, w_down)


def _combine_kernel(npieces_ref, glob_ref, x2_ref, rti_ref, rtf_ref, gfin_ref, yb_ref, out_ref,
                    rows_ref, sem):
    tt = x2_ref.shape[0] // MOE_STEP_TILES
    sl = rows_ref.shape[1] // MOE_STEP_TILES
    i = pl.program_id(0)
    n_steps = pl.num_programs(0)
    cur = lax.rem(i, 2)

    def piece_copy(buf_slot, sub, local_row, global_row):
        return pltpu.make_async_copy(yb_ref.at[pl.ds(global_row, ROW_PIECE)],
                                     rows_ref.at[buf_slot, pl.ds(sub * sl + local_row, ROW_PIECE)], sem.at[buf_slot])

    def for_step_pieces(step, buf_slot, act):
        for sub in range(MOE_STEP_TILES):
            _for_each_piece(npieces_ref, glob_ref, step * MOE_STEP_TILES + sub,
                            lambda l, g, sub=sub: act(piece_copy(buf_slot, sub, l, g)))

    @pl.when(i == 0)
    def _():
        rows_ref[...] = jnp.zeros_like(rows_ref)
        for_step_pieces(0, 0, lambda cp: cp.start())

    @pl.when(i + 1 < n_steps)
    def _():
        for_step_pieces(i + 1, 1 - cur, lambda cp: cp.start())

    for_step_pieces(i, cur, lambda cp: cp.wait())

    for sub in range(MOE_STEP_TILES):
        ts_rows = slice(sub * tt, (sub + 1) * tt)
        rti = rti_ref[ts_rows, :]
        rtf = rtf_ref[ts_rows, :]
        slot1 = rti[:, 2:3]
        slot2 = rti[:, 3:4]
        n_sub = _used_slot_groups(npieces_ref[i * MOE_STEP_TILES + sub])
        for k in range(TOP_K * tt // SLOT_SUB, sl // SLOT_SUB + 1):
            @pl.when(n_sub == k)
            def _(k=k, sub=sub, ts_rows=ts_rows, rtf=rtf, slot1=slot1, slot2=slot2):
                m = k * SLOT_SUB
                lane = lax.broadcasted_iota(jnp.int32, (tt, m), 1)
                g = jnp.where(lane == slot1, rtf[:, 0:1], jnp.where(lane == slot2, rtf[:, 1:2], 0.0)).astype(BF16)
                y = x2_ref[ts_rows, :].astype(F32) + _dot(g, rows_ref[cur, sub * sl:sub * sl + m, :])
                ms = jnp.mean(y * y, axis=-1, keepdims=True)
                out_ref[ts_rows, :] = y * lax.rsqrt(ms + EPS) * gfin_ref[...]


def _combine(x2, rti, rtf, g_final, yb, npieces, piece_glob):
    T, D = rti.shape[0], x2.shape[1]
    tt = MOE_STEP_TILES * MIX_TS
    tok = lambda i, *_: (i, 0)
    return pl.pallas_call(
        _combine_kernel,
        grid_spec=pltpu.PrefetchScalarGridSpec(
            num_scalar_prefetch=2,
            grid=(T // tt,),
            in_specs=[pl.BlockSpec((tt, D), tok),
                      pl.BlockSpec((tt, LANES), tok),
                      pl.BlockSpec((tt, LANES), tok),
                      pl.BlockSpec((1, D), lambda i, *_: (0, 0)),
                      pl.BlockSpec(memory_space=pl.ANY)],
            out_specs=pl.BlockSpec((tt, D), tok),
            scratch_shapes=[pltpu.VMEM((2, MOE_STEP_TILES * MOE_SL, D), BF16),
                            pltpu.SemaphoreType.DMA((2,))]),
        out_shape=jax.ShapeDtypeStruct((T, D), F32),
        compiler_params=_cparams(1),
        name="combine",
    )(npieces, piece_glob, x2, rti, rtf, g_final, yb)


def _pad_lanes(a, width=LANES):
    return jnp.pad(a, ((0, 0), (0, width - a.shape[1])))


def kernel(x, g_mix, w_in, b_if, conv_q, conv_k, g_head, w_pool, pool_scale, w_br_a, w_br_b, w_out,
           g_ffn, w_rg, b_rg, w_re, b_re, w_e_gate, w_e_up, w_e_down, g_final):
    B, S, D = x.shape
    T = B * S
    assert g_mix.shape[0] == 1, "single-layer block"
    assert S % MIX_TS == 0 and (T // MIX_TS) % MOE_STEP_TILES == 0
    d_pool = w_br_a.shape[1]
    d_ml = w_br_b.shape[1]
    x2d = x.reshape(T, D)

    params = {
        "b_if": _pad_lanes(b_if[0][None, :]),
        "b_if_t": jnp.pad(b_if[0][:, None], ((0, 16 - 2 * N_HEADS), (0, 0))),
        "conv_q": conv_q[0], "conv_k": conv_k[0],
        "g_head": g_head[0][None, :],
        "w_pool": w_pool[0],
        "pool_scale": pool_scale[0][None, :],
        "w_br_a": w_br_a[0], "w_br_b": w_br_b[0],
        "w_out": w_out[0],
        "g_ffn": g_ffn[0][None, :],
        "w_r": _pad_lanes(jnp.concatenate([w_rg[0], w_re[0]], axis=1)).astype(BF16),
        "b_r": _pad_lanes(jnp.concatenate([b_rg[0], b_re[0]])[None, :]),
    }

    x2, xn2, rti, rtf, tstat, srow = _mixer(x2d, g_mix[0][None, :], w_in[0].astype(BF16), params, B, S)

    n_tiles = T // MIX_TS
    n_rows = n_tiles * MOE_SL + N_EXPERTS * MOE_TM
    n_blocks = n_rows // MOE_TM
    i32 = lambda a: a.astype(jnp.int32)
    mm = lambda a, b: jnp.round(jnp.dot(a, b, precision=lax.Precision.HIGHEST, preferred_element_type=F32))
    e_ids = np.arange(N_EXPERTS)
    t_ids = np.arange(n_tiles)
    b_ids = np.arange(n_blocks)
    cum_e = jnp.asarray(e_ids[:, None] <= e_ids[None, :], F32)
    cum_t = jnp.asarray(t_ids[:, None] >= t_ids[None, :], F32)
    cum_b = jnp.asarray(b_ids[:, None] <= b_ids[None, :], F32)
    e_row = jnp.asarray(e_ids[None, :], F32)

    pcs = tstat.reshape(n_tiles, 8, LANES)[:, 0, ROUTER_LANE0:ROUTER_LANE0 + N_EXPERTS].astype(F32)
    piece_end = mm(pcs, cum_e)
    piece_loc = piece_end - pcs
    tile_cum = mm(cum_t, pcs)
    rows_e = tile_cum[-1:, :] * ROW_PIECE
    padded = jnp.floor((rows_e + (MOE_TM - 1)) * (1.0 / MOE_TM)) * MOE_TM
    pend = mm(padded, cum_e)
    poff = pend - padded
    piece_glob = poff * (1.0 / ROW_PIECE) + tile_cum - pcs
    nused_f = pend[0, -1] * (1.0 / MOE_TM)
    blk_start = jnp.asarray(b_ids[:, None] * MOE_TM, F32)
    blk_e_f = jnp.minimum(jnp.sum((pend <= blk_start).astype(F32), axis=1, keepdims=True), N_EXPERTS - 1.0)
    blk_oh = (blk_e_f == e_row).astype(F32)
    later_nonempty = (e_ids[None, :] > e_ids[:, None]) & (rows_e > 0)
    next_of_e = jnp.min(jnp.where(later_nonempty, e_row, float(N_EXPERTS)), axis=1)
    next_of_e = jnp.where(next_of_e == N_EXPERTS, -1.0, next_of_e)
    per_blk = mm(blk_oh, jnp.stack([pend[0], (poff + rows_e)[0], next_of_e], axis=1))
    blk_used = blk_start < pend[0, -1]
    rows_in_blk = jnp.clip(per_blk[:, 1:2] - blk_start, 0.0, float(MOE_TM))
    nsub = jnp.floor((rows_in_blk + (EXPERT_SUB - 1)) * (1.0 / EXPERT_SUB))
    prev_e = jnp.concatenate([jnp.full((1, 1), -1.0, F32), blk_e_f[:-1]], axis=0)
    run_first = (blk_used & (blk_e_f != prev_e)).astype(F32)
    run_idx = mm(run_first.reshape(1, n_blocks), cum_b) - 1.0
    run_slot = run_idx - 2.0 * jnp.floor(run_idx * 0.5)
    p_ids = jnp.asarray(np.arange(PIECES_PER_TILE), F32)
    e_of_p = jnp.minimum(jnp.sum((piece_end[:, None, :] <= p_ids[None, :, None]).astype(F32), axis=2),
                         N_EXPERTS - 1.0)
    shift = jnp.sum(jnp.where(e_of_p[:, :, None] == e_row[None], (piece_glob - piece_loc)[:, None, :], 0.0), axis=2)
    glob_of_p = i32(shift + p_ids[None, :]).reshape(n_tiles * PIECES_PER_TILE)
    npieces = i32(piece_end[:, -1])
    flat_b = lambda a: i32(a).reshape(n_blocks)
    blk_e, nused = flat_b(blk_e_f), i32(nused_f).reshape(1)

    flat_e = lambda a: i32(a).reshape(N_EXPERTS)
    buf = _dispatch(xn2, srow, npieces, glob_of_p, flat_e(poff + rows_e), flat_e(padded - rows_e), nused, n_rows)
    yb = _experts(buf, blk_e, nused, flat_b(nsub), flat_b(run_first), flat_b(per_blk[:, 2:3]), flat_b(run_slot),
                  w_e_gate[0], w_e_up[0], w_e_down[0])
    out = _combine(x2, rti, rtf, g_final[None, :], yb, npieces, glob_of_p)
    return out.reshape(B, S, D)
```

```python
import functools

import numpy as np
import jax
import jax.numpy as jnp
from jax import lax
from jax.experimental import pallas as pl
from jax.experimental.pallas import tpu as pltpu

F32 = jnp.float32
BF16 = jnp.bfloat16

CHUNK = 64
POOL_WINDOWS = (2, 4, 8, 16)
N_HEADS = 4
CONV_K = 4
N_GROUPS = 4
EXPERTS_PER_GROUP = 8
N_EXPERTS = N_GROUPS * EXPERTS_PER_GROUP
TOP_K = 2
EPS = 1e-6

LANES = 128
HALO = 16
ROUTER_LANE0 = N_GROUPS

INPROJ_TN = 256
MIX_TS = 256
MOE_TM = 512
ROW_PIECE = 16
MOE_SL = TOP_K * MIX_TS + N_EXPERTS * ROW_PIECE
PIECES_PER_TILE = MOE_SL // ROW_PIECE
MOE_STEP_TILES = 2
SLOT_SUB = 128
PIECE_UNROLL = 4
EXPERT_SUB = 128
SLOT_RADIX = 16
VMEM_LIMIT = 56 * 1024 * 1024


def _cparams(n_axes):
    return pltpu.CompilerParams(dimension_semantics=("arbitrary",) * n_axes,
                                vmem_limit_bytes=VMEM_LIMIT)


def _sigmoid(v):
    return 0.5 * jnp.tanh(0.5 * v) + 0.5


def _silu(v):
    return v * _sigmoid(v)


def _log_sigmoid(v):
    return jnp.minimum(v, 0.0) - jnp.log1p(jnp.exp(-jnp.abs(v)))


def _split3(v):
    hi = v.astype(BF16)
    r1 = v - hi.astype(F32)
    mid = r1.astype(BF16)
    lo = (r1 - mid.astype(F32)).astype(BF16)
    return hi, mid, lo


def _dot(a, b):
    return jnp.dot(a, b, preferred_element_type=F32)


def _dot_nt(a, b):
    return lax.dot_general(a, b, (((1,), (1,)), ((), ())), preferred_element_type=F32)


def _dot_tn(a, b):
    return lax.dot_general(a, b, (((0,), (0,)), ((), ())), preferred_element_type=F32)


def _inproj_steps(x_ref, g_ref, w_refs, z_refs, xn_ref):
    w_ref_all, wg_ref = w_refs
    zm_ref, zg_ref, zif_ref, zift_ref = z_refs
    n_main = zm_ref.shape[1]

    def norm():
        x = x_ref[...]
        ms = jnp.mean(x * x, axis=-1, keepdims=True)
        xn_ref[...] = (x * lax.rsqrt(ms + EPS) * g_ref[...]).astype(BF16)

    def block(w_ref, z_ref, c0):
        def run():
            cols = slice(c0, c0 + INPROJ_TN)
            z_ref[:, cols] = _dot(xn_ref[...], w_ref[:, cols]).astype(BF16)
        return run

    def gates():
        zif = _dot(xn_ref[...], w_ref_all[:, n_main:n_main + LANES])
        zif_ref[...] = zif
        zift_ref[...] = zif.T[:zift_ref.shape[0], :]

    steps = [norm, gates]
    steps += [block(w_ref_all, zm_ref, c0) for c0 in range(0, n_main, INPROJ_TN)]
    steps += [block(wg_ref, zg_ref, c0) for c0 in range(0, zg_ref.shape[1], INPROJ_TN)]
    return steps


def _mixer_kernel(x_ref, xnext_ref, gmix_ref, win_ref,
                  bif_ref, bift_ref, convq_ref, convk_ref, ghead_ref, wpool32_ref, pscale_ref,
                  wa32_ref, wb32_ref, wo32_ref, gffn_ref, wr_ref, br_ref,
                  tric_ref, trir_ref, stri_ref, ut_ref, sel_ref,
                  x2_ref, xn2_ref, rti_ref, rtf_ref, tstat_ref, srow_ref,
                  zm_ref, zg_ref, zif_ref, zift_ref, zm_nxt, zg_nxt, zif_nxt, zift_nxt, xn_ref,
                  ext_ref, q_ref, k_ref, h_ref, pool_ref, cst_ref, mst_ref, lg_ref, wg_ref,
                  wpool_ref, wa_ref, wb_ref, wo_ref, *, tiles_per_seq):
    ts = x_ref.shape[0]
    d_pool = wa_ref.shape[0]
    d_ml = wb_ref.shape[0]
    dh = d_ml // N_HEADS
    n_chunks = ts // CHUNK
    g_step = pl.program_id(0)
    j = lax.rem(g_step, tiles_per_seq)
    w_in_refs = (win_ref, wg_ref)
    z_cur = (zm_ref, zg_ref, zif_ref, zift_ref)
    z_nxt = (zm_nxt, zg_nxt, zif_nxt, zift_nxt)
    first = g_step == 0

    @pl.when(first)
    def _():
        g0 = zm_ref.shape[1] + 2 * N_HEADS
        wg_ref[...] = win_ref[:, g0:g0 + wg_ref.shape[1]]
        for dst, src in ((wpool_ref, wpool32_ref), (wa_ref, wa32_ref), (wb_ref, wb32_ref), (wo_ref, wo32_ref)):
            dst[...] = src[...].astype(BF16)
        for step in _inproj_steps(x_ref, gmix_ref, w_in_refs, z_cur, xn_ref):
            step()
        lg_ref[...] = jnp.zeros_like(lg_ref)

    @pl.when(jnp.logical_not(first))
    def _():
        for dst, src in zip(z_cur, z_nxt):
            dst[...] = src[...]

    @pl.when(j == 0)
    def _():
        ext_ref[:, :HALO, :] = jnp.zeros((ext_ref.shape[0], HALO, LANES), F32)
        cst_ref[...] = jnp.zeros_like(cst_ref)
        mst_ref[...] = jnp.zeros_like(mst_ref)

    pending = _inproj_steps(xnext_ref, gmix_ref, w_in_refs, z_nxt, xn_ref)

    def project_some(n=1):
        for _ in range(min(n, len(pending))):
            pending.pop(0)()

    project_some(2)

    routed = _route_select(lg_ref[...])

    row = lax.broadcasted_iota(jnp.int32, (ts, LANES), 0)
    pos1 = (row + j * ts + 1).astype(F32)

    def history(cg):
        cur = zm_ref[:, cg * LANES:(cg + 1) * LANES].astype(F32)
        ext_ref[cg, HALO:, :] = cur
        return cur, lambda s: ext_ref[cg, HALO - s:HALO - s + ts, :]

    def keep_history(cg, cur):
        ext_ref[cg, :HALO, :] = cur[ts - HALO:, :]

    n_pool_groups = d_pool // LANES
    for g in range(n_pool_groups):
        w = POOL_WINDOWS[g]
        cur, shifted = history(g)
        win = cur
        for s in range(1, w):
            win = win + shifted(s)
        keep_history(g, cur)
        cnt = jnp.minimum(pos1, float(w))
        d = win / cnt - cur
        y = _dot(d.astype(BF16), wpool_ref[g]) * pscale_ref[:, g * LANES:(g + 1) * LANES]
        pool_ref[:, g * LANES:(g + 1) * LANES] = y.astype(BF16)
        project_some()

    n_ml_groups = d_ml // LANES
    for which, (cw_ref, dst_ref, scale) in enumerate(((convq_ref, q_ref, 1.0), (convk_ref, k_ref, dh ** -0.5))):
        for g in range(n_ml_groups):
            cols = slice(g * LANES, (g + 1) * LANES)
            cg = n_pool_groups + which * n_ml_groups + g
            cur, shifted = history(cg)
            acc = cur * cw_ref[CONV_K - 1:CONV_K, cols]
            for sft in range(1, CONV_K):
                acc = acc + shifted(sft) * cw_ref[CONV_K - 1 - sft:CONV_K - sft, cols]
            keep_history(cg, cur)
            dst_ref[:, cols] = (_silu(acc) * scale).astype(BF16)
        project_some()

    _route_slots(routed, stri_ref, ut_ref, sel_ref, rti_ref, rtf_ref, tstat_ref, srow_ref)
    project_some()

    zc = zif_ref[...] + bif_ref[...]
    lf_c = _log_sigmoid(zc)
    bc = sum(_dot(tric_ref[...], p) for p in _split3(lf_c))
    zr = zift_ref[...] + bift_ref[...]
    lf_r = _log_sigmoid(zr)
    br = sum(_dot(p, trir_ref[...]) for p in _split3(lf_r))
    project_some(2)

    ti = lax.broadcasted_iota(jnp.int32, (CHUNK, CHUNK), 0)
    si = lax.broadcasted_iota(jnp.int32, (CHUNK, CHUNK), 1)
    causal = si <= ti
    ones_blk = jnp.ones((CHUNK, dh), BF16)
    v0 = d_pool + 2 * d_ml
    ig_rep = [jnp.broadcast_to(zc[:, h:h + 1], (ts, dh)) for h in range(N_HEADS)]
    bt_rep = [jnp.broadcast_to(bc[:, N_HEADS + h:N_HEADS + h + 1], (ts, dh)) for h in range(N_HEADS)]

    m_state = [mst_ref[h:h + 1, :] for h in range(N_HEADS)]
    c_state = [cst_ref[h] for h in range(N_HEADS)]
    def stage_scores(c):
        rs = slice(c * CHUNK, (c + 1) * CHUNK)
        out = []
        for h in range(N_HEADS):
            hs = slice(h * dh, (h + 1) * dh)
            q = q_ref[rs, hs]
            k = k_ref[rs, hs]
            bt = bt_rep[h][rs, :]
            r_row = zr[h:h + 1, rs] - br[N_HEADS + h:N_HEADS + h + 1, rs]
            dmat = jnp.where(causal, bt[:, :CHUNK] + r_row, -jnp.inf)
            out.append(dict(q=q, k=k, bt=bt, dmat=dmat, qk=_dot_nt(q, k),
                            m_intra=jnp.max(dmat, axis=-1, keepdims=True)))
        return out

    def stage_state(c, st):
        rs = slice(c * CHUNK, (c + 1) * CHUNK)
        for h in range(N_HEADS):
            s = st[h]
            bt, k = s["bt"], s["k"]
            m_prev, c_prev = m_state[h], c_state[h]
            v_aug = jnp.concatenate([zm_ref[rs, v0 + h * dh:v0 + (h + 1) * dh], ones_blk], axis=-1)
            igc = ig_rep[h][rs, :]
            b_last = bt[CHUNK - 1:CHUNK, :]
            a_log = b_last - bt + igc
            a_max = jnp.max(a_log, axis=0, keepdims=True)
            m_new = jnp.maximum(b_last + m_prev, a_max)
            kw = (k.astype(F32) * jnp.exp(a_log - m_new)).astype(BF16)
            decay = jnp.exp(b_last + m_prev - m_new)
            s.update(v_aug=v_aug, m_prev=m_prev, qc=_dot(s["q"], c_prev.astype(BF16)))
            c_state[h] = jnp.concatenate([decay, decay], axis=-1) * c_prev + _dot_tn(kw, v_aug)
            m_state[h] = m_new

    def stage_values(c, st):
        rs = slice(c * CHUNK, (c + 1) * CHUNK)
        for h in range(N_HEADS):
            s = st[h]
            hs = slice(h * dh, (h + 1) * dh)
            inter = s["bt"] + s["m_prev"]
            m_t = jnp.maximum(inter, s["m_intra"])
            w_inter = jnp.exp(inter - m_t)
            smat = s["qk"] * jnp.exp(s["dmat"] - m_t[:, :CHUNK])
            sv = _dot(smat.astype(BF16), s["v_aug"])
            qc = s["qc"]
            nq = w_inter * qc[:, dh:] + sv[:, dh:]
            den = jnp.maximum(jnp.abs(nq), jnp.exp(-m_t))
            h_ref[rs, hs] = (w_inter * qc[:, :dh] + sv[:, :dh]) / den

    staged = stage_scores(0)
    for c in range(n_chunks):
        stage_state(c, staged)
        project_some()
        nxt = stage_scores(c + 1) if c + 1 < n_chunks else None
        project_some()
        stage_values(c, staged)
        staged = nxt
    for h in range(N_HEADS):
        cst_ref[h] = c_state[h]
        mst_ref[h:h + 1, :] = m_state[h]

    o0 = v0 + d_ml
    for h in range(N_HEADS):
        hs = slice(h * dh, (h + 1) * dh)
        hv = h_ref[:, hs]
        mu = jnp.mean(hv, axis=-1, keepdims=True)
        hc = hv - mu
        var = jnp.mean(hc * hc, axis=-1, keepdims=True)
        hn = hc * lax.rsqrt(var + EPS) * ghead_ref[:, hs]
        og = _sigmoid(zm_ref[:, o0 + h * dh:o0 + (h + 1) * dh].astype(F32))
        q_ref[:, hs] = (og * hn).astype(BF16)
    y_a = _dot(pool_ref[...], wa_ref[...])
    y_b = _dot(q_ref[...], wb_ref[...])
    d_model = x_ref.shape[1]
    ga = _sigmoid(zg_ref[:, :d_model].astype(F32))
    gb = _sigmoid(zg_ref[:, d_model:].astype(F32))
    merged = (ga * y_a + gb * y_b).astype(BF16)
    x2 = x_ref[...] + _dot(merged, wo_ref[...])
    x2_ref[...] = x2.astype(BF16)
    project_some(len(pending))

    ms = jnp.mean(x2 * x2, axis=-1, keepdims=True)
    xn2 = x2 * lax.rsqrt(ms + EPS) * gffn_ref[...]
    xh = xn2.astype(BF16)
    xn2_ref[...] = xh
    lg_ref[...] = _dot(xh, wr_ref[...]) + br_ref[...]
    project_some(len(pending))


def _route_select(lg):
    ts = lg.shape[0]
    lane = lax.broadcasted_iota(jnp.int32, (ts, LANES), 1)
    lanef = lane.astype(F32)
    big = float(4 * LANES)
    gl = jnp.where(lane < N_GROUPS, lg, -jnp.inf)
    gmax = jnp.max(gl, axis=-1, keepdims=True)
    g_sel = jnp.min(jnp.where(gl == gmax, lanef, big), axis=-1, keepdims=True)
    p_g = 1.0 / jnp.sum(jnp.exp(gl - gmax), axis=-1, keepdims=True)
    lo = ROUTER_LANE0 + EXPERTS_PER_GROUP * g_sel
    el = jnp.where((lanef >= lo) & (lanef < lo + EXPERTS_PER_GROUP), lg, -jnp.inf)
    m1 = jnp.max(el, axis=-1, keepdims=True)
    i1 = jnp.min(jnp.where(el == m1, lanef, big), axis=-1, keepdims=True)
    el2 = jnp.where(lanef == i1, -jnp.inf, el)
    m2 = jnp.max(el2, axis=-1, keepdims=True)
    i2 = jnp.min(jnp.where(el2 == m2, lanef, big), axis=-1, keepdims=True)
    e2x = jnp.exp(m2 - m1)
    gate1 = p_g / (1.0 + e2x)
    gate2 = p_g * e2x / (1.0 + e2x)
    return dict(lane=lane, i1=i1, i2=i2, gate1=gate1, gate2=gate2, oh1=lanef == i1, oh2=lanef == i2)


def _route_slots(r, stri_ref, ut_ref, sel_ref, rti_ref, rtf_ref, tstat_ref, srow_ref):
    lane, oh1, oh2, i1, i2 = r["lane"], r["oh1"], r["oh2"], r["i1"], r["i2"]
    ohs = jnp.where(oh1 | oh2, 1.0, 0.0)
    n_loc = jnp.sum(ohs, axis=0, keepdims=True)
    pieces = jnp.floor((n_loc + (ROW_PIECE - 1.0)) * (1.0 / ROW_PIECE))
    piece_off = _dot(jnp.broadcast_to(pieces, (8, LANES)).astype(BF16), ut_ref[...])[0:1, :]
    base = _dot(stri_ref[...], ohs.astype(BF16)) + ROW_PIECE * piece_off
    slot1 = jnp.sum(jnp.where(oh1, base, 0.0), axis=-1, keepdims=True)
    slot2 = jnp.sum(jnp.where(oh2, base, 0.0), axis=-1, keepdims=True)
    tstat_ref[...] = jnp.broadcast_to(pieces, tstat_ref.shape).astype(jnp.int32)

    rti = jnp.where(lane == 0, i1 - ROUTER_LANE0,
                    jnp.where(lane == 1, i2 - ROUTER_LANE0,
                              jnp.where(lane == 2, slot1, jnp.where(lane == 3, slot2, 0.0))))
    rti_ref[...] = rti.astype(jnp.int32)
    rtf_ref[...] = jnp.where(lane == 0, r["gate1"], jnp.where(lane == 1, r["gate2"], 0.0))
    h1 = jnp.floor(slot1 * (1.0 / SLOT_RADIX))
    h2 = jnp.floor(slot2 * (1.0 / SLOT_RADIX))
    parts = jnp.where(lane == 0, h1, jnp.where(lane == 1, slot1 - SLOT_RADIX * h1,
                      jnp.where(lane == 2, h2, jnp.where(lane == 3, slot2 - SLOT_RADIX * h2, 0.0))))
    srow_ref[...] = _dot_nt(sel_ref[...], parts.astype(BF16))


def _mixer(x2d, g_mix, w_all, params, batch, seq):
    T, D = x2d.shape
    ts = min(MIX_TS, seq)
    nts = seq // ts
    d_pool = params["w_br_a"].shape[0]
    d_ml = params["w_br_b"].shape[0]
    dh = d_ml // N_HEADS

    idx = np.arange(ts)
    same_chunk = (idx[:, None] // CHUNK) == (idx[None, :] // CHUNK)
    tri_c = jnp.asarray((idx[None, :] <= idx[:, None]) & same_chunk, BF16)
    tri_r = jnp.asarray((idx[:, None] <= idx[None, :]) & same_chunk, BF16)
    stri = jnp.asarray(idx[None, :] < idx[:, None], BF16)
    lane_idx = np.arange(LANES)
    ut = jnp.asarray(lane_idx[:, None] < lane_idx[None, :], BF16)
    sel = jnp.asarray(np.arange(8)[:, None] == lane_idx[None, :], BF16)

    n_tiles = batch * nts
    tok = lambda g: (g, 0)
    tok_in = lambda g: (jnp.minimum(g, n_tiles - 1), 0)
    tok_next = lambda g: (jnp.minimum(g + 1, n_tiles - 1), 0)
    tok_prev = lambda g: (jnp.maximum(g - 1, 0), 0)
    tok_prev_t = lambda g: (0, jnp.maximum(g - 1, 0))
    c2 = lambda g: (0, 0)
    c3 = lambda g: (0, 0, 0)
    full = lambda a: pl.BlockSpec(a.shape, c2 if a.ndim == 2 else c3)
    consts = [params[n] for n in ("b_if", "b_if_t", "conv_q", "conv_k", "g_head", "w_pool", "pool_scale",
                                  "w_br_a", "w_br_b", "w_out", "g_ffn", "w_r", "b_r")]
    consts = [g_mix, w_all] + consts + [tri_c, tri_r, stri, ut, sel]
    nm, ng = d_pool + 4 * d_ml, 2 * D
    z_scratch = [pltpu.VMEM((ts, nm), BF16), pltpu.VMEM((ts, ng), BF16),
                 pltpu.VMEM((ts, LANES), F32), pltpu.VMEM((16, ts), F32)]
    return pl.pallas_call(
        functools.partial(_mixer_kernel, tiles_per_seq=nts),
        grid=(n_tiles + 1,),
        in_specs=[pl.BlockSpec((ts, D), tok_in),
                  pl.BlockSpec((ts, D), tok_next)] + [full(a) for a in consts],
        out_specs=[pl.BlockSpec((ts, D), tok),
                   pl.BlockSpec((ts, D), tok),
                   pl.BlockSpec((ts, LANES), tok_prev),
                   pl.BlockSpec((ts, LANES), tok_prev),
                   pl.BlockSpec((8, LANES), tok_prev),
                   pl.BlockSpec((8, ts), tok_prev_t)],
        out_shape=[jax.ShapeDtypeStruct((T + ts, D), BF16),
                   jax.ShapeDtypeStruct((T + ts, D), BF16),
                   jax.ShapeDtypeStruct((T, LANES), jnp.int32),
                   jax.ShapeDtypeStruct((T, LANES), F32),
                   jax.ShapeDtypeStruct((n_tiles * 8, LANES), jnp.int32),
                   jax.ShapeDtypeStruct((8, T), F32)],
        scratch_shapes=z_scratch + z_scratch + [
                        pltpu.VMEM((ts, D), BF16),
                        pltpu.VMEM(((d_pool + 2 * d_ml) // LANES, HALO + ts, LANES), F32),
                        pltpu.VMEM((ts, d_ml), BF16),
                        pltpu.VMEM((ts, d_ml), BF16),
                        pltpu.VMEM((ts, d_ml), F32),
                        pltpu.VMEM((ts, d_pool), BF16),
                        pltpu.VMEM((N_HEADS, dh, 2 * dh), F32),
                        pltpu.VMEM((8, LANES), F32),
                        pltpu.VMEM((ts, LANES), F32),
                        pltpu.VMEM((D, ng), BF16),
                        pltpu.VMEM(params["w_pool"].shape, BF16),
                        pltpu.VMEM((d_pool, D), BF16),
                        pltpu.VMEM((d_ml, D), BF16),
                        pltpu.VMEM((D, D), BF16)],
        compiler_params=_cparams(1),
        name="mixer",
    )(x2d, x2d, *consts)


def _for_each_piece(npieces_ref, glob_ref, tile, fn):
    base = tile * PIECES_PER_TILE
    n = npieces_ref[tile]

    def one(p):
        fn(pl.multiple_of(p * ROW_PIECE, ROW_PIECE), pl.multiple_of(glob_ref[base + p] * ROW_PIECE, ROW_PIECE))

    def group(g, carry):
        for u in range(PIECE_UNROLL):
            one(g * PIECE_UNROLL + u)
        return carry

    n_groups = lax.div(n, jnp.int32(PIECE_UNROLL))
    lax.fori_loop(0, n_groups, group, 0)
    for u in range(PIECE_UNROLL - 1):
        @pl.when(n_groups * PIECE_UNROLL + u < n)
        def _():
            one(n_groups * PIECE_UNROLL + u)


def _used_slot_groups(n_pieces):
    return lax.div(n_pieces * ROW_PIECE + (SLOT_SUB - 1), jnp.int32(SLOT_SUB))


def _dispatch_kernel(npieces_ref, glob_ref, pad_start_ref, pad_rows_ref, nused_ref,
                     xn_ref, srow_ref, buf_ref, rows_ref, zeros_ref, sem, zsem, tsem, *, n_steps):
    tt = xn_ref.shape[0] // MOE_STEP_TILES
    sl = rows_ref.shape[1] // MOE_STEP_TILES
    n_blocks = buf_ref.shape[0] // MOE_TM
    i = pl.program_id(0)
    cur = lax.rem(i, 2)

    def zero_fill_share(step, act):
        for k in range(-(-N_EXPERTS // n_steps)):
            e = step + k * n_steps

            @pl.when(e < N_EXPERTS)
            def _(e=e):
                e_c = jnp.minimum(e, N_EXPERTS - 1)
                rows = pad_rows_ref[e_c]
                off = pad_start_ref[e_c]
                size = MOE_TM // 2
                while size >= ROW_PIECE:
                    has = (rows & size) != 0

                    @pl.when(has)
                    def _(off=off, size=size):
                        act(pltpu.make_async_copy(zeros_ref.at[pl.ds(0, size)],
                                                  buf_ref.at[pl.ds(pl.multiple_of(off, ROW_PIECE), size)], zsem))
                    off = off + jnp.where(has, size, 0)
                    size //= 2
        for k in range(-(-n_blocks // n_steps)):
            b = step + k * n_steps

            @pl.when((b >= nused_ref[0]) & (b < n_blocks))
            def _(b=b):
                act(pltpu.make_async_copy(zeros_ref, buf_ref.at[pl.ds(b * MOE_TM, MOE_TM)], tsem))

    @pl.when(i == 0)
    def _():
        zeros_ref[...] = jnp.zeros_like(zeros_ref)

    zero_fill_share(i, lambda cp: cp.start())

    def piece_copy(buf_slot, sub, local_row, global_row):
        return pltpu.make_async_copy(rows_ref.at[buf_slot, pl.ds(sub * sl + local_row, ROW_PIECE)],
                                     buf_ref.at[pl.ds(global_row, ROW_PIECE)], sem.at[buf_slot])

    def for_step_pieces(step, buf_slot, act):
        for sub in range(MOE_STEP_TILES):
            _for_each_piece(npieces_ref, glob_ref, step * MOE_STEP_TILES + sub,
                            lambda l, g, sub=sub: act(piece_copy(buf_slot, sub, l, g)))

    @pl.when(i >= 2)
    def _():
        for_step_pieces(i - 2, cur, lambda cp: cp.wait())

    n_sub = _used_slot_groups(npieces_ref[i * MOE_STEP_TILES])
    for sub in range(1, MOE_STEP_TILES):
        n_sub = jnp.maximum(n_sub, _used_slot_groups(npieces_ref[i * MOE_STEP_TILES + sub]))
    for k in range(TOP_K * tt // SLOT_SUB, sl // SLOT_SUB + 1):
        @pl.when(n_sub == k)
        def _(k=k):
            m = k * SLOT_SUB
            r = lax.broadcasted_iota(jnp.int32, (m, tt), 0).astype(F32)
            for sub in range(MOE_STEP_TILES):
                sr = srow_ref[:, sub * tt:(sub + 1) * tt]
                slot1 = SLOT_RADIX * sr[0:1, :] + sr[1:2, :]
                slot2 = SLOT_RADIX * sr[2:3, :] + sr[3:4, :]
                sel = jnp.where((r == slot1) | (r == slot2), 1.0, 0.0).astype(BF16)
                rows_ref[cur, sub * sl:sub * sl + m, :] = _dot(sel, xn_ref[sub * tt:(sub + 1) * tt, :]).astype(BF16)
    for_step_pieces(i, cur, lambda cp: cp.start())

    @pl.when(i == n_steps - 1)
    def _():
        @pl.when(i >= 1)
        def _():
            for_step_pieces(i - 1, 1 - cur, lambda cp: cp.wait())
        for_step_pieces(i, cur, lambda cp: cp.wait())

        def wait_share(step, carry):
            zero_fill_share(step, lambda cp: cp.wait())
            return carry

        lax.fori_loop(0, n_steps, wait_share, 0)


def _dispatch(xn2, srow, npieces, piece_glob, pad_start, pad_rows, nused, n_rows):
    T, D = srow.shape[1], xn2.shape[1]
    tt = MOE_STEP_TILES * MIX_TS
    return pl.pallas_call(
        functools.partial(_dispatch_kernel, n_steps=T // tt),
        grid_spec=pltpu.PrefetchScalarGridSpec(
            num_scalar_prefetch=5,
            grid=(T // tt,),
            in_specs=[pl.BlockSpec((tt, D), lambda i, *_: (i, 0)),
                      pl.BlockSpec((8, tt), lambda i, *_: (0, i))],
            out_specs=pl.BlockSpec(memory_space=pl.ANY),
            scratch_shapes=[pltpu.VMEM((2, MOE_STEP_TILES * MOE_SL, D), BF16),
                            pltpu.VMEM((MOE_TM, D), BF16),
                            pltpu.SemaphoreType.DMA((2,)),
                            pltpu.SemaphoreType.DMA(()),
                            pltpu.SemaphoreType.DMA(())]),
        out_shape=jax.ShapeDtypeStruct((n_rows, D), BF16),
        compiler_params=_cparams(1),
        name="dispatch",
    )(npieces, piece_glob, pad_start, pad_rows, nused, xn2, srow)


def _experts_kernel(blk_e_ref, nused_ref, nsub_ref, first_ref, next_e_ref, slot_ref,
                    x_ref, wg_hbm, wu_hbm, wd_hbm, y_ref,
                    wg32_ref, wu32_ref, wd32_ref, wgb_ref, wub_ref, wdb_ref, sem):
    i = pl.program_id(0)
    used = i < nused_ref[0]
    n_sub = nsub_ref[i]
    landing = ((wg_hbm, wg32_ref), (wu_hbm, wu32_ref), (wd_hbm, wd32_ref))

    def weight_copies(e, s):
        return [pltpu.make_async_copy(hbm.at[e], vmem.at[s], sem.at[s, n]) for n, (hbm, vmem) in enumerate(landing)]

    @pl.when(used & (i == 0))
    def _():
        for cp in weight_copies(blk_e_ref[0], 0):
            cp.start()

    run_start = used & (first_ref[i] > 0)
    s = slot_ref[i]

    @pl.when(run_start)
    def _():
        for cp in weight_copies(blk_e_ref[i], s):
            cp.wait()

        @pl.when(next_e_ref[i] >= 0)
        def _():
            for cp in weight_copies(next_e_ref[i], 1 - s):
                cp.start()

    def swiglu(m, wg, wu, wd):
        x = x_ref[:m, :]
        hg = _dot(x, wg)
        hu = _dot(x, wu)
        hid = (_silu(hg) * hu).astype(BF16)
        y_ref[:m, :] = _dot(hid, wd).astype(BF16)
        if m < MOE_TM:
            y_ref[m:, :] = jnp.zeros((MOE_TM - m, y_ref.shape[1]), BF16)

    for k in range(1, MOE_TM // EXPERT_SUB + 1):
        @pl.when(run_start & (n_sub == k))
        def _(k=k):
            wg = wg32_ref[s].astype(BF16)
            wu = wu32_ref[s].astype(BF16)
            wd = wd32_ref[s].astype(BF16)
            wgb_ref[...] = wg
            wub_ref[...] = wu
            wdb_ref[...] = wd
            swiglu(k * EXPERT_SUB, wg, wu, wd)

        @pl.when(used & jnp.logical_not(run_start) & (n_sub == k))
        def _(k=k):
            swiglu(k * EXPERT_SUB, wgb_ref[...], wub_ref[...], wdb_ref[...])


def _experts(buf, blk_e, nused, nsub, run_first, next_e, run_slot, w_gate, w_up, w_down):
    R, D = buf.shape
    de = w_gate.shape[2]
    n_blocks = R // MOE_TM
    row_map = lambda i, be, nu, *_: (jnp.minimum(i, nu[0] - 1), 0)
    return pl.pallas_call(
        _experts_kernel,
        grid_spec=pltpu.PrefetchScalarGridSpec(
            num_scalar_prefetch=6,
            grid=(n_blocks,),
            in_specs=[pl.BlockSpec((MOE_TM, D), row_map),
                      pl.BlockSpec(memory_space=pl.ANY),
                      pl.BlockSpec(memory_space=pl.ANY),
                      pl.BlockSpec(memory_space=pl.ANY)],
            out_specs=pl.BlockSpec((MOE_TM, D), row_map),
            scratch_shapes=[pltpu.VMEM((2, D, de), F32), pltpu.VMEM((2, D, de), F32), pltpu.VMEM((2, de, D), F32),
                            pltpu.VMEM((D, de), BF16), pltpu.VMEM((D, de), BF16), pltpu.VMEM((de, D), BF16),
                            pltpu.SemaphoreType.DMA((2, 3))]),
        out_shape=jax.ShapeDtypeStruct((R, D), BF16),
        input_output_aliases={6: 0},
        compiler_params=_cparams(1),
        name="experts",
    )(blk_e, nused, nsub, run_first, next_e, run_slot, buf, w_gate, w_up, w_down)


def _combine_kernel(npieces_ref, glob_ref, x2_ref, rti_ref, rtf_ref, gfin_ref, yb_ref, out_ref,
                    rows_ref, sem):
    tt = x2_ref.shape[0] // MOE_STEP_TILES
    sl = rows_ref.shape[1] // MOE_STEP_TILES
    i = pl.program_id(0)
    n_steps = pl.num_programs(0)
    cur = lax.rem(i, 2)

    def piece_copy(buf_slot, sub, local_row, global_row):
        return pltpu.make_async_copy(yb_ref.at[pl.ds(global_row, ROW_PIECE)],
                                     rows_ref.at[buf_slot, pl.ds(sub * sl + local_row, ROW_PIECE)], sem.at[buf_slot])

    def for_step_pieces(step, buf_slot, act):
        for sub in range(MOE_STEP_TILES):
            _for_each_piece(npieces_ref, glob_ref, step * MOE_STEP_TILES + sub,
                            lambda l, g, sub=sub: act(piece_copy(buf_slot, sub, l, g)))

    @pl.when(i == 0)
    def _():
        rows_ref[...] = jnp.zeros_like(rows_ref)
        for_step_pieces(0, 0, lambda cp: cp.start())

    @pl.when(i + 1 < n_steps)
    def _():
        for_step_pieces(i + 1, 1 - cur, lambda cp: cp.start())

    for_step_pieces(i, cur, lambda cp: cp.wait())

    n_sub = _used_slot_groups(npieces_ref[i * MOE_STEP_TILES])
    for sub in range(1, MOE_STEP_TILES):
        n_sub = jnp.maximum(n_sub, _used_slot_groups(npieces_ref[i * MOE_STEP_TILES + sub]))
    for k in range(TOP_K * tt // SLOT_SUB, sl // SLOT_SUB + 1):
        @pl.when(n_sub == k)
        def _(k=k):
            m = k * SLOT_SUB
            lane = lax.broadcasted_iota(jnp.int32, (tt, m), 1)
            for sub in range(MOE_STEP_TILES):
                ts_rows = slice(sub * tt, (sub + 1) * tt)
                rti = rti_ref[ts_rows, :]
                rtf = rtf_ref[ts_rows, :]
                g = jnp.where(lane == rti[:, 2:3], rtf[:, 0:1],
                              jnp.where(lane == rti[:, 3:4], rtf[:, 1:2], 0.0)).astype(BF16)
                y = x2_ref[ts_rows, :].astype(F32) + _dot(g, rows_ref[cur, sub * sl:sub * sl + m, :])
                ms = jnp.mean(y * y, axis=-1, keepdims=True)
                out_ref[ts_rows, :] = y * lax.rsqrt(ms + EPS) * gfin_ref[...]


def _combine(x2, rti, rtf, g_final, yb, npieces, piece_glob):
    T, D = rti.shape[0], x2.shape[1]
    tt = MOE_STEP_TILES * MIX_TS
    tok = lambda i, *_: (i, 0)
    return pl.pallas_call(
        _combine_kernel,
        grid_spec=pltpu.PrefetchScalarGridSpec(
            num_scalar_prefetch=2,
            grid=(T // tt,),
            in_specs=[pl.BlockSpec((tt, D), tok),
                      pl.BlockSpec((tt, LANES), tok),
                      pl.BlockSpec((tt, LANES), tok),
                      pl.BlockSpec((1, D), lambda i, *_: (0, 0)),
                      pl.BlockSpec(memory_space=pl.ANY)],
            out_specs=pl.BlockSpec((tt, D), tok),
            scratch_shapes=[pltpu.VMEM((2, MOE_STEP_TILES * MOE_SL, D), BF16),
                            pltpu.SemaphoreType.DMA((2,))]),
        out_shape=jax.ShapeDtypeStruct((T, D), F32),
        compiler_params=_cparams(1),
        name="combine",
    )(npieces, piece_glob, x2, rti, rtf, g_final, yb)


def _pad_lanes(a, width=LANES):
    return jnp.pad(a, ((0, 0), (0, width - a.shape[1])))


def kernel(x, g_mix, w_in, b_if, conv_q, conv_k, g_head, w_pool, pool_scale, w_br_a, w_br_b, w_out,
           g_ffn, w_rg, b_rg, w_re, b_re, w_e_gate, w_e_up, w_e_down, g_final):
    B, S, D = x.shape
    T = B * S
    assert g_mix.shape[0] == 1, "single-layer block"
    assert S % MIX_TS == 0 and (T // MIX_TS) % MOE_STEP_TILES == 0
    d_pool = w_br_a.shape[1]
    d_ml = w_br_b.shape[1]
    x2d = x.reshape(T, D)

    params = {
        "b_if": _pad_lanes(b_if[0][None, :]),
        "b_if_t": jnp.pad(b_if[0][:, None], ((0, 16 - 2 * N_HEADS), (0, 0))),
        "conv_q": conv_q[0], "conv_k": conv_k[0],
        "g_head": g_head[0][None, :],
        "w_pool": w_pool[0],
        "pool_scale": pool_scale[0][None, :],
        "w_br_a": w_br_a[0], "w_br_b": w_br_b[0],
        "w_out": w_out[0],
        "g_ffn": g_ffn[0][None, :],
        "w_r": _pad_lanes(jnp.concatenate([w_rg[0], w_re[0]], axis=1)).astype(BF16),
        "b_r": _pad_lanes(jnp.concatenate([b_rg[0], b_re[0]])[None, :]),
    }

    x2, xn2, rti, rtf, tstat, srow = _mixer(x2d, g_mix[0][None, :], w_in[0].astype(BF16), params, B, S)

    n_tiles = T // MIX_TS
    n_rows = n_tiles * MOE_SL + N_EXPERTS * MOE_TM
    n_blocks = n_rows // MOE_TM
    i32 = lambda a: a.astype(jnp.int32)
    mm = lambda a, b: jnp.round(jnp.dot(a, b, precision=lax.Precision.HIGHEST, preferred_element_type=F32))
    e_ids = np.arange(N_EXPERTS)
    t_ids = np.arange(n_tiles)
    b_ids = np.arange(n_blocks)
    cum_e = jnp.asarray(e_ids[:, None] <= e_ids[None, :], F32)
    cum_t = jnp.asarray(t_ids[:, None] >= t_ids[None, :], F32)
    cum_b = jnp.asarray(b_ids[:, None] <= b_ids[None, :], F32)
    e_row = jnp.asarray(e_ids[None, :], F32)

    pcs = tstat.reshape(n_tiles, 8, LANES)[:, 0, ROUTER_LANE0:ROUTER_LANE0 + N_EXPERTS].astype(F32)
    piece_end = mm(pcs, cum_e)
    piece_loc = piece_end - pcs
    tile_cum = mm(cum_t, pcs)
    rows_e = tile_cum[-1:, :] * ROW_PIECE
    padded = jnp.floor((rows_e + (MOE_TM - 1)) * (1.0 / MOE_TM)) * MOE_TM
    pend = mm(padded, cum_e)
    poff = pend - padded
    piece_glob = poff * (1.0 / ROW_PIECE) + tile_cum - pcs
    nused_f = pend[0, -1] * (1.0 / MOE_TM)
    blk_start = jnp.asarray(b_ids[:, None] * MOE_TM, F32)
    blk_e_f = jnp.minimum(jnp.sum((pend <= blk_start).astype(F32), axis=1, keepdims=True), N_EXPERTS - 1.0)
    blk_oh = (blk_e_f == e_row).astype(F32)
    later_nonempty = (e_ids[None, :] > e_ids[:, None]) & (rows_e > 0)
    next_of_e = jnp.min(jnp.where(later_nonempty, e_row, float(N_EXPERTS)), axis=1)
    next_of_e = jnp.where(next_of_e == N_EXPERTS, -1.0, next_of_e)
    per_blk = mm(blk_oh, jnp.stack([pend[0], (poff + rows_e)[0], next_of_e], axis=1))
    blk_used = blk_start < pend[0, -1]
    rows_in_blk = jnp.clip(per_blk[:, 1:2] - blk_start, 0.0, float(MOE_TM))
    nsub = jnp.floor((rows_in_blk + (EXPERT_SUB - 1)) * (1.0 / EXPERT_SUB))
    prev_e = jnp.concatenate([jnp.full((1, 1), -1.0, F32), blk_e_f[:-1]], axis=0)
    run_first = (blk_used & (blk_e_f != prev_e)).astype(F32)
    run_idx = mm(run_first.reshape(1, n_blocks), cum_b) - 1.0
    run_slot = run_idx - 2.0 * jnp.floor(run_idx * 0.5)
    p_ids = jnp.asarray(np.arange(PIECES_PER_TILE), F32)
    e_of_p = jnp.minimum(jnp.sum((piece_end[:, None, :] <= p_ids[None, :, None]).astype(F32), axis=2),
                         N_EXPERTS - 1.0)
    shift = jnp.sum(jnp.where(e_of_p[:, :, None] == e_row[None], (piece_glob - piece_loc)[:, None, :], 0.0), axis=2)
    glob_of_p = i32(shift + p_ids[None, :]).reshape(n_tiles * PIECES_PER_TILE)
    npieces = i32(piece_end[:, -1])
    flat_b = lambda a: i32(a).reshape(n_blocks)
    blk_e, nused = flat_b(blk_e_f), i32(nused_f).reshape(1)

    flat_e = lambda a: i32(a).reshape(N_EXPERTS)
    buf = _dispatch(xn2, srow, npieces, glob_of_p, flat_e(poff + rows_e), flat_e(padded - rows_e), nused, n_rows)
    yb = _experts(buf, blk_e, nused, flat_b(nsub), flat_b(run_first), flat_b(per_blk[:, 2:3]), flat_b(run_slot),
                  w_e_gate[0], w_e_up[0], w_e_down[0])
    out = _combine(x2, rti, rtf, g_final[None, :], yb, npieces, glob_of_p)
    return out.reshape(B, S, D)
```

```python
import functools

import numpy as np
import jax
import jax.numpy as jnp
from jax import lax
from jax.experimental import pallas as pl
from jax.experimental.pallas import tpu as pltpu

F32 = jnp.float32
BF16 = jnp.bfloat16

CHUNK = 64
POOL_WINDOWS = (2, 4, 8, 16)
N_HEADS = 4
CONV_K = 4
N_GROUPS = 4
EXPERTS_PER_GROUP = 8
N_EXPERTS = N_GROUPS * EXPERTS_PER_GROUP
TOP_K = 2
EPS = 1e-6

LANES = 128
HALO = 16
ROUTER_LANE0 = N_GROUPS

INPROJ_TN = 256
MIX_TS = 256
MOE_TM = 512
ROW_PIECE = 16
MOE_SL = TOP_K * MIX_TS + N_EXPERTS * ROW_PIECE
PIECES_PER_TILE = MOE_SL // ROW_PIECE
MOE_STEP_TILES = 4
SLOT_SUB = 128
PIECE_UNROLL = 8
EXPERT_SUB = 128
SLOT_RADIX = 16
VMEM_LIMIT = 56 * 1024 * 1024


def _cparams(n_axes):
    return pltpu.CompilerParams(dimension_semantics=("arbitrary",) * n_axes,
                                vmem_limit_bytes=VMEM_LIMIT)


def _sigmoid(v):
    return 0.5 * jnp.tanh(0.5 * v) + 0.5


def _silu(v):
    return v * _sigmoid(v)


def _log_sigmoid(v):
    return jnp.minimum(v, 0.0) - jnp.log1p(jnp.exp(-jnp.abs(v)))


def _split3(v):
    hi = v.astype(BF16)
    r1 = v - hi.astype(F32)
    mid = r1.astype(BF16)
    lo = (r1 - mid.astype(F32)).astype(BF16)
    return hi, mid, lo


def _dot(a, b):
    return jnp.dot(a, b, preferred_element_type=F32)


def _dot_nt(a, b):
    return lax.dot_general(a, b, (((1,), (1,)), ((), ())), preferred_element_type=F32)


def _dot_tn(a, b):
    return lax.dot_general(a, b, (((0,), (0,)), ((), ())), preferred_element_type=F32)


def _inproj_steps(x_ref, g_ref, w_refs, z_refs, xn_ref):
    w_ref_all, wg_ref = w_refs
    zm_ref, zg_ref, zif_ref, zift_ref = z_refs
    n_main = zm_ref.shape[1]

    def norm():
        x = x_ref[...]
        ms = jnp.mean(x * x, axis=-1, keepdims=True)
        xn_ref[...] = (x * lax.rsqrt(ms + EPS) * g_ref[...]).astype(BF16)

    def block(w_ref, z_ref, c0):
        def run():
            cols = slice(c0, c0 + INPROJ_TN)
            z_ref[:, cols] = _dot(xn_ref[...], w_ref[:, cols]).astype(BF16)
        return run

    def gates():
        zif = _dot(xn_ref[...], w_ref_all[:, n_main:n_main + LANES])
        zif_ref[...] = zif
        zift_ref[...] = zif.T[:zift_ref.shape[0], :]

    steps = [norm, gates]
    steps += [block(w_ref_all, zm_ref, c0) for c0 in range(0, n_main, INPROJ_TN)]
    steps += [block(wg_ref, zg_ref, c0) for c0 in range(0, zg_ref.shape[1], INPROJ_TN)]
    return steps


def _mixer_kernel(x_ref, xnext_ref, gmix_ref, win_ref,
                  bif_ref, bift_ref, convq_ref, convk_ref, ghead_ref, wpool32_ref, pscale_ref,
                  wa32_ref, wb32_ref, wo32_ref, gffn_ref, wr_ref, br_ref,
                  tric_ref, trir_ref, stri_ref, ut_ref, sel_ref,
                  x2_ref, xn2_ref, rti_ref, rtf_ref, tstat_ref, srow_ref,
                  zm_ref, zg_ref, zif_ref, zift_ref, zm_nxt, zg_nxt, zif_nxt, zift_nxt, xn_ref,
                  ext_ref, q_ref, k_ref, h_ref, pool_ref, cst_ref, mst_ref, lg_ref, wg_ref,
                  wpool_ref, wa_ref, wb_ref, wo_ref, *, tiles_per_seq):
    ts = x_ref.shape[0]
    d_pool = wa_ref.shape[0]
    d_ml = wb_ref.shape[0]
    dh = d_ml // N_HEADS
    n_chunks = ts // CHUNK
    g_step = pl.program_id(0)
    j = lax.rem(g_step, tiles_per_seq)
    w_in_refs = (win_ref, wg_ref)
    z_cur = (zm_ref, zg_ref, zif_ref, zift_ref)
    z_nxt = (zm_nxt, zg_nxt, zif_nxt, zift_nxt)
    first = g_step == 0

    @pl.when(first)
    def _():
        g0 = zm_ref.shape[1] + 2 * N_HEADS
        wg_ref[...] = win_ref[:, g0:g0 + wg_ref.shape[1]]
        for dst, src in ((wpool_ref, wpool32_ref), (wa_ref, wa32_ref), (wb_ref, wb32_ref), (wo_ref, wo32_ref)):
            dst[...] = src[...].astype(BF16)
        for step in _inproj_steps(x_ref, gmix_ref, w_in_refs, z_cur, xn_ref):
            step()
        lg_ref[...] = jnp.zeros_like(lg_ref)

    @pl.when(jnp.logical_not(first))
    def _():
        for dst, src in zip(z_cur, z_nxt):
            dst[...] = src[...]

    @pl.when(j == 0)
    def _():
        ext_ref[:, :HALO, :] = jnp.zeros((ext_ref.shape[0], HALO, LANES), F32)
        cst_ref[...] = jnp.zeros_like(cst_ref)
        mst_ref[...] = jnp.zeros_like(mst_ref)

    pending = _inproj_steps(xnext_ref, gmix_ref, w_in_refs, z_nxt, xn_ref)

    def project_some(n=1):
        for _ in range(min(n, len(pending))):
            pending.pop(0)()

    project_some(2)

    routed = _route_select(lg_ref[...])

    row = lax.broadcasted_iota(jnp.int32, (ts, LANES), 0)
    pos1 = (row + j * ts + 1).astype(F32)

    def history(cg):
        cur = zm_ref[:, cg * LANES:(cg + 1) * LANES].astype(F32)
        ext_ref[cg, HALO:, :] = cur
        return cur, lambda s: ext_ref[cg, HALO - s:HALO - s + ts, :]

    def keep_history(cg, cur):
        ext_ref[cg, :HALO, :] = cur[ts - HALO:, :]

    n_pool_groups = d_pool // LANES
    for g in range(n_pool_groups):
        w = POOL_WINDOWS[g]
        cur, shifted = history(g)
        win = cur
        for s in range(1, w):
            win = win + shifted(s)
        keep_history(g, cur)
        cnt = jnp.minimum(pos1, float(w))
        d = win / cnt - cur
        y = _dot(d.astype(BF16), wpool_ref[g]) * pscale_ref[:, g * LANES:(g + 1) * LANES]
        pool_ref[:, g * LANES:(g + 1) * LANES] = y.astype(BF16)
        project_some()

    n_ml_groups = d_ml // LANES
    for which, (cw_ref, dst_ref, scale) in enumerate(((convq_ref, q_ref, 1.0), (convk_ref, k_ref, dh ** -0.5))):
        for g in range(n_ml_groups):
            cols = slice(g * LANES, (g + 1) * LANES)
            cg = n_pool_groups + which * n_ml_groups + g
            cur, shifted = history(cg)
            acc = cur * cw_ref[CONV_K - 1:CONV_K, cols]
            for sft in range(1, CONV_K):
                acc = acc + shifted(sft) * cw_ref[CONV_K - 1 - sft:CONV_K - sft, cols]
            keep_history(cg, cur)
            dst_ref[:, cols] = (_silu(acc) * scale).astype(BF16)
        project_some()

    _route_slots(routed, stri_ref, ut_ref, sel_ref, rti_ref, rtf_ref, tstat_ref, srow_ref)
    project_some()

    zc = zif_ref[...] + bif_ref[...]
    lf_c = _log_sigmoid(zc)
    bc = sum(_dot(tric_ref[...], p) for p in _split3(lf_c))
    zr = zift_ref[...] + bift_ref[...]
    lf_r = _log_sigmoid(zr)
    br = sum(_dot(p, trir_ref[...]) for p in _split3(lf_r))
    project_some(2)

    ti = lax.broadcasted_iota(jnp.int32, (CHUNK, CHUNK), 0)
    si = lax.broadcasted_iota(jnp.int32, (CHUNK, CHUNK), 1)
    causal = si <= ti
    ones_blk = jnp.ones((CHUNK, dh), BF16)
    v0 = d_pool + 2 * d_ml
    ig_rep = [jnp.broadcast_to(zc[:, h:h + 1], (ts, dh)) for h in range(N_HEADS)]
    bt_rep = [jnp.broadcast_to(bc[:, N_HEADS + h:N_HEADS + h + 1], (ts, dh)) for h in range(N_HEADS)]

    m_state = [mst_ref[h:h + 1, :] for h in range(N_HEADS)]
    c_state = [cst_ref[h] for h in range(N_HEADS)]
    def stage_scores(c):
        rs = slice(c * CHUNK, (c + 1) * CHUNK)
        out = []
        for h in range(N_HEADS):
            hs = slice(h * dh, (h + 1) * dh)
            q = q_ref[rs, hs]
            k = k_ref[rs, hs]
            bt = bt_rep[h][rs, :]
            r_row = zr[h:h + 1, rs] - br[N_HEADS + h:N_HEADS + h + 1, rs]
            dmat = jnp.where(causal, bt[:, :CHUNK] + r_row, -jnp.inf)
            out.append(dict(q=q, k=k, bt=bt, dmat=dmat, qk=_dot_nt(q, k),
                            m_intra=jnp.max(dmat, axis=-1, keepdims=True)))
        return out

    def stage_state(c, st):
        rs = slice(c * CHUNK, (c + 1) * CHUNK)
        for h in range(N_HEADS):
            s = st[h]
            bt, k = s["bt"], s["k"]
            m_prev, c_prev = m_state[h], c_state[h]
            v_aug = jnp.concatenate([zm_ref[rs, v0 + h * dh:v0 + (h + 1) * dh], ones_blk], axis=-1)
            igc = ig_rep[h][rs, :]
            b_last = bt[CHUNK - 1:CHUNK, :]
            a_log = b_last - bt + igc
            a_max = jnp.max(a_log, axis=0, keepdims=True)
            m_new = jnp.maximum(b_last + m_prev, a_max)
            kw = (k.astype(F32) * jnp.exp(a_log - m_new)).astype(BF16)
            decay = jnp.exp(b_last + m_prev - m_new)
            s.update(v_aug=v_aug, m_prev=m_prev, qc=_dot(s["q"], c_prev.astype(BF16)))
            c_state[h] = jnp.concatenate([decay, decay], axis=-1) * c_prev + _dot_tn(kw, v_aug)
            m_state[h] = m_new

    def stage_values(c, st):
        rs = slice(c * CHUNK, (c + 1) * CHUNK)
        for h in range(N_HEADS):
            s = st[h]
            hs = slice(h * dh, (h + 1) * dh)
            inter = s["bt"] + s["m_prev"]
            m_t = jnp.maximum(inter, s["m_intra"])
            w_inter = jnp.exp(inter - m_t)
            smat = s["qk"] * jnp.exp(s["dmat"] - m_t[:, :CHUNK])
            sv = _dot(smat.astype(BF16), s["v_aug"])
            qc = s["qc"]
            nq = w_inter * qc[:, dh:] + sv[:, dh:]
            den = jnp.maximum(jnp.abs(nq), jnp.exp(-m_t))
            h_ref[rs, hs] = (w_inter * qc[:, :dh] + sv[:, :dh]) / den

    staged = stage_scores(0)
    for c in range(n_chunks):
        stage_state(c, staged)
        project_some()
        nxt = stage_scores(c + 1) if c + 1 < n_chunks else None
        project_some()
        stage_values(c, staged)
        staged = nxt
    for h in range(N_HEADS):
        cst_ref[h] = c_state[h]
        mst_ref[h:h + 1, :] = m_state[h]

    o0 = v0 + d_ml
    for h in range(N_HEADS):
        hs = slice(h * dh, (h + 1) * dh)
        hv = h_ref[:, hs]
        mu = jnp.mean(hv, axis=-1, keepdims=True)
        hc = hv - mu
        var = jnp.mean(hc * hc, axis=-1, keepdims=True)
        hn = hc * lax.rsqrt(var + EPS) * ghead_ref[:, hs]
        og = _sigmoid(zm_ref[:, o0 + h * dh:o0 + (h + 1) * dh].astype(F32))
        q_ref[:, hs] = (og * hn).astype(BF16)
    y_a = _dot(pool_ref[...], wa_ref[...])
    y_b = _dot(q_ref[...], wb_ref[...])
    d_model = x_ref.shape[1]
    ga = _sigmoid(zg_ref[:, :d_model].astype(F32))
    gb = _sigmoid(zg_ref[:, d_model:].astype(F32))
    merged = (ga * y_a + gb * y_b).astype(BF16)
    x2 = x_ref[...] + _dot(merged, wo_ref[...])
    x2_ref[...] = x2.astype(BF16)
    project_some(len(pending))

    ms = jnp.mean(x2 * x2, axis=-1, keepdims=True)
    xn2 = x2 * lax.rsqrt(ms + EPS) * gffn_ref[...]
    xh = xn2.astype(BF16)
    xn2_ref[...] = xh
    lg_ref[...] = _dot(xh, wr_ref[...]) + br_ref[...]
    project_some(len(pending))


def _route_select(lg):
    ts = lg.shape[0]
    lane = lax.broadcasted_iota(jnp.int32, (ts, LANES), 1)
    lanef = lane.astype(F32)
    big = float(4 * LANES)
    gl = jnp.where(lane < N_GROUPS, lg, -jnp.inf)
    gmax = jnp.max(gl, axis=-1, keepdims=True)
    g_sel = jnp.min(jnp.where(gl == gmax, lanef, big), axis=-1, keepdims=True)
    p_g = 1.0 / jnp.sum(jnp.exp(gl - gmax), axis=-1, keepdims=True)
    lo = ROUTER_LANE0 + EXPERTS_PER_GROUP * g_sel
    el = jnp.where((lanef >= lo) & (lanef < lo + EXPERTS_PER_GROUP), lg, -jnp.inf)
    m1 = jnp.max(el, axis=-1, keepdims=True)
    i1 = jnp.min(jnp.where(el == m1, lanef, big), axis=-1, keepdims=True)
    el2 = jnp.where(lanef == i1, -jnp.inf, el)
    m2 = jnp.max(el2, axis=-1, keepdims=True)
    i2 = jnp.min(jnp.where(el2 == m2, lanef, big), axis=-1, keepdims=True)
    e2x = jnp.exp(m2 - m1)
    gate1 = p_g / (1.0 + e2x)
    gate2 = p_g * e2x / (1.0 + e2x)
    return dict(lane=lane, i1=i1, i2=i2, gate1=gate1, gate2=gate2, oh1=lanef == i1, oh2=lanef == i2)


def _route_slots(r, stri_ref, ut_ref, sel_ref, rti_ref, rtf_ref, tstat_ref, srow_ref):
    lane, oh1, oh2, i1, i2 = r["lane"], r["oh1"], r["oh2"], r["i1"], r["i2"]
    ohs = jnp.where(oh1 | oh2, 1.0, 0.0)
    n_loc = jnp.sum(ohs, axis=0, keepdims=True)
    pieces = jnp.floor((n_loc + (ROW_PIECE - 1.0)) * (1.0 / ROW_PIECE))
    piece_off = _dot(jnp.broadcast_to(pieces, (8, LANES)).astype(BF16), ut_ref[...])[0:1, :]
    base = _dot(stri_ref[...], ohs.astype(BF16)) + ROW_PIECE * piece_off
    slot1 = jnp.sum(jnp.where(oh1, base, 0.0), axis=-1, keepdims=True)
    slot2 = jnp.sum(jnp.where(oh2, base, 0.0), axis=-1, keepdims=True)
    tstat_ref[...] = jnp.broadcast_to(pieces, tstat_ref.shape).astype(jnp.int32)

    rti = jnp.where(lane == 0, i1 - ROUTER_LANE0,
                    jnp.where(lane == 1, i2 - ROUTER_LANE0,
                              jnp.where(lane == 2, slot1, jnp.where(lane == 3, slot2, 0.0))))
    rti_ref[...] = rti.astype(jnp.int32)
    rtf_ref[...] = jnp.where(lane == 0, r["gate1"], jnp.where(lane == 1, r["gate2"], 0.0))
    h1 = jnp.floor(slot1 * (1.0 / SLOT_RADIX))
    h2 = jnp.floor(slot2 * (1.0 / SLOT_RADIX))
    parts = jnp.where(lane == 0, h1, jnp.where(lane == 1, slot1 - SLOT_RADIX * h1,
                      jnp.where(lane == 2, h2, jnp.where(lane == 3, slot2 - SLOT_RADIX * h2, 0.0))))
    srow_ref[...] = _dot_nt(sel_ref[...], parts.astype(BF16))


def _mixer(x2d, g_mix, w_all, params, batch, seq):
    T, D = x2d.shape
    ts = min(MIX_TS, seq)
    nts = seq // ts
    d_pool = params["w_br_a"].shape[0]
    d_ml = params["w_br_b"].shape[0]
    dh = d_ml // N_HEADS

    idx = np.arange(ts)
    same_chunk = (idx[:, None] // CHUNK) == (idx[None, :] // CHUNK)
    tri_c = jnp.asarray((idx[None, :] <= idx[:, None]) & same_chunk, BF16)
    tri_r = jnp.asarray((idx[:, None] <= idx[None, :]) & same_chunk, BF16)
    stri = jnp.asarray(idx[None, :] < idx[:, None], BF16)
    lane_idx = np.arange(LANES)
    ut = jnp.asarray(lane_idx[:, None] < lane_idx[None, :], BF16)
    sel = jnp.asarray(np.arange(8)[:, None] == lane_idx[None, :], BF16)

    n_tiles = batch * nts
    tok = lambda g: (g, 0)
    tok_in = lambda g: (jnp.minimum(g, n_tiles - 1), 0)
    tok_next = lambda g: (jnp.minimum(g + 1, n_tiles - 1), 0)
    tok_prev = lambda g: (jnp.maximum(g - 1, 0), 0)
    tok_prev_t = lambda g: (0, jnp.maximum(g - 1, 0))
    c2 = lambda g: (0, 0)
    c3 = lambda g: (0, 0, 0)
    full = lambda a: pl.BlockSpec(a.shape, c2 if a.ndim == 2 else c3)
    consts = [params[n] for n in ("b_if", "b_if_t", "conv_q", "conv_k", "g_head", "w_pool", "pool_scale",
                                  "w_br_a", "w_br_b", "w_out", "g_ffn", "w_r", "b_r")]
    consts = [g_mix, w_all] + consts + [tri_c, tri_r, stri, ut, sel]
    nm, ng = d_pool + 4 * d_ml, 2 * D
    z_scratch = [pltpu.VMEM((ts, nm), BF16), pltpu.VMEM((ts, ng), BF16),
                 pltpu.VMEM((ts, LANES), F32), pltpu.VMEM((16, ts), F32)]
    return pl.pallas_call(
        functools.partial(_mixer_kernel, tiles_per_seq=nts),
        grid=(n_tiles + 1,),
        in_specs=[pl.BlockSpec((ts, D), tok_in),
                  pl.BlockSpec((ts, D), tok_next)] + [full(a) for a in consts],
        out_specs=[pl.BlockSpec((ts, D), tok),
                   pl.BlockSpec((ts, D), tok),
                   pl.BlockSpec((ts, LANES), tok_prev),
                   pl.BlockSpec((ts, LANES), tok_prev),
                   pl.BlockSpec((8, LANES), tok_prev),
                   pl.BlockSpec((8, ts), tok_prev_t)],
        out_shape=[jax.ShapeDtypeStruct((T + ts, D), BF16),
                   jax.ShapeDtypeStruct((T + ts, D), BF16),
                   jax.ShapeDtypeStruct((T, LANES), jnp.int32),
                   jax.ShapeDtypeStruct((T, LANES), F32),
                   jax.ShapeDtypeStruct((n_tiles * 8, LANES), jnp.int32),
                   jax.ShapeDtypeStruct((8, T), F32)],
        scratch_shapes=z_scratch + z_scratch + [
                        pltpu.VMEM((ts, D), BF16),
                        pltpu.VMEM(((d_pool + 2 * d_ml) // LANES, HALO + ts, LANES), F32),
                        pltpu.VMEM((ts, d_ml), BF16),
                        pltpu.VMEM((ts, d_ml), BF16),
                        pltpu.VMEM((ts, d_ml), F32),
                        pltpu.VMEM((ts, d_pool), BF16),
                        pltpu.VMEM((N_HEADS, dh, 2 * dh), F32),
                        pltpu.VMEM((8, LANES), F32),
                        pltpu.VMEM((ts, LANES), F32),
                        pltpu.VMEM((D, ng), BF16),
                        pltpu.VMEM(params["w_pool"].shape, BF16),
                        pltpu.VMEM((d_pool, D), BF16),
                        pltpu.VMEM((d_ml, D), BF16),
                        pltpu.VMEM((D, D), BF16)],
        compiler_params=_cparams(1),
        name="mixer",
    )(x2d, x2d, *consts)


def _for_each_piece(npieces_ref, glob_ref, tile, fn):
    base = tile * PIECES_PER_TILE
    n = npieces_ref[tile]

    def one(p):
        fn(pl.multiple_of(p * ROW_PIECE, ROW_PIECE), pl.multiple_of(glob_ref[base + p] * ROW_PIECE, ROW_PIECE))

    def group(g, carry):
        for u in range(PIECE_UNROLL):
            one(g * PIECE_UNROLL + u)
        return carry

    n_groups = lax.div(n, jnp.int32(PIECE_UNROLL))
    lax.fori_loop(0, n_groups, group, 0)
    for u in range(PIECE_UNROLL - 1):
        @pl.when(n_groups * PIECE_UNROLL + u < n)
        def _():
            one(n_groups * PIECE_UNROLL + u)


def _used_slot_groups(n_pieces):
    return lax.div(n_pieces * ROW_PIECE + (SLOT_SUB - 1), jnp.int32(SLOT_SUB))


def _dispatch_kernel(npieces_ref, glob_ref, pad_start_ref, pad_rows_ref, nused_ref,
                     xn_ref, srow_ref, buf_ref, rows_ref, zeros_ref, sem, zsem, tsem, *, n_steps):
    tt = xn_ref.shape[0] // MOE_STEP_TILES
    sl = rows_ref.shape[1] // MOE_STEP_TILES
    n_blocks = buf_ref.shape[0] // MOE_TM
    i = pl.program_id(0)
    cur = lax.rem(i, 2)

    def zero_fill_share(step, act):
        for k in range(-(-N_EXPERTS // n_steps)):
            e = step + k * n_steps

            @pl.when(e < N_EXPERTS)
            def _(e=e):
                e_c = jnp.minimum(e, N_EXPERTS - 1)
                rows = pad_rows_ref[e_c]
                off = pad_start_ref[e_c]
                size = MOE_TM // 2
                while size >= ROW_PIECE:
                    has = (rows & size) != 0

                    @pl.when(has)
                    def _(off=off, size=size):
                        act(pltpu.make_async_copy(zeros_ref.at[pl.ds(0, size)],
                                                  buf_ref.at[pl.ds(pl.multiple_of(off, ROW_PIECE), size)], zsem))
                    off = off + jnp.where(has, size, 0)
                    size //= 2
        for k in range(-(-n_blocks // n_steps)):
            b = step + k * n_steps

            @pl.when((b >= nused_ref[0]) & (b < n_blocks))
            def _(b=b):
                b_in = jnp.minimum(b, n_blocks - 1)
                act(pltpu.make_async_copy(zeros_ref, buf_ref.at[pl.ds(b_in * MOE_TM, MOE_TM)], tsem))

    @pl.when(i == 0)
    def _():
        zeros_ref[...] = jnp.zeros_like(zeros_ref)

    zero_fill_share(i, lambda cp: cp.start())

    def piece_copy(buf_slot, sub, local_row, global_row):
        return pltpu.make_async_copy(rows_ref.at[buf_slot, pl.ds(sub * sl + local_row, ROW_PIECE)],
                                     buf_ref.at[pl.ds(global_row, ROW_PIECE)], sem.at[buf_slot])

    def for_step_pieces(step, buf_slot, act):
        for sub in range(MOE_STEP_TILES):
            _for_each_piece(npieces_ref, glob_ref, step * MOE_STEP_TILES + sub,
                            lambda l, g, sub=sub: act(piece_copy(buf_slot, sub, l, g)))

    @pl.when(i >= 2)
    def _():
        for_step_pieces(i - 2, cur, lambda cp: cp.wait())

    n_sub = _used_slot_groups(npieces_ref[i * MOE_STEP_TILES])
    for sub in range(1, MOE_STEP_TILES):
        n_sub = jnp.maximum(n_sub, _used_slot_groups(npieces_ref[i * MOE_STEP_TILES + sub]))
    for k in range(TOP_K * tt // SLOT_SUB, sl // SLOT_SUB + 1):
        @pl.when(n_sub == k)
        def _(k=k):
            m = k * SLOT_SUB
            r = lax.broadcasted_iota(jnp.int32, (m, tt), 0).astype(F32)
            for sub in range(MOE_STEP_TILES):
                sr = srow_ref[:, sub * tt:(sub + 1) * tt]
                slot1 = SLOT_RADIX * sr[0:1, :] + sr[1:2, :]
                slot2 = SLOT_RADIX * sr[2:3, :] + sr[3:4, :]
                sel = jnp.where((r == slot1) | (r == slot2), 1.0, 0.0).astype(BF16)
                rows_ref[cur, sub * sl:sub * sl + m, :] = _dot(sel, xn_ref[sub * tt:(sub + 1) * tt, :]).astype(BF16)
    for_step_pieces(i, cur, lambda cp: cp.start())

    @pl.when(i == n_steps - 1)
    def _():
        @pl.when(i >= 1)
        def _():
            for_step_pieces(i - 1, 1 - cur, lambda cp: cp.wait())
        for_step_pieces(i, cur, lambda cp: cp.wait())

        def wait_share(step, carry):
            zero_fill_share(step, lambda cp: cp.wait())
            return carry

        lax.fori_loop(0, n_steps, wait_share, 0)


def _dispatch(xn2, srow, npieces, piece_glob, pad_start, pad_rows, nused, n_rows):
    T, D = srow.shape[1], xn2.shape[1]
    tt = MOE_STEP_TILES * MIX_TS
    return pl.pallas_call(
        functools.partial(_dispatch_kernel, n_steps=T // tt),
        grid_spec=pltpu.PrefetchScalarGridSpec(
            num_scalar_prefetch=5,
            grid=(T // tt,),
            in_specs=[pl.BlockSpec((tt, D), lambda i, *_: (i, 0)),
                      pl.BlockSpec((8, tt), lambda i, *_: (0, i))],
            out_specs=pl.BlockSpec(memory_space=pl.ANY),
            scratch_shapes=[pltpu.VMEM((2, MOE_STEP_TILES * MOE_SL, D), BF16),
                            pltpu.VMEM((MOE_TM, D), BF16),
                            pltpu.SemaphoreType.DMA((2,)),
                            pltpu.SemaphoreType.DMA(()),
                            pltpu.SemaphoreType.DMA(())]),
        out_shape=jax.ShapeDtypeStruct((n_rows, D), BF16),
        compiler_params=_cparams(1),
        name="dispatch",
    )(npieces, piece_glob, pad_start, pad_rows, nused, xn2, srow)


def _experts_kernel(blk_e_ref, nused_ref, nsub_ref, first_ref, next_e_ref, slot_ref,
                    x_ref, wg_hbm, wu_hbm, wd_hbm, y_ref,
                    wg32_ref, wu32_ref, wd32_ref, wgb_ref, wub_ref, wdb_ref, sem):
    i = pl.program_id(0)
    used = i < nused_ref[0]
    n_sub = nsub_ref[i]
    landing = ((wg_hbm, wg32_ref), (wu_hbm, wu32_ref), (wd_hbm, wd32_ref))

    def weight_copies(e, s):
        return [pltpu.make_async_copy(hbm.at[e], vmem.at[s], sem.at[s, n]) for n, (hbm, vmem) in enumerate(landing)]

    @pl.when(used & (i == 0))
    def _():
        for cp in weight_copies(blk_e_ref[0], 0):
            cp.start()

    run_start = used & (first_ref[i] > 0)
    s = slot_ref[i]

    @pl.when(run_start)
    def _():
        for cp in weight_copies(blk_e_ref[i], s):
            cp.wait()

        @pl.when(next_e_ref[i] >= 0)
        def _():
            for cp in weight_copies(next_e_ref[i], 1 - s):
                cp.start()

    def swiglu(m, wg, wu, wd):
        x = x_ref[:m, :]
        hg = _dot(x, wg)
        hu = _dot(x, wu)
        hid = (_silu(hg) * hu).astype(BF16)
        y_ref[:m, :] = _dot(hid, wd).astype(BF16)
        if m < MOE_TM:
            y_ref[m:, :] = jnp.zeros((MOE_TM - m, y_ref.shape[1]), BF16)

    for k in range(1, MOE_TM // EXPERT_SUB + 1):
        @pl.when(run_start & (n_sub == k))
        def _(k=k):
            wg = wg32_ref[s].astype(BF16)
            wu = wu32_ref[s].astype(BF16)
            wd = wd32_ref[s].astype(BF16)
            wgb_ref[...] = wg
            wub_ref[...] = wu
            wdb_ref[...] = wd
            swiglu(k * EXPERT_SUB, wg, wu, wd)

        @pl.when(used & jnp.logical_not(run_start) & (n_sub == k))
        def _(k=k):
            swiglu(k * EXPERT_SUB, wgb_ref[...], wub_ref[...], wdb_ref[...])


def _experts(buf, blk_e, nused, nsub, run_first, next_e, run_slot, w_gate, w_up, w_down):
    R, D = buf.shape
    de = w_gate.shape[2]
    n_blocks = R // MOE_TM
    row_map = lambda i, be, nu, *_: (jnp.minimum(i, nu[0] - 1), 0)
    return pl.pallas_call(
        _experts_kernel,
        grid_spec=pltpu.PrefetchScalarGridSpec(
            num_scalar_prefetch=6,
            grid=(n_blocks,),
            in_specs=[pl.BlockSpec((MOE_TM, D), row_map),
                      pl.BlockSpec(memory_space=pl.ANY),
                      pl.BlockSpec(memory_space=pl.ANY),
                      pl.BlockSpec(memory_space=pl.ANY)],
            out_specs=pl.BlockSpec((MOE_TM, D), row_map),
            scratch_shapes=[pltpu.VMEM((2, D, de), F32), pltpu.VMEM((2, D, de), F32), pltpu.VMEM((2, de, D), F32),
                            pltpu.VMEM((D, de), BF16), pltpu.VMEM((D, de), BF16), pltpu.VMEM((de, D), BF16),
                            pltpu.SemaphoreType.DMA((2, 3))]),
        out_shape=jax.ShapeDtypeStruct((R, D), BF16),
        input_output_aliases={6: 0},
        compiler_params=_cparams(1),
        name="experts",
    )(blk_e, nused, nsub, run_first, next_e, run_slot, buf, w_gate, w_up, w_down)


def _combine_kernel(npieces_ref, glob_ref, x2_ref, rti_ref, rtf_ref, gfin_ref, yb_ref, out_ref,
                    rows_ref, sem):
    tt = x2_ref.shape[0] // MOE_STEP_TILES
    sl = rows_ref.shape[1] // MOE_STEP_TILES
    i = pl.program_id(0)
    n_steps = pl.num_programs(0)
    cur = lax.rem(i, 2)

    def piece_copy(buf_slot, sub, local_row, global_row):
        return pltpu.make_async_copy(yb_ref.at[pl.ds(global_row, ROW_PIECE)],
                                     rows_ref.at[buf_slot, pl.ds(sub * sl + local_row, ROW_PIECE)], sem.at[buf_slot])

    def for_step_pieces(step, buf_slot, act):
        for sub in range(MOE_STEP_TILES):
            _for_each_piece(npieces_ref, glob_ref, step * MOE_STEP_TILES + sub,
                            lambda l, g, sub=sub: act(piece_copy(buf_slot, sub, l, g)))

    @pl.when(i == 0)
    def _():
        rows_ref[...] = jnp.zeros_like(rows_ref)
        for_step_pieces(0, 0, lambda cp: cp.start())

    @pl.when(i + 1 < n_steps)
    def _():
        for_step_pieces(i + 1, 1 - cur, lambda cp: cp.start())

    for_step_pieces(i, cur, lambda cp: cp.wait())

    n_sub = _used_slot_groups(npieces_ref[i * MOE_STEP_TILES])
    for sub in range(1, MOE_STEP_TILES):
        n_sub = jnp.maximum(n_sub, _used_slot_groups(npieces_ref[i * MOE_STEP_TILES + sub]))
    for k in range(TOP_K * tt // SLOT_SUB, sl // SLOT_SUB + 1):
        @pl.when(n_sub == k)
        def _(k=k):
            m = k * SLOT_SUB
            lane = lax.broadcasted_iota(jnp.int32, (tt, m), 1)
            for sub in range(MOE_STEP_TILES):
                ts_rows = slice(sub * tt, (sub + 1) * tt)
                rti = rti_ref[ts_rows, :]
                rtf = rtf_ref[ts_rows, :]
                g = jnp.where(lane == rti[:, 2:3], rtf[:, 0:1],
                              jnp.where(lane == rti[:, 3:4], rtf[:, 1:2], 0.0)).astype(BF16)
                y = x2_ref[ts_rows, :].astype(F32) + _dot(g, rows_ref[cur, sub * sl:sub * sl + m, :])
                ms = jnp.mean(y * y, axis=-1, keepdims=True)
                out_ref[ts_rows, :] = y * lax.rsqrt(ms + EPS) * gfin_ref[...]


def _combine(x2, rti, rtf, g_final, yb, npieces, piece_glob):
    T, D = rti.shape[0], x2.shape[1]
    tt = MOE_STEP_TILES * MIX_TS
    tok = lambda i, *_: (i, 0)
    return pl.pallas_call(
        _combine_kernel,
        grid_spec=pltpu.PrefetchScalarGridSpec(
            num_scalar_prefetch=2,
            grid=(T // tt,),
            in_specs=[pl.BlockSpec((tt, D), tok),
                      pl.BlockSpec((tt, LANES), tok),
                      pl.BlockSpec((tt, LANES), tok),
                      pl.BlockSpec((1, D), lambda i, *_: (0, 0)),
                      pl.BlockSpec(memory_space=pl.ANY)],
            out_specs=pl.BlockSpec((tt, D), tok),
            scratch_shapes=[pltpu.VMEM((2, MOE_STEP_TILES * MOE_SL, D), BF16),
                            pltpu.SemaphoreType.DMA((2,))]),
        out_shape=jax.ShapeDtypeStruct((T, D), F32),
        compiler_params=_cparams(1),
        name="combine",
    )(npieces, piece_glob, x2, rti, rtf, g_final, yb)


def _pad_lanes(a, width=LANES):
    return jnp.pad(a, ((0, 0), (0, width - a.shape[1])))


def kernel(x, g_mix, w_in, b_if, conv_q, conv_k, g_head, w_pool, pool_scale, w_br_a, w_br_b, w_out,
           g_ffn, w_rg, b_rg, w_re, b_re, w_e_gate, w_e_up, w_e_down, g_final):
    B, S, D = x.shape
    T = B * S
    assert g_mix.shape[0] == 1, "single-layer block"
    assert S % MIX_TS == 0 and (T // MIX_TS) % MOE_STEP_TILES == 0
    d_pool = w_br_a.shape[1]
    d_ml = w_br_b.shape[1]
    x2d = x.reshape(T, D)

    params = {
        "b_if": _pad_lanes(b_if[0][None, :]),
        "b_if_t": jnp.pad(b_if[0][:, None], ((0, 16 - 2 * N_HEADS), (0, 0))),
        "conv_q": conv_q[0], "conv_k": conv_k[0],
        "g_head": g_head[0][None, :],
        "w_pool": w_pool[0],
        "pool_scale": pool_scale[0][None, :],
        "w_br_a": w_br_a[0], "w_br_b": w_br_b[0],
        "w_out": w_out[0],
        "g_ffn": g_ffn[0][None, :],
        "w_r": _pad_lanes(jnp.concatenate([w_rg[0], w_re[0]], axis=1)).astype(BF16),
        "b_r": _pad_lanes(jnp.concatenate([b_rg[0], b_re[0]])[None, :]),
    }

    x2, xn2, rti, rtf, tstat, srow = _mixer(x2d, g_mix[0][None, :], w_in[0].astype(BF16), params, B, S)

    n_tiles = T // MIX_TS
    max_rows = n_tiles * (TOP_K * MIX_TS + N_EXPERTS * (ROW_PIECE - 1)) + N_EXPERTS * (MOE_TM - ROW_PIECE)
    n_rows = -(-max_rows // MOE_TM) * MOE_TM
    n_blocks = n_rows // MOE_TM
    i32 = lambda a: a.astype(jnp.int32)
    mm = lambda a, b: jnp.round(jnp.dot(a, b, precision=lax.Precision.HIGHEST, preferred_element_type=F32))
    e_ids = np.arange(N_EXPERTS)
    t_ids = np.arange(n_tiles)
    b_ids = np.arange(n_blocks)
    cum_e = jnp.asarray(e_ids[:, None] <= e_ids[None, :], F32)
    cum_t = jnp.asarray(t_ids[:, None] >= t_ids[None, :], F32)
    cum_b = jnp.asarray(b_ids[:, None] <= b_ids[None, :], F32)
    e_row = jnp.asarray(e_ids[None, :], F32)

    pcs = tstat.reshape(n_tiles, 8, LANES)[:, 0, ROUTER_LANE0:ROUTER_LANE0 + N_EXPERTS].astype(F32)
    piece_end = mm(pcs, cum_e)
    piece_loc = piece_end - pcs
    tile_cum = mm(cum_t, pcs)
    rows_e = tile_cum[-1:, :] * ROW_PIECE
    padded = jnp.floor((rows_e + (MOE_TM - 1)) * (1.0 / MOE_TM)) * MOE_TM
    pend = mm(padded, cum_e)
    poff = pend - padded
    piece_glob = poff * (1.0 / ROW_PIECE) + tile_cum - pcs
    nused_f = pend[0, -1] * (1.0 / MOE_TM)
    blk_start = jnp.asarray(b_ids[:, None] * MOE_TM, F32)
    blk_e_f = jnp.minimum(jnp.sum((pend <= blk_start).astype(F32), axis=1, keepdims=True), N_EXPERTS - 1.0)
    blk_oh = (blk_e_f == e_row).astype(F32)
    later_nonempty = (e_ids[None, :] > e_ids[:, None]) & (rows_e > 0)
    next_of_e = jnp.min(jnp.where(later_nonempty, e_row, float(N_EXPERTS)), axis=1)
    next_of_e = jnp.where(next_of_e == N_EXPERTS, -1.0, next_of_e)
    per_blk = mm(blk_oh, jnp.stack([pend[0], (poff + rows_e)[0], next_of_e], axis=1))
    blk_used = blk_start < pend[0, -1]
    rows_in_blk = jnp.clip(per_blk[:, 1:2] - blk_start, 0.0, float(MOE_TM))
    nsub = jnp.floor((rows_in_blk + (EXPERT_SUB - 1)) * (1.0 / EXPERT_SUB))
    prev_e = jnp.concatenate([jnp.full((1, 1), -1.0, F32), blk_e_f[:-1]], axis=0)
    run_first = (blk_used & (blk_e_f != prev_e)).astype(F32)
    run_idx = mm(run_first.reshape(1, n_blocks), cum_b) - 1.0
    run_slot = run_idx - 2.0 * jnp.floor(run_idx * 0.5)
    p_ids = jnp.asarray(np.arange(PIECES_PER_TILE), F32)
    e_of_p = jnp.minimum(jnp.sum((piece_end[:, None, :] <= p_ids[None, :, None]).astype(F32), axis=2),
                         N_EXPERTS - 1.0)
    shift = jnp.sum(jnp.where(e_of_p[:, :, None] == e_row[None], (piece_glob - piece_loc)[:, None, :], 0.0), axis=2)
    glob_of_p = i32(shift + p_ids[None, :]).reshape(n_tiles * PIECES_PER_TILE)
    npieces = i32(piece_end[:, -1])
    flat_b = lambda a: i32(a).reshape(n_blocks)
    blk_e, nused = flat_b(blk_e_f), i32(nused_f).reshape(1)

    flat_e = lambda a: i32(a).reshape(N_EXPERTS)
    buf = _dispatch(xn2, srow, npieces, glob_of_p, flat_e(poff + rows_e), flat_e(padded - rows_e), nused, n_rows)
    yb = _experts(buf, blk_e, nused, flat_b(nsub), flat_b(run_first), flat_b(per_blk[:, 2:3]), flat_b(run_slot),
                  w_e_gate[0], w_e_up[0], w_e_down[0])
    out = _combine(x2, rti, rtf, g_final[None, :], yb, npieces, glob_of_p)
    return out.reshape(B, S, D)
```

```python
import functools

import numpy as np
import jax
import jax.numpy as jnp
from jax import lax
from jax.experimental import pallas as pl
from jax.experimental.pallas import tpu as pltpu

F32 = jnp.float32
BF16 = jnp.bfloat16

CHUNK = 64
POOL_WINDOWS = (2, 4, 8, 16)
N_HEADS = 4
CONV_K = 4
N_GROUPS = 4
EXPERTS_PER_GROUP = 8
N_EXPERTS = N_GROUPS * EXPERTS_PER_GROUP
TOP_K = 2
EPS = 1e-6

LANES = 128
HALO = 16
ROUTER_LANE0 = N_GROUPS

INPROJ_TN = 256
MIX_TS = 256
MOE_TM = 512
ROW_PIECE = 16
MOE_SL = TOP_K * MIX_TS + N_EXPERTS * ROW_PIECE
PIECES_PER_TILE = MOE_SL // ROW_PIECE
MOE_STEP_TILES = 4
SLOT_SUB = 128
PIECE_UNROLL = 4
EXPERT_SUB = 128
SLOT_RADIX = 16
VMEM_LIMIT = 56 * 1024 * 1024


def _cparams(n_axes):
    return pltpu.CompilerParams(dimension_semantics=("arbitrary",) * n_axes,
                                vmem_limit_bytes=VMEM_LIMIT)


def _sigmoid(v):
    return 0.5 * jnp.tanh(0.5 * v) + 0.5


def _silu(v):
    return v * _sigmoid(v)


def _log_sigmoid(v):
    return jnp.minimum(v, 0.0) - jnp.log1p(jnp.exp(-jnp.abs(v)))


def _split3(v):
    hi = v.astype(BF16)
    r1 = v - hi.astype(F32)
    mid = r1.astype(BF16)
    lo = (r1 - mid.astype(F32)).astype(BF16)
    return hi, mid, lo


def _dot(a, b):
    return jnp.dot(a, b, preferred_element_type=F32)


def _dot_nt(a, b):
    return lax.dot_general(a, b, (((1,), (1,)), ((), ())), preferred_element_type=F32)


def _dot_tn(a, b):
    return lax.dot_general(a, b, (((0,), (0,)), ((), ())), preferred_element_type=F32)


def _inproj_steps(x_ref, g_ref, w_refs, z_refs, xn_ref):
    w_ref_all, wg_ref = w_refs
    zm_ref, zg_ref, zif_ref, zift_ref = z_refs
    n_main = zm_ref.shape[1]

    def norm():
        x = x_ref[...]
        ms = jnp.mean(x * x, axis=-1, keepdims=True)
        xn_ref[...] = (x * lax.rsqrt(ms + EPS) * g_ref[...]).astype(BF16)

    def block(w_ref, z_ref, c0):
        def run():
            cols = slice(c0, c0 + INPROJ_TN)
            z_ref[:, cols] = _dot(xn_ref[...], w_ref[:, cols]).astype(BF16)
        return run

    def gates():
        zif = _dot(xn_ref[...], w_ref_all[:, n_main:n_main + LANES])
        zif_ref[...] = zif
        zift_ref[...] = zif.T[:zift_ref.shape[0], :]

    steps = [norm, gates]
    steps += [block(w_ref_all, zm_ref, c0) for c0 in range(0, n_main, INPROJ_TN)]
    steps += [block(wg_ref, zg_ref, c0) for c0 in range(0, zg_ref.shape[1], INPROJ_TN)]
    return steps


def _mixer_kernel(x_ref, xnext_ref, gmix_ref, win_ref,
                  bif_ref, bift_ref, convq_ref, convk_ref, ghead_ref, wpool32_ref, pscale_ref,
                  wa32_ref, wb32_ref, wo32_ref, gffn_ref, wr_ref, br_ref,
                  tric_ref, trir_ref, stri_ref, ut_ref, sel_ref,
                  x2_ref, xn2_ref, rti_ref, rtf_ref, tstat_ref, srow_ref,
                  zm_ref, zg_ref, zif_ref, zift_ref, zm_nxt, zg_nxt, zif_nxt, zift_nxt, xn_ref,
                  ext_ref, q_ref, k_ref, h_ref, pool_ref, cst_ref, mst_ref, lg_ref, wg_ref,
                  wpool_ref, wa_ref, wb_ref, wo_ref, *, tiles_per_seq):
    ts = x_ref.shape[0]
    d_pool = wa_ref.shape[0]
    d_ml = wb_ref.shape[0]
    dh = d_ml // N_HEADS
    n_chunks = ts // CHUNK
    g_step = pl.program_id(0)
    j = lax.rem(g_step, tiles_per_seq)
    w_in_refs = (win_ref, wg_ref)
    z_cur = (zm_ref, zg_ref, zif_ref, zift_ref)
    z_nxt = (zm_nxt, zg_nxt, zif_nxt, zift_nxt)
    first = g_step == 0

    @pl.when(first)
    def _():
        g0 = zm_ref.shape[1] + 2 * N_HEADS
        wg_ref[...] = win_ref[:, g0:g0 + wg_ref.shape[1]]
        for dst, src in ((wpool_ref, wpool32_ref), (wa_ref, wa32_ref), (wb_ref, wb32_ref), (wo_ref, wo32_ref)):
            dst[...] = src[...].astype(BF16)
        for step in _inproj_steps(x_ref, gmix_ref, w_in_refs, z_cur, xn_ref):
            step()
        lg_ref[...] = jnp.zeros_like(lg_ref)

    @pl.when(jnp.logical_not(first))
    def _():
        for dst, src in zip(z_cur, z_nxt):
            dst[...] = src[...]

    @pl.when(j == 0)
    def _():
        ext_ref[:, :HALO, :] = jnp.zeros((ext_ref.shape[0], HALO, LANES), F32)
        cst_ref[...] = jnp.zeros_like(cst_ref)
        mst_ref[...] = jnp.zeros_like(mst_ref)

    pending = _inproj_steps(xnext_ref, gmix_ref, w_in_refs, z_nxt, xn_ref)

    def project_some(n=1):
        for _ in range(min(n, len(pending))):
            pending.pop(0)()

    project_some(2)

    routed = _route_select(lg_ref[...])

    row = lax.broadcasted_iota(jnp.int32, (ts, LANES), 0)
    pos1 = (row + j * ts + 1).astype(F32)

    def history(cg):
        cur = zm_ref[:, cg * LANES:(cg + 1) * LANES].astype(F32)
        ext_ref[cg, HALO:, :] = cur
        return cur, lambda s: ext_ref[cg, HALO - s:HALO - s + ts, :]

    def keep_history(cg, cur):
        ext_ref[cg, :HALO, :] = cur[ts - HALO:, :]

    n_pool_groups = d_pool // LANES
    for g in range(n_pool_groups):
        w = POOL_WINDOWS[g]
        cur, shifted = history(g)
        win = cur
        for s in range(1, w):
            win = win + shifted(s)
        keep_history(g, cur)
        cnt = jnp.minimum(pos1, float(w))
        d = win / cnt - cur
        y = _dot(d.astype(BF16), wpool_ref[g]) * pscale_ref[:, g * LANES:(g + 1) * LANES]
        pool_ref[:, g * LANES:(g + 1) * LANES] = y.astype(BF16)
        project_some()

    n_ml_groups = d_ml // LANES
    for which, (cw_ref, dst_ref, scale) in enumerate(((convq_ref, q_ref, 1.0), (convk_ref, k_ref, dh ** -0.5))):
        for g in range(n_ml_groups):
            cols = slice(g * LANES, (g + 1) * LANES)
            cg = n_pool_groups + which * n_ml_groups + g
            cur, shifted = history(cg)
            acc = cur * cw_ref[CONV_K - 1:CONV_K, cols]
            for sft in range(1, CONV_K):
                acc = acc + shifted(sft) * cw_ref[CONV_K - 1 - sft:CONV_K - sft, cols]
            keep_history(cg, cur)
            dst_ref[:, cols] = (_silu(acc) * scale).astype(BF16)
        project_some()

    _route_slots(routed, stri_ref, ut_ref, sel_ref, rti_ref, rtf_ref, tstat_ref, srow_ref)
    project_some()

    zc = zif_ref[...] + bif_ref[...]
    lf_c = _log_sigmoid(zc)
    bc = sum(_dot(tric_ref[...], p) for p in _split3(lf_c))
    zr = zift_ref[...] + bift_ref[...]
    lf_r = _log_sigmoid(zr)
    br = sum(_dot(p, trir_ref[...]) for p in _split3(lf_r))
    project_some(2)

    ti = lax.broadcasted_iota(jnp.int32, (CHUNK, CHUNK), 0)
    si = lax.broadcasted_iota(jnp.int32, (CHUNK, CHUNK), 1)
    causal = si <= ti
    ones_blk = jnp.ones((CHUNK, dh), BF16)
    v0 = d_pool + 2 * d_ml
    ig_rep = [jnp.broadcast_to(zc[:, h:h + 1], (ts, dh)) for h in range(N_HEADS)]
    bt_rep = [jnp.broadcast_to(bc[:, N_HEADS + h:N_HEADS + h + 1], (ts, dh)) for h in range(N_HEADS)]

    m_state = [mst_ref[h:h + 1, :] for h in range(N_HEADS)]
    c_state = [cst_ref[h] for h in range(N_HEADS)]
    def stage_scores(c):
        rs = slice(c * CHUNK, (c + 1) * CHUNK)
        out = []
        for h in range(N_HEADS):
            hs = slice(h * dh, (h + 1) * dh)
            q = q_ref[rs, hs]
            k = k_ref[rs, hs]
            bt = bt_rep[h][rs, :]
            r_row = zr[h:h + 1, rs] - br[N_HEADS + h:N_HEADS + h + 1, rs]
            dmat = jnp.where(causal, bt[:, :CHUNK] + r_row, -jnp.inf)
            out.append(dict(q=q, k=k, bt=bt, dmat=dmat, qk=_dot_nt(q, k),
                            m_intra=jnp.max(dmat, axis=-1, keepdims=True)))
        return out

    def stage_state(c, st):
        rs = slice(c * CHUNK, (c + 1) * CHUNK)
        for h in range(N_HEADS):
            s = st[h]
            bt, k = s["bt"], s["k"]
            m_prev, c_prev = m_state[h], c_state[h]
            v_aug = jnp.concatenate([zm_ref[rs, v0 + h * dh:v0 + (h + 1) * dh], ones_blk], axis=-1)
            igc = ig_rep[h][rs, :]
            b_last = bt[CHUNK - 1:CHUNK, :]
            a_log = b_last - bt + igc
            a_max = jnp.max(a_log, axis=0, keepdims=True)
            m_new = jnp.maximum(b_last + m_prev, a_max)
            kw = (k.astype(F32) * jnp.exp(a_log - m_new)).astype(BF16)
            decay = jnp.exp(b_last + m_prev - m_new)
            s.update(v_aug=v_aug, m_prev=m_prev, qc=_dot(s["q"], c_prev.astype(BF16)))
            c_state[h] = jnp.concatenate([decay, decay], axis=-1) * c_prev + _dot_tn(kw, v_aug)
            m_state[h] = m_new

    def stage_values(c, st):
        rs = slice(c * CHUNK, (c + 1) * CHUNK)
        for h in range(N_HEADS):
            s = st[h]
            hs = slice(h * dh, (h + 1) * dh)
            inter = s["bt"] + s["m_prev"]
            m_t = jnp.maximum(inter, s["m_intra"])
            w_inter = jnp.exp(inter - m_t)
            smat = s["qk"] * jnp.exp(s["dmat"] - m_t[:, :CHUNK])
            sv = _dot(smat.astype(BF16), s["v_aug"])
            qc = s["qc"]
            nq = w_inter * qc[:, dh:] + sv[:, dh:]
            den = jnp.maximum(jnp.abs(nq), jnp.exp(-m_t))
            h_ref[rs, hs] = (w_inter * qc[:, :dh] + sv[:, :dh]) / den

    staged = stage_scores(0)
    for c in range(n_chunks):
        stage_state(c, staged)
        project_some()
        nxt = stage_scores(c + 1) if c + 1 < n_chunks else None
        project_some()
        stage_values(c, staged)
        staged = nxt
    for h in range(N_HEADS):
        cst_ref[h] = c_state[h]
        mst_ref[h:h + 1, :] = m_state[h]

    o0 = v0 + d_ml
    for h in range(N_HEADS):
        hs = slice(h * dh, (h + 1) * dh)
        hv = h_ref[:, hs]
        mu = jnp.mean(hv, axis=-1, keepdims=True)
        hc = hv - mu
        var = jnp.mean(hc * hc, axis=-1, keepdims=True)
        hn = hc * lax.rsqrt(var + EPS) * ghead_ref[:, hs]
        og = _sigmoid(zm_ref[:, o0 + h * dh:o0 + (h + 1) * dh].astype(F32))
        q_ref[:, hs] = (og * hn).astype(BF16)
    y_a = _dot(pool_ref[...], wa_ref[...])
    y_b = _dot(q_ref[...], wb_ref[...])
    d_model = x_ref.shape[1]
    ga = _sigmoid(zg_ref[:, :d_model].astype(F32))
    gb = _sigmoid(zg_ref[:, d_model:].astype(F32))
    merged = (ga * y_a + gb * y_b).astype(BF16)
    x2 = x_ref[...] + _dot(merged, wo_ref[...])
    x2_ref[...] = x2.astype(BF16)
    project_some(len(pending))

    ms = jnp.mean(x2 * x2, axis=-1, keepdims=True)
    xn2 = x2 * lax.rsqrt(ms + EPS) * gffn_ref[...]
    xh = xn2.astype(BF16)
    xn2_ref[...] = xh
    lg_ref[...] = _dot(xh, wr_ref[...]) + br_ref[...]
    project_some(len(pending))


def _route_select(lg):
    ts = lg.shape[0]
    lane = lax.broadcasted_iota(jnp.int32, (ts, LANES), 1)
    lanef = lane.astype(F32)
    big = float(4 * LANES)
    gl = jnp.where(lane < N_GROUPS, lg, -jnp.inf)
    gmax = jnp.max(gl, axis=-1, keepdims=True)
    g_sel = jnp.min(jnp.where(gl == gmax, lanef, big), axis=-1, keepdims=True)
    p_g = 1.0 / jnp.sum(jnp.exp(gl - gmax), axis=-1, keepdims=True)
    lo = ROUTER_LANE0 + EXPERTS_PER_GROUP * g_sel
    el = jnp.where((lanef >= lo) & (lanef < lo + EXPERTS_PER_GROUP), lg, -jnp.inf)
    m1 = jnp.max(el, axis=-1, keepdims=True)
    i1 = jnp.min(jnp.where(el == m1, lanef, big), axis=-1, keepdims=True)
    el2 = jnp.where(lanef == i1, -jnp.inf, el)
    m2 = jnp.max(el2, axis=-1, keepdims=True)
    i2 = jnp.min(jnp.where(el2 == m2, lanef, big), axis=-1, keepdims=True)
    e2x = jnp.exp(m2 - m1)
    gate1 = p_g / (1.0 + e2x)
    gate2 = p_g * e2x / (1.0 + e2x)
    return dict(lane=lane, i1=i1, i2=i2, gate1=gate1, gate2=gate2, oh1=lanef == i1, oh2=lanef == i2)


def _route_slots(r, stri_ref, ut_ref, sel_ref, rti_ref, rtf_ref, tstat_ref, srow_ref):
    lane, oh1, oh2, i1, i2 = r["lane"], r["oh1"], r["oh2"], r["i1"], r["i2"]
    ohs = jnp.where(oh1 | oh2, 1.0, 0.0)
    n_loc = jnp.sum(ohs, axis=0, keepdims=True)
    pieces = jnp.floor((n_loc + (ROW_PIECE - 1.0)) * (1.0 / ROW_PIECE))
    piece_off = _dot(jnp.broadcast_to(pieces, (8, LANES)).astype(BF16), ut_ref[...])[0:1, :]
    base = _dot(stri_ref[...], ohs.astype(BF16)) + ROW_PIECE * piece_off
    slot1 = jnp.sum(jnp.where(oh1, base, 0.0), axis=-1, keepdims=True)
    slot2 = jnp.sum(jnp.where(oh2, base, 0.0), axis=-1, keepdims=True)
    tstat_ref[...] = jnp.broadcast_to(pieces, tstat_ref.shape).astype(jnp.int32)

    rti = jnp.where(lane == 0, i1 - ROUTER_LANE0,
                    jnp.where(lane == 1, i2 - ROUTER_LANE0,
                              jnp.where(lane == 2, slot1, jnp.where(lane == 3, slot2, 0.0))))
    rti_ref[...] = rti.astype(jnp.int32)
    rtf_ref[...] = jnp.where(lane == 0, r["gate1"], jnp.where(lane == 1, r["gate2"], 0.0))
    h1 = jnp.floor(slot1 * (1.0 / SLOT_RADIX))
    h2 = jnp.floor(slot2 * (1.0 / SLOT_RADIX))
    parts = jnp.where(lane == 0, h1, jnp.where(lane == 1, slot1 - SLOT_RADIX * h1,
                      jnp.where(lane == 2, h2, jnp.where(lane == 3, slot2 - SLOT_RADIX * h2, 0.0))))
    srow_ref[...] = _dot_nt(sel_ref[...], parts.astype(BF16))


def _mixer(x2d, g_mix, w_all, params, batch, seq):
    T, D = x2d.shape
    ts = min(MIX_TS, seq)
    nts = seq // ts
    d_pool = params["w_br_a"].shape[0]
    d_ml = params["w_br_b"].shape[0]
    dh = d_ml // N_HEADS

    idx = np.arange(ts)
    same_chunk = (idx[:, None] // CHUNK) == (idx[None, :] // CHUNK)
    tri_c = jnp.asarray((idx[None, :] <= idx[:, None]) & same_chunk, BF16)
    tri_r = jnp.asarray((idx[:, None] <= idx[None, :]) & same_chunk, BF16)
    stri = jnp.asarray(idx[None, :] < idx[:, None], BF16)
    lane_idx = np.arange(LANES)
    ut = jnp.asarray(lane_idx[:, None] < lane_idx[None, :], BF16)
    sel = jnp.asarray(np.arange(8)[:, None] == lane_idx[None, :], BF16)

    n_tiles = batch * nts
    tok = lambda g: (g, 0)
    tok_in = lambda g: (jnp.minimum(g, n_tiles - 1), 0)
    tok_next = lambda g: (jnp.minimum(g + 1, n_tiles - 1), 0)
    tok_prev = lambda g: (jnp.maximum(g - 1, 0), 0)
    tok_prev_t = lambda g: (0, jnp.maximum(g - 1, 0))
    c2 = lambda g: (0, 0)
    c3 = lambda g: (0, 0, 0)
    full = lambda a: pl.BlockSpec(a.shape, c2 if a.ndim == 2 else c3)
    consts = [params[n] for n in ("b_if", "b_if_t", "conv_q", "conv_k", "g_head", "w_pool", "pool_scale",
                                  "w_br_a", "w_br_b", "w_out", "g_ffn", "w_r", "b_r")]
    consts = [g_mix, w_all] + consts + [tri_c, tri_r, stri, ut, sel]
    nm, ng = d_pool + 4 * d_ml, 2 * D
    z_scratch = [pltpu.VMEM((ts, nm), BF16), pltpu.VMEM((ts, ng), BF16),
                 pltpu.VMEM((ts, LANES), F32), pltpu.VMEM((16, ts), F32)]
    return pl.pallas_call(
        functools.partial(_mixer_kernel, tiles_per_seq=nts),
        grid=(n_tiles + 1,),
        in_specs=[pl.BlockSpec((ts, D), tok_in),
                  pl.BlockSpec((ts, D), tok_next)] + [full(a) for a in consts],
        out_specs=[pl.BlockSpec((ts, D), tok),
                   pl.BlockSpec((ts, D), tok),
                   pl.BlockSpec((ts, LANES), tok_prev),
                   pl.BlockSpec((ts, LANES), tok_prev),
                   pl.BlockSpec((8, LANES), tok_prev),
                   pl.BlockSpec((8, ts), tok_prev_t)],
        out_shape=[jax.ShapeDtypeStruct((T + ts, D), BF16),
                   jax.ShapeDtypeStruct((T + ts, D), BF16),
                   jax.ShapeDtypeStruct((T, LANES), jnp.int32),
                   jax.ShapeDtypeStruct((T, LANES), F32),
                   jax.ShapeDtypeStruct((n_tiles * 8, LANES), jnp.int32),
                   jax.ShapeDtypeStruct((8, T), F32)],
        scratch_shapes=z_scratch + z_scratch + [
                        pltpu.VMEM((ts, D), BF16),
                        pltpu.VMEM(((d_pool + 2 * d_ml) // LANES, HALO + ts, LANES), F32),
                        pltpu.VMEM((ts, d_ml), BF16),
                        pltpu.VMEM((ts, d_ml), BF16),
                        pltpu.VMEM((ts, d_ml), F32),
                        pltpu.VMEM((ts, d_pool), BF16),
                        pltpu.VMEM((N_HEADS, dh, 2 * dh), F32),
                        pltpu.VMEM((8, LANES), F32),
                        pltpu.VMEM((ts, LANES), F32),
                        pltpu.VMEM((D, ng), BF16),
                        pltpu.VMEM(params["w_pool"].shape, BF16),
                        pltpu.VMEM((d_pool, D), BF16),
                        pltpu.VMEM((d_ml, D), BF16),
                        pltpu.VMEM((D, D), BF16)],
        compiler_params=_cparams(1),
        name="mixer",
    )(x2d, x2d, *consts)


def _for_each_piece(npieces_ref, glob_ref, tile, fn):
    base = tile * PIECES_PER_TILE
    n = npieces_ref[tile]

    def one(p):
        fn(pl.multiple_of(p * ROW_PIECE, ROW_PIECE), pl.multiple_of(glob_ref[base + p] * ROW_PIECE, ROW_PIECE))

    def group(g, carry):
        for u in range(PIECE_UNROLL):
            one(g * PIECE_UNROLL + u)
        return carry

    n_groups = lax.div(n, jnp.int32(PIECE_UNROLL))
    lax.fori_loop(0, n_groups, group, 0)
    for u in range(PIECE_UNROLL - 1):
        @pl.when(n_groups * PIECE_UNROLL + u < n)
        def _():
            one(n_groups * PIECE_UNROLL + u)


def _used_slot_groups(n_pieces):
    return lax.div(n_pieces * ROW_PIECE + (SLOT_SUB - 1), jnp.int32(SLOT_SUB))


def _dispatch_kernel(npieces_ref, glob_ref, pad_start_ref, pad_rows_ref, nused_ref,
                     xn_ref, srow_ref, buf_ref, rows_ref, zeros_ref, sem, zsem, tsem, *, n_steps):
    tt = xn_ref.shape[0] // MOE_STEP_TILES
    sl = rows_ref.shape[1] // MOE_STEP_TILES
    n_blocks = buf_ref.shape[0] // MOE_TM
    i = pl.program_id(0)
    cur = lax.rem(i, 2)

    def zero_fill_share(step, act):
        for k in range(-(-N_EXPERTS // n_steps)):
            e = step + k * n_steps

            @pl.when(e < N_EXPERTS)
            def _(e=e):
                e_c = jnp.minimum(e, N_EXPERTS - 1)
                rows = pad_rows_ref[e_c]
                off = pad_start_ref[e_c]
                size = MOE_TM // 2
                while size >= ROW_PIECE:
                    has = (rows & size) != 0

                    @pl.when(has)
                    def _(off=off, size=size):
                        act(pltpu.make_async_copy(zeros_ref.at[pl.ds(0, size)],
                                                  buf_ref.at[pl.ds(pl.multiple_of(off, ROW_PIECE), size)], zsem))
                    off = off + jnp.where(has, size, 0)
                    size //= 2
        for k in range(-(-n_blocks // n_steps)):
            b = step + k * n_steps

            @pl.when((b >= nused_ref[0]) & (b < n_blocks))
            def _(b=b):
                b_in = jnp.minimum(b, n_blocks - 1)
                act(pltpu.make_async_copy(zeros_ref, buf_ref.at[pl.ds(b_in * MOE_TM, MOE_TM)], tsem))

    @pl.when(i == 0)
    def _():
        zeros_ref[...] = jnp.zeros_like(zeros_ref)

    zero_fill_share(i, lambda cp: cp.start())

    def piece_copy(buf_slot, sub, local_row, global_row):
        return pltpu.make_async_copy(rows_ref.at[buf_slot, pl.ds(sub * sl + local_row, ROW_PIECE)],
                                     buf_ref.at[pl.ds(global_row, ROW_PIECE)], sem.at[buf_slot])

    def for_step_pieces(step, buf_slot, act):
        for sub in range(MOE_STEP_TILES):
            _for_each_piece(npieces_ref, glob_ref, step * MOE_STEP_TILES + sub,
                            lambda l, g, sub=sub: act(piece_copy(buf_slot, sub, l, g)))

    @pl.when(i >= 2)
    def _():
        for_step_pieces(i - 2, cur, lambda cp: cp.wait())

    n_sub = _used_slot_groups(npieces_ref[i * MOE_STEP_TILES])
    for sub in range(1, MOE_STEP_TILES):
        n_sub = jnp.maximum(n_sub, _used_slot_groups(npieces_ref[i * MOE_STEP_TILES + sub]))
    for k in range(TOP_K * tt // SLOT_SUB, sl // SLOT_SUB + 1):
        @pl.when(n_sub == k)
        def _(k=k):
            m = k * SLOT_SUB
            r = lax.broadcasted_iota(jnp.int32, (m, tt), 0).astype(F32)
            for sub in range(MOE_STEP_TILES):
                sr = srow_ref[:, sub * tt:(sub + 1) * tt]
                slot1 = SLOT_RADIX * sr[0:1, :] + sr[1:2, :]
                slot2 = SLOT_RADIX * sr[2:3, :] + sr[3:4, :]
                sel = jnp.where((r == slot1) | (r == slot2), 1.0, 0.0).astype(BF16)
                rows_ref[cur, sub * sl:sub * sl + m, :] = _dot(sel, xn_ref[sub * tt:(sub + 1) * tt, :]).astype(BF16)
    for_step_pieces(i, cur, lambda cp: cp.start())

    @pl.when(i == n_steps - 1)
    def _():
        @pl.when(i >= 1)
        def _():
            for_step_pieces(i - 1, 1 - cur, lambda cp: cp.wait())
        for_step_pieces(i, cur, lambda cp: cp.wait())

        def wait_share(step, carry):
            zero_fill_share(step, lambda cp: cp.wait())
            return carry

        lax.fori_loop(0, n_steps, wait_share, 0)


def _dispatch(xn2, srow, npieces, piece_glob, pad_start, pad_rows, nused, n_rows):
    T, D = srow.shape[1], xn2.shape[1]
    tt = MOE_STEP_TILES * MIX_TS
    return pl.pallas_call(
        functools.partial(_dispatch_kernel, n_steps=T // tt),
        grid_spec=pltpu.PrefetchScalarGridSpec(
            num_scalar_prefetch=5,
            grid=(T // tt,),
            in_specs=[pl.BlockSpec((tt, D), lambda i, *_: (i, 0)),
                      pl.BlockSpec((8, tt), lambda i, *_: (0, i))],
            out_specs=pl.BlockSpec(memory_space=pl.ANY),
            scratch_shapes=[pltpu.VMEM((2, MOE_STEP_TILES * MOE_SL, D), BF16),
                            pltpu.VMEM((MOE_TM, D), BF16),
                            pltpu.SemaphoreType.DMA((2,)),
                            pltpu.SemaphoreType.DMA(()),
                            pltpu.SemaphoreType.DMA(())]),
        out_shape=jax.ShapeDtypeStruct((n_rows, D), BF16),
        compiler_params=_cparams(1),
        name="dispatch",
    )(npieces, piece_glob, pad_start, pad_rows, nused, xn2, srow)


def _experts_kernel(blk_e_ref, nused_ref, nsub_ref, first_ref, next_e_ref, slot_ref,
                    x_ref, wg_hbm, wu_hbm, wd_hbm, y_ref,
                    wg32_ref, wu32_ref, wd32_ref, wgb_ref, wub_ref, wdb_ref, sem):
    i = pl.program_id(0)
    used = i < nused_ref[0]
    n_sub = nsub_ref[i]
    landing = ((wg_hbm, wg32_ref), (wu_hbm, wu32_ref), (wd_hbm, wd32_ref))

    def weight_copies(e, s):
        return [pltpu.make_async_copy(hbm.at[e], vmem.at[s], sem.at[s, n]) for n, (hbm, vmem) in enumerate(landing)]

    @pl.when(used & (i == 0))
    def _():
        for cp in weight_copies(blk_e_ref[0], 0):
            cp.start()

    run_start = used & (first_ref[i] > 0)
    s = slot_ref[i]

    @pl.when(run_start)
    def _():
        for cp in weight_copies(blk_e_ref[i], s):
            cp.wait()

        @pl.when(next_e_ref[i] >= 0)
        def _():
            for cp in weight_copies(next_e_ref[i], 1 - s):
                cp.start()

    def swiglu(m, wg, wu, wd):
        x = x_ref[:m, :]
        hg = _dot(x, wg)
        hu = _dot(x, wu)
        hid = (_silu(hg) * hu).astype(BF16)
        y_ref[:m, :] = _dot(hid, wd).astype(BF16)
        if m < MOE_TM:
            y_ref[m:, :] = jnp.zeros((MOE_TM - m, y_ref.shape[1]), BF16)

    for k in range(1, MOE_TM // EXPERT_SUB + 1):
        @pl.when(run_start & (n_sub == k))
        def _(k=k):
            wg = wg32_ref[s].astype(BF16)
            wu = wu32_ref[s].astype(BF16)
            wd = wd32_ref[s].astype(BF16)
            wgb_ref[...] = wg
            wub_ref[...] = wu
            wdb_ref[...] = wd
            swiglu(k * EXPERT_SUB, wg, wu, wd)

        @pl.when(used & jnp.logical_not(run_start) & (n_sub == k))
        def _(k=k):
            swiglu(k * EXPERT_SUB, wgb_ref[...], wub_ref[...], wdb_ref[...])


def _experts(buf, blk_e, nused, nsub, run_first, next_e, run_slot, w_gate, w_up, w_down):
    R, D = buf.shape
    de = w_gate.shape[2]
    n_blocks = R // MOE_TM
    row_map = lambda i, be, nu, *_: (jnp.minimum(i, nu[0] - 1), 0)
    return pl.pallas_call(
        _experts_kernel,
        grid_spec=pltpu.PrefetchScalarGridSpec(
            num_scalar_prefetch=6,
            grid=(n_blocks,),
            in_specs=[pl.BlockSpec((MOE_TM, D), row_map),
                      pl.BlockSpec(memory_space=pl.ANY),
                      pl.BlockSpec(memory_space=pl.ANY),
                      pl.BlockSpec(memory_space=pl.ANY)],
            out_specs=pl.BlockSpec((MOE_TM, D), row_map),
            scratch_shapes=[pltpu.VMEM((2, D, de), F32), pltpu.VMEM((2, D, de), F32), pltpu.VMEM((2, de, D), F32),
                            pltpu.VMEM((D, de), BF16), pltpu.VMEM((D, de), BF16), pltpu.VMEM((de, D), BF16),
                            pltpu.SemaphoreType.DMA((2, 3))]),
        out_shape=jax.ShapeDtypeStruct((R, D), BF16),
        input_output_aliases={6: 0},
        compiler_params=_cparams(1),
        name="experts",
    )(blk_e, nused, nsub, run_first, next_e, run_slot, buf, w_gate, w_up, w_down)


def _combine_kernel(npieces_ref, glob_ref, x2_ref, rti_ref, rtf_ref, gfin_ref, yb_ref, out_ref,
                    rows_ref, sem):
    tt = x2_ref.shape[0] // MOE_STEP_TILES
    sl = rows_ref.shape[1] // MOE_STEP_TILES
    i = pl.program_id(0)
    n_steps = pl.num_programs(0)
    cur = lax.rem(i, 2)

    def piece_copy(buf_slot, sub, local_row, global_row):
        return pltpu.make_async_copy(yb_ref.at[pl.ds(global_row, ROW_PIECE)],
                                     rows_ref.at[buf_slot, pl.ds(sub * sl + local_row, ROW_PIECE)], sem.at[buf_slot])

    def for_step_pieces(step, buf_slot, act):
        for sub in range(MOE_STEP_TILES):
            _for_each_piece(npieces_ref, glob_ref, step * MOE_STEP_TILES + sub,
                            lambda l, g, sub=sub: act(piece_copy(buf_slot, sub, l, g)))

    @pl.when(i == 0)
    def _():
        rows_ref[...] = jnp.zeros_like(rows_ref)
        for_step_pieces(0, 0, lambda cp: cp.start())

    @pl.when(i + 1 < n_steps)
    def _():
        for_step_pieces(i + 1, 1 - cur, lambda cp: cp.start())

    for_step_pieces(i, cur, lambda cp: cp.wait())

    n_sub = _used_slot_groups(npieces_ref[i * MOE_STEP_TILES])
    for sub in range(1, MOE_STEP_TILES):
        n_sub = jnp.maximum(n_sub, _used_slot_groups(npieces_ref[i * MOE_STEP_TILES + sub]))
    for k in range(TOP_K * tt // SLOT_SUB, sl // SLOT_SUB + 1):
        @pl.when(n_sub == k)
        def _(k=k):
            m = k * SLOT_SUB
            lane = lax.broadcasted_iota(jnp.int32, (tt, m), 1)
            for sub in range(MOE_STEP_TILES):
                ts_rows = slice(sub * tt, (sub + 1) * tt)
                rti = rti_ref[ts_rows, :]
                rtf = rtf_ref[ts_rows, :]
                g = jnp.where(lane == rti[:, 2:3], rtf[:, 0:1],
                              jnp.where(lane == rti[:, 3:4], rtf[:, 1:2], 0.0)).astype(BF16)
                y = x2_ref[ts_rows, :].astype(F32) + _dot(g, rows_ref[cur, sub * sl:sub * sl + m, :])
                ms = jnp.mean(y * y, axis=-1, keepdims=True)
                out_ref[ts_rows, :] = y * lax.rsqrt(ms + EPS) * gfin_ref[...]


def _combine(x2, rti, rtf, g_final, yb, npieces, piece_glob):
    T, D = rti.shape[0], x2.shape[1]
    tt = MOE_STEP_TILES * MIX_TS
    tok = lambda i, *_: (i, 0)
    return pl.pallas_call(
        _combine_kernel,
        grid_spec=pltpu.PrefetchScalarGridSpec(
            num_scalar_prefetch=2,
            grid=(T // tt,),
            in_specs=[pl.BlockSpec((tt, D), tok),
                      pl.BlockSpec((tt, LANES), tok),
                      pl.BlockSpec((tt, LANES), tok),
                      pl.BlockSpec((1, D), lambda i, *_: (0, 0)),
                      pl.BlockSpec(memory_space=pl.ANY)],
            out_specs=pl.BlockSpec((tt, D), tok),
            scratch_shapes=[pltpu.VMEM((2, MOE_STEP_TILES * MOE_SL, D), BF16),
                            pltpu.SemaphoreType.DMA((2,))]),
        out_shape=jax.ShapeDtypeStruct((T, D), F32),
        compiler_params=_cparams(1),
        name="combine",
    )(npieces, piece_glob, x2, rti, rtf, g_final, yb)


def _pad_lanes(a, width=LANES):
    return jnp.pad(a, ((0, 0), (0, width - a.shape[1])))


def kernel(x, g_mix, w_in, b_if, conv_q, conv_k, g_head, w_pool, pool_scale, w_br_a, w_br_b, w_out,
           g_ffn, w_rg, b_rg, w_re, b_re, w_e_gate, w_e_up, w_e_down, g_final):
    B, S, D = x.shape
    T = B * S
    assert g_mix.shape[0] == 1, "single-layer block"
    assert S % MIX_TS == 0 and (T // MIX_TS) % MOE_STEP_TILES == 0
    d_pool = w_br_a.shape[1]
    d_ml = w_br_b.shape[1]
    x2d = x.reshape(T, D)

    params = {
        "b_if": _pad_lanes(b_if[0][None, :]),
        "b_if_t": jnp.pad(b_if[0][:, None], ((0, 16 - 2 * N_HEADS), (0, 0))),
        "conv_q": conv_q[0], "conv_k": conv_k[0],
        "g_head": g_head[0][None, :],
        "w_pool": w_pool[0],
        "pool_scale": pool_scale[0][None, :],
        "w_br_a": w_br_a[0], "w_br_b": w_br_b[0],
        "w_out": w_out[0],
        "g_ffn": g_ffn[0][None, :],
        "w_r": _pad_lanes(jnp.concatenate([w_rg[0], w_re[0]], axis=1)).astype(BF16),
        "b_r": _pad_lanes(jnp.concatenate([b_rg[0], b_re[0]])[None, :]),
    }

    x2, xn2, rti, rtf, tstat, srow = _mixer(x2d, g_mix[0][None, :], w_in[0].astype(BF16), params, B, S)

    n_tiles = T // MIX_TS
    max_rows = n_tiles * (TOP_K * MIX_TS + N_EXPERTS * (ROW_PIECE - 1)) + N_EXPERTS * (MOE_TM - ROW_PIECE)
    n_rows = -(-max_rows // MOE_TM) * MOE_TM
    n_blocks = n_rows // MOE_TM
    i32 = lambda a: a.astype(jnp.int32)
    mm = lambda a, b: jnp.round(jnp.dot(a, b, precision=lax.Precision.HIGHEST, preferred_element_type=F32))
    e_ids = np.arange(N_EXPERTS)
    t_ids = np.arange(n_tiles)
    b_ids = np.arange(n_blocks)
    cum_e = jnp.asarray(e_ids[:, None] <= e_ids[None, :], F32)
    cum_t = jnp.asarray(t_ids[:, None] >= t_ids[None, :], F32)
    cum_b = jnp.asarray(b_ids[:, None] <= b_ids[None, :], F32)
    e_row = jnp.asarray(e_ids[None, :], F32)

    pcs = tstat.reshape(n_tiles, 8, LANES)[:, 0, ROUTER_LANE0:ROUTER_LANE0 + N_EXPERTS].astype(F32)
    piece_end = mm(pcs, cum_e)
    piece_loc = piece_end - pcs
    tile_cum = mm(cum_t, pcs)
    rows_e = tile_cum[-1:, :] * ROW_PIECE
    padded = jnp.floor((rows_e + (MOE_TM - 1)) * (1.0 / MOE_TM)) * MOE_TM
    pend = mm(padded, cum_e)
    poff = pend - padded
    piece_glob = poff * (1.0 / ROW_PIECE) + tile_cum - pcs
    nused_f = pend[0, -1] * (1.0 / MOE_TM)
    blk_start = jnp.asarray(b_ids[:, None] * MOE_TM, F32)
    blk_e_f = jnp.minimum(jnp.sum((pend <= blk_start).astype(F32), axis=1, keepdims=True), N_EXPERTS - 1.0)
    blk_oh = (blk_e_f == e_row).astype(F32)
    later_nonempty = (e_ids[None, :] > e_ids[:, None]) & (rows_e > 0)
    next_of_e = jnp.min(jnp.where(later_nonempty, e_row, float(N_EXPERTS)), axis=1)
    next_of_e = jnp.where(next_of_e == N_EXPERTS, -1.0, next_of_e)
    per_blk = mm(blk_oh, jnp.stack([pend[0], (poff + rows_e)[0], next_of_e], axis=1))
    blk_used = blk_start < pend[0, -1]
    rows_in_blk = jnp.clip(per_blk[:, 1:2] - blk_start, 0.0, float(MOE_TM))
    nsub = jnp.floor((rows_in_blk + (EXPERT_SUB - 1)) * (1.0 / EXPERT_SUB))
    prev_e = jnp.concatenate([jnp.full((1, 1), -1.0, F32), blk_e_f[:-1]], axis=0)
    run_first = (blk_used & (blk_e_f != prev_e)).astype(F32)
    run_idx = mm(run_first.reshape(1, n_blocks), cum_b) - 1.0
    run_slot = run_idx - 2.0 * jnp.floor(run_idx * 0.5)
    p_ids = jnp.asarray(np.arange(PIECES_PER_TILE), F32)
    e_of_p = jnp.minimum(jnp.sum((piece_end[:, None, :] <= p_ids[None, :, None]).astype(F32), axis=2),
                         N_EXPERTS - 1.0)
    shift = jnp.sum(jnp.where(e_of_p[:, :, None] == e_row[None], (piece_glob - piece_loc)[:, None, :], 0.0), axis=2)
    glob_of_p = i32(shift + p_ids[None, :]).reshape(n_tiles * PIECES_PER_TILE)
    npieces = i32(piece_end[:, -1])
    flat_b = lambda a: i32(a).reshape(n_blocks)
    blk_e, nused = flat_b(blk_e_f), i32(nused_f).reshape(1)

    flat_e = lambda a: i32(a).reshape(N_EXPERTS)
    buf = _dispatch(xn2, srow, npieces, glob_of_p, flat_e(poff + rows_e), flat_e(padded - rows_e), nused, n_rows)
    yb = _experts(buf, blk_e, nused, flat_b(nsub), flat_b(run_first), flat_b(per_blk[:, 2:3]), flat_b(run_slot),
                  w_e_gate[0], w_e_up[0], w_e_down[0])
    out = _combine(x2, rti, rtf, g_final[None, :], yb, npieces, glob_of_p)
    return out.reshape(B, S, D)
```

```python
import functools

import numpy as np
import jax
import jax.numpy as jnp
from jax import lax
from jax.experimental import pallas as pl
from jax.experimental.pallas import tpu as pltpu

F32 = jnp.float32
BF16 = jnp.bfloat16

CHUNK = 64
POOL_WINDOWS = (2, 4, 8, 16)
N_HEADS = 4
CONV_K = 4
N_GROUPS = 4
EXPERTS_PER_GROUP = 8
N_EXPERTS = N_GROUPS * EXPERTS_PER_GROUP
TOP_K = 2
EPS = 1e-6

LANES = 128
HALO = 16
ROUTER_LANE0 = N_GROUPS

INPROJ_TN = 256
MIX_TS = 256
MOE_TM = 512
ROW_PIECE = 16
MOE_SL = TOP_K * MIX_TS + N_EXPERTS * ROW_PIECE
PIECES_PER_TILE = MOE_SL // ROW_PIECE
MOE_STEP_TILES = 4
SLOT_SUB = 128
PIECE_UNROLL = 4
EXPERT_SUB = 128
SLOT_RADIX = 16
VMEM_LIMIT = 56 * 1024 * 1024


def _cparams(n_axes):
    return pltpu.CompilerParams(dimension_semantics=("arbitrary",) * n_axes,
                                vmem_limit_bytes=VMEM_LIMIT)


def _sigmoid(v):
    return 0.5 * jnp.tanh(0.5 * v) + 0.5


def _silu(v):
    return v * _sigmoid(v)


def _log_sigmoid(v):
    return jnp.minimum(v, 0.0) - jnp.log1p(jnp.exp(-jnp.abs(v)))


def _split3(v):
    hi = v.astype(BF16)
    r1 = v - hi.astype(F32)
    mid = r1.astype(BF16)
    lo = (r1 - mid.astype(F32)).astype(BF16)
    return hi, mid, lo


def _dot(a, b):
    return jnp.dot(a, b, preferred_element_type=F32)


def _dot_nt(a, b):
    return lax.dot_general(a, b, (((1,), (1,)), ((), ())), preferred_element_type=F32)


def _dot_tn(a, b):
    return lax.dot_general(a, b, (((0,), (0,)), ((), ())), preferred_element_type=F32)


def _inproj_steps(x_ref, g_ref, w_refs, z_refs, xn_ref):
    w_ref_all, wg_ref = w_refs
    zm_ref, zg_ref, zif_ref, zift_ref = z_refs
    n_main = zm_ref.shape[1]

    def norm():
        x = x_ref[...]
        ms = jnp.mean(x * x, axis=-1, keepdims=True)
        xn_ref[...] = (x * lax.rsqrt(ms + EPS) * g_ref[...]).astype(BF16)

    def block(w_ref, z_ref, c0):
        def run():
            cols = slice(c0, c0 + INPROJ_TN)
            z_ref[:, cols] = _dot(xn_ref[...], w_ref[:, cols]).astype(BF16)
        return run

    def gates():
        zif = _dot(xn_ref[...], w_ref_all[:, n_main:n_main + LANES])
        zif_ref[...] = zif
        zift_ref[...] = zif.T[:zift_ref.shape[0], :]

    steps = [norm, gates]
    steps += [block(w_ref_all, zm_ref, c0) for c0 in range(0, n_main, INPROJ_TN)]
    steps += [block(wg_ref, zg_ref, c0) for c0 in range(0, zg_ref.shape[1], INPROJ_TN)]
    return steps


def _mixer_kernel(x_ref, xnext_ref, gmix_ref, win_ref,
                  bif_ref, bift_ref, convq_ref, convk_ref, ghead_ref, wpool32_ref, pscale_ref,
                  wa32_ref, wb32_ref, wo32_ref, gffn_ref, wr_ref, br_ref,
                  tric_ref, trir_ref, stri_ref, ut_ref, sel_ref,
                  x2_ref, xn2_ref, rti_ref, rtf_ref, tstat_ref, srow_ref,
                  zm_ref, zg_ref, zif_ref, zift_ref, zm_nxt, zg_nxt, zif_nxt, zift_nxt, xn_ref,
                  ext_ref, q_ref, k_ref, h_ref, pool_ref, cst_ref, mst_ref, lg_ref, wg_ref,
                  wpool_ref, wa_ref, wb_ref, wo_ref, *, tiles_per_seq):
    ts = x_ref.shape[0]
    d_pool = wa_ref.shape[0]
    d_ml = wb_ref.shape[0]
    dh = d_ml // N_HEADS
    n_chunks = ts // CHUNK
    g_step = pl.program_id(0)
    j = lax.rem(g_step, tiles_per_seq)
    w_in_refs = (win_ref, wg_ref)
    z_cur = (zm_ref, zg_ref, zif_ref, zift_ref)
    z_nxt = (zm_nxt, zg_nxt, zif_nxt, zift_nxt)
    first = g_step == 0

    @pl.when(first)
    def _():
        g0 = zm_ref.shape[1] + 2 * N_HEADS
        wg_ref[...] = win_ref[:, g0:g0 + wg_ref.shape[1]]
        for dst, src in ((wpool_ref, wpool32_ref), (wa_ref, wa32_ref), (wb_ref, wb32_ref), (wo_ref, wo32_ref)):
            dst[...] = src[...].astype(BF16)
        for step in _inproj_steps(x_ref, gmix_ref, w_in_refs, z_cur, xn_ref):
            step()
        lg_ref[...] = jnp.zeros_like(lg_ref)

    @pl.when(jnp.logical_not(first))
    def _():
        for dst, src in zip(z_cur, z_nxt):
            dst[...] = src[...]

    @pl.when(j == 0)
    def _():
        ext_ref[:, :HALO, :] = jnp.zeros((ext_ref.shape[0], HALO, LANES), F32)
        cst_ref[...] = jnp.zeros_like(cst_ref)
        mst_ref[...] = jnp.zeros_like(mst_ref)

    pending = _inproj_steps(xnext_ref, gmix_ref, w_in_refs, z_nxt, xn_ref)

    def project_some(n=1):
        for _ in range(min(n, len(pending))):
            pending.pop(0)()

    project_some(2)

    routed = _route_select(lg_ref[...])

    row = lax.broadcasted_iota(jnp.int32, (ts, LANES), 0)
    pos1 = (row + j * ts + 1).astype(F32)

    def history(cg):
        cur = zm_ref[:, cg * LANES:(cg + 1) * LANES].astype(F32)
        ext_ref[cg, HALO:, :] = cur
        return cur, lambda s: ext_ref[cg, HALO - s:HALO - s + ts, :]

    def keep_history(cg, cur):
        ext_ref[cg, :HALO, :] = cur[ts - HALO:, :]

    n_pool_groups = d_pool // LANES
    for g in range(n_pool_groups):
        w = POOL_WINDOWS[g]
        cur, shifted = history(g)
        win = cur
        for s in range(1, w):
            win = win + shifted(s)
        keep_history(g, cur)
        cnt = jnp.minimum(pos1, float(w))
        d = win / cnt - cur
        y = _dot(d.astype(BF16), wpool_ref[g]) * pscale_ref[:, g * LANES:(g + 1) * LANES]
        pool_ref[:, g * LANES:(g + 1) * LANES] = y.astype(BF16)
        project_some()

    n_ml_groups = d_ml // LANES
    for which, (cw_ref, dst_ref, scale) in enumerate(((convq_ref, q_ref, 1.0), (convk_ref, k_ref, dh ** -0.5))):
        for g in range(n_ml_groups):
            cols = slice(g * LANES, (g + 1) * LANES)
            cg = n_pool_groups + which * n_ml_groups + g
            cur, shifted = history(cg)
            acc = cur * cw_ref[CONV_K - 1:CONV_K, cols]
            for sft in range(1, CONV_K):
                acc = acc + shifted(sft) * cw_ref[CONV_K - 1 - sft:CONV_K - sft, cols]
            keep_history(cg, cur)
            dst_ref[:, cols] = (_silu(acc) * scale).astype(BF16)
        project_some()

    _route_slots(routed, stri_ref, ut_ref, sel_ref, rti_ref, rtf_ref, tstat_ref, srow_ref)
    project_some()

    zc = zif_ref[...] + bif_ref[...]
    lf_c = _log_sigmoid(zc)
    bc = sum(_dot(tric_ref[...], p) for p in _split3(lf_c))
    zr = zift_ref[...] + bift_ref[...]
    lf_r = _log_sigmoid(zr)
    br = sum(_dot(p, trir_ref[...]) for p in _split3(lf_r))
    project_some(2)

    ti = lax.broadcasted_iota(jnp.int32, (CHUNK, CHUNK), 0)
    si = lax.broadcasted_iota(jnp.int32, (CHUNK, CHUNK), 1)
    causal = si <= ti
    ones_blk = jnp.ones((CHUNK, dh), BF16)
    v0 = d_pool + 2 * d_ml
    ig_rep = [jnp.broadcast_to(zc[:, h:h + 1], (ts, dh)) for h in range(N_HEADS)]
    bt_rep = [jnp.broadcast_to(bc[:, N_HEADS + h:N_HEADS + h + 1], (ts, dh)) for h in range(N_HEADS)]

    m_state = [mst_ref[h:h + 1, :] for h in range(N_HEADS)]
    c_state = [cst_ref[h] for h in range(N_HEADS)]
    def stage_scores(c):
        rs = slice(c * CHUNK, (c + 1) * CHUNK)
        out = []
        for h in range(N_HEADS):
            hs = slice(h * dh, (h + 1) * dh)
            q = q_ref[rs, hs]
            k = k_ref[rs, hs]
            bt = bt_rep[h][rs, :]
            r_row = zr[h:h + 1, rs] - br[N_HEADS + h:N_HEADS + h + 1, rs]
            dmat = jnp.where(causal, bt[:, :CHUNK] + r_row, -jnp.inf)
            out.append(dict(q=q, k=k, bt=bt, dmat=dmat, qk=_dot_nt(q, k),
                            m_intra=jnp.max(dmat, axis=-1, keepdims=True)))
        return out

    def stage_state(c, st):
        rs = slice(c * CHUNK, (c + 1) * CHUNK)
        for h in range(N_HEADS):
            s = st[h]
            bt, k = s["bt"], s["k"]
            m_prev, c_prev = m_state[h], c_state[h]
            v_aug = jnp.concatenate([zm_ref[rs, v0 + h * dh:v0 + (h + 1) * dh], ones_blk], axis=-1)
            igc = ig_rep[h][rs, :]
            b_last = bt[CHUNK - 1:CHUNK, :]
            a_log = b_last - bt + igc
            a_max = jnp.max(a_log, axis=0, keepdims=True)
            m_new = jnp.maximum(b_last + m_prev, a_max)
            kw = (k.astype(F32) * jnp.exp(a_log - m_new)).astype(BF16)
            decay = jnp.exp(b_last + m_prev - m_new)
            s.update(v_aug=v_aug, m_prev=m_prev, qc=_dot(s["q"], c_prev.astype(BF16)))
            c_state[h] = jnp.concatenate([decay, decay], axis=-1) * c_prev + _dot_tn(kw, v_aug)
            m_state[h] = m_new

    def stage_values(c, st):
        rs = slice(c * CHUNK, (c + 1) * CHUNK)
        for h in range(N_HEADS):
            s = st[h]
            hs = slice(h * dh, (h + 1) * dh)
            inter = s["bt"] + s["m_prev"]
            m_t = jnp.maximum(inter, s["m_intra"])
            w_inter = jnp.exp(inter - m_t)
            smat = s["qk"] * jnp.exp(s["dmat"] - m_t[:, :CHUNK])
            sv = _dot(smat.astype(BF16), s["v_aug"])
            qc = s["qc"]
            nq = w_inter * qc[:, dh:] + sv[:, dh:]
            den = jnp.maximum(jnp.abs(nq), jnp.exp(-m_t))
            h_ref[rs, hs] = (w_inter * qc[:, :dh] + sv[:, :dh]) / den

    staged = stage_scores(0)
    for c in range(n_chunks):
        stage_state(c, staged)
        project_some()
        nxt = stage_scores(c + 1) if c + 1 < n_chunks else None
        project_some()
        stage_values(c, staged)
        staged = nxt
    for h in range(N_HEADS):
        cst_ref[h] = c_state[h]
        mst_ref[h:h + 1, :] = m_state[h]

    o0 = v0 + d_ml
    for h in range(N_HEADS):
        hs = slice(h * dh, (h + 1) * dh)
        hv = h_ref[:, hs]
        mu = jnp.mean(hv, axis=-1, keepdims=True)
        hc = hv - mu
        var = jnp.mean(hc * hc, axis=-1, keepdims=True)
        hn = hc * lax.rsqrt(var + EPS) * ghead_ref[:, hs]
        og = _sigmoid(zm_ref[:, o0 + h * dh:o0 + (h + 1) * dh].astype(F32))
        q_ref[:, hs] = (og * hn).astype(BF16)
    y_a = _dot(pool_ref[...], wa_ref[...])
    y_b = _dot(q_ref[...], wb_ref[...])
    d_model = x_ref.shape[1]
    ga = _sigmoid(zg_ref[:, :d_model].astype(F32))
    gb = _sigmoid(zg_ref[:, d_model:].astype(F32))
    merged = (ga * y_a + gb * y_b).astype(BF16)
    x2 = x_ref[...] + _dot(merged, wo_ref[...])
    x2_ref[...] = x2.astype(BF16)
    project_some(len(pending))

    ms = jnp.mean(x2 * x2, axis=-1, keepdims=True)
    xn2 = x2 * lax.rsqrt(ms + EPS) * gffn_ref[...]
    xh = xn2.astype(BF16)
    xn2_ref[...] = xh
    lg_ref[...] = _dot(xh, wr_ref[...]) + br_ref[...]
    project_some(len(pending))


def _route_select(lg):
    ts = lg.shape[0]
    lane = lax.broadcasted_iota(jnp.int32, (ts, LANES), 1)
    lanef = lane.astype(F32)
    big = float(4 * LANES)
    gl = jnp.where(lane < N_GROUPS, lg, -jnp.inf)
    gmax = jnp.max(gl, axis=-1, keepdims=True)
    g_sel = jnp.min(jnp.where(gl == gmax, lanef, big), axis=-1, keepdims=True)
    p_g = 1.0 / jnp.sum(jnp.exp(gl - gmax), axis=-1, keepdims=True)
    lo = ROUTER_LANE0 + EXPERTS_PER_GROUP * g_sel
    el = jnp.where((lanef >= lo) & (lanef < lo + EXPERTS_PER_GROUP), lg, -jnp.inf)
    m1 = jnp.max(el, axis=-1, keepdims=True)
    i1 = jnp.min(jnp.where(el == m1, lanef, big), axis=-1, keepdims=True)
    el2 = jnp.where(lanef == i1, -jnp.inf, el)
    m2 = jnp.max(el2, axis=-1, keepdims=True)
    i2 = jnp.min(jnp.where(el2 == m2, lanef, big), axis=-1, keepdims=True)
    e2x = jnp.exp(m2 - m1)
    gate1 = p_g / (1.0 + e2x)
    gate2 = p_g * e2x / (1.0 + e2x)
    return dict(lane=lane, i1=i1, i2=i2, gate1=gate1, gate2=gate2, oh1=lanef == i1, oh2=lanef == i2)


def _route_slots(r, stri_ref, ut_ref, sel_ref, rti_ref, rtf_ref, tstat_ref, srow_ref):
    lane, oh1, oh2, i1, i2 = r["lane"], r["oh1"], r["oh2"], r["i1"], r["i2"]
    ohs = jnp.where(oh1 | oh2, 1.0, 0.0)
    n_loc = jnp.sum(ohs, axis=0, keepdims=True)
    pieces = jnp.floor((n_loc + (ROW_PIECE - 1.0)) * (1.0 / ROW_PIECE))
    piece_off = _dot(jnp.broadcast_to(pieces, (8, LANES)).astype(BF16), ut_ref[...])[0:1, :]
    base = _dot(stri_ref[...], ohs.astype(BF16)) + ROW_PIECE * piece_off
    slot1 = jnp.sum(jnp.where(oh1, base, 0.0), axis=-1, keepdims=True)
    slot2 = jnp.sum(jnp.where(oh2, base, 0.0), axis=-1, keepdims=True)
    tstat_ref[...] = jnp.broadcast_to(pieces, tstat_ref.shape).astype(jnp.int32)

    rti = jnp.where(lane == 0, i1 - ROUTER_LANE0,
                    jnp.where(lane == 1, i2 - ROUTER_LANE0,
                              jnp.where(lane == 2, slot1, jnp.where(lane == 3, slot2, 0.0))))
    rti_ref[...] = rti.astype(jnp.int32)
    rtf_ref[...] = jnp.where(lane == 0, r["gate1"], jnp.where(lane == 1, r["gate2"], 0.0))
    h1 = jnp.floor(slot1 * (1.0 / SLOT_RADIX))
    h2 = jnp.floor(slot2 * (1.0 / SLOT_RADIX))
    parts = jnp.where(lane == 0, h1, jnp.where(lane == 1, slot1 - SLOT_RADIX * h1,
                      jnp.where(lane == 2, h2, jnp.where(lane == 3, slot2 - SLOT_RADIX * h2, 0.0))))
    srow_ref[...] = _dot_nt(sel_ref[...], parts.astype(BF16))


def _mixer(x2d, g_mix, w_all, params, batch, seq):
    T, D = x2d.shape
    ts = min(MIX_TS, seq)
    nts = seq // ts
    d_pool = params["w_br_a"].shape[0]
    d_ml = params["w_br_b"].shape[0]
    dh = d_ml // N_HEADS

    idx = np.arange(ts)
    same_chunk = (idx[:, None] // CHUNK) == (idx[None, :] // CHUNK)
    tri_c = jnp.asarray((idx[None, :] <= idx[:, None]) & same_chunk, BF16)
    tri_r = jnp.asarray((idx[:, None] <= idx[None, :]) & same_chunk, BF16)
    stri = jnp.asarray(idx[None, :] < idx[:, None], BF16)
    lane_idx = np.arange(LANES)
    ut = jnp.asarray(lane_idx[:, None] < lane_idx[None, :], BF16)
    sel = jnp.asarray(np.arange(8)[:, None] == lane_idx[None, :], BF16)

    n_tiles = batch * nts
    tok = lambda g: (g, 0)
    tok_in = lambda g: (jnp.minimum(g, n_tiles - 1), 0)
    tok_next = lambda g: (jnp.minimum(g + 1, n_tiles - 1), 0)
    tok_prev = lambda g: (jnp.maximum(g - 1, 0), 0)
    tok_prev_t = lambda g: (0, jnp.maximum(g - 1, 0))
    c2 = lambda g: (0, 0)
    c3 = lambda g: (0, 0, 0)
    full = lambda a: pl.BlockSpec(a.shape, c2 if a.ndim == 2 else c3)
    consts = [params[n] for n in ("b_if", "b_if_t", "conv_q", "conv_k", "g_head", "w_pool", "pool_scale",
                                  "w_br_a", "w_br_b", "w_out", "g_ffn", "w_r", "b_r")]
    consts = [g_mix, w_all] + consts + [tri_c, tri_r, stri, ut, sel]
    nm, ng = d_pool + 4 * d_ml, 2 * D
    z_scratch = [pltpu.VMEM((ts, nm), BF16), pltpu.VMEM((ts, ng), BF16),
                 pltpu.VMEM((ts, LANES), F32), pltpu.VMEM((16, ts), F32)]
    return pl.pallas_call(
        functools.partial(_mixer_kernel, tiles_per_seq=nts),
        grid=(n_tiles + 1,),
        in_specs=[pl.BlockSpec((ts, D), tok_in),
                  pl.BlockSpec((ts, D), tok_next)] + [full(a) for a in consts],
        out_specs=[pl.BlockSpec((ts, D), tok),
                   pl.BlockSpec((ts, D), tok),
                   pl.BlockSpec((ts, LANES), tok_prev),
                   pl.BlockSpec((ts, LANES), tok_prev),
                   pl.BlockSpec((8, LANES), tok_prev),
                   pl.BlockSpec((8, ts), tok_prev_t)],
        out_shape=[jax.ShapeDtypeStruct((T + ts, D), BF16),
                   jax.ShapeDtypeStruct((T + ts, D), BF16),
                   jax.ShapeDtypeStruct((T, LANES), jnp.int32),
                   jax.ShapeDtypeStruct((T, LANES), F32),
                   jax.ShapeDtypeStruct((n_tiles * 8, LANES), jnp.int32),
                   jax.ShapeDtypeStruct((8, T), F32)],
        scratch_shapes=z_scratch + z_scratch + [
                        pltpu.VMEM((ts, D), BF16),
                        pltpu.VMEM(((d_pool + 2 * d_ml) // LANES, HALO + ts, LANES), F32),
                        pltpu.VMEM((ts, d_ml), BF16),
                        pltpu.VMEM((ts, d_ml), BF16),
                        pltpu.VMEM((ts, d_ml), F32),
                        pltpu.VMEM((ts, d_pool), BF16),
                        pltpu.VMEM((N_HEADS, dh, 2 * dh), F32),
                        pltpu.VMEM((8, LANES), F32),
                        pltpu.VMEM((ts, LANES), F32),
                        pltpu.VMEM((D, ng), BF16),
                        pltpu.VMEM(params["w_pool"].shape, BF16),
                        pltpu.VMEM((d_pool, D), BF16),
                        pltpu.VMEM((d_ml, D), BF16),
                        pltpu.VMEM((D, D), BF16)],
        compiler_params=_cparams(1),
        name="mixer",
    )(x2d, x2d, *consts)


def _for_each_piece(npieces_ref, glob_ref, tile, fn):
    base = tile * PIECES_PER_TILE
    n = npieces_ref[tile]

    def one(p, lane):
        fn(pl.multiple_of(p * ROW_PIECE, ROW_PIECE), pl.multiple_of(glob_ref[base + p] * ROW_PIECE, ROW_PIECE), lane)

    def group(g, carry):
        for u in range(PIECE_UNROLL):
            one(g * PIECE_UNROLL + u, u)
        return carry

    n_groups = lax.div(n, jnp.int32(PIECE_UNROLL))
    lax.fori_loop(0, n_groups, group, 0)
    for u in range(PIECE_UNROLL - 1):
        @pl.when(n_groups * PIECE_UNROLL + u < n)
        def _(u=u):
            one(n_groups * PIECE_UNROLL + u, u)


def _WAIT(cp, lane=None):
    cp.wait()


def _START_ALTERNATING(cp, lane):
    cp.start(priority=lane % 2)


def _used_slot_groups(n_pieces):
    return lax.div(n_pieces * ROW_PIECE + (SLOT_SUB - 1), jnp.int32(SLOT_SUB))


def _dispatch_kernel(npieces_ref, glob_ref, pad_start_ref, pad_rows_ref, nused_ref,
                     xn_ref, srow_ref, buf_ref, rows_ref, zeros_ref, sem, zsem, tsem, *, n_steps):
    tt = xn_ref.shape[0] // MOE_STEP_TILES
    sl = rows_ref.shape[1] // MOE_STEP_TILES
    n_blocks = buf_ref.shape[0] // MOE_TM
    i = pl.program_id(0)
    cur = lax.rem(i, 2)

    def zero_fill_share(step, act):
        for k in range(-(-N_EXPERTS // n_steps)):
            e = step + k * n_steps

            @pl.when(e < N_EXPERTS)
            def _(e=e):
                e_c = jnp.minimum(e, N_EXPERTS - 1)
                rows = pad_rows_ref[e_c]
                off = pad_start_ref[e_c]
                size = MOE_TM // 2
                while size >= ROW_PIECE:
                    has = (rows & size) != 0

                    @pl.when(has)
                    def _(off=off, size=size):
                        act(pltpu.make_async_copy(zeros_ref.at[pl.ds(0, size)],
                                                  buf_ref.at[pl.ds(pl.multiple_of(off, ROW_PIECE), size)], zsem))
                    off = off + jnp.where(has, size, 0)
                    size //= 2
        for k in range(-(-n_blocks // n_steps)):
            b = step + k * n_steps

            @pl.when((b >= nused_ref[0]) & (b < n_blocks))
            def _(b=b):
                b_in = jnp.minimum(b, n_blocks - 1)
                act(pltpu.make_async_copy(zeros_ref, buf_ref.at[pl.ds(b_in * MOE_TM, MOE_TM)], tsem))

    @pl.when(i == 0)
    def _():
        zeros_ref[...] = jnp.zeros_like(zeros_ref)

    zero_fill_share(i, lambda cp: cp.start())

    def piece_copy(buf_slot, sub, local_row, global_row):
        return pltpu.make_async_copy(rows_ref.at[buf_slot, pl.ds(sub * sl + local_row, ROW_PIECE)],
                                     buf_ref.at[pl.ds(global_row, ROW_PIECE)], sem.at[buf_slot])

    def for_step_pieces(step, buf_slot, act):
        for sub in range(MOE_STEP_TILES):
            _for_each_piece(npieces_ref, glob_ref, step * MOE_STEP_TILES + sub,
                            lambda l, g, lane, sub=sub: act(piece_copy(buf_slot, sub, l, g), lane))

    @pl.when(i >= 2)
    def _():
        for_step_pieces(i - 2, cur, _WAIT)

    n_sub = _used_slot_groups(npieces_ref[i * MOE_STEP_TILES])
    for sub in range(1, MOE_STEP_TILES):
        n_sub = jnp.maximum(n_sub, _used_slot_groups(npieces_ref[i * MOE_STEP_TILES + sub]))
    for k in range(TOP_K * tt // SLOT_SUB, sl // SLOT_SUB + 1):
        @pl.when(n_sub == k)
        def _(k=k):
            m = k * SLOT_SUB
            r = lax.broadcasted_iota(jnp.int32, (m, tt), 0).astype(F32)
            for sub in range(MOE_STEP_TILES):
                sr = srow_ref[:, sub * tt:(sub + 1) * tt]
                slot1 = SLOT_RADIX * sr[0:1, :] + sr[1:2, :]
                slot2 = SLOT_RADIX * sr[2:3, :] + sr[3:4, :]
                sel = jnp.where((r == slot1) | (r == slot2), 1.0, 0.0).astype(BF16)
                rows_ref[cur, sub * sl:sub * sl + m, :] = _dot(sel, xn_ref[sub * tt:(sub + 1) * tt, :]).astype(BF16)
    for_step_pieces(i, cur, _START_ALTERNATING)

    @pl.when(i == n_steps - 1)
    def _():
        @pl.when(i >= 1)
        def _():
            for_step_pieces(i - 1, 1 - cur, _WAIT)
        for_step_pieces(i, cur, _WAIT)

        def wait_share(step, carry):
            zero_fill_share(step, _WAIT)
            return carry

        lax.fori_loop(0, n_steps, wait_share, 0)


def _dispatch(xn2, srow, npieces, piece_glob, pad_start, pad_rows, nused, n_rows):
    T, D = srow.shape[1], xn2.shape[1]
    tt = MOE_STEP_TILES * MIX_TS
    return pl.pallas_call(
        functools.partial(_dispatch_kernel, n_steps=T // tt),
        grid_spec=pltpu.PrefetchScalarGridSpec(
            num_scalar_prefetch=5,
            grid=(T // tt,),
            in_specs=[pl.BlockSpec((tt, D), lambda i, *_: (i, 0)),
                      pl.BlockSpec((8, tt), lambda i, *_: (0, i))],
            out_specs=pl.BlockSpec(memory_space=pl.ANY),
            scratch_shapes=[pltpu.VMEM((2, MOE_STEP_TILES * MOE_SL, D), BF16),
                            pltpu.VMEM((MOE_TM, D), BF16),
                            pltpu.SemaphoreType.DMA((2,)),
                            pltpu.SemaphoreType.DMA(()),
                            pltpu.SemaphoreType.DMA(())]),
        out_shape=jax.ShapeDtypeStruct((n_rows, D), BF16),
        compiler_params=_cparams(1),
        name="dispatch",
    )(npieces, piece_glob, pad_start, pad_rows, nused, xn2, srow)


def _experts_kernel(blk_e_ref, nused_ref, nsub_ref, first_ref, next_e_ref, slot_ref,
                    x_ref, wg_hbm, wu_hbm, wd_hbm, y_ref,
                    wg32_ref, wu32_ref, wd32_ref, wgb_ref, wub_ref, wdb_ref, sem):
    i = pl.program_id(0)
    used = i < nused_ref[0]
    n_sub = nsub_ref[i]
    landing = ((wg_hbm, wg32_ref), (wu_hbm, wu32_ref), (wd_hbm, wd32_ref))

    def weight_copies(e, s):
        return [pltpu.make_async_copy(hbm.at[e], vmem.at[s], sem.at[s, n]) for n, (hbm, vmem) in enumerate(landing)]

    @pl.when(used & (i == 0))
    def _():
        for cp in weight_copies(blk_e_ref[0], 0):
            cp.start()

    run_start = used & (first_ref[i] > 0)
    s = slot_ref[i]

    @pl.when(run_start)
    def _():
        for cp in weight_copies(blk_e_ref[i], s):
            cp.wait()

        @pl.when(next_e_ref[i] >= 0)
        def _():
            for cp in weight_copies(next_e_ref[i], 1 - s):
                cp.start()

    def swiglu(m, wg, wu, wd):
        x = x_ref[:m, :]
        hg = _dot(x, wg)
        hu = _dot(x, wu)
        hid = (_silu(hg) * hu).astype(BF16)
        y_ref[:m, :] = _dot(hid, wd).astype(BF16)
        if m < MOE_TM:
            y_ref[m:, :] = jnp.zeros((MOE_TM - m, y_ref.shape[1]), BF16)

    for k in range(1, MOE_TM // EXPERT_SUB + 1):
        @pl.when(run_start & (n_sub == k))
        def _(k=k):
            wg = wg32_ref[s].astype(BF16)
            wu = wu32_ref[s].astype(BF16)
            wd = wd32_ref[s].astype(BF16)
            wgb_ref[...] = wg
            wub_ref[...] = wu
            wdb_ref[...] = wd
            swiglu(k * EXPERT_SUB, wg, wu, wd)

        @pl.when(used & jnp.logical_not(run_start) & (n_sub == k))
        def _(k=k):
            swiglu(k * EXPERT_SUB, wgb_ref[...], wub_ref[...], wdb_ref[...])


def _experts(buf, blk_e, nused, nsub, run_first, next_e, run_slot, w_gate, w_up, w_down):
    R, D = buf.shape
    de = w_gate.shape[2]
    n_blocks = R // MOE_TM
    row_map = lambda i, be, nu, *_: (jnp.minimum(i, nu[0] - 1), 0)
    return pl.pallas_call(
        _experts_kernel,
        grid_spec=pltpu.PrefetchScalarGridSpec(
            num_scalar_prefetch=6,
            grid=(n_blocks,),
            in_specs=[pl.BlockSpec((MOE_TM, D), row_map),
                      pl.BlockSpec(memory_space=pl.ANY),
                      pl.BlockSpec(memory_space=pl.ANY),
                      pl.BlockSpec(memory_space=pl.ANY)],
            out_specs=pl.BlockSpec((MOE_TM, D), row_map),
            scratch_shapes=[pltpu.VMEM((2, D, de), F32), pltpu.VMEM((2, D, de), F32), pltpu.VMEM((2, de, D), F32),
                            pltpu.VMEM((D, de), BF16), pltpu.VMEM((D, de), BF16), pltpu.VMEM((de, D), BF16),
                            pltpu.SemaphoreType.DMA((2, 3))]),
        out_shape=jax.ShapeDtypeStruct((R, D), BF16),
        input_output_aliases={6: 0},
        compiler_params=_cparams(1),
        name="experts",
    )(blk_e, nused, nsub, run_first, next_e, run_slot, buf, w_gate, w_up, w_down)


def _combine_kernel(npieces_ref, glob_ref, x2_ref, rti_ref, rtf_ref, gfin_ref, yb_ref, out_ref,
                    rows_ref, sem):
    tt = x2_ref.shape[0] // MOE_STEP_TILES
    sl = rows_ref.shape[1] // MOE_STEP_TILES
    i = pl.program_id(0)
    n_steps = pl.num_programs(0)
    cur = lax.rem(i, 2)

    def piece_copy(buf_slot, sub, local_row, global_row):
        return pltpu.make_async_copy(yb_ref.at[pl.ds(global_row, ROW_PIECE)],
                                     rows_ref.at[buf_slot, pl.ds(sub * sl + local_row, ROW_PIECE)], sem.at[buf_slot])

    def for_step_pieces(step, buf_slot, act):
        for sub in range(MOE_STEP_TILES):
            _for_each_piece(npieces_ref, glob_ref, step * MOE_STEP_TILES + sub,
                            lambda l, g, lane, sub=sub: act(piece_copy(buf_slot, sub, l, g), lane))

    @pl.when(i == 0)
    def _():
        rows_ref[...] = jnp.zeros_like(rows_ref)
        for_step_pieces(0, 0, _START_ALTERNATING)

    @pl.when(i + 1 < n_steps)
    def _():
        for_step_pieces(i + 1, 1 - cur, _START_ALTERNATING)

    for_step_pieces(i, cur, _WAIT)

    n_sub = _used_slot_groups(npieces_ref[i * MOE_STEP_TILES])
    for sub in range(1, MOE_STEP_TILES):
        n_sub = jnp.maximum(n_sub, _used_slot_groups(npieces_ref[i * MOE_STEP_TILES + sub]))
    for k in range(TOP_K * tt // SLOT_SUB, sl // SLOT_SUB + 1):
        @pl.when(n_sub == k)
        def _(k=k):
            m = k * SLOT_SUB
            lane = lax.broadcasted_iota(jnp.int32, (tt, m), 1)
            for sub in range(MOE_STEP_TILES):
                ts_rows = slice(sub * tt, (sub + 1) * tt)
                rti = rti_ref[ts_rows, :]
                rtf = rtf_ref[ts_rows, :]
                g = jnp.where(lane == rti[:, 2:3], rtf[:, 0:1],
                              jnp.where(lane == rti[:, 3:4], rtf[:, 1:2], 0.0)).astype(BF16)
                y = x2_ref[ts_rows, :].astype(F32) + _dot(g, rows_ref[cur, sub * sl:sub * sl + m, :])
                ms = jnp.mean(y * y, axis=-1, keepdims=True)
                out_ref[ts_rows, :] = y * lax.rsqrt(ms + EPS) * gfin_ref[...]


def _combine(x2, rti, rtf, g_final, yb, npieces, piece_glob):
    T, D = rti.shape[0], x2.shape[1]
    tt = MOE_STEP_TILES * MIX_TS
    tok = lambda i, *_: (i, 0)
    return pl.pallas_call(
        _combine_kernel,
        grid_spec=pltpu.PrefetchScalarGridSpec(
            num_scalar_prefetch=2,
            grid=(T // tt,),
            in_specs=[pl.BlockSpec((tt, D), tok),
                      pl.BlockSpec((tt, LANES), tok),
                      pl.BlockSpec((tt, LANES), tok),
                      pl.BlockSpec((1, D), lambda i, *_: (0, 0)),
                      pl.BlockSpec(memory_space=pl.ANY)],
            out_specs=pl.BlockSpec((tt, D), tok),
            scratch_shapes=[pltpu.VMEM((2, MOE_STEP_TILES * MOE_SL, D), BF16),
                            pltpu.SemaphoreType.DMA((2,))]),
        out_shape=jax.ShapeDtypeStruct((T, D), F32),
        compiler_params=_cparams(1),
        name="combine",
    )(npieces, piece_glob, x2, rti, rtf, g_final, yb)


def _pad_lanes(a, width=LANES):
    return jnp.pad(a, ((0, 0), (0, width - a.shape[1])))


def kernel(x, g_mix, w_in, b_if, conv_q, conv_k, g_head, w_pool, pool_scale, w_br_a, w_br_b, w_out,
           g_ffn, w_rg, b_rg, w_re, b_re, w_e_gate, w_e_up, w_e_down, g_final):
    B, S, D = x.shape
    T = B * S
    assert g_mix.shape[0] == 1, "single-layer block"
    assert S % MIX_TS == 0 and (T // MIX_TS) % MOE_STEP_TILES == 0
    d_pool = w_br_a.shape[1]
    d_ml = w_br_b.shape[1]
    x2d = x.reshape(T, D)

    params = {
        "b_if": _pad_lanes(b_if[0][None, :]),
        "b_if_t": jnp.pad(b_if[0][:, None], ((0, 16 - 2 * N_HEADS), (0, 0))),
        "conv_q": conv_q[0], "conv_k": conv_k[0],
        "g_head": g_head[0][None, :],
        "w_pool": w_pool[0],
        "pool_scale": pool_scale[0][None, :],
        "w_br_a": w_br_a[0], "w_br_b": w_br_b[0],
        "w_out": w_out[0],
        "g_ffn": g_ffn[0][None, :],
        "w_r": _pad_lanes(jnp.concatenate([w_rg[0], w_re[0]], axis=1)).astype(BF16),
        "b_r": _pad_lanes(jnp.concatenate([b_rg[0], b_re[0]])[None, :]),
    }

    x2, xn2, rti, rtf, tstat, srow = _mixer(x2d, g_mix[0][None, :], w_in[0].astype(BF16), params, B, S)

    n_tiles = T // MIX_TS
    max_rows = n_tiles * (TOP_K * MIX_TS + N_EXPERTS * (ROW_PIECE - 1)) + N_EXPERTS * (MOE_TM - ROW_PIECE)
    n_rows = -(-max_rows // MOE_TM) * MOE_TM
    n_blocks = n_rows // MOE_TM
    i32 = lambda a: a.astype(jnp.int32)
    mm = lambda a, b: jnp.round(jnp.dot(a, b, precision=lax.Precision.HIGHEST, preferred_element_type=F32))
    e_ids = np.arange(N_EXPERTS)
    t_ids = np.arange(n_tiles)
    b_ids = np.arange(n_blocks)
    cum_e = jnp.asarray(e_ids[:, None] <= e_ids[None, :], F32)
    cum_t = jnp.asarray(t_ids[:, None] >= t_ids[None, :], F32)
    cum_b = jnp.asarray(b_ids[:, None] <= b_ids[None, :], F32)
    e_row = jnp.asarray(e_ids[None, :], F32)

    pcs = tstat.reshape(n_tiles, 8, LANES)[:, 0, ROUTER_LANE0:ROUTER_LANE0 + N_EXPERTS].astype(F32)
    piece_end = mm(pcs, cum_e)
    piece_loc = piece_end - pcs
    tile_cum = mm(cum_t, pcs)
    rows_e = tile_cum[-1:, :] * ROW_PIECE
    padded = jnp.floor((rows_e + (MOE_TM - 1)) * (1.0 / MOE_TM)) * MOE_TM
    pend = mm(padded, cum_e)
    poff = pend - padded
    piece_glob = poff * (1.0 / ROW_PIECE) + tile_cum - pcs
    nused_f = pend[0, -1] * (1.0 / MOE_TM)
    blk_start = jnp.asarray(b_ids[:, None] * MOE_TM, F32)
    blk_e_f = jnp.minimum(jnp.sum((pend <= blk_start).astype(F32), axis=1, keepdims=True), N_EXPERTS - 1.0)
    blk_oh = (blk_e_f == e_row).astype(F32)
    later_nonempty = (e_ids[None, :] > e_ids[:, None]) & (rows_e > 0)
    next_of_e = jnp.min(jnp.where(later_nonempty, e_row, float(N_EXPERTS)), axis=1)
    next_of_e = jnp.where(next_of_e == N_EXPERTS, -1.0, next_of_e)
    per_blk = mm(blk_oh, jnp.stack([pend[0], (poff + rows_e)[0], next_of_e], axis=1))
    blk_used = blk_start < pend[0, -1]
    rows_in_blk = jnp.clip(per_blk[:, 1:2] - blk_start, 0.0, float(MOE_TM))
    nsub = jnp.floor((rows_in_blk + (EXPERT_SUB - 1)) * (1.0 / EXPERT_SUB))
    prev_e = jnp.concatenate([jnp.full((1, 1), -1.0, F32), blk_e_f[:-1]], axis=0)
    run_first = (blk_used & (blk_e_f != prev_e)).astype(F32)
    run_idx = mm(run_first.reshape(1, n_blocks), cum_b) - 1.0
    run_slot = run_idx - 2.0 * jnp.floor(run_idx * 0.5)
    p_ids = jnp.asarray(np.arange(PIECES_PER_TILE), F32)
    e_of_p = jnp.minimum(jnp.sum((piece_end[:, None, :] <= p_ids[None, :, None]).astype(F32), axis=2),
                         N_EXPERTS - 1.0)
    shift = jnp.sum(jnp.where(e_of_p[:, :, None] == e_row[None], (piece_glob - piece_loc)[:, None, :], 0.0), axis=2)
    glob_of_p = i32(shift + p_ids[None, :]).reshape(n_tiles * PIECES_PER_TILE)
    npieces = i32(piece_end[:, -1])
    flat_b = lambda a: i32(a).reshape(n_blocks)
    blk_e, nused = flat_b(blk_e_f), i32(nused_f).reshape(1)

    flat_e = lambda a: i32(a).reshape(N_EXPERTS)
    buf = _dispatch(xn2, srow, npieces, glob_of_p, flat_e(poff + rows_e), flat_e(padded - rows_e), nused, n_rows)
    yb = _experts(buf, blk_e, nused, flat_b(nsub), flat_b(run_first), flat_b(per_blk[:, 2:3]), flat_b(run_slot),
                  w_e_gate[0], w_e_up[0], w_e_down[0])
    out = _combine(x2, rti, rtf, g_final[None, :], yb, npieces, glob_of_p)
    return out.reshape(B, S, D)
```

```python
import functools

import numpy as np
import jax
import jax.numpy as jnp
from jax import lax
from jax.experimental import pallas as pl
from jax.experimental.pallas import tpu as pltpu

F32 = jnp.float32
BF16 = jnp.bfloat16

CHUNK = 64
POOL_WINDOWS = (2, 4, 8, 16)
N_HEADS = 4
CONV_K = 4
N_GROUPS = 4
EXPERTS_PER_GROUP = 8
N_EXPERTS = N_GROUPS * EXPERTS_PER_GROUP
TOP_K = 2
EPS = 1e-6

LANES = 128
HALO = 16
ROUTER_LANE0 = N_GROUPS

INPROJ_TN = 256
MIX_TS = 256
MOE_TM = 512
ROW_PIECE = 16
MOE_SL = TOP_K * MIX_TS + N_EXPERTS * ROW_PIECE
PIECES_PER_TILE = MOE_SL // ROW_PIECE
MOE_STEP_TILES = 4
SLOT_SUB = 128
PIECE_UNROLL = 4
EXPERT_SUB = 128
SLOT_RADIX = 16
VMEM_LIMIT = 56 * 1024 * 1024


def _cparams(n_axes):
    return pltpu.CompilerParams(dimension_semantics=("arbitrary",) * n_axes,
                                vmem_limit_bytes=VMEM_LIMIT)


def _sigmoid(v):
    return 0.5 * jnp.tanh(0.5 * v) + 0.5


def _silu(v):
    return v * _sigmoid(v)


def _log_sigmoid(v):
    return jnp.minimum(v, 0.0) - jnp.log1p(jnp.exp(-jnp.abs(v)))


def _split3(v):
    hi = v.astype(BF16)
    r1 = v - hi.astype(F32)
    mid = r1.astype(BF16)
    lo = (r1 - mid.astype(F32)).astype(BF16)
    return hi, mid, lo


def _dot(a, b):
    return jnp.dot(a, b, preferred_element_type=F32)


def _dot_nt(a, b):
    return lax.dot_general(a, b, (((1,), (1,)), ((), ())), preferred_element_type=F32)


def _dot_tn(a, b):
    return lax.dot_general(a, b, (((0,), (0,)), ((), ())), preferred_element_type=F32)


def _inproj_steps(x_ref, g_ref, w_refs, z_refs, xn_ref):
    w_ref_all, wg_ref = w_refs
    zm_ref, zg_ref, zif_ref, zift_ref = z_refs
    n_main = zm_ref.shape[1]

    def norm():
        x = x_ref[...]
        ms = jnp.mean(x * x, axis=-1, keepdims=True)
        xn_ref[...] = (x * lax.rsqrt(ms + EPS) * g_ref[...]).astype(BF16)

    def block(w_ref, z_ref, c0):
        def run():
            cols = slice(c0, c0 + INPROJ_TN)
            z_ref[:, cols] = _dot(xn_ref[...], w_ref[:, cols]).astype(BF16)
        return run

    def gates():
        zif = _dot(xn_ref[...], w_ref_all[:, n_main:n_main + LANES])
        zif_ref[...] = zif
        zift_ref[...] = zif.T[:zift_ref.shape[0], :]

    steps = [norm, gates]
    steps += [block(w_ref_all, zm_ref, c0) for c0 in range(0, n_main, INPROJ_TN)]
    steps += [block(wg_ref, zg_ref, c0) for c0 in range(0, zg_ref.shape[1], INPROJ_TN)]
    return steps


def _mixer_kernel(x_ref, xnext_ref, gmix_ref, win_ref,
                  bif_ref, bift_ref, convq_ref, convk_ref, ghead_ref, wpool32_ref, pscale_ref,
                  wa32_ref, wb32_ref, wo32_ref, gffn_ref, wr_ref, br_ref,
                  tric_ref, trir_ref, stri_ref, ut_ref, sel_ref,
                  x2_ref, xn2_ref, rti_ref, rtf_ref, tstat_ref, srow_ref,
                  zm_ref, zg_ref, zif_ref, zift_ref, zm_nxt, zg_nxt, zif_nxt, zift_nxt, xn_ref,
                  ext_ref, q_ref, k_ref, h_ref, pool_ref, cst_ref, mst_ref, lg_ref, wg_ref,
                  wpool_ref, wa_ref, wb_ref, wo_ref, gate_ref, *, tiles_per_seq):
    ts = x_ref.shape[0]
    d_pool = wa_ref.shape[0]
    d_ml = wb_ref.shape[0]
    dh = d_ml // N_HEADS
    n_chunks = ts // CHUNK
    g_step = pl.program_id(0)
    j = lax.rem(g_step, tiles_per_seq)
    w_in_refs = (win_ref, wg_ref)
    z_cur = (zm_ref, zg_ref, zif_ref, zift_ref)
    z_nxt = (zm_nxt, zg_nxt, zif_nxt, zift_nxt)
    first = g_step == 0

    @pl.when(first)
    def _():
        g0 = zm_ref.shape[1] + 2 * N_HEADS
        wg_ref[...] = win_ref[:, g0:g0 + wg_ref.shape[1]]
        for dst, src in ((wpool_ref, wpool32_ref), (wa_ref, wa32_ref), (wb_ref, wb32_ref), (wo_ref, wo32_ref)):
            dst[...] = src[...].astype(BF16)
        for step in _inproj_steps(x_ref, gmix_ref, w_in_refs, z_cur, xn_ref):
            step()
        lg_ref[...] = jnp.zeros_like(lg_ref)

    @pl.when(jnp.logical_not(first))
    def _():
        for dst, src in zip(z_cur, z_nxt):
            dst[...] = src[...]

    @pl.when(j == 0)
    def _():
        ext_ref[:, :HALO, :] = jnp.zeros((ext_ref.shape[0], HALO, LANES), F32)
        cst_ref[...] = jnp.zeros_like(cst_ref)
        mst_ref[...] = jnp.zeros_like(mst_ref)

    pending = _inproj_steps(xnext_ref, gmix_ref, w_in_refs, z_nxt, xn_ref)

    def project_some(n=1):
        for _ in range(min(n, len(pending))):
            pending.pop(0)()

    project_some(2)

    routed = _route_select(lg_ref[...])

    row = lax.broadcasted_iota(jnp.int32, (ts, LANES), 0)
    pos1 = (row + j * ts + 1).astype(F32)

    def history(cg):
        cur = zm_ref[:, cg * LANES:(cg + 1) * LANES].astype(F32)
        ext_ref[cg, HALO:, :] = cur
        return cur, lambda s: ext_ref[cg, HALO - s:HALO - s + ts, :]

    def keep_history(cg, cur):
        ext_ref[cg, :HALO, :] = cur[ts - HALO:, :]

    n_pool_groups = d_pool // LANES
    for g in range(n_pool_groups):
        w = POOL_WINDOWS[g]
        cur, shifted = history(g)
        win = cur
        for s in range(1, w):
            win = win + shifted(s)
        keep_history(g, cur)
        cnt = jnp.minimum(pos1, float(w))
        d = win / cnt - cur
        y = _dot(d.astype(BF16), wpool_ref[g]) * pscale_ref[:, g * LANES:(g + 1) * LANES]
        pool_ref[:, g * LANES:(g + 1) * LANES] = y.astype(BF16)
        project_some()

    n_ml_groups = d_ml // LANES
    for which, (cw_ref, dst_ref, scale) in enumerate(((convq_ref, q_ref, 1.0), (convk_ref, k_ref, dh ** -0.5))):
        for g in range(n_ml_groups):
            cols = slice(g * LANES, (g + 1) * LANES)
            cg = n_pool_groups + which * n_ml_groups + g
            cur, shifted = history(cg)
            acc = cur * cw_ref[CONV_K - 1:CONV_K, cols]
            for sft in range(1, CONV_K):
                acc = acc + shifted(sft) * cw_ref[CONV_K - 1 - sft:CONV_K - sft, cols]
            keep_history(cg, cur)
            dst_ref[:, cols] = (_silu(acc) * scale).astype(BF16)
        project_some()

    _route_slots(routed, stri_ref, ut_ref, sel_ref, rti_ref, rtf_ref, tstat_ref, srow_ref)
    project_some()

    zc = zif_ref[...] + bif_ref[...]
    lf_c = _log_sigmoid(zc)
    bc = sum(_dot(tric_ref[...], p) for p in _split3(lf_c))
    zr = zift_ref[...] + bift_ref[...]
    lf_r = _log_sigmoid(zr)
    br = sum(_dot(p, trir_ref[...]) for p in _split3(lf_r))
    project_some(2)

    ti = lax.broadcasted_iota(jnp.int32, (CHUNK, CHUNK), 0)
    si = lax.broadcasted_iota(jnp.int32, (CHUNK, CHUNK), 1)
    causal = si <= ti
    ones_blk = jnp.ones((CHUNK, dh), BF16)
    v0 = d_pool + 2 * d_ml
    for h in range(N_HEADS):
        gate_ref[h] = jnp.broadcast_to(zc[:, h:h + 1], (ts, dh))
        gate_ref[N_HEADS + h] = jnp.broadcast_to(bc[:, N_HEADS + h:N_HEADS + h + 1], (ts, dh))

    m_state = [mst_ref[h:h + 1, :] for h in range(N_HEADS)]
    c_state = [cst_ref[h] for h in range(N_HEADS)]
    def stage_scores(c):
        rs = slice(c * CHUNK, (c + 1) * CHUNK)
        out = []
        for h in range(N_HEADS):
            hs = slice(h * dh, (h + 1) * dh)
            q = q_ref[rs, hs]
            k = k_ref[rs, hs]
            bt = gate_ref[N_HEADS + h, rs, :]
            r_row = zr[h:h + 1, rs] - br[N_HEADS + h:N_HEADS + h + 1, rs]
            dmat = jnp.where(causal, bt[:, :CHUNK] + r_row, -jnp.inf)
            out.append(dict(q=q, k=k, bt=bt, dmat=dmat, qk=_dot_nt(q, k),
                            m_intra=jnp.max(dmat, axis=-1, keepdims=True)))
        return out

    def stage_state(c, st):
        rs = slice(c * CHUNK, (c + 1) * CHUNK)
        for h in range(N_HEADS):
            s = st[h]
            bt, k = s["bt"], s["k"]
            m_prev, c_prev = m_state[h], c_state[h]
            v_aug = jnp.concatenate([zm_ref[rs, v0 + h * dh:v0 + (h + 1) * dh], ones_blk], axis=-1)
            igc = gate_ref[h, rs, :]
            b_last = bt[CHUNK - 1:CHUNK, :]
            a_log = b_last - bt + igc
            a_max = jnp.max(a_log, axis=0, keepdims=True)
            m_new = jnp.maximum(b_last + m_prev, a_max)
            kw = (k.astype(F32) * jnp.exp(a_log - m_new)).astype(BF16)
            decay = jnp.exp(b_last + m_prev - m_new)
            s.update(v_aug=v_aug, m_prev=m_prev, qc=_dot(s["q"], c_prev.astype(BF16)))
            c_state[h] = jnp.concatenate([decay, decay], axis=-1) * c_prev + _dot_tn(kw, v_aug)
            m_state[h] = m_new

    def stage_values(c, st):
        rs = slice(c * CHUNK, (c + 1) * CHUNK)
        for h in range(N_HEADS):
            s = st[h]
            hs = slice(h * dh, (h + 1) * dh)
            inter = s["bt"] + s["m_prev"]
            m_t = jnp.maximum(inter, s["m_intra"])
            w_inter = jnp.exp(inter - m_t)
            smat = s["qk"] * jnp.exp(s["dmat"] - m_t[:, :CHUNK])
            sv = _dot(smat.astype(BF16), s["v_aug"])
            qc = s["qc"]
            nq = w_inter * qc[:, dh:] + sv[:, dh:]
            den = jnp.maximum(jnp.abs(nq), jnp.exp(-m_t))
            h_ref[rs, hs] = (w_inter * qc[:, :dh] + sv[:, :dh]) / den

    staged = stage_scores(0)
    for c in range(n_chunks):
        stage_state(c, staged)
        project_some()
        nxt = stage_scores(c + 1) if c + 1 < n_chunks else None
        project_some()
        stage_values(c, staged)
        staged = nxt
    for h in range(N_HEADS):
        cst_ref[h] = c_state[h]
        mst_ref[h:h + 1, :] = m_state[h]

    o0 = v0 + d_ml
    for h in range(N_HEADS):
        hs = slice(h * dh, (h + 1) * dh)
        hv = h_ref[:, hs]
        mu = jnp.mean(hv, axis=-1, keepdims=True)
        hc = hv - mu
        var = jnp.mean(hc * hc, axis=-1, keepdims=True)
        hn = hc * lax.rsqrt(var + EPS) * ghead_ref[:, hs]
        og = _sigmoid(zm_ref[:, o0 + h * dh:o0 + (h + 1) * dh].astype(F32))
        q_ref[:, hs] = (og * hn).astype(BF16)
    y_a = _dot(pool_ref[...], wa_ref[...])
    y_b = _dot(q_ref[...], wb_ref[...])
    d_model = x_ref.shape[1]
    ga = _sigmoid(zg_ref[:, :d_model].astype(F32))
    gb = _sigmoid(zg_ref[:, d_model:].astype(F32))
    merged = (ga * y_a + gb * y_b).astype(BF16)
    x2 = x_ref[...] + _dot(merged, wo_ref[...])
    x2_ref[...] = x2.astype(BF16)
    project_some(len(pending))

    ms = jnp.mean(x2 * x2, axis=-1, keepdims=True)
    xn2 = x2 * lax.rsqrt(ms + EPS) * gffn_ref[...]
    xh = xn2.astype(BF16)
    xn2_ref[...] = xh
    lg_ref[...] = _dot(xh, wr_ref[...]) + br_ref[...]
    project_some(len(pending))


def _route_select(lg):
    ts = lg.shape[0]
    lane = lax.broadcasted_iota(jnp.int32, (ts, LANES), 1)
    lanef = lane.astype(F32)
    big = float(4 * LANES)
    gl = jnp.where(lane < N_GROUPS, lg, -jnp.inf)
    gmax = jnp.max(gl, axis=-1, keepdims=True)
    g_sel = jnp.min(jnp.where(gl == gmax, lanef, big), axis=-1, keepdims=True)
    p_g = 1.0 / jnp.sum(jnp.exp(gl - gmax), axis=-1, keepdims=True)
    lo = ROUTER_LANE0 + EXPERTS_PER_GROUP * g_sel
    el = jnp.where((lanef >= lo) & (lanef < lo + EXPERTS_PER_GROUP), lg, -jnp.inf)
    m1 = jnp.max(el, axis=-1, keepdims=True)
    i1 = jnp.min(jnp.where(el == m1, lanef, big), axis=-1, keepdims=True)
    el2 = jnp.where(lanef == i1, -jnp.inf, el)
    m2 = jnp.max(el2, axis=-1, keepdims=True)
    i2 = jnp.min(jnp.where(el2 == m2, lanef, big), axis=-1, keepdims=True)
    e2x = jnp.exp(m2 - m1)
    gate1 = p_g / (1.0 + e2x)
    gate2 = p_g * e2x / (1.0 + e2x)
    return dict(lane=lane, i1=i1, i2=i2, gate1=gate1, gate2=gate2, oh1=lanef == i1, oh2=lanef == i2)


def _route_slots(r, stri_ref, ut_ref, sel_ref, rti_ref, rtf_ref, tstat_ref, srow_ref):
    lane, oh1, oh2, i1, i2 = r["lane"], r["oh1"], r["oh2"], r["i1"], r["i2"]
    ohs = jnp.where(oh1 | oh2, 1.0, 0.0)
    n_loc = jnp.sum(ohs, axis=0, keepdims=True)
    pieces = jnp.floor((n_loc + (ROW_PIECE - 1.0)) * (1.0 / ROW_PIECE))
    piece_off = _dot(jnp.broadcast_to(pieces, (8, LANES)).astype(BF16), ut_ref[...])[0:1, :]
    base = _dot(stri_ref[...], ohs.astype(BF16)) + ROW_PIECE * piece_off
    slot1 = jnp.sum(jnp.where(oh1, base, 0.0), axis=-1, keepdims=True)
    slot2 = jnp.sum(jnp.where(oh2, base, 0.0), axis=-1, keepdims=True)
    tstat_ref[...] = jnp.broadcast_to(pieces, tstat_ref.shape).astype(jnp.int32)

    rti = jnp.where(lane == 0, i1 - ROUTER_LANE0,
                    jnp.where(lane == 1, i2 - ROUTER_LANE0,
                              jnp.where(lane == 2, slot1, jnp.where(lane == 3, slot2, 0.0))))
    rti_ref[...] = rti.astype(jnp.int32)
    rtf_ref[...] = jnp.where(lane == 0, r["gate1"], jnp.where(lane == 1, r["gate2"], 0.0))
    h1 = jnp.floor(slot1 * (1.0 / SLOT_RADIX))
    h2 = jnp.floor(slot2 * (1.0 / SLOT_RADIX))
    parts = jnp.where(lane == 0, h1, jnp.where(lane == 1, slot1 - SLOT_RADIX * h1,
                      jnp.where(lane == 2, h2, jnp.where(lane == 3, slot2 - SLOT_RADIX * h2, 0.0))))
    srow_ref[...] = _dot_nt(sel_ref[...], parts.astype(BF16))


def _mixer(x2d, g_mix, w_all, params, batch, seq):
    T, D = x2d.shape
    ts = min(MIX_TS, seq)
    nts = seq // ts
    d_pool = params["w_br_a"].shape[0]
    d_ml = params["w_br_b"].shape[0]
    dh = d_ml // N_HEADS

    idx = np.arange(ts)
    same_chunk = (idx[:, None] // CHUNK) == (idx[None, :] // CHUNK)
    tri_c = jnp.asarray((idx[None, :] <= idx[:, None]) & same_chunk, BF16)
    tri_r = jnp.asarray((idx[:, None] <= idx[None, :]) & same_chunk, BF16)
    stri = jnp.asarray(idx[None, :] < idx[:, None], BF16)
    lane_idx = np.arange(LANES)
    ut = jnp.asarray(lane_idx[:, None] < lane_idx[None, :], BF16)
    sel = jnp.asarray(np.arange(8)[:, None] == lane_idx[None, :], BF16)

    n_tiles = batch * nts
    tok = lambda g: (g, 0)
    tok_in = lambda g: (jnp.minimum(g, n_tiles - 1), 0)
    tok_next = lambda g: (jnp.minimum(g + 1, n_tiles - 1), 0)
    tok_prev = lambda g: (jnp.maximum(g - 1, 0), 0)
    tok_prev_t = lambda g: (0, jnp.maximum(g - 1, 0))
    c2 = lambda g: (0, 0)
    c3 = lambda g: (0, 0, 0)
    full = lambda a: pl.BlockSpec(a.shape, c2 if a.ndim == 2 else c3)
    consts = [params[n] for n in ("b_if", "b_if_t", "conv_q", "conv_k", "g_head", "w_pool", "pool_scale",
                                  "w_br_a", "w_br_b", "w_out", "g_ffn", "w_r", "b_r")]
    consts = [g_mix, w_all] + consts + [tri_c, tri_r, stri, ut, sel]
    nm, ng = d_pool + 4 * d_ml, 2 * D
    z_scratch = [pltpu.VMEM((ts, nm), BF16), pltpu.VMEM((ts, ng), BF16),
                 pltpu.VMEM((ts, LANES), F32), pltpu.VMEM((16, ts), F32)]
    return pl.pallas_call(
        functools.partial(_mixer_kernel, tiles_per_seq=nts),
        grid=(n_tiles + 1,),
        in_specs=[pl.BlockSpec((ts, D), tok_in),
                  pl.BlockSpec((ts, D), tok_next)] + [full(a) for a in consts],
        out_specs=[pl.BlockSpec((ts, D), tok),
                   pl.BlockSpec((ts, D), tok),
                   pl.BlockSpec((ts, LANES), tok_prev),
                   pl.BlockSpec((ts, LANES), tok_prev),
                   pl.BlockSpec((8, LANES), tok_prev),
                   pl.BlockSpec((8, ts), tok_prev_t)],
        out_shape=[jax.ShapeDtypeStruct((T + ts, D), BF16),
                   jax.ShapeDtypeStruct((T + ts, D), BF16),
                   jax.ShapeDtypeStruct((T, LANES), jnp.int32),
                   jax.ShapeDtypeStruct((T, LANES), F32),
                   jax.ShapeDtypeStruct((n_tiles * 8, LANES), jnp.int32),
                   jax.ShapeDtypeStruct((8, T), F32)],
        scratch_shapes=z_scratch + z_scratch + [
                        pltpu.VMEM((ts, D), BF16),
                        pltpu.VMEM(((d_pool + 2 * d_ml) // LANES, HALO + ts, LANES), F32),
                        pltpu.VMEM((ts, d_ml), BF16),
                        pltpu.VMEM((ts, d_ml), BF16),
                        pltpu.VMEM((ts, d_ml), F32),
                        pltpu.VMEM((ts, d_pool), BF16),
                        pltpu.VMEM((N_HEADS, dh, 2 * dh), F32),
                        pltpu.VMEM((8, LANES), F32),
                        pltpu.VMEM((ts, LANES), F32),
                        pltpu.VMEM((D, ng), BF16),
                        pltpu.VMEM(params["w_pool"].shape, BF16),
                        pltpu.VMEM((d_pool, D), BF16),
                        pltpu.VMEM((d_ml, D), BF16),
                        pltpu.VMEM((D, D), BF16),
                        pltpu.VMEM((2 * N_HEADS, ts, dh), F32)],
        compiler_params=_cparams(1),
        name="mixer",
    )(x2d, x2d, *consts)


def _for_each_piece(npieces_ref, glob_ref, tile, fn):
    base = tile * PIECES_PER_TILE
    n = npieces_ref[tile]

    def one(p):
        fn(pl.multiple_of(p * ROW_PIECE, ROW_PIECE), pl.multiple_of(glob_ref[base + p] * ROW_PIECE, ROW_PIECE))

    def group(g, carry):
        for u in range(PIECE_UNROLL):
            one(g * PIECE_UNROLL + u)
        return carry

    n_groups = lax.div(n, jnp.int32(PIECE_UNROLL))
    lax.fori_loop(0, n_groups, group, 0)
    for u in range(PIECE_UNROLL - 1):
        @pl.when(n_groups * PIECE_UNROLL + u < n)
        def _():
            one(n_groups * PIECE_UNROLL + u)


def _used_slot_groups(n_pieces):
    return lax.div(n_pieces * ROW_PIECE + (SLOT_SUB - 1), jnp.int32(SLOT_SUB))


def _dispatch_kernel(npieces_ref, glob_ref, pad_start_ref, pad_rows_ref, nused_ref,
                     xn_ref, srow_ref, buf_ref, rows_ref, zeros_ref, sem, zsem, tsem, *, n_steps):
    tt = xn_ref.shape[0] // MOE_STEP_TILES
    sl = rows_ref.shape[1] // MOE_STEP_TILES
    n_blocks = buf_ref.shape[0] // MOE_TM
    i = pl.program_id(0)
    cur = lax.rem(i, 2)

    def zero_fill_share(step, act):
        for k in range(-(-N_EXPERTS // n_steps)):
            e = step + k * n_steps

            @pl.when(e < N_EXPERTS)
            def _(e=e):
                e_c = jnp.minimum(e, N_EXPERTS - 1)
                rows = pad_rows_ref[e_c]
                off = pad_start_ref[e_c]
                size = MOE_TM // 2
                while size >= ROW_PIECE:
                    has = (rows & size) != 0

                    @pl.when(has)
                    def _(off=off, size=size):
                        act(pltpu.make_async_copy(zeros_ref.at[pl.ds(0, size)],
                                                  buf_ref.at[pl.ds(pl.multiple_of(off, ROW_PIECE), size)], zsem))
                    off = off + jnp.where(has, size, 0)
                    size //= 2
        for k in range(-(-n_blocks // n_steps)):
            b = step + k * n_steps

            @pl.when((b >= nused_ref[0]) & (b < n_blocks))
            def _(b=b):
                b_in = jnp.minimum(b, n_blocks - 1)
                act(pltpu.make_async_copy(zeros_ref, buf_ref.at[pl.ds(b_in * MOE_TM, MOE_TM)], tsem))

    @pl.when(i == 0)
    def _():
        zeros_ref[...] = jnp.zeros_like(zeros_ref)

    zero_fill_share(i, lambda cp: cp.start())

    def piece_copy(buf_slot, sub, local_row, global_row):
        return pltpu.make_async_copy(rows_ref.at[buf_slot, pl.ds(sub * sl + local_row, ROW_PIECE)],
                                     buf_ref.at[pl.ds(global_row, ROW_PIECE)], sem.at[buf_slot])

    def for_step_pieces(step, buf_slot, act):
        for sub in range(MOE_STEP_TILES):
            _for_each_piece(npieces_ref, glob_ref, step * MOE_STEP_TILES + sub,
                            lambda l, g, sub=sub: act(piece_copy(buf_slot, sub, l, g)))

    @pl.when(i >= 2)
    def _():
        for_step_pieces(i - 2, cur, lambda cp: cp.wait())

    n_sub = _used_slot_groups(npieces_ref[i * MOE_STEP_TILES])
    for sub in range(1, MOE_STEP_TILES):
        n_sub = jnp.maximum(n_sub, _used_slot_groups(npieces_ref[i * MOE_STEP_TILES + sub]))
    for k in range(TOP_K * tt // SLOT_SUB, sl // SLOT_SUB + 1):
        @pl.when(n_sub == k)
        def _(k=k):
            m = k * SLOT_SUB
            r = lax.broadcasted_iota(jnp.int32, (m, tt), 0).astype(F32)
            for sub in range(MOE_STEP_TILES):
                sr = srow_ref[:, sub * tt:(sub + 1) * tt]
                slot1 = SLOT_RADIX * sr[0:1, :] + sr[1:2, :]
                slot2 = SLOT_RADIX * sr[2:3, :] + sr[3:4, :]
                sel = jnp.where((r == slot1) | (r == slot2), 1.0, 0.0).astype(BF16)
                rows_ref[cur, sub * sl:sub * sl + m, :] = _dot(sel, xn_ref[sub * tt:(sub + 1) * tt, :]).astype(BF16)
    for_step_pieces(i, cur, lambda cp: cp.start())

    @pl.when(i == n_steps - 1)
    def _():
        @pl.when(i >= 1)
        def _():
            for_step_pieces(i - 1, 1 - cur, lambda cp: cp.wait())
        for_step_pieces(i, cur, lambda cp: cp.wait())

        def wait_share(step, carry):
            zero_fill_share(step, lambda cp: cp.wait())
            return carry

        lax.fori_loop(0, n_steps, wait_share, 0)


def _dispatch(xn2, srow, npieces, piece_glob, pad_start, pad_rows, nused, n_rows):
    T, D = srow.shape[1], xn2.shape[1]
    tt = MOE_STEP_TILES * MIX_TS
    return pl.pallas_call(
        functools.partial(_dispatch_kernel, n_steps=T // tt),
        grid_spec=pltpu.PrefetchScalarGridSpec(
            num_scalar_prefetch=5,
            grid=(T // tt,),
            in_specs=[pl.BlockSpec((tt, D), lambda i, *_: (i, 0)),
                      pl.BlockSpec((8, tt), lambda i, *_: (0, i))],
            out_specs=pl.BlockSpec(memory_space=pl.ANY),
            scratch_shapes=[pltpu.VMEM((2, MOE_STEP_TILES * MOE_SL, D), BF16),
                            pltpu.VMEM((MOE_TM, D), BF16),
                            pltpu.SemaphoreType.DMA((2,)),
                            pltpu.SemaphoreType.DMA(()),
                            pltpu.SemaphoreType.DMA(())]),
        out_shape=jax.ShapeDtypeStruct((n_rows, D), BF16),
        compiler_params=_cparams(1),
        name="dispatch",
    )(npieces, piece_glob, pad_start, pad_rows, nused, xn2, srow)


def _experts_kernel(blk_e_ref, nused_ref, nsub_ref, first_ref, next_e_ref, slot_ref,
                    x_ref, wg_hbm, wu_hbm, wd_hbm, y_ref,
                    wg32_ref, wu32_ref, wd32_ref, wgb_ref, wub_ref, wdb_ref, sem):
    i = pl.program_id(0)
    used = i < nused_ref[0]
    n_sub = nsub_ref[i]
    landing = ((wg_hbm, wg32_ref), (wu_hbm, wu32_ref), (wd_hbm, wd32_ref))

    def weight_copies(e, s):
        return [pltpu.make_async_copy(hbm.at[e], vmem.at[s], sem.at[s, n]) for n, (hbm, vmem) in enumerate(landing)]

    @pl.when(used & (i == 0))
    def _():
        for cp in weight_copies(blk_e_ref[0], 0):
            cp.start()

    run_start = used & (first_ref[i] > 0)
    s = slot_ref[i]

    @pl.when(run_start)
    def _():
        for cp in weight_copies(blk_e_ref[i], s):
            cp.wait()

        @pl.when(next_e_ref[i] >= 0)
        def _():
            for cp in weight_copies(next_e_ref[i], 1 - s):
                cp.start()

    def swiglu(m, wg, wu, wd):
        x = x_ref[:m, :]
        hg = _dot(x, wg)
        hu = _dot(x, wu)
        hid = (_silu(hg) * hu).astype(BF16)
        y_ref[:m, :] = _dot(hid, wd).astype(BF16)
        if m < MOE_TM:
            y_ref[m:, :] = jnp.zeros((MOE_TM - m, y_ref.shape[1]), BF16)

    for k in range(1, MOE_TM // EXPERT_SUB + 1):
        @pl.when(run_start & (n_sub == k))
        def _(k=k):
            wg = wg32_ref[s].astype(BF16)
            wu = wu32_ref[s].astype(BF16)
            wd = wd32_ref[s].astype(BF16)
            wgb_ref[...] = wg
            wub_ref[...] = wu
            wdb_ref[...] = wd
            swiglu(k * EXPERT_SUB, wg, wu, wd)

        @pl.when(used & jnp.logical_not(run_start) & (n_sub == k))
        def _(k=k):
            swiglu(k * EXPERT_SUB, wgb_ref[...], wub_ref[...], wdb_ref[...])


def _experts(buf, blk_e, nused, nsub, run_first, next_e, run_slot, w_gate, w_up, w_down):
    R, D = buf.shape
    de = w_gate.shape[2]
    n_blocks = R // MOE_TM
    row_map = lambda i, be, nu, *_: (jnp.minimum(i, nu[0] - 1), 0)
    return pl.pallas_call(
        _experts_kernel,
        grid_spec=pltpu.PrefetchScalarGridSpec(
            num_scalar_prefetch=6,
            grid=(n_blocks,),
            in_specs=[pl.BlockSpec((MOE_TM, D), row_map),
                      pl.BlockSpec(memory_space=pl.ANY),
                      pl.BlockSpec(memory_space=pl.ANY),
                      pl.BlockSpec(memory_space=pl.ANY)],
            out_specs=pl.BlockSpec((MOE_TM, D), row_map),
            scratch_shapes=[pltpu.VMEM((2, D, de), F32), pltpu.VMEM((2, D, de), F32), pltpu.VMEM((2, de, D), F32),
                            pltpu.VMEM((D, de), BF16), pltpu.VMEM((D, de), BF16), pltpu.VMEM((de, D), BF16),
                            pltpu.SemaphoreType.DMA((2, 3))]),
        out_shape=jax.ShapeDtypeStruct((R, D), BF16),
        input_output_aliases={6: 0},
        compiler_params=_cparams(1),
        name="experts",
    )(blk_e, nused, nsub, run_first, next_e, run_slot, buf, w_gate, w_up, w_down)


def _combine_kernel(npieces_ref, glob_ref, x2_ref, rti_ref, rtf_ref, gfin_ref, yb_ref, out_ref,
                    rows_ref, sem):
    tt = x2_ref.shape[0] // MOE_STEP_TILES
    sl = rows_ref.shape[1] // MOE_STEP_TILES
    i = pl.program_id(0)
    n_steps = pl.num_programs(0)
    cur = lax.rem(i, 2)

    def piece_copy(buf_slot, sub, local_row, global_row):
        return pltpu.make_async_copy(yb_ref.at[pl.ds(global_row, ROW_PIECE)],
                                     rows_ref.at[buf_slot, pl.ds(sub * sl + local_row, ROW_PIECE)], sem.at[buf_slot])

    def for_step_pieces(step, buf_slot, act):
        for sub in range(MOE_STEP_TILES):
            _for_each_piece(npieces_ref, glob_ref, step * MOE_STEP_TILES + sub,
                            lambda l, g, sub=sub: act(piece_copy(buf_slot, sub, l, g)))

    @pl.when(i == 0)
    def _():
        rows_ref[...] = jnp.zeros_like(rows_ref)
        for_step_pieces(0, 0, lambda cp: cp.start())

    @pl.when(i + 1 < n_steps)
    def _():
        for_step_pieces(i + 1, 1 - cur, lambda cp: cp.start())

    for_step_pieces(i, cur, lambda cp: cp.wait())

    n_sub = _used_slot_groups(npieces_ref[i * MOE_STEP_TILES])
    for sub in range(1, MOE_STEP_TILES):
        n_sub = jnp.maximum(n_sub, _used_slot_groups(npieces_ref[i * MOE_STEP_TILES + sub]))
    for k in range(TOP_K * tt // SLOT_SUB, sl // SLOT_SUB + 1):
        @pl.when(n_sub == k)
        def _(k=k):
            m = k * SLOT_SUB
            lane = lax.broadcasted_iota(jnp.int32, (tt, m), 1)
            for sub in range(MOE_STEP_TILES):
                ts_rows = slice(sub * tt, (sub + 1) * tt)
                rti = rti_ref[ts_rows, :]
                rtf = rtf_ref[ts_rows, :]
                g = jnp.where(lane == rti[:, 2:3], rtf[:, 0:1],
                              jnp.where(lane == rti[:, 3:4], rtf[:, 1:2], 0.0)).astype(BF16)
                y = x2_ref[ts_rows, :].astype(F32) + _dot(g, rows_ref[cur, sub * sl:sub * sl + m, :])
                ms = jnp.mean(y * y, axis=-1, keepdims=True)
                out_ref[ts_rows, :] = y * lax.rsqrt(ms + EPS) * gfin_ref[...]


def _combine(x2, rti, rtf, g_final, yb, npieces, piece_glob):
    T, D = rti.shape[0], x2.shape[1]
    tt = MOE_STEP_TILES * MIX_TS
    tok = lambda i, *_: (i, 0)
    return pl.pallas_call(
        _combine_kernel,
        grid_spec=pltpu.PrefetchScalarGridSpec(
            num_scalar_prefetch=2,
            grid=(T // tt,),
            in_specs=[pl.BlockSpec((tt, D), tok),
                      pl.BlockSpec((tt, LANES), tok),
                      pl.BlockSpec((tt, LANES), tok),
                      pl.BlockSpec((1, D), lambda i, *_: (0, 0)),
                      pl.BlockSpec(memory_space=pl.ANY)],
            out_specs=pl.BlockSpec((tt, D), tok),
            scratch_shapes=[pltpu.VMEM((2, MOE_STEP_TILES * MOE_SL, D), BF16),
                            pltpu.SemaphoreType.DMA((2,))]),
        out_shape=jax.ShapeDtypeStruct((T, D), F32),
        compiler_params=_cparams(1),
        name="combine",
    )(npieces, piece_glob, x2, rti, rtf, g_final, yb)


def _pad_lanes(a, width=LANES):
    return jnp.pad(a, ((0, 0), (0, width - a.shape[1])))


def kernel(x, g_mix, w_in, b_if, conv_q, conv_k, g_head, w_pool, pool_scale, w_br_a, w_br_b, w_out,
           g_ffn, w_rg, b_rg, w_re, b_re, w_e_gate, w_e_up, w_e_down, g_final):
    B, S, D = x.shape
    T = B * S
    assert g_mix.shape[0] == 1, "single-layer block"
    assert S % MIX_TS == 0 and (T // MIX_TS) % MOE_STEP_TILES == 0
    d_pool = w_br_a.shape[1]
    d_ml = w_br_b.shape[1]
    x2d = x.reshape(T, D)

    params = {
        "b_if": _pad_lanes(b_if[0][None, :]),
        "b_if_t": jnp.pad(b_if[0][:, None], ((0, 16 - 2 * N_HEADS), (0, 0))),
        "conv_q": conv_q[0], "conv_k": conv_k[0],
        "g_head": g_head[0][None, :],
        "w_pool": w_pool[0],
        "pool_scale": pool_scale[0][None, :],
        "w_br_a": w_br_a[0], "w_br_b": w_br_b[0],
        "w_out": w_out[0],
        "g_ffn": g_ffn[0][None, :],
        "w_r": _pad_lanes(jnp.concatenate([w_rg[0], w_re[0]], axis=1)).astype(BF16),
        "b_r": _pad_lanes(jnp.concatenate([b_rg[0], b_re[0]])[None, :]),
    }

    x2, xn2, rti, rtf, tstat, srow = _mixer(x2d, g_mix[0][None, :], w_in[0].astype(BF16), params, B, S)

    n_tiles = T // MIX_TS
    max_rows = n_tiles * (TOP_K * MIX_TS + N_EXPERTS * (ROW_PIECE - 1)) + N_EXPERTS * (MOE_TM - ROW_PIECE)
    n_rows = -(-max_rows // MOE_TM) * MOE_TM
    n_blocks = n_rows // MOE_TM
    i32 = lambda a: a.astype(jnp.int32)
    mm = lambda a, b: jnp.round(jnp.dot(a, b, precision=lax.Precision.HIGHEST, preferred_element_type=F32))
    e_ids = np.arange(N_EXPERTS)
    t_ids = np.arange(n_tiles)
    b_ids = np.arange(n_blocks)
    cum_e = jnp.asarray(e_ids[:, None] <= e_ids[None, :], F32)
    cum_t = jnp.asarray(t_ids[:, None] >= t_ids[None, :], F32)
    cum_b = jnp.asarray(b_ids[:, None] <= b_ids[None, :], F32)
    e_row = jnp.asarray(e_ids[None, :], F32)

    pcs = tstat.reshape(n_tiles, 8, LANES)[:, 0, ROUTER_LANE0:ROUTER_LANE0 + N_EXPERTS].astype(F32)
    piece_end = mm(pcs, cum_e)
    piece_loc = piece_end - pcs
    tile_cum = mm(cum_t, pcs)
    rows_e = tile_cum[-1:, :] * ROW_PIECE
    padded = jnp.floor((rows_e + (MOE_TM - 1)) * (1.0 / MOE_TM)) * MOE_TM
    pend = mm(padded, cum_e)
    poff = pend - padded
    piece_glob = poff * (1.0 / ROW_PIECE) + tile_cum - pcs
    nused_f = pend[0, -1] * (1.0 / MOE_TM)
    blk_start = jnp.asarray(b_ids[:, None] * MOE_TM, F32)
    blk_e_f = jnp.minimum(jnp.sum((pend <= blk_start).astype(F32), axis=1, keepdims=True), N_EXPERTS - 1.0)
    blk_oh = (blk_e_f == e_row).astype(F32)
    later_nonempty = (e_ids[None, :] > e_ids[:, None]) & (rows_e > 0)
    next_of_e = jnp.min(jnp.where(later_nonempty, e_row, float(N_EXPERTS)), axis=1)
    next_of_e = jnp.where(next_of_e == N_EXPERTS, -1.0, next_of_e)
    per_blk = mm(blk_oh, jnp.stack([pend[0], (poff + rows_e)[0], next_of_e], axis=1))
    blk_used = blk_start < pend[0, -1]
    rows_in_blk = jnp.clip(per_blk[:, 1:2] - blk_start, 0.0, float(MOE_TM))
    nsub = jnp.floor((rows_in_blk + (EXPERT_SUB - 1)) * (1.0 / EXPERT_SUB))
    prev_e = jnp.concatenate([jnp.full((1, 1), -1.0, F32), blk_e_f[:-1]], axis=0)
    run_first = (blk_used & (blk_e_f != prev_e)).astype(F32)
    run_idx = mm(run_first.reshape(1, n_blocks), cum_b) - 1.0
    run_slot = run_idx - 2.0 * jnp.floor(run_idx * 0.5)
    p_ids = jnp.asarray(np.arange(PIECES_PER_TILE), F32)
    e_of_p = jnp.minimum(jnp.sum((piece_end[:, None, :] <= p_ids[None, :, None]).astype(F32), axis=2),
                         N_EXPERTS - 1.0)
    shift = jnp.sum(jnp.where(e_of_p[:, :, None] == e_row[None], (piece_glob - piece_loc)[:, None, :], 0.0), axis=2)
    glob_of_p = i32(shift + p_ids[None, :]).reshape(n_tiles * PIECES_PER_TILE)
    npieces = i32(piece_end[:, -1])
    flat_b = lambda a: i32(a).reshape(n_blocks)
    blk_e, nused = flat_b(blk_e_f), i32(nused_f).reshape(1)

    flat_e = lambda a: i32(a).reshape(N_EXPERTS)
    buf = _dispatch(xn2, srow, npieces, glob_of_p, flat_e(poff + rows_e), flat_e(padded - rows_e), nused, n_rows)
    yb = _experts(buf, blk_e, nused, flat_b(nsub), flat_b(run_first), flat_b(per_blk[:, 2:3]), flat_b(run_slot),
                  w_e_gate[0], w_e_up[0], w_e_down[0])
    out = _combine(x2, rti, rtf, g_final[None, :], yb, npieces, glob_of_p)
    return out.reshape(B, S, D)
```
